```python
import jax, jax.numpy as jnp
from jax import lax
import numpy as np

D_MODEL = 1024
BATCH = 8
SEQ = 8192
DEPTH = 4

GRID_W = 64
CTX_LEN = 256
A_WIDTH = 512
B_WIDTH = 512
A_CONV = 3
B_CONV = 31
AB_IN = 3 * A_WIDTH + 2 * B_WIDTH
AB_OUT = A_WIDTH + B_WIDTH
N_HEADS = 16
N_KV_HEADS = 4
HEAD_DIM = 64
Q_GROUP = N_HEADS // N_KV_HEADS
Q_W = N_HEADS * HEAD_DIM
KV_W = N_KV_HEADS * HEAD_DIM
WINDOW = 128
BLOCK = 128
ROPE_THETA = 10000.0
D_FF = 2816
FFN_CONV = 3
EPS = 1e-6
NEG_INF = -1e30
N_EVEN = (DEPTH + 1) // 2
N_ODD = DEPTH // 2

kernel_name = 'hybrid_conv_swa_diffusion_block'


def rmsnorm(x, g):
    xf = x.astype(jnp.float32)
    y = xf * lax.rsqrt(jnp.mean(xf * xf, axis=-1, keepdims=True) + EPS)
    return (y * g.astype(jnp.float32)).astype(x.dtype)


def layernorm(x, g, b):
    xf = x.astype(jnp.float32)
    mu = jnp.mean(xf, axis=-1, keepdims=True)
    var = jnp.mean(jnp.square(xf - mu), axis=-1, keepdims=True)
    y = (xf - mu) * lax.rsqrt(var + EPS)
    return (y * g.astype(jnp.float32) + b.astype(jnp.float32)).astype(x.dtype)


def modulate(x, shift, scale):
    return x * (1.0 + scale) + shift


def adaln(cond, w_mod, b_mod):
    m = jax.nn.silu(cond) @ w_mod + b_mod
    return jnp.split(m, 6, axis=-1)


def dwconv(x, w):
    k = w.shape[0]
    return lax.conv_general_dilated(
        x, w[:, None, :], window_strides=(1,), padding=[(k // 2, k // 2)],
        dimension_numbers=('NWC', 'WIO', 'NWC'), feature_group_count=x.shape[-1])


def axial_rope_tables(length):
    rows = length // GRID_W
    row = jnp.repeat(jnp.arange(rows), GRID_W).astype(jnp.float32)
    col = jnp.tile(jnp.arange(GRID_W), rows).astype(jnp.float32)
    n_freq = HEAD_DIM // 4
    inv_freq = ROPE_THETA ** (-jnp.arange(n_freq, dtype=jnp.float32) / n_freq)
    ang = jnp.concatenate([row[:, None] * inv_freq, col[:, None] * inv_freq], axis=-1)
    return jnp.cos(ang)[:, None, :], jnp.sin(ang)[:, None, :]


def apply_rope(x, cos, sin):
    xf = x.astype(jnp.float32)
    x1, x2 = jnp.split(xf, 2, axis=-1)
    return jnp.concatenate([x1 * cos - x2 * sin, x2 * cos + x1 * sin], axis=-1).astype(x.dtype)


def sink_softmax(logits, sink):
    full = jnp.concatenate([logits, jnp.broadcast_to(sink, logits.shape[:-1] + (1,))], axis=-1)
    return jax.nn.softmax(full, axis=-1)[..., :-1]


def conv_mixers(h, w_in, conv_a, conv_b, conv_b_bias, ln_g, ln_b, w_out):
    p = h @ w_in
    g_b, g_c, u_a, v_b, gate_b = jnp.split(
        p, [A_WIDTH, 2 * A_WIDTH, 3 * A_WIDTH, 3 * A_WIDTH + B_WIDTH], axis=-1)
    y_a = g_b * dwconv(g_c * u_a, conv_a)
    u = v_b * jax.nn.sigmoid(gate_b)
    u = dwconv(u, conv_b) + conv_b_bias
    y_b = jax.nn.silu(layernorm(u, ln_g, ln_b))
    return jnp.concatenate([y_a, y_b], axis=-1) @ w_out


def windowed_gqa(h, hc, w_qkv, w_o, sinks, need_ctx_out):
    bsz, length, _ = h.shape
    n_ctx = hc.shape[1]
    scale = HEAD_DIM ** -0.5
    q, k, v = jnp.split(h @ w_qkv, [Q_W, Q_W + KV_W], axis=-1)
    q = q.reshape(bsz, length, N_HEADS, HEAD_DIM)
    k = k.reshape(bsz, length, N_KV_HEADS, HEAD_DIM)
    v = v.reshape(bsz, length, N_KV_HEADS, HEAD_DIM)
    cos, sin = axial_rope_tables(length)
    q = apply_rope(q, cos, sin) * scale
    k = apply_rope(k, cos, sin)
    kc, vc = jnp.split(hc @ w_qkv[:, Q_W:], [KV_W], axis=-1)
    kc = kc.reshape(bsz, n_ctx, N_KV_HEADS, HEAD_DIM)
    vc = vc.reshape(bsz, n_ctx, N_KV_HEADS, HEAD_DIM)
    sink = sinks.astype(jnp.float32).reshape(1, N_KV_HEADS, Q_GROUP, 1, 1)

    nblk = length // BLOCK
    qb = q.reshape(bsz, nblk, BLOCK, N_KV_HEADS, Q_GROUP, HEAD_DIM)

    def band(t):
        tb = t.reshape(bsz, nblk, BLOCK, N_KV_HEADS, HEAD_DIM)
        tp = jnp.pad(tb, ((0, 0), (1, 1), (0, 0), (0, 0), (0, 0)))
        return jnp.concatenate([tp[:, :-2], tp[:, 1:-1], tp[:, 2:]], axis=2)

    k_band, v_band = band(k), band(v)
    blk = jnp.arange(nblk)[:, None, None]
    q_pos = blk * BLOCK + jnp.arange(BLOCK)[None, :, None]
    k_pos = (blk - 1) * BLOCK + jnp.arange(3 * BLOCK)[None, None, :]
    mask = (jnp.abs(q_pos - k_pos) <= WINDOW) & (k_pos >= 0) & (k_pos < length)

    def attend_block(args):
        q_blk, k_blk, v_blk, m = args
        s_loc = jnp.einsum('bqhgd,bshd->bhgqs', q_blk, k_blk).astype(jnp.float32)
        s_loc = jnp.where(m, s_loc, NEG_INF)
        s_ctx = jnp.einsum('bqhgd,bchd->bhgqc', q_blk, kc).astype(jnp.float32)
        p = sink_softmax(jnp.concatenate([s_loc, s_ctx], axis=-1), sink)
        p_loc = p[..., :3 * BLOCK].astype(v_blk.dtype)
        p_ctx = p[..., 3 * BLOCK:].astype(vc.dtype)
        return (jnp.einsum('bhgqs,bshd->bqhgd', p_loc, v_blk)
                + jnp.einsum('bhgqc,bchd->bqhgd', p_ctx, vc))

    out = lax.map(attend_block, (jnp.moveaxis(qb, 1, 0), jnp.moveaxis(k_band, 1, 0),
                                 jnp.moveaxis(v_band, 1, 0), mask))
    y = jnp.moveaxis(out, 0, 1).reshape(bsz, length, Q_W) @ w_o

    yc = None
    if need_ctx_out:
        qc = (hc @ w_qkv[:, :Q_W]).reshape(bsz, n_ctx, N_KV_HEADS, Q_GROUP, HEAD_DIM) * scale
        s = jnp.einsum('bqhgd,bchd->bhgqc', qc, kc).astype(jnp.float32)
        p = sink_softmax(s, sink).astype(vc.dtype)
        yc = jnp.einsum('bhgqc,bchd->bqhgd', p, vc).reshape(bsz, n_ctx, Q_W) @ w_o
    return y, yc


def conv_ffn(h, w_up, conv, w_down):
    u = dwconv(h @ w_up, conv)
    a, g = jnp.split(u, 2, axis=-1)
    return (jax.nn.silu(g) * a) @ w_down


def _fwd_setup_inputs(seed: int = 0) -> dict:
    key = jax.random.key(seed)
    ks = jax.random.split(key, 22)

    def nrm(k, shape, s):
        return jax.random.normal(k, shape, jnp.float32) * s

    return {
        'x': nrm(ks[0], (BATCH, SEQ, D_MODEL), 1.0),
        'c': nrm(ks[1], (BATCH, D_MODEL), 1.0),
        'ctx': nrm(ks[2], (BATCH, CTX_LEN, D_MODEL), 1.0),
        'c_ctx': nrm(ks[3], (D_MODEL,), 1.0),
        'w_mod': nrm(ks[4], (DEPTH, D_MODEL, 6 * D_MODEL), 0.5 * D_MODEL ** -0.5),
        'b_mod': nrm(ks[5], (DEPTH, 6 * D_MODEL), 0.02),
        'norm_mix': 1.0 + nrm(ks[6], (DEPTH, D_MODEL), 0.02),
        'norm_ffn': 1.0 + nrm(ks[7], (DEPTH, D_MODEL), 0.02),
        'w_in_ab': nrm(ks[8], (N_EVEN, D_MODEL, AB_IN), D_MODEL ** -0.5),
        'conv_a': nrm(ks[9], (N_EVEN, A_CONV, A_WIDTH), A_CONV ** -0.5),
        'conv_b': nrm(ks[10], (N_EVEN, B_CONV, B_WIDTH), B_CONV ** -0.5),
        'conv_b_bias': nrm(ks[11], (N_EVEN, B_WIDTH), 0.02),
        'ln_b_gain': 1.0 + nrm(ks[12], (N_EVEN, B_WIDTH), 0.02),
        'ln_b_bias': nrm(ks[13], (N_EVEN, B_WIDTH), 0.02),
        'w_out_ab': nrm(ks[14], (N_EVEN, AB_OUT, D_MODEL), AB_OUT ** -0.5),
        'w_qkv': nrm(ks[15], (N_ODD, D_MODEL, Q_W + 2 * KV_W), D_MODEL ** -0.5),
        'w_o': nrm(ks[16], (N_ODD, Q_W, D_MODEL), Q_W ** -0.5),
        'sinks': nrm(ks[17], (N_ODD, N_HEADS), 1.0),
        'w_up': nrm(ks[18], (DEPTH, D_MODEL, 2 * D_FF), D_MODEL ** -0.5),
        'w_conv_ffn': nrm(ks[19], (DEPTH, FFN_CONV, 2 * D_FF), FFN_CONV ** -0.5),
        'w_down': nrm(ks[20], (DEPTH, D_FF, D_MODEL), D_FF ** -0.5),
        'final_norm': 1.0 + nrm(ks[21], (D_MODEL,), 0.02),
    }


def _fwd_reference(x, c, ctx, c_ctx, w_mod, b_mod, norm_mix, norm_ffn, w_in_ab, conv_a, conv_b,
              conv_b_bias, ln_b_gain, ln_b_bias, w_out_ab, w_qkv, w_o, sinks, w_up, w_conv_ffn,
              w_down, final_norm):
    xc = ctx
    for l in range(DEPTH):
        last = l == DEPTH - 1
        sh1, sc1, g1, sh2, sc2, g2 = adaln(c[:, None, :], w_mod[l], b_mod[l])
        csh1, csc1, cg1, csh2, csc2, cg2 = adaln(c_ctx, w_mod[l], b_mod[l])
        h = modulate(rmsnorm(x, norm_mix[l]), sh1, sc1)
        hc = modulate(rmsnorm(xc, norm_mix[l]), csh1, csc1)
        if l % 2 == 0:
            e = l // 2
            y = conv_mixers(h, w_in_ab[e], conv_a[e], conv_b[e], conv_b_bias[e],
                            ln_b_gain[e], ln_b_bias[e], w_out_ab[e])
            yc = None if last else conv_mixers(hc, w_in_ab[e], conv_a[e], conv_b[e], conv_b_bias[e],
                                               ln_b_gain[e], ln_b_bias[e], w_out_ab[e])
        else:
            o = l // 2
            y, yc = windowed_gqa(h, hc, w_qkv[o], w_o[o], sinks[o], not last)
        x = x + g1 * y
        x = x + g2 * conv_ffn(modulate(rmsnorm(x, norm_ffn[l]), sh2, sc2), w_up[l], w_conv_ffn[l], w_down[l])
        if not last:
            xc = xc + cg1 * yc
            xc = xc + cg2 * conv_ffn(modulate(rmsnorm(xc, norm_ffn[l]), csh2, csc2),
                                     w_up[l], w_conv_ffn[l], w_down[l])
    return rmsnorm(x, final_norm)


import jax as _jax
import jax.numpy as _jnp

TWIN_FORMAT = 'train_step'
FWD_PARAMS = ['x', 'c', 'ctx', 'c_ctx', 'w_mod', 'b_mod', 'norm_mix', 'norm_ffn', 'w_in_ab', 'conv_a', 'conv_b', 'conv_b_bias', 'ln_b_gain', 'ln_b_bias', 'w_out_ab', 'w_qkv', 'w_o', 'sinks', 'w_up', 'w_conv_ffn', 'w_down', 'final_norm']
TWIN_WEIGHTS = ['c_ctx', 'w_mod', 'b_mod', 'norm_mix', 'norm_ffn', 'w_in_ab', 'conv_a', 'conv_b', 'conv_b_bias', 'ln_b_gain', 'ln_b_bias', 'w_out_ab', 'w_qkv', 'w_o', 'sinks', 'w_up', 'w_conv_ffn', 'w_down', 'final_norm']
TWIN_DIFF_INPUT = 'x'
TWIN_INPUTS = ['x', 'c', 'ctx', 'c_ctx', 'w_mod', 'b_mod', 'norm_mix', 'norm_ffn', 'w_in_ab', 'conv_a', 'conv_b', 'conv_b_bias', 'ln_b_gain', 'ln_b_bias', 'w_out_ab', 'w_qkv', 'w_o', 'sinks', 'w_up', 'w_conv_ffn', 'w_down', 'final_norm', 'loss_target', 'm_c_ctx', 'm_w_mod', 'm_b_mod', 'm_norm_mix', 'm_norm_ffn', 'm_w_in_ab', 'm_conv_a', 'm_conv_b', 'm_conv_b_bias', 'm_ln_b_gain', 'm_ln_b_bias', 'm_w_out_ab', 'm_w_qkv', 'm_w_o', 'm_sinks', 'm_w_up', 'm_w_conv_ffn', 'm_w_down', 'm_final_norm', 'v_c_ctx', 'v_w_mod', 'v_b_mod', 'v_norm_mix', 'v_norm_ffn', 'v_w_in_ab', 'v_conv_a', 'v_conv_b', 'v_conv_b_bias', 'v_ln_b_gain', 'v_ln_b_bias', 'v_w_out_ab', 'v_w_qkv', 'v_w_o', 'v_sinks', 'v_w_up', 'v_w_conv_ffn', 'v_w_down', 'v_final_norm']
TWIN_OUTPUTS = ['loss', 'grad_x', 'grad_c_ctx', 'grad_w_mod', 'grad_b_mod', 'grad_norm_mix', 'grad_norm_ffn', 'grad_w_in_ab', 'grad_conv_a', 'grad_conv_b', 'grad_conv_b_bias', 'grad_ln_b_gain', 'grad_ln_b_bias', 'grad_w_out_ab', 'grad_w_qkv', 'grad_w_o', 'grad_sinks', 'grad_w_up', 'grad_w_conv_ffn', 'grad_w_down', 'grad_final_norm', 'delta_c_ctx', 'delta_w_mod', 'delta_b_mod', 'delta_norm_mix', 'delta_norm_ffn', 'delta_w_in_ab', 'delta_conv_a', 'delta_conv_b', 'delta_conv_b_bias', 'delta_ln_b_gain', 'delta_ln_b_bias', 'delta_w_out_ab', 'delta_w_qkv', 'delta_w_o', 'delta_sinks', 'delta_w_up', 'delta_w_conv_ffn', 'delta_w_down', 'delta_final_norm', 'new_m_c_ctx', 'new_m_w_mod', 'new_m_b_mod', 'new_m_norm_mix', 'new_m_norm_ffn', 'new_m_w_in_ab', 'new_m_conv_a', 'new_m_conv_b', 'new_m_conv_b_bias', 'new_m_ln_b_gain', 'new_m_ln_b_bias', 'new_m_w_out_ab', 'new_m_w_qkv', 'new_m_w_o', 'new_m_sinks', 'new_m_w_up', 'new_m_w_conv_ffn', 'new_m_w_down', 'new_m_final_norm', 'new_v_c_ctx', 'new_v_w_mod', 'new_v_b_mod', 'new_v_norm_mix', 'new_v_norm_ffn', 'new_v_w_in_ab', 'new_v_conv_a', 'new_v_conv_b', 'new_v_conv_b_bias', 'new_v_ln_b_gain', 'new_v_ln_b_bias', 'new_v_w_out_ab', 'new_v_w_qkv', 'new_v_w_o', 'new_v_sinks', 'new_v_w_up', 'new_v_w_conv_ffn', 'new_v_w_down', 'new_v_final_norm']
TWIN_LEAF_KINDS = {'loss': 'loss', 'grad_x': 'grad_x', 'grad_c_ctx': 'grad_w', 'grad_w_mod': 'grad_w', 'grad_b_mod': 'grad_w', 'grad_norm_mix': 'grad_w', 'grad_norm_ffn': 'grad_w', 'grad_w_in_ab': 'grad_w', 'grad_conv_a': 'grad_w', 'grad_conv_b': 'grad_w', 'grad_conv_b_bias': 'grad_w', 'grad_ln_b_gain': 'grad_w', 'grad_ln_b_bias': 'grad_w', 'grad_w_out_ab': 'grad_w', 'grad_w_qkv': 'grad_w', 'grad_w_o': 'grad_w', 'grad_sinks': 'grad_w', 'grad_w_up': 'grad_w', 'grad_w_conv_ffn': 'grad_w', 'grad_w_down': 'grad_w', 'grad_final_norm': 'grad_w', 'delta_c_ctx': 'delta_w', 'delta_w_mod': 'delta_w', 'delta_b_mod': 'delta_w', 'delta_norm_mix': 'delta_w', 'delta_norm_ffn': 'delta_w', 'delta_w_in_ab': 'delta_w', 'delta_conv_a': 'delta_w', 'delta_conv_b': 'delta_w', 'delta_conv_b_bias': 'delta_w', 'delta_ln_b_gain': 'delta_w', 'delta_ln_b_bias': 'delta_w', 'delta_w_out_ab': 'delta_w', 'delta_w_qkv': 'delta_w', 'delta_w_o': 'delta_w', 'delta_sinks': 'delta_w', 'delta_w_up': 'delta_w', 'delta_w_conv_ffn': 'delta_w', 'delta_w_down': 'delta_w', 'delta_final_norm': 'delta_w', 'new_m_c_ctx': 'new_m', 'new_m_w_mod': 'new_m', 'new_m_b_mod': 'new_m', 'new_m_norm_mix': 'new_m', 'new_m_norm_ffn': 'new_m', 'new_m_w_in_ab': 'new_m', 'new_m_conv_a': 'new_m', 'new_m_conv_b': 'new_m', 'new_m_conv_b_bias': 'new_m', 'new_m_ln_b_gain': 'new_m', 'new_m_ln_b_bias': 'new_m', 'new_m_w_out_ab': 'new_m', 'new_m_w_qkv': 'new_m', 'new_m_w_o': 'new_m', 'new_m_sinks': 'new_m', 'new_m_w_up': 'new_m', 'new_m_w_conv_ffn': 'new_m', 'new_m_w_down': 'new_m', 'new_m_final_norm': 'new_m', 'new_v_c_ctx': 'new_v', 'new_v_w_mod': 'new_v', 'new_v_b_mod': 'new_v', 'new_v_norm_mix': 'new_v', 'new_v_norm_ffn': 'new_v', 'new_v_w_in_ab': 'new_v', 'new_v_conv_a': 'new_v', 'new_v_conv_b': 'new_v', 'new_v_conv_b_bias': 'new_v', 'new_v_ln_b_gain': 'new_v', 'new_v_ln_b_bias': 'new_v', 'new_v_w_out_ab': 'new_v', 'new_v_w_qkv': 'new_v', 'new_v_w_o': 'new_v', 'new_v_sinks': 'new_v', 'new_v_w_up': 'new_v', 'new_v_w_conv_ffn': 'new_v', 'new_v_w_down': 'new_v', 'new_v_final_norm': 'new_v'}


def _forward(args):
    return _fwd_reference(*[args[k] for k in FWD_PARAMS])


def _output_shape():
    def fwd():
        inp = _fwd_setup_inputs(0)
        return _fwd_reference(*[inp[k] for k in FWD_PARAMS])
    out = _jax.eval_shape(fwd)
    return out.shape, out.dtype

N_MICROBATCH = 1
ADAM_LR = 0.001
ADAM_B1 = 0.9
ADAM_B2 = 0.999
ADAM_EPS = 1e-08
ADAM_WD = 0.01
ADAM_STEP = 10
PER_EXAMPLE_BATCH_AXIS = {'x': 0, 'c': 0, 'ctx': 0, 'loss_target': 0}
SHARED_INPUTS = []
_WEIGHT_DTYPES = {'c_ctx': _jnp.float32, 'w_mod': _jnp.float32, 'b_mod': _jnp.float32, 'norm_mix': _jnp.float32, 'norm_ffn': _jnp.float32, 'w_in_ab': _jnp.float32, 'conv_a': _jnp.float32, 'conv_b': _jnp.float32, 'conv_b_bias': _jnp.float32, 'ln_b_gain': _jnp.float32, 'ln_b_bias': _jnp.float32, 'w_out_ab': _jnp.float32, 'w_qkv': _jnp.float32, 'w_o': _jnp.float32, 'sinks': _jnp.float32, 'w_up': _jnp.float32, 'w_conv_ffn': _jnp.float32, 'w_down': _jnp.float32, 'final_norm': _jnp.float32}
MOMENT_SCALE = {'c_ctx': 2.055792e-02, 'w_mod': 7.734315e-02, 'b_mod': 1.273928e-01, 'norm_mix': 8.807628e-02, 'norm_ffn': 7.412551e-02, 'w_in_ab': 8.015564e-02, 'conv_a': 9.819082e-02, 'conv_b': 4.740212e-02, 'conv_b_bias': 7.965766e-02, 'ln_b_gain': 5.358636e-02, 'ln_b_bias': 4.637024e-02, 'w_out_ab': 7.707050e-02, 'w_qkv': 1.837307e-02, 'w_o': 1.850847e-02, 'sinks': 5.719723e-04, 'w_up': 3.225547e-02, 'w_conv_ffn': 3.205607e-02, 'w_down': 5.264509e-02, 'final_norm': 6.407335e+01}


def _to_microbatches(a, axis):
    t = _jnp.moveaxis(a, axis, 0)
    t = t.reshape((N_MICROBATCH, t.shape[0] // N_MICROBATCH) + t.shape[1:])
    return _jnp.moveaxis(t, 1, axis + 1)


def setup_inputs(seed: int = 0) -> dict:
    inp = _fwd_setup_inputs(seed)
    key = _jax.random.fold_in(_jax.random.key(seed), 7919)
    shape, _ = _output_shape()
    out = dict(inp)
    out["loss_target"] = _jax.random.normal(_jax.random.fold_in(key, 0), shape, _jnp.float32)
    for i, name in enumerate(TWIN_WEIGHTS):
        w = inp[name].astype(_jnp.float32)
        if MOMENT_SCALE is None:
            s = _jnp.sqrt(_jnp.mean(_jnp.square(w)) + 1e-30)
        else:
            s = MOMENT_SCALE[name]
        km, kv = _jax.random.split(_jax.random.fold_in(key, i + 1))
        out[name] = w
        out["m_" + name] = s * _jax.random.normal(km, w.shape, _jnp.float32)
        out["v_" + name] = (s * s) * _jax.random.uniform(kv, w.shape, _jnp.float32, 0.5, 1.5)
    if N_MICROBATCH > 1:
        for name, axis in PER_EXAMPLE_BATCH_AXIS.items():
            out[name] = _to_microbatches(out[name], axis)
    return {'x': out['x'], 'c': out['c'], 'ctx': out['ctx'], 'c_ctx': out['c_ctx'], 'w_mod': out['w_mod'], 'b_mod': out['b_mod'], 'norm_mix': out['norm_mix'], 'norm_ffn': out['norm_ffn'], 'w_in_ab': out['w_in_ab'], 'conv_a': out['conv_a'], 'conv_b': out['conv_b'], 'conv_b_bias': out['conv_b_bias'], 'ln_b_gain': out['ln_b_gain'], 'ln_b_bias': out['ln_b_bias'], 'w_out_ab': out['w_out_ab'], 'w_qkv': out['w_qkv'], 'w_o': out['w_o'], 'sinks': out['sinks'], 'w_up': out['w_up'], 'w_conv_ffn': out['w_conv_ffn'], 'w_down': out['w_down'], 'final_norm': out['final_norm'], 'loss_target': out['loss_target'], 'm_c_ctx': out['m_c_ctx'], 'm_w_mod': out['m_w_mod'], 'm_b_mod': out['m_b_mod'], 'm_norm_mix': out['m_norm_mix'], 'm_norm_ffn': out['m_norm_ffn'], 'm_w_in_ab': out['m_w_in_ab'], 'm_conv_a': out['m_conv_a'], 'm_conv_b': out['m_conv_b'], 'm_conv_b_bias': out['m_conv_b_bias'], 'm_ln_b_gain': out['m_ln_b_gain'], 'm_ln_b_bias': out['m_ln_b_bias'], 'm_w_out_ab': out['m_w_out_ab'], 'm_w_qkv': out['m_w_qkv'], 'm_w_o': out['m_w_o'], 'm_sinks': out['m_sinks'], 'm_w_up': out['m_w_up'], 'm_w_conv_ffn': out['m_w_conv_ffn'], 'm_w_down': out['m_w_down'], 'm_final_norm': out['m_final_norm'], 'v_c_ctx': out['v_c_ctx'], 'v_w_mod': out['v_w_mod'], 'v_b_mod': out['v_b_mod'], 'v_norm_mix': out['v_norm_mix'], 'v_norm_ffn': out['v_norm_ffn'], 'v_w_in_ab': out['v_w_in_ab'], 'v_conv_a': out['v_conv_a'], 'v_conv_b': out['v_conv_b'], 'v_conv_b_bias': out['v_conv_b_bias'], 'v_ln_b_gain': out['v_ln_b_gain'], 'v_ln_b_bias': out['v_ln_b_bias'], 'v_w_out_ab': out['v_w_out_ab'], 'v_w_qkv': out['v_w_qkv'], 'v_w_o': out['v_w_o'], 'v_sinks': out['v_sinks'], 'v_w_up': out['v_w_up'], 'v_w_conv_ffn': out['v_w_conv_ffn'], 'v_w_down': out['v_w_down'], 'v_final_norm': out['v_final_norm']}


def _loss(weights, diff, rest, loss_target):
    with _jax.named_scope("forward"):
        args = {**rest, TWIN_DIFF_INPUT: diff, **{k: w.astype(_WEIGHT_DTYPES[k]) for k, w in weights.items()}}
        y = _forward(args)
    with _jax.named_scope("loss_head"):
        err = _jnp.square(y.astype(_jnp.float32) - loss_target)
        return 0.5 * _jnp.sum(_jnp.mean(err, axis=-1)) if err.ndim else 0.5 * err


def _adamw(w, g, m, v):
    m = ADAM_B1 * m + (1.0 - ADAM_B1) * g
    v = ADAM_B2 * v + (1.0 - ADAM_B2) * _jnp.square(g)
    m_hat = m / (1.0 - ADAM_B1 ** ADAM_STEP)
    v_hat = v / (1.0 - ADAM_B2 ** ADAM_STEP)
    delta = -ADAM_LR * (m_hat / (_jnp.sqrt(v_hat) + ADAM_EPS) + ADAM_WD * w)
    return delta, m, v


def reference(x, c, ctx, c_ctx, w_mod, b_mod, norm_mix, norm_ffn, w_in_ab, conv_a, conv_b, conv_b_bias, ln_b_gain, ln_b_bias, w_out_ab, w_qkv, w_o, sinks, w_up, w_conv_ffn, w_down, final_norm, loss_target, m_c_ctx, m_w_mod, m_b_mod, m_norm_mix, m_norm_ffn, m_w_in_ab, m_conv_a, m_conv_b, m_conv_b_bias, m_ln_b_gain, m_ln_b_bias, m_w_out_ab, m_w_qkv, m_w_o, m_sinks, m_w_up, m_w_conv_ffn, m_w_down, m_final_norm, v_c_ctx, v_w_mod, v_b_mod, v_norm_mix, v_norm_ffn, v_w_in_ab, v_conv_a, v_conv_b, v_conv_b_bias, v_ln_b_gain, v_ln_b_bias, v_w_out_ab, v_w_qkv, v_w_o, v_sinks, v_w_up, v_w_conv_ffn, v_w_down, v_final_norm):
    given = dict(x=x, c=c, ctx=ctx, c_ctx=c_ctx, w_mod=w_mod, b_mod=b_mod, norm_mix=norm_mix, norm_ffn=norm_ffn, w_in_ab=w_in_ab, conv_a=conv_a, conv_b=conv_b, conv_b_bias=conv_b_bias, ln_b_gain=ln_b_gain, ln_b_bias=ln_b_bias, w_out_ab=w_out_ab, w_qkv=w_qkv, w_o=w_o, sinks=sinks, w_up=w_up, w_conv_ffn=w_conv_ffn, w_down=w_down, final_norm=final_norm, loss_target=loss_target, m_c_ctx=m_c_ctx, m_w_mod=m_w_mod, m_b_mod=m_b_mod, m_norm_mix=m_norm_mix, m_norm_ffn=m_norm_ffn, m_w_in_ab=m_w_in_ab, m_conv_a=m_conv_a, m_conv_b=m_conv_b, m_conv_b_bias=m_conv_b_bias, m_ln_b_gain=m_ln_b_gain, m_ln_b_bias=m_ln_b_bias, m_w_out_ab=m_w_out_ab, m_w_qkv=m_w_qkv, m_w_o=m_w_o, m_sinks=m_sinks, m_w_up=m_w_up, m_w_conv_ffn=m_w_conv_ffn, m_w_down=m_w_down, m_final_norm=m_final_norm, v_c_ctx=v_c_ctx, v_w_mod=v_w_mod, v_b_mod=v_b_mod, v_norm_mix=v_norm_mix, v_norm_ffn=v_norm_ffn, v_w_in_ab=v_w_in_ab, v_conv_a=v_conv_a, v_conv_b=v_conv_b, v_conv_b_bias=v_conv_b_bias, v_ln_b_gain=v_ln_b_gain, v_ln_b_bias=v_ln_b_bias, v_w_out_ab=v_w_out_ab, v_w_qkv=v_w_qkv, v_w_o=v_w_o, v_sinks=v_sinks, v_w_up=v_w_up, v_w_conv_ffn=v_w_conv_ffn, v_w_down=v_w_down, v_final_norm=v_final_norm)
    weights = {n: given[n] for n in TWIN_WEIGHTS}
    shared = {n: given[n] for n in SHARED_INPUTS}
    per_example = {n: given[n] for n in ['x', 'c', 'ctx']}
    grad_fn = _jax.value_and_grad(_loss, argnums=(0, 1))

    def one_microbatch(ex, loss_target):
        ex = dict(ex)
        diff = ex.pop(TWIN_DIFF_INPUT)
        return grad_fn(weights, diff, {**shared, **ex}, loss_target)

    if N_MICROBATCH == 1:
        loss, (grad_w, grad_x) = one_microbatch(per_example, given["loss_target"])
    else:
        def body(carry, xs):
            loss_sum, grad_sum = carry
            l_k, (gw_k, gx_k) = one_microbatch(xs[0], xs[1])
            with _jax.named_scope("update"):
                return (loss_sum + l_k, _jax.tree.map(_jnp.add, grad_sum, gw_k)), gx_k

        init = (_jnp.zeros((), _jnp.float32), _jax.tree.map(_jnp.zeros_like, weights))
        (loss, grad_w), grad_x = _jax.lax.scan(body, init, (per_example, given["loss_target"]))
    with _jax.named_scope("update"):
        delta_w, new_m, new_v = {}, {}, {}
        for n in TWIN_WEIGHTS:
            delta_w[n], new_m[n], new_v[n] = _adamw(weights[n], grad_w[n], given["m_" + n], given["v_" + n])
    return (loss, grad_x, *[grad_w[n] for n in TWIN_WEIGHTS], *[delta_w[n] for n in TWIN_WEIGHTS],
            *[new_m[n] for n in TWIN_WEIGHTS], *[new_v[n] for n in TWIN_WEIGHTS])
```

```python
import functools

import jax
import jax.numpy as jnp
from jax import lax
from jax.experimental import pallas as pl
from jax.experimental.pallas import tpu as pltpu

F32 = jnp.float32
BF16 = jnp.bfloat16
MESH = pl.DeviceIdType.MESH

EPS = 1e-6
NEG_INF = -1e30
GRID_W = 64
HEAD_DIM = 64
N_HEADS = 16
WINDOW = 128
QB = 128
ROPE_THETA = 10000.0
A_W = 512
B_CONV = 31
D_FF = 2816
ADAM_LR, ADAM_B1, ADAM_B2, ADAM_EPS, ADAM_WD, ADAM_STEP = 0.001, 0.9, 0.999, 1e-8, 0.01, 10

TMR = 256
HALO = 16
N_DEV = 8
N_CHIP = 4


def _params(vmem_mb=None):
    if vmem_mb is None:
        return pltpu.CompilerParams()
    return pltpu.CompilerParams(vmem_limit_bytes=vmem_mb * 1024 * 1024)


def _row_tile(rows):
    for t in (768, 512, 256, 128, 64, 32, 16, 8):
        if rows % t == 0:
            return t
    raise ValueError(rows)


def _colsum8(v):
    r, c = v.shape
    return v.reshape(r // 8, 8, c).sum(axis=0)


def _sigmoid(v):
    return 1.0 / (1.0 + jnp.exp(-v))


def mm_nn(a, w, l, kind, out_dtype):
    R = a.shape[0]
    _, _, kb, nb = w.shape
    tm = _row_tile(R)
    if kind == "col":
        def body(a_ref, w_ref, o_ref):
            o_ref[...] = jnp.dot(a_ref[...].astype(BF16), w_ref[...],
                                 preferred_element_type=F32).astype(o_ref.dtype)
        return pl.pallas_call(
            body, name="mm_nn_col", grid=(N_CHIP, R // tm),
            in_specs=[pl.BlockSpec((tm, kb), lambda q, i: (i, 0)),
                      pl.BlockSpec((None, None, kb, nb), lambda q, i: (q, l, 0, 0))],
            out_specs=pl.BlockSpec((tm, nb), lambda q, i: (i, q)),
            out_shape=jax.ShapeDtypeStruct((R, N_CHIP * nb), out_dtype),
            compiler_params=_params(48))(a, w)

    def body(a_ref, w_ref, o_ref):
        wv = w_ref[...].reshape(N_CHIP * kb, nb)
        o_ref[...] = jnp.dot(a_ref[...].astype(BF16), wv, preferred_element_type=F32).astype(o_ref.dtype)
    return pl.pallas_call(
        body, name="mm_nn_row", grid=(R // tm,),
        in_specs=[pl.BlockSpec((tm, N_CHIP * kb), lambda i: (i, 0)),
                  pl.BlockSpec((N_CHIP, None, kb, nb), lambda i: (0, l, 0, 0))],
        out_specs=pl.BlockSpec((tm, nb), lambda i: (i, 0)),
        out_shape=jax.ShapeDtypeStruct((R, nb), out_dtype),
        compiler_params=_params(48))(a, w)


def mm_nt(d, w, l, kind, out_dtype):
    R = d.shape[0]
    _, _, kb, nb = w.shape
    tm = _row_tile(R)
    contract_last = (((1,), (1,)), ((), ()))
    if kind == "col":
        def body(d_ref, w_ref, o_ref, acc_ref):
            q = pl.program_id(1)

            @pl.when(q == 0)
            def _():
                acc_ref[...] = jnp.zeros_like(acc_ref)
            acc_ref[...] += lax.dot_general(d_ref[...].astype(BF16), w_ref[...], contract_last,
                                            preferred_element_type=F32)

            @pl.when(q == N_CHIP - 1)
            def _():
                o_ref[...] = acc_ref[...].astype(o_ref.dtype)
        return pl.pallas_call(
            body, name="mm_nt_col", grid=(R // tm, N_CHIP),
            in_specs=[pl.BlockSpec((tm, nb), lambda i, q: (i, q)),
                      pl.BlockSpec((None, None, kb, nb), lambda i, q: (q, l, 0, 0))],
            out_specs=pl.BlockSpec((tm, kb), lambda i, q: (i, 0)),
            out_shape=jax.ShapeDtypeStruct((R, kb), out_dtype),
            scratch_shapes=[pltpu.VMEM((tm, kb), F32)],
            compiler_params=_params(48))(d, w)

    def body(d_ref, w_ref, o_ref):
        wv = w_ref[...].reshape(N_CHIP * kb, nb)
        o_ref[...] = lax.dot_general(d_ref[...].astype(BF16), wv, contract_last,
                                     preferred_element_type=F32).astype(o_ref.dtype)
    return pl.pallas_call(
        body, name="mm_nt_row", grid=(R // tm,),
        in_specs=[pl.BlockSpec((tm, nb), lambda i: (i, 0)),
                  pl.BlockSpec((N_CHIP, None, kb, nb), lambda i: (0, l, 0, 0))],
        out_specs=pl.BlockSpec((tm, N_CHIP * kb), lambda i: (i, 0)),
        out_shape=jax.ShapeDtypeStruct((R, N_CHIP * kb), out_dtype),
        compiler_params=_params(48))(d, w)


def mm_tn(a, d, l, kind, like, buf):
    R = a.shape[0]
    _, nl, kb, nb = like.shape
    tm = _row_tile(R)
    contract_rows = (((0,), (0,)), ((), ()))
    out_shape = jax.ShapeDtypeStruct(like.shape, BF16)
    if kind == "col":
        grid = (N_CHIP, R // tm)
        nsteps = R // tm

        def compute(a_ref, d_ref, o_ref, acc_ref):
            i = pl.program_id(1)

            @pl.when(i == 0)
            def _():
                acc_ref[...] = jnp.zeros_like(acc_ref)
            acc_ref[...] += lax.dot_general(a_ref[...].astype(BF16), d_ref[...].astype(BF16), contract_rows,
                                            preferred_element_type=F32)

            @pl.when(i == nsteps - 1)
            def _():
                o_ref[...] = acc_ref[...].astype(BF16)
        in_specs = [pl.BlockSpec((tm, kb), lambda q, i: (i, 0)),
                    pl.BlockSpec((tm, nb), lambda q, i: (i, q))]
        out_spec = pl.BlockSpec((None, None, kb, nb), lambda q, i: (q, l, 0, 0))
        scratch = [pltpu.VMEM((kb, nb), F32)]
        name = "mm_tn_col"
    else:
        tn = 512
        grid = (nb // tn, R // tm)
        nsteps = R // tm

        def compute(a_ref, d_ref, o_ref, acc_ref):
            i = pl.program_id(1)

            @pl.when(i == 0)
            def _():
                acc_ref[...] = jnp.zeros_like(acc_ref)
            acc_ref[...] += lax.dot_general(a_ref[...].astype(BF16), d_ref[...].astype(BF16), contract_rows,
                                            preferred_element_type=F32)

            @pl.when(i == nsteps - 1)
            def _():
                o_ref[...] = acc_ref[...].astype(BF16).reshape(N_CHIP, kb, tn)
        in_specs = [pl.BlockSpec((tm, N_CHIP * kb), lambda n, i: (i, 0)),
                    pl.BlockSpec((tm, tn), lambda n, i: (i, n))]
        out_spec = pl.BlockSpec((N_CHIP, None, kb, tn), lambda n, i: (0, l, 0, n))
        scratch = [pltpu.VMEM((N_CHIP * kb, tn), F32)]
        name = "mm_tn_row"

    if buf is None:
        def body(a_ref, d_ref, o_ref, acc_ref):
            compute(a_ref, d_ref, o_ref, acc_ref)
        return pl.pallas_call(body, name=name, grid=grid, in_specs=in_specs, out_specs=out_spec,
                              out_shape=out_shape, scratch_shapes=scratch,
                              compiler_params=_params(48))(a, d)

    def body(a_ref, d_ref, buf_ref, o_ref, acc_ref):
        del buf_ref
        compute(a_ref, d_ref, o_ref, acc_ref)
    return pl.pallas_call(body, name=name + "_into", grid=grid,
                          in_specs=in_specs + [pl.BlockSpec(memory_space=pl.ANY)], out_specs=out_spec,
                          out_shape=out_shape, scratch_shapes=scratch, input_output_aliases={2: 0},
                          compiler_params=_params(48))(a, d, buf)


def _seg(i, T):
    return (i >= T // TMR).astype(jnp.int32)


def norm_mod_fwd(x, nw, mod, k, T):
    R, dm = x.shape

    def body(x_ref, nw_ref, mod_ref, h_ref):
        seg = _seg(pl.program_id(0), T)
        sh = mod_ref[seg, pl.ds(k, 1), :]
        sc = mod_ref[seg, pl.ds(k + 1, 1), :]
        xv = x_ref[...]
        r = lax.rsqrt(jnp.mean(xv * xv, axis=-1, keepdims=True) + EPS)
        h_ref[...] = ((xv * r * nw_ref[...]) * (1.0 + sc) + sh).astype(BF16)
    return pl.pallas_call(
        body, name="norm_mod_fwd", grid=(R // TMR,),
        in_specs=[pl.BlockSpec((TMR, dm), lambda i: (i, 0)),
                  pl.BlockSpec((1, dm), lambda i: (0, 0)),
                  pl.BlockSpec((2, 6, dm), lambda i: (0, 0, 0))],
        out_specs=pl.BlockSpec((TMR, dm), lambda i: (i, 0)),
        out_shape=jax.ShapeDtypeStruct((R, dm), BF16))(x, nw, mod)


def norm_mod_bwd(dh, x, nw, mod, dxr, k, T):
    R, dm = x.shape

    def body(dh_ref, x_ref, nw_ref, mod_ref, dxr_ref, dx_ref, dmod_ref, dnw_ref):
        i = pl.program_id(0)
        seg = _seg(i, T)

        @pl.when(i == 0)
        def _():
            dmod_ref[...] = jnp.zeros_like(dmod_ref)
            dnw_ref[...] = jnp.zeros_like(dnw_ref)
        sc = mod_ref[seg, pl.ds(k + 1, 1), :]
        nwv = nw_ref[...]
        xv = x_ref[...]
        r = lax.rsqrt(jnp.mean(xv * xv, axis=-1, keepdims=True) + EPS)
        xh = xv * r
        dhv = dh_ref[...]
        dmod_ref[seg, 0] += _colsum8(dhv)
        dmod_ref[seg, 1] += _colsum8(dhv * (xh * nwv))
        dn = dhv * (1.0 + sc)
        dnw_ref[...] += _colsum8(dn * xh)
        dxh = dn * nwv
        dx = r * (dxh - xh * jnp.mean(dxh * xh, axis=-1, keepdims=True))
        dx_ref[...] = dxr_ref[...] + dx
    tile = pl.BlockSpec((TMR, dm), lambda i: (i, 0))
    return pl.pallas_call(
        body, name="norm_mod_bwd", grid=(R // TMR,),
        in_specs=[tile, tile, pl.BlockSpec((1, dm), lambda i: (0, 0)),
                  pl.BlockSpec((2, 6, dm), lambda i: (0, 0, 0)), tile],
        out_specs=[tile, pl.BlockSpec((2, 2, 8, dm), lambda i: (0, 0, 0, 0)),
                   pl.BlockSpec((8, dm), lambda i: (0, 0))],
        out_shape=[jax.ShapeDtypeStruct((R, dm), F32), jax.ShapeDtypeStruct((2, 2, 8, dm), F32),
                   jax.ShapeDtypeStruct((8, dm), F32)])(dh, x, nw, mod, dxr)


def resid_fwd(x, y, mod, k, T):
    R, dm = x.shape

    def body(x_ref, y_ref, mod_ref, o_ref):
        seg = _seg(pl.program_id(0), T)
        o_ref[...] = x_ref[...] + mod_ref[seg, pl.ds(k, 1), :] * y_ref[...]
    tile = pl.BlockSpec((TMR, dm), lambda i: (i, 0))
    return pl.pallas_call(
        body, name="resid_fwd", grid=(R // TMR,),
        in_specs=[tile, tile, pl.BlockSpec((2, 6, dm), lambda i: (0, 0, 0))],
        out_specs=tile, out_shape=jax.ShapeDtypeStruct((R, dm), F32))(x, y, mod)


def resid_bwd(dxn, y, mod, k, T):
    R, dm = dxn.shape

    def body(dx_ref, y_ref, mod_ref, dy_ref, dg_ref):
        i = pl.program_id(0)
        seg = _seg(i, T)

        @pl.when(i == 0)
        def _():
            dg_ref[...] = jnp.zeros_like(dg_ref)
        dxv = dx_ref[...]
        dy_ref[...] = (mod_ref[seg, pl.ds(k, 1), :] * dxv).astype(BF16)
        dg_ref[seg] += _colsum8(dxv * y_ref[...])
    tile = pl.BlockSpec((TMR, dm), lambda i: (i, 0))
    return pl.pallas_call(
        body, name="resid_bwd", grid=(R // TMR,),
        in_specs=[tile, tile, pl.BlockSpec((2, 6, dm), lambda i: (0, 0, 0))],
        out_specs=[tile, pl.BlockSpec((2, 8, dm), lambda i: (0, 0, 0))],
        out_shape=[jax.ShapeDtypeStruct((R, dm), BF16), jax.ShapeDtypeStruct((2, 8, dm), F32)])(dxn, y, mod)


def _halo_specs(width, R):
    nblk = R // HALO
    per = TMR // HALO
    return (pl.BlockSpec((HALO, width), lambda i: (jnp.maximum(i * per - 1, 0), 0)),
            pl.BlockSpec((TMR, width), lambda i: (i, 0)),
            pl.BlockSpec((HALO, width), lambda i: (jnp.minimum((i + 1) * per, nblk - 1), 0)))


def _ext(refs, c0, cw):
    pref, ref, nref = refs
    return jnp.concatenate([pref[:, c0:c0 + cw].astype(F32), ref[:, c0:c0 + cw].astype(F32),
                            nref[:, c0:c0 + cw].astype(F32)], axis=0)


def _ext_mask(i, T, R):
    pos = i * TMR - HALO + lax.broadcasted_iota(jnp.int32, (TMR + 2 * HALO, 1), 0)
    lat = i < T // TMR
    lo = jnp.where(lat, 0, T)
    hi = jnp.where(lat, T, R)
    return (pos >= lo) & (pos < hi)


def _at(ext, off):
    n = ext.shape[0]
    s = (-off) % n
    y = pltpu.roll(ext, s, 0) if s else ext
    return y[HALO:HALO + TMR]


def ffnconv_fwd(u, wc, T):
    R, w2 = u.shape
    cw = 256

    def body(up_ref, u_ref, un_ref, wc_ref, z_ref):
        i = pl.program_id(0)
        mask = _ext_mask(i, T, R)

        def conv(c0):
            e = jnp.where(mask, _ext((up_ref, u_ref, un_ref), c0, cw), 0.0)
            return (wc_ref[pl.ds(0, 1), c0:c0 + cw] * _at(e, -1) + wc_ref[pl.ds(1, 1), c0:c0 + cw] * _at(e, 0)
                    + wc_ref[pl.ds(2, 1), c0:c0 + cw] * _at(e, 1))
        for j in range(D_FF // cw):
            a = conv(j * cw)
            g = conv(D_FF + j * cw)
            z_ref[:, j * cw:(j + 1) * cw] = (g * _sigmoid(g) * a).astype(BF16)
    return pl.pallas_call(
        body, name="ffnconv_fwd", grid=(R // TMR,),
        in_specs=[*_halo_specs(w2, R), pl.BlockSpec((3, w2), lambda i: (0, 0))],
        out_specs=pl.BlockSpec((TMR, D_FF), lambda i: (i, 0)),
        out_shape=jax.ShapeDtypeStruct((R, D_FF), BF16), compiler_params=_params(48))(u, u, u, wc)


def ffnconv_bwd1(dz, u, wc, T):
    R, w2 = u.shape
    cw = 256

    def body(dz_ref, up_ref, u_ref, un_ref, wc_ref, duc_ref, dwc_ref):
        i = pl.program_id(0)
        mask = _ext_mask(i, T, R)

        @pl.when(i == 0)
        def _():
            dwc_ref[...] = jnp.zeros_like(dwc_ref)

        def taps(c0):
            e = jnp.where(mask, _ext((up_ref, u_ref, un_ref), c0, cw), 0.0)
            return [_at(e, -1), _at(e, 0), _at(e, 1)]

        def conv(t, c0):
            return (wc_ref[pl.ds(0, 1), c0:c0 + cw] * t[0] + wc_ref[pl.ds(1, 1), c0:c0 + cw] * t[1]
                    + wc_ref[pl.ds(2, 1), c0:c0 + cw] * t[2])
        for j in range(D_FF // cw):
            ca, cg = j * cw, D_FF + j * cw
            ta, tg = taps(ca), taps(cg)
            a, g = conv(ta, ca), conv(tg, cg)
            dzv = dz_ref[:, ca:ca + cw].astype(F32)
            sg = _sigmoid(g)
            da = dzv * (g * sg)
            dg = dzv * a * (sg * (1.0 + g * (1.0 - sg)))
            duc_ref[:, ca:ca + cw] = da.astype(BF16)
            duc_ref[:, cg:cg + cw] = dg.astype(BF16)
            for k in range(3):
                dwc_ref[k, :, ca:ca + cw] += _colsum8(da * ta[k])
                dwc_ref[k, :, cg:cg + cw] += _colsum8(dg * tg[k])
    return pl.pallas_call(
        body, name="ffnconv_bwd1", grid=(R // TMR,),
        in_specs=[pl.BlockSpec((TMR, D_FF), lambda i: (i, 0)), *_halo_specs(w2, R),
                  pl.BlockSpec((3, w2), lambda i: (0, 0))],
        out_specs=[pl.BlockSpec((TMR, w2), lambda i: (i, 0)), pl.BlockSpec((3, 8, w2), lambda i: (0, 0, 0))],
        out_shape=[jax.ShapeDtypeStruct((R, w2), BF16), jax.ShapeDtypeStruct((3, 8, w2), F32)],
        compiler_params=_params(48))(dz, u, u, u, wc)


def ffnconv_bwd2(duc, wc, T):
    R, w2 = duc.shape
    cw = 256

    def body(dp_ref, d_ref, dn_ref, wc_ref, du_ref):
        mask = _ext_mask(pl.program_id(0), T, R)
        for j in range(w2 // cw):
            c0 = j * cw
            e = jnp.where(mask, _ext((dp_ref, d_ref, dn_ref), c0, cw), 0.0)
            du_ref[:, c0:c0 + cw] = (wc_ref[pl.ds(0, 1), c0:c0 + cw] * _at(e, 1)
                                     + wc_ref[pl.ds(1, 1), c0:c0 + cw] * _at(e, 0)
                                     + wc_ref[pl.ds(2, 1), c0:c0 + cw] * _at(e, -1)).astype(BF16)
    return pl.pallas_call(
        body, name="ffnconv_bwd2", grid=(R // TMR,),
        in_specs=[*_halo_specs(w2, R), pl.BlockSpec((3, w2), lambda i: (0, 0))],
        out_specs=pl.BlockSpec((TMR, w2), lambda i: (i, 0)),
        out_shape=jax.ShapeDtypeStruct((R, w2), BF16), compiler_params=_params(48))(duc, duc, duc, wc)


_CW = 128


def _mixer_a(prefs, wa_ref, mask):
    cin = jnp.where(mask, _ext(prefs, A_W, A_W) * _ext(prefs, 2 * A_W, A_W), 0.0)
    ca = (wa_ref[pl.ds(0, 1), :] * _at(cin, -1) + wa_ref[pl.ds(1, 1), :] * _at(cin, 0)
          + wa_ref[pl.ds(2, 1), :] * _at(cin, 1))
    return cin, ca


def _mixer_b(prefs, wb_ref, bias_ref, mask, ub_s, ub2_s):
    for cc in range(A_W // _CW):
        c0 = cc * _CW
        ub = jnp.where(mask, _ext(prefs, 3 * A_W + c0, _CW) * _sigmoid(_ext(prefs, 4 * A_W + c0, _CW)), 0.0)
        ub_s[:, c0:c0 + _CW] = ub
        acc = jnp.zeros((TMR, _CW), F32) + bias_ref[:, c0:c0 + _CW]
        for k in range(B_CONV):
            acc = acc + wb_ref[pl.ds(k, 1), c0:c0 + _CW] * _at(ub, k - B_CONV // 2)
        ub2_s[:, c0:c0 + _CW] = acc


def _layernorm_stats(v):
    mu = jnp.mean(v, axis=-1, keepdims=True)
    xc = v - mu
    rs = lax.rsqrt(jnp.mean(xc * xc, axis=-1, keepdims=True) + EPS)
    return xc * rs, rs


def convmix_fwd(p, wa, wb, bias, lng, lnb, T):
    R, wp = p.shape

    def body(pp_ref, p_ref, pn_ref, wa_ref, wb_ref, bias_ref, lng_ref, lnb_ref, o_ref, ub_s, ub2_s):
        mask = _ext_mask(pl.program_id(0), T, R)
        prefs = (pp_ref, p_ref, pn_ref)
        _, ca = _mixer_a(prefs, wa_ref, mask)
        o_ref[:, 0:A_W] = (p_ref[:, 0:A_W].astype(F32) * ca).astype(BF16)
        _mixer_b(prefs, wb_ref, bias_ref, mask, ub_s, ub2_s)
        xh, _ = _layernorm_stats(ub2_s[...])
        lv = xh * lng_ref[...] + lnb_ref[...]
        o_ref[:, A_W:2 * A_W] = (lv * _sigmoid(lv)).astype(BF16)
    vec = pl.BlockSpec((1, A_W), lambda i: (0, 0))
    return pl.pallas_call(
        body, name="convmix_fwd", grid=(R // TMR,),
        in_specs=[*_halo_specs(wp, R), pl.BlockSpec((3, A_W), lambda i: (0, 0)),
                  pl.BlockSpec((B_CONV, A_W), lambda i: (0, 0)), vec, vec, vec],
        out_specs=pl.BlockSpec((TMR, 2 * A_W), lambda i: (i, 0)),
        out_shape=jax.ShapeDtypeStruct((R, 2 * A_W), BF16),
        scratch_shapes=[pltpu.VMEM((TMR + 2 * HALO, A_W), F32), pltpu.VMEM((TMR, A_W), F32)],
        compiler_params=_params(48))(p, p, p, wa, wb, bias, lng, lnb)


def convmix_bwd1(dyab, p, wa, wb, bias, lng, lnb, T):
    R, wp = p.shape

    def body(dy_ref, pp_ref, p_ref, pn_ref, wa_ref, wb_ref, bias_ref, lng_ref, lnb_ref,
             dmid_ref, dwa_ref, dwb_ref, dvec_ref, ub_s, ub2_s):
        i = pl.program_id(0)
        mask = _ext_mask(i, T, R)

        @pl.when(i == 0)
        def _():
            dwa_ref[...] = jnp.zeros_like(dwa_ref)
            dwb_ref[...] = jnp.zeros_like(dwb_ref)
            dvec_ref[...] = jnp.zeros_like(dvec_ref)
        prefs = (pp_ref, p_ref, pn_ref)
        cin, ca = _mixer_a(prefs, wa_ref, mask)
        dya = dy_ref[:, 0:A_W]
        dmid_ref[:, 0:A_W] = dya * ca
        dca = dya * p_ref[:, 0:A_W].astype(F32)
        dmid_ref[:, A_W:2 * A_W] = dca
        for k in range(3):
            dwa_ref[k] += _colsum8(dca * _at(cin, k - 1))
        _mixer_b(prefs, wb_ref, bias_ref, mask, ub_s, ub2_s)
        xh, rs = _layernorm_stats(ub2_s[...])
        gain = lng_ref[...]
        lv = xh * gain + lnb_ref[...]
        sl = _sigmoid(lv)
        dl = dy_ref[:, A_W:2 * A_W] * (sl * (1.0 + lv * (1.0 - sl)))
        dvec_ref[1] += _colsum8(dl * xh)
        dvec_ref[2] += _colsum8(dl)
        dxh = dl * gain
        dub2 = rs * (dxh - jnp.mean(dxh, axis=-1, keepdims=True)
                     - xh * jnp.mean(dxh * xh, axis=-1, keepdims=True))
        dvec_ref[0] += _colsum8(dub2)
        dmid_ref[:, 2 * A_W:3 * A_W] = dub2
        for cc in range(A_W // _CW):
            c0 = cc * _CW
            ub = ub_s[:, c0:c0 + _CW]
            d = dmid_ref[:, 2 * A_W + c0:2 * A_W + c0 + _CW]
            for k in range(B_CONV):
                dwb_ref[k, :, c0:c0 + _CW] += _colsum8(d * _at(ub, k - B_CONV // 2))
    vec = pl.BlockSpec((1, A_W), lambda i: (0, 0))
    return pl.pallas_call(
        body, name="convmix_bwd1", grid=(R // TMR,),
        in_specs=[pl.BlockSpec((TMR, 2 * A_W), lambda i: (i, 0)), *_halo_specs(wp, R),
                  pl.BlockSpec((3, A_W), lambda i: (0, 0)), pl.BlockSpec((B_CONV, A_W), lambda i: (0, 0)),
                  vec, vec, vec],
        out_specs=[pl.BlockSpec((TMR, 3 * A_W), lambda i: (i, 0)),
                   pl.BlockSpec((3, 8, A_W), lambda i: (0, 0, 0)),
                   pl.BlockSpec((B_CONV, 8, A_W), lambda i: (0, 0, 0)),
                   pl.BlockSpec((3, 8, A_W), lambda i: (0, 0, 0))],
        out_shape=[jax.ShapeDtypeStruct((R, 3 * A_W), F32), jax.ShapeDtypeStruct((3, 8, A_W), F32),
                   jax.ShapeDtypeStruct((B_CONV, 8, A_W), F32), jax.ShapeDtypeStruct((3, 8, A_W), F32)],
        scratch_shapes=[pltpu.VMEM((TMR + 2 * HALO, A_W), F32), pltpu.VMEM((TMR, A_W), F32)],
        compiler_params=_params(48))(dyab, p, p, p, wa, wb, bias, lng, lnb)


def convmix_bwd2(dmid, p, wa, wb, T):
    R, wp = p.shape

    def body(mp_ref, m_ref, mn_ref, p_ref, wa_ref, wb_ref, dp_ref):
        mask = _ext_mask(pl.program_id(0), T, R)
        mrefs = (mp_ref, m_ref, mn_ref)
        dp_ref[:, 0:A_W] = m_ref[:, 0:A_W].astype(BF16)
        dca = jnp.where(mask, _ext(mrefs, A_W, A_W), 0.0)
        dcin = (wa_ref[pl.ds(0, 1), :] * _at(dca, 1) + wa_ref[pl.ds(1, 1), :] * _at(dca, 0)
                + wa_ref[pl.ds(2, 1), :] * _at(dca, -1))
        dp_ref[:, A_W:2 * A_W] = (dcin * p_ref[:, 2 * A_W:3 * A_W].astype(F32)).astype(BF16)
        dp_ref[:, 2 * A_W:3 * A_W] = (dcin * p_ref[:, A_W:2 * A_W].astype(F32)).astype(BF16)
        for cc in range(A_W // _CW):
            c0 = cc * _CW
            d = jnp.where(mask, _ext(mrefs, 2 * A_W + c0, _CW), 0.0)
            dub = jnp.zeros((TMR, _CW), F32)
            for k in range(B_CONV):
                dub = dub + wb_ref[pl.ds(k, 1), c0:c0 + _CW] * _at(d, B_CONV // 2 - k)
            vb = p_ref[:, 3 * A_W + c0:3 * A_W + c0 + _CW].astype(F32)
            s = _sigmoid(p_ref[:, 4 * A_W + c0:4 * A_W + c0 + _CW].astype(F32))
            dp_ref[:, 3 * A_W + c0:3 * A_W + c0 + _CW] = (dub * s).astype(BF16)
            dp_ref[:, 4 * A_W + c0:4 * A_W + c0 + _CW] = (dub * vb * s * (1.0 - s)).astype(BF16)
    return pl.pallas_call(
        body, name="convmix_bwd2", grid=(R // TMR,),
        in_specs=[*_halo_specs(3 * A_W, R), pl.BlockSpec((TMR, wp), lambda i: (i, 0)),
                  pl.BlockSpec((3, A_W), lambda i: (0, 0)), pl.BlockSpec((B_CONV, A_W), lambda i: (0, 0))],
        out_specs=pl.BlockSpec((TMR, wp), lambda i: (i, 0)),
        out_shape=jax.ShapeDtypeStruct((R, wp), BF16), compiler_params=_params(48))(dmid, dmid, dmid, p, wa, wb)


def _rot_half(v):
    w = v.shape[-1]
    lane = lax.broadcasted_iota(jnp.int32, (1, w), 1)
    return jnp.where(lane % HEAD_DIM < HEAD_DIM // 2, pltpu.roll(v, w - HEAD_DIM // 2, 1),
                     pltpu.roll(v, HEAD_DIM // 2, 1))


def rope_fwd(qkv, cs, sn):
    R, wq = qkv.shape
    qw = N_HEADS * HEAD_DIM
    kw = (wq - qw) // 2
    scale = HEAD_DIM ** -0.5

    def body(x_ref, cs_ref, sn_ref, o_ref):
        c, s = cs_ref[...], sn_ref[...]
        q = x_ref[:, 0:qw]
        o_ref[:, 0:qw] = ((q * jnp.tile(c, (1, qw // 128)) + _rot_half(q) * jnp.tile(s, (1, qw // 128)))
                          * scale).astype(BF16)
        k = x_ref[:, qw:qw + kw]
        o_ref[:, qw:qw + kw] = (k * jnp.tile(c, (1, kw // 128))
                                + _rot_half(k) * jnp.tile(s, (1, kw // 128))).astype(BF16)
        o_ref[:, qw + kw:] = x_ref[:, qw + kw:].astype(BF16)
    tab = pl.BlockSpec((TMR, 128), lambda i: (i, 0))
    return pl.pallas_call(
        body, name="rope_fwd", grid=(R // TMR,),
        in_specs=[pl.BlockSpec((TMR, wq), lambda i: (i, 0)), tab, tab],
        out_specs=pl.BlockSpec((TMR, wq), lambda i: (i, 0)),
        out_shape=jax.ShapeDtypeStruct((R, wq), BF16))(qkv, cs, sn)


def rope_bwd(dq, dks, dvs, dkc, dvc, cs, sn, T):
    R, qw = dq.shape
    kw = dkc.shape[1]
    nb = R // QB
    nl = T // QB
    scale = HEAD_DIM ** -0.5

    def body(dq_ref, kp_ref, ko_ref, kn_ref, vp_ref, vo_ref, vn_ref, kc_ref, vc_ref, cs_ref, sn_ref, o_ref):
        b = pl.program_id(0)
        c, s = cs_ref[...], sn_ref[...]
        has_next = (b + 1 < nb).astype(F32)
        has_prev = (b >= 1).astype(F32)
        is_ctx = (b >= nl).astype(F32)
        g = dq_ref[...] * scale
        o_ref[:, 0:qw] = (g * jnp.tile(c, (1, qw // 128)) + _rot_half(g * jnp.tile(s, (1, qw // 128)))).astype(BF16)
        g = ko_ref[...] + kp_ref[...] * has_next + kn_ref[...] * has_prev + kc_ref[...] * is_ctx
        o_ref[:, qw:qw + kw] = (g * jnp.tile(c, (1, kw // 128))
                                + _rot_half(g * jnp.tile(s, (1, kw // 128)))).astype(BF16)
        o_ref[:, qw + kw:] = (vo_ref[...] + vp_ref[...] * has_next + vn_ref[...] * has_prev
                              + vc_ref[...] * is_ctx).astype(BF16)
    own = pl.BlockSpec((QB, kw), lambda b: (b, 0))
    from_next = pl.BlockSpec((QB, kw), lambda b: (jnp.minimum(b + 1, nb - 1), 0))
    from_prev = pl.BlockSpec((QB, kw), lambda b: (jnp.maximum(b - 1, 0), 0))
    ctx = pl.BlockSpec((QB, kw), lambda b: (jnp.maximum(b - nl, 0), 0))
    tab = pl.BlockSpec((QB, 128), lambda b: (b, 0))
    return pl.pallas_call(
        body, name="rope_bwd", grid=(nb,),
        in_specs=[pl.BlockSpec((QB, qw), lambda b: (b, 0)), from_next, own, from_prev, from_next, own, from_prev,
                  ctx, ctx, tab, tab],
        out_specs=pl.BlockSpec((QB, qw + 2 * kw), lambda b: (b, 0)),
        out_shape=jax.ShapeDtypeStruct((R, qw + 2 * kw), BF16))(
            dq, dks[0], dks[1], dks[2], dvs[0], dvs[1], dvs[2], dkc, dvc, cs, sn)


def _attn_specs(T, R):
    nl = T // QB
    qcols = N_HEADS * HEAD_DIM // 128
    kcols = 2

    def band(col0, shift):
        return pl.BlockSpec((QB, 128), lambda jj, b: (jnp.clip(b + shift, 0, nl - 1), col0 + jj))

    def ctx(col0):
        return pl.BlockSpec((R - T, 128), lambda jj, b: (T // (R - T), col0 + jj))
    q = pl.BlockSpec((QB, 512), lambda jj, b: (b, jj))
    k0, v0 = qcols, qcols + kcols
    return q, [band(k0, -1), band(k0, 0), band(k0, 1), ctx(k0)], [band(v0, -1), band(v0, 0), band(v0, 1), ctx(v0)]


def _attn_common(T, R):
    nl = T // QB

    def low_lanes():
        return lax.broadcasted_iota(jnp.int32, (1, 128), 1) < HEAD_DIM

    def dup(v, par):
        low = low_lanes()
        vf = v.astype(F32)
        r = pltpu.roll(vf, HEAD_DIM, 1)
        return (jnp.where(low, vf, r) if par == 0 else jnp.where(low, r, vf)).astype(BF16)

    def mask_of(b):
        nk = 3 * QB + (R - T)
        col = lax.broadcasted_iota(jnp.int32, (QB, nk), 1)
        row = lax.broadcasted_iota(jnp.int32, (QB, nk), 0)
        qpos = b * QB + row
        kpos = (b - 1) * QB + col
        in_band = (jnp.abs(qpos - kpos) <= WINDOW) & (kpos >= 0) & (kpos < T) & (b < nl)
        return in_band | (col >= 3 * QB)

    def probs(qm, kd, mask, sink):
        s = lax.dot_general(qm, kd, (((1,), (1,)), ((), ())), preferred_element_type=F32)
        s = jnp.where(mask, s, NEG_INF)
        m = jnp.maximum(jnp.max(s, axis=-1, keepdims=True), sink)
        e = jnp.exp(s - m)
        es = jnp.exp(sink - m)
        z = jnp.sum(e, axis=-1, keepdims=True) + es
        return e / z, es / z
    return low_lanes, dup, mask_of, probs


def attn_fwd(qkvr, sinks, T):
    R = qkvr.shape[0]
    qspec, kspecs, vspecs = _attn_specs(T, R)
    low_lanes, dup, mask_of, probs = _attn_common(T, R)

    def body(q_ref, kp, ko, kn, kc, vp, vo, vn, vc, sink_ref, o_ref):
        jj, b = pl.program_id(0), pl.program_id(1)
        low = low_lanes()
        mask = mask_of(b)
        k_all = jnp.concatenate([kp[...], ko[...], kn[...], kc[...]], axis=0)
        v_all = jnp.concatenate([vp[...], vo[...], vn[...], vc[...]], axis=0)
        for par in range(2):
            kd, vd = dup(k_all, par), dup(v_all, par)
            for pp in range(2):
                pair = None
                qp = q_ref[:, (2 * par + pp) * 128:(2 * par + pp + 1) * 128]
                for half in range(2):
                    sel = low if half == 0 else jnp.logical_not(low)
                    qm = jnp.where(sel, qp, jnp.zeros_like(qp))
                    sink = sink_ref[jj * 8 + par * 4 + pp * 2 + half]
                    p, _ = probs(qm, kd, mask, sink)
                    o = jnp.dot(p.astype(BF16), vd, preferred_element_type=F32)
                    pair = o if pair is None else jnp.where(low, pair, o)
                o_ref[:, (2 * par + pp) * 128:(2 * par + pp + 1) * 128] = pair.astype(BF16)
    return pl.pallas_call(
        body, name="attn_fwd", grid=(2, R // QB),
        in_specs=[qspec, *kspecs, *vspecs, pl.BlockSpec(memory_space=pltpu.SMEM)],
        out_specs=pl.BlockSpec((QB, 512), lambda jj, b: (b, jj)),
        out_shape=jax.ShapeDtypeStruct((R, N_HEADS * HEAD_DIM), BF16))(
            qkvr, *([qkvr] * 8), sinks)


def attn_bwd(qkvr, do, sinks, T):
    R = qkvr.shape[0]
    tc = R - T
    qspec, kspecs, vspecs = _attn_specs(T, R)
    low_lanes, dup, mask_of, probs = _attn_common(T, R)
    contract_rows = (((0,), (0,)), ((), ()))
    contract_last = (((1,), (1,)), ((), ()))

    def body(q_ref, kp, ko, kn, kc, vp, vo, vn, vc, do_ref, sink_ref,
             dq_ref, dkp, dko, dkn, dvp, dvo, dvn, dkc_ref, dvc_ref, dsink_ref):
        jj, b = pl.program_id(0), pl.program_id(1)

        @pl.when((jj == 0) & (b == 0))
        def _():
            dsink_ref[...] = jnp.zeros_like(dsink_ref)

        @pl.when(b == 0)
        def _():
            dkc_ref[...] = jnp.zeros_like(dkc_ref)
            dvc_ref[...] = jnp.zeros_like(dvc_ref)
        low = low_lanes()
        mask = mask_of(b)
        k_all = jnp.concatenate([kp[...], ko[...], kn[...], kc[...]], axis=0)
        v_all = jnp.concatenate([vp[...], vo[...], vn[...], vc[...]], axis=0)
        lane = lax.broadcasted_iota(jnp.int32, (8, 128), 1)
        srow = lax.broadcasted_iota(jnp.int32, (8, 128), 0)
        dk_fold, dv_fold = [], []
        for par in range(2):
            kd, vd = dup(k_all, par), dup(v_all, par)
            dk_acc = jnp.zeros((k_all.shape[0], 128), F32)
            dv_acc = jnp.zeros((k_all.shape[0], 128), F32)
            for pp in range(2):
                pair = None
                c0 = (2 * par + pp) * 128
                qp = q_ref[:, c0:c0 + 128]
                dop = do_ref[:, c0:c0 + 128].astype(BF16)
                for half in range(2):
                    sel = low if half == 0 else jnp.logical_not(low)
                    qm = jnp.where(sel, qp, jnp.zeros_like(qp))
                    dom = jnp.where(sel, dop, jnp.zeros_like(dop))
                    head = jj * 8 + par * 4 + pp * 2 + half
                    p, ps = probs(qm, kd, mask, sink_ref[head])
                    dp = lax.dot_general(dom, vd, contract_last, preferred_element_type=F32)
                    delta = jnp.sum(p * dp, axis=-1, keepdims=True)
                    ds = (p * (dp - delta)).astype(BF16)
                    dsink = -jnp.sum(ps * delta)
                    dsink_ref[...] += jnp.where((lane == head) & (srow == 0), dsink, 0.0)
                    dq = jnp.dot(ds, kd, preferred_element_type=F32)
                    pair = dq if pair is None else jnp.where(low, pair, dq)
                    dk_acc = dk_acc + lax.dot_general(ds, qm, contract_rows, preferred_element_type=F32)
                    dv_acc = dv_acc + lax.dot_general(p.astype(BF16), dom, contract_rows,
                                                      preferred_element_type=F32)
                dq_ref[:, c0:c0 + 128] = pair
            dk_fold.append(dk_acc + pltpu.roll(dk_acc, HEAD_DIM, 1))
            dv_fold.append(dv_acc + pltpu.roll(dv_acc, HEAD_DIM, 1))
        dk = jnp.where(low, dk_fold[0], dk_fold[1])
        dv = jnp.where(low, dv_fold[0], dv_fold[1])
        dkp[...], dko[...], dkn[...] = dk[0:QB], dk[QB:2 * QB], dk[2 * QB:3 * QB]
        dvp[...], dvo[...], dvn[...] = dv[0:QB], dv[QB:2 * QB], dv[2 * QB:3 * QB]
        dkc_ref[...] += dk[3 * QB:]
        dvc_ref[...] += dv[3 * QB:]
    blk = pl.BlockSpec((QB, 128), lambda jj, b: (b, jj))
    cblk = pl.BlockSpec((tc, 128), lambda jj, b: (0, jj))
    part = jax.ShapeDtypeStruct((R, 256), F32)
    csum = jax.ShapeDtypeStruct((tc, 256), F32)
    outs = pl.pallas_call(
        body, name="attn_bwd", grid=(2, R // QB),
        in_specs=[qspec, *kspecs, *vspecs, pl.BlockSpec((QB, 512), lambda jj, b: (b, jj)),
                  pl.BlockSpec(memory_space=pltpu.SMEM)],
        out_specs=[pl.BlockSpec((QB, 512), lambda jj, b: (b, jj)), blk, blk, blk, blk, blk, blk, cblk, cblk,
                   pl.BlockSpec((8, 128), lambda jj, b: (0, 0))],
        out_shape=[jax.ShapeDtypeStruct((R, N_HEADS * HEAD_DIM), F32), part, part, part, part, part, part,
                   csum, csum, jax.ShapeDtypeStruct((8, 128), F32)],
        compiler_params=_params(48))(qkvr, *([qkvr] * 8), do, sinks)
    return outs[0], outs[1:4], outs[4:7], outs[7], outs[8], outs[9]


def loss_head(x, nw, target, T):
    R, dm = x.shape
    nl = T // TMR

    def body(x_ref, nw_ref, t_ref, loss_ref, dx_ref, dnw_ref):
        i = pl.program_id(0)

        @pl.when(i == 0)
        def _():
            loss_ref[...] = jnp.zeros_like(loss_ref)
            dnw_ref[...] = jnp.zeros_like(dnw_ref)
        live = (i < nl).astype(F32)
        nwv = nw_ref[...]
        xv = x_ref[...]
        r = lax.rsqrt(jnp.mean(xv * xv, axis=-1, keepdims=True) + EPS)
        xh = xv * r
        err = xh * nwv - t_ref[...]
        per_row = jnp.mean(err * err, axis=-1, keepdims=True)
        loss_ref[...] += 0.5 * live * jnp.sum(per_row)
        dy = err * (live / dm)
        dnw_ref[...] += _colsum8(dy * xh)
        dxh = dy * nwv
        dx_ref[...] = r * (dxh - xh * jnp.mean(dxh * xh, axis=-1, keepdims=True))
    tile = pl.BlockSpec((TMR, dm), lambda i: (i, 0))
    return pl.pallas_call(
        body, name="loss_head", grid=(R // TMR,),
        in_specs=[tile, pl.BlockSpec((1, dm), lambda i: (0, 0)),
                  pl.BlockSpec((TMR, dm), lambda i: (jnp.minimum(i, nl - 1), 0))],
        out_specs=[pl.BlockSpec((8, 128), lambda i: (0, 0)), tile, pl.BlockSpec((8, dm), lambda i: (0, 0))],
        out_shape=[jax.ShapeDtypeStruct((8, 128), F32), jax.ShapeDtypeStruct((R, dm), F32),
                   jax.ShapeDtypeStruct((8, dm), F32)])(x, nw, target)


def adaln_fwd(cond, w_mod, b_mod):
    nl, dm, ns = w_mod.shape

    def body(c_ref, w_ref, b_ref, o_ref):
        cv = c_ref[...]
        s = (cv * _sigmoid(cv)).astype(BF16)
        o_ref[...] = jnp.dot(s, w_ref[...].astype(BF16), preferred_element_type=F32) + b_ref[...]
    return pl.pallas_call(
        body, name="adaln_fwd", grid=(nl,),
        in_specs=[pl.BlockSpec((16, dm), lambda l: (0, 0)), pl.BlockSpec((None, dm, ns), lambda l: (l, 0, 0)),
                  pl.BlockSpec((None, 1, ns), lambda l: (l, 0, 0))],
        out_specs=pl.BlockSpec((None, 16, ns), lambda l: (l, 0, 0)),
        out_shape=jax.ShapeDtypeStruct((nl, 16, ns), F32), compiler_params=_params(48))(cond, w_mod, b_mod)


def adaln_bwd(cond, dmod, w_mod):
    nl, dm, ns = w_mod.shape

    def body(c_ref, d_ref, w_ref, gw_ref, ds_ref):
        l = pl.program_id(0)

        @pl.when(l == 0)
        def _():
            ds_ref[...] = jnp.zeros_like(ds_ref)
        cv = c_ref[...]
        s = (cv * _sigmoid(cv)).astype(BF16)
        dv = d_ref[...].astype(BF16)
        gw_ref[...] = lax.dot_general(s, dv, (((0,), (0,)), ((), ())), preferred_element_type=F32)
        ds_ref[...] += lax.dot_general(dv, w_ref[...].astype(BF16), (((1,), (1,)), ((), ())),
                                       preferred_element_type=F32)
    return pl.pallas_call(
        body, name="adaln_bwd", grid=(nl,),
        in_specs=[pl.BlockSpec((16, dm), lambda l: (0, 0)), pl.BlockSpec((None, 16, ns), lambda l: (l, 0, 0)),
                  pl.BlockSpec((None, dm, ns), lambda l: (l, 0, 0))],
        out_specs=[pl.BlockSpec((None, dm, ns), lambda l: (l, 0, 0)), pl.BlockSpec((16, dm), lambda l: (0, 0))],
        out_shape=[jax.ShapeDtypeStruct((nl, dm, ns), F32), jax.ShapeDtypeStruct((16, dm), F32)],
        compiler_params=_params(48))(cond, dmod, w_mod)


def _me():
    return lax.axis_index("x"), lax.axis_index("y"), lax.axis_index("c")


def allgather8(block):
    m_per, n = block.shape

    def body(x_ref, out_ref, send_sems, recv_sems, local_sem):
        x, y, c = _me()
        me, sibling = (x, y, c), (x, y, 1 - c)
        chips = [(1 - x, y), (x, 1 - y), (1 - x, 1 - y)]

        def rows(px, py, pc):
            return out_ref.at[pl.ds((4 * px + 2 * py + pc) * m_per, m_per), :]

        def copy(k, blk, to, src=None):
            return pltpu.make_async_remote_copy(
                src_ref=rows(*blk) if src is None else src, dst_ref=rows(*blk),
                send_sem=send_sems.at[k], recv_sem=recv_sems.at[k], device_id=to, device_id_type=MESH)
        mine = pltpu.make_async_copy(x_ref, rows(*me), local_sem)
        mine.start()
        first = [copy(0, me, sibling, src=x_ref)]
        first += [copy(1 + j, me, (*chip, c), src=x_ref) for j, chip in enumerate(chips)]
        for cp in first:
            cp.start()
        passed = [copy(4 + j, (*chip, c), sibling) for j, chip in enumerate(chips)]
        for j, chip in enumerate(chips):
            copy(1 + j, (*chip, c), me).wait_recv()
            passed[j].start()
        copy(0, sibling, me).wait_recv()
        for j, chip in enumerate(chips):
            copy(4 + j, (*chip, 1 - c), me).wait_recv()
        for cp in first + passed:
            cp.wait_send()
        mine.wait()
    return pl.pallas_call(
        body, name="allgather8",
        out_shape=jax.ShapeDtypeStruct((N_DEV * m_per, n), block.dtype),
        in_specs=[pl.BlockSpec(memory_space=pltpu.VMEM)],
        out_specs=pl.BlockSpec(memory_space=pltpu.VMEM),
        scratch_shapes=[pltpu.SemaphoreType.DMA((7,)), pltpu.SemaphoreType.DMA((7,)), pltpu.SemaphoreType.DMA],
        compiler_params=_params(48))(block)


def _other_chips(x, y):
    return [(1 - x, y), (x, 1 - y), (1 - x, 1 - y)]


def gather_weight(shard):
    def body(s_ref, out_ref, send_sems, recv_sems, local_sem):
        x, y, c = _me()
        mine = pltpu.make_async_copy(s_ref, out_ref.at[2 * x + y], local_sem)
        mine.start()

        def copy(k, chip, src_chip):
            return pltpu.make_async_remote_copy(
                src_ref=s_ref, dst_ref=out_ref.at[2 * src_chip[0] + src_chip[1]],
                send_sem=send_sems.at[k], recv_sem=recv_sems.at[k], device_id=(*chip, c), device_id_type=MESH)
        chips = _other_chips(x, y)
        sends = [copy(k, chip, (x, y)) for k, chip in enumerate(chips)]
        for cp in sends:
            cp.start()
        for k, chip in enumerate(chips):
            copy(k, chip, chip).wait_recv()
        for cp in sends:
            cp.wait_send()
        mine.wait()
    return pl.pallas_call(
        body, name="gather_weight",
        out_shape=jax.ShapeDtypeStruct((N_CHIP, *shard.shape), shard.dtype),
        in_specs=[pl.BlockSpec(memory_space=pl.ANY)], out_specs=pl.BlockSpec(memory_space=pl.ANY),
        scratch_shapes=[pltpu.SemaphoreType.DMA((3,)), pltpu.SemaphoreType.DMA((3,)),
                        pltpu.SemaphoreType.DMA])(shard)


def scatter_grad(grad):
    def body(g_ref, out_ref, send_sems, recv_sems, local_sem):
        x, y, c = _me()
        mine = pltpu.make_async_copy(g_ref.at[2 * x + y], out_ref.at[3], local_sem)
        mine.start()
        chips = _other_chips(x, y)

        def copy(k, chip):
            return pltpu.make_async_remote_copy(
                src_ref=g_ref.at[2 * chip[0] + chip[1]], dst_ref=out_ref.at[k],
                send_sem=send_sems.at[k], recv_sem=recv_sems.at[k], device_id=(*chip, c), device_id_type=MESH)
        sends = [copy(k, chip) for k, chip in enumerate(chips)]
        for cp in sends:
            cp.start()
        for cp in sends:
            cp.wait_recv()
        for cp in sends:
            cp.wait_send()
        mine.wait()
    return pl.pallas_call(
        body, name="scatter_grad",
        out_shape=jax.ShapeDtypeStruct(grad.shape, grad.dtype),
        in_specs=[pl.BlockSpec(memory_space=pl.ANY)], out_specs=pl.BlockSpec(memory_space=pl.ANY),
        scratch_shapes=[pltpu.SemaphoreType.DMA((3,)), pltpu.SemaphoreType.DMA((3,)),
                        pltpu.SemaphoreType.DMA])(grad)


def swap_with_sibling(v):
    def body(v_ref, out_ref, send_sem, recv_sem):
        x, y, c = _me()
        cp = pltpu.make_async_remote_copy(src_ref=v_ref, dst_ref=out_ref, send_sem=send_sem, recv_sem=recv_sem,
                                          device_id=(x, y, 1 - c), device_id_type=MESH)
        cp.start()
        cp.wait()
    return pl.pallas_call(
        body, name="swap_with_sibling", out_shape=jax.ShapeDtypeStruct(v.shape, v.dtype),
        in_specs=[pl.BlockSpec(memory_space=pl.ANY)], out_specs=pl.BlockSpec(memory_space=pl.ANY),
        scratch_shapes=[pltpu.SemaphoreType.DMA, pltpu.SemaphoreType.DMA])(v)


def sum_slots(parts):
    n, rows, w = parts.shape
    tr = _row_tile(rows)

    def body(p_ref, o_ref):
        acc = p_ref[0].astype(F32)
        for k in range(1, n):
            acc = acc + p_ref[k].astype(F32)
        o_ref[...] = acc
    return pl.pallas_call(
        body, name="sum_slots", grid=(rows // tr,),
        in_specs=[pl.BlockSpec((n, tr, w), lambda i: (0, i, 0))],
        out_specs=pl.BlockSpec((tr, w), lambda i: (i, 0)),
        out_shape=jax.ShapeDtypeStruct((rows, w), F32), compiler_params=_params(48))(parts)


def adamw(w, ga, gb, m, v):
    rows, wd = w.shape
    tr = _row_tile(rows)
    tr = min(tr, 256)
    c1 = 1.0 / (1.0 - ADAM_B1 ** ADAM_STEP)
    c2 = 1.0 / (1.0 - ADAM_B2 ** ADAM_STEP)

    def update(wv, g, mv, vv, g_ref, d_ref, m_ref, v_ref):
        mn = ADAM_B1 * mv + (1.0 - ADAM_B1) * g
        vn = ADAM_B2 * vv + (1.0 - ADAM_B2) * (g * g)
        g_ref[...] = g
        m_ref[...] = mn
        v_ref[...] = vn
        d_ref[...] = -ADAM_LR * ((mn * c1) / (jnp.sqrt(vn * c2) + ADAM_EPS) + ADAM_WD * wv)
    tile = pl.BlockSpec((tr, wd), lambda i: (i, 0))
    out = jax.ShapeDtypeStruct((rows, wd), F32)
    if gb is None:
        def body(w_ref, ga_ref, m_ref, v_ref, g_out, d_out, m_out, v_out):
            update(w_ref[...], ga_ref[...], m_ref[...], v_ref[...], g_out, d_out, m_out, v_out)
        return pl.pallas_call(body, name="adamw", grid=(rows // tr,), in_specs=[tile] * 4,
                              out_specs=[tile] * 4, out_shape=[out] * 4)(w, ga, m, v)

    def body(w_ref, ga_ref, gb_ref, m_ref, v_ref, g_out, d_out, m_out, v_out):
        update(w_ref[...], ga_ref[...] + gb_ref[...], m_ref[...], v_ref[...], g_out, d_out, m_out, v_out)
    return pl.pallas_call(body, name="adamw_sum", grid=(rows // tr,), in_specs=[tile] * 5,
                          out_specs=[tile] * 4, out_shape=[out] * 4)(w, ga, gb, m, v)


def _rope_tables(T, R):
    rows = T // GRID_W
    row = jnp.repeat(jnp.arange(rows), GRID_W).astype(F32)
    col = jnp.tile(jnp.arange(GRID_W), rows).astype(F32)
    n_freq = HEAD_DIM // 4
    inv_freq = ROPE_THETA ** (-jnp.arange(n_freq, dtype=F32) / n_freq)
    ang = jnp.concatenate([row[:, None] * inv_freq, col[:, None] * inv_freq], axis=-1)
    cos, sin = jnp.cos(ang), jnp.sin(ang)
    cs = jnp.tile(cos, (1, 4))
    sn = jnp.tile(jnp.concatenate([-sin, sin], axis=-1), (1, 2))
    pad = R - T
    return (jnp.concatenate([cs, jnp.ones((pad, 128), F32)], axis=0),
            jnp.concatenate([sn, jnp.zeros((pad, 128), F32)], axis=0))


def _pack(parts, mult=8 * 128):
    flat = jnp.concatenate([p.reshape(-1).astype(F32) for p in parts])
    pad = (-flat.shape[0]) % mult
    return jnp.pad(flat, (0, pad)).reshape(-1, 128)


def _unpack(buf, shapes):
    flat = buf.reshape(-1)
    out, o = [], 0
    for s in shapes:
        n = 1
        for d in s:
            n *= d
        out.append(flat[o:o + n].reshape(s))
        o += n
    return out


def kernel(x, c, ctx, c_ctx, w_mod, b_mod, norm_mix, norm_ffn, w_in_ab, conv_a, conv_b, conv_b_bias, ln_b_gain, ln_b_bias, w_out_ab, w_qkv, w_o, sinks, w_up, w_conv_ffn, w_down, final_norm, loss_target, m_c_ctx, m_w_mod, m_b_mod, m_norm_mix, m_norm_ffn, m_w_in_ab, m_conv_a, m_conv_b, m_conv_b_bias, m_ln_b_gain, m_ln_b_bias, m_w_out_ab, m_w_qkv, m_w_o, m_sinks, m_w_up, m_w_conv_ffn, m_w_down, m_final_norm, v_c_ctx, v_w_mod, v_b_mod, v_norm_mix, v_norm_ffn, v_w_in_ab, v_conv_a, v_conv_b, v_conv_b_bias, v_ln_b_gain, v_ln_b_bias, v_w_out_ab, v_w_qkv, v_w_o, v_sinks, v_w_up, v_w_conv_ffn, v_w_down, v_final_norm):
    T, dm = x.shape[1], x.shape[2]
    tc = ctx.shape[1]
    R = T + tc
    depth = w_mod.shape[0]
    ax, ay, ac = lax.axis_index("x"), lax.axis_index("y"), lax.axis_index("c")
    chip = 2 * ax + ay
    dev = 4 * ax + 2 * ay + ac

    small_w = [conv_a, conv_b, w_conv_ffn]
    gathered = allgather8(_pack([c] + small_w)).reshape(N_DEV, -1)
    cond8 = gathered[:, :dm]
    off = dm
    full_small = []
    for wsh in small_w:
        n = wsh.size
        per_chip = gathered[0::2, off:off + n].reshape((N_CHIP,) + wsh.shape)
        full_small.append(jnp.concatenate([per_chip[q] for q in range(N_CHIP)], axis=-1))
        off += n
    conv_a_f, conv_b_f, w_conv_ffn_f = full_small
    cond = jnp.concatenate([cond8, c_ctx[None, :], jnp.zeros((7, dm), F32)], axis=0)

    ns_mod = w_mod.shape[2]
    b_mod_sh = lax.dynamic_slice_in_dim(b_mod, chip * ns_mod, ns_mod, axis=1)[:, None, :]
    mod_sh = adaln_fwd(cond, w_mod, b_mod_sh)
    mod_all = allgather8(mod_sh.reshape(depth * 16, ns_mod)).reshape(N_DEV, depth, 16, ns_mod)
    mod_full = jnp.concatenate([mod_all[2 * q] for q in range(N_CHIP)], axis=-1)
    mine = lax.dynamic_index_in_dim(mod_full, dev, axis=1, keepdims=False)
    mods = jnp.stack([mine, mod_full[:, 8]], axis=1).reshape(depth, 2, 6, dm)

    big = {"w_in_ab": (w_in_ab, "col"), "w_out_ab": (w_out_ab, "row"), "w_qkv": (w_qkv, "col"),
           "w_o": (w_o, "row"), "w_up": (w_up, "col"), "w_down": (w_down, "row")}
    W = {n: gather_weight(w.astype(BF16)) for n, (w, _) in big.items()}
    G = {n: None for n in big}

    def grad_into(name, a, d, l):
        G[name] = mm_tn(a, d, l, big[name][1], W[name], G[name])

    cs, sn = _rope_tables(T, R)
    sinks_flat = sinks.reshape(-1)

    xs = jnp.concatenate([x[0], ctx[0]], axis=0)
    saved = []
    for l in range(depth):
        e = l // 2
        s = {"x0": xs}
        h1 = norm_mod_fwd(xs, norm_mix[l][None], mods[l], 0, T)
        s["h1"] = h1
        if l % 2 == 0:
            p = mm_nn(h1, W["w_in_ab"], e, "col", BF16)
            yab = convmix_fwd(p, conv_a_f[e], conv_b_f[e], conv_b_bias[e][None], ln_b_gain[e][None],
                              ln_b_bias[e][None], T)
            y1 = mm_nn(yab, W["w_out_ab"], e, "row", F32)
            s.update(p=p, mix=yab)
        else:
            qkv = mm_nn(h1, W["w_qkv"], e, "col", F32)
            qkvr = rope_fwd(qkv, cs, sn)
            att = attn_fwd(qkvr, sinks_flat[e * N_HEADS:(e + 1) * N_HEADS], T)
            y1 = mm_nn(att, W["w_o"], e, "row", F32)
            s.update(qkvr=qkvr, mix=att)
        x1 = resid_fwd(xs, y1, mods[l], 2, T)
        h2 = norm_mod_fwd(x1, norm_ffn[l][None], mods[l], 3, T)
        u = mm_nn(h2, W["w_up"], l, "col", BF16)
        z = ffnconv_fwd(u, w_conv_ffn_f[l], T)
        y2 = mm_nn(z, W["w_down"], l, "row", F32)
        xs = resid_fwd(x1, y2, mods[l], 5, T)
        s.update(y1=y1, x1=x1, h2=h2, u=u, z=z, y2=y2)
        saved.append(s)

    loss_part, dx, d_final = loss_head(xs, final_norm[None], loss_target[0], T)
    loss = lax.psum(loss_part[0, 0], ("x", "y", "c"))

    d_mods, d_norm_mix, d_norm_ffn = [None] * depth, [None] * depth, [None] * depth
    d_conv_a, d_conv_b, d_vecs, d_sinks, d_wc = [None] * 2, [None] * 2, [None] * 2, [None] * 2, [None] * depth
    for l in reversed(range(depth)):
        e = l // 2
        s = saved[l]
        dy2, dg2 = resid_bwd(dx, s["y2"], mods[l], 5, T)
        grad_into("w_down", s["z"], dy2, l)
        dz = mm_nt(dy2, W["w_down"], l, "row", BF16)
        duc, d_wc[l] = ffnconv_bwd1(dz, s["u"], w_conv_ffn_f[l], T)
        du = ffnconv_bwd2(duc, w_conv_ffn_f[l], T)
        grad_into("w_up", s["h2"], du, l)
        dh2 = mm_nt(du, W["w_up"], l, "col", F32)
        dx, dss2, d_norm_ffn[l] = norm_mod_bwd(dh2, s["x1"], norm_ffn[l][None], mods[l], dx, 3, T)
        dy1, dg1 = resid_bwd(dx, s["y1"], mods[l], 2, T)
        if l % 2 == 0:
            grad_into("w_out_ab", s["mix"], dy1, e)
            dyab = mm_nt(dy1, W["w_out_ab"], e, "row", F32)
            dmid, d_conv_a[e], d_conv_b[e], d_vecs[e] = convmix_bwd1(
                dyab, s["p"], conv_a_f[e], conv_b_f[e], conv_b_bias[e][None], ln_b_gain[e][None],
                ln_b_bias[e][None], T)
            dp = convmix_bwd2(dmid, s["p"], conv_a_f[e], conv_b_f[e], T)
            grad_into("w_in_ab", s["h1"], dp, e)
            dh1 = mm_nt(dp, W["w_in_ab"], e, "col", F32)
        else:
            grad_into("w_o", s["mix"], dy1, e)
            datt = mm_nt(dy1, W["w_o"], e, "row", BF16)
            dq, dks, dvs, dkc, dvc, d_sinks[e] = attn_bwd(
                s["qkvr"], datt, sinks_flat[e * N_HEADS:(e + 1) * N_HEADS], T)
            dqkv = rope_bwd(dq, dks, dvs, dkc, dvc, cs, sn, T)
            grad_into("w_qkv", s["h1"], dqkv, e)
            dh1 = mm_nt(dqkv, W["w_qkv"], e, "col", F32)
        dx, dss1, d_norm_mix[l] = norm_mod_bwd(dh1, s["x0"], norm_mix[l][None], mods[l], dx, 0, T)
        dss1, dss2, dg1, dg2 = dss1.sum(2), dss2.sum(2), dg1.sum(1), dg2.sum(1)
        d_mods[l] = jnp.stack([dss1[:, 0], dss1[:, 1], dg1, dss2[:, 0], dss2[:, 1], dg2], axis=1)
    grad_x = dx[:T][None]

    d_mods = jnp.stack(d_mods)
    summed_parts = [
        d_mods[:, 1],
        jnp.stack(d_norm_mix).sum(1), jnp.stack(d_norm_ffn).sum(1),
        jnp.stack(d_conv_a).sum(2), jnp.stack(d_conv_b).sum(2),
        jnp.stack(d_vecs).sum(2),
        jnp.stack(d_sinks)[:, 0, :N_HEADS],
        jnp.stack(d_wc).sum(2), d_final.sum(0)]
    summed_shapes = [p.shape for p in summed_parts]
    n_own = depth * 6 * dm
    pack = _pack([d_mods[:, 0]] + summed_parts)
    parts = allgather8(pack).reshape(N_DEV, -1, 128)
    total = sum_slots(parts)
    own_rows = parts.reshape(N_DEV, -1)[:, :n_own].reshape(N_DEV, depth, 6 * dm)
    (dmod_ctx, g_norm_mix, g_norm_ffn, g_conv_a, g_conv_b, g_vecs, g_sinks, g_wc, g_final) = _unpack(
        total.reshape(-1)[n_own:], summed_shapes)
    dmod_rows = jnp.concatenate([jnp.moveaxis(own_rows, 0, 1), dmod_ctx.reshape(depth, 1, 6 * dm),
                                 jnp.zeros((depth, 7, 6 * dm), F32)], axis=1)
    g_b_mod = dmod_rows.sum(1)
    dmod_sh = lax.dynamic_slice_in_dim(dmod_rows, chip * ns_mod, ns_mod, axis=2)
    g_w_mod, dsilu = adaln_bwd(cond, dmod_sh, w_mod)
    dsilu_all = allgather8(dsilu[8:16]).reshape(N_DEV, 8, dm)
    dsilu_ctx = sum_slots(dsilu_all[0::2])[0]
    sg = jax.nn.sigmoid(c_ctx)
    g_c_ctx = dsilu_ctx * (sg * (1.0 + c_ctx * (1.0 - sg)))

    def shard_cols(full, width):
        return lax.dynamic_slice_in_dim(full, chip * width, width, axis=full.ndim - 1)
    g_conv_a_s = shard_cols(g_conv_a, conv_a.shape[-1])
    g_conv_b_s = shard_cols(g_conv_b, conv_b.shape[-1])
    g_wc_s = shard_cols(g_wc, w_conv_ffn.shape[-1])

    grads, deltas, new_m, new_v = {}, {}, {}, {}

    def step_2d(name, wv, ga, gb, mv, vv):
        shp = wv.shape
        r2 = lambda t: t.reshape(-1, shp[-1])
        g, d, mn, vn = adamw(r2(wv), r2(ga), None if gb is None else r2(gb), r2(mv), r2(vv))
        grads[name], deltas[name], new_m[name], new_v[name] = (t.reshape(shp) for t in (g, d, mn, vn))

    moments = {"w_in_ab": (m_w_in_ab, v_w_in_ab), "w_out_ab": (m_w_out_ab, v_w_out_ab),
               "w_qkv": (m_w_qkv, v_w_qkv), "w_o": (m_w_o, v_w_o), "w_up": (m_w_up, v_w_up),
               "w_down": (m_w_down, v_w_down)}
    for name, (wv, _) in big.items():
        landed = scatter_grad(G[name])
        part = sum_slots(landed.reshape(N_CHIP, -1, wv.shape[-1]))
        other = swap_with_sibling(part)
        step_2d(name, wv, part.reshape(wv.shape), other.reshape(wv.shape), *moments[name])
    step_2d("w_mod", w_mod, g_w_mod, None, m_w_mod, v_w_mod)

    small = [("c_ctx", c_ctx, g_c_ctx, m_c_ctx, v_c_ctx), ("b_mod", b_mod, g_b_mod, m_b_mod, v_b_mod),
             ("norm_mix", norm_mix, g_norm_mix, m_norm_mix, v_norm_mix),
             ("norm_ffn", norm_ffn, g_norm_ffn, m_norm_ffn, v_norm_ffn),
             ("conv_a", conv_a, g_conv_a_s, m_conv_a, v_conv_a), ("conv_b", conv_b, g_conv_b_s, m_conv_b, v_conv_b),
             ("conv_b_bias", conv_b_bias, g_vecs[:, 0], m_conv_b_bias, v_conv_b_bias),
             ("ln_b_gain", ln_b_gain, g_vecs[:, 1], m_ln_b_gain, v_ln_b_gain),
             ("ln_b_bias", ln_b_bias, g_vecs[:, 2], m_ln_b_bias, v_ln_b_bias),
             ("sinks", sinks, g_sinks, m_sinks, v_sinks),
             ("w_conv_ffn", w_conv_ffn, g_wc_s, m_w_conv_ffn, v_w_conv_ffn),
             ("final_norm", final_norm, g_final, m_final_norm, v_final_norm)]
    shapes = [t[1].shape for t in small]
    packed = [_pack([t[k] for t in small]) for k in (1, 2, 3, 4)]
    n_real = sum(t[1].size for t in small)
    lane_id = jnp.arange(packed[3].size).reshape(packed[3].shape)
    packed[3] = jnp.where(lane_id < n_real, packed[3], 1.0)
    outs = adamw(packed[0], packed[1], None, packed[2], packed[3])
    for (name, *_), g, d, mn, vn in zip(small, *[_unpack(o, shapes) for o in outs]):
        grads[name], deltas[name], new_m[name], new_v[name] = g, d, mn, vn

    order = ["c_ctx", "w_mod", "b_mod", "norm_mix", "norm_ffn", "w_in_ab", "conv_a", "conv_b", "conv_b_bias",
             "ln_b_gain", "ln_b_bias", "w_out_ab", "w_qkv", "w_o", "sinks", "w_up", "w_conv_ffn", "w_down",
             "final_norm"]
    return (loss, grad_x, *[grads[n] for n in order], *[deltas[n] for n in order],
            *[new_m[n] for n in order], *[new_v[n] for n in order])
```

```python
import jax
import jax.numpy as jnp
from jax import lax
from jax.experimental import pallas as pl
from jax.experimental.pallas import tpu as pltpu

F32 = jnp.float32
BF16 = jnp.bfloat16
MESH = pl.DeviceIdType.MESH

EPS = 1e-6
NEG_INF = -1e30
GRID_W = 64
HEAD_DIM = 64
N_HEADS = 16
WINDOW = 128
QB = 128
ROPE_THETA = 10000.0
A_W = 512
B_CONV = 31
D_FF = 2816
ADAM_LR, ADAM_B1, ADAM_B2, ADAM_EPS, ADAM_WD, ADAM_STEP = 0.001, 0.9, 0.999, 1e-8, 0.01, 10

TMR = 256
HALO = 16
N_DEV = 8
N_CHIP = 4


def _params(vmem_mb=None):
    if vmem_mb is None:
        return pltpu.CompilerParams()
    return pltpu.CompilerParams(vmem_limit_bytes=vmem_mb * 1024 * 1024)


def _row_tile(rows):
    for t in (768, 512, 256, 128, 64, 32, 16, 8):
        if rows % t == 0:
            return t
    raise ValueError(rows)


def _colsum8(v):
    r, c = v.shape
    return v.reshape(r // 8, 8, c).sum(axis=0)


def _sigmoid(v):
    return 1.0 / (1.0 + jnp.exp(-v))


def mm_nn(a, w, kind, out_dtype):
    R = a.shape[0]
    _, kb, nb = w.shape
    tm = _row_tile(R)
    if kind == "col":
        def body(a_ref, w_ref, o_ref):
            o_ref[...] = jnp.dot(a_ref[...].astype(BF16), w_ref[...],
                                 preferred_element_type=F32).astype(o_ref.dtype)
        return pl.pallas_call(
            body, name="mm_nn_col", grid=(N_CHIP, R // tm),
            in_specs=[pl.BlockSpec((tm, kb), lambda q, i: (i, 0)),
                      pl.BlockSpec((None, kb, nb), lambda q, i: (q, 0, 0))],
            out_specs=pl.BlockSpec((tm, nb), lambda q, i: (i, q)),
            out_shape=jax.ShapeDtypeStruct((R, N_CHIP * nb), out_dtype),
            compiler_params=_params(48))(a, w)

    def body(a_ref, w_ref, o_ref):
        wv = w_ref[...].reshape(N_CHIP * kb, nb)
        o_ref[...] = jnp.dot(a_ref[...].astype(BF16), wv, preferred_element_type=F32).astype(o_ref.dtype)
    return pl.pallas_call(
        body, name="mm_nn_row", grid=(R // tm,),
        in_specs=[pl.BlockSpec((tm, N_CHIP * kb), lambda i: (i, 0)),
                  pl.BlockSpec((N_CHIP, kb, nb), lambda i: (0, 0, 0))],
        out_specs=pl.BlockSpec((tm, nb), lambda i: (i, 0)),
        out_shape=jax.ShapeDtypeStruct((R, nb), out_dtype),
        compiler_params=_params(48))(a, w)


def mm_nt(d, w, kind, out_dtype):
    R = d.shape[0]
    _, kb, nb = w.shape
    tm = _row_tile(R)
    contract_last = (((1,), (1,)), ((), ()))
    if kind == "col":
        def body(d_ref, w_ref, o_ref, acc_ref):
            q = pl.program_id(1)

            @pl.when(q == 0)
            def _():
                acc_ref[...] = jnp.zeros_like(acc_ref)
            acc_ref[...] += lax.dot_general(d_ref[...].astype(BF16), w_ref[...], contract_last,
                                            preferred_element_type=F32)

            @pl.when(q == N_CHIP - 1)
            def _():
                o_ref[...] = acc_ref[...].astype(o_ref.dtype)
        return pl.pallas_call(
            body, name="mm_nt_col", grid=(R // tm, N_CHIP),
            in_specs=[pl.BlockSpec((tm, nb), lambda i, q: (i, q)),
                      pl.BlockSpec((None, kb, nb), lambda i, q: (q, 0, 0))],
            out_specs=pl.BlockSpec((tm, kb), lambda i, q: (i, 0)),
            out_shape=jax.ShapeDtypeStruct((R, kb), out_dtype),
            scratch_shapes=[pltpu.VMEM((tm, kb), F32)],
            compiler_params=_params(48))(d, w)

    def body(d_ref, w_ref, o_ref):
        wv = w_ref[...].reshape(N_CHIP * kb, nb)
        o_ref[...] = lax.dot_general(d_ref[...].astype(BF16), wv, contract_last,
                                     preferred_element_type=F32).astype(o_ref.dtype)
    return pl.pallas_call(
        body, name="mm_nt_row", grid=(R // tm,),
        in_specs=[pl.BlockSpec((tm, nb), lambda i: (i, 0)),
                  pl.BlockSpec((N_CHIP, kb, nb), lambda i: (0, 0, 0))],
        out_specs=pl.BlockSpec((tm, N_CHIP * kb), lambda i: (i, 0)),
        out_shape=jax.ShapeDtypeStruct((R, N_CHIP * kb), out_dtype),
        compiler_params=_params(48))(d, w)


def mm_tn(a, d, kind, like):
    R = a.shape[0]
    _, kb, nb = like.shape
    tm = _row_tile(R)
    nsteps = R // tm
    contract_rows = (((0,), (0,)), ((), ()))
    out_shape = jax.ShapeDtypeStruct(like.shape, BF16)

    def accumulate(a_ref, d_ref, acc_ref):
        @pl.when(pl.program_id(1) == 0)
        def _():
            acc_ref[...] = jnp.zeros_like(acc_ref)
        acc_ref[...] += lax.dot_general(a_ref[...].astype(BF16), d_ref[...].astype(BF16), contract_rows,
                                        preferred_element_type=F32)
    if kind == "col":
        def body(a_ref, d_ref, o_ref, acc_ref):
            accumulate(a_ref, d_ref, acc_ref)

            @pl.when(pl.program_id(1) == nsteps - 1)
            def _():
                o_ref[...] = acc_ref[...].astype(BF16)
        return pl.pallas_call(
            body, name="mm_tn_col", grid=(N_CHIP, nsteps),
            in_specs=[pl.BlockSpec((tm, kb), lambda q, i: (i, 0)), pl.BlockSpec((tm, nb), lambda q, i: (i, q))],
            out_specs=pl.BlockSpec((None, kb, nb), lambda q, i: (q, 0, 0)), out_shape=out_shape,
            scratch_shapes=[pltpu.VMEM((kb, nb), F32)], compiler_params=_params(48))(a, d)
    tn = 512

    def body(a_ref, d_ref, o_ref, acc_ref):
        accumulate(a_ref, d_ref, acc_ref)

        @pl.when(pl.program_id(1) == nsteps - 1)
        def _():
            o_ref[...] = acc_ref[...].astype(BF16).reshape(N_CHIP, kb, tn)
    return pl.pallas_call(
        body, name="mm_tn_row", grid=(nb // tn, nsteps),
        in_specs=[pl.BlockSpec((tm, N_CHIP * kb), lambda n, i: (i, 0)), pl.BlockSpec((tm, tn), lambda n, i: (i, n))],
        out_specs=pl.BlockSpec((N_CHIP, kb, tn), lambda n, i: (0, 0, n)), out_shape=out_shape,
        scratch_shapes=[pltpu.VMEM((N_CHIP * kb, tn), F32)], compiler_params=_params(48))(a, d)


def _seg(i, T):
    return (i >= T // TMR).astype(jnp.int32)


def norm_mod_fwd(x, nw, mod, k, T):
    R, dm = x.shape

    def body(x_ref, nw_ref, mod_ref, h_ref):
        seg = _seg(pl.program_id(0), T)
        sh = mod_ref[seg, pl.ds(k, 1), :]
        sc = mod_ref[seg, pl.ds(k + 1, 1), :]
        xv = x_ref[...]
        r = lax.rsqrt(jnp.mean(xv * xv, axis=-1, keepdims=True) + EPS)
        h_ref[...] = ((xv * r * nw_ref[...]) * (1.0 + sc) + sh).astype(BF16)
    return pl.pallas_call(
        body, name="norm_mod_fwd", grid=(R // TMR,),
        in_specs=[pl.BlockSpec((TMR, dm), lambda i: (i, 0)),
                  pl.BlockSpec((1, dm), lambda i: (0, 0)),
                  pl.BlockSpec((2, 6, dm), lambda i: (0, 0, 0))],
        out_specs=pl.BlockSpec((TMR, dm), lambda i: (i, 0)),
        out_shape=jax.ShapeDtypeStruct((R, dm), BF16))(x, nw, mod)


def norm_mod_bwd(dh, x, nw, mod, dxr, k, T):
    R, dm = x.shape

    def body(dh_ref, x_ref, nw_ref, mod_ref, dxr_ref, dx_ref, dmod_ref, dnw_ref):
        i = pl.program_id(0)
        seg = _seg(i, T)

        @pl.when(i == 0)
        def _():
            dmod_ref[...] = jnp.zeros_like(dmod_ref)
            dnw_ref[...] = jnp.zeros_like(dnw_ref)
        sc = mod_ref[seg, pl.ds(k + 1, 1), :]
        nwv = nw_ref[...]
        xv = x_ref[...]
        r = lax.rsqrt(jnp.mean(xv * xv, axis=-1, keepdims=True) + EPS)
        xh = xv * r
        dhv = dh_ref[...]
        dmod_ref[seg, 0] += _colsum8(dhv)
        dmod_ref[seg, 1] += _colsum8(dhv * (xh * nwv))
        dn = dhv * (1.0 + sc)
        dnw_ref[...] += _colsum8(dn * xh)
        dxh = dn * nwv
        dx = r * (dxh - xh * jnp.mean(dxh * xh, axis=-1, keepdims=True))
        dx_ref[...] = dxr_ref[...] + dx
    tile = pl.BlockSpec((TMR, dm), lambda i: (i, 0))
    return pl.pallas_call(
        body, name="norm_mod_bwd", grid=(R // TMR,),
        in_specs=[tile, tile, pl.BlockSpec((1, dm), lambda i: (0, 0)),
                  pl.BlockSpec((2, 6, dm), lambda i: (0, 0, 0)), tile],
        out_specs=[tile, pl.BlockSpec((2, 2, 8, dm), lambda i: (0, 0, 0, 0)),
                   pl.BlockSpec((8, dm), lambda i: (0, 0))],
        out_shape=[jax.ShapeDtypeStruct((R, dm), F32), jax.ShapeDtypeStruct((2, 2, 8, dm), F32),
                   jax.ShapeDtypeStruct((8, dm), F32)])(dh, x, nw, mod, dxr)


def resid_norm_fwd(x, y, nw, mod_g, mod_n, kg, kn, T):
    R, dm = x.shape

    def body(x_ref, y_ref, nw_ref, mg_ref, mn_ref, xo_ref, h_ref):
        seg = _seg(pl.program_id(0), T)
        xv = x_ref[...] + mg_ref[seg, pl.ds(kg, 1), :] * y_ref[...]
        xo_ref[...] = xv
        r = lax.rsqrt(jnp.mean(xv * xv, axis=-1, keepdims=True) + EPS)
        h_ref[...] = ((xv * r * nw_ref[...]) * (1.0 + mn_ref[seg, pl.ds(kn + 1, 1), :])
                      + mn_ref[seg, pl.ds(kn, 1), :]).astype(BF16)
    tile = pl.BlockSpec((TMR, dm), lambda i: (i, 0))
    modspec = pl.BlockSpec((2, 6, dm), lambda i: (0, 0, 0))
    return pl.pallas_call(
        body, name="resid_norm_fwd", grid=(R // TMR,),
        in_specs=[tile, tile, pl.BlockSpec((1, dm), lambda i: (0, 0)), modspec, modspec],
        out_specs=[tile, tile],
        out_shape=[jax.ShapeDtypeStruct((R, dm), F32), jax.ShapeDtypeStruct((R, dm), BF16)])(x, y, nw, mod_g, mod_n)


def norm_resid_bwd(dh, x, nw, mod_n, dxr, y, mod_g, kn, kg, T):
    R, dm = x.shape

    def body(dh_ref, x_ref, nw_ref, mn_ref, dxr_ref, y_ref, mg_ref, dx_ref, dy_ref, dmod_ref, dnw_ref, dg_ref):
        i = pl.program_id(0)
        seg = _seg(i, T)

        @pl.when(i == 0)
        def _():
            dmod_ref[...] = jnp.zeros_like(dmod_ref)
            dnw_ref[...] = jnp.zeros_like(dnw_ref)
            dg_ref[...] = jnp.zeros_like(dg_ref)
        sc = mn_ref[seg, pl.ds(kn + 1, 1), :]
        nwv = nw_ref[...]
        xv = x_ref[...]
        r = lax.rsqrt(jnp.mean(xv * xv, axis=-1, keepdims=True) + EPS)
        xh = xv * r
        dhv = dh_ref[...]
        dmod_ref[seg, 0] += _colsum8(dhv)
        dmod_ref[seg, 1] += _colsum8(dhv * (xh * nwv))
        dn = dhv * (1.0 + sc)
        dnw_ref[...] += _colsum8(dn * xh)
        dxh = dn * nwv
        dx = dxr_ref[...] + r * (dxh - xh * jnp.mean(dxh * xh, axis=-1, keepdims=True))
        dx_ref[...] = dx
        dy_ref[...] = (mg_ref[seg, pl.ds(kg, 1), :] * dx).astype(BF16)
        dg_ref[seg] += _colsum8(dx * y_ref[...])
    tile = pl.BlockSpec((TMR, dm), lambda i: (i, 0))
    modspec = pl.BlockSpec((2, 6, dm), lambda i: (0, 0, 0))
    return pl.pallas_call(
        body, name="norm_resid_bwd", grid=(R // TMR,),
        in_specs=[tile, tile, pl.BlockSpec((1, dm), lambda i: (0, 0)), modspec, tile, tile, modspec],
        out_specs=[tile, tile, pl.BlockSpec((2, 2, 8, dm), lambda i: (0, 0, 0, 0)),
                   pl.BlockSpec((8, dm), lambda i: (0, 0)), pl.BlockSpec((2, 8, dm), lambda i: (0, 0, 0))],
        out_shape=[jax.ShapeDtypeStruct((R, dm), F32), jax.ShapeDtypeStruct((R, dm), BF16),
                   jax.ShapeDtypeStruct((2, 2, 8, dm), F32), jax.ShapeDtypeStruct((8, dm), F32),
                   jax.ShapeDtypeStruct((2, 8, dm), F32)])(dh, x, nw, mod_n, dxr, y, mod_g)


def resid_fwd(x, y, mod, k, T):
    R, dm = x.shape

    def body(x_ref, y_ref, mod_ref, o_ref):
        seg = _seg(pl.program_id(0), T)
        o_ref[...] = x_ref[...] + mod_ref[seg, pl.ds(k, 1), :] * y_ref[...]
    tile = pl.BlockSpec((TMR, dm), lambda i: (i, 0))
    return pl.pallas_call(
        body, name="resid_fwd", grid=(R // TMR,),
        in_specs=[tile, tile, pl.BlockSpec((2, 6, dm), lambda i: (0, 0, 0))],
        out_specs=tile, out_shape=jax.ShapeDtypeStruct((R, dm), F32))(x, y, mod)


def resid_bwd(dxn, y, mod, k, T):
    R, dm = dxn.shape

    def body(dx_ref, y_ref, mod_ref, dy_ref, dg_ref):
        i = pl.program_id(0)
        seg = _seg(i, T)

        @pl.when(i == 0)
        def _():
            dg_ref[...] = jnp.zeros_like(dg_ref)
        dxv = dx_ref[...]
        dy_ref[...] = (mod_ref[seg, pl.ds(k, 1), :] * dxv).astype(BF16)
        dg_ref[seg] += _colsum8(dxv * y_ref[...])
    tile = pl.BlockSpec((TMR, dm), lambda i: (i, 0))
    return pl.pallas_call(
        body, name="resid_bwd", grid=(R // TMR,),
        in_specs=[tile, tile, pl.BlockSpec((2, 6, dm), lambda i: (0, 0, 0))],
        out_specs=[tile, pl.BlockSpec((2, 8, dm), lambda i: (0, 0, 0))],
        out_shape=[jax.ShapeDtypeStruct((R, dm), BF16), jax.ShapeDtypeStruct((2, 8, dm), F32)])(dxn, y, mod)


def _halo_specs(width, R):
    nblk = R // HALO
    per = TMR // HALO
    return (pl.BlockSpec((HALO, width), lambda i: (jnp.maximum(i * per - 1, 0), 0)),
            pl.BlockSpec((TMR, width), lambda i: (i, 0)),
            pl.BlockSpec((HALO, width), lambda i: (jnp.minimum((i + 1) * per, nblk - 1), 0)))


def _ext(refs, c0, cw):
    pref, ref, nref = refs
    return jnp.concatenate([pref[:, c0:c0 + cw].astype(F32), ref[:, c0:c0 + cw].astype(F32),
                            nref[:, c0:c0 + cw].astype(F32)], axis=0)


def _ext_mask(i, T, R):
    pos = i * TMR - HALO + lax.broadcasted_iota(jnp.int32, (TMR + 2 * HALO, 1), 0)
    lat = i < T // TMR
    lo = jnp.where(lat, 0, T)
    hi = jnp.where(lat, T, R)
    return (pos >= lo) & (pos < hi)


def _at(ext, off):
    n = ext.shape[0]
    s = (-off) % n
    y = pltpu.roll(ext, s, 0) if s else ext
    return y[HALO:HALO + TMR]


def ffnconv_fwd(u, wc, T):
    R, w2 = u.shape
    cw = 256

    def body(up_ref, u_ref, un_ref, wc_ref, z_ref):
        i = pl.program_id(0)
        mask = _ext_mask(i, T, R)

        def conv(c0):
            e = jnp.where(mask, _ext((up_ref, u_ref, un_ref), c0, cw), 0.0)
            return (wc_ref[pl.ds(0, 1), c0:c0 + cw] * _at(e, -1) + wc_ref[pl.ds(1, 1), c0:c0 + cw] * _at(e, 0)
                    + wc_ref[pl.ds(2, 1), c0:c0 + cw] * _at(e, 1))
        for j in range(D_FF // cw):
            a = conv(j * cw)
            g = conv(D_FF + j * cw)
            z_ref[:, j * cw:(j + 1) * cw] = (g * _sigmoid(g) * a).astype(BF16)
    return pl.pallas_call(
        body, name="ffnconv_fwd", grid=(R // TMR,),
        in_specs=[*_halo_specs(w2, R), pl.BlockSpec((3, w2), lambda i: (0, 0))],
        out_specs=pl.BlockSpec((TMR, D_FF), lambda i: (i, 0)),
        out_shape=jax.ShapeDtypeStruct((R, D_FF), BF16), compiler_params=_params(48))(u, u, u, wc)


def ffnconv_bwd1(dz, u, wc, T):
    R, w2 = u.shape
    cw = 256

    def body(dz_ref, up_ref, u_ref, un_ref, wc_ref, duc_ref, dwc_ref):
        i = pl.program_id(0)
        mask = _ext_mask(i, T, R)

        @pl.when(i == 0)
        def _():
            dwc_ref[...] = jnp.zeros_like(dwc_ref)

        def taps(c0):
            e = jnp.where(mask, _ext((up_ref, u_ref, un_ref), c0, cw), 0.0)
            return [_at(e, -1), _at(e, 0), _at(e, 1)]

        def conv(t, c0):
            return (wc_ref[pl.ds(0, 1), c0:c0 + cw] * t[0] + wc_ref[pl.ds(1, 1), c0:c0 + cw] * t[1]
                    + wc_ref[pl.ds(2, 1), c0:c0 + cw] * t[2])
        for j in range(D_FF // cw):
            ca, cg = j * cw, D_FF + j * cw
            ta, tg = taps(ca), taps(cg)
            a, g = conv(ta, ca), conv(tg, cg)
            dzv = dz_ref[:, ca:ca + cw].astype(F32)
            sg = _sigmoid(g)
            da = dzv * (g * sg)
            dg = dzv * a * (sg * (1.0 + g * (1.0 - sg)))
            duc_ref[:, ca:ca + cw] = da.astype(BF16)
            duc_ref[:, cg:cg + cw] = dg.astype(BF16)
            for k in range(3):
                dwc_ref[k, :, ca:ca + cw] += _colsum8(da * ta[k])
                dwc_ref[k, :, cg:cg + cw] += _colsum8(dg * tg[k])
    return pl.pallas_call(
        body, name="ffnconv_bwd1", grid=(R // TMR,),
        in_specs=[pl.BlockSpec((TMR, D_FF), lambda i: (i, 0)), *_halo_specs(w2, R),
                  pl.BlockSpec((3, w2), lambda i: (0, 0))],
        out_specs=[pl.BlockSpec((TMR, w2), lambda i: (i, 0)), pl.BlockSpec((3, 8, w2), lambda i: (0, 0, 0))],
        out_shape=[jax.ShapeDtypeStruct((R, w2), BF16), jax.ShapeDtypeStruct((3, 8, w2), F32)],
        compiler_params=_params(48))(dz, u, u, u, wc)


def ffnconv_bwd2(duc, wc, T):
    R, w2 = duc.shape
    cw = 256

    def body(dp_ref, d_ref, dn_ref, wc_ref, du_ref):
        mask = _ext_mask(pl.program_id(0), T, R)
        for j in range(w2 // cw):
            c0 = j * cw
            e = jnp.where(mask, _ext((dp_ref, d_ref, dn_ref), c0, cw), 0.0)
            du_ref[:, c0:c0 + cw] = (wc_ref[pl.ds(0, 1), c0:c0 + cw] * _at(e, 1)
                                     + wc_ref[pl.ds(1, 1), c0:c0 + cw] * _at(e, 0)
                                     + wc_ref[pl.ds(2, 1), c0:c0 + cw] * _at(e, -1)).astype(BF16)
    return pl.pallas_call(
        body, name="ffnconv_bwd2", grid=(R // TMR,),
        in_specs=[*_halo_specs(w2, R), pl.BlockSpec((3, w2), lambda i: (0, 0))],
        out_specs=pl.BlockSpec((TMR, w2), lambda i: (i, 0)),
        out_shape=jax.ShapeDtypeStruct((R, w2), BF16), compiler_params=_params(48))(duc, duc, duc, wc)


_CW = 128


def _mixer_a(prefs, wa_ref, mask):
    cin = jnp.where(mask, _ext(prefs, A_W, A_W) * _ext(prefs, 2 * A_W, A_W), 0.0)
    ca = (wa_ref[pl.ds(0, 1), :] * _at(cin, -1) + wa_ref[pl.ds(1, 1), :] * _at(cin, 0)
          + wa_ref[pl.ds(2, 1), :] * _at(cin, 1))
    return cin, ca


def _mixer_b(prefs, wb_ref, bias_ref, mask, ub_s, ub2_s):
    for cc in range(A_W // _CW):
        c0 = cc * _CW
        ub = jnp.where(mask, _ext(prefs, 3 * A_W + c0, _CW) * _sigmoid(_ext(prefs, 4 * A_W + c0, _CW)), 0.0)
        ub_s[:, c0:c0 + _CW] = ub
        acc = jnp.zeros((TMR, _CW), F32) + bias_ref[:, c0:c0 + _CW]
        for k in range(B_CONV):
            acc = acc + wb_ref[pl.ds(k, 1), c0:c0 + _CW] * _at(ub, k - B_CONV // 2)
        ub2_s[:, c0:c0 + _CW] = acc


def _layernorm_stats(v):
    mu = jnp.mean(v, axis=-1, keepdims=True)
    xc = v - mu
    rs = lax.rsqrt(jnp.mean(xc * xc, axis=-1, keepdims=True) + EPS)
    return xc * rs, rs


def convmix_fwd(p, wa, wb, bias, lng, lnb, T):
    R, wp = p.shape

    def body(pp_ref, p_ref, pn_ref, wa_ref, wb_ref, bias_ref, lng_ref, lnb_ref, o_ref, ub_s, ub2_s):
        mask = _ext_mask(pl.program_id(0), T, R)
        prefs = (pp_ref, p_ref, pn_ref)
        _, ca = _mixer_a(prefs, wa_ref, mask)
        o_ref[:, 0:A_W] = (p_ref[:, 0:A_W].astype(F32) * ca).astype(BF16)
        _mixer_b(prefs, wb_ref, bias_ref, mask, ub_s, ub2_s)
        xh, _ = _layernorm_stats(ub2_s[...])
        lv = xh * lng_ref[...] + lnb_ref[...]
        o_ref[:, A_W:2 * A_W] = (lv * _sigmoid(lv)).astype(BF16)
    vec = pl.BlockSpec((1, A_W), lambda i: (0, 0))
    return pl.pallas_call(
        body, name="convmix_fwd", grid=(R // TMR,),
        in_specs=[*_halo_specs(wp, R), pl.BlockSpec((3, A_W), lambda i: (0, 0)),
                  pl.BlockSpec((B_CONV, A_W), lambda i: (0, 0)), vec, vec, vec],
        out_specs=pl.BlockSpec((TMR, 2 * A_W), lambda i: (i, 0)),
        out_shape=jax.ShapeDtypeStruct((R, 2 * A_W), BF16),
        scratch_shapes=[pltpu.VMEM((TMR + 2 * HALO, A_W), F32), pltpu.VMEM((TMR, A_W), F32)],
        compiler_params=_params(48))(p, p, p, wa, wb, bias, lng, lnb)


def convmix_bwd1(dyab, p, wa, wb, bias, lng, lnb, T):
    R, wp = p.shape

    def body(dy_ref, pp_ref, p_ref, pn_ref, wa_ref, wb_ref, bias_ref, lng_ref, lnb_ref,
             dmid_ref, dwa_ref, dwb_ref, dvec_ref, ub_s, ub2_s):
        i = pl.program_id(0)
        mask = _ext_mask(i, T, R)

        @pl.when(i == 0)
        def _():
            dwa_ref[...] = jnp.zeros_like(dwa_ref)
            dwb_ref[...] = jnp.zeros_like(dwb_ref)
            dvec_ref[...] = jnp.zeros_like(dvec_ref)
        prefs = (pp_ref, p_ref, pn_ref)
        cin, ca = _mixer_a(prefs, wa_ref, mask)
        dya = dy_ref[:, 0:A_W]
        dmid_ref[:, 0:A_W] = dya * ca
        dca = dya * p_ref[:, 0:A_W].astype(F32)
        dmid_ref[:, A_W:2 * A_W] = dca
        for k in range(3):
            dwa_ref[k] += _colsum8(dca * _at(cin, k - 1))
        _mixer_b(prefs, wb_ref, bias_ref, mask, ub_s, ub2_s)
        xh, rs = _layernorm_stats(ub2_s[...])
        gain = lng_ref[...]
        lv = xh * gain + lnb_ref[...]
        sl = _sigmoid(lv)
        dl = dy_ref[:, A_W:2 * A_W] * (sl * (1.0 + lv * (1.0 - sl)))
        dvec_ref[1] += _colsum8(dl * xh)
        dvec_ref[2] += _colsum8(dl)
        dxh = dl * gain
        dub2 = rs * (dxh - jnp.mean(dxh, axis=-1, keepdims=True)
                     - xh * jnp.mean(dxh * xh, axis=-1, keepdims=True))
        dvec_ref[0] += _colsum8(dub2)
        dmid_ref[:, 2 * A_W:3 * A_W] = dub2
        for cc in range(A_W // _CW):
            c0 = cc * _CW
            ub = ub_s[:, c0:c0 + _CW]
            d = dmid_ref[:, 2 * A_W + c0:2 * A_W + c0 + _CW]
            for k in range(B_CONV):
                dwb_ref[k, :, c0:c0 + _CW] += _colsum8(d * _at(ub, k - B_CONV // 2))
    vec = pl.BlockSpec((1, A_W), lambda i: (0, 0))
    return pl.pallas_call(
        body, name="convmix_bwd1", grid=(R // TMR,),
        in_specs=[pl.BlockSpec((TMR, 2 * A_W), lambda i: (i, 0)), *_halo_specs(wp, R),
                  pl.BlockSpec((3, A_W), lambda i: (0, 0)), pl.BlockSpec((B_CONV, A_W), lambda i: (0, 0)),
                  vec, vec, vec],
        out_specs=[pl.BlockSpec((TMR, 3 * A_W), lambda i: (i, 0)),
                   pl.BlockSpec((3, 8, A_W), lambda i: (0, 0, 0)),
                   pl.BlockSpec((B_CONV, 8, A_W), lambda i: (0, 0, 0)),
                   pl.BlockSpec((3, 8, A_W), lambda i: (0, 0, 0))],
        out_shape=[jax.ShapeDtypeStruct((R, 3 * A_W), F32), jax.ShapeDtypeStruct((3, 8, A_W), F32),
                   jax.ShapeDtypeStruct((B_CONV, 8, A_W), F32), jax.ShapeDtypeStruct((3, 8, A_W), F32)],
        scratch_shapes=[pltpu.VMEM((TMR + 2 * HALO, A_W), F32), pltpu.VMEM((TMR, A_W), F32)],
        compiler_params=_params(48))(dyab, p, p, p, wa, wb, bias, lng, lnb)


def convmix_bwd2(dmid, p, wa, wb, T):
    R, wp = p.shape

    def body(mp_ref, m_ref, mn_ref, p_ref, wa_ref, wb_ref, dp_ref):
        mask = _ext_mask(pl.program_id(0), T, R)
        mrefs = (mp_ref, m_ref, mn_ref)
        dp_ref[:, 0:A_W] = m_ref[:, 0:A_W].astype(BF16)
        dca = jnp.where(mask, _ext(mrefs, A_W, A_W), 0.0)
        dcin = (wa_ref[pl.ds(0, 1), :] * _at(dca, 1) + wa_ref[pl.ds(1, 1), :] * _at(dca, 0)
                + wa_ref[pl.ds(2, 1), :] * _at(dca, -1))
        dp_ref[:, A_W:2 * A_W] = (dcin * p_ref[:, 2 * A_W:3 * A_W].astype(F32)).astype(BF16)
        dp_ref[:, 2 * A_W:3 * A_W] = (dcin * p_ref[:, A_W:2 * A_W].astype(F32)).astype(BF16)
        for cc in range(A_W // _CW):
            c0 = cc * _CW
            d = jnp.where(mask, _ext(mrefs, 2 * A_W + c0, _CW), 0.0)
            dub = jnp.zeros((TMR, _CW), F32)
            for k in range(B_CONV):
                dub = dub + wb_ref[pl.ds(k, 1), c0:c0 + _CW] * _at(d, B_CONV // 2 - k)
            vb = p_ref[:, 3 * A_W + c0:3 * A_W + c0 + _CW].astype(F32)
            s = _sigmoid(p_ref[:, 4 * A_W + c0:4 * A_W + c0 + _CW].astype(F32))
            dp_ref[:, 3 * A_W + c0:3 * A_W + c0 + _CW] = (dub * s).astype(BF16)
            dp_ref[:, 4 * A_W + c0:4 * A_W + c0 + _CW] = (dub * vb * s * (1.0 - s)).astype(BF16)
    return pl.pallas_call(
        body, name="convmix_bwd2", grid=(R // TMR,),
        in_specs=[*_halo_specs(3 * A_W, R), pl.BlockSpec((TMR, wp), lambda i: (i, 0)),
                  pl.BlockSpec((3, A_W), lambda i: (0, 0)), pl.BlockSpec((B_CONV, A_W), lambda i: (0, 0))],
        out_specs=pl.BlockSpec((TMR, wp), lambda i: (i, 0)),
        out_shape=jax.ShapeDtypeStruct((R, wp), BF16), compiler_params=_params(48))(dmid, dmid, dmid, p, wa, wb)


def _rot_half(v):
    w = v.shape[-1]
    lane = lax.broadcasted_iota(jnp.int32, (1, w), 1)
    return jnp.where(lane % HEAD_DIM < HEAD_DIM // 2, pltpu.roll(v, w - HEAD_DIM // 2, 1),
                     pltpu.roll(v, HEAD_DIM // 2, 1))


def rope_fwd(qkv, cs, sn):
    R, wq = qkv.shape
    qw = N_HEADS * HEAD_DIM
    kw = (wq - qw) // 2
    scale = HEAD_DIM ** -0.5

    def body(x_ref, cs_ref, sn_ref, o_ref):
        c, s = cs_ref[...], sn_ref[...]
        q = x_ref[:, 0:qw]
        o_ref[:, 0:qw] = ((q * jnp.tile(c, (1, qw // 128)) + _rot_half(q) * jnp.tile(s, (1, qw // 128)))
                          * scale).astype(BF16)
        k = x_ref[:, qw:qw + kw]
        o_ref[:, qw:qw + kw] = (k * jnp.tile(c, (1, kw // 128))
                                + _rot_half(k) * jnp.tile(s, (1, kw // 128))).astype(BF16)
        o_ref[:, qw + kw:] = x_ref[:, qw + kw:].astype(BF16)
    tab = pl.BlockSpec((TMR, 128), lambda i: (i, 0))
    return pl.pallas_call(
        body, name="rope_fwd", grid=(R // TMR,),
        in_specs=[pl.BlockSpec((TMR, wq), lambda i: (i, 0)), tab, tab],
        out_specs=pl.BlockSpec((TMR, wq), lambda i: (i, 0)),
        out_shape=jax.ShapeDtypeStruct((R, wq), BF16))(qkv, cs, sn)


def rope_bwd(dq, dks, dvs, dkc, dvc, cs, sn, T):
    R, qw = dq.shape
    kw = dkc.shape[1]
    nb = R // QB
    nl = T // QB
    scale = HEAD_DIM ** -0.5

    def body(dq_ref, kp_ref, ko_ref, kn_ref, vp_ref, vo_ref, vn_ref, kc_ref, vc_ref, cs_ref, sn_ref, o_ref):
        b = pl.program_id(0)
        c, s = cs_ref[...], sn_ref[...]
        has_next = (b + 1 < nb).astype(F32)
        has_prev = (b >= 1).astype(F32)
        is_ctx = (b >= nl).astype(F32)
        g = dq_ref[...] * scale
        o_ref[:, 0:qw] = (g * jnp.tile(c, (1, qw // 128)) + _rot_half(g * jnp.tile(s, (1, qw // 128)))).astype(BF16)
        g = ko_ref[...] + kp_ref[...] * has_next + kn_ref[...] * has_prev + kc_ref[...] * is_ctx
        o_ref[:, qw:qw + kw] = (g * jnp.tile(c, (1, kw // 128))
                                + _rot_half(g * jnp.tile(s, (1, kw // 128)))).astype(BF16)
        o_ref[:, qw + kw:] = (vo_ref[...] + vp_ref[...] * has_next + vn_ref[...] * has_prev
                              + vc_ref[...] * is_ctx).astype(BF16)
    own = pl.BlockSpec((QB, kw), lambda b: (b, 0))
    from_next = pl.BlockSpec((QB, kw), lambda b: (jnp.minimum(b + 1, nb - 1), 0))
    from_prev = pl.BlockSpec((QB, kw), lambda b: (jnp.maximum(b - 1, 0), 0))
    ctx = pl.BlockSpec((QB, kw), lambda b: (jnp.maximum(b - nl, 0), 0))
    tab = pl.BlockSpec((QB, 128), lambda b: (b, 0))
    return pl.pallas_call(
        body, name="rope_bwd", grid=(nb,),
        in_specs=[pl.BlockSpec((QB, qw), lambda b: (b, 0)), from_next, own, from_prev, from_next, own, from_prev,
                  ctx, ctx, tab, tab],
        out_specs=pl.BlockSpec((QB, qw + 2 * kw), lambda b: (b, 0)),
        out_shape=jax.ShapeDtypeStruct((R, qw + 2 * kw), BF16))(
            dq, dks[0], dks[1], dks[2], dvs[0], dvs[1], dvs[2], dkc, dvc, cs, sn)


def _attn_specs(T, R):
    nl = T // QB
    qcols = N_HEADS * HEAD_DIM // 128
    kcols = 2

    def band(col0, shift):
        return pl.BlockSpec((QB, 128), lambda jj, b: (jnp.clip(b + shift, 0, nl - 1), col0 + jj))

    def ctx(col0):
        return pl.BlockSpec((R - T, 128), lambda jj, b: (T // (R - T), col0 + jj))
    q = pl.BlockSpec((QB, 512), lambda jj, b: (b, jj))
    k0, v0 = qcols, qcols + kcols
    return q, [band(k0, -1), band(k0, 0), band(k0, 1), ctx(k0)], [band(v0, -1), band(v0, 0), band(v0, 1), ctx(v0)]


def _attn_common(T, R):
    nl = T // QB
    nk = 3 * QB + (R - T)

    def low_lanes():
        return lax.broadcasted_iota(jnp.int32, (1, 128), 1) < HEAD_DIM

    def dup(v, par):
        low = low_lanes()
        vf = v.astype(F32)
        r = pltpu.roll(vf, HEAD_DIM, 1)
        return (jnp.where(low, vf, r) if par == 0 else jnp.where(low, r, vf)).astype(BF16)

    def stack(ref, par):
        low = low_lanes()
        pa = ref[:, (2 * par) * 128:(2 * par + 1) * 128].astype(BF16)
        pb = ref[:, (2 * par + 1) * 128:(2 * par + 2) * 128].astype(BF16)
        zero = jnp.zeros_like(pa)
        return jnp.concatenate([jnp.where(low, pa, zero), jnp.where(low, zero, pa),
                                jnp.where(low, pb, zero), jnp.where(low, zero, pb)], axis=0)

    def unstack(v):
        low = low_lanes()
        return (jnp.where(low, v[0:QB], v[QB:2 * QB]), jnp.where(low, v[2 * QB:3 * QB], v[3 * QB:4 * QB]))

    def mask_of(b):
        col = lax.broadcasted_iota(jnp.int32, (4 * QB, nk), 1)
        row = lax.broadcasted_iota(jnp.int32, (4 * QB, nk), 0) & (QB - 1)
        qpos = b * QB + row
        kpos = (b - 1) * QB + col
        in_band = (jnp.abs(qpos - kpos) <= WINDOW) & (kpos >= 0) & (kpos < T) & (b < nl)
        return in_band | (col >= 3 * QB)

    def sink_col(sink_ref, first):
        blk = lax.broadcasted_iota(jnp.int32, (4 * QB, 1), 0) // QB
        out = jnp.zeros((4 * QB, 1), F32) + sink_ref[first]
        for h in range(1, 4):
            out = jnp.where(blk == h, sink_ref[first + h], out)
        return out

    def scores(qs, kd, mask, sink):
        s = lax.dot_general(qs, kd, (((1,), (1,)), ((), ())), preferred_element_type=F32)
        s = jnp.where(mask, s, NEG_INF)
        m = jnp.maximum(jnp.max(s, axis=-1, keepdims=True), sink)
        e = jnp.exp(s - m)
        es = jnp.exp(sink - m)
        return e, es, 1.0 / (jnp.sum(e, axis=-1, keepdims=True) + es)
    return low_lanes, dup, stack, unstack, mask_of, sink_col, scores


def attn_fwd(qkvr, sinks, T):
    R = qkvr.shape[0]
    qspec, kspecs, vspecs = _attn_specs(T, R)
    _, dup, stack, unstack, mask_of, sink_col, scores = _attn_common(T, R)

    def body(q_ref, kp, ko, kn, kc, vp, vo, vn, vc, sink_ref, o_ref):
        jj, b = pl.program_id(0), pl.program_id(1)
        mask = mask_of(b)
        k_all = jnp.concatenate([kp[...], ko[...], kn[...], kc[...]], axis=0)
        v_all = jnp.concatenate([vp[...], vo[...], vn[...], vc[...]], axis=0)
        for par in range(2):
            kd, vd = dup(k_all, par), dup(v_all, par)
            e, _, rz = scores(stack(q_ref, par), kd, mask, sink_col(sink_ref, jj * 8 + par * 4))
            o = jnp.dot((e * rz).astype(BF16), vd, preferred_element_type=F32)
            pa, pb = unstack(o)
            o_ref[:, (2 * par) * 128:(2 * par + 1) * 128] = pa.astype(BF16)
            o_ref[:, (2 * par + 1) * 128:(2 * par + 2) * 128] = pb.astype(BF16)
    return pl.pallas_call(
        body, name="attn_fwd", grid=(2, R // QB),
        in_specs=[qspec, *kspecs, *vspecs, pl.BlockSpec(memory_space=pltpu.SMEM)],
        out_specs=pl.BlockSpec((QB, 512), lambda jj, b: (b, jj)),
        out_shape=jax.ShapeDtypeStruct((R, N_HEADS * HEAD_DIM), BF16), compiler_params=_params(48))(
            qkvr, *([qkvr] * 8), sinks)


def attn_bwd(qkvr, do, sinks, T):
    R = qkvr.shape[0]
    tc = R - T
    qspec, kspecs, vspecs = _attn_specs(T, R)
    low_lanes, dup, stack, unstack, mask_of, sink_col, scores = _attn_common(T, R)
    contract_rows = (((0,), (0,)), ((), ()))
    contract_last = (((1,), (1,)), ((), ()))

    def body(q_ref, kp, ko, kn, kc, vp, vo, vn, vc, do_ref, sink_ref,
             dq_ref, dkp, dko, dkn, dvp, dvo, dvn, dkc_ref, dvc_ref, dsink_ref):
        jj, b = pl.program_id(0), pl.program_id(1)

        @pl.when((jj == 0) & (b == 0))
        def _():
            dsink_ref[...] = jnp.zeros_like(dsink_ref)

        @pl.when(b == 0)
        def _():
            dkc_ref[...] = jnp.zeros_like(dkc_ref)
            dvc_ref[...] = jnp.zeros_like(dvc_ref)
        mask = mask_of(b)
        k_all = jnp.concatenate([kp[...], ko[...], kn[...], kc[...]], axis=0)
        v_all = jnp.concatenate([vp[...], vo[...], vn[...], vc[...]], axis=0)
        lane = lax.broadcasted_iota(jnp.int32, (8, 128), 1)
        srow = lax.broadcasted_iota(jnp.int32, (8, 128), 0)
        dk_fold, dv_fold = [], []
        for par in range(2):
            kd, vd = dup(k_all, par), dup(v_all, par)
            first = jj * 8 + par * 4
            qs, dos = stack(q_ref, par), stack(do_ref, par)
            e, es, rz = scores(qs, kd, mask, sink_col(sink_ref, first))
            p = e * rz
            dp = lax.dot_general(dos, vd, contract_last, preferred_element_type=F32)
            delta = jnp.sum(p * dp, axis=-1, keepdims=True)
            ds = (p * (dp - delta)).astype(BF16)
            t = es * rz * delta
            for h in range(4):
                dsink = -jnp.sum(t[h * QB:(h + 1) * QB])
                dsink_ref[...] += jnp.where((lane == first + h) & (srow == 0), dsink, 0.0)
            pa, pb = unstack(jnp.dot(ds, kd, preferred_element_type=F32))
            dq_ref[:, (2 * par) * 128:(2 * par + 1) * 128] = pa
            dq_ref[:, (2 * par + 1) * 128:(2 * par + 2) * 128] = pb
            dk_t = lax.dot_general(qs, ds, contract_rows, preferred_element_type=F32)
            dv_t = lax.dot_general(dos, p.astype(BF16), contract_rows, preferred_element_type=F32)
            dk_fold.append(dk_t + pltpu.roll(dk_t, HEAD_DIM, 0))
            dv_fold.append(dv_t + pltpu.roll(dv_t, HEAD_DIM, 0))
        low_rows = lax.broadcasted_iota(jnp.int32, (128, 1), 0) < HEAD_DIM
        dk = jnp.where(low_rows, dk_fold[0], dk_fold[1]).T
        dv = jnp.where(low_rows, dv_fold[0], dv_fold[1]).T
        dkp[...], dko[...], dkn[...] = dk[0:QB], dk[QB:2 * QB], dk[2 * QB:3 * QB]
        dvp[...], dvo[...], dvn[...] = dv[0:QB], dv[QB:2 * QB], dv[2 * QB:3 * QB]
        dkc_ref[...] += dk[3 * QB:]
        dvc_ref[...] += dv[3 * QB:]
    blk = pl.BlockSpec((QB, 128), lambda jj, b: (b, jj))
    cblk = pl.BlockSpec((tc, 128), lambda jj, b: (0, jj))
    part = jax.ShapeDtypeStruct((R, 256), F32)
    csum = jax.ShapeDtypeStruct((tc, 256), F32)
    outs = pl.pallas_call(
        body, name="attn_bwd", grid=(2, R // QB),
        in_specs=[qspec, *kspecs, *vspecs, pl.BlockSpec((QB, 512), lambda jj, b: (b, jj)),
                  pl.BlockSpec(memory_space=pltpu.SMEM)],
        out_specs=[pl.BlockSpec((QB, 512), lambda jj, b: (b, jj)), blk, blk, blk, blk, blk, blk, cblk, cblk,
                   pl.BlockSpec((8, 128), lambda jj, b: (0, 0))],
        out_shape=[jax.ShapeDtypeStruct((R, N_HEADS * HEAD_DIM), F32), part, part, part, part, part, part,
                   csum, csum, jax.ShapeDtypeStruct((8, 128), F32)],
        compiler_params=_params(48))(qkvr, *([qkvr] * 8), do, sinks)
    return outs[0], outs[1:4], outs[4:7], outs[7], outs[8], outs[9]


def loss_head(x, nw, target, T):
    R, dm = x.shape
    nl = T // TMR

    def body(x_ref, nw_ref, t_ref, loss_ref, dx_ref, dnw_ref):
        i = pl.program_id(0)

        @pl.when(i == 0)
        def _():
            loss_ref[...] = jnp.zeros_like(loss_ref)
            dnw_ref[...] = jnp.zeros_like(dnw_ref)
        live = (i < nl).astype(F32)
        nwv = nw_ref[...]
        xv = x_ref[...]
        r = lax.rsqrt(jnp.mean(xv * xv, axis=-1, keepdims=True) + EPS)
        xh = xv * r
        err = xh * nwv - t_ref[...]
        per_row = jnp.mean(err * err, axis=-1, keepdims=True)
        loss_ref[...] += 0.5 * live * jnp.sum(per_row)
        dy = err * (live / dm)
        dnw_ref[...] += _colsum8(dy * xh)
        dxh = dy * nwv
        dx_ref[...] = r * (dxh - xh * jnp.mean(dxh * xh, axis=-1, keepdims=True))
    tile = pl.BlockSpec((TMR, dm), lambda i: (i, 0))
    return pl.pallas_call(
        body, name="loss_head", grid=(R // TMR,),
        in_specs=[tile, pl.BlockSpec((1, dm), lambda i: (0, 0)),
                  pl.BlockSpec((TMR, dm), lambda i: (jnp.minimum(i, nl - 1), 0))],
        out_specs=[pl.BlockSpec((8, 128), lambda i: (0, 0)), tile, pl.BlockSpec((8, dm), lambda i: (0, 0))],
        out_shape=[jax.ShapeDtypeStruct((8, 128), F32), jax.ShapeDtypeStruct((R, dm), F32),
                   jax.ShapeDtypeStruct((8, dm), F32)])(x, nw, target)


def adaln_fwd(cond, w_mod, b_mod):
    nl, dm, ns = w_mod.shape

    def body(c_ref, w_ref, b_ref, o_ref):
        cv = c_ref[...]
        s = (cv * _sigmoid(cv)).astype(BF16)
        o_ref[...] = jnp.dot(s, w_ref[...].astype(BF16), preferred_element_type=F32) + b_ref[...]
    return pl.pallas_call(
        body, name="adaln_fwd", grid=(nl,),
        in_specs=[pl.BlockSpec((16, dm), lambda l: (0, 0)), pl.BlockSpec((None, dm, ns), lambda l: (l, 0, 0)),
                  pl.BlockSpec((None, 1, ns), lambda l: (l, 0, 0))],
        out_specs=pl.BlockSpec((None, 16, ns), lambda l: (l, 0, 0)),
        out_shape=jax.ShapeDtypeStruct((nl, 16, ns), F32), compiler_params=_params(48))(cond, w_mod, b_mod)


def adaln_bwd(cond, dmod, w_mod):
    nl, dm, ns = w_mod.shape

    def body(c_ref, d_ref, w_ref, gw_ref, ds_ref):
        l = pl.program_id(0)

        @pl.when(l == 0)
        def _():
            ds_ref[...] = jnp.zeros_like(ds_ref)
        cv = c_ref[...]
        s = (cv * _sigmoid(cv)).astype(BF16)
        dv = d_ref[...].astype(BF16)
        gw_ref[...] = lax.dot_general(s, dv, (((0,), (0,)), ((), ())), preferred_element_type=F32)
        ds_ref[...] += lax.dot_general(dv, w_ref[...].astype(BF16), (((1,), (1,)), ((), ())),
                                       preferred_element_type=F32)
    return pl.pallas_call(
        body, name="adaln_bwd", grid=(nl,),
        in_specs=[pl.BlockSpec((16, dm), lambda l: (0, 0)), pl.BlockSpec((None, 16, ns), lambda l: (l, 0, 0)),
                  pl.BlockSpec((None, dm, ns), lambda l: (l, 0, 0))],
        out_specs=[pl.BlockSpec((None, dm, ns), lambda l: (l, 0, 0)), pl.BlockSpec((16, dm), lambda l: (0, 0))],
        out_shape=[jax.ShapeDtypeStruct((nl, dm, ns), F32), jax.ShapeDtypeStruct((16, dm), F32)],
        compiler_params=_params(48))(cond, dmod, w_mod)


def _me():
    return lax.axis_index("x"), lax.axis_index("y"), lax.axis_index("c")


def allgather8(block):
    m_per, n = block.shape

    def body(x_ref, out_ref, send_sems, recv_sems, local_sem):
        x, y, c = _me()
        me, sibling = (x, y, c), (x, y, 1 - c)
        chips = [(1 - x, y), (x, 1 - y), (1 - x, 1 - y)]

        def rows(px, py, pc):
            return out_ref.at[pl.ds((4 * px + 2 * py + pc) * m_per, m_per), :]

        def copy(k, blk, to, src=None):
            return pltpu.make_async_remote_copy(
                src_ref=rows(*blk) if src is None else src, dst_ref=rows(*blk),
                send_sem=send_sems.at[k], recv_sem=recv_sems.at[k], device_id=to, device_id_type=MESH)
        mine = pltpu.make_async_copy(x_ref, rows(*me), local_sem)
        mine.start()
        first = [copy(0, me, sibling, src=x_ref)]
        first += [copy(1 + j, me, (*chip, c), src=x_ref) for j, chip in enumerate(chips)]
        for cp in first:
            cp.start()
        passed = [copy(4 + j, (*chip, c), sibling) for j, chip in enumerate(chips)]
        for j, chip in enumerate(chips):
            copy(1 + j, (*chip, c), me).wait_recv()
            passed[j].start()
        copy(0, sibling, me).wait_recv()
        for j, chip in enumerate(chips):
            copy(4 + j, (*chip, 1 - c), me).wait_recv()
        for cp in first + passed:
            cp.wait_send()
        mine.wait()
    return pl.pallas_call(
        body, name="allgather8",
        out_shape=jax.ShapeDtypeStruct((N_DEV * m_per, n), block.dtype),
        in_specs=[pl.BlockSpec(memory_space=pltpu.VMEM)],
        out_specs=pl.BlockSpec(memory_space=pltpu.VMEM),
        scratch_shapes=[pltpu.SemaphoreType.DMA((7,)), pltpu.SemaphoreType.DMA((7,)), pltpu.SemaphoreType.DMA],
        compiler_params=_params(48))(block)


def _other_chips(x, y):
    return [(1 - x, y), (x, 1 - y), (1 - x, 1 - y)]


_HBM = pl.BlockSpec(memory_space=pltpu.HBM)
_SEM = pl.BlockSpec(memory_space=pltpu.SEMAPHORE)
_ANY = pl.BlockSpec(memory_space=pl.ANY)
_EFFECT = pltpu.SideEffectType.DATAFLOW_SIDE_EFFECTING


def _in_hbm(v):
    return pltpu.with_memory_space_constraint(v, pltpu.HBM)


def own_into_gather(shards):
    n = len(shards)

    def body(*refs):
        srcs, outs, sems = refs[:n], refs[n:2 * n], refs[2 * n]
        x, y, _ = _me()
        cps = [pltpu.make_async_copy(srcs[t], outs[t].at[2 * x + y], sems.at[t]) for t in range(n)]
        for cp in cps:
            cp.start()
        for cp in cps:
            cp.wait()
    return pl.pallas_call(
        body, name="own_into_gather",
        out_shape=[jax.ShapeDtypeStruct((N_CHIP, *v.shape), v.dtype) for v in shards],
        in_specs=[_ANY] * n, out_specs=[_ANY] * n,
        scratch_shapes=[pltpu.SemaphoreType.DMA((n,))])(*shards)


def own_into_scatter(grads):
    n = len(grads)

    def body(*refs):
        srcs, outs, sems = refs[:n], refs[n:2 * n], refs[2 * n]
        x, y, _ = _me()
        cps = [pltpu.make_async_copy(srcs[t].at[2 * x + y], outs[t].at[3], sems.at[t]) for t in range(n)]
        for cp in cps:
            cp.start()
        for cp in cps:
            cp.wait()
    return pl.pallas_call(
        body, name="own_into_scatter",
        out_shape=[jax.ShapeDtypeStruct(v.shape, v.dtype) for v in grads],
        in_specs=[_ANY] * n, out_specs=[_ANY] * n,
        scratch_shapes=[pltpu.SemaphoreType.DMA((n,))])(*grads)


def _split_copies(mode, srcs, lands, send_sems, recv_sems):
    x, y, c = _me()
    out = []
    for t in range(len(srcs)):
        for k, chip in enumerate(_other_chips(x, y)):
            if mode == "gather":
                src, dst, landed = srcs[t], lands[t].at[2 * x + y], lands[t].at[2 * chip[0] + chip[1]]
            else:
                src, dst, landed = srcs[t].at[2 * chip[0] + chip[1]], lands[t].at[k], lands[t].at[k]
            send = pltpu.make_async_remote_copy(src_ref=src, dst_ref=dst, send_sem=send_sems.at[3 * t + k],
                                                recv_sem=recv_sems.at[3 * t + k], device_id=(*chip, c),
                                                device_id_type=MESH)
            recv = pltpu.make_async_remote_copy(src_ref=src, dst_ref=landed, send_sem=send_sems.at[3 * t + k],
                                                recv_sem=recv_sems.at[3 * t + k], device_id=(*chip, c),
                                                device_id_type=MESH)
            out.append((send, recv))
    return out


def exchange_start(name, mode, srcs, lands, after):
    n = len(srcs)

    def body(*refs):
        src_refs, land_refs = refs[:n], refs[n:2 * n]
        send_sems, recv_sems = refs[2 * n + 1], refs[2 * n + 2]
        token = refs[-1]
        for send, _ in _split_copies(mode, src_refs, land_refs, send_sems, recv_sems):
            send.start()
        token[...] = jnp.zeros_like(token)
    outs = pl.pallas_call(
        body, name=name,
        out_shape=(pltpu.SemaphoreType.DMA((3 * n,)), pltpu.SemaphoreType.DMA((3 * n,)),
                   *[pltpu.HBM(v.shape, v.dtype) for v in srcs], *[pltpu.HBM(v.shape, v.dtype) for v in lands],
                   jax.ShapeDtypeStruct((8, 128), F32)),
        in_specs=[_HBM] * (2 * n) + [_ANY],
        out_specs=(_SEM, _SEM, *[_HBM] * (2 * n), pl.BlockSpec(memory_space=pltpu.VMEM)),
        input_output_aliases={i: 2 + i for i in range(2 * n)},
        compiler_params=pltpu.CompilerParams(has_side_effects=_EFFECT))(
            *[_in_hbm(v) for v in srcs], *[_in_hbm(v) for v in lands], after)
    return outs[0], outs[1], list(outs[2:2 + n]), list(outs[2 + n:2 + 2 * n]), outs[-1]


def exchange_wait(name, mode, send_sems, recv_sems, srcs, lands, after):
    n = len(srcs)

    def body(*refs):
        src_refs, land_refs = refs[:n], refs[n:2 * n]
        for _, recv in _split_copies(mode, src_refs, land_refs, refs[2 * n], refs[2 * n + 1]):
            recv.wait_send()
            recv.wait_recv()
    outs = pl.pallas_call(
        body, name=name,
        out_shape=[pltpu.HBM(v.shape, v.dtype) for v in srcs] + [pltpu.HBM(v.shape, v.dtype) for v in lands],
        in_specs=[_HBM] * (2 * n) + [_SEM, _SEM, _ANY], out_specs=[_HBM] * (2 * n),
        input_output_aliases={i: i for i in range(2 * n)},
        compiler_params=pltpu.CompilerParams(has_side_effects=_EFFECT))(
            *srcs, *lands, send_sems, recv_sems, after)
    return list(outs[n:])


def swap_with_sibling(v):
    def body(v_ref, out_ref, send_sem, recv_sem):
        x, y, c = _me()
        cp = pltpu.make_async_remote_copy(src_ref=v_ref, dst_ref=out_ref, send_sem=send_sem, recv_sem=recv_sem,
                                          device_id=(x, y, 1 - c), device_id_type=MESH)
        cp.start()
        cp.wait()
    return pl.pallas_call(
        body, name="swap_with_sibling", out_shape=jax.ShapeDtypeStruct(v.shape, v.dtype),
        in_specs=[pl.BlockSpec(memory_space=pl.ANY)], out_specs=pl.BlockSpec(memory_space=pl.ANY),
        scratch_shapes=[pltpu.SemaphoreType.DMA, pltpu.SemaphoreType.DMA])(v)


def sum_slots(parts, layer=0, n_layers=1, buf=None):
    n, rows, w = parts.shape
    tr = _row_tile(rows)
    base = layer * (rows // tr)

    def compute(p_ref, o_ref):
        acc = p_ref[0].astype(F32)
        for k in range(1, n):
            acc = acc + p_ref[k].astype(F32)
        o_ref[...] = acc
    in_specs = [pl.BlockSpec((n, tr, w), lambda i: (0, i, 0))]
    out_spec = pl.BlockSpec((tr, w), lambda i: (base + i, 0))
    out_shape = jax.ShapeDtypeStruct((n_layers * rows, w), F32)
    if buf is None:
        def body(p_ref, o_ref):
            compute(p_ref, o_ref)
        return pl.pallas_call(body, name="sum_slots", grid=(rows // tr,), in_specs=in_specs, out_specs=out_spec,
                              out_shape=out_shape, compiler_params=_params(48))(parts)

    def body(p_ref, buf_ref, o_ref):
        del buf_ref
        compute(p_ref, o_ref)
    return pl.pallas_call(body, name="sum_slots_into", grid=(rows // tr,), in_specs=in_specs + [_ANY],
                          out_specs=out_spec, out_shape=out_shape, input_output_aliases={1: 0},
                          compiler_params=_params(48))(parts, buf)


def adamw(w, ga, gb, m, v):
    rows, wd = w.shape
    tr = min(_row_tile(rows), 128)
    c1 = 1.0 / (1.0 - ADAM_B1 ** ADAM_STEP)
    c2 = 1.0 / (1.0 - ADAM_B2 ** ADAM_STEP)

    def update(wv, g, mv, vv, g_ref, d_ref, m_ref, v_ref):
        mn = ADAM_B1 * mv + (1.0 - ADAM_B1) * g
        vn = ADAM_B2 * vv + (1.0 - ADAM_B2) * (g * g)
        g_ref[...] = g
        m_ref[...] = mn
        v_ref[...] = vn
        d_ref[...] = -ADAM_LR * ((mn * c1) / (jnp.sqrt(vn * c2) + ADAM_EPS) + ADAM_WD * wv)
    tile = pl.BlockSpec((tr, wd), lambda i: (i, 0))
    out = jax.ShapeDtypeStruct((rows, wd), F32)
    if gb is None:
        def body(w_ref, ga_ref, m_ref, v_ref, g_out, d_out, m_out, v_out):
            update(w_ref[...], ga_ref[...], m_ref[...], v_ref[...], g_out, d_out, m_out, v_out)
        return pl.pallas_call(body, name="adamw", grid=(rows // tr,), in_specs=[tile] * 4,
                              out_specs=[tile] * 4, out_shape=[out] * 4)(w, ga, m, v)

    def body(w_ref, ga_ref, gb_ref, m_ref, v_ref, g_out, d_out, m_out, v_out):
        update(w_ref[...], ga_ref[...] + gb_ref[...], m_ref[...], v_ref[...], g_out, d_out, m_out, v_out)
    return pl.pallas_call(body, name="adamw_sum", grid=(rows // tr,), in_specs=[tile] * 5,
                          out_specs=[tile] * 4, out_shape=[out] * 4)(w, ga, gb, m, v)


def _rope_tables(T, R):
    rows = T // GRID_W
    row = jnp.repeat(jnp.arange(rows), GRID_W).astype(F32)
    col = jnp.tile(jnp.arange(GRID_W), rows).astype(F32)
    n_freq = HEAD_DIM // 4
    inv_freq = ROPE_THETA ** (-jnp.arange(n_freq, dtype=F32) / n_freq)
    ang = jnp.concatenate([row[:, None] * inv_freq, col[:, None] * inv_freq], axis=-1)
    cos, sin = jnp.cos(ang), jnp.sin(ang)
    cs = jnp.tile(cos, (1, 4))
    sn = jnp.tile(jnp.concatenate([-sin, sin], axis=-1), (1, 2))
    pad = R - T
    return (jnp.concatenate([cs, jnp.ones((pad, 128), F32)], axis=0),
            jnp.concatenate([sn, jnp.zeros((pad, 128), F32)], axis=0))


def _pack(parts, mult=8 * 128):
    flat = jnp.concatenate([p.reshape(-1).astype(F32) for p in parts])
    pad = (-flat.shape[0]) % mult
    return jnp.pad(flat, (0, pad)).reshape(-1, 128)


def _unpack(buf, shapes):
    flat = buf.reshape(-1)
    out, o = [], 0
    for s in shapes:
        n = 1
        for d in s:
            n *= d
        out.append(flat[o:o + n].reshape(s))
        o += n
    return out


def kernel(x, c, ctx, c_ctx, w_mod, b_mod, norm_mix, norm_ffn, w_in_ab, conv_a, conv_b, conv_b_bias, ln_b_gain, ln_b_bias, w_out_ab, w_qkv, w_o, sinks, w_up, w_conv_ffn, w_down, final_norm, loss_target, m_c_ctx, m_w_mod, m_b_mod, m_norm_mix, m_norm_ffn, m_w_in_ab, m_conv_a, m_conv_b, m_conv_b_bias, m_ln_b_gain, m_ln_b_bias, m_w_out_ab, m_w_qkv, m_w_o, m_sinks, m_w_up, m_w_conv_ffn, m_w_down, m_final_norm, v_c_ctx, v_w_mod, v_b_mod, v_norm_mix, v_norm_ffn, v_w_in_ab, v_conv_a, v_conv_b, v_conv_b_bias, v_ln_b_gain, v_ln_b_bias, v_w_out_ab, v_w_qkv, v_w_o, v_sinks, v_w_up, v_w_conv_ffn, v_w_down, v_final_norm):
    T, dm = x.shape[1], x.shape[2]
    tc = ctx.shape[1]
    R = T + tc
    depth = w_mod.shape[0]
    ax, ay, ac = lax.axis_index("x"), lax.axis_index("y"), lax.axis_index("c")
    chip = 2 * ax + ay
    dev = 4 * ax + 2 * ay + ac

    small_w = [conv_a, conv_b, w_conv_ffn]
    gathered = allgather8(_pack([c] + small_w)).reshape(N_DEV, -1)
    cond8 = gathered[:, :dm]
    off = dm
    full_small = []
    for wsh in small_w:
        n = wsh.size
        per_chip = gathered[0::2, off:off + n].reshape((N_CHIP,) + wsh.shape)
        full_small.append(jnp.concatenate([per_chip[q] for q in range(N_CHIP)], axis=-1))
        off += n
    conv_a_f, conv_b_f, w_conv_ffn_f = full_small
    cond = jnp.concatenate([cond8, c_ctx[None, :], jnp.zeros((7, dm), F32)], axis=0)

    masters = {"w_in_ab": w_in_ab, "w_out_ab": w_out_ab, "w_qkv": w_qkv, "w_o": w_o, "w_up": w_up, "w_down": w_down}

    def layer_weights(l):
        mixer = [("w_in_ab", l // 2), ("w_out_ab", l // 2)] if l % 2 == 0 else [("w_qkv", l // 2), ("w_o", l // 2)]
        return mixer + [("w_up", l), ("w_down", l)]
    in_flight, after = [], c
    for l in range(depth):
        shards = [masters[n][j].astype(BF16) for n, j in layer_weights(l)]
        lands = own_into_gather(shards)
        send_sems, recv_sems, shards, lands, after = exchange_start(f"gather_start_{l}", "gather", shards, lands, after)
        in_flight.append((send_sems, recv_sems, shards, lands))
    cond = cond + after[0, 0]

    cs, sn = _rope_tables(T, R)
    sinks_flat = sinks.reshape(-1)

    xs = jnp.concatenate([x[0], ctx[0]], axis=0)
    ns_mod = w_mod.shape[2]
    b_mod_sh = lax.dynamic_slice_in_dim(b_mod, chip * ns_mod, ns_mod, axis=1)[:, None, :]
    mod_sh = adaln_fwd(cond, w_mod, b_mod_sh)
    mod_all = allgather8(mod_sh.reshape(depth * 16, ns_mod)).reshape(N_DEV, depth, 16, ns_mod)
    mod_full = jnp.concatenate([mod_all[2 * q] for q in range(N_CHIP)], axis=-1)
    mine = lax.dynamic_index_in_dim(mod_full, dev, axis=1, keepdims=False)
    mods = jnp.stack([mine, mod_full[:, 8]], axis=1).reshape(depth, 2, 6, dm)

    saved, W = [], []
    h1 = norm_mod_fwd(xs, norm_mix[0][None], mods[0], 0, T)
    for l in range(depth):
        e = l // 2
        send_sems, recv_sems, shards, lands = in_flight[l]
        wl = dict(zip([n for n, _ in layer_weights(l)],
                      exchange_wait(f"gather_wait_{l}", "gather", send_sems, recv_sems, shards, lands, h1)))
        W.append(wl)
        s = {"x0": xs, "h1": h1}
        if l % 2 == 0:
            p = mm_nn(h1, wl["w_in_ab"], "col", BF16)
            yab = convmix_fwd(p, conv_a_f[e], conv_b_f[e], conv_b_bias[e][None], ln_b_gain[e][None],
                              ln_b_bias[e][None], T)
            y1 = mm_nn(yab, wl["w_out_ab"], "row", F32)
            s.update(p=p, mix=yab)
        else:
            qkv = mm_nn(h1, wl["w_qkv"], "col", F32)
            qkvr = rope_fwd(qkv, cs, sn)
            att = attn_fwd(qkvr, sinks_flat[e * N_HEADS:(e + 1) * N_HEADS], T)
            y1 = mm_nn(att, wl["w_o"], "row", F32)
            s.update(qkvr=qkvr, mix=att)
        x1, h2 = resid_norm_fwd(xs, y1, norm_ffn[l][None], mods[l], mods[l], 2, 3, T)
        u = mm_nn(h2, wl["w_up"], "col", BF16)
        z = ffnconv_fwd(u, w_conv_ffn_f[l], T)
        y2 = mm_nn(z, wl["w_down"], "row", F32)
        if l + 1 < depth:
            xs, h1 = resid_norm_fwd(x1, y2, norm_mix[l + 1][None], mods[l], mods[l + 1], 5, 0, T)
        else:
            xs = resid_fwd(x1, y2, mods[l], 5, T)
        s.update(y1=y1, x1=x1, h2=h2, u=u, z=z, y2=y2)
        saved.append(s)

    loss_part, dx, d_final = loss_head(xs, final_norm[None], loss_target[0], T)
    loss = lax.psum(loss_part[0, 0], ("x", "y", "c"))

    d_mods, d_norm_mix, d_norm_ffn = [None] * depth, [None] * depth, [None] * depth
    d_conv_a, d_conv_b, d_vecs, d_sinks, d_wc = [None] * 2, [None] * 2, [None] * 2, [None] * 2, [None] * depth
    dss1, dss2, dg1, dg2 = [None] * depth, [None] * depth, [None] * depth, [None] * depth
    scattering = [None] * depth
    token = None
    dy2, dg2[depth - 1] = resid_bwd(dx, saved[depth - 1]["y2"], mods[depth - 1], 5, T)
    for l in reversed(range(depth)):
        e = l // 2
        s, wl = saved[l], W[l]
        mod_l = mods[l] if token is None else mods[l] + token[0, 0]
        G = {}
        G["w_down"] = mm_tn(s["z"], dy2, "row", wl["w_down"])
        dz = mm_nt(dy2, wl["w_down"], "row", BF16)
        duc, d_wc[l] = ffnconv_bwd1(dz, s["u"], w_conv_ffn_f[l], T)
        du = ffnconv_bwd2(duc, w_conv_ffn_f[l], T)
        G["w_up"] = mm_tn(s["h2"], du, "col", wl["w_up"])
        dh2 = mm_nt(du, wl["w_up"], "col", F32)
        dx, dy1, dss2[l], d_norm_ffn[l], dg1[l] = norm_resid_bwd(
            dh2, s["x1"], norm_ffn[l][None], mod_l, dx, s["y1"], mod_l, 3, 2, T)
        if l % 2 == 0:
            G["w_out_ab"] = mm_tn(s["mix"], dy1, "row", wl["w_out_ab"])
            dyab = mm_nt(dy1, wl["w_out_ab"], "row", F32)
            dmid, d_conv_a[e], d_conv_b[e], d_vecs[e] = convmix_bwd1(
                dyab, s["p"], conv_a_f[e], conv_b_f[e], conv_b_bias[e][None], ln_b_gain[e][None],
                ln_b_bias[e][None], T)
            dp = convmix_bwd2(dmid, s["p"], conv_a_f[e], conv_b_f[e], T)
            G["w_in_ab"] = mm_tn(s["h1"], dp, "col", wl["w_in_ab"])
            dh1 = mm_nt(dp, wl["w_in_ab"], "col", F32)
        else:
            G["w_o"] = mm_tn(s["mix"], dy1, "row", wl["w_o"])
            datt = mm_nt(dy1, wl["w_o"], "row", BF16)
            dq, dks, dvs, dkc, dvc, d_sinks[e] = attn_bwd(
                s["qkvr"], datt, sinks_flat[e * N_HEADS:(e + 1) * N_HEADS], T)
            dqkv = rope_bwd(dq, dks, dvs, dkc, dvc, cs, sn, T)
            G["w_qkv"] = mm_tn(s["h1"], dqkv, "col", wl["w_qkv"])
            dh1 = mm_nt(dqkv, wl["w_qkv"], "col", F32)
        if l > 0:
            dx, dy2, dss1[l], d_norm_mix[l], dg2[l - 1] = norm_resid_bwd(
                dh1, s["x0"], norm_mix[l][None], mod_l, dx, saved[l - 1]["y2"], mods[l - 1], 0, 5, T)
        else:
            dx, dss1[l], d_norm_mix[l] = norm_mod_bwd(dh1, s["x0"], norm_mix[l][None], mod_l, dx, 0, T)
        names = [n for n, _ in layer_weights(l)]
        grads_l = [G[n] for n in names]
        lands = own_into_scatter(grads_l)
        send_sems, recv_sems, grads_l, lands, token = exchange_start(
            f"scatter_start_{l}", "scatter", grads_l, lands, dx)
        scattering[l] = (send_sems, recv_sems, grads_l, lands)
    grad_x = dx[:T][None]
    for l in range(depth):
        a1, a2 = dss1[l].sum(2), dss2[l].sum(2)
        d_mods[l] = jnp.stack([a1[:, 0], a1[:, 1], dg1[l].sum(1), a2[:, 0], a2[:, 1], dg2[l].sum(1)], axis=1)

    d_mods = jnp.stack(d_mods)
    summed_parts = [
        d_mods[:, 1],
        jnp.stack(d_norm_mix).sum(1), jnp.stack(d_norm_ffn).sum(1),
        jnp.stack(d_conv_a).sum(2), jnp.stack(d_conv_b).sum(2),
        jnp.stack(d_vecs).sum(2),
        jnp.stack(d_sinks)[:, 0, :N_HEADS],
        jnp.stack(d_wc).sum(2), d_final.sum(0)]
    summed_shapes = [p.shape for p in summed_parts]
    n_own = depth * 6 * dm
    pack = _pack([d_mods[:, 0]] + summed_parts)
    parts = allgather8(pack).reshape(N_DEV, -1, 128)
    total = sum_slots(parts)
    own_rows = parts.reshape(N_DEV, -1)[:, :n_own].reshape(N_DEV, depth, 6 * dm)
    (dmod_ctx, g_norm_mix, g_norm_ffn, g_conv_a, g_conv_b, g_vecs, g_sinks, g_wc, g_final) = _unpack(
        total.reshape(-1)[n_own:], summed_shapes)
    dmod_rows = jnp.concatenate([jnp.moveaxis(own_rows, 0, 1), dmod_ctx.reshape(depth, 1, 6 * dm),
                                 jnp.zeros((depth, 7, 6 * dm), F32)], axis=1)
    g_b_mod = dmod_rows.sum(1)
    dmod_sh = lax.dynamic_slice_in_dim(dmod_rows, chip * ns_mod, ns_mod, axis=2)
    g_w_mod, dsilu = adaln_bwd(cond, dmod_sh, w_mod)
    dsilu_all = allgather8(dsilu[8:16]).reshape(N_DEV, 8, dm)
    dsilu_ctx = sum_slots(dsilu_all[0::2])[0]
    sg = jax.nn.sigmoid(c_ctx)
    g_c_ctx = dsilu_ctx * (sg * (1.0 + c_ctx * (1.0 - sg)))

    def shard_cols(full, width):
        return lax.dynamic_slice_in_dim(full, chip * width, width, axis=full.ndim - 1)
    g_conv_a_s = shard_cols(g_conv_a, conv_a.shape[-1])
    g_conv_b_s = shard_cols(g_conv_b, conv_b.shape[-1])
    g_wc_s = shard_cols(g_wc, w_conv_ffn.shape[-1])

    grads, deltas, new_m, new_v = {}, {}, {}, {}

    def step_2d(name, wv, ga, gb, mv, vv):
        shp = wv.shape
        r2 = lambda t: t.reshape(-1, shp[-1])
        g, d, mn, vn = adamw(r2(wv), r2(ga), None if gb is None else r2(gb), r2(mv), r2(vv))
        grads[name], deltas[name], new_m[name], new_v[name] = (t.reshape(shp) for t in (g, d, mn, vn))

    sums = {n: None for n in masters}
    for l in reversed(range(depth)):
        send_sems, recv_sems, grads_l, lands = scattering[l]
        landed = exchange_wait(f"scatter_wait_{l}", "scatter", send_sems, recv_sems, grads_l, lands, token)
        for (n, j), arr in zip(layer_weights(l), landed):
            wv = masters[n]
            sums[n] = sum_slots(arr.reshape(N_CHIP, -1, wv.shape[-1]), j, wv.shape[0], sums[n])
    moments = {"w_in_ab": (m_w_in_ab, v_w_in_ab), "w_out_ab": (m_w_out_ab, v_w_out_ab),
               "w_qkv": (m_w_qkv, v_w_qkv), "w_o": (m_w_o, v_w_o), "w_up": (m_w_up, v_w_up),
               "w_down": (m_w_down, v_w_down)}
    for name, wv in masters.items():
        other = swap_with_sibling(sums[name])
        step_2d(name, wv, sums[name].reshape(wv.shape), other.reshape(wv.shape), *moments[name])
    step_2d("w_mod", w_mod, g_w_mod, None, m_w_mod, v_w_mod)

    small = [("c_ctx", c_ctx, g_c_ctx, m_c_ctx, v_c_ctx), ("b_mod", b_mod, g_b_mod, m_b_mod, v_b_mod),
             ("norm_mix", norm_mix, g_norm_mix, m_norm_mix, v_norm_mix),
             ("norm_ffn", norm_ffn, g_norm_ffn, m_norm_ffn, v_norm_ffn),
             ("conv_a", conv_a, g_conv_a_s, m_conv_a, v_conv_a), ("conv_b", conv_b, g_conv_b_s, m_conv_b, v_conv_b),
             ("conv_b_bias", conv_b_bias, g_vecs[:, 0], m_conv_b_bias, v_conv_b_bias),
             ("ln_b_gain", ln_b_gain, g_vecs[:, 1], m_ln_b_gain, v_ln_b_gain),
             ("ln_b_bias", ln_b_bias, g_vecs[:, 2], m_ln_b_bias, v_ln_b_bias),
             ("sinks", sinks, g_sinks, m_sinks, v_sinks),
             ("w_conv_ffn", w_conv_ffn, g_wc_s, m_w_conv_ffn, v_w_conv_ffn),
             ("final_norm", final_norm, g_final, m_final_norm, v_final_norm)]
    shapes = [t[1].shape for t in small]
    packed = [_pack([t[k] for t in small]) for k in (1, 2, 3, 4)]
    n_real = sum(t[1].size for t in small)
    lane_id = jnp.arange(packed[3].size).reshape(packed[3].shape)
    packed[3] = jnp.where(lane_id < n_real, packed[3], 1.0)
    outs = adamw(packed[0], packed[1], None, packed[2], packed[3])
    for (name, *_), g, d, mn, vn in zip(small, *[_unpack(o, shapes) for o in outs]):
        grads[name], deltas[name], new_m[name], new_v[name] = g, d, mn, vn

    order = ["c_ctx", "w_mod", "b_mod", "norm_mix", "norm_ffn", "w_in_ab", "conv_a", "conv_b", "conv_b_bias",
             "ln_b_gain", "ln_b_bias", "w_out_ab", "w_qkv", "w_o", "sinks", "w_up", "w_conv_ffn", "w_down",
             "final_norm"]
    return (loss, grad_x, *[grads[n] for n in order], *[deltas[n] for n in order],
            *[new_m[n] for n in order], *[new_v[n] for n in order])
```

```python
import jax
import jax.numpy as jnp
from jax import lax
from jax.experimental import pallas as pl
from jax.experimental.pallas import tpu as pltpu

F32 = jnp.float32
BF16 = jnp.bfloat16
MESH = pl.DeviceIdType.MESH

EPS = 1e-6
NEG_INF = -1e30
GRID_W = 64
HEAD_DIM = 64
N_HEADS = 16
WINDOW = 128
QB = 128
ROPE_THETA = 10000.0
A_W = 512
B_CONV = 31
D_FF = 2816
ADAM_LR, ADAM_B1, ADAM_B2, ADAM_EPS, ADAM_WD, ADAM_STEP = 0.001, 0.9, 0.999, 1e-8, 0.01, 10

TMR = 256
HALO = 16
N_DEV = 8
N_CHIP = 4


def _params(vmem_mb=None):
    if vmem_mb is None:
        return pltpu.CompilerParams()
    return pltpu.CompilerParams(vmem_limit_bytes=vmem_mb * 1024 * 1024)


def _row_tile(rows):
    for t in (768, 512, 256, 128, 64, 32, 16, 8):
        if rows % t == 0:
            return t
    raise ValueError(rows)


def _colsum8(v):
    r, c = v.shape
    return v.reshape(r // 8, 8, c).sum(axis=0)


def _sigmoid(v):
    return 1.0 / (1.0 + jnp.exp(-v))


def mm_nn(a, w, kind, out_dtype):
    R = a.shape[0]
    _, kb, nb = w.shape
    tm = _row_tile(R)
    if kind == "col":
        def body(a_ref, w_ref, o_ref):
            o_ref[...] = jnp.dot(a_ref[...].astype(BF16), w_ref[...],
                                 preferred_element_type=F32).astype(o_ref.dtype)
        return pl.pallas_call(
            body, name="mm_nn_col", grid=(N_CHIP, R // tm),
            in_specs=[pl.BlockSpec((tm, kb), lambda q, i: (i, 0)),
                      pl.BlockSpec((None, kb, nb), lambda q, i: (q, 0, 0))],
            out_specs=pl.BlockSpec((tm, nb), lambda q, i: (i, q)),
            out_shape=jax.ShapeDtypeStruct((R, N_CHIP * nb), out_dtype),
            compiler_params=_params(48))(a, w)

    def body(a_ref, w_ref, o_ref):
        wv = w_ref[...].reshape(N_CHIP * kb, nb)
        o_ref[...] = jnp.dot(a_ref[...].astype(BF16), wv, preferred_element_type=F32).astype(o_ref.dtype)
    return pl.pallas_call(
        body, name="mm_nn_row", grid=(R // tm,),
        in_specs=[pl.BlockSpec((tm, N_CHIP * kb), lambda i: (i, 0)),
                  pl.BlockSpec((N_CHIP, kb, nb), lambda i: (0, 0, 0))],
        out_specs=pl.BlockSpec((tm, nb), lambda i: (i, 0)),
        out_shape=jax.ShapeDtypeStruct((R, nb), out_dtype),
        compiler_params=_params(48))(a, w)


def mm_nt(d, w, kind, out_dtype):
    R = d.shape[0]
    _, kb, nb = w.shape
    tm = _row_tile(R)
    contract_last = (((1,), (1,)), ((), ()))
    if kind == "col":
        def body(d_ref, w_ref, o_ref, acc_ref):
            q = pl.program_id(1)

            @pl.when(q == 0)
            def _():
                acc_ref[...] = jnp.zeros_like(acc_ref)
            acc_ref[...] += lax.dot_general(d_ref[...].astype(BF16), w_ref[...], contract_last,
                                            preferred_element_type=F32)

            @pl.when(q == N_CHIP - 1)
            def _():
                o_ref[...] = acc_ref[...].astype(o_ref.dtype)
        return pl.pallas_call(
            body, name="mm_nt_col", grid=(R // tm, N_CHIP),
            in_specs=[pl.BlockSpec((tm, nb), lambda i, q: (i, q)),
                      pl.BlockSpec((None, kb, nb), lambda i, q: (q, 0, 0))],
            out_specs=pl.BlockSpec((tm, kb), lambda i, q: (i, 0)),
            out_shape=jax.ShapeDtypeStruct((R, kb), out_dtype),
            scratch_shapes=[pltpu.VMEM((tm, kb), F32)],
            compiler_params=_params(48))(d, w)

    def body(d_ref, w_ref, o_ref):
        wv = w_ref[...].reshape(N_CHIP * kb, nb)
        o_ref[...] = lax.dot_general(d_ref[...].astype(BF16), wv, contract_last,
                                     preferred_element_type=F32).astype(o_ref.dtype)
    return pl.pallas_call(
        body, name="mm_nt_row", grid=(R // tm,),
        in_specs=[pl.BlockSpec((tm, nb), lambda i: (i, 0)),
                  pl.BlockSpec((N_CHIP, kb, nb), lambda i: (0, 0, 0))],
        out_specs=pl.BlockSpec((tm, N_CHIP * kb), lambda i: (i, 0)),
        out_shape=jax.ShapeDtypeStruct((R, N_CHIP * kb), out_dtype),
        compiler_params=_params(48))(d, w)


def mm_tn(a, d, kind, like):
    R = a.shape[0]
    _, kb, nb = like.shape
    tm = _row_tile(R)
    nsteps = R // tm
    contract_rows = (((0,), (0,)), ((), ()))
    out_shape = jax.ShapeDtypeStruct(like.shape, BF16)

    def accumulate(a_ref, d_ref, acc_ref):
        @pl.when(pl.program_id(1) == 0)
        def _():
            acc_ref[...] = jnp.zeros_like(acc_ref)
        acc_ref[...] += lax.dot_general(a_ref[...].astype(BF16), d_ref[...].astype(BF16), contract_rows,
                                        preferred_element_type=F32)
    if kind == "col":
        def body(a_ref, d_ref, o_ref, acc_ref):
            accumulate(a_ref, d_ref, acc_ref)

            @pl.when(pl.program_id(1) == nsteps - 1)
            def _():
                o_ref[...] = acc_ref[...].astype(BF16)
        return pl.pallas_call(
            body, name="mm_tn_col", grid=(N_CHIP, nsteps),
            in_specs=[pl.BlockSpec((tm, kb), lambda q, i: (i, 0)), pl.BlockSpec((tm, nb), lambda q, i: (i, q))],
            out_specs=pl.BlockSpec((None, kb, nb), lambda q, i: (q, 0, 0)), out_shape=out_shape,
            scratch_shapes=[pltpu.VMEM((kb, nb), F32)], compiler_params=_params(48))(a, d)
    tn = 512

    def body(a_ref, d_ref, o_ref, acc_ref):
        accumulate(a_ref, d_ref, acc_ref)

        @pl.when(pl.program_id(1) == nsteps - 1)
        def _():
            o_ref[...] = acc_ref[...].astype(BF16).reshape(N_CHIP, kb, tn)
    return pl.pallas_call(
        body, name="mm_tn_row", grid=(nb // tn, nsteps),
        in_specs=[pl.BlockSpec((tm, N_CHIP * kb), lambda n, i: (i, 0)), pl.BlockSpec((tm, tn), lambda n, i: (i, n))],
        out_specs=pl.BlockSpec((N_CHIP, kb, tn), lambda n, i: (0, 0, n)), out_shape=out_shape,
        scratch_shapes=[pltpu.VMEM((N_CHIP * kb, tn), F32)], compiler_params=_params(48))(a, d)


def _seg(i, T):
    return (i >= T // TMR).astype(jnp.int32)


def norm_mod_fwd(x, nw, mod, k, T):
    R, dm = x.shape

    def body(x_ref, nw_ref, mod_ref, h_ref):
        seg = _seg(pl.program_id(0), T)
        sh = mod_ref[seg, pl.ds(k, 1), :]
        sc = mod_ref[seg, pl.ds(k + 1, 1), :]
        xv = x_ref[...]
        r = lax.rsqrt(jnp.mean(xv * xv, axis=-1, keepdims=True) + EPS)
        h_ref[...] = ((xv * r * nw_ref[...]) * (1.0 + sc) + sh).astype(BF16)
    return pl.pallas_call(
        body, name="norm_mod_fwd", grid=(R // TMR,),
        in_specs=[pl.BlockSpec((TMR, dm), lambda i: (i, 0)),
                  pl.BlockSpec((1, dm), lambda i: (0, 0)),
                  pl.BlockSpec((2, 6, dm), lambda i: (0, 0, 0))],
        out_specs=pl.BlockSpec((TMR, dm), lambda i: (i, 0)),
        out_shape=jax.ShapeDtypeStruct((R, dm), BF16))(x, nw, mod)


def norm_mod_bwd(dh, x, nw, mod, dxr, k, T):
    R, dm = x.shape

    def body(dh_ref, x_ref, nw_ref, mod_ref, dxr_ref, dx_ref, dmod_ref, dnw_ref):
        i = pl.program_id(0)
        seg = _seg(i, T)

        @pl.when(i == 0)
        def _():
            dmod_ref[...] = jnp.zeros_like(dmod_ref)
            dnw_ref[...] = jnp.zeros_like(dnw_ref)
        sc = mod_ref[seg, pl.ds(k + 1, 1), :]
        nwv = nw_ref[...]
        xv = x_ref[...]
        r = lax.rsqrt(jnp.mean(xv * xv, axis=-1, keepdims=True) + EPS)
        xh = xv * r
        dhv = dh_ref[...]
        dmod_ref[seg, 0] += _colsum8(dhv)
        dmod_ref[seg, 1] += _colsum8(dhv * (xh * nwv))
        dn = dhv * (1.0 + sc)
        dnw_ref[...] += _colsum8(dn * xh)
        dxh = dn * nwv
        dx = r * (dxh - xh * jnp.mean(dxh * xh, axis=-1, keepdims=True))
        dx_ref[...] = dxr_ref[...] + dx
    tile = pl.BlockSpec((TMR, dm), lambda i: (i, 0))
    return pl.pallas_call(
        body, name="norm_mod_bwd", grid=(R // TMR,),
        in_specs=[tile, tile, pl.BlockSpec((1, dm), lambda i: (0, 0)),
                  pl.BlockSpec((2, 6, dm), lambda i: (0, 0, 0)), tile],
        out_specs=[tile, pl.BlockSpec((2, 2, 8, dm), lambda i: (0, 0, 0, 0)),
                   pl.BlockSpec((8, dm), lambda i: (0, 0))],
        out_shape=[jax.ShapeDtypeStruct((R, dm), F32), jax.ShapeDtypeStruct((2, 2, 8, dm), F32),
                   jax.ShapeDtypeStruct((8, dm), F32)])(dh, x, nw, mod, dxr)


def resid_norm_fwd(x, y, nw, mod_g, mod_n, kg, kn, T):
    R, dm = x.shape

    def body(x_ref, y_ref, nw_ref, mg_ref, mn_ref, xo_ref, h_ref):
        seg = _seg(pl.program_id(0), T)
        xv = x_ref[...] + mg_ref[seg, pl.ds(kg, 1), :] * y_ref[...]
        xo_ref[...] = xv
        r = lax.rsqrt(jnp.mean(xv * xv, axis=-1, keepdims=True) + EPS)
        h_ref[...] = ((xv * r * nw_ref[...]) * (1.0 + mn_ref[seg, pl.ds(kn + 1, 1), :])
                      + mn_ref[seg, pl.ds(kn, 1), :]).astype(BF16)
    tile = pl.BlockSpec((TMR, dm), lambda i: (i, 0))
    modspec = pl.BlockSpec((2, 6, dm), lambda i: (0, 0, 0))
    return pl.pallas_call(
        body, name="resid_norm_fwd", grid=(R // TMR,),
        in_specs=[tile, tile, pl.BlockSpec((1, dm), lambda i: (0, 0)), modspec, modspec],
        out_specs=[tile, tile],
        out_shape=[jax.ShapeDtypeStruct((R, dm), F32), jax.ShapeDtypeStruct((R, dm), BF16)])(x, y, nw, mod_g, mod_n)


def norm_resid_bwd(dh, x, nw, mod_n, dxr, y, mod_g, kn, kg, T):
    R, dm = x.shape

    def body(dh_ref, x_ref, nw_ref, mn_ref, dxr_ref, y_ref, mg_ref, dx_ref, dy_ref, dmod_ref, dnw_ref, dg_ref):
        i = pl.program_id(0)
        seg = _seg(i, T)

        @pl.when(i == 0)
        def _():
            dmod_ref[...] = jnp.zeros_like(dmod_ref)
            dnw_ref[...] = jnp.zeros_like(dnw_ref)
            dg_ref[...] = jnp.zeros_like(dg_ref)
        sc = mn_ref[seg, pl.ds(kn + 1, 1), :]
        nwv = nw_ref[...]
        xv = x_ref[...]
        r = lax.rsqrt(jnp.mean(xv * xv, axis=-1, keepdims=True) + EPS)
        xh = xv * r
        dhv = dh_ref[...]
        dmod_ref[seg, 0] += _colsum8(dhv)
        dmod_ref[seg, 1] += _colsum8(dhv * (xh * nwv))
        dn = dhv * (1.0 + sc)
        dnw_ref[...] += _colsum8(dn * xh)
        dxh = dn * nwv
        dx = dxr_ref[...] + r * (dxh - xh * jnp.mean(dxh * xh, axis=-1, keepdims=True))
        dx_ref[...] = dx
        dy_ref[...] = (mg_ref[seg, pl.ds(kg, 1), :] * dx).astype(BF16)
        dg_ref[seg] += _colsum8(dx * y_ref[...])
    tile = pl.BlockSpec((TMR, dm), lambda i: (i, 0))
    modspec = pl.BlockSpec((2, 6, dm), lambda i: (0, 0, 0))
    return pl.pallas_call(
        body, name="norm_resid_bwd", grid=(R // TMR,),
        in_specs=[tile, tile, pl.BlockSpec((1, dm), lambda i: (0, 0)), modspec, tile, tile, modspec],
        out_specs=[tile, tile, pl.BlockSpec((2, 2, 8, dm), lambda i: (0, 0, 0, 0)),
                   pl.BlockSpec((8, dm), lambda i: (0, 0)), pl.BlockSpec((2, 8, dm), lambda i: (0, 0, 0))],
        out_shape=[jax.ShapeDtypeStruct((R, dm), F32), jax.ShapeDtypeStruct((R, dm), BF16),
                   jax.ShapeDtypeStruct((2, 2, 8, dm), F32), jax.ShapeDtypeStruct((8, dm), F32),
                   jax.ShapeDtypeStruct((2, 8, dm), F32)])(dh, x, nw, mod_n, dxr, y, mod_g)


def resid_fwd(x, y, mod, k, T):
    R, dm = x.shape

    def body(x_ref, y_ref, mod_ref, o_ref):
        seg = _seg(pl.program_id(0), T)
        o_ref[...] = x_ref[...] + mod_ref[seg, pl.ds(k, 1), :] * y_ref[...]
    tile = pl.BlockSpec((TMR, dm), lambda i: (i, 0))
    return pl.pallas_call(
        body, name="resid_fwd", grid=(R // TMR,),
        in_specs=[tile, tile, pl.BlockSpec((2, 6, dm), lambda i: (0, 0, 0))],
        out_specs=tile, out_shape=jax.ShapeDtypeStruct((R, dm), F32))(x, y, mod)


def resid_bwd(dxn, y, mod, k, T):
    R, dm = dxn.shape

    def body(dx_ref, y_ref, mod_ref, dy_ref, dg_ref):
        i = pl.program_id(0)
        seg = _seg(i, T)

        @pl.when(i == 0)
        def _():
            dg_ref[...] = jnp.zeros_like(dg_ref)
        dxv = dx_ref[...]
        dy_ref[...] = (mod_ref[seg, pl.ds(k, 1), :] * dxv).astype(BF16)
        dg_ref[seg] += _colsum8(dxv * y_ref[...])
    tile = pl.BlockSpec((TMR, dm), lambda i: (i, 0))
    return pl.pallas_call(
        body, name="resid_bwd", grid=(R // TMR,),
        in_specs=[tile, tile, pl.BlockSpec((2, 6, dm), lambda i: (0, 0, 0))],
        out_specs=[tile, pl.BlockSpec((2, 8, dm), lambda i: (0, 0, 0))],
        out_shape=[jax.ShapeDtypeStruct((R, dm), BF16), jax.ShapeDtypeStruct((2, 8, dm), F32)])(dxn, y, mod)


def _halo_specs(width, R):
    nblk = R // HALO
    per = TMR // HALO
    return (pl.BlockSpec((HALO, width), lambda i: (jnp.maximum(i * per - 1, 0), 0)),
            pl.BlockSpec((TMR, width), lambda i: (i, 0)),
            pl.BlockSpec((HALO, width), lambda i: (jnp.minimum((i + 1) * per, nblk - 1), 0)))


def _ext(refs, c0, cw):
    pref, ref, nref = refs
    return jnp.concatenate([pref[:, c0:c0 + cw].astype(F32), ref[:, c0:c0 + cw].astype(F32),
                            nref[:, c0:c0 + cw].astype(F32)], axis=0)


def _ext_mask(i, T, R):
    pos = i * TMR - HALO + lax.broadcasted_iota(jnp.int32, (TMR + 2 * HALO, 1), 0)
    lat = i < T // TMR
    lo = jnp.where(lat, 0, T)
    hi = jnp.where(lat, T, R)
    return (pos >= lo) & (pos < hi)


def _at(ext, off):
    n = ext.shape[0]
    s = (-off) % n
    y = pltpu.roll(ext, s, 0) if s else ext
    return y[HALO:HALO + TMR]


def ffnconv_fwd(u, wc, T):
    R, w2 = u.shape
    cw = 256

    def body(up_ref, u_ref, un_ref, wc_ref, z_ref):
        i = pl.program_id(0)
        mask = _ext_mask(i, T, R)

        def conv(c0):
            e = jnp.where(mask, _ext((up_ref, u_ref, un_ref), c0, cw), 0.0)
            return (wc_ref[pl.ds(0, 1), c0:c0 + cw] * _at(e, -1) + wc_ref[pl.ds(1, 1), c0:c0 + cw] * _at(e, 0)
                    + wc_ref[pl.ds(2, 1), c0:c0 + cw] * _at(e, 1))
        for j in range(D_FF // cw):
            a = conv(j * cw)
            g = conv(D_FF + j * cw)
            z_ref[:, j * cw:(j + 1) * cw] = (g * _sigmoid(g) * a).astype(BF16)
    return pl.pallas_call(
        body, name="ffnconv_fwd", grid=(R // TMR,),
        in_specs=[*_halo_specs(w2, R), pl.BlockSpec((3, w2), lambda i: (0, 0))],
        out_specs=pl.BlockSpec((TMR, D_FF), lambda i: (i, 0)),
        out_shape=jax.ShapeDtypeStruct((R, D_FF), BF16), compiler_params=_params(48))(u, u, u, wc)


def ffnconv_bwd1(dz, u, wc, T):
    R, w2 = u.shape
    cw = 256

    def body(dz_ref, up_ref, u_ref, un_ref, wc_ref, duc_ref, dwc_ref):
        i = pl.program_id(0)
        mask = _ext_mask(i, T, R)

        @pl.when(i == 0)
        def _():
            dwc_ref[...] = jnp.zeros_like(dwc_ref)

        def taps(c0):
            e = jnp.where(mask, _ext((up_ref, u_ref, un_ref), c0, cw), 0.0)
            return [_at(e, -1), _at(e, 0), _at(e, 1)]

        def conv(t, c0):
            return (wc_ref[pl.ds(0, 1), c0:c0 + cw] * t[0] + wc_ref[pl.ds(1, 1), c0:c0 + cw] * t[1]
                    + wc_ref[pl.ds(2, 1), c0:c0 + cw] * t[2])
        for j in range(D_FF // cw):
            ca, cg = j * cw, D_FF + j * cw
            ta, tg = taps(ca), taps(cg)
            a, g = conv(ta, ca), conv(tg, cg)
            dzv = dz_ref[:, ca:ca + cw].astype(F32)
            sg = _sigmoid(g)
            da = dzv * (g * sg)
            dg = dzv * a * (sg * (1.0 + g * (1.0 - sg)))
            duc_ref[:, ca:ca + cw] = da.astype(BF16)
            duc_ref[:, cg:cg + cw] = dg.astype(BF16)
            for k in range(3):
                dwc_ref[k, :, ca:ca + cw] += _colsum8(da * ta[k])
                dwc_ref[k, :, cg:cg + cw] += _colsum8(dg * tg[k])
    return pl.pallas_call(
        body, name="ffnconv_bwd1", grid=(R // TMR,),
        in_specs=[pl.BlockSpec((TMR, D_FF), lambda i: (i, 0)), *_halo_specs(w2, R),
                  pl.BlockSpec((3, w2), lambda i: (0, 0))],
        out_specs=[pl.BlockSpec((TMR, w2), lambda i: (i, 0)), pl.BlockSpec((3, 8, w2), lambda i: (0, 0, 0))],
        out_shape=[jax.ShapeDtypeStruct((R, w2), BF16), jax.ShapeDtypeStruct((3, 8, w2), F32)],
        compiler_params=_params(48))(dz, u, u, u, wc)


def ffnconv_bwd2(duc, wc, T):
    R, w2 = duc.shape
    cw = 256

    def body(dp_ref, d_ref, dn_ref, wc_ref, du_ref):
        mask = _ext_mask(pl.program_id(0), T, R)
        for j in range(w2 // cw):
            c0 = j * cw
            e = jnp.where(mask, _ext((dp_ref, d_ref, dn_ref), c0, cw), 0.0)
            du_ref[:, c0:c0 + cw] = (wc_ref[pl.ds(0, 1), c0:c0 + cw] * _at(e, 1)
                                     + wc_ref[pl.ds(1, 1), c0:c0 + cw] * _at(e, 0)
                                     + wc_ref[pl.ds(2, 1), c0:c0 + cw] * _at(e, -1)).astype(BF16)
    return pl.pallas_call(
        body, name="ffnconv_bwd2", grid=(R // TMR,),
        in_specs=[*_halo_specs(w2, R), pl.BlockSpec((3, w2), lambda i: (0, 0))],
        out_specs=pl.BlockSpec((TMR, w2), lambda i: (i, 0)),
        out_shape=jax.ShapeDtypeStruct((R, w2), BF16), compiler_params=_params(48))(duc, duc, duc, wc)


_CW = 128


def _mixer_a(prefs, wa_ref, mask):
    cin = jnp.where(mask, _ext(prefs, A_W, A_W) * _ext(prefs, 2 * A_W, A_W), 0.0)
    ca = (wa_ref[pl.ds(0, 1), :] * _at(cin, -1) + wa_ref[pl.ds(1, 1), :] * _at(cin, 0)
          + wa_ref[pl.ds(2, 1), :] * _at(cin, 1))
    return cin, ca


def _mixer_b(prefs, wb_ref, bias_ref, mask, ub_s, ub2_s):
    for cc in range(A_W // _CW):
        c0 = cc * _CW
        ub = jnp.where(mask, _ext(prefs, 3 * A_W + c0, _CW) * _sigmoid(_ext(prefs, 4 * A_W + c0, _CW)), 0.0)
        ub_s[:, c0:c0 + _CW] = ub
        acc = jnp.zeros((TMR, _CW), F32) + bias_ref[:, c0:c0 + _CW]
        for k in range(B_CONV):
            acc = acc + wb_ref[pl.ds(k, 1), c0:c0 + _CW] * _at(ub, k - B_CONV // 2)
        ub2_s[:, c0:c0 + _CW] = acc


def _layernorm_stats(v):
    mu = jnp.mean(v, axis=-1, keepdims=True)
    xc = v - mu
    rs = lax.rsqrt(jnp.mean(xc * xc, axis=-1, keepdims=True) + EPS)
    return xc * rs, rs


def convmix_fwd(p, wa, wb, bias, lng, lnb, T):
    R, wp = p.shape

    def body(pp_ref, p_ref, pn_ref, wa_ref, wb_ref, bias_ref, lng_ref, lnb_ref, o_ref, ub_s, ub2_s):
        mask = _ext_mask(pl.program_id(0), T, R)
        prefs = (pp_ref, p_ref, pn_ref)
        _, ca = _mixer_a(prefs, wa_ref, mask)
        o_ref[:, 0:A_W] = (p_ref[:, 0:A_W].astype(F32) * ca).astype(BF16)
        _mixer_b(prefs, wb_ref, bias_ref, mask, ub_s, ub2_s)
        xh, _ = _layernorm_stats(ub2_s[...])
        lv = xh * lng_ref[...] + lnb_ref[...]
        o_ref[:, A_W:2 * A_W] = (lv * _sigmoid(lv)).astype(BF16)
    vec = pl.BlockSpec((1, A_W), lambda i: (0, 0))
    return pl.pallas_call(
        body, name="convmix_fwd", grid=(R // TMR,),
        in_specs=[*_halo_specs(wp, R), pl.BlockSpec((3, A_W), lambda i: (0, 0)),
                  pl.BlockSpec((B_CONV, A_W), lambda i: (0, 0)), vec, vec, vec],
        out_specs=pl.BlockSpec((TMR, 2 * A_W), lambda i: (i, 0)),
        out_shape=jax.ShapeDtypeStruct((R, 2 * A_W), BF16),
        scratch_shapes=[pltpu.VMEM((TMR + 2 * HALO, A_W), F32), pltpu.VMEM((TMR, A_W), F32)],
        compiler_params=_params(48))(p, p, p, wa, wb, bias, lng, lnb)


def convmix_bwd1(dyab, p, wa, wb, bias, lng, lnb, T):
    R, wp = p.shape

    def body(dy_ref, pp_ref, p_ref, pn_ref, wa_ref, wb_ref, bias_ref, lng_ref, lnb_ref,
             dmid_ref, dwa_ref, dwb_ref, dvec_ref, ub_s, ub2_s):
        i = pl.program_id(0)
        mask = _ext_mask(i, T, R)

        @pl.when(i == 0)
        def _():
            dwa_ref[...] = jnp.zeros_like(dwa_ref)
            dwb_ref[...] = jnp.zeros_like(dwb_ref)
            dvec_ref[...] = jnp.zeros_like(dvec_ref)
        prefs = (pp_ref, p_ref, pn_ref)
        cin, ca = _mixer_a(prefs, wa_ref, mask)
        dya = dy_ref[:, 0:A_W]
        dmid_ref[:, 0:A_W] = dya * ca
        dca = dya * p_ref[:, 0:A_W].astype(F32)
        dmid_ref[:, A_W:2 * A_W] = dca
        for k in range(3):
            dwa_ref[k] += _colsum8(dca * _at(cin, k - 1))
        _mixer_b(prefs, wb_ref, bias_ref, mask, ub_s, ub2_s)
        xh, rs = _layernorm_stats(ub2_s[...])
        gain = lng_ref[...]
        lv = xh * gain + lnb_ref[...]
        sl = _sigmoid(lv)
        dl = dy_ref[:, A_W:2 * A_W] * (sl * (1.0 + lv * (1.0 - sl)))
        dvec_ref[1] += _colsum8(dl * xh)
        dvec_ref[2] += _colsum8(dl)
        dxh = dl * gain
        dub2 = rs * (dxh - jnp.mean(dxh, axis=-1, keepdims=True)
                     - xh * jnp.mean(dxh * xh, axis=-1, keepdims=True))
        dvec_ref[0] += _colsum8(dub2)
        dmid_ref[:, 2 * A_W:3 * A_W] = dub2
        for cc in range(A_W // _CW):
            c0 = cc * _CW
            ub = ub_s[:, c0:c0 + _CW]
            d = dmid_ref[:, 2 * A_W + c0:2 * A_W + c0 + _CW]
            for k in range(B_CONV):
                dwb_ref[k, :, c0:c0 + _CW] += _colsum8(d * _at(ub, k - B_CONV // 2))
    vec = pl.BlockSpec((1, A_W), lambda i: (0, 0))
    return pl.pallas_call(
        body, name="convmix_bwd1", grid=(R // TMR,),
        in_specs=[pl.BlockSpec((TMR, 2 * A_W), lambda i: (i, 0)), *_halo_specs(wp, R),
                  pl.BlockSpec((3, A_W), lambda i: (0, 0)), pl.BlockSpec((B_CONV, A_W), lambda i: (0, 0)),
                  vec, vec, vec],
        out_specs=[pl.BlockSpec((TMR, 3 * A_W), lambda i: (i, 0)),
                   pl.BlockSpec((3, 8, A_W), lambda i: (0, 0, 0)),
                   pl.BlockSpec((B_CONV, 8, A_W), lambda i: (0, 0, 0)),
                   pl.BlockSpec((3, 8, A_W), lambda i: (0, 0, 0))],
        out_shape=[jax.ShapeDtypeStruct((R, 3 * A_W), F32), jax.ShapeDtypeStruct((3, 8, A_W), F32),
                   jax.ShapeDtypeStruct((B_CONV, 8, A_W), F32), jax.ShapeDtypeStruct((3, 8, A_W), F32)],
        scratch_shapes=[pltpu.VMEM((TMR + 2 * HALO, A_W), F32), pltpu.VMEM((TMR, A_W), F32)],
        compiler_params=_params(48))(dyab, p, p, p, wa, wb, bias, lng, lnb)


def convmix_bwd2(dmid, p, wa, wb, T):
    R, wp = p.shape

    def body(mp_ref, m_ref, mn_ref, p_ref, wa_ref, wb_ref, dp_ref):
        mask = _ext_mask(pl.program_id(0), T, R)
        mrefs = (mp_ref, m_ref, mn_ref)
        dp_ref[:, 0:A_W] = m_ref[:, 0:A_W].astype(BF16)
        dca = jnp.where(mask, _ext(mrefs, A_W, A_W), 0.0)
        dcin = (wa_ref[pl.ds(0, 1), :] * _at(dca, 1) + wa_ref[pl.ds(1, 1), :] * _at(dca, 0)
                + wa_ref[pl.ds(2, 1), :] * _at(dca, -1))
        dp_ref[:, A_W:2 * A_W] = (dcin * p_ref[:, 2 * A_W:3 * A_W].astype(F32)).astype(BF16)
        dp_ref[:, 2 * A_W:3 * A_W] = (dcin * p_ref[:, A_W:2 * A_W].astype(F32)).astype(BF16)
        for cc in range(A_W // _CW):
            c0 = cc * _CW
            d = jnp.where(mask, _ext(mrefs, 2 * A_W + c0, _CW), 0.0)
            dub = jnp.zeros((TMR, _CW), F32)
            for k in range(B_CONV):
                dub = dub + wb_ref[pl.ds(k, 1), c0:c0 + _CW] * _at(d, B_CONV // 2 - k)
            vb = p_ref[:, 3 * A_W + c0:3 * A_W + c0 + _CW].astype(F32)
            s = _sigmoid(p_ref[:, 4 * A_W + c0:4 * A_W + c0 + _CW].astype(F32))
            dp_ref[:, 3 * A_W + c0:3 * A_W + c0 + _CW] = (dub * s).astype(BF16)
            dp_ref[:, 4 * A_W + c0:4 * A_W + c0 + _CW] = (dub * vb * s * (1.0 - s)).astype(BF16)
    return pl.pallas_call(
        body, name="convmix_bwd2", grid=(R // TMR,),
        in_specs=[*_halo_specs(3 * A_W, R), pl.BlockSpec((TMR, wp), lambda i: (i, 0)),
                  pl.BlockSpec((3, A_W), lambda i: (0, 0)), pl.BlockSpec((B_CONV, A_W), lambda i: (0, 0))],
        out_specs=pl.BlockSpec((TMR, wp), lambda i: (i, 0)),
        out_shape=jax.ShapeDtypeStruct((R, wp), BF16), compiler_params=_params(48))(dmid, dmid, dmid, p, wa, wb)


def _rot_half(v):
    w = v.shape[-1]
    lane = lax.broadcasted_iota(jnp.int32, (1, w), 1)
    return jnp.where(lane % HEAD_DIM < HEAD_DIM // 2, pltpu.roll(v, w - HEAD_DIM // 2, 1),
                     pltpu.roll(v, HEAD_DIM // 2, 1))


def rope_fwd(qkv, cs, sn):
    R, wq = qkv.shape
    qw = N_HEADS * HEAD_DIM
    kw = (wq - qw) // 2
    scale = HEAD_DIM ** -0.5

    def body(x_ref, cs_ref, sn_ref, o_ref):
        c, s = cs_ref[...], sn_ref[...]
        q = x_ref[:, 0:qw]
        o_ref[:, 0:qw] = ((q * jnp.tile(c, (1, qw // 128)) + _rot_half(q) * jnp.tile(s, (1, qw // 128)))
                          * scale).astype(BF16)
        k = x_ref[:, qw:qw + kw]
        o_ref[:, qw:qw + kw] = (k * jnp.tile(c, (1, kw // 128))
                                + _rot_half(k) * jnp.tile(s, (1, kw // 128))).astype(BF16)
        o_ref[:, qw + kw:] = x_ref[:, qw + kw:].astype(BF16)
    tab = pl.BlockSpec((TMR, 128), lambda i: (i, 0))
    return pl.pallas_call(
        body, name="rope_fwd", grid=(R // TMR,),
        in_specs=[pl.BlockSpec((TMR, wq), lambda i: (i, 0)), tab, tab],
        out_specs=pl.BlockSpec((TMR, wq), lambda i: (i, 0)),
        out_shape=jax.ShapeDtypeStruct((R, wq), BF16))(qkv, cs, sn)


def rope_bwd(dq, dks, dvs, dkc, dvc, cs, sn, T):
    R, qw = dq.shape
    kw = dkc.shape[1]
    nb = R // QB
    nl = T // QB
    scale = HEAD_DIM ** -0.5

    def body(dq_ref, kp_ref, ko_ref, kn_ref, vp_ref, vo_ref, vn_ref, kc_ref, vc_ref, cs_ref, sn_ref, o_ref):
        b = pl.program_id(0)
        c, s = cs_ref[...], sn_ref[...]
        has_next = (b + 1 < nb).astype(F32)
        has_prev = (b >= 1).astype(F32)
        is_ctx = (b >= nl).astype(F32)
        g = dq_ref[...] * scale
        o_ref[:, 0:qw] = (g * jnp.tile(c, (1, qw // 128)) + _rot_half(g * jnp.tile(s, (1, qw // 128)))).astype(BF16)
        g = ko_ref[...] + kp_ref[...] * has_next + kn_ref[...] * has_prev + kc_ref[...] * is_ctx
        o_ref[:, qw:qw + kw] = (g * jnp.tile(c, (1, kw // 128))
                                + _rot_half(g * jnp.tile(s, (1, kw // 128)))).astype(BF16)
        o_ref[:, qw + kw:] = (vo_ref[...] + vp_ref[...] * has_next + vn_ref[...] * has_prev
                              + vc_ref[...] * is_ctx).astype(BF16)
    own = pl.BlockSpec((QB, kw), lambda b: (b, 0))
    from_next = pl.BlockSpec((QB, kw), lambda b: (jnp.minimum(b + 1, nb - 1), 0))
    from_prev = pl.BlockSpec((QB, kw), lambda b: (jnp.maximum(b - 1, 0), 0))
    ctx = pl.BlockSpec((QB, kw), lambda b: (jnp.maximum(b - nl, 0), 0))
    tab = pl.BlockSpec((QB, 128), lambda b: (b, 0))
    return pl.pallas_call(
        body, name="rope_bwd", grid=(nb,),
        in_specs=[pl.BlockSpec((QB, qw), lambda b: (b, 0)), from_next, own, from_prev, from_next, own, from_prev,
                  ctx, ctx, tab, tab],
        out_specs=pl.BlockSpec((QB, qw + 2 * kw), lambda b: (b, 0)),
        out_shape=jax.ShapeDtypeStruct((R, qw + 2 * kw), BF16))(
            dq, dks[0], dks[1], dks[2], dvs[0], dvs[1], dvs[2], dkc, dvc, cs, sn)


def _attn_specs(T, R):
    nl = T // QB
    qcols = N_HEADS * HEAD_DIM // 128
    kcols = 2

    def band(col0, shift):
        return pl.BlockSpec((QB, 128), lambda jj, b: (jnp.clip(b + shift, 0, nl - 1), col0 + jj))

    def ctx(col0):
        return pl.BlockSpec((R - T, 128), lambda jj, b: (T // (R - T), col0 + jj))
    q = pl.BlockSpec((QB, 512), lambda jj, b: (b, jj))
    k0, v0 = qcols, qcols + kcols
    return q, [band(k0, -1), band(k0, 0), band(k0, 1), ctx(k0)], [band(v0, -1), band(v0, 0), band(v0, 1), ctx(v0)]


def _attn_common(T, R):
    nl = T // QB
    nk = 3 * QB + (R - T)

    def low_lanes():
        return lax.broadcasted_iota(jnp.int32, (1, 128), 1) < HEAD_DIM

    def dup(v, par):
        low = low_lanes()
        vf = v.astype(F32)
        r = pltpu.roll(vf, HEAD_DIM, 1)
        return (jnp.where(low, vf, r) if par == 0 else jnp.where(low, r, vf)).astype(BF16)

    def stack(ref, par):
        low = low_lanes()
        pa = ref[:, (2 * par) * 128:(2 * par + 1) * 128].astype(BF16)
        pb = ref[:, (2 * par + 1) * 128:(2 * par + 2) * 128].astype(BF16)
        zero = jnp.zeros_like(pa)
        return jnp.concatenate([jnp.where(low, pa, zero), jnp.where(low, zero, pa),
                                jnp.where(low, pb, zero), jnp.where(low, zero, pb)], axis=0)

    def unstack(v):
        low = low_lanes()
        return (jnp.where(low, v[0:QB], v[QB:2 * QB]), jnp.where(low, v[2 * QB:3 * QB], v[3 * QB:4 * QB]))

    def mask_of(b):
        col = lax.broadcasted_iota(jnp.int32, (4 * QB, nk), 1)
        row = lax.broadcasted_iota(jnp.int32, (4 * QB, nk), 0) & (QB - 1)
        qpos = b * QB + row
        kpos = (b - 1) * QB + col
        in_band = (jnp.abs(qpos - kpos) <= WINDOW) & (kpos >= 0) & (kpos < T) & (b < nl)
        return in_band | (col >= 3 * QB)

    def sink_col(sink_ref, first):
        blk = lax.broadcasted_iota(jnp.int32, (4 * QB, 1), 0) // QB
        out = jnp.zeros((4 * QB, 1), F32) + sink_ref[first]
        for h in range(1, 4):
            out = jnp.where(blk == h, sink_ref[first + h], out)
        return out

    def scores(qs, kd, mask, sink):
        s = lax.dot_general(qs, kd, (((1,), (1,)), ((), ())), preferred_element_type=F32)
        s = jnp.where(mask, s, NEG_INF)
        m = jnp.maximum(jnp.max(s, axis=-1, keepdims=True), sink)
        e = jnp.exp(s - m)
        es = jnp.exp(sink - m)
        return e, es, 1.0 / (jnp.sum(e, axis=-1, keepdims=True) + es)
    return low_lanes, dup, stack, unstack, mask_of, sink_col, scores


def attn_fwd(qkvr, sinks, T):
    R = qkvr.shape[0]
    qspec, kspecs, vspecs = _attn_specs(T, R)
    _, dup, stack, unstack, mask_of, sink_col, scores = _attn_common(T, R)

    def body(q_ref, kp, ko, kn, kc, vp, vo, vn, vc, sink_ref, o_ref):
        jj, b = pl.program_id(0), pl.program_id(1)
        mask = mask_of(b)
        k_all = jnp.concatenate([kp[...], ko[...], kn[...], kc[...]], axis=0)
        v_all = jnp.concatenate([vp[...], vo[...], vn[...], vc[...]], axis=0)
        for par in range(2):
            kd, vd = dup(k_all, par), dup(v_all, par)
            e, _, rz = scores(stack(q_ref, par), kd, mask, sink_col(sink_ref, jj * 8 + par * 4))
            o = jnp.dot((e * rz).astype(BF16), vd, preferred_element_type=F32)
            pa, pb = unstack(o)
            o_ref[:, (2 * par) * 128:(2 * par + 1) * 128] = pa.astype(BF16)
            o_ref[:, (2 * par + 1) * 128:(2 * par + 2) * 128] = pb.astype(BF16)
    return pl.pallas_call(
        body, name="attn_fwd", grid=(2, R // QB),
        in_specs=[qspec, *kspecs, *vspecs, pl.BlockSpec(memory_space=pltpu.SMEM)],
        out_specs=pl.BlockSpec((QB, 512), lambda jj, b: (b, jj)),
        out_shape=jax.ShapeDtypeStruct((R, N_HEADS * HEAD_DIM), BF16), compiler_params=_params(48))(
            qkvr, *([qkvr] * 8), sinks)


def attn_bwd(qkvr, do, sinks, T):
    R = qkvr.shape[0]
    tc = R - T
    qspec, kspecs, vspecs = _attn_specs(T, R)
    low_lanes, dup, stack, unstack, mask_of, sink_col, scores = _attn_common(T, R)
    contract_rows = (((0,), (0,)), ((), ()))
    contract_last = (((1,), (1,)), ((), ()))

    def body(q_ref, kp, ko, kn, kc, vp, vo, vn, vc, do_ref, sink_ref,
             dq_ref, dkp, dko, dkn, dvp, dvo, dvn, dkc_ref, dvc_ref, dsink_ref):
        jj, b = pl.program_id(0), pl.program_id(1)

        @pl.when((jj == 0) & (b == 0))
        def _():
            dsink_ref[...] = jnp.zeros_like(dsink_ref)

        @pl.when(b == 0)
        def _():
            dkc_ref[...] = jnp.zeros_like(dkc_ref)
            dvc_ref[...] = jnp.zeros_like(dvc_ref)
        mask = mask_of(b)
        k_all = jnp.concatenate([kp[...], ko[...], kn[...], kc[...]], axis=0)
        v_all = jnp.concatenate([vp[...], vo[...], vn[...], vc[...]], axis=0)
        lane = lax.broadcasted_iota(jnp.int32, (8, 128), 1)
        srow = lax.broadcasted_iota(jnp.int32, (8, 128), 0)
        dk_fold, dv_fold = [], []
        for par in range(2):
            kd, vd = dup(k_all, par), dup(v_all, par)
            first = jj * 8 + par * 4
            qs, dos = stack(q_ref, par), stack(do_ref, par)
            e, es, rz = scores(qs, kd, mask, sink_col(sink_ref, first))
            p = e * rz
            dp = lax.dot_general(dos, vd, contract_last, preferred_element_type=F32)
            delta = jnp.sum(p * dp, axis=-1, keepdims=True)
            ds = (p * (dp - delta)).astype(BF16)
            t = es * rz * delta
            for h in range(4):
                dsink = -jnp.sum(t[h * QB:(h + 1) * QB])
                dsink_ref[...] += jnp.where((lane == first + h) & (srow == 0), dsink, 0.0)
            pa, pb = unstack(jnp.dot(ds, kd, preferred_element_type=F32))
            dq_ref[:, (2 * par) * 128:(2 * par + 1) * 128] = pa
            dq_ref[:, (2 * par + 1) * 128:(2 * par + 2) * 128] = pb
            dk_t = lax.dot_general(qs, ds, contract_rows, preferred_element_type=F32)
            dv_t = lax.dot_general(dos, p.astype(BF16), contract_rows, preferred_element_type=F32)
            dk_fold.append(dk_t + pltpu.roll(dk_t, HEAD_DIM, 0))
            dv_fold.append(dv_t + pltpu.roll(dv_t, HEAD_DIM, 0))
        low_rows = lax.broadcasted_iota(jnp.int32, (128, 1), 0) < HEAD_DIM
        dk = jnp.where(low_rows, dk_fold[0], dk_fold[1]).T
        dv = jnp.where(low_rows, dv_fold[0], dv_fold[1]).T
        dkp[...], dko[...], dkn[...] = dk[0:QB], dk[QB:2 * QB], dk[2 * QB:3 * QB]
        dvp[...], dvo[...], dvn[...] = dv[0:QB], dv[QB:2 * QB], dv[2 * QB:3 * QB]
        dkc_ref[...] += dk[3 * QB:]
        dvc_ref[...] += dv[3 * QB:]
    blk = pl.BlockSpec((QB, 128), lambda jj, b: (b, jj))
    cblk = pl.BlockSpec((tc, 128), lambda jj, b: (0, jj))
    part = jax.ShapeDtypeStruct((R, 256), F32)
    csum = jax.ShapeDtypeStruct((tc, 256), F32)
    outs = pl.pallas_call(
        body, name="attn_bwd", grid=(2, R // QB),
        in_specs=[qspec, *kspecs, *vspecs, pl.BlockSpec((QB, 512), lambda jj, b: (b, jj)),
                  pl.BlockSpec(memory_space=pltpu.SMEM)],
        out_specs=[pl.BlockSpec((QB, 512), lambda jj, b: (b, jj)), blk, blk, blk, blk, blk, blk, cblk, cblk,
                   pl.BlockSpec((8, 128), lambda jj, b: (0, 0))],
        out_shape=[jax.ShapeDtypeStruct((R, N_HEADS * HEAD_DIM), F32), part, part, part, part, part, part,
                   csum, csum, jax.ShapeDtypeStruct((8, 128), F32)],
        compiler_params=_params(48))(qkvr, *([qkvr] * 8), do, sinks)
    return outs[0], outs[1:4], outs[4:7], outs[7], outs[8], outs[9]


def loss_head(x, nw, target, T):
    R, dm = x.shape
    nl = T // TMR

    def body(x_ref, nw_ref, t_ref, loss_ref, dx_ref, dnw_ref):
        i = pl.program_id(0)

        @pl.when(i == 0)
        def _():
            loss_ref[...] = jnp.zeros_like(loss_ref)
            dnw_ref[...] = jnp.zeros_like(dnw_ref)
        live = (i < nl).astype(F32)
        nwv = nw_ref[...]
        xv = x_ref[...]
        r = lax.rsqrt(jnp.mean(xv * xv, axis=-1, keepdims=True) + EPS)
        xh = xv * r
        err = xh * nwv - t_ref[...]
        per_row = jnp.mean(err * err, axis=-1, keepdims=True)
        loss_ref[...] += 0.5 * live * jnp.sum(per_row)
        dy = err * (live / dm)
        dnw_ref[...] += _colsum8(dy * xh)
        dxh = dy * nwv
        dx_ref[...] = r * (dxh - xh * jnp.mean(dxh * xh, axis=-1, keepdims=True))
    tile = pl.BlockSpec((TMR, dm), lambda i: (i, 0))
    return pl.pallas_call(
        body, name="loss_head", grid=(R // TMR,),
        in_specs=[tile, pl.BlockSpec((1, dm), lambda i: (0, 0)),
                  pl.BlockSpec((TMR, dm), lambda i: (jnp.minimum(i, nl - 1), 0))],
        out_specs=[pl.BlockSpec((8, 128), lambda i: (0, 0)), tile, pl.BlockSpec((8, dm), lambda i: (0, 0))],
        out_shape=[jax.ShapeDtypeStruct((8, 128), F32), jax.ShapeDtypeStruct((R, dm), F32),
                   jax.ShapeDtypeStruct((8, dm), F32)])(x, nw, target)


def adaln_fwd(cond, w_mod, b_mod):
    nl, dm, ns = w_mod.shape

    def body(c_ref, w_ref, b_ref, o_ref):
        cv = c_ref[...]
        s = (cv * _sigmoid(cv)).astype(BF16)
        o_ref[...] = jnp.dot(s, w_ref[...].astype(BF16), preferred_element_type=F32) + b_ref[...]
    return pl.pallas_call(
        body, name="adaln_fwd", grid=(nl,),
        in_specs=[pl.BlockSpec((16, dm), lambda l: (0, 0)), pl.BlockSpec((None, dm, ns), lambda l: (l, 0, 0)),
                  pl.BlockSpec((None, 1, ns), lambda l: (l, 0, 0))],
        out_specs=pl.BlockSpec((None, 16, ns), lambda l: (l, 0, 0)),
        out_shape=jax.ShapeDtypeStruct((nl, 16, ns), F32), compiler_params=_params(48))(cond, w_mod, b_mod)


def adaln_bwd(cond, dmod, w_mod):
    nl, dm, ns = w_mod.shape

    def body(c_ref, d_ref, w_ref, gw_ref, ds_ref):
        l = pl.program_id(0)

        @pl.when(l == 0)
        def _():
            ds_ref[...] = jnp.zeros_like(ds_ref)
        cv = c_ref[...]
        s = (cv * _sigmoid(cv)).astype(BF16)
        dv = d_ref[...].astype(BF16)
        gw_ref[...] = lax.dot_general(s, dv, (((0,), (0,)), ((), ())), preferred_element_type=F32)
        ds_ref[...] += lax.dot_general(dv, w_ref[...].astype(BF16), (((1,), (1,)), ((), ())),
                                       preferred_element_type=F32)
    return pl.pallas_call(
        body, name="adaln_bwd", grid=(nl,),
        in_specs=[pl.BlockSpec((16, dm), lambda l: (0, 0)), pl.BlockSpec((None, 16, ns), lambda l: (l, 0, 0)),
                  pl.BlockSpec((None, dm, ns), lambda l: (l, 0, 0))],
        out_specs=[pl.BlockSpec((None, dm, ns), lambda l: (l, 0, 0)), pl.BlockSpec((16, dm), lambda l: (0, 0))],
        out_shape=[jax.ShapeDtypeStruct((nl, dm, ns), F32), jax.ShapeDtypeStruct((16, dm), F32)],
        compiler_params=_params(48))(cond, dmod, w_mod)


def _me():
    return lax.axis_index("x"), lax.axis_index("y"), lax.axis_index("c")


def allgather8(block):
    m_per, n = block.shape

    def body(x_ref, out_ref, send_sems, recv_sems, local_sem):
        x, y, c = _me()
        me, sibling = (x, y, c), (x, y, 1 - c)
        chips = [(1 - x, y), (x, 1 - y), (1 - x, 1 - y)]

        def rows(px, py, pc):
            return out_ref.at[pl.ds((4 * px + 2 * py + pc) * m_per, m_per), :]

        def copy(k, blk, to, src=None):
            return pltpu.make_async_remote_copy(
                src_ref=rows(*blk) if src is None else src, dst_ref=rows(*blk),
                send_sem=send_sems.at[k], recv_sem=recv_sems.at[k], device_id=to, device_id_type=MESH)
        mine = pltpu.make_async_copy(x_ref, rows(*me), local_sem)
        mine.start()
        first = [copy(0, me, sibling, src=x_ref)]
        first += [copy(1 + j, me, (*chip, c), src=x_ref) for j, chip in enumerate(chips)]
        for cp in first:
            cp.start()
        passed = [copy(4 + j, (*chip, c), sibling) for j, chip in enumerate(chips)]
        for j, chip in enumerate(chips):
            copy(1 + j, (*chip, c), me).wait_recv()
            passed[j].start()
        copy(0, sibling, me).wait_recv()
        for j, chip in enumerate(chips):
            copy(4 + j, (*chip, 1 - c), me).wait_recv()
        for cp in first + passed:
            cp.wait_send()
        mine.wait()
    return pl.pallas_call(
        body, name="allgather8",
        out_shape=jax.ShapeDtypeStruct((N_DEV * m_per, n), block.dtype),
        in_specs=[pl.BlockSpec(memory_space=pltpu.VMEM)],
        out_specs=pl.BlockSpec(memory_space=pltpu.VMEM),
        scratch_shapes=[pltpu.SemaphoreType.DMA((7,)), pltpu.SemaphoreType.DMA((7,)), pltpu.SemaphoreType.DMA],
        compiler_params=_params(48))(block)


def _other_chips(x, y):
    return [(1 - x, y), (x, 1 - y), (1 - x, 1 - y)]


_HBM = pl.BlockSpec(memory_space=pltpu.HBM)
_SEM = pl.BlockSpec(memory_space=pltpu.SEMAPHORE)
_ANY = pl.BlockSpec(memory_space=pl.ANY)
_EFFECT = pltpu.SideEffectType.DATAFLOW_SIDE_EFFECTING


def _in_hbm(v):
    return pltpu.with_memory_space_constraint(v, pltpu.HBM)


def cast_into_slot(w, chip_id):
    kb, nb = w.shape
    tr = _row_tile(kb)

    def body(chip_ref, w_ref, o_ref):
        del chip_ref
        o_ref[...] = w_ref[...].astype(BF16)
    return pl.pallas_call(
        body, name="cast_into_slot",
        grid_spec=pltpu.PrefetchScalarGridSpec(
            num_scalar_prefetch=1, grid=(kb // tr,),
            in_specs=[pl.BlockSpec((tr, nb), lambda i, chip: (i, 0))],
            out_specs=pl.BlockSpec((None, tr, nb), lambda i, chip: (chip[0], i, 0))),
        out_shape=jax.ShapeDtypeStruct((N_CHIP, kb, nb), BF16))(chip_id, w)


def _split_copies(mode, srcs, lands, send_sems, recv_sems):
    x, y, c = _me()
    out = []
    for t in range(len(lands)):
        for k, chip in enumerate(_other_chips(x, y)):
            if mode == "gather":
                src = dst = lands[t].at[2 * x + y]
                landed = lands[t].at[2 * chip[0] + chip[1]]
            else:
                src, dst, landed = srcs[t].at[2 * chip[0] + chip[1]], lands[t].at[k], lands[t].at[k]
            send = pltpu.make_async_remote_copy(src_ref=src, dst_ref=dst, send_sem=send_sems.at[3 * t + k],
                                                recv_sem=recv_sems.at[3 * t + k], device_id=(*chip, c),
                                                device_id_type=MESH)
            recv = pltpu.make_async_remote_copy(src_ref=src, dst_ref=landed, send_sem=send_sems.at[3 * t + k],
                                                recv_sem=recv_sems.at[3 * t + k], device_id=(*chip, c),
                                                device_id_type=MESH)
            out.append((send, recv))
    return out


def exchange_start(name, mode, srcs, lands, after):
    ns, nl = len(srcs), len(lands)
    na = ns + nl

    def body(*refs):
        src_refs, land_refs = refs[:ns], refs[ns:na]
        send_sems, recv_sems = refs[na + 1], refs[na + 2]
        token = refs[-1]
        for send, _ in _split_copies(mode, src_refs, land_refs, send_sems, recv_sems):
            send.start()
        token[...] = jnp.zeros_like(token)
    arrays = list(srcs) + list(lands)
    outs = pl.pallas_call(
        body, name=name,
        out_shape=(pltpu.SemaphoreType.DMA((3 * nl,)), pltpu.SemaphoreType.DMA((3 * nl,)),
                   *[pltpu.HBM(v.shape, v.dtype) for v in arrays], jax.ShapeDtypeStruct((8, 128), F32)),
        in_specs=[_HBM] * na + [_ANY],
        out_specs=(_SEM, _SEM, *[_HBM] * na, pl.BlockSpec(memory_space=pltpu.VMEM)),
        input_output_aliases={i: 2 + i for i in range(na)},
        compiler_params=pltpu.CompilerParams(has_side_effects=_EFFECT))(*[_in_hbm(v) for v in arrays], after)
    return outs[0], outs[1], list(outs[2:2 + ns]), list(outs[2 + ns:2 + na]), outs[-1]


def exchange_wait(name, mode, send_sems, recv_sems, srcs, lands, after):
    ns, nl = len(srcs), len(lands)
    na = ns + nl

    def body(*refs):
        for _, recv in _split_copies(mode, refs[:ns], refs[ns:na], refs[na], refs[na + 1]):
            recv.wait_send()
            recv.wait_recv()
    arrays = list(srcs) + list(lands)
    outs = pl.pallas_call(
        body, name=name,
        out_shape=[pltpu.HBM(v.shape, v.dtype) for v in arrays],
        in_specs=[_HBM] * na + [_SEM, _SEM, _ANY], out_specs=[_HBM] * na,
        input_output_aliases={i: i for i in range(na)},
        compiler_params=pltpu.CompilerParams(has_side_effects=_EFFECT))(*arrays, send_sems, recv_sems, after)
    return list(outs[:ns]), list(outs[ns:])


def swap_with_sibling(v):
    def body(v_ref, out_ref, send_sem, recv_sem):
        x, y, c = _me()
        cp = pltpu.make_async_remote_copy(src_ref=v_ref, dst_ref=out_ref, send_sem=send_sem, recv_sem=recv_sem,
                                          device_id=(x, y, 1 - c), device_id_type=MESH)
        cp.start()
        cp.wait()
    return pl.pallas_call(
        body, name="swap_with_sibling", out_shape=jax.ShapeDtypeStruct(v.shape, v.dtype),
        in_specs=[pl.BlockSpec(memory_space=pl.ANY)], out_specs=pl.BlockSpec(memory_space=pl.ANY),
        scratch_shapes=[pltpu.SemaphoreType.DMA, pltpu.SemaphoreType.DMA])(v)


def sum_slots(parts):
    n, rows, w = parts.shape
    tr = _row_tile(rows)

    def body(p_ref, o_ref):
        acc = p_ref[0].astype(F32)
        for k in range(1, n):
            acc = acc + p_ref[k].astype(F32)
        o_ref[...] = acc
    return pl.pallas_call(
        body, name="sum_slots", grid=(rows // tr,),
        in_specs=[pl.BlockSpec((n, tr, w), lambda i: (0, i, 0))], out_specs=pl.BlockSpec((tr, w), lambda i: (i, 0)),
        out_shape=jax.ShapeDtypeStruct((rows, w), F32), compiler_params=_params(48))(parts)


def sum_landed(landed, own, chip_id, layer, n_layers, buf):
    n, rows, w = landed.shape
    tr = _row_tile(rows)
    base = layer * (rows // tr)

    def compute(l_ref, g_ref, o_ref):
        acc = g_ref[...].astype(F32)
        for k in range(n):
            acc = acc + l_ref[k].astype(F32)
        o_ref[...] = acc
    in_specs = [pl.BlockSpec((n, tr, w), lambda i, chip: (0, i, 0)),
                pl.BlockSpec((None, tr, w), lambda i, chip: (chip[0], i, 0))]
    out_spec = pl.BlockSpec((tr, w), lambda i, chip: (base + i, 0))
    out_shape = jax.ShapeDtypeStruct((n_layers * rows, w), F32)
    if buf is None:
        def body(chip_ref, l_ref, g_ref, o_ref):
            del chip_ref
            compute(l_ref, g_ref, o_ref)
        return pl.pallas_call(
            body, name="sum_landed",
            grid_spec=pltpu.PrefetchScalarGridSpec(num_scalar_prefetch=1, grid=(rows // tr,), in_specs=in_specs,
                                                   out_specs=out_spec),
            out_shape=out_shape, compiler_params=_params(48))(chip_id, landed, own)

    def body(chip_ref, l_ref, g_ref, buf_ref, o_ref):
        del chip_ref, buf_ref
        compute(l_ref, g_ref, o_ref)
    return pl.pallas_call(
        body, name="sum_landed_into",
        grid_spec=pltpu.PrefetchScalarGridSpec(num_scalar_prefetch=1, grid=(rows // tr,),
                                               in_specs=in_specs + [_ANY], out_specs=out_spec),
        out_shape=out_shape, input_output_aliases={3: 0}, compiler_params=_params(48))(chip_id, landed, own, buf)


def adamw(w, ga, gb, m, v):
    rows, wd = w.shape
    tr = min(_row_tile(rows), 128)
    c1 = 1.0 / (1.0 - ADAM_B1 ** ADAM_STEP)
    c2 = 1.0 / (1.0 - ADAM_B2 ** ADAM_STEP)

    def update(wv, g, mv, vv, g_ref, d_ref, m_ref, v_ref):
        mn = ADAM_B1 * mv + (1.0 - ADAM_B1) * g
        vn = ADAM_B2 * vv + (1.0 - ADAM_B2) * (g * g)
        g_ref[...] = g
        m_ref[...] = mn
        v_ref[...] = vn
        d_ref[...] = -ADAM_LR * ((mn * c1) / (jnp.sqrt(vn * c2) + ADAM_EPS) + ADAM_WD * wv)
    tile = pl.BlockSpec((tr, wd), lambda i: (i, 0))
    out = jax.ShapeDtypeStruct((rows, wd), F32)
    if gb is None:
        def body(w_ref, ga_ref, m_ref, v_ref, g_out, d_out, m_out, v_out):
            update(w_ref[...], ga_ref[...], m_ref[...], v_ref[...], g_out, d_out, m_out, v_out)
        return pl.pallas_call(body, name="adamw", grid=(rows // tr,), in_specs=[tile] * 4,
                              out_specs=[tile] * 4, out_shape=[out] * 4)(w, ga, m, v)

    def body(w_ref, ga_ref, gb_ref, m_ref, v_ref, g_out, d_out, m_out, v_out):
        update(w_ref[...], ga_ref[...] + gb_ref[...], m_ref[...], v_ref[...], g_out, d_out, m_out, v_out)
    return pl.pallas_call(body, name="adamw_sum", grid=(rows // tr,), in_specs=[tile] * 5,
                          out_specs=[tile] * 4, out_shape=[out] * 4)(w, ga, gb, m, v)


def _rope_tables(T, R):
    rows = T // GRID_W
    row = jnp.repeat(jnp.arange(rows), GRID_W).astype(F32)
    col = jnp.tile(jnp.arange(GRID_W), rows).astype(F32)
    n_freq = HEAD_DIM // 4
    inv_freq = ROPE_THETA ** (-jnp.arange(n_freq, dtype=F32) / n_freq)
    ang = jnp.concatenate([row[:, None] * inv_freq, col[:, None] * inv_freq], axis=-1)
    cos, sin = jnp.cos(ang), jnp.sin(ang)
    cs = jnp.tile(cos, (1, 4))
    sn = jnp.tile(jnp.concatenate([-sin, sin], axis=-1), (1, 2))
    pad = R - T
    return (jnp.concatenate([cs, jnp.ones((pad, 128), F32)], axis=0),
            jnp.concatenate([sn, jnp.zeros((pad, 128), F32)], axis=0))


def _pack(parts, mult=8 * 128):
    flat = jnp.concatenate([p.reshape(-1).astype(F32) for p in parts])
    pad = (-flat.shape[0]) % mult
    return jnp.pad(flat, (0, pad)).reshape(-1, 128)


def _unpack(buf, shapes):
    flat = buf.reshape(-1)
    out, o = [], 0
    for s in shapes:
        n = 1
        for d in s:
            n *= d
        out.append(flat[o:o + n].reshape(s))
        o += n
    return out


def kernel(x, c, ctx, c_ctx, w_mod, b_mod, norm_mix, norm_ffn, w_in_ab, conv_a, conv_b, conv_b_bias, ln_b_gain, ln_b_bias, w_out_ab, w_qkv, w_o, sinks, w_up, w_conv_ffn, w_down, final_norm, loss_target, m_c_ctx, m_w_mod, m_b_mod, m_norm_mix, m_norm_ffn, m_w_in_ab, m_conv_a, m_conv_b, m_conv_b_bias, m_ln_b_gain, m_ln_b_bias, m_w_out_ab, m_w_qkv, m_w_o, m_sinks, m_w_up, m_w_conv_ffn, m_w_down, m_final_norm, v_c_ctx, v_w_mod, v_b_mod, v_norm_mix, v_norm_ffn, v_w_in_ab, v_conv_a, v_conv_b, v_conv_b_bias, v_ln_b_gain, v_ln_b_bias, v_w_out_ab, v_w_qkv, v_w_o, v_sinks, v_w_up, v_w_conv_ffn, v_w_down, v_final_norm):
    T, dm = x.shape[1], x.shape[2]
    tc = ctx.shape[1]
    R = T + tc
    depth = w_mod.shape[0]
    ax, ay, ac = lax.axis_index("x"), lax.axis_index("y"), lax.axis_index("c")
    chip = 2 * ax + ay
    dev = 4 * ax + 2 * ay + ac

    small_w = [conv_a, conv_b, w_conv_ffn]
    gathered = allgather8(_pack([c] + small_w)).reshape(N_DEV, -1)
    cond8 = gathered[:, :dm]
    off = dm
    full_small = []
    for wsh in small_w:
        n = wsh.size
        per_chip = gathered[0::2, off:off + n].reshape((N_CHIP,) + wsh.shape)
        full_small.append(jnp.concatenate([per_chip[q] for q in range(N_CHIP)], axis=-1))
        off += n
    conv_a_f, conv_b_f, w_conv_ffn_f = full_small
    cond = jnp.concatenate([cond8, c_ctx[None, :], jnp.zeros((7, dm), F32)], axis=0)

    ns_mod = w_mod.shape[2]
    b_mod_sh = lax.dynamic_slice_in_dim(b_mod, chip * ns_mod, ns_mod, axis=1)[:, None, :]
    mod_sh = adaln_fwd(cond, w_mod, b_mod_sh)
    mod_all = allgather8(mod_sh.reshape(depth * 16, ns_mod)).reshape(N_DEV, depth, 16, ns_mod)
    mod_full = jnp.concatenate([mod_all[2 * q] for q in range(N_CHIP)], axis=-1)
    mine = lax.dynamic_index_in_dim(mod_full, dev, axis=1, keepdims=False)
    mods = jnp.stack([mine, mod_full[:, 8]], axis=1).reshape(depth, 2, 6, dm)

    masters = {"w_in_ab": w_in_ab, "w_out_ab": w_out_ab, "w_qkv": w_qkv, "w_o": w_o, "w_up": w_up, "w_down": w_down}
    chip_id = chip.astype(jnp.int32).reshape(1)

    def half_weights(l, half):
        if half == 1:
            return [("w_up", l), ("w_down", l)]
        return [("w_in_ab", l // 2), ("w_out_ab", l // 2)] if l % 2 == 0 else [("w_qkv", l // 2), ("w_o", l // 2)]
    in_flight, after = {}, mods
    for l in range(depth):
        for half in range(2):
            lands = [cast_into_slot(masters[n][j], chip_id) for n, j in half_weights(l, half)]
            send_sems, recv_sems, _, lands, after = exchange_start(f"gather_start_{l}_{half}", "gather", [], lands, after)
            in_flight[l, half] = (send_sems, recv_sems, lands)
    mods = mods + after[0, 0]

    def gathered_weights(l, half, after):
        send_sems, recv_sems, lands = in_flight[l, half]
        _, landed = exchange_wait(f"gather_wait_{l}_{half}", "gather", send_sems, recv_sems, [], lands, after)
        return dict(zip([n for n, _ in half_weights(l, half)], landed))

    cs, sn = _rope_tables(T, R)
    sinks_flat = sinks.reshape(-1)

    xs = jnp.concatenate([x[0], ctx[0]], axis=0)
    saved, W = [], []
    h1 = norm_mod_fwd(xs, norm_mix[0][None], mods[0], 0, T)
    for l in range(depth):
        e = l // 2
        wl = gathered_weights(l, 0, h1)
        W.append(wl)
        s = {"x0": xs, "h1": h1}
        if l % 2 == 0:
            p = mm_nn(h1, wl["w_in_ab"], "col", BF16)
            yab = convmix_fwd(p, conv_a_f[e], conv_b_f[e], conv_b_bias[e][None], ln_b_gain[e][None],
                              ln_b_bias[e][None], T)
            y1 = mm_nn(yab, wl["w_out_ab"], "row", F32)
            s.update(p=p, mix=yab)
        else:
            qkv = mm_nn(h1, wl["w_qkv"], "col", F32)
            qkvr = rope_fwd(qkv, cs, sn)
            att = attn_fwd(qkvr, sinks_flat[e * N_HEADS:(e + 1) * N_HEADS], T)
            y1 = mm_nn(att, wl["w_o"], "row", F32)
            s.update(qkvr=qkvr, mix=att)
        x1, h2 = resid_norm_fwd(xs, y1, norm_ffn[l][None], mods[l], mods[l], 2, 3, T)
        wl.update(gathered_weights(l, 1, h2))
        u = mm_nn(h2, wl["w_up"], "col", BF16)
        z = ffnconv_fwd(u, w_conv_ffn_f[l], T)
        y2 = mm_nn(z, wl["w_down"], "row", F32)
        if l + 1 < depth:
            xs, h1 = resid_norm_fwd(x1, y2, norm_mix[l + 1][None], mods[l], mods[l + 1], 5, 0, T)
        else:
            xs = resid_fwd(x1, y2, mods[l], 5, T)
        s.update(y1=y1, x1=x1, h2=h2, u=u, z=z, y2=y2)
        saved.append(s)

    loss_part, dx, d_final = loss_head(xs, final_norm[None], loss_target[0], T)
    loss = lax.psum(loss_part[0, 0], ("x", "y", "c"))

    d_mods, d_norm_mix, d_norm_ffn = [None] * depth, [None] * depth, [None] * depth
    d_conv_a, d_conv_b, d_vecs, d_sinks, d_wc = [None] * 2, [None] * 2, [None] * 2, [None] * 2, [None] * depth
    dss1, dss2, dg1, dg2 = [None] * depth, [None] * depth, [None] * depth, [None] * depth
    scattering = {}

    def scatter(l, half, G, after):
        grads_h = [G[n] for n, _ in half_weights(l, half)]
        lands = [lax.empty((N_CHIP - 1, *g.shape[1:]), g.dtype) for g in grads_h]
        send_sems, recv_sems, grads_h, lands, token = exchange_start(
            f"scatter_start_{l}_{half}", "scatter", grads_h, lands, after)
        scattering[l, half] = (send_sems, recv_sems, grads_h, lands)
        return token

    dy2, dg2[depth - 1] = resid_bwd(dx, saved[depth - 1]["y2"], mods[depth - 1], 5, T)
    for l in reversed(range(depth)):
        e = l // 2
        s, wl = saved[l], W[l]
        G = {}
        G["w_down"] = mm_tn(s["z"], dy2, "row", wl["w_down"])
        dz = mm_nt(dy2, wl["w_down"], "row", BF16)
        duc, d_wc[l] = ffnconv_bwd1(dz, s["u"], w_conv_ffn_f[l], T)
        du = ffnconv_bwd2(duc, w_conv_ffn_f[l], T)
        G["w_up"] = mm_tn(s["h2"], du, "col", wl["w_up"])
        dh2 = mm_nt(du, wl["w_up"], "col", F32)
        token = scatter(l, 1, G, dh2)
        mod_l = mods[l] + token[0, 0]
        dx, dy1, dss2[l], d_norm_ffn[l], dg1[l] = norm_resid_bwd(
            dh2, s["x1"], norm_ffn[l][None], mod_l, dx, s["y1"], mod_l, 3, 2, T)
        if l % 2 == 0:
            G["w_out_ab"] = mm_tn(s["mix"], dy1, "row", wl["w_out_ab"])
            dyab = mm_nt(dy1, wl["w_out_ab"], "row", F32)
            dmid, d_conv_a[e], d_conv_b[e], d_vecs[e] = convmix_bwd1(
                dyab, s["p"], conv_a_f[e], conv_b_f[e], conv_b_bias[e][None], ln_b_gain[e][None],
                ln_b_bias[e][None], T)
            dp = convmix_bwd2(dmid, s["p"], conv_a_f[e], conv_b_f[e], T)
            G["w_in_ab"] = mm_tn(s["h1"], dp, "col", wl["w_in_ab"])
            dh1 = mm_nt(dp, wl["w_in_ab"], "col", F32)
        else:
            G["w_o"] = mm_tn(s["mix"], dy1, "row", wl["w_o"])
            datt = mm_nt(dy1, wl["w_o"], "row", BF16)
            dq, dks, dvs, dkc, dvc, d_sinks[e] = attn_bwd(
                s["qkvr"], datt, sinks_flat[e * N_HEADS:(e + 1) * N_HEADS], T)
            dqkv = rope_bwd(dq, dks, dvs, dkc, dvc, cs, sn, T)
            G["w_qkv"] = mm_tn(s["h1"], dqkv, "col", wl["w_qkv"])
            dh1 = mm_nt(dqkv, wl["w_qkv"], "col", F32)
        token = scatter(l, 0, G, dh1)
        mod_l = mods[l] + token[0, 0]
        if l > 0:
            dx, dy2, dss1[l], d_norm_mix[l], dg2[l - 1] = norm_resid_bwd(
                dh1, s["x0"], norm_mix[l][None], mod_l, dx, saved[l - 1]["y2"], mods[l - 1], 0, 5, T)
        else:
            dx, dss1[l], d_norm_mix[l] = norm_mod_bwd(dh1, s["x0"], norm_mix[l][None], mod_l, dx, 0, T)
    grad_x = dx[:T][None]
    for l in range(depth):
        a1, a2 = dss1[l].sum(2), dss2[l].sum(2)
        d_mods[l] = jnp.stack([a1[:, 0], a1[:, 1], dg1[l].sum(1), a2[:, 0], a2[:, 1], dg2[l].sum(1)], axis=1)

    d_mods = jnp.stack(d_mods)
    summed_parts = [
        d_mods[:, 1],
        jnp.stack(d_norm_mix).sum(1), jnp.stack(d_norm_ffn).sum(1),
        jnp.stack(d_conv_a).sum(2), jnp.stack(d_conv_b).sum(2),
        jnp.stack(d_vecs).sum(2),
        jnp.stack(d_sinks)[:, 0, :N_HEADS],
        jnp.stack(d_wc).sum(2), d_final.sum(0)]
    summed_shapes = [p.shape for p in summed_parts]
    n_own = depth * 6 * dm
    pack = _pack([d_mods[:, 0]] + summed_parts)
    parts = allgather8(pack).reshape(N_DEV, -1, 128)
    total = sum_slots(parts)
    own_rows = parts.reshape(N_DEV, -1)[:, :n_own].reshape(N_DEV, depth, 6 * dm)
    (dmod_ctx, g_norm_mix, g_norm_ffn, g_conv_a, g_conv_b, g_vecs, g_sinks, g_wc, g_final) = _unpack(
        total.reshape(-1)[n_own:], summed_shapes)
    dmod_rows = jnp.concatenate([jnp.moveaxis(own_rows, 0, 1), dmod_ctx.reshape(depth, 1, 6 * dm),
                                 jnp.zeros((depth, 7, 6 * dm), F32)], axis=1)
    g_b_mod = dmod_rows.sum(1)
    dmod_sh = lax.dynamic_slice_in_dim(dmod_rows, chip * ns_mod, ns_mod, axis=2)
    g_w_mod, dsilu = adaln_bwd(cond, dmod_sh, w_mod)
    dsilu_all = allgather8(dsilu[8:16]).reshape(N_DEV, 8, dm)
    dsilu_ctx = sum_slots(dsilu_all[0::2])[0]
    sg = jax.nn.sigmoid(c_ctx)
    g_c_ctx = dsilu_ctx * (sg * (1.0 + c_ctx * (1.0 - sg)))

    def shard_cols(full, width):
        return lax.dynamic_slice_in_dim(full, chip * width, width, axis=full.ndim - 1)
    g_conv_a_s = shard_cols(g_conv_a, conv_a.shape[-1])
    g_conv_b_s = shard_cols(g_conv_b, conv_b.shape[-1])
    g_wc_s = shard_cols(g_wc, w_conv_ffn.shape[-1])

    grads, deltas, new_m, new_v = {}, {}, {}, {}

    def step_2d(name, wv, ga, gb, mv, vv):
        shp = wv.shape
        r2 = lambda t: t.reshape(-1, shp[-1])
        g, d, mn, vn = adamw(r2(wv), r2(ga), None if gb is None else r2(gb), r2(mv), r2(vv))
        grads[name], deltas[name], new_m[name], new_v[name] = (t.reshape(shp) for t in (g, d, mn, vn))

    sums = {n: None for n in masters}
    for l in reversed(range(depth)):
        for half in (1, 0):
            send_sems, recv_sems, grads_h, lands = scattering[l, half]
            grads_h, landed = exchange_wait(f"scatter_wait_{l}_{half}", "scatter", send_sems, recv_sems, grads_h,
                                            lands, token)
            for (n, j), own, arr in zip(half_weights(l, half), grads_h, landed):
                sums[n] = sum_landed(arr, own, chip_id, j, masters[n].shape[0], sums[n])
    moments = {"w_in_ab": (m_w_in_ab, v_w_in_ab), "w_out_ab": (m_w_out_ab, v_w_out_ab),
               "w_qkv": (m_w_qkv, v_w_qkv), "w_o": (m_w_o, v_w_o), "w_up": (m_w_up, v_w_up),
               "w_down": (m_w_down, v_w_down)}
    for name, wv in masters.items():
        other = swap_with_sibling(sums[name])
        step_2d(name, wv, sums[name].reshape(wv.shape), other.reshape(wv.shape), *moments[name])
    step_2d("w_mod", w_mod, g_w_mod, None, m_w_mod, v_w_mod)

    small = [("c_ctx", c_ctx, g_c_ctx, m_c_ctx, v_c_ctx), ("b_mod", b_mod, g_b_mod, m_b_mod, v_b_mod),
             ("norm_mix", norm_mix, g_norm_mix, m_norm_mix, v_norm_mix),
             ("norm_ffn", norm_ffn, g_norm_ffn, m_norm_ffn, v_norm_ffn),
             ("conv_a", conv_a, g_conv_a_s, m_conv_a, v_conv_a), ("conv_b", conv_b, g_conv_b_s, m_conv_b, v_conv_b),
             ("conv_b_bias", conv_b_bias, g_vecs[:, 0], m_conv_b_bias, v_conv_b_bias),
             ("ln_b_gain", ln_b_gain, g_vecs[:, 1], m_ln_b_gain, v_ln_b_gain),
             ("ln_b_bias", ln_b_bias, g_vecs[:, 2], m_ln_b_bias, v_ln_b_bias),
             ("sinks", sinks, g_sinks, m_sinks, v_sinks),
             ("w_conv_ffn", w_conv_ffn, g_wc_s, m_w_conv_ffn, v_w_conv_ffn),
             ("final_norm", final_norm, g_final, m_final_norm, v_final_norm)]
    shapes = [t[1].shape for t in small]
    packed = [_pack([t[k] for t in small]) for k in (1, 2, 3, 4)]
    n_real = sum(t[1].size for t in small)
    lane_id = jnp.arange(packed[3].size).reshape(packed[3].shape)
    packed[3] = jnp.where(lane_id < n_real, packed[3], 1.0)
    outs = adamw(packed[0], packed[1], None, packed[2], packed[3])
    for (name, *_), g, d, mn, vn in zip(small, *[_unpack(o, shapes) for o in outs]):
        grads[name], deltas[name], new_m[name], new_v[name] = g, d, mn, vn

    order = ["c_ctx", "w_mod", "b_mod", "norm_mix", "norm_ffn", "w_in_ab", "conv_a", "conv_b", "conv_b_bias",
             "ln_b_gain", "ln_b_bias", "w_out_ab", "w_qkv", "w_o", "sinks", "w_up", "w_conv_ffn", "w_down",
             "final_norm"]
    return (loss, grad_x, *[grads[n] for n in order], *[deltas[n] for n in order],
            *[new_m[n] for n in order], *[new_v[n] for n in order])
```

```python
import jax
import jax.numpy as jnp
from jax import lax
from jax.experimental import pallas as pl
from jax.experimental.pallas import tpu as pltpu

F32 = jnp.float32
BF16 = jnp.bfloat16
MESH = pl.DeviceIdType.MESH

EPS = 1e-6
NEG_INF = -1e30
GRID_W = 64
HEAD_DIM = 64
N_HEADS = 16
WINDOW = 128
QB = 128
ROPE_THETA = 10000.0
A_W = 512
B_CONV = 31
D_FF = 2816
ADAM_LR, ADAM_B1, ADAM_B2, ADAM_EPS, ADAM_WD, ADAM_STEP = 0.001, 0.9, 0.999, 1e-8, 0.01, 10

TMR = 256
HALO = 16
N_DEV = 8
N_CHIP = 4


def _params(vmem_mb=None):
    if vmem_mb is None:
        return pltpu.CompilerParams()
    return pltpu.CompilerParams(vmem_limit_bytes=vmem_mb * 1024 * 1024)


def _row_tile(rows, cap=768):
    for t in (2816, 1408, 768, 704, 512, 384, 256, 128, 64, 32, 16, 8):
        if t <= cap and rows % t == 0:
            return t
    raise ValueError(rows)


def _colsum8(v):
    r, c = v.shape
    return v.reshape(r // 8, 8, c).sum(axis=0)


def _sigmoid(v):
    return 0.5 * jnp.tanh(0.5 * v) + 0.5


def mm_nn(a, w, kind, out_dtype):
    R = a.shape[0]
    _, kb, nb = w.shape
    tm = _row_tile(R)
    resident = pl.BlockSpec((N_CHIP, kb, nb), lambda i: (0, 0, 0), pipeline_mode=pl.Buffered(1))
    if kind == "col":
        def body(a_ref, w_ref, o_ref):
            av = a_ref[...].astype(BF16)
            for q in range(N_CHIP):
                o_ref[:, q * nb:(q + 1) * nb] = jnp.dot(av, w_ref[q], preferred_element_type=F32).astype(o_ref.dtype)
        return pl.pallas_call(
            body, name="mm_nn_col", grid=(R // tm,),
            in_specs=[pl.BlockSpec((tm, kb), lambda i: (i, 0)), resident],
            out_specs=pl.BlockSpec((tm, N_CHIP * nb), lambda i: (i, 0)),
            out_shape=jax.ShapeDtypeStruct((R, N_CHIP * nb), out_dtype),
            compiler_params=_params(48))(a, w)

    def body(a_ref, w_ref, o_ref):
        wv = w_ref[...].reshape(N_CHIP * kb, nb)
        o_ref[...] = jnp.dot(a_ref[...].astype(BF16), wv, preferred_element_type=F32).astype(o_ref.dtype)
    return pl.pallas_call(
        body, name="mm_nn_row", grid=(R // tm,),
        in_specs=[pl.BlockSpec((tm, N_CHIP * kb), lambda i: (i, 0)), resident],
        out_specs=pl.BlockSpec((tm, nb), lambda i: (i, 0)),
        out_shape=jax.ShapeDtypeStruct((R, nb), out_dtype),
        compiler_params=_params(48))(a, w)


def mm_nt(d, w, kind, out_dtype):
    R = d.shape[0]
    _, kb, nb = w.shape
    tm = _row_tile(R)
    contract_last = (((1,), (1,)), ((), ()))
    resident = pl.BlockSpec((N_CHIP, kb, nb), lambda i: (0, 0, 0), pipeline_mode=pl.Buffered(1))
    if kind == "col":
        def body(d_ref, w_ref, o_ref):
            acc = None
            for q in range(N_CHIP):
                t = lax.dot_general(d_ref[:, q * nb:(q + 1) * nb].astype(BF16), w_ref[q], contract_last,
                                    preferred_element_type=F32)
                acc = t if acc is None else acc + t
            o_ref[...] = acc.astype(o_ref.dtype)
        return pl.pallas_call(
            body, name="mm_nt_col", grid=(R // tm,),
            in_specs=[pl.BlockSpec((tm, N_CHIP * nb), lambda i: (i, 0)), resident],
            out_specs=pl.BlockSpec((tm, kb), lambda i: (i, 0)),
            out_shape=jax.ShapeDtypeStruct((R, kb), out_dtype),
            compiler_params=_params(48))(d, w)

    def body(d_ref, w_ref, o_ref):
        wv = w_ref[...].reshape(N_CHIP * kb, nb)
        o_ref[...] = lax.dot_general(d_ref[...].astype(BF16), wv, contract_last,
                                     preferred_element_type=F32).astype(o_ref.dtype)
    return pl.pallas_call(
        body, name="mm_nt_row", grid=(R // tm,),
        in_specs=[pl.BlockSpec((tm, nb), lambda i: (i, 0)), resident],
        out_specs=pl.BlockSpec((tm, N_CHIP * kb), lambda i: (i, 0)),
        out_shape=jax.ShapeDtypeStruct((R, N_CHIP * kb), out_dtype),
        compiler_params=_params(48))(d, w)


def mm_tn(a, d, kind, like):
    R = a.shape[0]
    _, kb, nb = like.shape
    tm = _row_tile(R, 1408 if kind == "col" else 768)
    nsteps = R // tm
    contract_rows = (((0,), (0,)), ((), ()))
    out_shape = jax.ShapeDtypeStruct(like.shape, BF16)

    def accumulate(a_ref, d_ref, acc_ref):
        @pl.when(pl.program_id(1) == 0)
        def _():
            acc_ref[...] = jnp.zeros_like(acc_ref)
        acc_ref[...] += lax.dot_general(a_ref[...].astype(BF16), d_ref[...].astype(BF16), contract_rows,
                                        preferred_element_type=F32)
    if kind == "col":
        def body(a_ref, d_ref, o_ref, acc_ref):
            accumulate(a_ref, d_ref, acc_ref)

            @pl.when(pl.program_id(1) == nsteps - 1)
            def _():
                o_ref[...] = acc_ref[...].astype(BF16)
        return pl.pallas_call(
            body, name="mm_tn_col", grid=(N_CHIP, nsteps),
            in_specs=[pl.BlockSpec((tm, kb), lambda q, i: (i, 0)), pl.BlockSpec((tm, nb), lambda q, i: (i, q))],
            out_specs=pl.BlockSpec((None, kb, nb), lambda q, i: (q, 0, 0)), out_shape=out_shape,
            scratch_shapes=[pltpu.VMEM((kb, nb), F32)], compiler_params=_params(48))(a, d)
    tn = 512

    def body(a_ref, d_ref, o_ref, acc_ref):
        accumulate(a_ref, d_ref, acc_ref)

        @pl.when(pl.program_id(1) == nsteps - 1)
        def _():
            o_ref[...] = acc_ref[...].astype(BF16).reshape(N_CHIP, kb, tn)
    return pl.pallas_call(
        body, name="mm_tn_row", grid=(nb // tn, nsteps),
        in_specs=[pl.BlockSpec((tm, N_CHIP * kb), lambda n, i: (i, 0)), pl.BlockSpec((tm, tn), lambda n, i: (i, n))],
        out_specs=pl.BlockSpec((N_CHIP, kb, tn), lambda n, i: (0, 0, n)), out_shape=out_shape,
        scratch_shapes=[pltpu.VMEM((N_CHIP * kb, tn), F32)], compiler_params=_params(48))(a, d)


def _seg(i, T):
    return (i >= T // TMR).astype(jnp.int32)


def norm_mod_fwd(x, nw, mod, k, T):
    R, dm = x.shape

    def body(x_ref, nw_ref, mod_ref, h_ref):
        seg = _seg(pl.program_id(0), T)
        sh = mod_ref[seg, pl.ds(k, 1), :]
        sc = mod_ref[seg, pl.ds(k + 1, 1), :]
        xv = x_ref[...]
        r = lax.rsqrt(jnp.mean(xv * xv, axis=-1, keepdims=True) + EPS)
        h_ref[...] = ((xv * r * nw_ref[...]) * (1.0 + sc) + sh).astype(BF16)
    return pl.pallas_call(
        body, name="norm_mod_fwd", grid=(R // TMR,),
        in_specs=[pl.BlockSpec((TMR, dm), lambda i: (i, 0)),
                  pl.BlockSpec((1, dm), lambda i: (0, 0)),
                  pl.BlockSpec((2, 6, dm), lambda i: (0, 0, 0))],
        out_specs=pl.BlockSpec((TMR, dm), lambda i: (i, 0)),
        out_shape=jax.ShapeDtypeStruct((R, dm), BF16))(x, nw, mod)


def norm_mod_bwd(dh, x, nw, mod, dxr, k, T):
    R, dm = x.shape

    def body(dh_ref, x_ref, nw_ref, mod_ref, dxr_ref, dx_ref, dmod_ref, dnw_ref):
        i = pl.program_id(0)
        seg = _seg(i, T)

        @pl.when(i == 0)
        def _():
            dmod_ref[...] = jnp.zeros_like(dmod_ref)
            dnw_ref[...] = jnp.zeros_like(dnw_ref)
        sc = mod_ref[seg, pl.ds(k + 1, 1), :]
        nwv = nw_ref[...]
        xv = x_ref[...]
        r = lax.rsqrt(jnp.mean(xv * xv, axis=-1, keepdims=True) + EPS)
        xh = xv * r
        dhv = dh_ref[...]
        dmod_ref[seg, 0] += _colsum8(dhv)
        dmod_ref[seg, 1] += _colsum8(dhv * (xh * nwv))
        dn = dhv * (1.0 + sc)
        dnw_ref[...] += _colsum8(dn * xh)
        dxh = dn * nwv
        dx = r * (dxh - xh * jnp.mean(dxh * xh, axis=-1, keepdims=True))
        dx_ref[...] = dxr_ref[...] + dx
    tile = pl.BlockSpec((TMR, dm), lambda i: (i, 0))
    return pl.pallas_call(
        body, name="norm_mod_bwd", grid=(R // TMR,),
        in_specs=[tile, tile, pl.BlockSpec((1, dm), lambda i: (0, 0)),
                  pl.BlockSpec((2, 6, dm), lambda i: (0, 0, 0)), tile],
        out_specs=[tile, pl.BlockSpec((2, 2, 8, dm), lambda i: (0, 0, 0, 0)),
                   pl.BlockSpec((8, dm), lambda i: (0, 0))],
        out_shape=[jax.ShapeDtypeStruct((R, dm), F32), jax.ShapeDtypeStruct((2, 2, 8, dm), F32),
                   jax.ShapeDtypeStruct((8, dm), F32)])(dh, x, nw, mod, dxr)


def resid_norm_fwd(x, y, nw, mod_g, mod_n, kg, kn, T):
    R, dm = x.shape

    def body(x_ref, y_ref, nw_ref, mg_ref, mn_ref, xo_ref, h_ref):
        seg = _seg(pl.program_id(0), T)
        xv = x_ref[...] + mg_ref[seg, pl.ds(kg, 1), :] * y_ref[...]
        xo_ref[...] = xv
        r = lax.rsqrt(jnp.mean(xv * xv, axis=-1, keepdims=True) + EPS)
        h_ref[...] = ((xv * r * nw_ref[...]) * (1.0 + mn_ref[seg, pl.ds(kn + 1, 1), :])
                      + mn_ref[seg, pl.ds(kn, 1), :]).astype(BF16)
    tile = pl.BlockSpec((TMR, dm), lambda i: (i, 0))
    modspec = pl.BlockSpec((2, 6, dm), lambda i: (0, 0, 0))
    return pl.pallas_call(
        body, name="resid_norm_fwd", grid=(R // TMR,),
        in_specs=[tile, tile, pl.BlockSpec((1, dm), lambda i: (0, 0)), modspec, modspec],
        out_specs=[tile, tile],
        out_shape=[jax.ShapeDtypeStruct((R, dm), F32), jax.ShapeDtypeStruct((R, dm), BF16)])(x, y, nw, mod_g, mod_n)


def norm_resid_bwd(dh, x, nw, mod_n, dxr, y, mod_g, kn, kg, T):
    R, dm = x.shape

    def body(dh_ref, x_ref, nw_ref, mn_ref, dxr_ref, y_ref, mg_ref, dx_ref, dy_ref, dmod_ref, dnw_ref, dg_ref):
        i = pl.program_id(0)
        seg = _seg(i, T)

        @pl.when(i == 0)
        def _():
            dmod_ref[...] = jnp.zeros_like(dmod_ref)
            dnw_ref[...] = jnp.zeros_like(dnw_ref)
            dg_ref[...] = jnp.zeros_like(dg_ref)
        sc = mn_ref[seg, pl.ds(kn + 1, 1), :]
        nwv = nw_ref[...]
        xv = x_ref[...]
        r = lax.rsqrt(jnp.mean(xv * xv, axis=-1, keepdims=True) + EPS)
        xh = xv * r
        dhv = dh_ref[...]
        dmod_ref[seg, 0] += _colsum8(dhv)
        dmod_ref[seg, 1] += _colsum8(dhv * (xh * nwv))
        dn = dhv * (1.0 + sc)
        dnw_ref[...] += _colsum8(dn * xh)
        dxh = dn * nwv
        dx = dxr_ref[...] + r * (dxh - xh * jnp.mean(dxh * xh, axis=-1, keepdims=True))
        dx_ref[...] = dx
        dy_ref[...] = (mg_ref[seg, pl.ds(kg, 1), :] * dx).astype(BF16)
        dg_ref[seg] += _colsum8(dx * y_ref[...])
    tile = pl.BlockSpec((TMR, dm), lambda i: (i, 0))
    modspec = pl.BlockSpec((2, 6, dm), lambda i: (0, 0, 0))
    return pl.pallas_call(
        body, name="norm_resid_bwd", grid=(R // TMR,),
        in_specs=[tile, tile, pl.BlockSpec((1, dm), lambda i: (0, 0)), modspec, tile, tile, modspec],
        out_specs=[tile, tile, pl.BlockSpec((2, 2, 8, dm), lambda i: (0, 0, 0, 0)),
                   pl.BlockSpec((8, dm), lambda i: (0, 0)), pl.BlockSpec((2, 8, dm), lambda i: (0, 0, 0))],
        out_shape=[jax.ShapeDtypeStruct((R, dm), F32), jax.ShapeDtypeStruct((R, dm), BF16),
                   jax.ShapeDtypeStruct((2, 2, 8, dm), F32), jax.ShapeDtypeStruct((8, dm), F32),
                   jax.ShapeDtypeStruct((2, 8, dm), F32)])(dh, x, nw, mod_n, dxr, y, mod_g)


def resid_fwd(x, y, mod, k, T):
    R, dm = x.shape

    def body(x_ref, y_ref, mod_ref, o_ref):
        seg = _seg(pl.program_id(0), T)
        o_ref[...] = x_ref[...] + mod_ref[seg, pl.ds(k, 1), :] * y_ref[...]
    tile = pl.BlockSpec((TMR, dm), lambda i: (i, 0))
    return pl.pallas_call(
        body, name="resid_fwd", grid=(R // TMR,),
        in_specs=[tile, tile, pl.BlockSpec((2, 6, dm), lambda i: (0, 0, 0))],
        out_specs=tile, out_shape=jax.ShapeDtypeStruct((R, dm), F32))(x, y, mod)


def resid_bwd(dxn, y, mod, k, T):
    R, dm = dxn.shape

    def body(dx_ref, y_ref, mod_ref, dy_ref, dg_ref):
        i = pl.program_id(0)
        seg = _seg(i, T)

        @pl.when(i == 0)
        def _():
            dg_ref[...] = jnp.zeros_like(dg_ref)
        dxv = dx_ref[...]
        dy_ref[...] = (mod_ref[seg, pl.ds(k, 1), :] * dxv).astype(BF16)
        dg_ref[seg] += _colsum8(dxv * y_ref[...])
    tile = pl.BlockSpec((TMR, dm), lambda i: (i, 0))
    return pl.pallas_call(
        body, name="resid_bwd", grid=(R // TMR,),
        in_specs=[tile, tile, pl.BlockSpec((2, 6, dm), lambda i: (0, 0, 0))],
        out_specs=[tile, pl.BlockSpec((2, 8, dm), lambda i: (0, 0, 0))],
        out_shape=[jax.ShapeDtypeStruct((R, dm), BF16), jax.ShapeDtypeStruct((2, 8, dm), F32)])(dxn, y, mod)


def _halo_specs(width, R):
    nblk = R // HALO
    per = TMR // HALO
    return (pl.BlockSpec((HALO, width), lambda i: (jnp.maximum(i * per - 1, 0), 0)),
            pl.BlockSpec((TMR, width), lambda i: (i, 0)),
            pl.BlockSpec((HALO, width), lambda i: (jnp.minimum((i + 1) * per, nblk - 1), 0)))


def _halo_live(i, T, R):
    nl = T // TMR
    return (i != 0) & (i != nl), (i != nl - 1) & (i != R // TMR - 1)


def _ext(refs, c0, cw, live, halo=HALO):
    pref, ref, nref = refs
    before = jnp.where(live[0], pref[:, c0:c0 + cw].astype(F32)[HALO - halo:], 0.0)
    after = jnp.where(live[1], nref[:, c0:c0 + cw].astype(F32)[:halo], 0.0)
    return jnp.concatenate([before, ref[:, c0:c0 + cw].astype(F32), after], axis=0)


def _at(ext, off, halo=HALO):
    n = ext.shape[0]
    s = (-off) % n
    y = pltpu.roll(ext, s, 0) if s else ext
    return y[halo:halo + TMR]


def ffnconv_fwd(u, wc, T):
    R, w2 = u.shape
    cw = 256

    def body(up_ref, u_ref, un_ref, wc_ref, z_ref):
        live = _halo_live(pl.program_id(0), T, R)

        def conv(c0):
            e = _ext((up_ref, u_ref, un_ref), c0, cw, live, 8)
            return (wc_ref[pl.ds(0, 1), c0:c0 + cw] * _at(e, -1, 8) + wc_ref[pl.ds(1, 1), c0:c0 + cw] * _at(e, 0, 8)
                    + wc_ref[pl.ds(2, 1), c0:c0 + cw] * _at(e, 1, 8))
        for j in range(D_FF // cw):
            a = conv(j * cw)
            g = conv(D_FF + j * cw)
            z_ref[:, j * cw:(j + 1) * cw] = (g * _sigmoid(g) * a).astype(BF16)
    return pl.pallas_call(
        body, name="ffnconv_fwd", grid=(R // TMR,),
        in_specs=[*_halo_specs(w2, R), pl.BlockSpec((3, w2), lambda i: (0, 0))],
        out_specs=pl.BlockSpec((TMR, D_FF), lambda i: (i, 0)),
        out_shape=jax.ShapeDtypeStruct((R, D_FF), BF16), compiler_params=_params(48))(u, u, u, wc)


def ffnconv_bwd1(dz, u, wc, T):
    R, w2 = u.shape
    cw = 256

    def body(dz_ref, up_ref, u_ref, un_ref, wc_ref, duc_ref, dwc_ref):
        i = pl.program_id(0)
        live = _halo_live(i, T, R)

        @pl.when(i == 0)
        def _():
            dwc_ref[...] = jnp.zeros_like(dwc_ref)

        def taps(c0):
            e = _ext((up_ref, u_ref, un_ref), c0, cw, live, 8)
            return [_at(e, -1, 8), _at(e, 0, 8), _at(e, 1, 8)]

        def conv(t, c0):
            return (wc_ref[pl.ds(0, 1), c0:c0 + cw] * t[0] + wc_ref[pl.ds(1, 1), c0:c0 + cw] * t[1]
                    + wc_ref[pl.ds(2, 1), c0:c0 + cw] * t[2])
        for j in range(D_FF // cw):
            ca, cg = j * cw, D_FF + j * cw
            ta, tg = taps(ca), taps(cg)
            a, g = conv(ta, ca), conv(tg, cg)
            dzv = dz_ref[:, ca:ca + cw].astype(F32)
            sg = _sigmoid(g)
            da = dzv * (g * sg)
            dg = dzv * a * (sg * (1.0 + g * (1.0 - sg)))
            duc_ref[:, ca:ca + cw] = da.astype(BF16)
            duc_ref[:, cg:cg + cw] = dg.astype(BF16)
            for k in range(3):
                dwc_ref[k, :, ca:ca + cw] += _colsum8(da * ta[k])
                dwc_ref[k, :, cg:cg + cw] += _colsum8(dg * tg[k])
    return pl.pallas_call(
        body, name="ffnconv_bwd1", grid=(R // TMR,),
        in_specs=[pl.BlockSpec((TMR, D_FF), lambda i: (i, 0)), *_halo_specs(w2, R),
                  pl.BlockSpec((3, w2), lambda i: (0, 0))],
        out_specs=[pl.BlockSpec((TMR, w2), lambda i: (i, 0)), pl.BlockSpec((3, 8, w2), lambda i: (0, 0, 0))],
        out_shape=[jax.ShapeDtypeStruct((R, w2), BF16), jax.ShapeDtypeStruct((3, 8, w2), F32)],
        compiler_params=_params(48))(dz, u, u, u, wc)


def ffnconv_bwd2(duc, wc, T):
    R, w2 = duc.shape
    cw = 256

    def body(dp_ref, d_ref, dn_ref, wc_ref, du_ref):
        live = _halo_live(pl.program_id(0), T, R)
        for j in range(w2 // cw):
            c0 = j * cw
            e = _ext((dp_ref, d_ref, dn_ref), c0, cw, live, 8)
            du_ref[:, c0:c0 + cw] = (wc_ref[pl.ds(0, 1), c0:c0 + cw] * _at(e, 1, 8)
                                     + wc_ref[pl.ds(1, 1), c0:c0 + cw] * _at(e, 0, 8)
                                     + wc_ref[pl.ds(2, 1), c0:c0 + cw] * _at(e, -1, 8)).astype(BF16)
    return pl.pallas_call(
        body, name="ffnconv_bwd2", grid=(R // TMR,),
        in_specs=[*_halo_specs(w2, R), pl.BlockSpec((3, w2), lambda i: (0, 0))],
        out_specs=pl.BlockSpec((TMR, w2), lambda i: (i, 0)),
        out_shape=jax.ShapeDtypeStruct((R, w2), BF16), compiler_params=_params(48))(duc, duc, duc, wc)


_CW = 128


def _mixer_a(prefs, wa_ref, live):
    cin = _ext(prefs, A_W, A_W, live) * _ext(prefs, 2 * A_W, A_W, live)
    ca = (wa_ref[pl.ds(0, 1), :] * _at(cin, -1) + wa_ref[pl.ds(1, 1), :] * _at(cin, 0)
          + wa_ref[pl.ds(2, 1), :] * _at(cin, 1))
    return cin, ca


def _mixer_b(prefs, wb_ref, bias_ref, live, ub_s, ub2_s):
    for cc in range(A_W // _CW):
        c0 = cc * _CW
        ub = _ext(prefs, 3 * A_W + c0, _CW, live) * _sigmoid(_ext(prefs, 4 * A_W + c0, _CW, live))
        ub_s[:, c0:c0 + _CW] = ub
        acc = jnp.zeros((TMR, _CW), F32) + bias_ref[:, c0:c0 + _CW]
        for k in range(B_CONV):
            acc = acc + wb_ref[pl.ds(k, 1), c0:c0 + _CW] * _at(ub, k - B_CONV // 2)
        ub2_s[:, c0:c0 + _CW] = acc


def _layernorm_stats(v):
    mu = jnp.mean(v, axis=-1, keepdims=True)
    xc = v - mu
    rs = lax.rsqrt(jnp.mean(xc * xc, axis=-1, keepdims=True) + EPS)
    return xc * rs, rs


def convmix_fwd(p, wa, wb, bias, lng, lnb, T):
    R, wp = p.shape

    def body(pp_ref, p_ref, pn_ref, wa_ref, wb_ref, bias_ref, lng_ref, lnb_ref, o_ref, ub_s, ub2_s):
        live = _halo_live(pl.program_id(0), T, R)
        prefs = (pp_ref, p_ref, pn_ref)
        _, ca = _mixer_a(prefs, wa_ref, live)
        o_ref[:, 0:A_W] = (p_ref[:, 0:A_W].astype(F32) * ca).astype(BF16)
        _mixer_b(prefs, wb_ref, bias_ref, live, ub_s, ub2_s)
        xh, _ = _layernorm_stats(ub2_s[...])
        lv = xh * lng_ref[...] + lnb_ref[...]
        o_ref[:, A_W:2 * A_W] = (lv * _sigmoid(lv)).astype(BF16)
    vec = pl.BlockSpec((1, A_W), lambda i: (0, 0))
    return pl.pallas_call(
        body, name="convmix_fwd", grid=(R // TMR,),
        in_specs=[*_halo_specs(wp, R), pl.BlockSpec((3, A_W), lambda i: (0, 0)),
                  pl.BlockSpec((B_CONV, A_W), lambda i: (0, 0)), vec, vec, vec],
        out_specs=pl.BlockSpec((TMR, 2 * A_W), lambda i: (i, 0)),
        out_shape=jax.ShapeDtypeStruct((R, 2 * A_W), BF16),
        scratch_shapes=[pltpu.VMEM((TMR + 2 * HALO, A_W), F32), pltpu.VMEM((TMR, A_W), F32)],
        compiler_params=_params(48))(p, p, p, wa, wb, bias, lng, lnb)


def convmix_bwd1(dyab, p, wa, wb, bias, lng, lnb, T):
    R, wp = p.shape

    def body(dy_ref, pp_ref, p_ref, pn_ref, wa_ref, wb_ref, bias_ref, lng_ref, lnb_ref,
             dmid_ref, dwa_ref, dwb_ref, dvec_ref, ub_s, ub2_s):
        i = pl.program_id(0)
        live = _halo_live(i, T, R)

        @pl.when(i == 0)
        def _():
            dwa_ref[...] = jnp.zeros_like(dwa_ref)
            dwb_ref[...] = jnp.zeros_like(dwb_ref)
            dvec_ref[...] = jnp.zeros_like(dvec_ref)
        prefs = (pp_ref, p_ref, pn_ref)
        cin, ca = _mixer_a(prefs, wa_ref, live)
        dya = dy_ref[:, 0:A_W]
        dmid_ref[:, 0:A_W] = dya * ca
        dca = dya * p_ref[:, 0:A_W].astype(F32)
        dmid_ref[:, A_W:2 * A_W] = dca
        for k in range(3):
            dwa_ref[k] += _colsum8(dca * _at(cin, k - 1))
        _mixer_b(prefs, wb_ref, bias_ref, live, ub_s, ub2_s)
        xh, rs = _layernorm_stats(ub2_s[...])
        gain = lng_ref[...]
        lv = xh * gain + lnb_ref[...]
        sl = _sigmoid(lv)
        dl = dy_ref[:, A_W:2 * A_W] * (sl * (1.0 + lv * (1.0 - sl)))
        dvec_ref[1] += _colsum8(dl * xh)
        dvec_ref[2] += _colsum8(dl)
        dxh = dl * gain
        dub2 = rs * (dxh - jnp.mean(dxh, axis=-1, keepdims=True)
                     - xh * jnp.mean(dxh * xh, axis=-1, keepdims=True))
        dvec_ref[0] += _colsum8(dub2)
        dmid_ref[:, 2 * A_W:3 * A_W] = dub2
        for cc in range(A_W // _CW):
            c0 = cc * _CW
            ub = ub_s[:, c0:c0 + _CW]
            d = dmid_ref[:, 2 * A_W + c0:2 * A_W + c0 + _CW]
            for k in range(B_CONV):
                dwb_ref[k, :, c0:c0 + _CW] += _colsum8(d * _at(ub, k - B_CONV // 2))
    vec = pl.BlockSpec((1, A_W), lambda i: (0, 0))
    return pl.pallas_call(
        body, name="convmix_bwd1", grid=(R // TMR,),
        in_specs=[pl.BlockSpec((TMR, 2 * A_W), lambda i: (i, 0)), *_halo_specs(wp, R),
                  pl.BlockSpec((3, A_W), lambda i: (0, 0)), pl.BlockSpec((B_CONV, A_W), lambda i: (0, 0)),
                  vec, vec, vec],
        out_specs=[pl.BlockSpec((TMR, 3 * A_W), lambda i: (i, 0)),
                   pl.BlockSpec((3, 8, A_W), lambda i: (0, 0, 0)),
                   pl.BlockSpec((B_CONV, 8, A_W), lambda i: (0, 0, 0)),
                   pl.BlockSpec((3, 8, A_W), lambda i: (0, 0, 0))],
        out_shape=[jax.ShapeDtypeStruct((R, 3 * A_W), F32), jax.ShapeDtypeStruct((3, 8, A_W), F32),
                   jax.ShapeDtypeStruct((B_CONV, 8, A_W), F32), jax.ShapeDtypeStruct((3, 8, A_W), F32)],
        scratch_shapes=[pltpu.VMEM((TMR + 2 * HALO, A_W), F32), pltpu.VMEM((TMR, A_W), F32)],
        compiler_params=_params(48))(dyab, p, p, p, wa, wb, bias, lng, lnb)


def convmix_bwd2(dmid, p, wa, wb, T):
    R, wp = p.shape

    def body(mp_ref, m_ref, mn_ref, p_ref, wa_ref, wb_ref, dp_ref):
        live = _halo_live(pl.program_id(0), T, R)
        mrefs = (mp_ref, m_ref, mn_ref)
        dp_ref[:, 0:A_W] = m_ref[:, 0:A_W].astype(BF16)
        dca = _ext(mrefs, A_W, A_W, live)
        dcin = (wa_ref[pl.ds(0, 1), :] * _at(dca, 1) + wa_ref[pl.ds(1, 1), :] * _at(dca, 0)
                + wa_ref[pl.ds(2, 1), :] * _at(dca, -1))
        dp_ref[:, A_W:2 * A_W] = (dcin * p_ref[:, 2 * A_W:3 * A_W].astype(F32)).astype(BF16)
        dp_ref[:, 2 * A_W:3 * A_W] = (dcin * p_ref[:, A_W:2 * A_W].astype(F32)).astype(BF16)
        for cc in range(A_W // _CW):
            c0 = cc * _CW
            d = _ext(mrefs, 2 * A_W + c0, _CW, live)
            dub = jnp.zeros((TMR, _CW), F32)
            for k in range(B_CONV):
                dub = dub + wb_ref[pl.ds(k, 1), c0:c0 + _CW] * _at(d, B_CONV // 2 - k)
            vb = p_ref[:, 3 * A_W + c0:3 * A_W + c0 + _CW].astype(F32)
            s = _sigmoid(p_ref[:, 4 * A_W + c0:4 * A_W + c0 + _CW].astype(F32))
            dp_ref[:, 3 * A_W + c0:3 * A_W + c0 + _CW] = (dub * s).astype(BF16)
            dp_ref[:, 4 * A_W + c0:4 * A_W + c0 + _CW] = (dub * vb * s * (1.0 - s)).astype(BF16)
    return pl.pallas_call(
        body, name="convmix_bwd2", grid=(R // TMR,),
        in_specs=[*_halo_specs(3 * A_W, R), pl.BlockSpec((TMR, wp), lambda i: (i, 0)),
                  pl.BlockSpec((3, A_W), lambda i: (0, 0)), pl.BlockSpec((B_CONV, A_W), lambda i: (0, 0))],
        out_specs=pl.BlockSpec((TMR, wp), lambda i: (i, 0)),
        out_shape=jax.ShapeDtypeStruct((R, wp), BF16), compiler_params=_params(48))(dmid, dmid, dmid, p, wa, wb)


def _rot_half(v):
    w = v.shape[-1]
    lane = lax.broadcasted_iota(jnp.int32, (1, w), 1)
    return jnp.where(lane % HEAD_DIM < HEAD_DIM // 2, pltpu.roll(v, w - HEAD_DIM // 2, 1),
                     pltpu.roll(v, HEAD_DIM // 2, 1))


def rope_fwd(qkv, cs, sn):
    R, wq = qkv.shape
    qw = N_HEADS * HEAD_DIM
    kw = (wq - qw) // 2
    scale = HEAD_DIM ** -0.5

    def body(x_ref, cs_ref, sn_ref, o_ref):
        c, s = cs_ref[...], sn_ref[...]
        q = x_ref[:, 0:qw]
        o_ref[:, 0:qw] = ((q * jnp.tile(c, (1, qw // 128)) + _rot_half(q) * jnp.tile(s, (1, qw // 128)))
                          * scale).astype(BF16)
        k = x_ref[:, qw:qw + kw]
        o_ref[:, qw:qw + kw] = (k * jnp.tile(c, (1, kw // 128))
                                + _rot_half(k) * jnp.tile(s, (1, kw // 128))).astype(BF16)
        o_ref[:, qw + kw:] = x_ref[:, qw + kw:].astype(BF16)
    tab = pl.BlockSpec((TMR, 128), lambda i: (i, 0))
    return pl.pallas_call(
        body, name="rope_fwd", grid=(R // TMR,),
        in_specs=[pl.BlockSpec((TMR, wq), lambda i: (i, 0)), tab, tab],
        out_specs=pl.BlockSpec((TMR, wq), lambda i: (i, 0)),
        out_shape=jax.ShapeDtypeStruct((R, wq), BF16))(qkv, cs, sn)


def rope_bwd(dq, dks, dvs, dkc, dvc, cs, sn, T):
    R, qw = dq.shape
    kw = dkc.shape[1]
    nb = R // QB
    nl = T // QB
    scale = HEAD_DIM ** -0.5

    def body(dq_ref, kp_ref, ko_ref, kn_ref, vp_ref, vo_ref, vn_ref, kc_ref, vc_ref, cs_ref, sn_ref, o_ref):
        b = pl.program_id(0)
        c, s = cs_ref[...], sn_ref[...]
        has_next = (b + 1 < nb).astype(F32)
        has_prev = (b >= 1).astype(F32)
        is_ctx = (b >= nl).astype(F32)
        g = dq_ref[...] * scale
        o_ref[:, 0:qw] = (g * jnp.tile(c, (1, qw // 128)) + _rot_half(g * jnp.tile(s, (1, qw // 128)))).astype(BF16)
        g = ko_ref[...] + kp_ref[...] * has_next + kn_ref[...] * has_prev + kc_ref[...] * is_ctx
        o_ref[:, qw:qw + kw] = (g * jnp.tile(c, (1, kw // 128))
                                + _rot_half(g * jnp.tile(s, (1, kw // 128)))).astype(BF16)
        o_ref[:, qw + kw:] = (vo_ref[...] + vp_ref[...] * has_next + vn_ref[...] * has_prev
                              + vc_ref[...] * is_ctx).astype(BF16)
    own = pl.BlockSpec((QB, kw), lambda b: (b, 0))
    from_next = pl.BlockSpec((QB, kw), lambda b: (jnp.minimum(b + 1, nb - 1), 0))
    from_prev = pl.BlockSpec((QB, kw), lambda b: (jnp.maximum(b - 1, 0), 0))
    ctx = pl.BlockSpec((QB, kw), lambda b: (jnp.maximum(b - nl, 0), 0))
    tab = pl.BlockSpec((QB, 128), lambda b: (b, 0))
    return pl.pallas_call(
        body, name="rope_bwd", grid=(nb,),
        in_specs=[pl.BlockSpec((QB, qw), lambda b: (b, 0)), from_next, own, from_prev, from_next, own, from_prev,
                  ctx, ctx, tab, tab],
        out_specs=pl.BlockSpec((QB, qw + 2 * kw), lambda b: (b, 0)),
        out_shape=jax.ShapeDtypeStruct((R, qw + 2 * kw), BF16))(
            dq, dks[0], dks[1], dks[2], dvs[0], dvs[1], dvs[2], dkc, dvc, cs, sn)


def _attn_specs(T, R):
    nl = T // QB
    qcols = N_HEADS * HEAD_DIM // 128
    kcols = 2

    def band(col0, shift):
        return pl.BlockSpec((QB, 128), lambda jj, b: (jnp.clip(b + shift, 0, nl - 1), col0 + jj))

    def ctx(col0):
        return pl.BlockSpec((R - T, 128), lambda jj, b: (T // (R - T), col0 + jj))
    q = pl.BlockSpec((QB, 512), lambda jj, b: (b, jj))
    k0, v0 = qcols, qcols + kcols
    return q, [band(k0, -1), band(k0, 0), band(k0, 1), ctx(k0)], [band(v0, -1), band(v0, 0), band(v0, 1), ctx(v0)]


def _attn_common(T, R):
    nl = T // QB
    nk = 3 * QB + (R - T)

    def low_lanes():
        return lax.broadcasted_iota(jnp.int32, (1, 128), 1) < HEAD_DIM

    def dup(v, par):
        low = low_lanes()
        vf = v.astype(F32)
        r = pltpu.roll(vf, HEAD_DIM, 1)
        return (jnp.where(low, vf, r) if par == 0 else jnp.where(low, r, vf)).astype(BF16)

    def stack(ref, par):
        low = low_lanes()
        pa = ref[:, (2 * par) * 128:(2 * par + 1) * 128].astype(BF16)
        pb = ref[:, (2 * par + 1) * 128:(2 * par + 2) * 128].astype(BF16)
        zero = jnp.zeros_like(pa)
        return jnp.concatenate([jnp.where(low, pa, zero), jnp.where(low, zero, pa),
                                jnp.where(low, pb, zero), jnp.where(low, zero, pb)], axis=0)

    def unstack(v):
        low = low_lanes()
        return (jnp.where(low, v[0:QB], v[QB:2 * QB]), jnp.where(low, v[2 * QB:3 * QB], v[3 * QB:4 * QB]))

    def mask_of(b):
        col = lax.broadcasted_iota(jnp.int32, (1, nk), 1)
        gone = (((col < QB) & (b == 0)) | ((col >= 2 * QB) & (col < 3 * QB) & (b == nl - 1))
                | ((col < 3 * QB) & (b >= nl)))
        return jnp.where(gone, NEG_INF, 0.0)

    def sink_col(sink_ref, first):
        blk = lax.broadcasted_iota(jnp.int32, (4 * QB, 1), 0) // QB
        out = jnp.zeros((4 * QB, 1), F32) + sink_ref[first]
        for h in range(1, 4):
            out = jnp.where(blk == h, sink_ref[first + h], out)
        return out

    def scores(qs, kd, mask, sink):
        s = lax.dot_general(qs, kd, (((1,), (1,)), ((), ())), preferred_element_type=F32) + mask
        m = jnp.maximum(jnp.max(s, axis=-1, keepdims=True), sink)
        e = jnp.exp(s - m)
        es = jnp.exp(sink - m)
        return e, es, 1.0 / (jnp.sum(e, axis=-1, keepdims=True) + es)
    return low_lanes, dup, stack, unstack, mask_of, sink_col, scores


def window_bias(T, R):
    nk = 3 * QB + (R - T)
    row = jnp.arange(QB)[:, None]
    col = jnp.arange(nk)[None, :]
    near = (jnp.abs(col - QB - row) <= WINDOW) | (col >= 3 * QB)
    return jnp.tile(jnp.where(near, 0.0, NEG_INF).astype(F32), (4, 1))


def attn_fwd(qkvr, sinks, bias, T):
    R = qkvr.shape[0]
    qspec, kspecs, vspecs = _attn_specs(T, R)
    _, dup, stack, unstack, mask_of, sink_col, scores = _attn_common(T, R)

    def body(q_ref, kp, ko, kn, kc, vp, vo, vn, vc, sink_ref, bias_ref, o_ref):
        jj, b = pl.program_id(0), pl.program_id(1)
        mask = bias_ref[...] + mask_of(b)
        k_all = jnp.concatenate([kp[...], ko[...], kn[...], kc[...]], axis=0)
        v_all = jnp.concatenate([vp[...], vo[...], vn[...], vc[...]], axis=0)
        for par in range(2):
            kd, vd = dup(k_all, par), dup(v_all, par)
            e, _, rz = scores(stack(q_ref, par), kd, mask, sink_col(sink_ref, jj * 8 + par * 4))
            o = jnp.dot((e * rz).astype(BF16), vd, preferred_element_type=F32)
            pa, pb = unstack(o)
            o_ref[:, (2 * par) * 128:(2 * par + 1) * 128] = pa.astype(BF16)
            o_ref[:, (2 * par + 1) * 128:(2 * par + 2) * 128] = pb.astype(BF16)
    return pl.pallas_call(
        body, name="attn_fwd", grid=(2, R // QB),
        in_specs=[qspec, *kspecs, *vspecs, pl.BlockSpec(memory_space=pltpu.SMEM),
                  pl.BlockSpec(bias.shape, lambda jj, b: (0, 0))],
        out_specs=pl.BlockSpec((QB, 512), lambda jj, b: (b, jj)),
        out_shape=jax.ShapeDtypeStruct((R, N_HEADS * HEAD_DIM), BF16), compiler_params=_params(48))(
            qkvr, *([qkvr] * 8), sinks, bias)


def attn_bwd(qkvr, do, sinks, bias, T):
    R = qkvr.shape[0]
    tc = R - T
    qspec, kspecs, vspecs = _attn_specs(T, R)
    low_lanes, dup, stack, unstack, mask_of, sink_col, scores = _attn_common(T, R)
    contract_rows = (((0,), (0,)), ((), ()))
    contract_last = (((1,), (1,)), ((), ()))

    def body(q_ref, kp, ko, kn, kc, vp, vo, vn, vc, do_ref, sink_ref, bias_ref,
             dq_ref, dkp, dko, dkn, dvp, dvo, dvn, dkc_ref, dvc_ref, dsink_ref):
        jj, b = pl.program_id(0), pl.program_id(1)

        @pl.when((jj == 0) & (b == 0))
        def _():
            dsink_ref[...] = jnp.zeros_like(dsink_ref)

        @pl.when(b == 0)
        def _():
            dkc_ref[...] = jnp.zeros_like(dkc_ref)
            dvc_ref[...] = jnp.zeros_like(dvc_ref)
        mask = bias_ref[...] + mask_of(b)
        k_all = jnp.concatenate([kp[...], ko[...], kn[...], kc[...]], axis=0)
        v_all = jnp.concatenate([vp[...], vo[...], vn[...], vc[...]], axis=0)
        lane = lax.broadcasted_iota(jnp.int32, (8, 128), 1)
        srow = lax.broadcasted_iota(jnp.int32, (8, 128), 0)
        dk_fold, dv_fold = [], []
        for par in range(2):
            kd, vd = dup(k_all, par), dup(v_all, par)
            first = jj * 8 + par * 4
            qs, dos = stack(q_ref, par), stack(do_ref, par)
            e, es, rz = scores(qs, kd, mask, sink_col(sink_ref, first))
            p = e * rz
            dp = lax.dot_general(dos, vd, contract_last, preferred_element_type=F32)
            delta = jnp.sum(p * dp, axis=-1, keepdims=True)
            ds = (p * (dp - delta)).astype(BF16)
            t = es * rz * delta
            for h in range(4):
                dsink = -jnp.sum(t[h * QB:(h + 1) * QB])
                dsink_ref[...] += jnp.where((lane == first + h) & (srow == 0), dsink, 0.0)
            pa, pb = unstack(jnp.dot(ds, kd, preferred_element_type=F32))
            dq_ref[:, (2 * par) * 128:(2 * par + 1) * 128] = pa
            dq_ref[:, (2 * par + 1) * 128:(2 * par + 2) * 128] = pb
            dk_t = lax.dot_general(qs, ds, contract_rows, preferred_element_type=F32)
            dv_t = lax.dot_general(dos, p.astype(BF16), contract_rows, preferred_element_type=F32)
            dk_fold.append(dk_t + pltpu.roll(dk_t, HEAD_DIM, 0))
            dv_fold.append(dv_t + pltpu.roll(dv_t, HEAD_DIM, 0))
        low_rows = lax.broadcasted_iota(jnp.int32, (128, 1), 0) < HEAD_DIM
        dk = jnp.where(low_rows, dk_fold[0], dk_fold[1]).T
        dv = jnp.where(low_rows, dv_fold[0], dv_fold[1]).T
        dkp[...], dko[...], dkn[...] = dk[0:QB], dk[QB:2 * QB], dk[2 * QB:3 * QB]
        dvp[...], dvo[...], dvn[...] = dv[0:QB], dv[QB:2 * QB], dv[2 * QB:3 * QB]
        dkc_ref[...] += dk[3 * QB:]
        dvc_ref[...] += dv[3 * QB:]
    blk = pl.BlockSpec((QB, 128), lambda jj, b: (b, jj))
    cblk = pl.BlockSpec((tc, 128), lambda jj, b: (0, jj))
    part = jax.ShapeDtypeStruct((R, 256), F32)
    csum = jax.ShapeDtypeStruct((tc, 256), F32)
    outs = pl.pallas_call(
        body, name="attn_bwd", grid=(2, R // QB),
        in_specs=[qspec, *kspecs, *vspecs, pl.BlockSpec((QB, 512), lambda jj, b: (b, jj)),
                  pl.BlockSpec(memory_space=pltpu.SMEM), pl.BlockSpec(bias.shape, lambda jj, b: (0, 0))],
        out_specs=[pl.BlockSpec((QB, 512), lambda jj, b: (b, jj)), blk, blk, blk, blk, blk, blk, cblk, cblk,
                   pl.BlockSpec((8, 128), lambda jj, b: (0, 0))],
        out_shape=[jax.ShapeDtypeStruct((R, N_HEADS * HEAD_DIM), F32), part, part, part, part, part, part,
                   csum, csum, jax.ShapeDtypeStruct((8, 128), F32)],
        compiler_params=_params(48))(qkvr, *([qkvr] * 8), do, sinks, bias)
    return outs[0], outs[1:4], outs[4:7], outs[7], outs[8], outs[9]


def loss_head(x, nw, target, T):
    R, dm = x.shape
    nl = T // TMR

    def body(x_ref, nw_ref, t_ref, loss_ref, dx_ref, dnw_ref):
        i = pl.program_id(0)

        @pl.when(i == 0)
        def _():
            loss_ref[...] = jnp.zeros_like(loss_ref)
            dnw_ref[...] = jnp.zeros_like(dnw_ref)
        live = (i < nl).astype(F32)
        nwv = nw_ref[...]
        xv = x_ref[...]
        r = lax.rsqrt(jnp.mean(xv * xv, axis=-1, keepdims=True) + EPS)
        xh = xv * r
        err = xh * nwv - t_ref[...]
        per_row = jnp.mean(err * err, axis=-1, keepdims=True)
        loss_ref[...] += 0.5 * live * jnp.sum(per_row)
        dy = err * (live / dm)
        dnw_ref[...] += _colsum8(dy * xh)
        dxh = dy * nwv
        dx_ref[...] = r * (dxh - xh * jnp.mean(dxh * xh, axis=-1, keepdims=True))
    tile = pl.BlockSpec((TMR, dm), lambda i: (i, 0))
    return pl.pallas_call(
        body, name="loss_head", grid=(R // TMR,),
        in_specs=[tile, pl.BlockSpec((1, dm), lambda i: (0, 0)),
                  pl.BlockSpec((TMR, dm), lambda i: (jnp.minimum(i, nl - 1), 0))],
        out_specs=[pl.BlockSpec((8, 128), lambda i: (0, 0)), tile, pl.BlockSpec((8, dm), lambda i: (0, 0))],
        out_shape=[jax.ShapeDtypeStruct((8, 128), F32), jax.ShapeDtypeStruct((R, dm), F32),
                   jax.ShapeDtypeStruct((8, dm), F32)])(x, nw, target)


def adaln_fwd(cond, w_mod, b_mod):
    nl, dm, ns = w_mod.shape

    def body(c_ref, w_ref, b_ref, o_ref):
        cv = c_ref[...]
        s = (cv * _sigmoid(cv)).astype(BF16)
        o_ref[...] = jnp.dot(s, w_ref[...].astype(BF16), preferred_element_type=F32) + b_ref[...]
    return pl.pallas_call(
        body, name="adaln_fwd", grid=(nl,),
        in_specs=[pl.BlockSpec((16, dm), lambda l: (0, 0)), pl.BlockSpec((None, dm, ns), lambda l: (l, 0, 0)),
                  pl.BlockSpec((None, 1, ns), lambda l: (l, 0, 0))],
        out_specs=pl.BlockSpec((None, 16, ns), lambda l: (l, 0, 0)),
        out_shape=jax.ShapeDtypeStruct((nl, 16, ns), F32), compiler_params=_params(48))(cond, w_mod, b_mod)


def adaln_bwd(cond, dmod, w_mod):
    nl, dm, ns = w_mod.shape

    def body(c_ref, d_ref, w_ref, gw_ref, ds_ref):
        l = pl.program_id(0)

        @pl.when(l == 0)
        def _():
            ds_ref[...] = jnp.zeros_like(ds_ref)
        cv = c_ref[...]
        s = (cv * _sigmoid(cv)).astype(BF16)
        dv = d_ref[...].astype(BF16)
        gw_ref[...] = lax.dot_general(s, dv, (((0,), (0,)), ((), ())), preferred_element_type=F32)
        ds_ref[...] += lax.dot_general(dv, w_ref[...].astype(BF16), (((1,), (1,)), ((), ())),
                                       preferred_element_type=F32)
    return pl.pallas_call(
        body, name="adaln_bwd", grid=(nl,),
        in_specs=[pl.BlockSpec((16, dm), lambda l: (0, 0)), pl.BlockSpec((None, 16, ns), lambda l: (l, 0, 0)),
                  pl.BlockSpec((None, dm, ns), lambda l: (l, 0, 0))],
        out_specs=[pl.BlockSpec((None, dm, ns), lambda l: (l, 0, 0)), pl.BlockSpec((16, dm), lambda l: (0, 0))],
        out_shape=[jax.ShapeDtypeStruct((nl, dm, ns), F32), jax.ShapeDtypeStruct((16, dm), F32)],
        compiler_params=_params(48))(cond, dmod, w_mod)


def _me():
    return lax.axis_index("x"), lax.axis_index("y"), lax.axis_index("c")


def allgather8(block):
    m_per, n = block.shape

    def body(x_ref, out_ref, send_sems, recv_sems, local_sem):
        x, y, c = _me()
        me, sibling = (x, y, c), (x, y, 1 - c)
        chips = [(1 - x, y), (x, 1 - y), (1 - x, 1 - y)]

        def rows(px, py, pc):
            return out_ref.at[pl.ds((4 * px + 2 * py + pc) * m_per, m_per), :]

        def copy(k, blk, to, src=None):
            return pltpu.make_async_remote_copy(
                src_ref=rows(*blk) if src is None else src, dst_ref=rows(*blk),
                send_sem=send_sems.at[k], recv_sem=recv_sems.at[k], device_id=to, device_id_type=MESH)
        mine = pltpu.make_async_copy(x_ref, rows(*me), local_sem)
        mine.start()
        first = [copy(0, me, sibling, src=x_ref)]
        first += [copy(1 + j, me, (*chip, c), src=x_ref) for j, chip in enumerate(chips)]
        for cp in first:
            cp.start()
        passed = [copy(4 + j, (*chip, c), sibling) for j, chip in enumerate(chips)]
        for j, chip in enumerate(chips):
            copy(1 + j, (*chip, c), me).wait_recv()
            passed[j].start()
        copy(0, sibling, me).wait_recv()
        for j, chip in enumerate(chips):
            copy(4 + j, (*chip, 1 - c), me).wait_recv()
        for cp in first + passed:
            cp.wait_send()
        mine.wait()
    return pl.pallas_call(
        body, name="allgather8",
        out_shape=jax.ShapeDtypeStruct((N_DEV * m_per, n), block.dtype),
        in_specs=[pl.BlockSpec(memory_space=pltpu.VMEM)],
        out_specs=pl.BlockSpec(memory_space=pltpu.VMEM),
        scratch_shapes=[pltpu.SemaphoreType.DMA((7,)), pltpu.SemaphoreType.DMA((7,)), pltpu.SemaphoreType.DMA],
        compiler_params=_params(48))(block)


def _other_chips(x, y):
    return [(1 - x, y), (x, 1 - y), (1 - x, 1 - y)]


_HBM = pl.BlockSpec(memory_space=pltpu.HBM)
_SEM = pl.BlockSpec(memory_space=pltpu.SEMAPHORE)
_ANY = pl.BlockSpec(memory_space=pl.ANY)
_EFFECT = pltpu.SideEffectType.DATAFLOW_SIDE_EFFECTING


def _in_hbm(v):
    return pltpu.with_memory_space_constraint(v, pltpu.HBM)


def cast_into_slot(w, chip_id):
    kb, nb = w.shape
    tr = _row_tile(kb)

    def body(chip_ref, w_ref, o_ref):
        del chip_ref
        o_ref[...] = w_ref[...].astype(BF16)
    return pl.pallas_call(
        body, name="cast_into_slot",
        grid_spec=pltpu.PrefetchScalarGridSpec(
            num_scalar_prefetch=1, grid=(kb // tr,),
            in_specs=[pl.BlockSpec((tr, nb), lambda i, chip: (i, 0))],
            out_specs=pl.BlockSpec((None, tr, nb), lambda i, chip: (chip[0], i, 0))),
        out_shape=jax.ShapeDtypeStruct((N_CHIP, kb, nb), BF16))(chip_id, w)


def _split_copies(mode, srcs, lands, send_sems, recv_sems):
    x, y, c = _me()
    out = []
    for t in range(len(lands)):
        for k, chip in enumerate(_other_chips(x, y)):
            if mode == "gather":
                src = dst = lands[t].at[2 * x + y]
                landed = lands[t].at[2 * chip[0] + chip[1]]
            else:
                src, dst, landed = srcs[t].at[2 * chip[0] + chip[1]], lands[t].at[k], lands[t].at[k]
            send = pltpu.make_async_remote_copy(src_ref=src, dst_ref=dst, send_sem=send_sems.at[3 * t + k],
                                                recv_sem=recv_sems.at[3 * t + k], device_id=(*chip, c),
                                                device_id_type=MESH)
            recv = pltpu.make_async_remote_copy(src_ref=src, dst_ref=landed, send_sem=send_sems.at[3 * t + k],
                                                recv_sem=recv_sems.at[3 * t + k], device_id=(*chip, c),
                                                device_id_type=MESH)
            out.append((send, recv))
    return out


def exchange_start(name, mode, srcs, lands, after):
    ns, nl = len(srcs), len(lands)
    na = ns + nl

    def body(*refs):
        src_refs, land_refs = refs[:ns], refs[ns:na]
        send_sems, recv_sems = refs[na + 1], refs[na + 2]
        token = refs[-1]
        for send, _ in _split_copies(mode, src_refs, land_refs, send_sems, recv_sems):
            send.start()
        token[...] = jnp.zeros_like(token)
    arrays = list(srcs) + list(lands)
    outs = pl.pallas_call(
        body, name=name,
        out_shape=(pltpu.SemaphoreType.DMA((3 * nl,)), pltpu.SemaphoreType.DMA((3 * nl,)),
                   *[pltpu.HBM(v.shape, v.dtype) for v in arrays], jax.ShapeDtypeStruct((8, 128), F32)),
        in_specs=[_HBM] * na + [_ANY],
        out_specs=(_SEM, _SEM, *[_HBM] * na, pl.BlockSpec(memory_space=pltpu.VMEM)),
        input_output_aliases={i: 2 + i for i in range(na)},
        compiler_params=pltpu.CompilerParams(has_side_effects=_EFFECT))(*[_in_hbm(v) for v in arrays], after)
    return outs[0], outs[1], list(outs[2:2 + ns]), list(outs[2 + ns:2 + na]), outs[-1]


def exchange_wait(name, mode, send_sems, recv_sems, srcs, lands, after):
    ns, nl = len(srcs), len(lands)
    na = ns + nl

    def body(*refs):
        for _, recv in _split_copies(mode, refs[:ns], refs[ns:na], refs[na], refs[na + 1]):
            recv.wait_send()
            recv.wait_recv()
    arrays = list(srcs) + list(lands)
    outs = pl.pallas_call(
        body, name=name,
        out_shape=[pltpu.HBM(v.shape, v.dtype) for v in arrays],
        in_specs=[_HBM] * na + [_SEM, _SEM, _ANY], out_specs=[_HBM] * na,
        input_output_aliases={i: i for i in range(na)},
        compiler_params=pltpu.CompilerParams(has_side_effects=_EFFECT))(*arrays, send_sems, recv_sems, after)
    return list(outs[:ns]), list(outs[ns:])


def swap_with_sibling(v):
    def body(v_ref, out_ref, send_sem, recv_sem):
        x, y, c = _me()
        cp = pltpu.make_async_remote_copy(src_ref=v_ref, dst_ref=out_ref, send_sem=send_sem, recv_sem=recv_sem,
                                          device_id=(x, y, 1 - c), device_id_type=MESH)
        cp.start()
        cp.wait()
    return pl.pallas_call(
        body, name="swap_with_sibling", out_shape=jax.ShapeDtypeStruct(v.shape, v.dtype),
        in_specs=[pl.BlockSpec(memory_space=pl.ANY)], out_specs=pl.BlockSpec(memory_space=pl.ANY),
        scratch_shapes=[pltpu.SemaphoreType.DMA, pltpu.SemaphoreType.DMA])(v)


def sum_slots(parts):
    n, rows, w = parts.shape
    tr = _row_tile(rows)

    def body(p_ref, o_ref):
        acc = p_ref[0].astype(F32)
        for k in range(1, n):
            acc = acc + p_ref[k].astype(F32)
        o_ref[...] = acc
    return pl.pallas_call(
        body, name="sum_slots", grid=(rows // tr,),
        in_specs=[pl.BlockSpec((n, tr, w), lambda i: (0, i, 0))], out_specs=pl.BlockSpec((tr, w), lambda i: (i, 0)),
        out_shape=jax.ShapeDtypeStruct((rows, w), F32), compiler_params=_params(48))(parts)


def sum_landed(landed, own, chip_id, layer, n_layers, buf):
    n, rows, w = landed.shape
    tr = _row_tile(rows)
    base = layer * (rows // tr)

    def compute(l_ref, g_ref, o_ref):
        acc = g_ref[...].astype(F32)
        for k in range(n):
            acc = acc + l_ref[k].astype(F32)
        o_ref[...] = acc
    in_specs = [pl.BlockSpec((n, tr, w), lambda i, chip: (0, i, 0)),
                pl.BlockSpec((None, tr, w), lambda i, chip: (chip[0], i, 0))]
    out_spec = pl.BlockSpec((tr, w), lambda i, chip: (base + i, 0))
    out_shape = jax.ShapeDtypeStruct((n_layers * rows, w), F32)
    if buf is None:
        def body(chip_ref, l_ref, g_ref, o_ref):
            del chip_ref
            compute(l_ref, g_ref, o_ref)
        return pl.pallas_call(
            body, name="sum_landed",
            grid_spec=pltpu.PrefetchScalarGridSpec(num_scalar_prefetch=1, grid=(rows // tr,), in_specs=in_specs,
                                                   out_specs=out_spec),
            out_shape=out_shape, compiler_params=_params(48))(chip_id, landed, own)

    def body(chip_ref, l_ref, g_ref, buf_ref, o_ref):
        del chip_ref, buf_ref
        compute(l_ref, g_ref, o_ref)
    return pl.pallas_call(
        body, name="sum_landed_into",
        grid_spec=pltpu.PrefetchScalarGridSpec(num_scalar_prefetch=1, grid=(rows // tr,),
                                               in_specs=in_specs + [_ANY], out_specs=out_spec),
        out_shape=out_shape, input_output_aliases={3: 0}, compiler_params=_params(48))(chip_id, landed, own, buf)


def adamw(w, ga, gb, m, v):
    rows, wd = w.shape
    tr = min(_row_tile(rows), 128)
    c1 = 1.0 / (1.0 - ADAM_B1 ** ADAM_STEP)
    c2 = 1.0 / (1.0 - ADAM_B2 ** ADAM_STEP)

    def update(wv, g, mv, vv, g_ref, d_ref, m_ref, v_ref):
        mn = ADAM_B1 * mv + (1.0 - ADAM_B1) * g
        vn = ADAM_B2 * vv + (1.0 - ADAM_B2) * (g * g)
        g_ref[...] = g
        m_ref[...] = mn
        v_ref[...] = vn
        d_ref[...] = -ADAM_LR * ((mn * c1) / (jnp.sqrt(vn * c2) + ADAM_EPS) + ADAM_WD * wv)
    tile = pl.BlockSpec((tr, wd), lambda i: (i, 0))
    out = jax.ShapeDtypeStruct((rows, wd), F32)
    if gb is None:
        def body(w_ref, ga_ref, m_ref, v_ref, g_out, d_out, m_out, v_out):
            update(w_ref[...], ga_ref[...], m_ref[...], v_ref[...], g_out, d_out, m_out, v_out)
        return pl.pallas_call(body, name="adamw", grid=(rows // tr,), in_specs=[tile] * 4,
                              out_specs=[tile] * 4, out_shape=[out] * 4)(w, ga, m, v)

    def body(w_ref, ga_ref, gb_ref, m_ref, v_ref, g_out, d_out, m_out, v_out):
        update(w_ref[...], ga_ref[...] + gb_ref[...], m_ref[...], v_ref[...], g_out, d_out, m_out, v_out)
    return pl.pallas_call(body, name="adamw_sum", grid=(rows // tr,), in_specs=[tile] * 5,
                          out_specs=[tile] * 4, out_shape=[out] * 4)(w, ga, gb, m, v)


def _rope_tables(T, R):
    rows = T // GRID_W
    row = jnp.repeat(jnp.arange(rows), GRID_W).astype(F32)
    col = jnp.tile(jnp.arange(GRID_W), rows).astype(F32)
    n_freq = HEAD_DIM // 4
    inv_freq = ROPE_THETA ** (-jnp.arange(n_freq, dtype=F32) / n_freq)
    ang = jnp.concatenate([row[:, None] * inv_freq, col[:, None] * inv_freq], axis=-1)
    cos, sin = jnp.cos(ang), jnp.sin(ang)
    cs = jnp.tile(cos, (1, 4))
    sn = jnp.tile(jnp.concatenate([-sin, sin], axis=-1), (1, 2))
    pad = R - T
    return (jnp.concatenate([cs, jnp.ones((pad, 128), F32)], axis=0),
            jnp.concatenate([sn, jnp.zeros((pad, 128), F32)], axis=0))


def _pack(parts, mult=8 * 128):
    flat = jnp.concatenate([p.reshape(-1).astype(F32) for p in parts])
    pad = (-flat.shape[0]) % mult
    return jnp.pad(flat, (0, pad)).reshape(-1, 128)


def _unpack(buf, shapes):
    flat = buf.reshape(-1)
    out, o = [], 0
    for s in shapes:
        n = 1
        for d in s:
            n *= d
        out.append(flat[o:o + n].reshape(s))
        o += n
    return out


def kernel(x, c, ctx, c_ctx, w_mod, b_mod, norm_mix, norm_ffn, w_in_ab, conv_a, conv_b, conv_b_bias, ln_b_gain, ln_b_bias, w_out_ab, w_qkv, w_o, sinks, w_up, w_conv_ffn, w_down, final_norm, loss_target, m_c_ctx, m_w_mod, m_b_mod, m_norm_mix, m_norm_ffn, m_w_in_ab, m_conv_a, m_conv_b, m_conv_b_bias, m_ln_b_gain, m_ln_b_bias, m_w_out_ab, m_w_qkv, m_w_o, m_sinks, m_w_up, m_w_conv_ffn, m_w_down, m_final_norm, v_c_ctx, v_w_mod, v_b_mod, v_norm_mix, v_norm_ffn, v_w_in_ab, v_conv_a, v_conv_b, v_conv_b_bias, v_ln_b_gain, v_ln_b_bias, v_w_out_ab, v_w_qkv, v_w_o, v_sinks, v_w_up, v_w_conv_ffn, v_w_down, v_final_norm):
    T, dm = x.shape[1], x.shape[2]
    tc = ctx.shape[1]
    R = T + tc
    depth = w_mod.shape[0]
    ax, ay, ac = lax.axis_index("x"), lax.axis_index("y"), lax.axis_index("c")
    chip = 2 * ax + ay
    dev = 4 * ax + 2 * ay + ac

    small_w = [conv_a, conv_b, w_conv_ffn]
    gathered = allgather8(_pack([c] + small_w)).reshape(N_DEV, -1)
    cond8 = gathered[:, :dm]
    off = dm
    full_small = []
    for wsh in small_w:
        n = wsh.size
        per_chip = gathered[0::2, off:off + n].reshape((N_CHIP,) + wsh.shape)
        full_small.append(jnp.concatenate([per_chip[q] for q in range(N_CHIP)], axis=-1))
        off += n
    conv_a_f, conv_b_f, w_conv_ffn_f = full_small
    cond = jnp.concatenate([cond8, c_ctx[None, :], jnp.zeros((7, dm), F32)], axis=0)

    ns_mod = w_mod.shape[2]
    b_mod_sh = lax.dynamic_slice_in_dim(b_mod, chip * ns_mod, ns_mod, axis=1)[:, None, :]
    mod_sh = adaln_fwd(cond, w_mod, b_mod_sh)
    mod_all = allgather8(mod_sh.reshape(depth * 16, ns_mod)).reshape(N_DEV, depth, 16, ns_mod)
    mod_full = jnp.concatenate([mod_all[2 * q] for q in range(N_CHIP)], axis=-1)
    mine = lax.dynamic_index_in_dim(mod_full, dev, axis=1, keepdims=False)
    mods = jnp.stack([mine, mod_full[:, 8]], axis=1).reshape(depth, 2, 6, dm)

    masters = {"w_in_ab": w_in_ab, "w_out_ab": w_out_ab, "w_qkv": w_qkv, "w_o": w_o, "w_up": w_up, "w_down": w_down}
    chip_id = chip.astype(jnp.int32).reshape(1)

    def half_weights(l, half):
        if half == 1:
            return [("w_up", l), ("w_down", l)]
        return [("w_in_ab", l // 2), ("w_out_ab", l // 2)] if l % 2 == 0 else [("w_qkv", l // 2), ("w_o", l // 2)]
    in_flight, after = {}, mods
    for l in range(depth):
        for half in range(2):
            lands = [cast_into_slot(masters[n][j], chip_id) for n, j in half_weights(l, half)]
            send_sems, recv_sems, _, lands, after = exchange_start(f"gather_start_{l}_{half}", "gather", [], lands, after)
            in_flight[l, half] = (send_sems, recv_sems, lands)
    mods = mods + after[0, 0]

    def gathered_weights(l, half, after):
        send_sems, recv_sems, lands = in_flight[l, half]
        _, landed = exchange_wait(f"gather_wait_{l}_{half}", "gather", send_sems, recv_sems, [], lands, after)
        return dict(zip([n for n, _ in half_weights(l, half)], landed))

    cs, sn = _rope_tables(T, R)
    bias = window_bias(T, R)
    sinks_flat = sinks.reshape(-1)

    xs = jnp.concatenate([x[0], ctx[0]], axis=0)
    saved, W = [], []
    h1 = norm_mod_fwd(xs, norm_mix[0][None], mods[0], 0, T)
    for l in range(depth):
        e = l // 2
        wl = gathered_weights(l, 0, h1)
        W.append(wl)
        s = {"x0": xs, "h1": h1}
        if l % 2 == 0:
            p = mm_nn(h1, wl["w_in_ab"], "col", BF16)
            yab = convmix_fwd(p, conv_a_f[e], conv_b_f[e], conv_b_bias[e][None], ln_b_gain[e][None],
                              ln_b_bias[e][None], T)
            y1 = mm_nn(yab, wl["w_out_ab"], "row", F32)
            s.update(p=p, mix=yab)
        else:
            qkv = mm_nn(h1, wl["w_qkv"], "col", F32)
            qkvr = rope_fwd(qkv, cs, sn)
            att = attn_fwd(qkvr, sinks_flat[e * N_HEADS:(e + 1) * N_HEADS], bias, T)
            y1 = mm_nn(att, wl["w_o"], "row", F32)
            s.update(qkvr=qkvr, mix=att)
        x1, h2 = resid_norm_fwd(xs, y1, norm_ffn[l][None], mods[l], mods[l], 2, 3, T)
        wl.update(gathered_weights(l, 1, h2))
        u = mm_nn(h2, wl["w_up"], "col", BF16)
        z = ffnconv_fwd(u, w_conv_ffn_f[l], T)
        y2 = mm_nn(z, wl["w_down"], "row", F32)
        if l + 1 < depth:
            xs, h1 = resid_norm_fwd(x1, y2, norm_mix[l + 1][None], mods[l], mods[l + 1], 5, 0, T)
        else:
            xs = resid_fwd(x1, y2, mods[l], 5, T)
        s.update(y1=y1, x1=x1, h2=h2, u=u, z=z, y2=y2)
        saved.append(s)

    loss_part, dx, d_final = loss_head(xs, final_norm[None], loss_target[0], T)
    loss = lax.psum(loss_part[0, 0], ("x", "y", "c"))

    d_mods, d_norm_mix, d_norm_ffn = [None] * depth, [None] * depth, [None] * depth
    d_conv_a, d_conv_b, d_vecs, d_sinks, d_wc = [None] * 2, [None] * 2, [None] * 2, [None] * 2, [None] * depth
    dss1, dss2, dg1, dg2 = [None] * depth, [None] * depth, [None] * depth, [None] * depth
    scattering = {}

    def scatter(l, half, G, after):
        grads_h = [G[n] for n, _ in half_weights(l, half)]
        lands = [lax.empty((N_CHIP - 1, *g.shape[1:]), g.dtype) for g in grads_h]
        send_sems, recv_sems, grads_h, lands, token = exchange_start(
            f"scatter_start_{l}_{half}", "scatter", grads_h, lands, after)
        scattering[l, half] = (send_sems, recv_sems, grads_h, lands)
        return token

    dy2, dg2[depth - 1] = resid_bwd(dx, saved[depth - 1]["y2"], mods[depth - 1], 5, T)
    for l in reversed(range(depth)):
        e = l // 2
        s, wl = saved[l], W[l]
        G = {}
        G["w_down"] = mm_tn(s["z"], dy2, "row", wl["w_down"])
        dz = mm_nt(dy2, wl["w_down"], "row", BF16)
        duc, d_wc[l] = ffnconv_bwd1(dz, s["u"], w_conv_ffn_f[l], T)
        du = ffnconv_bwd2(duc, w_conv_ffn_f[l], T)
        G["w_up"] = mm_tn(s["h2"], du, "col", wl["w_up"])
        dh2 = mm_nt(du, wl["w_up"], "col", F32)
        token = scatter(l, 1, G, dh2)
        mod_l = mods[l] + token[0, 0]
        dx, dy1, dss2[l], d_norm_ffn[l], dg1[l] = norm_resid_bwd(
            dh2, s["x1"], norm_ffn[l][None], mod_l, dx, s["y1"], mod_l, 3, 2, T)
        if l % 2 == 0:
            G["w_out_ab"] = mm_tn(s["mix"], dy1, "row", wl["w_out_ab"])
            dyab = mm_nt(dy1, wl["w_out_ab"], "row", F32)
            dmid, d_conv_a[e], d_conv_b[e], d_vecs[e] = convmix_bwd1(
                dyab, s["p"], conv_a_f[e], conv_b_f[e], conv_b_bias[e][None], ln_b_gain[e][None],
                ln_b_bias[e][None], T)
            dp = convmix_bwd2(dmid, s["p"], conv_a_f[e], conv_b_f[e], T)
            G["w_in_ab"] = mm_tn(s["h1"], dp, "col", wl["w_in_ab"])
            dh1 = mm_nt(dp, wl["w_in_ab"], "col", F32)
        else:
            G["w_o"] = mm_tn(s["mix"], dy1, "row", wl["w_o"])
            datt = mm_nt(dy1, wl["w_o"], "row", BF16)
            dq, dks, dvs, dkc, dvc, d_sinks[e] = attn_bwd(
                s["qkvr"], datt, sinks_flat[e * N_HEADS:(e + 1) * N_HEADS], bias, T)
            dqkv = rope_bwd(dq, dks, dvs, dkc, dvc, cs, sn, T)
            G["w_qkv"] = mm_tn(s["h1"], dqkv, "col", wl["w_qkv"])
            dh1 = mm_nt(dqkv, wl["w_qkv"], "col", F32)
        token = scatter(l, 0, G, dh1)
        mod_l = mods[l] + token[0, 0]
        if l > 0:
            dx, dy2, dss1[l], d_norm_mix[l], dg2[l - 1] = norm_resid_bwd(
                dh1, s["x0"], norm_mix[l][None], mod_l, dx, saved[l - 1]["y2"], mods[l - 1], 0, 5, T)
        else:
            dx, dss1[l], d_norm_mix[l] = norm_mod_bwd(dh1, s["x0"], norm_mix[l][None], mod_l, dx, 0, T)
    grad_x = dx[:T][None]
    for l in range(depth):
        a1, a2 = dss1[l].sum(2), dss2[l].sum(2)
        d_mods[l] = jnp.stack([a1[:, 0], a1[:, 1], dg1[l].sum(1), a2[:, 0], a2[:, 1], dg2[l].sum(1)], axis=1)

    d_mods = jnp.stack(d_mods)
    summed_parts = [
        d_mods[:, 1],
        jnp.stack(d_norm_mix).sum(1), jnp.stack(d_norm_ffn).sum(1),
        jnp.stack(d_conv_a).sum(2), jnp.stack(d_conv_b).sum(2),
        jnp.stack(d_vecs).sum(2),
        jnp.stack(d_sinks)[:, 0, :N_HEADS],
        jnp.stack(d_wc).sum(2), d_final.sum(0)]
    summed_shapes = [p.shape for p in summed_parts]
    n_own = depth * 6 * dm
    pack = _pack([d_mods[:, 0]] + summed_parts)
    parts = allgather8(pack).reshape(N_DEV, -1, 128)
    total = sum_slots(parts)
    own_rows = parts.reshape(N_DEV, -1)[:, :n_own].reshape(N_DEV, depth, 6 * dm)
    (dmod_ctx, g_norm_mix, g_norm_ffn, g_conv_a, g_conv_b, g_vecs, g_sinks, g_wc, g_final) = _unpack(
        total.reshape(-1)[n_own:], summed_shapes)
    dmod_rows = jnp.concatenate([jnp.moveaxis(own_rows, 0, 1), dmod_ctx.reshape(depth, 1, 6 * dm),
                                 jnp.zeros((depth, 7, 6 * dm), F32)], axis=1)
    g_b_mod = dmod_rows.sum(1)
    dmod_sh = lax.dynamic_slice_in_dim(dmod_rows, chip * ns_mod, ns_mod, axis=2)
    g_w_mod, dsilu = adaln_bwd(cond, dmod_sh, w_mod)
    dsilu_all = allgather8(dsilu[8:16]).reshape(N_DEV, 8, dm)
    dsilu_ctx = sum_slots(dsilu_all[0::2])[0]
    sg = jax.nn.sigmoid(c_ctx)
    g_c_ctx = dsilu_ctx * (sg * (1.0 + c_ctx * (1.0 - sg)))

    def shard_cols(full, width):
        return lax.dynamic_slice_in_dim(full, chip * width, width, axis=full.ndim - 1)
    g_conv_a_s = shard_cols(g_conv_a, conv_a.shape[-1])
    g_conv_b_s = shard_cols(g_conv_b, conv_b.shape[-1])
    g_wc_s = shard_cols(g_wc, w_conv_ffn.shape[-1])

    grads, deltas, new_m, new_v = {}, {}, {}, {}

    def step_2d(name, wv, ga, gb, mv, vv):
        shp = wv.shape
        r2 = lambda t: t.reshape(-1, shp[-1])
        g, d, mn, vn = adamw(r2(wv), r2(ga), None if gb is None else r2(gb), r2(mv), r2(vv))
        grads[name], deltas[name], new_m[name], new_v[name] = (t.reshape(shp) for t in (g, d, mn, vn))

    sums = {n: None for n in masters}
    for l in reversed(range(depth)):
        for half in (1, 0):
            send_sems, recv_sems, grads_h, lands = scattering[l, half]
            grads_h, landed = exchange_wait(f"scatter_wait_{l}_{half}", "scatter", send_sems, recv_sems, grads_h,
                                            lands, token)
            for (n, j), own, arr in zip(half_weights(l, half), grads_h, landed):
                sums[n] = sum_landed(arr, own, chip_id, j, masters[n].shape[0], sums[n])
    moments = {"w_in_ab": (m_w_in_ab, v_w_in_ab), "w_out_ab": (m_w_out_ab, v_w_out_ab),
               "w_qkv": (m_w_qkv, v_w_qkv), "w_o": (m_w_o, v_w_o), "w_up": (m_w_up, v_w_up),
               "w_down": (m_w_down, v_w_down)}
    for name, wv in masters.items():
        other = swap_with_sibling(sums[name])
        step_2d(name, wv, sums[name].reshape(wv.shape), other.reshape(wv.shape), *moments[name])
    step_2d("w_mod", w_mod, g_w_mod, None, m_w_mod, v_w_mod)

    small = [("c_ctx", c_ctx, g_c_ctx, m_c_ctx, v_c_ctx), ("b_mod", b_mod, g_b_mod, m_b_mod, v_b_mod),
             ("norm_mix", norm_mix, g_norm_mix, m_norm_mix, v_norm_mix),
             ("norm_ffn", norm_ffn, g_norm_ffn, m_norm_ffn, v_norm_ffn),
             ("conv_a", conv_a, g_conv_a_s, m_conv_a, v_conv_a), ("conv_b", conv_b, g_conv_b_s, m_conv_b, v_conv_b),
             ("conv_b_bias", conv_b_bias, g_vecs[:, 0], m_conv_b_bias, v_conv_b_bias),
             ("ln_b_gain", ln_b_gain, g_vecs[:, 1], m_ln_b_gain, v_ln_b_gain),
             ("ln_b_bias", ln_b_bias, g_vecs[:, 2], m_ln_b_bias, v_ln_b_bias),
             ("sinks", sinks, g_sinks, m_sinks, v_sinks),
             ("w_conv_ffn", w_conv_ffn, g_wc_s, m_w_conv_ffn, v_w_conv_ffn),
             ("final_norm", final_norm, g_final, m_final_norm, v_final_norm)]
    shapes = [t[1].shape for t in small]
    packed = [_pack([t[k] for t in small]) for k in (1, 2, 3, 4)]
    n_real = sum(t[1].size for t in small)
    lane_id = jnp.arange(packed[3].size).reshape(packed[3].shape)
    packed[3] = jnp.where(lane_id < n_real, packed[3], 1.0)
    outs = adamw(packed[0], packed[1], None, packed[2], packed[3])
    for (name, *_), g, d, mn, vn in zip(small, *[_unpack(o, shapes) for o in outs]):
        grads[name], deltas[name], new_m[name], new_v[name] = g, d, mn, vn

    order = ["c_ctx", "w_mod", "b_mod", "norm_mix", "norm_ffn", "w_in_ab", "conv_a", "conv_b", "conv_b_bias",
             "ln_b_gain", "ln_b_bias", "w_out_ab", "w_qkv", "w_o", "sinks", "w_up", "w_conv_ffn", "w_down",
             "final_norm"]
    return (loss, grad_x, *[grads[n] for n in order], *[deltas[n] for n in order],
            *[new_m[n] for n in order], *[new_v[n] for n in order])
```

```python
import jax
import jax.numpy as jnp
from jax import lax
from jax.experimental import pallas as pl
from jax.experimental.pallas import tpu as pltpu

F32 = jnp.float32
BF16 = jnp.bfloat16
MESH = pl.DeviceIdType.MESH

EPS = 1e-6
NEG_INF = -1e30
GRID_W = 64
HEAD_DIM = 64
N_HEADS = 16
WINDOW = 128
QB = 128
ROPE_THETA = 10000.0
A_W = 512
B_CONV = 31
D_FF = 2816
ADAM_LR, ADAM_B1, ADAM_B2, ADAM_EPS, ADAM_WD, ADAM_STEP = 0.001, 0.9, 0.999, 1e-8, 0.01, 10

TMR = 256
HALO = 16
N_DEV = 8
N_CHIP = 4


def _params(vmem_mb=None):
    if vmem_mb is None:
        return pltpu.CompilerParams()
    return pltpu.CompilerParams(vmem_limit_bytes=vmem_mb * 1024 * 1024)


def _row_tile(rows, cap=768):
    for t in (2816, 1408, 768, 704, 512, 384, 256, 128, 64, 32, 16, 8):
        if t <= cap and rows % t == 0:
            return t
    raise ValueError(rows)


def _colsum8(v):
    r, c = v.shape
    return v.reshape(r // 8, 8, c).sum(axis=0)


def _sigmoid(v):
    return 0.5 * jnp.tanh(0.5 * v) + 0.5


def mm_nn(a, w, kind, out_dtype):
    R = a.shape[0]
    _, kb, nb = w.shape
    tm = _row_tile(R)
    resident = pl.BlockSpec((N_CHIP, kb, nb), lambda i: (0, 0, 0), pipeline_mode=pl.Buffered(1))
    if kind == "col":
        def body(a_ref, w_ref, o_ref):
            av = a_ref[...].astype(BF16)
            for q in range(N_CHIP):
                o_ref[:, q * nb:(q + 1) * nb] = jnp.dot(av, w_ref[q], preferred_element_type=F32).astype(o_ref.dtype)
        return pl.pallas_call(
            body, name="mm_nn_col", grid=(R // tm,),
            in_specs=[pl.BlockSpec((tm, kb), lambda i: (i, 0)), resident],
            out_specs=pl.BlockSpec((tm, N_CHIP * nb), lambda i: (i, 0)),
            out_shape=jax.ShapeDtypeStruct((R, N_CHIP * nb), out_dtype),
            compiler_params=_params(48))(a, w)

    def body(a_ref, w_ref, o_ref):
        wv = w_ref[...].reshape(N_CHIP * kb, nb)
        o_ref[...] = jnp.dot(a_ref[...].astype(BF16), wv, preferred_element_type=F32).astype(o_ref.dtype)
    return pl.pallas_call(
        body, name="mm_nn_row", grid=(R // tm,),
        in_specs=[pl.BlockSpec((tm, N_CHIP * kb), lambda i: (i, 0)), resident],
        out_specs=pl.BlockSpec((tm, nb), lambda i: (i, 0)),
        out_shape=jax.ShapeDtypeStruct((R, nb), out_dtype),
        compiler_params=_params(48))(a, w)


def mm_nt(d, w, kind, out_dtype):
    R = d.shape[0]
    _, kb, nb = w.shape
    tm = _row_tile(R)
    contract_last = (((1,), (1,)), ((), ()))
    resident = pl.BlockSpec((N_CHIP, kb, nb), lambda i: (0, 0, 0), pipeline_mode=pl.Buffered(1))
    if kind == "col":
        def body(d_ref, w_ref, o_ref):
            acc = None
            for q in range(N_CHIP):
                t = lax.dot_general(d_ref[:, q * nb:(q + 1) * nb].astype(BF16), w_ref[q], contract_last,
                                    preferred_element_type=F32)
                acc = t if acc is None else acc + t
            o_ref[...] = acc.astype(o_ref.dtype)
        return pl.pallas_call(
            body, name="mm_nt_col", grid=(R // tm,),
            in_specs=[pl.BlockSpec((tm, N_CHIP * nb), lambda i: (i, 0)), resident],
            out_specs=pl.BlockSpec((tm, kb), lambda i: (i, 0)),
            out_shape=jax.ShapeDtypeStruct((R, kb), out_dtype),
            compiler_params=_params(48))(d, w)

    def body(d_ref, w_ref, o_ref):
        wv = w_ref[...].reshape(N_CHIP * kb, nb)
        o_ref[...] = lax.dot_general(d_ref[...].astype(BF16), wv, contract_last,
                                     preferred_element_type=F32).astype(o_ref.dtype)
    return pl.pallas_call(
        body, name="mm_nt_row", grid=(R // tm,),
        in_specs=[pl.BlockSpec((tm, nb), lambda i: (i, 0)), resident],
        out_specs=pl.BlockSpec((tm, N_CHIP * kb), lambda i: (i, 0)),
        out_shape=jax.ShapeDtypeStruct((R, N_CHIP * kb), out_dtype),
        compiler_params=_params(48))(d, w)


def mm_tn(a, d, kind, like):
    R = a.shape[0]
    _, kb, nb = like.shape
    tm = _row_tile(R, 1408 if kind == "col" else 768)
    nsteps = R // tm
    contract_rows = (((0,), (0,)), ((), ()))
    out_shape = jax.ShapeDtypeStruct(like.shape, BF16)

    def accumulate(a_ref, d_ref, acc_ref):
        @pl.when(pl.program_id(1) == 0)
        def _():
            acc_ref[...] = jnp.zeros_like(acc_ref)
        acc_ref[...] += lax.dot_general(a_ref[...].astype(BF16), d_ref[...].astype(BF16), contract_rows,
                                        preferred_element_type=F32)
    if kind == "col":
        def body(a_ref, d_ref, o_ref, acc_ref):
            accumulate(a_ref, d_ref, acc_ref)

            @pl.when(pl.program_id(1) == nsteps - 1)
            def _():
                o_ref[...] = acc_ref[...].astype(BF16)
        return pl.pallas_call(
            body, name="mm_tn_col", grid=(N_CHIP, nsteps),
            in_specs=[pl.BlockSpec((tm, kb), lambda q, i: (i, 0)), pl.BlockSpec((tm, nb), lambda q, i: (i, q))],
            out_specs=pl.BlockSpec((None, kb, nb), lambda q, i: (q, 0, 0)), out_shape=out_shape,
            scratch_shapes=[pltpu.VMEM((kb, nb), F32)], compiler_params=_params(48))(a, d)
    tn = 512

    def body(a_ref, d_ref, o_ref, acc_ref):
        accumulate(a_ref, d_ref, acc_ref)

        @pl.when(pl.program_id(1) == nsteps - 1)
        def _():
            o_ref[...] = acc_ref[...].astype(BF16).reshape(N_CHIP, kb, tn)
    return pl.pallas_call(
        body, name="mm_tn_row", grid=(nb // tn, nsteps),
        in_specs=[pl.BlockSpec((tm, N_CHIP * kb), lambda n, i: (i, 0)), pl.BlockSpec((tm, tn), lambda n, i: (i, n))],
        out_specs=pl.BlockSpec((N_CHIP, kb, tn), lambda n, i: (0, 0, n)), out_shape=out_shape,
        scratch_shapes=[pltpu.VMEM((N_CHIP * kb, tn), F32)], compiler_params=_params(48))(a, d)


def _seg(i, T):
    return (i >= T // TMR).astype(jnp.int32)


def norm_mod_fwd(x, nw, mod, k, T):
    R, dm = x.shape

    def body(x_ref, nw_ref, mod_ref, h_ref):
        seg = _seg(pl.program_id(0), T)
        sh = mod_ref[seg, pl.ds(k, 1), :]
        sc = mod_ref[seg, pl.ds(k + 1, 1), :]
        xv = x_ref[...]
        r = lax.rsqrt(jnp.mean(xv * xv, axis=-1, keepdims=True) + EPS)
        h_ref[...] = ((xv * r * nw_ref[...]) * (1.0 + sc) + sh).astype(BF16)
    return pl.pallas_call(
        body, name="norm_mod_fwd", grid=(R // TMR,),
        in_specs=[pl.BlockSpec((TMR, dm), lambda i: (i, 0)),
                  pl.BlockSpec((1, dm), lambda i: (0, 0)),
                  pl.BlockSpec((2, 6, dm), lambda i: (0, 0, 0))],
        out_specs=pl.BlockSpec((TMR, dm), lambda i: (i, 0)),
        out_shape=jax.ShapeDtypeStruct((R, dm), BF16))(x, nw, mod)


def norm_mod_bwd(dh, x, nw, mod, dxr, k, T):
    R, dm = x.shape

    def body(dh_ref, x_ref, nw_ref, mod_ref, dxr_ref, dx_ref, dmod_ref, dnw_ref):
        i = pl.program_id(0)
        seg = _seg(i, T)

        @pl.when(i == 0)
        def _():
            dmod_ref[...] = jnp.zeros_like(dmod_ref)
            dnw_ref[...] = jnp.zeros_like(dnw_ref)
        sc = mod_ref[seg, pl.ds(k + 1, 1), :]
        nwv = nw_ref[...]
        xv = x_ref[...]
        r = lax.rsqrt(jnp.mean(xv * xv, axis=-1, keepdims=True) + EPS)
        xh = xv * r
        dhv = dh_ref[...]
        dmod_ref[seg, 0] += _colsum8(dhv)
        dmod_ref[seg, 1] += _colsum8(dhv * (xh * nwv))
        dn = dhv * (1.0 + sc)
        dnw_ref[...] += _colsum8(dn * xh)
        dxh = dn * nwv
        dx = r * (dxh - xh * jnp.mean(dxh * xh, axis=-1, keepdims=True))
        dx_ref[...] = dxr_ref[...] + dx
    tile = pl.BlockSpec((TMR, dm), lambda i: (i, 0))
    return pl.pallas_call(
        body, name="norm_mod_bwd", grid=(R // TMR,),
        in_specs=[tile, tile, pl.BlockSpec((1, dm), lambda i: (0, 0)),
                  pl.BlockSpec((2, 6, dm), lambda i: (0, 0, 0)), tile],
        out_specs=[tile, pl.BlockSpec((2, 2, 8, dm), lambda i: (0, 0, 0, 0)),
                   pl.BlockSpec((8, dm), lambda i: (0, 0))],
        out_shape=[jax.ShapeDtypeStruct((R, dm), F32), jax.ShapeDtypeStruct((2, 2, 8, dm), F32),
                   jax.ShapeDtypeStruct((8, dm), F32)])(dh, x, nw, mod, dxr)


def resid_norm_fwd(x, y, nw, mod_g, mod_n, kg, kn, T):
    R, dm = x.shape

    def body(x_ref, y_ref, nw_ref, mg_ref, mn_ref, xo_ref, h_ref):
        seg = _seg(pl.program_id(0), T)
        xv = x_ref[...] + mg_ref[seg, pl.ds(kg, 1), :] * y_ref[...]
        xo_ref[...] = xv
        r = lax.rsqrt(jnp.mean(xv * xv, axis=-1, keepdims=True) + EPS)
        h_ref[...] = ((xv * r * nw_ref[...]) * (1.0 + mn_ref[seg, pl.ds(kn + 1, 1), :])
                      + mn_ref[seg, pl.ds(kn, 1), :]).astype(BF16)
    tile = pl.BlockSpec((TMR, dm), lambda i: (i, 0))
    modspec = pl.BlockSpec((2, 6, dm), lambda i: (0, 0, 0))
    return pl.pallas_call(
        body, name="resid_norm_fwd", grid=(R // TMR,),
        in_specs=[tile, tile, pl.BlockSpec((1, dm), lambda i: (0, 0)), modspec, modspec],
        out_specs=[tile, tile],
        out_shape=[jax.ShapeDtypeStruct((R, dm), F32), jax.ShapeDtypeStruct((R, dm), BF16)])(x, y, nw, mod_g, mod_n)


def norm_resid_bwd(dh, x, nw, mod_n, dxr, y, mod_g, kn, kg, T):
    R, dm = x.shape

    def body(dh_ref, x_ref, nw_ref, mn_ref, dxr_ref, y_ref, mg_ref, dx_ref, dy_ref, dmod_ref, dnw_ref, dg_ref):
        i = pl.program_id(0)
        seg = _seg(i, T)

        @pl.when(i == 0)
        def _():
            dmod_ref[...] = jnp.zeros_like(dmod_ref)
            dnw_ref[...] = jnp.zeros_like(dnw_ref)
            dg_ref[...] = jnp.zeros_like(dg_ref)
        sc = mn_ref[seg, pl.ds(kn + 1, 1), :]
        nwv = nw_ref[...]
        xv = x_ref[...]
        r = lax.rsqrt(jnp.mean(xv * xv, axis=-1, keepdims=True) + EPS)
        xh = xv * r
        dhv = dh_ref[...]
        dmod_ref[seg, 0] += _colsum8(dhv)
        dmod_ref[seg, 1] += _colsum8(dhv * (xh * nwv))
        dn = dhv * (1.0 + sc)
        dnw_ref[...] += _colsum8(dn * xh)
        dxh = dn * nwv
        dx = dxr_ref[...] + r * (dxh - xh * jnp.mean(dxh * xh, axis=-1, keepdims=True))
        dx_ref[...] = dx
        dy_ref[...] = (mg_ref[seg, pl.ds(kg, 1), :] * dx).astype(BF16)
        dg_ref[seg] += _colsum8(dx * y_ref[...])
    tile = pl.BlockSpec((TMR, dm), lambda i: (i, 0))
    modspec = pl.BlockSpec((2, 6, dm), lambda i: (0, 0, 0))
    return pl.pallas_call(
        body, name="norm_resid_bwd", grid=(R // TMR,),
        in_specs=[tile, tile, pl.BlockSpec((1, dm), lambda i: (0, 0)), modspec, tile, tile, modspec],
        out_specs=[tile, tile, pl.BlockSpec((2, 2, 8, dm), lambda i: (0, 0, 0, 0)),
                   pl.BlockSpec((8, dm), lambda i: (0, 0)), pl.BlockSpec((2, 8, dm), lambda i: (0, 0, 0))],
        out_shape=[jax.ShapeDtypeStruct((R, dm), F32), jax.ShapeDtypeStruct((R, dm), BF16),
                   jax.ShapeDtypeStruct((2, 2, 8, dm), F32), jax.ShapeDtypeStruct((8, dm), F32),
                   jax.ShapeDtypeStruct((2, 8, dm), F32)])(dh, x, nw, mod_n, dxr, y, mod_g)


def mm_resid_norm_fwd(a, w, x, nw, mod_g, mod_n, kg, kn, T):
    R, dm = x.shape
    _, kb, nb = w.shape

    def body(a_ref, w_ref, x_ref, nw_ref, mg_ref, mn_ref, y_ref, xo_ref, h_ref):
        seg = _seg(pl.program_id(0), T)
        yv = jnp.dot(a_ref[...].astype(BF16), w_ref[...].reshape(N_CHIP * kb, nb), preferred_element_type=F32)
        y_ref[...] = yv
        xv = x_ref[...] + mg_ref[seg, pl.ds(kg, 1), :] * yv
        xo_ref[...] = xv
        r = lax.rsqrt(jnp.mean(xv * xv, axis=-1, keepdims=True) + EPS)
        h_ref[...] = ((xv * r * nw_ref[...]) * (1.0 + mn_ref[seg, pl.ds(kn + 1, 1), :])
                      + mn_ref[seg, pl.ds(kn, 1), :]).astype(BF16)
    tile = pl.BlockSpec((TMR, dm), lambda i: (i, 0))
    modspec = pl.BlockSpec((2, 6, dm), lambda i: (0, 0, 0))
    return pl.pallas_call(
        body, name="mm_resid_norm_fwd", grid=(R // TMR,),
        in_specs=[pl.BlockSpec((TMR, N_CHIP * kb), lambda i: (i, 0)),
                  pl.BlockSpec((N_CHIP, kb, nb), lambda i: (0, 0, 0), pipeline_mode=pl.Buffered(1)),
                  tile, pl.BlockSpec((1, dm), lambda i: (0, 0)), modspec, modspec],
        out_specs=[tile, tile, tile],
        out_shape=[jax.ShapeDtypeStruct((R, dm), F32), jax.ShapeDtypeStruct((R, dm), F32),
                   jax.ShapeDtypeStruct((R, dm), BF16)],
        compiler_params=_params(48))(a, w, x, nw, mod_g, mod_n)


def mm_norm_resid_bwd(d, w, x, nw, mod_n, dxr, y, mod_g, kn, kg, T):
    R, dm = x.shape
    _, kb, nb = w.shape
    with_resid = y is not None
    contract_last = (((1,), (1,)), ((), ()))

    def body(*refs):
        if with_resid:
            d_ref, w_ref, x_ref, nw_ref, mn_ref, dxr_ref, y_ref, mg_ref, dx_ref, dy_ref, dmod_ref, dnw_ref, dg_ref = refs
        else:
            d_ref, w_ref, x_ref, nw_ref, mn_ref, dxr_ref, dx_ref, dmod_ref, dnw_ref = refs
        i = pl.program_id(0)
        seg = _seg(i, T)

        @pl.when(i == 0)
        def _():
            dmod_ref[...] = jnp.zeros_like(dmod_ref)
            dnw_ref[...] = jnp.zeros_like(dnw_ref)
            if with_resid:
                dg_ref[...] = jnp.zeros_like(dg_ref)
        dhv = None
        for q in range(N_CHIP):
            t = lax.dot_general(d_ref[:, q * nb:(q + 1) * nb].astype(BF16), w_ref[q], contract_last,
                                preferred_element_type=F32)
            dhv = t if dhv is None else dhv + t
        sc = mn_ref[seg, pl.ds(kn + 1, 1), :]
        nwv = nw_ref[...]
        xv = x_ref[...]
        r = lax.rsqrt(jnp.mean(xv * xv, axis=-1, keepdims=True) + EPS)
        xh = xv * r
        dmod_ref[seg, 0] += _colsum8(dhv)
        dmod_ref[seg, 1] += _colsum8(dhv * (xh * nwv))
        dn = dhv * (1.0 + sc)
        dnw_ref[...] += _colsum8(dn * xh)
        dxh = dn * nwv
        dx = dxr_ref[...] + r * (dxh - xh * jnp.mean(dxh * xh, axis=-1, keepdims=True))
        dx_ref[...] = dx
        if with_resid:
            dy_ref[...] = (mg_ref[seg, pl.ds(kg, 1), :] * dx).astype(BF16)
            dg_ref[seg] += _colsum8(dx * y_ref[...])
    tile = pl.BlockSpec((TMR, dm), lambda i: (i, 0))
    modspec = pl.BlockSpec((2, 6, dm), lambda i: (0, 0, 0))
    in_specs = [pl.BlockSpec((TMR, N_CHIP * nb), lambda i: (i, 0)),
                pl.BlockSpec((N_CHIP, kb, nb), lambda i: (0, 0, 0), pipeline_mode=pl.Buffered(1)),
                tile, pl.BlockSpec((1, dm), lambda i: (0, 0)), modspec, tile]
    acc_specs = [pl.BlockSpec((2, 2, 8, dm), lambda i: (0, 0, 0, 0)), pl.BlockSpec((8, dm), lambda i: (0, 0))]
    acc_shapes = [jax.ShapeDtypeStruct((2, 2, 8, dm), F32), jax.ShapeDtypeStruct((8, dm), F32)]
    if with_resid:
        return pl.pallas_call(
            body, name="mm_norm_resid_bwd", grid=(R // TMR,),
            in_specs=in_specs + [tile, modspec],
            out_specs=[tile, tile] + acc_specs + [pl.BlockSpec((2, 8, dm), lambda i: (0, 0, 0))],
            out_shape=[jax.ShapeDtypeStruct((R, dm), F32), jax.ShapeDtypeStruct((R, dm), BF16)] + acc_shapes
            + [jax.ShapeDtypeStruct((2, 8, dm), F32)],
            compiler_params=_params(48))(d, w, x, nw, mod_n, dxr, y, mod_g)
    return pl.pallas_call(
        body, name="mm_norm_bwd", grid=(R // TMR,), in_specs=in_specs,
        out_specs=[tile] + acc_specs, out_shape=[jax.ShapeDtypeStruct((R, dm), F32)] + acc_shapes,
        compiler_params=_params(48))(d, w, x, nw, mod_n, dxr)


def resid_fwd(x, y, mod, k, T):
    R, dm = x.shape

    def body(x_ref, y_ref, mod_ref, o_ref):
        seg = _seg(pl.program_id(0), T)
        o_ref[...] = x_ref[...] + mod_ref[seg, pl.ds(k, 1), :] * y_ref[...]
    tile = pl.BlockSpec((TMR, dm), lambda i: (i, 0))
    return pl.pallas_call(
        body, name="resid_fwd", grid=(R // TMR,),
        in_specs=[tile, tile, pl.BlockSpec((2, 6, dm), lambda i: (0, 0, 0))],
        out_specs=tile, out_shape=jax.ShapeDtypeStruct((R, dm), F32))(x, y, mod)


def resid_bwd(dxn, y, mod, k, T):
    R, dm = dxn.shape

    def body(dx_ref, y_ref, mod_ref, dy_ref, dg_ref):
        i = pl.program_id(0)
        seg = _seg(i, T)

        @pl.when(i == 0)
        def _():
            dg_ref[...] = jnp.zeros_like(dg_ref)
        dxv = dx_ref[...]
        dy_ref[...] = (mod_ref[seg, pl.ds(k, 1), :] * dxv).astype(BF16)
        dg_ref[seg] += _colsum8(dxv * y_ref[...])
    tile = pl.BlockSpec((TMR, dm), lambda i: (i, 0))
    return pl.pallas_call(
        body, name="resid_bwd", grid=(R // TMR,),
        in_specs=[tile, tile, pl.BlockSpec((2, 6, dm), lambda i: (0, 0, 0))],
        out_specs=[tile, pl.BlockSpec((2, 8, dm), lambda i: (0, 0, 0))],
        out_shape=[jax.ShapeDtypeStruct((R, dm), BF16), jax.ShapeDtypeStruct((2, 8, dm), F32)])(dxn, y, mod)


def _halo_specs(width, R):
    nblk = R // HALO
    per = TMR // HALO
    return (pl.BlockSpec((HALO, width), lambda i: (jnp.maximum(i * per - 1, 0), 0)),
            pl.BlockSpec((TMR, width), lambda i: (i, 0)),
            pl.BlockSpec((HALO, width), lambda i: (jnp.minimum((i + 1) * per, nblk - 1), 0)))


def _halo_live(i, T, R):
    nl = T // TMR
    return (i != 0) & (i != nl), (i != nl - 1) & (i != R // TMR - 1)


def _ext(refs, c0, cw, live, halo=HALO):
    pref, ref, nref = refs
    before = jnp.where(live[0], pref[:, c0:c0 + cw].astype(F32)[HALO - halo:], 0.0)
    after = jnp.where(live[1], nref[:, c0:c0 + cw].astype(F32)[:halo], 0.0)
    return jnp.concatenate([before, ref[:, c0:c0 + cw].astype(F32), after], axis=0)


def _at(ext, off, halo=HALO):
    n = ext.shape[0]
    s = (-off) % n
    y = pltpu.roll(ext, s, 0) if s else ext
    return y[halo:halo + TMR]


def ffnconv_fwd(u, wc, T):
    R, w2 = u.shape
    cw = 256

    def body(up_ref, u_ref, un_ref, wc_ref, z_ref):
        live = _halo_live(pl.program_id(0), T, R)

        def conv(c0):
            e = _ext((up_ref, u_ref, un_ref), c0, cw, live, 8)
            return (wc_ref[pl.ds(0, 1), c0:c0 + cw] * _at(e, -1, 8) + wc_ref[pl.ds(1, 1), c0:c0 + cw] * _at(e, 0, 8)
                    + wc_ref[pl.ds(2, 1), c0:c0 + cw] * _at(e, 1, 8))
        for j in range(D_FF // cw):
            a = conv(j * cw)
            g = conv(D_FF + j * cw)
            z_ref[:, j * cw:(j + 1) * cw] = (g * _sigmoid(g) * a).astype(BF16)
    return pl.pallas_call(
        body, name="ffnconv_fwd", grid=(R // TMR,),
        in_specs=[*_halo_specs(w2, R), pl.BlockSpec((3, w2), lambda i: (0, 0))],
        out_specs=pl.BlockSpec((TMR, D_FF), lambda i: (i, 0)),
        out_shape=jax.ShapeDtypeStruct((R, D_FF), BF16), compiler_params=_params(48))(u, u, u, wc)


def ffnconv_bwd1(dz, u, wc, T):
    R, w2 = u.shape
    cw = 256

    def body(dz_ref, up_ref, u_ref, un_ref, wc_ref, duc_ref, dwc_ref):
        i = pl.program_id(0)
        live = _halo_live(i, T, R)

        @pl.when(i == 0)
        def _():
            dwc_ref[...] = jnp.zeros_like(dwc_ref)

        def taps(c0):
            e = _ext((up_ref, u_ref, un_ref), c0, cw, live, 8)
            return [_at(e, -1, 8), _at(e, 0, 8), _at(e, 1, 8)]

        def conv(t, c0):
            return (wc_ref[pl.ds(0, 1), c0:c0 + cw] * t[0] + wc_ref[pl.ds(1, 1), c0:c0 + cw] * t[1]
                    + wc_ref[pl.ds(2, 1), c0:c0 + cw] * t[2])
        for j in range(D_FF // cw):
            ca, cg = j * cw, D_FF + j * cw
            ta, tg = taps(ca), taps(cg)
            a, g = conv(ta, ca), conv(tg, cg)
            dzv = dz_ref[:, ca:ca + cw].astype(F32)
            sg = _sigmoid(g)
            da = dzv * (g * sg)
            dg = dzv * a * (sg * (1.0 + g * (1.0 - sg)))
            duc_ref[:, ca:ca + cw] = da.astype(BF16)
            duc_ref[:, cg:cg + cw] = dg.astype(BF16)
            for k in range(3):
                dwc_ref[k, :, ca:ca + cw] += _colsum8(da * ta[k])
                dwc_ref[k, :, cg:cg + cw] += _colsum8(dg * tg[k])
    return pl.pallas_call(
        body, name="ffnconv_bwd1", grid=(R // TMR,),
        in_specs=[pl.BlockSpec((TMR, D_FF), lambda i: (i, 0)), *_halo_specs(w2, R),
                  pl.BlockSpec((3, w2), lambda i: (0, 0))],
        out_specs=[pl.BlockSpec((TMR, w2), lambda i: (i, 0)), pl.BlockSpec((3, 8, w2), lambda i: (0, 0, 0))],
        out_shape=[jax.ShapeDtypeStruct((R, w2), BF16), jax.ShapeDtypeStruct((3, 8, w2), F32)],
        compiler_params=_params(48))(dz, u, u, u, wc)


def ffnconv_bwd2(duc, wc, T):
    R, w2 = duc.shape
    cw = 256

    def body(dp_ref, d_ref, dn_ref, wc_ref, du_ref):
        live = _halo_live(pl.program_id(0), T, R)
        for j in range(w2 // cw):
            c0 = j * cw
            e = _ext((dp_ref, d_ref, dn_ref), c0, cw, live, 8)
            du_ref[:, c0:c0 + cw] = (wc_ref[pl.ds(0, 1), c0:c0 + cw] * _at(e, 1, 8)
                                     + wc_ref[pl.ds(1, 1), c0:c0 + cw] * _at(e, 0, 8)
                                     + wc_ref[pl.ds(2, 1), c0:c0 + cw] * _at(e, -1, 8)).astype(BF16)
    return pl.pallas_call(
        body, name="ffnconv_bwd2", grid=(R // TMR,),
        in_specs=[*_halo_specs(w2, R), pl.BlockSpec((3, w2), lambda i: (0, 0))],
        out_specs=pl.BlockSpec((TMR, w2), lambda i: (i, 0)),
        out_shape=jax.ShapeDtypeStruct((R, w2), BF16), compiler_params=_params(48))(duc, duc, duc, wc)


_CW = 128


def _mixer_a(prefs, wa_ref, live):
    cin = _ext(prefs, A_W, A_W, live) * _ext(prefs, 2 * A_W, A_W, live)
    ca = (wa_ref[pl.ds(0, 1), :] * _at(cin, -1) + wa_ref[pl.ds(1, 1), :] * _at(cin, 0)
          + wa_ref[pl.ds(2, 1), :] * _at(cin, 1))
    return cin, ca


def _mixer_b(prefs, wb_ref, bias_ref, live, ub_s, ub2_s):
    for cc in range(A_W // _CW):
        c0 = cc * _CW
        ub = _ext(prefs, 3 * A_W + c0, _CW, live) * _sigmoid(_ext(prefs, 4 * A_W + c0, _CW, live))
        ub_s[:, c0:c0 + _CW] = ub
        acc = jnp.zeros((TMR, _CW), F32) + bias_ref[:, c0:c0 + _CW]
        for k in range(B_CONV):
            acc = acc + wb_ref[pl.ds(k, 1), c0:c0 + _CW] * _at(ub, k - B_CONV // 2)
        ub2_s[:, c0:c0 + _CW] = acc


def _layernorm_stats(v):
    mu = jnp.mean(v, axis=-1, keepdims=True)
    xc = v - mu
    rs = lax.rsqrt(jnp.mean(xc * xc, axis=-1, keepdims=True) + EPS)
    return xc * rs, rs


def convmix_fwd(p, wa, wb, bias, lng, lnb, T):
    R, wp = p.shape

    def body(pp_ref, p_ref, pn_ref, wa_ref, wb_ref, bias_ref, lng_ref, lnb_ref, o_ref, ub_s, ub2_s):
        live = _halo_live(pl.program_id(0), T, R)
        prefs = (pp_ref, p_ref, pn_ref)
        _, ca = _mixer_a(prefs, wa_ref, live)
        o_ref[:, 0:A_W] = (p_ref[:, 0:A_W].astype(F32) * ca).astype(BF16)
        _mixer_b(prefs, wb_ref, bias_ref, live, ub_s, ub2_s)
        xh, _ = _layernorm_stats(ub2_s[...])
        lv = xh * lng_ref[...] + lnb_ref[...]
        o_ref[:, A_W:2 * A_W] = (lv * _sigmoid(lv)).astype(BF16)
    vec = pl.BlockSpec((1, A_W), lambda i: (0, 0))
    return pl.pallas_call(
        body, name="convmix_fwd", grid=(R // TMR,),
        in_specs=[*_halo_specs(wp, R), pl.BlockSpec((3, A_W), lambda i: (0, 0)),
                  pl.BlockSpec((B_CONV, A_W), lambda i: (0, 0)), vec, vec, vec],
        out_specs=pl.BlockSpec((TMR, 2 * A_W), lambda i: (i, 0)),
        out_shape=jax.ShapeDtypeStruct((R, 2 * A_W), BF16),
        scratch_shapes=[pltpu.VMEM((TMR + 2 * HALO, A_W), F32), pltpu.VMEM((TMR, A_W), F32)],
        compiler_params=_params(48))(p, p, p, wa, wb, bias, lng, lnb)


def convmix_bwd1(dyab, p, wa, wb, bias, lng, lnb, T):
    R, wp = p.shape

    def body(dy_ref, pp_ref, p_ref, pn_ref, wa_ref, wb_ref, bias_ref, lng_ref, lnb_ref,
             dmid_ref, dwa_ref, dwb_ref, dvec_ref, ub_s, ub2_s):
        i = pl.program_id(0)
        live = _halo_live(i, T, R)

        @pl.when(i == 0)
        def _():
            dwa_ref[...] = jnp.zeros_like(dwa_ref)
            dwb_ref[...] = jnp.zeros_like(dwb_ref)
            dvec_ref[...] = jnp.zeros_like(dvec_ref)
        prefs = (pp_ref, p_ref, pn_ref)
        cin, ca = _mixer_a(prefs, wa_ref, live)
        dya = dy_ref[:, 0:A_W]
        dmid_ref[:, 0:A_W] = dya * ca
        dca = dya * p_ref[:, 0:A_W].astype(F32)
        dmid_ref[:, A_W:2 * A_W] = dca
        for k in range(3):
            dwa_ref[k] += _colsum8(dca * _at(cin, k - 1))
        _mixer_b(prefs, wb_ref, bias_ref, live, ub_s, ub2_s)
        xh, rs = _layernorm_stats(ub2_s[...])
        gain = lng_ref[...]
        lv = xh * gain + lnb_ref[...]
        sl = _sigmoid(lv)
        dl = dy_ref[:, A_W:2 * A_W] * (sl * (1.0 + lv * (1.0 - sl)))
        dvec_ref[1] += _colsum8(dl * xh)
        dvec_ref[2] += _colsum8(dl)
        dxh = dl * gain
        dub2 = rs * (dxh - jnp.mean(dxh, axis=-1, keepdims=True)
                     - xh * jnp.mean(dxh * xh, axis=-1, keepdims=True))
        dvec_ref[0] += _colsum8(dub2)
        dmid_ref[:, 2 * A_W:3 * A_W] = dub2
        for cc in range(A_W // _CW):
            c0 = cc * _CW
            ub = ub_s[:, c0:c0 + _CW]
            d = dmid_ref[:, 2 * A_W + c0:2 * A_W + c0 + _CW]
            for k in range(B_CONV):
                dwb_ref[k, :, c0:c0 + _CW] += _colsum8(d * _at(ub, k - B_CONV // 2))
    vec = pl.BlockSpec((1, A_W), lambda i: (0, 0))
    return pl.pallas_call(
        body, name="convmix_bwd1", grid=(R // TMR,),
        in_specs=[pl.BlockSpec((TMR, 2 * A_W), lambda i: (i, 0)), *_halo_specs(wp, R),
                  pl.BlockSpec((3, A_W), lambda i: (0, 0)), pl.BlockSpec((B_CONV, A_W), lambda i: (0, 0)),
                  vec, vec, vec],
        out_specs=[pl.BlockSpec((TMR, 3 * A_W), lambda i: (i, 0)),
                   pl.BlockSpec((3, 8, A_W), lambda i: (0, 0, 0)),
                   pl.BlockSpec((B_CONV, 8, A_W), lambda i: (0, 0, 0)),
                   pl.BlockSpec((3, 8, A_W), lambda i: (0, 0, 0))],
        out_shape=[jax.ShapeDtypeStruct((R, 3 * A_W), F32), jax.ShapeDtypeStruct((3, 8, A_W), F32),
                   jax.ShapeDtypeStruct((B_CONV, 8, A_W), F32), jax.ShapeDtypeStruct((3, 8, A_W), F32)],
        scratch_shapes=[pltpu.VMEM((TMR + 2 * HALO, A_W), F32), pltpu.VMEM((TMR, A_W), F32)],
        compiler_params=_params(48))(dyab, p, p, p, wa, wb, bias, lng, lnb)


def convmix_bwd2(dmid, p, wa, wb, T):
    R, wp = p.shape

    def body(mp_ref, m_ref, mn_ref, p_ref, wa_ref, wb_ref, dp_ref):
        live = _halo_live(pl.program_id(0), T, R)
        mrefs = (mp_ref, m_ref, mn_ref)
        dp_ref[:, 0:A_W] = m_ref[:, 0:A_W].astype(BF16)
        dca = _ext(mrefs, A_W, A_W, live)
        dcin = (wa_ref[pl.ds(0, 1), :] * _at(dca, 1) + wa_ref[pl.ds(1, 1), :] * _at(dca, 0)
                + wa_ref[pl.ds(2, 1), :] * _at(dca, -1))
        dp_ref[:, A_W:2 * A_W] = (dcin * p_ref[:, 2 * A_W:3 * A_W].astype(F32)).astype(BF16)
        dp_ref[:, 2 * A_W:3 * A_W] = (dcin * p_ref[:, A_W:2 * A_W].astype(F32)).astype(BF16)
        for cc in range(A_W // _CW):
            c0 = cc * _CW
            d = _ext(mrefs, 2 * A_W + c0, _CW, live)
            dub = jnp.zeros((TMR, _CW), F32)
            for k in range(B_CONV):
                dub = dub + wb_ref[pl.ds(k, 1), c0:c0 + _CW] * _at(d, B_CONV // 2 - k)
            vb = p_ref[:, 3 * A_W + c0:3 * A_W + c0 + _CW].astype(F32)
            s = _sigmoid(p_ref[:, 4 * A_W + c0:4 * A_W + c0 + _CW].astype(F32))
            dp_ref[:, 3 * A_W + c0:3 * A_W + c0 + _CW] = (dub * s).astype(BF16)
            dp_ref[:, 4 * A_W + c0:4 * A_W + c0 + _CW] = (dub * vb * s * (1.0 - s)).astype(BF16)
    return pl.pallas_call(
        body, name="convmix_bwd2", grid=(R // TMR,),
        in_specs=[*_halo_specs(3 * A_W, R), pl.BlockSpec((TMR, wp), lambda i: (i, 0)),
                  pl.BlockSpec((3, A_W), lambda i: (0, 0)), pl.BlockSpec((B_CONV, A_W), lambda i: (0, 0))],
        out_specs=pl.BlockSpec((TMR, wp), lambda i: (i, 0)),
        out_shape=jax.ShapeDtypeStruct((R, wp), BF16), compiler_params=_params(48))(dmid, dmid, dmid, p, wa, wb)


def _rot_half(v):
    w = v.shape[-1]
    lane = lax.broadcasted_iota(jnp.int32, (1, w), 1)
    return jnp.where(lane % HEAD_DIM < HEAD_DIM // 2, pltpu.roll(v, w - HEAD_DIM // 2, 1),
                     pltpu.roll(v, HEAD_DIM // 2, 1))


def rope_fwd(qkv, cs, sn):
    R, wq = qkv.shape
    qw = N_HEADS * HEAD_DIM
    kw = (wq - qw) // 2
    scale = HEAD_DIM ** -0.5

    def body(x_ref, cs_ref, sn_ref, o_ref):
        c, s = cs_ref[...], sn_ref[...]
        q = x_ref[:, 0:qw]
        o_ref[:, 0:qw] = ((q * jnp.tile(c, (1, qw // 128)) + _rot_half(q) * jnp.tile(s, (1, qw // 128)))
                          * scale).astype(BF16)
        k = x_ref[:, qw:qw + kw]
        o_ref[:, qw:qw + kw] = (k * jnp.tile(c, (1, kw // 128))
                                + _rot_half(k) * jnp.tile(s, (1, kw // 128))).astype(BF16)
        o_ref[:, qw + kw:] = x_ref[:, qw + kw:].astype(BF16)
    tab = pl.BlockSpec((TMR, 128), lambda i: (i, 0))
    return pl.pallas_call(
        body, name="rope_fwd", grid=(R // TMR,),
        in_specs=[pl.BlockSpec((TMR, wq), lambda i: (i, 0)), tab, tab],
        out_specs=pl.BlockSpec((TMR, wq), lambda i: (i, 0)),
        out_shape=jax.ShapeDtypeStruct((R, wq), BF16))(qkv, cs, sn)


def rope_bwd(dq, dks, dvs, dkc, dvc, cs, sn, T):
    R, qw = dq.shape
    kw = dkc.shape[1]
    nb = R // QB
    nl = T // QB
    scale = HEAD_DIM ** -0.5

    def body(dq_ref, kp_ref, ko_ref, kn_ref, vp_ref, vo_ref, vn_ref, kc_ref, vc_ref, cs_ref, sn_ref, o_ref):
        b = pl.program_id(0)
        c, s = cs_ref[...], sn_ref[...]
        has_next = (b + 1 < nb).astype(F32)
        has_prev = (b >= 1).astype(F32)
        is_ctx = (b >= nl).astype(F32)
        g = dq_ref[...] * scale
        o_ref[:, 0:qw] = (g * jnp.tile(c, (1, qw // 128)) + _rot_half(g * jnp.tile(s, (1, qw // 128)))).astype(BF16)
        g = ko_ref[...] + kp_ref[...] * has_next + kn_ref[...] * has_prev + kc_ref[...] * is_ctx
        o_ref[:, qw:qw + kw] = (g * jnp.tile(c, (1, kw // 128))
                                + _rot_half(g * jnp.tile(s, (1, kw // 128)))).astype(BF16)
        o_ref[:, qw + kw:] = (vo_ref[...] + vp_ref[...] * has_next + vn_ref[...] * has_prev
                              + vc_ref[...] * is_ctx).astype(BF16)
    own = pl.BlockSpec((QB, kw), lambda b: (b, 0))
    from_next = pl.BlockSpec((QB, kw), lambda b: (jnp.minimum(b + 1, nb - 1), 0))
    from_prev = pl.BlockSpec((QB, kw), lambda b: (jnp.maximum(b - 1, 0), 0))
    ctx = pl.BlockSpec((QB, kw), lambda b: (jnp.maximum(b - nl, 0), 0))
    tab = pl.BlockSpec((QB, 128), lambda b: (b, 0))
    return pl.pallas_call(
        body, name="rope_bwd", grid=(nb,),
        in_specs=[pl.BlockSpec((QB, qw), lambda b: (b, 0)), from_next, own, from_prev, from_next, own, from_prev,
                  ctx, ctx, tab, tab],
        out_specs=pl.BlockSpec((QB, qw + 2 * kw), lambda b: (b, 0)),
        out_shape=jax.ShapeDtypeStruct((R, qw + 2 * kw), BF16))(
            dq, dks[0], dks[1], dks[2], dvs[0], dvs[1], dvs[2], dkc, dvc, cs, sn)


def _attn_specs(T, R):
    nl = T // QB
    qcols = N_HEADS * HEAD_DIM // 128
    kcols = 2

    def band(col0, shift):
        return pl.BlockSpec((QB, 128), lambda jj, b: (jnp.clip(b + shift, 0, nl - 1), col0 + jj))

    def ctx(col0):
        return pl.BlockSpec((R - T, 128), lambda jj, b: (T // (R - T), col0 + jj))
    q = pl.BlockSpec((QB, 512), lambda jj, b: (b, jj))
    k0, v0 = qcols, qcols + kcols
    return q, [band(k0, -1), band(k0, 0), band(k0, 1), ctx(k0)], [band(v0, -1), band(v0, 0), band(v0, 1), ctx(v0)]


def _attn_common(T, R):
    nl = T // QB
    nk = 3 * QB + (R - T)

    def low_lanes():
        return lax.broadcasted_iota(jnp.int32, (1, 128), 1) < HEAD_DIM

    def dup(v, par):
        low = low_lanes()
        vf = v.astype(F32)
        r = pltpu.roll(vf, HEAD_DIM, 1)
        return (jnp.where(low, vf, r) if par == 0 else jnp.where(low, r, vf)).astype(BF16)

    def stack(ref, par):
        low = low_lanes()
        pa = ref[:, (2 * par) * 128:(2 * par + 1) * 128].astype(BF16)
        pb = ref[:, (2 * par + 1) * 128:(2 * par + 2) * 128].astype(BF16)
        zero = jnp.zeros_like(pa)
        return jnp.concatenate([jnp.where(low, pa, zero), jnp.where(low, zero, pa),
                                jnp.where(low, pb, zero), jnp.where(low, zero, pb)], axis=0)

    def unstack(v):
        low = low_lanes()
        return (jnp.where(low, v[0:QB], v[QB:2 * QB]), jnp.where(low, v[2 * QB:3 * QB], v[3 * QB:4 * QB]))

    def mask_of(b):
        col = lax.broadcasted_iota(jnp.int32, (1, nk), 1)
        gone = (((col < QB) & (b == 0)) | ((col >= 2 * QB) & (col < 3 * QB) & (b == nl - 1))
                | ((col < 3 * QB) & (b >= nl)))
        return jnp.where(gone, NEG_INF, 0.0)

    def sink_col(sink_ref, first):
        blk = lax.broadcasted_iota(jnp.int32, (4 * QB, 1), 0) // QB
        out = jnp.zeros((4 * QB, 1), F32) + sink_ref[first]
        for h in range(1, 4):
            out = jnp.where(blk == h, sink_ref[first + h], out)
        return out

    def scores(qs, kd, mask, sink):
        s = lax.dot_general(qs, kd, (((1,), (1,)), ((), ())), preferred_element_type=F32) + mask
        m = jnp.maximum(jnp.max(s, axis=-1, keepdims=True), sink)
        e = jnp.exp(s - m)
        es = jnp.exp(sink - m)
        return e, es, 1.0 / (jnp.sum(e, axis=-1, keepdims=True) + es)
    return low_lanes, dup, stack, unstack, mask_of, sink_col, scores


def window_bias(T, R):
    nk = 3 * QB + (R - T)
    row = jnp.arange(QB)[:, None]
    col = jnp.arange(nk)[None, :]
    near = (jnp.abs(col - QB - row) <= WINDOW) | (col >= 3 * QB)
    return jnp.tile(jnp.where(near, 0.0, NEG_INF).astype(F32), (4, 1))


def attn_fwd(qkvr, sinks, bias, T):
    R = qkvr.shape[0]
    qspec, kspecs, vspecs = _attn_specs(T, R)
    _, dup, stack, unstack, mask_of, sink_col, scores = _attn_common(T, R)

    def body(q_ref, kp, ko, kn, kc, vp, vo, vn, vc, sink_ref, bias_ref, o_ref):
        jj, b = pl.program_id(0), pl.program_id(1)
        mask = bias_ref[...] + mask_of(b)
        k_all = jnp.concatenate([kp[...], ko[...], kn[...], kc[...]], axis=0)
        v_all = jnp.concatenate([vp[...], vo[...], vn[...], vc[...]], axis=0)
        for par in range(2):
            kd, vd = dup(k_all, par), dup(v_all, par)
            e, _, rz = scores(stack(q_ref, par), kd, mask, sink_col(sink_ref, jj * 8 + par * 4))
            o = jnp.dot((e * rz).astype(BF16), vd, preferred_element_type=F32)
            pa, pb = unstack(o)
            o_ref[:, (2 * par) * 128:(2 * par + 1) * 128] = pa.astype(BF16)
            o_ref[:, (2 * par + 1) * 128:(2 * par + 2) * 128] = pb.astype(BF16)
    return pl.pallas_call(
        body, name="attn_fwd", grid=(2, R // QB),
        in_specs=[qspec, *kspecs, *vspecs, pl.BlockSpec(memory_space=pltpu.SMEM),
                  pl.BlockSpec(bias.shape, lambda jj, b: (0, 0))],
        out_specs=pl.BlockSpec((QB, 512), lambda jj, b: (b, jj)),
        out_shape=jax.ShapeDtypeStruct((R, N_HEADS * HEAD_DIM), BF16), compiler_params=_params(48))(
            qkvr, *([qkvr] * 8), sinks, bias)


def attn_bwd(qkvr, do, sinks, bias, T):
    R = qkvr.shape[0]
    tc = R - T
    qspec, kspecs, vspecs = _attn_specs(T, R)
    low_lanes, dup, stack, unstack, mask_of, sink_col, scores = _attn_common(T, R)
    contract_rows = (((0,), (0,)), ((), ()))
    contract_last = (((1,), (1,)), ((), ()))

    def body(q_ref, kp, ko, kn, kc, vp, vo, vn, vc, do_ref, sink_ref, bias_ref,
             dq_ref, dkp, dko, dkn, dvp, dvo, dvn, dkc_ref, dvc_ref, dsink_ref):
        jj, b = pl.program_id(0), pl.program_id(1)

        @pl.when((jj == 0) & (b == 0))
        def _():
            dsink_ref[...] = jnp.zeros_like(dsink_ref)

        @pl.when(b == 0)
        def _():
            dkc_ref[...] = jnp.zeros_like(dkc_ref)
            dvc_ref[...] = jnp.zeros_like(dvc_ref)
        mask = bias_ref[...] + mask_of(b)
        k_all = jnp.concatenate([kp[...], ko[...], kn[...], kc[...]], axis=0)
        v_all = jnp.concatenate([vp[...], vo[...], vn[...], vc[...]], axis=0)
        lane = lax.broadcasted_iota(jnp.int32, (8, 128), 1)
        srow = lax.broadcasted_iota(jnp.int32, (8, 128), 0)
        dk_fold, dv_fold = [], []
        for par in range(2):
            kd, vd = dup(k_all, par), dup(v_all, par)
            first = jj * 8 + par * 4
            qs, dos = stack(q_ref, par), stack(do_ref, par)
            e, es, rz = scores(qs, kd, mask, sink_col(sink_ref, first))
            p = e * rz
            dp = lax.dot_general(dos, vd, contract_last, preferred_element_type=F32)
            delta = jnp.sum(p * dp, axis=-1, keepdims=True)
            ds = (p * (dp - delta)).astype(BF16)
            t = es * rz * delta
            for h in range(4):
                dsink = -jnp.sum(t[h * QB:(h + 1) * QB])
                dsink_ref[...] += jnp.where((lane == first + h) & (srow == 0), dsink, 0.0)
            pa, pb = unstack(jnp.dot(ds, kd, preferred_element_type=F32))
            dq_ref[:, (2 * par) * 128:(2 * par + 1) * 128] = pa
            dq_ref[:, (2 * par + 1) * 128:(2 * par + 2) * 128] = pb
            dk_t = lax.dot_general(qs, ds, contract_rows, preferred_element_type=F32)
            dv_t = lax.dot_general(dos, p.astype(BF16), contract_rows, preferred_element_type=F32)
            dk_fold.append(dk_t + pltpu.roll(dk_t, HEAD_DIM, 0))
            dv_fold.append(dv_t + pltpu.roll(dv_t, HEAD_DIM, 0))
        low_rows = lax.broadcasted_iota(jnp.int32, (128, 1), 0) < HEAD_DIM
        dk = jnp.where(low_rows, dk_fold[0], dk_fold[1]).T
        dv = jnp.where(low_rows, dv_fold[0], dv_fold[1]).T
        dkp[...], dko[...], dkn[...] = dk[0:QB], dk[QB:2 * QB], dk[2 * QB:3 * QB]
        dvp[...], dvo[...], dvn[...] = dv[0:QB], dv[QB:2 * QB], dv[2 * QB:3 * QB]
        dkc_ref[...] += dk[3 * QB:]
        dvc_ref[...] += dv[3 * QB:]
    blk = pl.BlockSpec((QB, 128), lambda jj, b: (b, jj))
    cblk = pl.BlockSpec((tc, 128), lambda jj, b: (0, jj))
    part = jax.ShapeDtypeStruct((R, 256), F32)
    csum = jax.ShapeDtypeStruct((tc, 256), F32)
    outs = pl.pallas_call(
        body, name="attn_bwd", grid=(2, R // QB),
        in_specs=[qspec, *kspecs, *vspecs, pl.BlockSpec((QB, 512), lambda jj, b: (b, jj)),
                  pl.BlockSpec(memory_space=pltpu.SMEM), pl.BlockSpec(bias.shape, lambda jj, b: (0, 0))],
        out_specs=[pl.BlockSpec((QB, 512), lambda jj, b: (b, jj)), blk, blk, blk, blk, blk, blk, cblk, cblk,
                   pl.BlockSpec((8, 128), lambda jj, b: (0, 0))],
        out_shape=[jax.ShapeDtypeStruct((R, N_HEADS * HEAD_DIM), F32), part, part, part, part, part, part,
                   csum, csum, jax.ShapeDtypeStruct((8, 128), F32)],
        compiler_params=_params(48))(qkvr, *([qkvr] * 8), do, sinks, bias)
    return outs[0], outs[1:4], outs[4:7], outs[7], outs[8], outs[9]


def loss_head(x, nw, target, T):
    R, dm = x.shape
    nl = T // TMR

    def body(x_ref, nw_ref, t_ref, loss_ref, dx_ref, dnw_ref):
        i = pl.program_id(0)

        @pl.when(i == 0)
        def _():
            loss_ref[...] = jnp.zeros_like(loss_ref)
            dnw_ref[...] = jnp.zeros_like(dnw_ref)
        live = (i < nl).astype(F32)
        nwv = nw_ref[...]
        xv = x_ref[...]
        r = lax.rsqrt(jnp.mean(xv * xv, axis=-1, keepdims=True) + EPS)
        xh = xv * r
        err = xh * nwv - t_ref[...]
        per_row = jnp.mean(err * err, axis=-1, keepdims=True)
        loss_ref[...] += 0.5 * live * jnp.sum(per_row)
        dy = err * (live / dm)
        dnw_ref[...] += _colsum8(dy * xh)
        dxh = dy * nwv
        dx_ref[...] = r * (dxh - xh * jnp.mean(dxh * xh, axis=-1, keepdims=True))
    tile = pl.BlockSpec((TMR, dm), lambda i: (i, 0))
    return pl.pallas_call(
        body, name="loss_head", grid=(R // TMR,),
        in_specs=[tile, pl.BlockSpec((1, dm), lambda i: (0, 0)),
                  pl.BlockSpec((TMR, dm), lambda i: (jnp.minimum(i, nl - 1), 0))],
        out_specs=[pl.BlockSpec((8, 128), lambda i: (0, 0)), tile, pl.BlockSpec((8, dm), lambda i: (0, 0))],
        out_shape=[jax.ShapeDtypeStruct((8, 128), F32), jax.ShapeDtypeStruct((R, dm), F32),
                   jax.ShapeDtypeStruct((8, dm), F32)])(x, nw, target)


def adaln_fwd(cond, w_mod, b_mod):
    nl, dm, ns = w_mod.shape

    def body(c_ref, w_ref, b_ref, o_ref):
        cv = c_ref[...]
        s = (cv * _sigmoid(cv)).astype(BF16)
        o_ref[...] = jnp.dot(s, w_ref[...].astype(BF16), preferred_element_type=F32) + b_ref[...]
    return pl.pallas_call(
        body, name="adaln_fwd", grid=(nl,),
        in_specs=[pl.BlockSpec((16, dm), lambda l: (0, 0)), pl.BlockSpec((None, dm, ns), lambda l: (l, 0, 0)),
                  pl.BlockSpec((None, 1, ns), lambda l: (l, 0, 0))],
        out_specs=pl.BlockSpec((None, 16, ns), lambda l: (l, 0, 0)),
        out_shape=jax.ShapeDtypeStruct((nl, 16, ns), F32), compiler_params=_params(48))(cond, w_mod, b_mod)


def adaln_bwd(cond, dmod, w_mod):
    nl, dm, ns = w_mod.shape

    def body(c_ref, d_ref, w_ref, gw_ref, ds_ref):
        l = pl.program_id(0)

        @pl.when(l == 0)
        def _():
            ds_ref[...] = jnp.zeros_like(ds_ref)
        cv = c_ref[...]
        s = (cv * _sigmoid(cv)).astype(BF16)
        dv = d_ref[...].astype(BF16)
        gw_ref[...] = lax.dot_general(s, dv, (((0,), (0,)), ((), ())), preferred_element_type=F32)
        ds_ref[...] += lax.dot_general(dv, w_ref[...].astype(BF16), (((1,), (1,)), ((), ())),
                                       preferred_element_type=F32)
    return pl.pallas_call(
        body, name="adaln_bwd", grid=(nl,),
        in_specs=[pl.BlockSpec((16, dm), lambda l: (0, 0)), pl.BlockSpec((None, 16, ns), lambda l: (l, 0, 0)),
                  pl.BlockSpec((None, dm, ns), lambda l: (l, 0, 0))],
        out_specs=[pl.BlockSpec((None, dm, ns), lambda l: (l, 0, 0)), pl.BlockSpec((16, dm), lambda l: (0, 0))],
        out_shape=[jax.ShapeDtypeStruct((nl, dm, ns), F32), jax.ShapeDtypeStruct((16, dm), F32)],
        compiler_params=_params(48))(cond, dmod, w_mod)


def _me():
    return lax.axis_index("x"), lax.axis_index("y"), lax.axis_index("c")


def allgather8(block):
    m_per, n = block.shape

    def body(x_ref, out_ref, send_sems, recv_sems, local_sem):
        x, y, c = _me()
        me, sibling = (x, y, c), (x, y, 1 - c)
        chips = [(1 - x, y), (x, 1 - y), (1 - x, 1 - y)]

        def rows(px, py, pc):
            return out_ref.at[pl.ds((4 * px + 2 * py + pc) * m_per, m_per), :]

        def copy(k, blk, to, src=None):
            return pltpu.make_async_remote_copy(
                src_ref=rows(*blk) if src is None else src, dst_ref=rows(*blk),
                send_sem=send_sems.at[k], recv_sem=recv_sems.at[k], device_id=to, device_id_type=MESH)
        mine = pltpu.make_async_copy(x_ref, rows(*me), local_sem)
        mine.start()
        first = [copy(0, me, sibling, src=x_ref)]
        first += [copy(1 + j, me, (*chip, c), src=x_ref) for j, chip in enumerate(chips)]
        for cp in first:
            cp.start()
        passed = [copy(4 + j, (*chip, c), sibling) for j, chip in enumerate(chips)]
        for j, chip in enumerate(chips):
            copy(1 + j, (*chip, c), me).wait_recv()
            passed[j].start()
        copy(0, sibling, me).wait_recv()
        for j, chip in enumerate(chips):
            copy(4 + j, (*chip, 1 - c), me).wait_recv()
        for cp in first + passed:
            cp.wait_send()
        mine.wait()
    return pl.pallas_call(
        body, name="allgather8",
        out_shape=jax.ShapeDtypeStruct((N_DEV * m_per, n), block.dtype),
        in_specs=[pl.BlockSpec(memory_space=pltpu.VMEM)],
        out_specs=pl.BlockSpec(memory_space=pltpu.VMEM),
        scratch_shapes=[pltpu.SemaphoreType.DMA((7,)), pltpu.SemaphoreType.DMA((7,)), pltpu.SemaphoreType.DMA],
        compiler_params=_params(48))(block)


def _other_chips(x, y):
    return [(1 - x, y), (x, 1 - y), (1 - x, 1 - y)]


_HBM = pl.BlockSpec(memory_space=pltpu.HBM)
_SEM = pl.BlockSpec(memory_space=pltpu.SEMAPHORE)
_ANY = pl.BlockSpec(memory_space=pl.ANY)
_EFFECT = pltpu.SideEffectType.DATAFLOW_SIDE_EFFECTING


def _in_hbm(v):
    return pltpu.with_memory_space_constraint(v, pltpu.HBM)


def cast_into_slot(w, chip_id):
    kb, nb = w.shape
    tr = _row_tile(kb)

    def body(chip_ref, w_ref, o_ref):
        del chip_ref
        o_ref[...] = w_ref[...].astype(BF16)
    return pl.pallas_call(
        body, name="cast_into_slot",
        grid_spec=pltpu.PrefetchScalarGridSpec(
            num_scalar_prefetch=1, grid=(kb // tr,),
            in_specs=[pl.BlockSpec((tr, nb), lambda i, chip: (i, 0))],
            out_specs=pl.BlockSpec((None, tr, nb), lambda i, chip: (chip[0], i, 0))),
        out_shape=jax.ShapeDtypeStruct((N_CHIP, kb, nb), BF16))(chip_id, w)


def _split_copies(mode, srcs, lands, send_sems, recv_sems):
    x, y, c = _me()
    out = []
    for t in range(len(lands)):
        for k, chip in enumerate(_other_chips(x, y)):
            if mode == "gather":
                src = dst = lands[t].at[2 * x + y]
                landed = lands[t].at[2 * chip[0] + chip[1]]
            else:
                src, dst, landed = srcs[t].at[2 * chip[0] + chip[1]], lands[t].at[k], lands[t].at[k]
            send = pltpu.make_async_remote_copy(src_ref=src, dst_ref=dst, send_sem=send_sems.at[3 * t + k],
                                                recv_sem=recv_sems.at[3 * t + k], device_id=(*chip, c),
                                                device_id_type=MESH)
            recv = pltpu.make_async_remote_copy(src_ref=src, dst_ref=landed, send_sem=send_sems.at[3 * t + k],
                                                recv_sem=recv_sems.at[3 * t + k], device_id=(*chip, c),
                                                device_id_type=MESH)
            out.append((send, recv))
    return out


def exchange_start(name, mode, srcs, lands, after):
    ns, nl = len(srcs), len(lands)
    na = ns + nl

    def body(*refs):
        src_refs, land_refs = refs[:ns], refs[ns:na]
        send_sems, recv_sems = refs[na + 1], refs[na + 2]
        token = refs[-1]
        for send, _ in _split_copies(mode, src_refs, land_refs, send_sems, recv_sems):
            send.start()
        token[...] = jnp.zeros_like(token)
    arrays = list(srcs) + list(lands)
    outs = pl.pallas_call(
        body, name=name,
        out_shape=(pltpu.SemaphoreType.DMA((3 * nl,)), pltpu.SemaphoreType.DMA((3 * nl,)),
                   *[pltpu.HBM(v.shape, v.dtype) for v in arrays], jax.ShapeDtypeStruct((8, 128), F32)),
        in_specs=[_HBM] * na + [_ANY],
        out_specs=(_SEM, _SEM, *[_HBM] * na, pl.BlockSpec(memory_space=pltpu.VMEM)),
        input_output_aliases={i: 2 + i for i in range(na)},
        compiler_params=pltpu.CompilerParams(has_side_effects=_EFFECT))(*[_in_hbm(v) for v in arrays], after)
    return outs[0], outs[1], list(outs[2:2 + ns]), list(outs[2 + ns:2 + na]), outs[-1]


def exchange_wait(name, mode, send_sems, recv_sems, srcs, lands, after):
    ns, nl = len(srcs), len(lands)
    na = ns + nl

    def body(*refs):
        for _, recv in _split_copies(mode, refs[:ns], refs[ns:na], refs[na], refs[na + 1]):
            recv.wait_send()
            recv.wait_recv()
    arrays = list(srcs) + list(lands)
    outs = pl.pallas_call(
        body, name=name,
        out_shape=[pltpu.HBM(v.shape, v.dtype) for v in arrays],
        in_specs=[_HBM] * na + [_SEM, _SEM, _ANY], out_specs=[_HBM] * na,
        input_output_aliases={i: i for i in range(na)},
        compiler_params=pltpu.CompilerParams(has_side_effects=_EFFECT))(*arrays, send_sems, recv_sems, after)
    return list(outs[:ns]), list(outs[ns:])


def swap_with_sibling(v):
    def body(v_ref, out_ref, send_sem, recv_sem):
        x, y, c = _me()
        cp = pltpu.make_async_remote_copy(src_ref=v_ref, dst_ref=out_ref, send_sem=send_sem, recv_sem=recv_sem,
                                          device_id=(x, y, 1 - c), device_id_type=MESH)
        cp.start()
        cp.wait()
    return pl.pallas_call(
        body, name="swap_with_sibling", out_shape=jax.ShapeDtypeStruct(v.shape, v.dtype),
        in_specs=[pl.BlockSpec(memory_space=pl.ANY)], out_specs=pl.BlockSpec(memory_space=pl.ANY),
        scratch_shapes=[pltpu.SemaphoreType.DMA, pltpu.SemaphoreType.DMA])(v)


def sum_slots(parts):
    n, rows, w = parts.shape
    tr = _row_tile(rows)

    def body(p_ref, o_ref):
        acc = p_ref[0].astype(F32)
        for k in range(1, n):
            acc = acc + p_ref[k].astype(F32)
        o_ref[...] = acc
    return pl.pallas_call(
        body, name="sum_slots", grid=(rows // tr,),
        in_specs=[pl.BlockSpec((n, tr, w), lambda i: (0, i, 0))], out_specs=pl.BlockSpec((tr, w), lambda i: (i, 0)),
        out_shape=jax.ShapeDtypeStruct((rows, w), F32), compiler_params=_params(48))(parts)


def sum_landed(landed, own, chip_id, layer, n_layers, buf):
    n, rows, w = landed.shape
    tr = _row_tile(rows)
    base = layer * (rows // tr)

    def compute(l_ref, g_ref, o_ref):
        acc = g_ref[...].astype(F32)
        for k in range(n):
            acc = acc + l_ref[k].astype(F32)
        o_ref[...] = acc
    in_specs = [pl.BlockSpec((n, tr, w), lambda i, chip: (0, i, 0)),
                pl.BlockSpec((None, tr, w), lambda i, chip: (chip[0], i, 0))]
    out_spec = pl.BlockSpec((tr, w), lambda i, chip: (base + i, 0))
    out_shape = jax.ShapeDtypeStruct((n_layers * rows, w), F32)
    if buf is None:
        def body(chip_ref, l_ref, g_ref, o_ref):
            del chip_ref
            compute(l_ref, g_ref, o_ref)
        return pl.pallas_call(
            body, name="sum_landed",
            grid_spec=pltpu.PrefetchScalarGridSpec(num_scalar_prefetch=1, grid=(rows // tr,), in_specs=in_specs,
                                                   out_specs=out_spec),
            out_shape=out_shape, compiler_params=_params(48))(chip_id, landed, own)

    def body(chip_ref, l_ref, g_ref, buf_ref, o_ref):
        del chip_ref, buf_ref
        compute(l_ref, g_ref, o_ref)
    return pl.pallas_call(
        body, name="sum_landed_into",
        grid_spec=pltpu.PrefetchScalarGridSpec(num_scalar_prefetch=1, grid=(rows // tr,),
                                               in_specs=in_specs + [_ANY], out_specs=out_spec),
        out_shape=out_shape, input_output_aliases={3: 0}, compiler_params=_params(48))(chip_id, landed, own, buf)


def adamw(w, ga, gb, m, v):
    rows, wd = w.shape
    tr = min(_row_tile(rows), 128)
    c1 = 1.0 / (1.0 - ADAM_B1 ** ADAM_STEP)
    c2 = 1.0 / (1.0 - ADAM_B2 ** ADAM_STEP)

    def update(wv, g, mv, vv, g_ref, d_ref, m_ref, v_ref):
        mn = ADAM_B1 * mv + (1.0 - ADAM_B1) * g
        vn = ADAM_B2 * vv + (1.0 - ADAM_B2) * (g * g)
        g_ref[...] = g
        m_ref[...] = mn
        v_ref[...] = vn
        d_ref[...] = -ADAM_LR * ((mn * c1) / (jnp.sqrt(vn * c2) + ADAM_EPS) + ADAM_WD * wv)
    tile = pl.BlockSpec((tr, wd), lambda i: (i, 0))
    out = jax.ShapeDtypeStruct((rows, wd), F32)
    if gb is None:
        def body(w_ref, ga_ref, m_ref, v_ref, g_out, d_out, m_out, v_out):
            update(w_ref[...], ga_ref[...], m_ref[...], v_ref[...], g_out, d_out, m_out, v_out)
        return pl.pallas_call(body, name="adamw", grid=(rows // tr,), in_specs=[tile] * 4,
                              out_specs=[tile] * 4, out_shape=[out] * 4)(w, ga, m, v)

    def body(w_ref, ga_ref, gb_ref, m_ref, v_ref, g_out, d_out, m_out, v_out):
        update(w_ref[...], ga_ref[...] + gb_ref[...], m_ref[...], v_ref[...], g_out, d_out, m_out, v_out)
    return pl.pallas_call(body, name="adamw_sum", grid=(rows // tr,), in_specs=[tile] * 5,
                          out_specs=[tile] * 4, out_shape=[out] * 4)(w, ga, gb, m, v)


def _rope_tables(T, R):
    rows = T // GRID_W
    row = jnp.repeat(jnp.arange(rows), GRID_W).astype(F32)
    col = jnp.tile(jnp.arange(GRID_W), rows).astype(F32)
    n_freq = HEAD_DIM // 4
    inv_freq = ROPE_THETA ** (-jnp.arange(n_freq, dtype=F32) / n_freq)
    ang = jnp.concatenate([row[:, None] * inv_freq, col[:, None] * inv_freq], axis=-1)
    cos, sin = jnp.cos(ang), jnp.sin(ang)
    cs = jnp.tile(cos, (1, 4))
    sn = jnp.tile(jnp.concatenate([-sin, sin], axis=-1), (1, 2))
    pad = R - T
    return (jnp.concatenate([cs, jnp.ones((pad, 128), F32)], axis=0),
            jnp.concatenate([sn, jnp.zeros((pad, 128), F32)], axis=0))


def _pack(parts, mult=8 * 128):
    flat = jnp.concatenate([p.reshape(-1).astype(F32) for p in parts])
    pad = (-flat.shape[0]) % mult
    return jnp.pad(flat, (0, pad)).reshape(-1, 128)


def _unpack(buf, shapes):
    flat = buf.reshape(-1)
    out, o = [], 0
    for s in shapes:
        n = 1
        for d in s:
            n *= d
        out.append(flat[o:o + n].reshape(s))
        o += n
    return out


def kernel(x, c, ctx, c_ctx, w_mod, b_mod, norm_mix, norm_ffn, w_in_ab, conv_a, conv_b, conv_b_bias, ln_b_gain, ln_b_bias, w_out_ab, w_qkv, w_o, sinks, w_up, w_conv_ffn, w_down, final_norm, loss_target, m_c_ctx, m_w_mod, m_b_mod, m_norm_mix, m_norm_ffn, m_w_in_ab, m_conv_a, m_conv_b, m_conv_b_bias, m_ln_b_gain, m_ln_b_bias, m_w_out_ab, m_w_qkv, m_w_o, m_sinks, m_w_up, m_w_conv_ffn, m_w_down, m_final_norm, v_c_ctx, v_w_mod, v_b_mod, v_norm_mix, v_norm_ffn, v_w_in_ab, v_conv_a, v_conv_b, v_conv_b_bias, v_ln_b_gain, v_ln_b_bias, v_w_out_ab, v_w_qkv, v_w_o, v_sinks, v_w_up, v_w_conv_ffn, v_w_down, v_final_norm):
    T, dm = x.shape[1], x.shape[2]
    tc = ctx.shape[1]
    R = T + tc
    depth = w_mod.shape[0]
    ax, ay, ac = lax.axis_index("x"), lax.axis_index("y"), lax.axis_index("c")
    chip = 2 * ax + ay
    dev = 4 * ax + 2 * ay + ac

    small_w = [conv_a, conv_b, w_conv_ffn]
    gathered = allgather8(_pack([c] + small_w)).reshape(N_DEV, -1)
    cond8 = gathered[:, :dm]
    off = dm
    full_small = []
    for wsh in small_w:
        n = wsh.size
        per_chip = gathered[0::2, off:off + n].reshape((N_CHIP,) + wsh.shape)
        full_small.append(jnp.concatenate([per_chip[q] for q in range(N_CHIP)], axis=-1))
        off += n
    conv_a_f, conv_b_f, w_conv_ffn_f = full_small
    cond = jnp.concatenate([cond8, c_ctx[None, :], jnp.zeros((7, dm), F32)], axis=0)

    ns_mod = w_mod.shape[2]
    b_mod_sh = lax.dynamic_slice_in_dim(b_mod, chip * ns_mod, ns_mod, axis=1)[:, None, :]
    mod_sh = adaln_fwd(cond, w_mod, b_mod_sh)
    mod_all = allgather8(mod_sh.reshape(depth * 16, ns_mod)).reshape(N_DEV, depth, 16, ns_mod)
    mod_full = jnp.concatenate([mod_all[2 * q] for q in range(N_CHIP)], axis=-1)
    mine = lax.dynamic_index_in_dim(mod_full, dev, axis=1, keepdims=False)
    mods = jnp.stack([mine, mod_full[:, 8]], axis=1).reshape(depth, 2, 6, dm)

    masters = {"w_in_ab": w_in_ab, "w_out_ab": w_out_ab, "w_qkv": w_qkv, "w_o": w_o, "w_up": w_up, "w_down": w_down}
    chip_id = chip.astype(jnp.int32).reshape(1)

    def half_weights(l, half):
        if half == 1:
            return [("w_up", l), ("w_down", l)]
        return [("w_in_ab", l // 2), ("w_out_ab", l // 2)] if l % 2 == 0 else [("w_qkv", l // 2), ("w_o", l // 2)]
    in_flight, after = {}, mods
    for l in range(depth):
        for half in range(2):
            lands = [cast_into_slot(masters[n][j], chip_id) for n, j in half_weights(l, half)]
            send_sems, recv_sems, _, lands, after = exchange_start(f"gather_start_{l}_{half}", "gather", [], lands, after)
            in_flight[l, half] = (send_sems, recv_sems, lands)
    mods = mods + after[0, 0]

    def gathered_weights(l, half, after):
        send_sems, recv_sems, lands = in_flight[l, half]
        _, landed = exchange_wait(f"gather_wait_{l}_{half}", "gather", send_sems, recv_sems, [], lands, after)
        return dict(zip([n for n, _ in half_weights(l, half)], landed))

    cs, sn = _rope_tables(T, R)
    bias = window_bias(T, R)
    sinks_flat = sinks.reshape(-1)

    xs = jnp.concatenate([x[0], ctx[0]], axis=0)
    saved, W = [], []
    h1 = norm_mod_fwd(xs, norm_mix[0][None], mods[0], 0, T)
    for l in range(depth):
        e = l // 2
        wl = gathered_weights(l, 0, h1)
        W.append(wl)
        s = {"x0": xs, "h1": h1}
        if l % 2 == 0:
            p = mm_nn(h1, wl["w_in_ab"], "col", BF16)
            yab = convmix_fwd(p, conv_a_f[e], conv_b_f[e], conv_b_bias[e][None], ln_b_gain[e][None],
                              ln_b_bias[e][None], T)
            s.update(p=p, mix=yab)
        else:
            qkv = mm_nn(h1, wl["w_qkv"], "col", F32)
            qkvr = rope_fwd(qkv, cs, sn)
            att = attn_fwd(qkvr, sinks_flat[e * N_HEADS:(e + 1) * N_HEADS], bias, T)
            s.update(qkvr=qkvr, mix=att)
        y1, x1, h2 = mm_resid_norm_fwd(s["mix"], wl["w_out_ab" if l % 2 == 0 else "w_o"], xs, norm_ffn[l][None],
                                       mods[l], mods[l], 2, 3, T)
        wl.update(gathered_weights(l, 1, h2))
        u = mm_nn(h2, wl["w_up"], "col", BF16)
        z = ffnconv_fwd(u, w_conv_ffn_f[l], T)
        if l + 1 < depth:
            y2, xs, h1 = mm_resid_norm_fwd(z, wl["w_down"], x1, norm_mix[l + 1][None], mods[l], mods[l + 1], 5, 0, T)
        else:
            y2 = mm_nn(z, wl["w_down"], "row", F32)
            xs = resid_fwd(x1, y2, mods[l], 5, T)
        s.update(y1=y1, x1=x1, h2=h2, u=u, z=z, y2=y2)
        saved.append(s)

    loss_part, dx, d_final = loss_head(xs, final_norm[None], loss_target[0], T)
    loss = lax.psum(loss_part[0, 0], ("x", "y", "c"))

    d_mods, d_norm_mix, d_norm_ffn = [None] * depth, [None] * depth, [None] * depth
    d_conv_a, d_conv_b, d_vecs, d_sinks, d_wc = [None] * 2, [None] * 2, [None] * 2, [None] * 2, [None] * depth
    dss1, dss2, dg1, dg2 = [None] * depth, [None] * depth, [None] * depth, [None] * depth
    scattering = {}

    def scatter(l, half, G, after):
        grads_h = [G[n] for n, _ in half_weights(l, half)]
        lands = [lax.empty((N_CHIP - 1, *g.shape[1:]), g.dtype) for g in grads_h]
        send_sems, recv_sems, grads_h, lands, token = exchange_start(
            f"scatter_start_{l}_{half}", "scatter", grads_h, lands, after)
        scattering[l, half] = (send_sems, recv_sems, grads_h, lands)
        return token

    dy2, dg2[depth - 1] = resid_bwd(dx, saved[depth - 1]["y2"], mods[depth - 1], 5, T)
    for l in reversed(range(depth)):
        e = l // 2
        s, wl = saved[l], W[l]
        G = {}
        G["w_down"] = mm_tn(s["z"], dy2, "row", wl["w_down"])
        dz = mm_nt(dy2, wl["w_down"], "row", BF16)
        duc, d_wc[l] = ffnconv_bwd1(dz, s["u"], w_conv_ffn_f[l], T)
        du = ffnconv_bwd2(duc, w_conv_ffn_f[l], T)
        G["w_up"] = mm_tn(s["h2"], du, "col", wl["w_up"])
        token = scatter(l, 1, G, du)
        mod_l = mods[l] + token[0, 0]
        dx, dy1, dss2[l], d_norm_ffn[l], dg1[l] = mm_norm_resid_bwd(
            du, wl["w_up"], s["x1"], norm_ffn[l][None], mod_l, dx, s["y1"], mod_l, 3, 2, T)
        if l % 2 == 0:
            G["w_out_ab"] = mm_tn(s["mix"], dy1, "row", wl["w_out_ab"])
            dyab = mm_nt(dy1, wl["w_out_ab"], "row", F32)
            dmid, d_conv_a[e], d_conv_b[e], d_vecs[e] = convmix_bwd1(
                dyab, s["p"], conv_a_f[e], conv_b_f[e], conv_b_bias[e][None], ln_b_gain[e][None],
                ln_b_bias[e][None], T)
            dp = convmix_bwd2(dmid, s["p"], conv_a_f[e], conv_b_f[e], T)
            G["w_in_ab"] = mm_tn(s["h1"], dp, "col", wl["w_in_ab"])
            dcol, wcol = dp, wl["w_in_ab"]
        else:
            G["w_o"] = mm_tn(s["mix"], dy1, "row", wl["w_o"])
            datt = mm_nt(dy1, wl["w_o"], "row", BF16)
            dq, dks, dvs, dkc, dvc, d_sinks[e] = attn_bwd(
                s["qkvr"], datt, sinks_flat[e * N_HEADS:(e + 1) * N_HEADS], bias, T)
            dqkv = rope_bwd(dq, dks, dvs, dkc, dvc, cs, sn, T)
            G["w_qkv"] = mm_tn(s["h1"], dqkv, "col", wl["w_qkv"])
            dcol, wcol = dqkv, wl["w_qkv"]
        token = scatter(l, 0, G, dcol)
        mod_l = mods[l] + token[0, 0]
        if l > 0:
            dx, dy2, dss1[l], d_norm_mix[l], dg2[l - 1] = mm_norm_resid_bwd(
                dcol, wcol, s["x0"], norm_mix[l][None], mod_l, dx, saved[l - 1]["y2"], mods[l - 1], 0, 5, T)
        else:
            dx, dss1[l], d_norm_mix[l] = mm_norm_resid_bwd(
                dcol, wcol, s["x0"], norm_mix[l][None], mod_l, dx, None, None, 0, 0, T)
    grad_x = dx[:T][None]
    for l in range(depth):
        a1, a2 = dss1[l].sum(2), dss2[l].sum(2)
        d_mods[l] = jnp.stack([a1[:, 0], a1[:, 1], dg1[l].sum(1), a2[:, 0], a2[:, 1], dg2[l].sum(1)], axis=1)

    d_mods = jnp.stack(d_mods)
    summed_parts = [
        d_mods[:, 1],
        jnp.stack(d_norm_mix).sum(1), jnp.stack(d_norm_ffn).sum(1),
        jnp.stack(d_conv_a).sum(2), jnp.stack(d_conv_b).sum(2),
        jnp.stack(d_vecs).sum(2),
        jnp.stack(d_sinks)[:, 0, :N_HEADS],
        jnp.stack(d_wc).sum(2), d_final.sum(0)]
    summed_shapes = [p.shape for p in summed_parts]
    n_own = depth * 6 * dm
    pack = _pack([d_mods[:, 0]] + summed_parts)
    parts = allgather8(pack).reshape(N_DEV, -1, 128)
    total = sum_slots(parts)
    own_rows = parts.reshape(N_DEV, -1)[:, :n_own].reshape(N_DEV, depth, 6 * dm)
    (dmod_ctx, g_norm_mix, g_norm_ffn, g_conv_a, g_conv_b, g_vecs, g_sinks, g_wc, g_final) = _unpack(
        total.reshape(-1)[n_own:], summed_shapes)
    dmod_rows = jnp.concatenate([jnp.moveaxis(own_rows, 0, 1), dmod_ctx.reshape(depth, 1, 6 * dm),
                                 jnp.zeros((depth, 7, 6 * dm), F32)], axis=1)
    g_b_mod = dmod_rows.sum(1)
    dmod_sh = lax.dynamic_slice_in_dim(dmod_rows, chip * ns_mod, ns_mod, axis=2)
    g_w_mod, dsilu = adaln_bwd(cond, dmod_sh, w_mod)
    dsilu_all = allgather8(dsilu[8:16]).reshape(N_DEV, 8, dm)
    dsilu_ctx = sum_slots(dsilu_all[0::2])[0]
    sg = jax.nn.sigmoid(c_ctx)
    g_c_ctx = dsilu_ctx * (sg * (1.0 + c_ctx * (1.0 - sg)))

    def shard_cols(full, width):
        return lax.dynamic_slice_in_dim(full, chip * width, width, axis=full.ndim - 1)
    g_conv_a_s = shard_cols(g_conv_a, conv_a.shape[-1])
    g_conv_b_s = shard_cols(g_conv_b, conv_b.shape[-1])
    g_wc_s = shard_cols(g_wc, w_conv_ffn.shape[-1])

    grads, deltas, new_m, new_v = {}, {}, {}, {}

    def step_2d(name, wv, ga, gb, mv, vv):
        shp = wv.shape
        r2 = lambda t: t.reshape(-1, shp[-1])
        g, d, mn, vn = adamw(r2(wv), r2(ga), None if gb is None else r2(gb), r2(mv), r2(vv))
        grads[name], deltas[name], new_m[name], new_v[name] = (t.reshape(shp) for t in (g, d, mn, vn))

    sums = {n: None for n in masters}
    for l in reversed(range(depth)):
        for half in (1, 0):
            send_sems, recv_sems, grads_h, lands = scattering[l, half]
            grads_h, landed = exchange_wait(f"scatter_wait_{l}_{half}", "scatter", send_sems, recv_sems, grads_h,
                                            lands, token)
            for (n, j), own, arr in zip(half_weights(l, half), grads_h, landed):
                sums[n] = sum_landed(arr, own, chip_id, j, masters[n].shape[0], sums[n])
    moments = {"w_in_ab": (m_w_in_ab, v_w_in_ab), "w_out_ab": (m_w_out_ab, v_w_out_ab),
               "w_qkv": (m_w_qkv, v_w_qkv), "w_o": (m_w_o, v_w_o), "w_up": (m_w_up, v_w_up),
               "w_down": (m_w_down, v_w_down)}
    for name, wv in masters.items():
        other = swap_with_sibling(sums[name])
        step_2d(name, wv, sums[name].reshape(wv.shape), other.reshape(wv.shape), *moments[name])
    step_2d("w_mod", w_mod, g_w_mod, None, m_w_mod, v_w_mod)

    small = [("c_ctx", c_ctx, g_c_ctx, m_c_ctx, v_c_ctx), ("b_mod", b_mod, g_b_mod, m_b_mod, v_b_mod),
             ("norm_mix", norm_mix, g_norm_mix, m_norm_mix, v_norm_mix),
             ("norm_ffn", norm_ffn, g_norm_ffn, m_norm_ffn, v_norm_ffn),
             ("conv_a", conv_a, g_conv_a_s, m_conv_a, v_conv_a), ("conv_b", conv_b, g_conv_b_s, m_conv_b, v_conv_b),
             ("conv_b_bias", conv_b_bias, g_vecs[:, 0], m_conv_b_bias, v_conv_b_bias),
             ("ln_b_gain", ln_b_gain, g_vecs[:, 1], m_ln_b_gain, v_ln_b_gain),
             ("ln_b_bias", ln_b_bias, g_vecs[:, 2], m_ln_b_bias, v_ln_b_bias),
             ("sinks", sinks, g_sinks, m_sinks, v_sinks),
             ("w_conv_ffn", w_conv_ffn, g_wc_s, m_w_conv_ffn, v_w_conv_ffn),
             ("final_norm", final_norm, g_final, m_final_norm, v_final_norm)]
    shapes = [t[1].shape for t in small]
    packed = [_pack([t[k] for t in small]) for k in (1, 2, 3, 4)]
    n_real = sum(t[1].size for t in small)
    lane_id = jnp.arange(packed[3].size).reshape(packed[3].shape)
    packed[3] = jnp.where(lane_id < n_real, packed[3], 1.0)
    outs = adamw(packed[0], packed[1], None, packed[2], packed[3])
    for (name, *_), g, d, mn, vn in zip(small, *[_unpack(o, shapes) for o in outs]):
        grads[name], deltas[name], new_m[name], new_v[name] = g, d, mn, vn

    order = ["c_ctx", "w_mod", "b_mod", "norm_mix", "norm_ffn", "w_in_ab", "conv_a", "conv_b", "conv_b_bias",
             "ln_b_gain", "ln_b_bias", "w_out_ab", "w_qkv", "w_o", "sinks", "w_up", "w_conv_ffn", "w_down",
             "final_norm"]
    return (loss, grad_x, *[grads[n] for n in order], *[deltas[n] for n in order],
            *[new_m[n] for n in order], *[new_v[n] for n in order])
```

```python
import jax
import jax.numpy as jnp
from jax import lax
from jax.experimental import pallas as pl
from jax.experimental.pallas import tpu as pltpu

F32 = jnp.float32
BF16 = jnp.bfloat16
MESH = pl.DeviceIdType.MESH

EPS = 1e-6
NEG_INF = -1e30
GRID_W = 64
HEAD_DIM = 64
N_HEADS = 16
WINDOW = 128
QB = 128
ROPE_THETA = 10000.0
A_W = 512
B_CONV = 31
D_FF = 2816
ADAM_LR, ADAM_B1, ADAM_B2, ADAM_EPS, ADAM_WD, ADAM_STEP = 0.001, 0.9, 0.999, 1e-8, 0.01, 10

TMR = 256
HALO = 16
N_DEV = 8
N_CHIP = 4


def _params(vmem_mb=None):
    if vmem_mb is None:
        return pltpu.CompilerParams()
    return pltpu.CompilerParams(vmem_limit_bytes=vmem_mb * 1024 * 1024)


def _row_tile(rows, cap=768):
    for t in (2816, 1408, 768, 704, 512, 384, 256, 128, 64, 32, 16, 8):
        if t <= cap and rows % t == 0:
            return t
    raise ValueError(rows)


def _colsum8(v):
    r, c = v.shape
    return v.reshape(r // 8, 8, c).sum(axis=0)


def _sigmoid(v):
    return 0.5 * jnp.tanh(0.5 * v) + 0.5


def mm_nn(a, w, out_dtype):
    R = a.shape[0]
    _, kb, nb = w.shape
    tm = _row_tile(R)

    def body(a_ref, w_ref, o_ref):
        av = a_ref[...].astype(BF16)
        for q in range(N_CHIP):
            o_ref[:, q * nb:(q + 1) * nb] = jnp.dot(av, w_ref[q], preferred_element_type=F32).astype(o_ref.dtype)
    return pl.pallas_call(
        body, name="mm_nn_col", grid=(R // tm,),
        in_specs=[pl.BlockSpec((tm, kb), lambda i: (i, 0)),
                  pl.BlockSpec((N_CHIP, kb, nb), lambda i: (0, 0, 0), pipeline_mode=pl.Buffered(1))],
        out_specs=pl.BlockSpec((tm, N_CHIP * nb), lambda i: (i, 0)),
        out_shape=jax.ShapeDtypeStruct((R, N_CHIP * nb), out_dtype),
        compiler_params=_params(48))(a, w)


def mm_nt(d, w, out_dtype):
    R = d.shape[0]
    _, kb, nb = w.shape
    tm = _row_tile(R)
    contract_last = (((1,), (1,)), ((), ()))
    resident = pl.BlockSpec((N_CHIP, kb, nb), lambda i: (0, 0, 0), pipeline_mode=pl.Buffered(1))

    def body(d_ref, w_ref, o_ref):
        wv = w_ref[...].reshape(N_CHIP * kb, nb)
        o_ref[...] = lax.dot_general(d_ref[...].astype(BF16), wv, contract_last,
                                     preferred_element_type=F32).astype(o_ref.dtype)
    return pl.pallas_call(
        body, name="mm_nt_row", grid=(R // tm,),
        in_specs=[pl.BlockSpec((tm, nb), lambda i: (i, 0)), resident],
        out_specs=pl.BlockSpec((tm, N_CHIP * kb), lambda i: (i, 0)),
        out_shape=jax.ShapeDtypeStruct((R, N_CHIP * kb), out_dtype),
        compiler_params=_params(48))(d, w)


def mm_tn(a, d, kind, like):
    R = a.shape[0]
    _, kb, nb = like.shape
    tm = _row_tile(R, 1408 if kind == "col" else 768)
    nsteps = R // tm
    contract_rows = (((0,), (0,)), ((), ()))
    out_shape = jax.ShapeDtypeStruct(like.shape, BF16)

    def accumulate(a_ref, d_ref, acc_ref):
        @pl.when(pl.program_id(1) == 0)
        def _():
            acc_ref[...] = jnp.zeros_like(acc_ref)
        acc_ref[...] += lax.dot_general(a_ref[...].astype(BF16), d_ref[...].astype(BF16), contract_rows,
                                        preferred_element_type=F32)
    if kind == "col":
        def body(a_ref, d_ref, o_ref, acc_ref):
            accumulate(a_ref, d_ref, acc_ref)

            @pl.when(pl.program_id(1) == nsteps - 1)
            def _():
                o_ref[...] = acc_ref[...].astype(BF16)
        return pl.pallas_call(
            body, name="mm_tn_col", grid=(N_CHIP, nsteps),
            in_specs=[pl.BlockSpec((tm, kb), lambda q, i: (i, 0)), pl.BlockSpec((tm, nb), lambda q, i: (i, q))],
            out_specs=pl.BlockSpec((None, kb, nb), lambda q, i: (q, 0, 0)), out_shape=out_shape,
            scratch_shapes=[pltpu.VMEM((kb, nb), F32)], compiler_params=_params(48))(a, d)
    tn = 512

    def body(a_ref, d_ref, o_ref, acc_ref):
        accumulate(a_ref, d_ref, acc_ref)

        @pl.when(pl.program_id(1) == nsteps - 1)
        def _():
            o_ref[...] = acc_ref[...].astype(BF16).reshape(N_CHIP, kb, tn)
    return pl.pallas_call(
        body, name="mm_tn_row", grid=(nb // tn, nsteps),
        in_specs=[pl.BlockSpec((tm, N_CHIP * kb), lambda n, i: (i, 0)), pl.BlockSpec((tm, tn), lambda n, i: (i, n))],
        out_specs=pl.BlockSpec((N_CHIP, kb, tn), lambda n, i: (0, 0, n)), out_shape=out_shape,
        scratch_shapes=[pltpu.VMEM((N_CHIP * kb, tn), F32)], compiler_params=_params(48))(a, d)


def _seg(i, T):
    return (i >= T // TMR).astype(jnp.int32)


def norm_mod_fwd(x, nw, mod, k, T):
    R, dm = x.shape

    def body(x_ref, nw_ref, mod_ref, h_ref):
        seg = _seg(pl.program_id(0), T)
        sh = mod_ref[seg, pl.ds(k, 1), :]
        sc = mod_ref[seg, pl.ds(k + 1, 1), :]
        xv = x_ref[...]
        r = lax.rsqrt(jnp.mean(xv * xv, axis=-1, keepdims=True) + EPS)
        h_ref[...] = ((xv * r * nw_ref[...]) * (1.0 + sc) + sh).astype(BF16)
    return pl.pallas_call(
        body, name="norm_mod_fwd", grid=(R // TMR,),
        in_specs=[pl.BlockSpec((TMR, dm), lambda i: (i, 0)),
                  pl.BlockSpec((1, dm), lambda i: (0, 0)),
                  pl.BlockSpec((2, 6, dm), lambda i: (0, 0, 0))],
        out_specs=pl.BlockSpec((TMR, dm), lambda i: (i, 0)),
        out_shape=jax.ShapeDtypeStruct((R, dm), BF16))(x, nw, mod)


def mm_resid_norm_fwd(a, w, x, nw, mod_g, mod_n, kg, kn, T):
    R, dm = x.shape
    _, kb, nb = w.shape

    def body(a_ref, w_ref, x_ref, nw_ref, mg_ref, mn_ref, y_ref, xo_ref, h_ref):
        seg = _seg(pl.program_id(0), T)
        yv = jnp.dot(a_ref[...].astype(BF16), w_ref[...].reshape(N_CHIP * kb, nb), preferred_element_type=F32)
        y_ref[...] = yv
        xv = x_ref[...] + mg_ref[seg, pl.ds(kg, 1), :] * yv
        xo_ref[...] = xv
        r = lax.rsqrt(jnp.mean(xv * xv, axis=-1, keepdims=True) + EPS)
        h_ref[...] = ((xv * r * nw_ref[...]) * (1.0 + mn_ref[seg, pl.ds(kn + 1, 1), :])
                      + mn_ref[seg, pl.ds(kn, 1), :]).astype(BF16)
    tile = pl.BlockSpec((TMR, dm), lambda i: (i, 0))
    modspec = pl.BlockSpec((2, 6, dm), lambda i: (0, 0, 0))
    return pl.pallas_call(
        body, name="mm_resid_norm_fwd", grid=(R // TMR,),
        in_specs=[pl.BlockSpec((TMR, N_CHIP * kb), lambda i: (i, 0)),
                  pl.BlockSpec((N_CHIP, kb, nb), lambda i: (0, 0, 0), pipeline_mode=pl.Buffered(1)),
                  tile, pl.BlockSpec((1, dm), lambda i: (0, 0)), modspec, modspec],
        out_specs=[tile, tile, tile],
        out_shape=[jax.ShapeDtypeStruct((R, dm), F32), jax.ShapeDtypeStruct((R, dm), F32),
                   jax.ShapeDtypeStruct((R, dm), BF16)],
        compiler_params=_params(48))(a, w, x, nw, mod_g, mod_n)


def mm_norm_resid_bwd(d, w, x, nw, mod_n, dxr, y, mod_g, kn, kg, T):
    R, dm = x.shape
    _, kb, nb = w.shape
    with_resid = y is not None
    contract_last = (((1,), (1,)), ((), ()))

    def body(*refs):
        if with_resid:
            d_ref, w_ref, x_ref, nw_ref, mn_ref, dxr_ref, y_ref, mg_ref, dx_ref, dy_ref, dmod_ref, dnw_ref, dg_ref = refs
        else:
            d_ref, w_ref, x_ref, nw_ref, mn_ref, dxr_ref, dx_ref, dmod_ref, dnw_ref = refs
        i = pl.program_id(0)
        seg = _seg(i, T)

        @pl.when(i == 0)
        def _():
            dmod_ref[...] = jnp.zeros_like(dmod_ref)
            dnw_ref[...] = jnp.zeros_like(dnw_ref)
            if with_resid:
                dg_ref[...] = jnp.zeros_like(dg_ref)
        dhv = None
        for q in range(N_CHIP):
            t = lax.dot_general(d_ref[:, q * nb:(q + 1) * nb].astype(BF16), w_ref[q], contract_last,
                                preferred_element_type=F32)
            dhv = t if dhv is None else dhv + t
        sc = mn_ref[seg, pl.ds(kn + 1, 1), :]
        nwv = nw_ref[...]
        xv = x_ref[...]
        r = lax.rsqrt(jnp.mean(xv * xv, axis=-1, keepdims=True) + EPS)
        xh = xv * r
        dmod_ref[seg, 0] += _colsum8(dhv)
        dmod_ref[seg, 1] += _colsum8(dhv * (xh * nwv))
        dn = dhv * (1.0 + sc)
        dnw_ref[...] += _colsum8(dn * xh)
        dxh = dn * nwv
        dx = dxr_ref[...] + r * (dxh - xh * jnp.mean(dxh * xh, axis=-1, keepdims=True))
        dx_ref[...] = dx
        if with_resid:
            dy_ref[...] = (mg_ref[seg, pl.ds(kg, 1), :] * dx).astype(BF16)
            dg_ref[seg] += _colsum8(dx * y_ref[...])
    tile = pl.BlockSpec((TMR, dm), lambda i: (i, 0))
    modspec = pl.BlockSpec((2, 6, dm), lambda i: (0, 0, 0))
    in_specs = [pl.BlockSpec((TMR, N_CHIP * nb), lambda i: (i, 0)),
                pl.BlockSpec((N_CHIP, kb, nb), lambda i: (0, 0, 0), pipeline_mode=pl.Buffered(1)),
                tile, pl.BlockSpec((1, dm), lambda i: (0, 0)), modspec, tile]
    acc_specs = [pl.BlockSpec((2, 2, 8, dm), lambda i: (0, 0, 0, 0)), pl.BlockSpec((8, dm), lambda i: (0, 0))]
    acc_shapes = [jax.ShapeDtypeStruct((2, 2, 8, dm), F32), jax.ShapeDtypeStruct((8, dm), F32)]
    if with_resid:
        return pl.pallas_call(
            body, name="mm_norm_resid_bwd", grid=(R // TMR,),
            in_specs=in_specs + [tile, modspec],
            out_specs=[tile, tile] + acc_specs + [pl.BlockSpec((2, 8, dm), lambda i: (0, 0, 0))],
            out_shape=[jax.ShapeDtypeStruct((R, dm), F32), jax.ShapeDtypeStruct((R, dm), BF16)] + acc_shapes
            + [jax.ShapeDtypeStruct((2, 8, dm), F32)],
            compiler_params=_params(48))(d, w, x, nw, mod_n, dxr, y, mod_g)
    return pl.pallas_call(
        body, name="mm_norm_bwd", grid=(R // TMR,), in_specs=in_specs,
        out_specs=[tile] + acc_specs, out_shape=[jax.ShapeDtypeStruct((R, dm), F32)] + acc_shapes,
        compiler_params=_params(48))(d, w, x, nw, mod_n, dxr)


def resid_bwd(dxn, y, mod, k, T):
    R, dm = dxn.shape

    def body(dx_ref, y_ref, mod_ref, dy_ref, dg_ref):
        i = pl.program_id(0)
        seg = _seg(i, T)

        @pl.when(i == 0)
        def _():
            dg_ref[...] = jnp.zeros_like(dg_ref)
        dxv = dx_ref[...]
        dy_ref[...] = (mod_ref[seg, pl.ds(k, 1), :] * dxv).astype(BF16)
        dg_ref[seg] += _colsum8(dxv * y_ref[...])
    tile = pl.BlockSpec((TMR, dm), lambda i: (i, 0))
    return pl.pallas_call(
        body, name="resid_bwd", grid=(R // TMR,),
        in_specs=[tile, tile, pl.BlockSpec((2, 6, dm), lambda i: (0, 0, 0))],
        out_specs=[tile, pl.BlockSpec((2, 8, dm), lambda i: (0, 0, 0))],
        out_shape=[jax.ShapeDtypeStruct((R, dm), BF16), jax.ShapeDtypeStruct((2, 8, dm), F32)])(dxn, y, mod)


def _halo_specs(width, R):
    nblk = R // HALO
    per = TMR // HALO
    return (pl.BlockSpec((HALO, width), lambda i: (jnp.maximum(i * per - 1, 0), 0)),
            pl.BlockSpec((TMR, width), lambda i: (i, 0)),
            pl.BlockSpec((HALO, width), lambda i: (jnp.minimum((i + 1) * per, nblk - 1), 0)))


def _halo_live(i, T, R):
    nl = T // TMR
    return (i != 0) & (i != nl), (i != nl - 1) & (i != R // TMR - 1)


def _ext(refs, c0, cw, live, halo=HALO):
    pref, ref, nref = refs
    before = jnp.where(live[0], pref[:, c0:c0 + cw].astype(F32)[HALO - halo:], 0.0)
    after = jnp.where(live[1], nref[:, c0:c0 + cw].astype(F32)[:halo], 0.0)
    return jnp.concatenate([before, ref[:, c0:c0 + cw].astype(F32), after], axis=0)


def _at(ext, off, halo=HALO):
    n = ext.shape[0]
    s = (-off) % n
    y = pltpu.roll(ext, s, 0) if s else ext
    return y[halo:halo + TMR]


def ffn_mid_fwd(u, wc, w_down, x, nw, mod_g, mod_n, kg, kn, T):
    R, w2 = u.shape
    dm = x.shape[1]
    _, kb, nb = w_down.shape
    cw = 256
    with_norm = nw is not None

    def body(*refs):
        if with_norm:
            up_ref, u_ref, un_ref, wc_ref, w_ref, x_ref, nw_ref, mg_ref, mn_ref, z_ref, y_ref, xo_ref, h_ref = refs
        else:
            up_ref, u_ref, un_ref, wc_ref, w_ref, x_ref, mg_ref, z_ref, y_ref, xo_ref = refs
        i = pl.program_id(0)
        seg = _seg(i, T)
        live = _halo_live(i, T, R)

        def conv(c0):
            e = _ext((up_ref, u_ref, un_ref), c0, cw, live, 8)
            return (wc_ref[pl.ds(0, 1), c0:c0 + cw] * _at(e, -1, 8) + wc_ref[pl.ds(1, 1), c0:c0 + cw] * _at(e, 0, 8)
                    + wc_ref[pl.ds(2, 1), c0:c0 + cw] * _at(e, 1, 8))
        yv = None
        for j in range(D_FF // cw):
            a = conv(j * cw)
            g = conv(D_FF + j * cw)
            zc = (g * _sigmoid(g) * a).astype(BF16)
            z_ref[:, j * cw:(j + 1) * cw] = zc
            t = jnp.dot(zc, w_ref[j * cw:(j + 1) * cw, :], preferred_element_type=F32)
            yv = t if yv is None else yv + t
        y_ref[...] = yv
        xv = x_ref[...] + mg_ref[seg, pl.ds(kg, 1), :] * yv
        xo_ref[...] = xv
        if with_norm:
            r = lax.rsqrt(jnp.mean(xv * xv, axis=-1, keepdims=True) + EPS)
            h_ref[...] = ((xv * r * nw_ref[...]) * (1.0 + mn_ref[seg, pl.ds(kn + 1, 1), :])
                          + mn_ref[seg, pl.ds(kn, 1), :]).astype(BF16)
    tile = pl.BlockSpec((TMR, dm), lambda i: (i, 0))
    modspec = pl.BlockSpec((2, 6, dm), lambda i: (0, 0, 0))
    w_down = w_down.reshape(N_CHIP * kb, nb)
    in_specs = [*_halo_specs(w2, R), pl.BlockSpec((3, w2), lambda i: (0, 0)),
                pl.BlockSpec((N_CHIP * kb, nb), lambda i: (0, 0), pipeline_mode=pl.Buffered(1)), tile]
    out_specs = [pl.BlockSpec((TMR, D_FF), lambda i: (i, 0)), tile, tile]
    out_shape = [jax.ShapeDtypeStruct((R, D_FF), BF16), jax.ShapeDtypeStruct((R, dm), F32),
                 jax.ShapeDtypeStruct((R, dm), F32)]
    if with_norm:
        return pl.pallas_call(
            body, name="ffn_mid_fwd", grid=(R // TMR,),
            in_specs=in_specs + [pl.BlockSpec((1, dm), lambda i: (0, 0)), modspec, modspec],
            out_specs=out_specs + [tile], out_shape=out_shape + [jax.ShapeDtypeStruct((R, dm), BF16)],
            compiler_params=_params(48))(u, u, u, wc, w_down, x, nw, mod_g, mod_n)
    return pl.pallas_call(
        body, name="ffn_mid_fwd_last", grid=(R // TMR,), in_specs=in_specs + [modspec],
        out_specs=out_specs, out_shape=out_shape, compiler_params=_params(48))(u, u, u, wc, w_down, x, mod_g)


def ffn_mid_bwd(dy, w_down, u, wc, T):
    R, w2 = u.shape
    _, kb, nb = w_down.shape
    cw = 256
    contract_last = (((1,), (1,)), ((), ()))

    def body(dy_ref, w_ref, up_ref, u_ref, un_ref, wc_ref, duc_ref, dwc_ref):
        i = pl.program_id(0)
        live = _halo_live(i, T, R)

        @pl.when(i == 0)
        def _():
            dwc_ref[...] = jnp.zeros_like(dwc_ref)

        def taps(c0):
            e = _ext((up_ref, u_ref, un_ref), c0, cw, live, 8)
            return [_at(e, -1, 8), _at(e, 0, 8), _at(e, 1, 8)]

        def conv(t, c0):
            return (wc_ref[pl.ds(0, 1), c0:c0 + cw] * t[0] + wc_ref[pl.ds(1, 1), c0:c0 + cw] * t[1]
                    + wc_ref[pl.ds(2, 1), c0:c0 + cw] * t[2])
        dyv = dy_ref[...]
        for j in range(D_FF // cw):
            ca, cg = j * cw, D_FF + j * cw
            dzv = lax.dot_general(dyv, w_ref[ca:ca + cw, :], contract_last, preferred_element_type=F32)
            ta, tg = taps(ca), taps(cg)
            a, g = conv(ta, ca), conv(tg, cg)
            sg = _sigmoid(g)
            da = dzv * (g * sg)
            dg = dzv * a * (sg * (1.0 + g * (1.0 - sg)))
            duc_ref[:, ca:ca + cw] = da.astype(BF16)
            duc_ref[:, cg:cg + cw] = dg.astype(BF16)
            for k in range(3):
                dwc_ref[k, :, ca:ca + cw] += _colsum8(da * ta[k])
                dwc_ref[k, :, cg:cg + cw] += _colsum8(dg * tg[k])
    return pl.pallas_call(
        body, name="ffn_mid_bwd", grid=(R // TMR,),
        in_specs=[pl.BlockSpec((TMR, nb), lambda i: (i, 0)),
                  pl.BlockSpec((N_CHIP * kb, nb), lambda i: (0, 0), pipeline_mode=pl.Buffered(1)),
                  *_halo_specs(w2, R), pl.BlockSpec((3, w2), lambda i: (0, 0))],
        out_specs=[pl.BlockSpec((TMR, w2), lambda i: (i, 0)), pl.BlockSpec((3, 8, w2), lambda i: (0, 0, 0))],
        out_shape=[jax.ShapeDtypeStruct((R, w2), BF16), jax.ShapeDtypeStruct((3, 8, w2), F32)],
        compiler_params=_params(48))(dy, w_down.reshape(N_CHIP * kb, nb), u, u, u, wc)


def ffn_up_bwd(duc, wc, w_up, x, nw, mod_n, dxr, y, mod_g, kn, kg, T):
    R, w2 = duc.shape
    dm = x.shape[1]
    _, kb, nb = w_up.shape
    cw = 128
    contract_last = (((1,), (1,)), ((), ()))

    def body(dp_ref, d_ref, dn_ref, wc_ref, w_ref, x_ref, nw_ref, mn_ref, dxr_ref, y_ref, mg_ref,
             du_ref, dx_ref, dy_ref, dmod_ref, dnw_ref, dg_ref):
        i = pl.program_id(0)
        seg = _seg(i, T)
        live = _halo_live(i, T, R)

        @pl.when(i == 0)
        def _():
            dmod_ref[...] = jnp.zeros_like(dmod_ref)
            dnw_ref[...] = jnp.zeros_like(dnw_ref)
            dg_ref[...] = jnp.zeros_like(dg_ref)
        dhv = None
        for q in range(N_CHIP):
            for j in range(nb // cw):
                c0 = q * nb + j * cw
                e = _ext((dp_ref, d_ref, dn_ref), c0, cw, live, 8)
                du_ref[:, c0:c0 + cw] = (wc_ref[pl.ds(0, 1), c0:c0 + cw] * _at(e, 1, 8)
                                         + wc_ref[pl.ds(1, 1), c0:c0 + cw] * _at(e, 0, 8)
                                         + wc_ref[pl.ds(2, 1), c0:c0 + cw] * _at(e, -1, 8)).astype(BF16)
            t = lax.dot_general(du_ref[:, q * nb:(q + 1) * nb], w_ref[q], contract_last,
                                preferred_element_type=F32)
            dhv = t if dhv is None else dhv + t
        sc = mn_ref[seg, pl.ds(kn + 1, 1), :]
        nwv = nw_ref[...]
        xv = x_ref[...]
        r = lax.rsqrt(jnp.mean(xv * xv, axis=-1, keepdims=True) + EPS)
        xh = xv * r
        dmod_ref[seg, 0] += _colsum8(dhv)
        dmod_ref[seg, 1] += _colsum8(dhv * (xh * nwv))
        dn = dhv * (1.0 + sc)
        dnw_ref[...] += _colsum8(dn * xh)
        dxh = dn * nwv
        dx = dxr_ref[...] + r * (dxh - xh * jnp.mean(dxh * xh, axis=-1, keepdims=True))
        dx_ref[...] = dx
        dy_ref[...] = (mg_ref[seg, pl.ds(kg, 1), :] * dx).astype(BF16)
        dg_ref[seg] += _colsum8(dx * y_ref[...])
    tile = pl.BlockSpec((TMR, dm), lambda i: (i, 0))
    modspec = pl.BlockSpec((2, 6, dm), lambda i: (0, 0, 0))
    return pl.pallas_call(
        body, name="ffn_up_bwd", grid=(R // TMR,),
        in_specs=[*_halo_specs(w2, R), pl.BlockSpec((3, w2), lambda i: (0, 0)),
                  pl.BlockSpec((N_CHIP, kb, nb), lambda i: (0, 0, 0), pipeline_mode=pl.Buffered(1)),
                  tile, pl.BlockSpec((1, dm), lambda i: (0, 0)), modspec, tile, tile, modspec],
        out_specs=[pl.BlockSpec((TMR, w2), lambda i: (i, 0)), tile, tile,
                   pl.BlockSpec((2, 2, 8, dm), lambda i: (0, 0, 0, 0)), pl.BlockSpec((8, dm), lambda i: (0, 0)),
                   pl.BlockSpec((2, 8, dm), lambda i: (0, 0, 0))],
        out_shape=[jax.ShapeDtypeStruct((R, w2), BF16), jax.ShapeDtypeStruct((R, dm), F32),
                   jax.ShapeDtypeStruct((R, dm), BF16), jax.ShapeDtypeStruct((2, 2, 8, dm), F32),
                   jax.ShapeDtypeStruct((8, dm), F32), jax.ShapeDtypeStruct((2, 8, dm), F32)],
        compiler_params=_params(48))(duc, duc, duc, wc, w_up, x, nw, mod_n, dxr, y, mod_g)


_CW = 128


def _mixer_a(prefs, wa_ref, live):
    cin = _ext(prefs, A_W, A_W, live) * _ext(prefs, 2 * A_W, A_W, live)
    ca = (wa_ref[pl.ds(0, 1), :] * _at(cin, -1) + wa_ref[pl.ds(1, 1), :] * _at(cin, 0)
          + wa_ref[pl.ds(2, 1), :] * _at(cin, 1))
    return cin, ca


def _mixer_b(prefs, wb_ref, bias_ref, live, ub_s, ub2_s):
    for cc in range(A_W // _CW):
        c0 = cc * _CW
        ub = _ext(prefs, 3 * A_W + c0, _CW, live) * _sigmoid(_ext(prefs, 4 * A_W + c0, _CW, live))
        ub_s[:, c0:c0 + _CW] = ub
        acc = jnp.zeros((TMR, _CW), F32) + bias_ref[:, c0:c0 + _CW]
        for k in range(B_CONV):
            acc = acc + wb_ref[pl.ds(k, 1), c0:c0 + _CW] * _at(ub, k - B_CONV // 2)
        ub2_s[:, c0:c0 + _CW] = acc


def _layernorm_stats(v):
    mu = jnp.mean(v, axis=-1, keepdims=True)
    xc = v - mu
    rs = lax.rsqrt(jnp.mean(xc * xc, axis=-1, keepdims=True) + EPS)
    return xc * rs, rs


def convmix_fwd(p, wa, wb, bias, lng, lnb, T):
    R, wp = p.shape

    def body(pp_ref, p_ref, pn_ref, wa_ref, wb_ref, bias_ref, lng_ref, lnb_ref, o_ref, ub_s, ub2_s):
        live = _halo_live(pl.program_id(0), T, R)
        prefs = (pp_ref, p_ref, pn_ref)
        _, ca = _mixer_a(prefs, wa_ref, live)
        o_ref[:, 0:A_W] = (p_ref[:, 0:A_W].astype(F32) * ca).astype(BF16)
        _mixer_b(prefs, wb_ref, bias_ref, live, ub_s, ub2_s)
        xh, _ = _layernorm_stats(ub2_s[...])
        lv = xh * lng_ref[...] + lnb_ref[...]
        o_ref[:, A_W:2 * A_W] = (lv * _sigmoid(lv)).astype(BF16)
    vec = pl.BlockSpec((1, A_W), lambda i: (0, 0))
    return pl.pallas_call(
        body, name="convmix_fwd", grid=(R // TMR,),
        in_specs=[*_halo_specs(wp, R), pl.BlockSpec((3, A_W), lambda i: (0, 0)),
                  pl.BlockSpec((B_CONV, A_W), lambda i: (0, 0)), vec, vec, vec],
        out_specs=pl.BlockSpec((TMR, 2 * A_W), lambda i: (i, 0)),
        out_shape=jax.ShapeDtypeStruct((R, 2 * A_W), BF16),
        scratch_shapes=[pltpu.VMEM((TMR + 2 * HALO, A_W), F32), pltpu.VMEM((TMR, A_W), F32)],
        compiler_params=_params(48))(p, p, p, wa, wb, bias, lng, lnb)


def convmix_bwd1(dyab, p, wa, wb, bias, lng, lnb, T):
    R, wp = p.shape

    def body(dy_ref, pp_ref, p_ref, pn_ref, wa_ref, wb_ref, bias_ref, lng_ref, lnb_ref,
             dmid_ref, dwa_ref, dwb_ref, dvec_ref, ub_s, ub2_s):
        i = pl.program_id(0)
        live = _halo_live(i, T, R)

        @pl.when(i == 0)
        def _():
            dwa_ref[...] = jnp.zeros_like(dwa_ref)
            dwb_ref[...] = jnp.zeros_like(dwb_ref)
            dvec_ref[...] = jnp.zeros_like(dvec_ref)
        prefs = (pp_ref, p_ref, pn_ref)
        cin, ca = _mixer_a(prefs, wa_ref, live)
        dya = dy_ref[:, 0:A_W]
        dmid_ref[:, 0:A_W] = dya * ca
        dca = dya * p_ref[:, 0:A_W].astype(F32)
        dmid_ref[:, A_W:2 * A_W] = dca
        for k in range(3):
            dwa_ref[k] += _colsum8(dca * _at(cin, k - 1))
        _mixer_b(prefs, wb_ref, bias_ref, live, ub_s, ub2_s)
        xh, rs = _layernorm_stats(ub2_s[...])
        gain = lng_ref[...]
        lv = xh * gain + lnb_ref[...]
        sl = _sigmoid(lv)
        dl = dy_ref[:, A_W:2 * A_W] * (sl * (1.0 + lv * (1.0 - sl)))
        dvec_ref[1] += _colsum8(dl * xh)
        dvec_ref[2] += _colsum8(dl)
        dxh = dl * gain
        dub2 = rs * (dxh - jnp.mean(dxh, axis=-1, keepdims=True)
                     - xh * jnp.mean(dxh * xh, axis=-1, keepdims=True))
        dvec_ref[0] += _colsum8(dub2)
        dmid_ref[:, 2 * A_W:3 * A_W] = dub2
        for cc in range(A_W // _CW):
            c0 = cc * _CW
            ub = ub_s[:, c0:c0 + _CW]
            d = dmid_ref[:, 2 * A_W + c0:2 * A_W + c0 + _CW]
            for k in range(B_CONV):
                dwb_ref[k, :, c0:c0 + _CW] += _colsum8(d * _at(ub, k - B_CONV // 2))
    vec = pl.BlockSpec((1, A_W), lambda i: (0, 0))
    return pl.pallas_call(
        body, name="convmix_bwd1", grid=(R // TMR,),
        in_specs=[pl.BlockSpec((TMR, 2 * A_W), lambda i: (i, 0)), *_halo_specs(wp, R),
                  pl.BlockSpec((3, A_W), lambda i: (0, 0)), pl.BlockSpec((B_CONV, A_W), lambda i: (0, 0)),
                  vec, vec, vec],
        out_specs=[pl.BlockSpec((TMR, 3 * A_W), lambda i: (i, 0)),
                   pl.BlockSpec((3, 8, A_W), lambda i: (0, 0, 0)),
                   pl.BlockSpec((B_CONV, 8, A_W), lambda i: (0, 0, 0)),
                   pl.BlockSpec((3, 8, A_W), lambda i: (0, 0, 0))],
        out_shape=[jax.ShapeDtypeStruct((R, 3 * A_W), F32), jax.ShapeDtypeStruct((3, 8, A_W), F32),
                   jax.ShapeDtypeStruct((B_CONV, 8, A_W), F32), jax.ShapeDtypeStruct((3, 8, A_W), F32)],
        scratch_shapes=[pltpu.VMEM((TMR + 2 * HALO, A_W), F32), pltpu.VMEM((TMR, A_W), F32)],
        compiler_params=_params(48))(dyab, p, p, p, wa, wb, bias, lng, lnb)


def convmix_bwd2(dmid, p, wa, wb, T):
    R, wp = p.shape

    def body(mp_ref, m_ref, mn_ref, p_ref, wa_ref, wb_ref, dp_ref):
        live = _halo_live(pl.program_id(0), T, R)
        mrefs = (mp_ref, m_ref, mn_ref)
        dp_ref[:, 0:A_W] = m_ref[:, 0:A_W].astype(BF16)
        dca = _ext(mrefs, A_W, A_W, live)
        dcin = (wa_ref[pl.ds(0, 1), :] * _at(dca, 1) + wa_ref[pl.ds(1, 1), :] * _at(dca, 0)
                + wa_ref[pl.ds(2, 1), :] * _at(dca, -1))
        dp_ref[:, A_W:2 * A_W] = (dcin * p_ref[:, 2 * A_W:3 * A_W].astype(F32)).astype(BF16)
        dp_ref[:, 2 * A_W:3 * A_W] = (dcin * p_ref[:, A_W:2 * A_W].astype(F32)).astype(BF16)
        for cc in range(A_W // _CW):
            c0 = cc * _CW
            d = _ext(mrefs, 2 * A_W + c0, _CW, live)
            dub = jnp.zeros((TMR, _CW), F32)
            for k in range(B_CONV):
                dub = dub + wb_ref[pl.ds(k, 1), c0:c0 + _CW] * _at(d, B_CONV // 2 - k)
            vb = p_ref[:, 3 * A_W + c0:3 * A_W + c0 + _CW].astype(F32)
            s = _sigmoid(p_ref[:, 4 * A_W + c0:4 * A_W + c0 + _CW].astype(F32))
            dp_ref[:, 3 * A_W + c0:3 * A_W + c0 + _CW] = (dub * s).astype(BF16)
            dp_ref[:, 4 * A_W + c0:4 * A_W + c0 + _CW] = (dub * vb * s * (1.0 - s)).astype(BF16)
    return pl.pallas_call(
        body, name="convmix_bwd2", grid=(R // TMR,),
        in_specs=[*_halo_specs(3 * A_W, R), pl.BlockSpec((TMR, wp), lambda i: (i, 0)),
                  pl.BlockSpec((3, A_W), lambda i: (0, 0)), pl.BlockSpec((B_CONV, A_W), lambda i: (0, 0))],
        out_specs=pl.BlockSpec((TMR, wp), lambda i: (i, 0)),
        out_shape=jax.ShapeDtypeStruct((R, wp), BF16), compiler_params=_params(48))(dmid, dmid, dmid, p, wa, wb)


def _rot_half(v):
    w = v.shape[-1]
    lane = lax.broadcasted_iota(jnp.int32, (1, w), 1)
    return jnp.where(lane % HEAD_DIM < HEAD_DIM // 2, pltpu.roll(v, w - HEAD_DIM // 2, 1),
                     pltpu.roll(v, HEAD_DIM // 2, 1))


def rope_fwd(qkv, cs, sn):
    R, wq = qkv.shape
    qw = N_HEADS * HEAD_DIM
    kw = (wq - qw) // 2
    scale = HEAD_DIM ** -0.5

    def body(x_ref, cs_ref, sn_ref, o_ref):
        c, s = cs_ref[...], sn_ref[...]
        q = x_ref[:, 0:qw]
        o_ref[:, 0:qw] = ((q * jnp.tile(c, (1, qw // 128)) + _rot_half(q) * jnp.tile(s, (1, qw // 128)))
                          * scale).astype(BF16)
        k = x_ref[:, qw:qw + kw]
        o_ref[:, qw:qw + kw] = (k * jnp.tile(c, (1, kw // 128))
                                + _rot_half(k) * jnp.tile(s, (1, kw // 128))).astype(BF16)
        o_ref[:, qw + kw:] = x_ref[:, qw + kw:].astype(BF16)
    tab = pl.BlockSpec((TMR, 128), lambda i: (i, 0))
    return pl.pallas_call(
        body, name="rope_fwd", grid=(R // TMR,),
        in_specs=[pl.BlockSpec((TMR, wq), lambda i: (i, 0)), tab, tab],
        out_specs=pl.BlockSpec((TMR, wq), lambda i: (i, 0)),
        out_shape=jax.ShapeDtypeStruct((R, wq), BF16))(qkv, cs, sn)


def rope_bwd(dq, dks, dvs, dkc, dvc, cs, sn, T):
    R, qw = dq.shape
    kw = dkc.shape[1]
    nb = R // QB
    nl = T // QB
    scale = HEAD_DIM ** -0.5

    def body(dq_ref, kp_ref, ko_ref, kn_ref, vp_ref, vo_ref, vn_ref, kc_ref, vc_ref, cs_ref, sn_ref, o_ref):
        b = pl.program_id(0)
        c, s = cs_ref[...], sn_ref[...]
        has_next = (b + 1 < nb).astype(F32)
        has_prev = (b >= 1).astype(F32)
        is_ctx = (b >= nl).astype(F32)
        g = dq_ref[...] * scale
        o_ref[:, 0:qw] = (g * jnp.tile(c, (1, qw // 128)) + _rot_half(g * jnp.tile(s, (1, qw // 128)))).astype(BF16)
        g = ko_ref[...] + kp_ref[...] * has_next + kn_ref[...] * has_prev + kc_ref[...] * is_ctx
        o_ref[:, qw:qw + kw] = (g * jnp.tile(c, (1, kw // 128))
                                + _rot_half(g * jnp.tile(s, (1, kw // 128)))).astype(BF16)
        o_ref[:, qw + kw:] = (vo_ref[...] + vp_ref[...] * has_next + vn_ref[...] * has_prev
                              + vc_ref[...] * is_ctx).astype(BF16)
    own = pl.BlockSpec((QB, kw), lambda b: (b, 0))
    from_next = pl.BlockSpec((QB, kw), lambda b: (jnp.minimum(b + 1, nb - 1), 0))
    from_prev = pl.BlockSpec((QB, kw), lambda b: (jnp.maximum(b - 1, 0), 0))
    ctx = pl.BlockSpec((QB, kw), lambda b: (jnp.maximum(b - nl, 0), 0))
    tab = pl.BlockSpec((QB, 128), lambda b: (b, 0))
    return pl.pallas_call(
        body, name="rope_bwd", grid=(nb,),
        in_specs=[pl.BlockSpec((QB, qw), lambda b: (b, 0)), from_next, own, from_prev, from_next, own, from_prev,
                  ctx, ctx, tab, tab],
        out_specs=pl.BlockSpec((QB, qw + 2 * kw), lambda b: (b, 0)),
        out_shape=jax.ShapeDtypeStruct((R, qw + 2 * kw), BF16))(
            dq, dks[0], dks[1], dks[2], dvs[0], dvs[1], dvs[2], dkc, dvc, cs, sn)


def _attn_specs(T, R):
    nl = T // QB
    qcols = N_HEADS * HEAD_DIM // 128
    kcols = 2

    def band(col0, shift):
        return pl.BlockSpec((QB, 128), lambda jj, b: (jnp.clip(b + shift, 0, nl - 1), col0 + jj))

    def ctx(col0):
        return pl.BlockSpec((R - T, 128), lambda jj, b: (T // (R - T), col0 + jj))
    q = pl.BlockSpec((QB, 512), lambda jj, b: (b, jj))
    k0, v0 = qcols, qcols + kcols
    return q, [band(k0, -1), band(k0, 0), band(k0, 1), ctx(k0)], [band(v0, -1), band(v0, 0), band(v0, 1), ctx(v0)]


def _attn_common(T, R):
    nl = T // QB
    nk = 3 * QB + (R - T)

    def low_lanes():
        return lax.broadcasted_iota(jnp.int32, (1, 128), 1) < HEAD_DIM

    def dup(v, par):
        low = low_lanes()
        vf = v.astype(F32)
        r = pltpu.roll(vf, HEAD_DIM, 1)
        return (jnp.where(low, vf, r) if par == 0 else jnp.where(low, r, vf)).astype(BF16)

    def stack(ref, par):
        low = low_lanes()
        pa = ref[:, (2 * par) * 128:(2 * par + 1) * 128].astype(BF16)
        pb = ref[:, (2 * par + 1) * 128:(2 * par + 2) * 128].astype(BF16)
        zero = jnp.zeros_like(pa)
        return jnp.concatenate([jnp.where(low, pa, zero), jnp.where(low, zero, pa),
                                jnp.where(low, pb, zero), jnp.where(low, zero, pb)], axis=0)

    def unstack(v):
        low = low_lanes()
        return (jnp.where(low, v[0:QB], v[QB:2 * QB]), jnp.where(low, v[2 * QB:3 * QB], v[3 * QB:4 * QB]))

    def mask_of(b):
        col = lax.broadcasted_iota(jnp.int32, (1, nk), 1)
        gone = (((col < QB) & (b == 0)) | ((col >= 2 * QB) & (col < 3 * QB) & (b == nl - 1))
                | ((col < 3 * QB) & (b >= nl)))
        return jnp.where(gone, NEG_INF, 0.0)

    def sink_col(sink_ref, first):
        blk = lax.broadcasted_iota(jnp.int32, (4 * QB, 1), 0) // QB
        out = jnp.zeros((4 * QB, 1), F32) + sink_ref[first]
        for h in range(1, 4):
            out = jnp.where(blk == h, sink_ref[first + h], out)
        return out

    def scores(qs, kd, mask, sink):
        s = lax.dot_general(qs, kd, (((1,), (1,)), ((), ())), preferred_element_type=F32) + mask
        m = jnp.maximum(jnp.max(s, axis=-1, keepdims=True), sink)
        e = jnp.exp(s - m)
        es = jnp.exp(sink - m)
        return e, es, 1.0 / (jnp.sum(e, axis=-1, keepdims=True) + es)
    return low_lanes, dup, stack, unstack, mask_of, sink_col, scores


def window_bias(T, R):
    nk = 3 * QB + (R - T)
    row = jnp.arange(QB)[:, None]
    col = jnp.arange(nk)[None, :]
    near = (jnp.abs(col - QB - row) <= WINDOW) | (col >= 3 * QB)
    return jnp.tile(jnp.where(near, 0.0, NEG_INF).astype(F32), (4, 1))


def attn_fwd(qkvr, sinks, bias, T):
    R = qkvr.shape[0]
    qspec, kspecs, vspecs = _attn_specs(T, R)
    _, dup, stack, unstack, mask_of, sink_col, scores = _attn_common(T, R)

    def body(q_ref, kp, ko, kn, kc, vp, vo, vn, vc, sink_ref, bias_ref, o_ref):
        jj, b = pl.program_id(0), pl.program_id(1)
        mask = bias_ref[...] + mask_of(b)
        k_all = jnp.concatenate([kp[...], ko[...], kn[...], kc[...]], axis=0)
        v_all = jnp.concatenate([vp[...], vo[...], vn[...], vc[...]], axis=0)
        for par in range(2):
            kd, vd = dup(k_all, par), dup(v_all, par)
            e, _, rz = scores(stack(q_ref, par), kd, mask, sink_col(sink_ref, jj * 8 + par * 4))
            o = jnp.dot((e * rz).astype(BF16), vd, preferred_element_type=F32)
            pa, pb = unstack(o)
            o_ref[:, (2 * par) * 128:(2 * par + 1) * 128] = pa.astype(BF16)
            o_ref[:, (2 * par + 1) * 128:(2 * par + 2) * 128] = pb.astype(BF16)
    return pl.pallas_call(
        body, name="attn_fwd", grid=(2, R // QB),
        in_specs=[qspec, *kspecs, *vspecs, pl.BlockSpec(memory_space=pltpu.SMEM),
                  pl.BlockSpec(bias.shape, lambda jj, b: (0, 0))],
        out_specs=pl.BlockSpec((QB, 512), lambda jj, b: (b, jj)),
        out_shape=jax.ShapeDtypeStruct((R, N_HEADS * HEAD_DIM), BF16),
        compiler_params=_params(48))(qkvr, *([qkvr] * 8), sinks, bias)


def attn_bwd(qkvr, do, sinks, bias, T):
    R = qkvr.shape[0]
    tc = R - T
    qspec, kspecs, vspecs = _attn_specs(T, R)
    _, dup, stack, unstack, mask_of, sink_col, scores = _attn_common(T, R)
    contract_rows = (((0,), (0,)), ((), ()))
    contract_last = (((1,), (1,)), ((), ()))

    def body(q_ref, kp, ko, kn, kc, vp, vo, vn, vc, do_ref, sink_ref, bias_ref,
             dq_ref, dkp, dko, dkn, dvp, dvo, dvn, dkc_ref, dvc_ref, dsink_ref):
        jj, b = pl.program_id(0), pl.program_id(1)

        @pl.when((jj == 0) & (b == 0))
        def _():
            dsink_ref[...] = jnp.zeros_like(dsink_ref)

        @pl.when(b == 0)
        def _():
            dkc_ref[...] = jnp.zeros_like(dkc_ref)
            dvc_ref[...] = jnp.zeros_like(dvc_ref)
        mask = bias_ref[...] + mask_of(b)
        k_all = jnp.concatenate([kp[...], ko[...], kn[...], kc[...]], axis=0)
        v_all = jnp.concatenate([vp[...], vo[...], vn[...], vc[...]], axis=0)
        lane = lax.broadcasted_iota(jnp.int32, (8, 128), 1)
        srow = lax.broadcasted_iota(jnp.int32, (8, 128), 0)
        dk_fold, dv_fold = [], []
        for par in range(2):
            kd, vd = dup(k_all, par), dup(v_all, par)
            first = jj * 8 + par * 4
            qs, dos = stack(q_ref, par), stack(do_ref, par)
            e, es, rz = scores(qs, kd, mask, sink_col(sink_ref, first))
            p = e * rz
            dp = lax.dot_general(dos, vd, contract_last, preferred_element_type=F32)
            delta = jnp.sum(p * dp, axis=-1, keepdims=True)
            ds = (p * (dp - delta)).astype(BF16)
            t = es * rz * delta
            for h in range(4):
                dsink = -jnp.sum(t[h * QB:(h + 1) * QB])
                dsink_ref[...] += jnp.where((lane == first + h) & (srow == 0), dsink, 0.0)
            pa, pb = unstack(jnp.dot(ds, kd, preferred_element_type=F32))
            dq_ref[:, (2 * par) * 128:(2 * par + 1) * 128] = pa
            dq_ref[:, (2 * par + 1) * 128:(2 * par + 2) * 128] = pb
            dk_t = lax.dot_general(qs, ds, contract_rows, preferred_element_type=F32)
            dv_t = lax.dot_general(dos, p.astype(BF16), contract_rows, preferred_element_type=F32)
            dk_fold.append(dk_t + pltpu.roll(dk_t, HEAD_DIM, 0))
            dv_fold.append(dv_t + pltpu.roll(dv_t, HEAD_DIM, 0))
        low_rows = lax.broadcasted_iota(jnp.int32, (128, 1), 0) < HEAD_DIM
        dk = jnp.where(low_rows, dk_fold[0], dk_fold[1]).T
        dv = jnp.where(low_rows, dv_fold[0], dv_fold[1]).T
        dkp[...], dko[...], dkn[...] = dk[0:QB], dk[QB:2 * QB], dk[2 * QB:3 * QB]
        dvp[...], dvo[...], dvn[...] = dv[0:QB], dv[QB:2 * QB], dv[2 * QB:3 * QB]
        dkc_ref[...] += dk[3 * QB:]
        dvc_ref[...] += dv[3 * QB:]
    blk = pl.BlockSpec((QB, 128), lambda jj, b: (b, jj))
    cblk = pl.BlockSpec((tc, 128), lambda jj, b: (0, jj))
    part = jax.ShapeDtypeStruct((R, 256), F32)
    csum = jax.ShapeDtypeStruct((tc, 256), F32)
    outs = pl.pallas_call(
        body, name="attn_bwd", grid=(2, R // QB),
        in_specs=[qspec, *kspecs, *vspecs, pl.BlockSpec((QB, 512), lambda jj, b: (b, jj)),
                  pl.BlockSpec(memory_space=pltpu.SMEM), pl.BlockSpec(bias.shape, lambda jj, b: (0, 0))],
        out_specs=[pl.BlockSpec((QB, 512), lambda jj, b: (b, jj)), blk, blk, blk, blk, blk, blk, cblk, cblk,
                   pl.BlockSpec((8, 128), lambda jj, b: (0, 0))],
        out_shape=[jax.ShapeDtypeStruct((R, N_HEADS * HEAD_DIM), F32), part, part, part, part, part, part,
                   csum, csum, jax.ShapeDtypeStruct((8, 128), F32)],
        compiler_params=_params(48))(qkvr, *([qkvr] * 8), do, sinks, bias)
    return outs[0], outs[1:4], outs[4:7], outs[7], outs[8], outs[9]


def loss_head(x, nw, target, T):
    R, dm = x.shape
    nl = T // TMR

    def body(x_ref, nw_ref, t_ref, loss_ref, dx_ref, dnw_ref):
        i = pl.program_id(0)

        @pl.when(i == 0)
        def _():
            loss_ref[...] = jnp.zeros_like(loss_ref)
            dnw_ref[...] = jnp.zeros_like(dnw_ref)
        live = (i < nl).astype(F32)
        nwv = nw_ref[...]
        xv = x_ref[...]
        r = lax.rsqrt(jnp.mean(xv * xv, axis=-1, keepdims=True) + EPS)
        xh = xv * r
        err = xh * nwv - t_ref[...]
        per_row = jnp.mean(err * err, axis=-1, keepdims=True)
        loss_ref[...] += 0.5 * live * jnp.sum(per_row)
        dy = err * (live / dm)
        dnw_ref[...] += _colsum8(dy * xh)
        dxh = dy * nwv
        dx_ref[...] = r * (dxh - xh * jnp.mean(dxh * xh, axis=-1, keepdims=True))
    tile = pl.BlockSpec((TMR, dm), lambda i: (i, 0))
    return pl.pallas_call(
        body, name="loss_head", grid=(R // TMR,),
        in_specs=[tile, pl.BlockSpec((1, dm), lambda i: (0, 0)),
                  pl.BlockSpec((TMR, dm), lambda i: (jnp.minimum(i, nl - 1), 0))],
        out_specs=[pl.BlockSpec((8, 128), lambda i: (0, 0)), tile, pl.BlockSpec((8, dm), lambda i: (0, 0))],
        out_shape=[jax.ShapeDtypeStruct((8, 128), F32), jax.ShapeDtypeStruct((R, dm), F32),
                   jax.ShapeDtypeStruct((8, dm), F32)])(x, nw, target)


def adaln_fwd(cond, w_mod, b_mod):
    nl, dm, ns = w_mod.shape

    def body(c_ref, w_ref, b_ref, o_ref):
        cv = c_ref[...]
        s = (cv * _sigmoid(cv)).astype(BF16)
        o_ref[...] = jnp.dot(s, w_ref[...].astype(BF16), preferred_element_type=F32) + b_ref[...]
    return pl.pallas_call(
        body, name="adaln_fwd", grid=(nl,),
        in_specs=[pl.BlockSpec((16, dm), lambda l: (0, 0)), pl.BlockSpec((None, dm, ns), lambda l: (l, 0, 0)),
                  pl.BlockSpec((None, 1, ns), lambda l: (l, 0, 0))],
        out_specs=pl.BlockSpec((None, 16, ns), lambda l: (l, 0, 0)),
        out_shape=jax.ShapeDtypeStruct((nl, 16, ns), F32), compiler_params=_params(48))(cond, w_mod, b_mod)


def adaln_bwd(cond, dmod, w_mod):
    nl, dm, ns = w_mod.shape

    def body(c_ref, d_ref, w_ref, gw_ref, ds_ref):
        l = pl.program_id(0)

        @pl.when(l == 0)
        def _():
            ds_ref[...] = jnp.zeros_like(ds_ref)
        cv = c_ref[...]
        s = (cv * _sigmoid(cv)).astype(BF16)
        dv = d_ref[...].astype(BF16)
        gw_ref[...] = lax.dot_general(s, dv, (((0,), (0,)), ((), ())), preferred_element_type=F32)
        ds_ref[...] += lax.dot_general(dv, w_ref[...].astype(BF16), (((1,), (1,)), ((), ())),
                                       preferred_element_type=F32)
    return pl.pallas_call(
        body, name="adaln_bwd", grid=(nl,),
        in_specs=[pl.BlockSpec((16, dm), lambda l: (0, 0)), pl.BlockSpec((None, 16, ns), lambda l: (l, 0, 0)),
                  pl.BlockSpec((None, dm, ns), lambda l: (l, 0, 0))],
        out_specs=[pl.BlockSpec((None, dm, ns), lambda l: (l, 0, 0)), pl.BlockSpec((16, dm), lambda l: (0, 0))],
        out_shape=[jax.ShapeDtypeStruct((nl, dm, ns), F32), jax.ShapeDtypeStruct((16, dm), F32)],
        compiler_params=_params(48))(cond, dmod, w_mod)


def _me():
    return lax.axis_index("x"), lax.axis_index("y"), lax.axis_index("c")


def allgather8(block):
    m_per, n = block.shape

    def body(x_ref, out_ref, send_sems, recv_sems, local_sem):
        x, y, c = _me()
        me, sibling = (x, y, c), (x, y, 1 - c)
        chips = [(1 - x, y), (x, 1 - y), (1 - x, 1 - y)]

        def rows(px, py, pc):
            return out_ref.at[pl.ds((4 * px + 2 * py + pc) * m_per, m_per), :]

        def copy(k, blk, to, src=None):
            return pltpu.make_async_remote_copy(
                src_ref=rows(*blk) if src is None else src, dst_ref=rows(*blk),
                send_sem=send_sems.at[k], recv_sem=recv_sems.at[k], device_id=to, device_id_type=MESH)
        mine = pltpu.make_async_copy(x_ref, rows(*me), local_sem)
        mine.start()
        first = [copy(0, me, sibling, src=x_ref)]
        first += [copy(1 + j, me, (*chip, c), src=x_ref) for j, chip in enumerate(chips)]
        for cp in first:
            cp.start()
        passed = [copy(4 + j, (*chip, c), sibling) for j, chip in enumerate(chips)]
        for j, chip in enumerate(chips):
            copy(1 + j, (*chip, c), me).wait_recv()
            passed[j].start()
        copy(0, sibling, me).wait_recv()
        for j, chip in enumerate(chips):
            copy(4 + j, (*chip, 1 - c), me).wait_recv()
        for cp in first + passed:
            cp.wait_send()
        mine.wait()
    return pl.pallas_call(
        body, name="allgather8",
        out_shape=jax.ShapeDtypeStruct((N_DEV * m_per, n), block.dtype),
        in_specs=[pl.BlockSpec(memory_space=pltpu.VMEM)],
        out_specs=pl.BlockSpec(memory_space=pltpu.VMEM),
        scratch_shapes=[pltpu.SemaphoreType.DMA((7,)), pltpu.SemaphoreType.DMA((7,)), pltpu.SemaphoreType.DMA],
        compiler_params=_params(48))(block)


def _other_chips(x, y):
    return [(1 - x, y), (x, 1 - y), (1 - x, 1 - y)]


_HBM = pl.BlockSpec(memory_space=pltpu.HBM)
_SEM = pl.BlockSpec(memory_space=pltpu.SEMAPHORE)
_ANY = pl.BlockSpec(memory_space=pl.ANY)
_EFFECT = pltpu.SideEffectType.DATAFLOW_SIDE_EFFECTING


def _in_hbm(v):
    return pltpu.with_memory_space_constraint(v, pltpu.HBM)


def cast_into_slot(w, chip_id):
    kb, nb = w.shape
    tr = _row_tile(kb)

    def body(chip_ref, w_ref, o_ref):
        del chip_ref
        o_ref[...] = w_ref[...].astype(BF16)
    return pl.pallas_call(
        body, name="cast_into_slot",
        grid_spec=pltpu.PrefetchScalarGridSpec(
            num_scalar_prefetch=1, grid=(kb // tr,),
            in_specs=[pl.BlockSpec((tr, nb), lambda i, chip: (i, 0))],
            out_specs=pl.BlockSpec((None, tr, nb), lambda i, chip: (chip[0], i, 0))),
        out_shape=jax.ShapeDtypeStruct((N_CHIP, kb, nb), BF16))(chip_id, w)


def _split_copies(mode, srcs, lands, send_sems, recv_sems):
    x, y, c = _me()
    out = []
    for t in range(len(lands)):
        for k, chip in enumerate(_other_chips(x, y)):
            if mode == "gather":
                src = dst = lands[t].at[2 * x + y]
                landed = lands[t].at[2 * chip[0] + chip[1]]
            else:
                src, dst, landed = srcs[t].at[2 * chip[0] + chip[1]], lands[t].at[k], lands[t].at[k]
            send = pltpu.make_async_remote_copy(src_ref=src, dst_ref=dst, send_sem=send_sems.at[3 * t + k],
                                                recv_sem=recv_sems.at[3 * t + k], device_id=(*chip, c),
                                                device_id_type=MESH)
            recv = pltpu.make_async_remote_copy(src_ref=src, dst_ref=landed, send_sem=send_sems.at[3 * t + k],
                                                recv_sem=recv_sems.at[3 * t + k], device_id=(*chip, c),
                                                device_id_type=MESH)
            out.append((send, recv))
    return out


def exchange_start(name, mode, srcs, lands, after):
    ns, nl = len(srcs), len(lands)
    na = ns + nl

    def body(*refs):
        src_refs, land_refs = refs[:ns], refs[ns:na]
        send_sems, recv_sems = refs[na + 1], refs[na + 2]
        token = refs[-1]
        for send, _ in _split_copies(mode, src_refs, land_refs, send_sems, recv_sems):
            send.start()
        token[...] = jnp.zeros_like(token)
    arrays = list(srcs) + list(lands)
    outs = pl.pallas_call(
        body, name=name,
        out_shape=(pltpu.SemaphoreType.DMA((3 * nl,)), pltpu.SemaphoreType.DMA((3 * nl,)),
                   *[pltpu.HBM(v.shape, v.dtype) for v in arrays], jax.ShapeDtypeStruct((8, 128), F32)),
        in_specs=[_HBM] * na + [_ANY],
        out_specs=(_SEM, _SEM, *[_HBM] * na, pl.BlockSpec(memory_space=pltpu.VMEM)),
        input_output_aliases={i: 2 + i for i in range(na)},
        compiler_params=pltpu.CompilerParams(has_side_effects=_EFFECT))(*[_in_hbm(v) for v in arrays], after)
    return outs[0], outs[1], list(outs[2:2 + ns]), list(outs[2 + ns:2 + na]), outs[-1]


def exchange_wait(name, mode, send_sems, recv_sems, srcs, lands, after):
    ns, nl = len(srcs), len(lands)
    na = ns + nl

    def body(*refs):
        for _, recv in _split_copies(mode, refs[:ns], refs[ns:na], refs[na], refs[na + 1]):
            recv.wait_send()
            recv.wait_recv()
    arrays = list(srcs) + list(lands)
    outs = pl.pallas_call(
        body, name=name,
        out_shape=[pltpu.HBM(v.shape, v.dtype) for v in arrays],
        in_specs=[_HBM] * na + [_SEM, _SEM, _ANY], out_specs=[_HBM] * na,
        input_output_aliases={i: i for i in range(na)},
        compiler_params=pltpu.CompilerParams(has_side_effects=_EFFECT))(*arrays, send_sems, recv_sems, after)
    return list(outs[:ns]), list(outs[ns:])


def swap_with_sibling(v):
    def body(v_ref, out_ref, send_sem, recv_sem):
        x, y, c = _me()
        cp = pltpu.make_async_remote_copy(src_ref=v_ref, dst_ref=out_ref, send_sem=send_sem, recv_sem=recv_sem,
                                          device_id=(x, y, 1 - c), device_id_type=MESH)
        cp.start()
        cp.wait()
    return pl.pallas_call(
        body, name="swap_with_sibling", out_shape=jax.ShapeDtypeStruct(v.shape, v.dtype),
        in_specs=[pl.BlockSpec(memory_space=pl.ANY)], out_specs=pl.BlockSpec(memory_space=pl.ANY),
        scratch_shapes=[pltpu.SemaphoreType.DMA, pltpu.SemaphoreType.DMA])(v)


def sum_slots(parts):
    n, rows, w = parts.shape
    tr = _row_tile(rows)

    def body(p_ref, o_ref):
        acc = p_ref[0].astype(F32)
        for k in range(1, n):
            acc = acc + p_ref[k].astype(F32)
        o_ref[...] = acc
    return pl.pallas_call(
        body, name="sum_slots", grid=(rows // tr,),
        in_specs=[pl.BlockSpec((n, tr, w), lambda i: (0, i, 0))], out_specs=pl.BlockSpec((tr, w), lambda i: (i, 0)),
        out_shape=jax.ShapeDtypeStruct((rows, w), F32), compiler_params=_params(48))(parts)


def sum_landed(landed, own, chip_id, layer, n_layers, buf):
    n, rows, w = landed.shape
    tr = _row_tile(rows)
    base = layer * (rows // tr)

    def compute(l_ref, g_ref, o_ref):
        acc = g_ref[...].astype(F32)
        for k in range(n):
            acc = acc + l_ref[k].astype(F32)
        o_ref[...] = acc
    in_specs = [pl.BlockSpec((n, tr, w), lambda i, chip: (0, i, 0)),
                pl.BlockSpec((None, tr, w), lambda i, chip: (chip[0], i, 0))]
    out_spec = pl.BlockSpec((tr, w), lambda i, chip: (base + i, 0))
    out_shape = jax.ShapeDtypeStruct((n_layers * rows, w), F32)
    if buf is None:
        def body(chip_ref, l_ref, g_ref, o_ref):
            del chip_ref
            compute(l_ref, g_ref, o_ref)
        return pl.pallas_call(
            body, name="sum_landed",
            grid_spec=pltpu.PrefetchScalarGridSpec(num_scalar_prefetch=1, grid=(rows // tr,), in_specs=in_specs,
                                                   out_specs=out_spec),
            out_shape=out_shape, compiler_params=_params(48))(chip_id, landed, own)

    def body(chip_ref, l_ref, g_ref, buf_ref, o_ref):
        del chip_ref, buf_ref
        compute(l_ref, g_ref, o_ref)
    return pl.pallas_call(
        body, name="sum_landed_into",
        grid_spec=pltpu.PrefetchScalarGridSpec(num_scalar_prefetch=1, grid=(rows // tr,),
                                               in_specs=in_specs + [_ANY], out_specs=out_spec),
        out_shape=out_shape, input_output_aliases={3: 0}, compiler_params=_params(48))(chip_id, landed, own, buf)


def adamw(w, ga, gb, m, v):
    rows, wd = w.shape
    tr = min(_row_tile(rows), 128)
    c1 = 1.0 / (1.0 - ADAM_B1 ** ADAM_STEP)
    c2 = 1.0 / (1.0 - ADAM_B2 ** ADAM_STEP)

    def update(wv, g, mv, vv, g_ref, d_ref, m_ref, v_ref):
        mn = ADAM_B1 * mv + (1.0 - ADAM_B1) * g
        vn = ADAM_B2 * vv + (1.0 - ADAM_B2) * (g * g)
        g_ref[...] = g
        m_ref[...] = mn
        v_ref[...] = vn
        d_ref[...] = -ADAM_LR * ((mn * c1) / (jnp.sqrt(vn * c2) + ADAM_EPS) + ADAM_WD * wv)
    tile = pl.BlockSpec((tr, wd), lambda i: (i, 0))
    out = jax.ShapeDtypeStruct((rows, wd), F32)
    if gb is None:
        def body(w_ref, ga_ref, m_ref, v_ref, g_out, d_out, m_out, v_out):
            update(w_ref[...], ga_ref[...], m_ref[...], v_ref[...], g_out, d_out, m_out, v_out)
        return pl.pallas_call(body, name="adamw", grid=(rows // tr,), in_specs=[tile] * 4,
                              out_specs=[tile] * 4, out_shape=[out] * 4)(w, ga, m, v)

    def body(w_ref, ga_ref, gb_ref, m_ref, v_ref, g_out, d_out, m_out, v_out):
        update(w_ref[...], ga_ref[...] + gb_ref[...], m_ref[...], v_ref[...], g_out, d_out, m_out, v_out)
    return pl.pallas_call(body, name="adamw_sum", grid=(rows // tr,), in_specs=[tile] * 5,
                          out_specs=[tile] * 4, out_shape=[out] * 4)(w, ga, gb, m, v)


def _rope_tables(T, R):
    rows = T // GRID_W
    row = jnp.repeat(jnp.arange(rows), GRID_W).astype(F32)
    col = jnp.tile(jnp.arange(GRID_W), rows).astype(F32)
    n_freq = HEAD_DIM // 4
    inv_freq = ROPE_THETA ** (-jnp.arange(n_freq, dtype=F32) / n_freq)
    ang = jnp.concatenate([row[:, None] * inv_freq, col[:, None] * inv_freq], axis=-1)
    cos, sin = jnp.cos(ang), jnp.sin(ang)
    cs = jnp.tile(cos, (1, 4))
    sn = jnp.tile(jnp.concatenate([-sin, sin], axis=-1), (1, 2))
    pad = R - T
    return (jnp.concatenate([cs, jnp.ones((pad, 128), F32)], axis=0),
            jnp.concatenate([sn, jnp.zeros((pad, 128), F32)], axis=0))


def _pack(parts, mult=8 * 128):
    flat = jnp.concatenate([p.reshape(-1).astype(F32) for p in parts])
    pad = (-flat.shape[0]) % mult
    return jnp.pad(flat, (0, pad)).reshape(-1, 128)


def _unpack(buf, shapes):
    flat = buf.reshape(-1)
    out, o = [], 0
    for s in shapes:
        n = 1
        for d in s:
            n *= d
        out.append(flat[o:o + n].reshape(s))
        o += n
    return out


def kernel(x, c, ctx, c_ctx, w_mod, b_mod, norm_mix, norm_ffn, w_in_ab, conv_a, conv_b, conv_b_bias, ln_b_gain, ln_b_bias, w_out_ab, w_qkv, w_o, sinks, w_up, w_conv_ffn, w_down, final_norm, loss_target, m_c_ctx, m_w_mod, m_b_mod, m_norm_mix, m_norm_ffn, m_w_in_ab, m_conv_a, m_conv_b, m_conv_b_bias, m_ln_b_gain, m_ln_b_bias, m_w_out_ab, m_w_qkv, m_w_o, m_sinks, m_w_up, m_w_conv_ffn, m_w_down, m_final_norm, v_c_ctx, v_w_mod, v_b_mod, v_norm_mix, v_norm_ffn, v_w_in_ab, v_conv_a, v_conv_b, v_conv_b_bias, v_ln_b_gain, v_ln_b_bias, v_w_out_ab, v_w_qkv, v_w_o, v_sinks, v_w_up, v_w_conv_ffn, v_w_down, v_final_norm):
    T, dm = x.shape[1], x.shape[2]
    tc = ctx.shape[1]
    R = T + tc
    depth = w_mod.shape[0]
    ax, ay, ac = lax.axis_index("x"), lax.axis_index("y"), lax.axis_index("c")
    chip = 2 * ax + ay
    dev = 4 * ax + 2 * ay + ac

    small_w = [conv_a, conv_b, w_conv_ffn]
    gathered = allgather8(_pack([c] + small_w)).reshape(N_DEV, -1)
    cond8 = gathered[:, :dm]
    off = dm
    full_small = []
    for wsh in small_w:
        n = wsh.size
        per_chip = gathered[0::2, off:off + n].reshape((N_CHIP,) + wsh.shape)
        full_small.append(jnp.concatenate([per_chip[q] for q in range(N_CHIP)], axis=-1))
        off += n
    conv_a_f, conv_b_f, w_conv_ffn_f = full_small
    cond = jnp.concatenate([cond8, c_ctx[None, :], jnp.zeros((7, dm), F32)], axis=0)

    ns_mod = w_mod.shape[2]
    b_mod_sh = lax.dynamic_slice_in_dim(b_mod, chip * ns_mod, ns_mod, axis=1)[:, None, :]
    mod_sh = adaln_fwd(cond, w_mod, b_mod_sh)
    mod_all = allgather8(mod_sh.reshape(depth * 16, ns_mod)).reshape(N_DEV, depth, 16, ns_mod)
    mod_full = jnp.concatenate([mod_all[2 * q] for q in range(N_CHIP)], axis=-1)
    mine = lax.dynamic_index_in_dim(mod_full, dev, axis=1, keepdims=False)
    mods = jnp.stack([mine, mod_full[:, 8]], axis=1).reshape(depth, 2, 6, dm)

    masters = {"w_in_ab": w_in_ab, "w_out_ab": w_out_ab, "w_qkv": w_qkv, "w_o": w_o, "w_up": w_up, "w_down": w_down}
    chip_id = chip.astype(jnp.int32).reshape(1)

    def half_weights(l, half):
        if half == 1:
            return [("w_up", l), ("w_down", l)]
        return [("w_in_ab", l // 2), ("w_out_ab", l // 2)] if l % 2 == 0 else [("w_qkv", l // 2), ("w_o", l // 2)]
    in_flight, after = {}, mods
    for l in range(depth):
        for half in range(2):
            lands = [cast_into_slot(masters[n][j], chip_id) for n, j in half_weights(l, half)]
            send_sems, recv_sems, _, lands, after = exchange_start(f"gather_start_{l}_{half}", "gather", [], lands, after)
            in_flight[l, half] = (send_sems, recv_sems, lands)
    mods = mods + after[0, 0]

    def gathered_weights(l, half, after):
        send_sems, recv_sems, lands = in_flight[l, half]
        _, landed = exchange_wait(f"gather_wait_{l}_{half}", "gather", send_sems, recv_sems, [], lands, after)
        return dict(zip([n for n, _ in half_weights(l, half)], landed))

    cs, sn = _rope_tables(T, R)
    bias = window_bias(T, R)
    sinks_flat = sinks.reshape(-1)

    xs = jnp.concatenate([x[0], ctx[0]], axis=0)
    saved, W = [], []
    h1 = norm_mod_fwd(xs, norm_mix[0][None], mods[0], 0, T)
    for l in range(depth):
        e = l // 2
        wl = gathered_weights(l, 0, h1)
        W.append(wl)
        s = {"x0": xs, "h1": h1}
        if l % 2 == 0:
            p = mm_nn(h1, wl["w_in_ab"], BF16)
            yab = convmix_fwd(p, conv_a_f[e], conv_b_f[e], conv_b_bias[e][None], ln_b_gain[e][None],
                              ln_b_bias[e][None], T)
            s.update(p=p, mix=yab)
        else:
            qkv = mm_nn(h1, wl["w_qkv"], F32)
            qkvr = rope_fwd(qkv, cs, sn)
            att = attn_fwd(qkvr, sinks_flat[e * N_HEADS:(e + 1) * N_HEADS], bias, T)
            s.update(qkvr=qkvr, mix=att)
        y1, x1, h2 = mm_resid_norm_fwd(s["mix"], wl["w_out_ab" if l % 2 == 0 else "w_o"], xs, norm_ffn[l][None],
                                       mods[l], mods[l], 2, 3, T)
        wl.update(gathered_weights(l, 1, h2))
        u = mm_nn(h2, wl["w_up"], BF16)
        if l + 1 < depth:
            z, y2, xs, h1 = ffn_mid_fwd(u, w_conv_ffn_f[l], wl["w_down"], x1, norm_mix[l + 1][None], mods[l],
                                        mods[l + 1], 5, 0, T)
        else:
            z, y2, xs = ffn_mid_fwd(u, w_conv_ffn_f[l], wl["w_down"], x1, None, mods[l], None, 5, 0, T)
        s.update(y1=y1, x1=x1, h2=h2, u=u, z=z, y2=y2)
        saved.append(s)

    loss_part, dx, d_final = loss_head(xs, final_norm[None], loss_target[0], T)
    loss = lax.psum(loss_part[0, 0], ("x", "y", "c"))

    d_mods, d_norm_mix, d_norm_ffn = [None] * depth, [None] * depth, [None] * depth
    d_conv_a, d_conv_b, d_vecs, d_sinks, d_wc = [None] * 2, [None] * 2, [None] * 2, [None] * 2, [None] * depth
    dss1, dss2, dg1, dg2 = [None] * depth, [None] * depth, [None] * depth, [None] * depth
    scattering = {}

    def scatter(l, half, G, after):
        grads_h = [G[n] for n, _ in half_weights(l, half)]
        lands = [lax.empty((N_CHIP - 1, *g.shape[1:]), g.dtype) for g in grads_h]
        send_sems, recv_sems, grads_h, lands, token = exchange_start(
            f"scatter_start_{l}_{half}", "scatter", grads_h, lands, after)
        scattering[l, half] = (send_sems, recv_sems, grads_h, lands)
        return token

    dy2, dg2[depth - 1] = resid_bwd(dx, saved[depth - 1]["y2"], mods[depth - 1], 5, T)
    for l in reversed(range(depth)):
        e = l // 2
        s, wl = saved[l], W[l]
        G = {}
        G["w_down"] = mm_tn(s["z"], dy2, "row", wl["w_down"])
        duc, d_wc[l] = ffn_mid_bwd(dy2, wl["w_down"], s["u"], w_conv_ffn_f[l], T)
        du, dx, dy1, dss2[l], d_norm_ffn[l], dg1[l] = ffn_up_bwd(
            duc, w_conv_ffn_f[l], wl["w_up"], s["x1"], norm_ffn[l][None], mods[l], dx, s["y1"], mods[l], 3, 2, T)
        G["w_up"] = mm_tn(s["h2"], du, "col", wl["w_up"])
        started = scatter(l, 1, G, du)[0, 0]
        if l % 2 == 0:
            G["w_out_ab"] = mm_tn(s["mix"], dy1, "row", wl["w_out_ab"])
            dyab = mm_nt(dy1, wl["w_out_ab"], F32)
            dmid, d_conv_a[e], d_conv_b[e], d_vecs[e] = convmix_bwd1(
                dyab, s["p"], conv_a_f[e] + started, conv_b_f[e], conv_b_bias[e][None], ln_b_gain[e][None],
                ln_b_bias[e][None], T)
            dp = convmix_bwd2(dmid, s["p"], conv_a_f[e], conv_b_f[e], T)
            G["w_in_ab"] = mm_tn(s["h1"], dp, "col", wl["w_in_ab"])
            dcol, wcol = dp, wl["w_in_ab"]
        else:
            G["w_o"] = mm_tn(s["mix"], dy1, "row", wl["w_o"])
            datt = mm_nt(dy1, wl["w_o"], BF16)
            dq, dks, dvs, dkc, dvc, d_sinks[e] = attn_bwd(
                s["qkvr"], datt, sinks_flat[e * N_HEADS:(e + 1) * N_HEADS] + started, bias, T)
            dqkv = rope_bwd(dq, dks, dvs, dkc, dvc, cs, sn, T)
            G["w_qkv"] = mm_tn(s["h1"], dqkv, "col", wl["w_qkv"])
            dcol, wcol = dqkv, wl["w_qkv"]
        token = scatter(l, 0, G, dcol)
        mod_l = mods[l] + token[0, 0]
        if l > 0:
            dx, dy2, dss1[l], d_norm_mix[l], dg2[l - 1] = mm_norm_resid_bwd(
                dcol, wcol, s["x0"], norm_mix[l][None], mod_l, dx, saved[l - 1]["y2"], mods[l - 1], 0, 5, T)
        else:
            dx, dss1[l], d_norm_mix[l] = mm_norm_resid_bwd(
                dcol, wcol, s["x0"], norm_mix[l][None], mod_l, dx, None, None, 0, 0, T)
    grad_x = dx[:T][None]
    for l in range(depth):
        a1, a2 = dss1[l].sum(2), dss2[l].sum(2)
        d_mods[l] = jnp.stack([a1[:, 0], a1[:, 1], dg1[l].sum(1), a2[:, 0], a2[:, 1], dg2[l].sum(1)], axis=1)

    d_mods = jnp.stack(d_mods)
    summed_parts = [
        d_mods[:, 1],
        jnp.stack(d_norm_mix).sum(1), jnp.stack(d_norm_ffn).sum(1),
        jnp.stack(d_conv_a).sum(2), jnp.stack(d_conv_b).sum(2),
        jnp.stack(d_vecs).sum(2),
        jnp.stack(d_sinks)[:, 0, :N_HEADS],
        jnp.stack(d_wc).sum(2), d_final.sum(0)]
    summed_shapes = [p.shape for p in summed_parts]
    n_own = depth * 6 * dm
    pack = _pack([d_mods[:, 0]] + summed_parts)
    parts = allgather8(pack).reshape(N_DEV, -1, 128)
    total = sum_slots(parts)
    own_rows = parts.reshape(N_DEV, -1)[:, :n_own].reshape(N_DEV, depth, 6 * dm)
    (dmod_ctx, g_norm_mix, g_norm_ffn, g_conv_a, g_conv_b, g_vecs, g_sinks, g_wc, g_final) = _unpack(
        total.reshape(-1)[n_own:], summed_shapes)
    dmod_rows = jnp.concatenate([jnp.moveaxis(own_rows, 0, 1), dmod_ctx.reshape(depth, 1, 6 * dm),
                                 jnp.zeros((depth, 7, 6 * dm), F32)], axis=1)
    g_b_mod = dmod_rows.sum(1)
    dmod_sh = lax.dynamic_slice_in_dim(dmod_rows, chip * ns_mod, ns_mod, axis=2)
    g_w_mod, dsilu = adaln_bwd(cond, dmod_sh, w_mod)
    dsilu_all = allgather8(dsilu[8:16]).reshape(N_DEV, 8, dm)
    dsilu_ctx = sum_slots(dsilu_all[0::2])[0]
    sg = jax.nn.sigmoid(c_ctx)
    g_c_ctx = dsilu_ctx * (sg * (1.0 + c_ctx * (1.0 - sg)))

    def shard_cols(full, width):
        return lax.dynamic_slice_in_dim(full, chip * width, width, axis=full.ndim - 1)
    g_conv_a_s = shard_cols(g_conv_a, conv_a.shape[-1])
    g_conv_b_s = shard_cols(g_conv_b, conv_b.shape[-1])
    g_wc_s = shard_cols(g_wc, w_conv_ffn.shape[-1])

    grads, deltas, new_m, new_v = {}, {}, {}, {}

    def step_2d(name, wv, ga, gb, mv, vv):
        shp = wv.shape
        r2 = lambda t: t.reshape(-1, shp[-1])
        g, d, mn, vn = adamw(r2(wv), r2(ga), None if gb is None else r2(gb), r2(mv), r2(vv))
        grads[name], deltas[name], new_m[name], new_v[name] = (t.reshape(shp) for t in (g, d, mn, vn))

    sums = {n: None for n in masters}
    for l in reversed(range(depth)):
        for half in (1, 0):
            send_sems, recv_sems, grads_h, lands = scattering[l, half]
            grads_h, landed = exchange_wait(f"scatter_wait_{l}_{half}", "scatter", send_sems, recv_sems, grads_h,
                                            lands, token)
            for (n, j), own, arr in zip(half_weights(l, half), grads_h, landed):
                sums[n] = sum_landed(arr, own, chip_id, j, masters[n].shape[0], sums[n])
    moments = {"w_in_ab": (m_w_in_ab, v_w_in_ab), "w_out_ab": (m_w_out_ab, v_w_out_ab),
               "w_qkv": (m_w_qkv, v_w_qkv), "w_o": (m_w_o, v_w_o), "w_up": (m_w_up, v_w_up),
               "w_down": (m_w_down, v_w_down)}
    for name, wv in masters.items():
        other = swap_with_sibling(sums[name])
        step_2d(name, wv, sums[name].reshape(wv.shape), other.reshape(wv.shape), *moments[name])
    step_2d("w_mod", w_mod, g_w_mod, None, m_w_mod, v_w_mod)

    small = [("c_ctx", c_ctx, g_c_ctx, m_c_ctx, v_c_ctx), ("b_mod", b_mod, g_b_mod, m_b_mod, v_b_mod),
             ("norm_mix", norm_mix, g_norm_mix, m_norm_mix, v_norm_mix),
             ("norm_ffn", norm_ffn, g_norm_ffn, m_norm_ffn, v_norm_ffn),
             ("conv_a", conv_a, g_conv_a_s, m_conv_a, v_conv_a), ("conv_b", conv_b, g_conv_b_s, m_conv_b, v_conv_b),
             ("conv_b_bias", conv_b_bias, g_vecs[:, 0], m_conv_b_bias, v_conv_b_bias),
             ("ln_b_gain", ln_b_gain, g_vecs[:, 1], m_ln_b_gain, v_ln_b_gain),
             ("ln_b_bias", ln_b_bias, g_vecs[:, 2], m_ln_b_bias, v_ln_b_bias),
             ("sinks", sinks, g_sinks, m_sinks, v_sinks),
             ("w_conv_ffn", w_conv_ffn, g_wc_s, m_w_conv_ffn, v_w_conv_ffn),
             ("final_norm", final_norm, g_final, m_final_norm, v_final_norm)]
    shapes = [t[1].shape for t in small]
    packed = [_pack([t[k] for t in small]) for k in (1, 2, 3, 4)]
    n_real = sum(t[1].size for t in small)
    lane_id = jnp.arange(packed[3].size).reshape(packed[3].shape)
    packed[3] = jnp.where(lane_id < n_real, packed[3], 1.0)
    outs = adamw(packed[0], packed[1], None, packed[2], packed[3])
    for (name, *_), g, d, mn, vn in zip(small, *[_unpack(o, shapes) for o in outs]):
        grads[name], deltas[name], new_m[name], new_v[name] = g, d, mn, vn

    order = ["c_ctx", "w_mod", "b_mod", "norm_mix", "norm_ffn", "w_in_ab", "conv_a", "conv_b", "conv_b_bias",
             "ln_b_gain", "ln_b_bias", "w_out_ab", "w_qkv", "w_o", "sinks", "w_up", "w_conv_ffn", "w_down",
             "final_norm"]
    return (loss, grad_x, *[grads[n] for n in order], *[deltas[n] for n in order],
            *[new_m[n] for n in order], *[new_v[n] for n in order])
```

```python
import jax
import jax.numpy as jnp
from jax import lax
from jax.experimental import pallas as pl
from jax.experimental.pallas import tpu as pltpu

F32 = jnp.float32
BF16 = jnp.bfloat16
MESH = pl.DeviceIdType.MESH

EPS = 1e-6
NEG_INF = -1e30
GRID_W = 64
HEAD_DIM = 64
N_HEADS = 16
WINDOW = 128
QB = 128
ROPE_THETA = 10000.0
A_W = 512
B_CONV = 31
D_FF = 2816
ADAM_LR, ADAM_B1, ADAM_B2, ADAM_EPS, ADAM_WD, ADAM_STEP = 0.001, 0.9, 0.999, 1e-8, 0.01, 10

TMR = 256
HALO = 16
N_DEV = 8
N_CHIP = 4


def _params(vmem_mb=None):
    if vmem_mb is None:
        return pltpu.CompilerParams()
    return pltpu.CompilerParams(vmem_limit_bytes=vmem_mb * 1024 * 1024)


def _row_tile(rows, cap=768):
    for t in (2816, 1408, 768, 704, 512, 384, 256, 128, 64, 32, 16, 8):
        if t <= cap and rows % t == 0:
            return t
    raise ValueError(rows)


def _colsum8(v):
    r, c = v.shape
    return v.reshape(r // 8, 8, c).sum(axis=0)


def _sigmoid(v):
    return 0.5 * jnp.tanh(0.5 * v) + 0.5


def mm_nn(a, w, out_dtype):
    R = a.shape[0]
    _, kb, nb = w.shape
    tm = _row_tile(R)

    def body(a_ref, w_ref, o_ref):
        av = a_ref[...].astype(BF16)
        for q in range(N_CHIP):
            o_ref[:, q * nb:(q + 1) * nb] = jnp.dot(av, w_ref[q], preferred_element_type=F32).astype(o_ref.dtype)
    return pl.pallas_call(
        body, name="mm_nn_col", grid=(R // tm,),
        in_specs=[pl.BlockSpec((tm, kb), lambda i: (i, 0)),
                  pl.BlockSpec((N_CHIP, kb, nb), lambda i: (0, 0, 0), pipeline_mode=pl.Buffered(1))],
        out_specs=pl.BlockSpec((tm, N_CHIP * nb), lambda i: (i, 0)),
        out_shape=jax.ShapeDtypeStruct((R, N_CHIP * nb), out_dtype),
        compiler_params=_params(48))(a, w)


def mm_nt(d, w, out_dtype):
    R = d.shape[0]
    _, kb, nb = w.shape
    tm = _row_tile(R)
    contract_last = (((1,), (1,)), ((), ()))
    resident = pl.BlockSpec((N_CHIP, kb, nb), lambda i: (0, 0, 0), pipeline_mode=pl.Buffered(1))

    def body(d_ref, w_ref, o_ref):
        wv = w_ref[...].reshape(N_CHIP * kb, nb)
        o_ref[...] = lax.dot_general(d_ref[...].astype(BF16), wv, contract_last,
                                     preferred_element_type=F32).astype(o_ref.dtype)
    return pl.pallas_call(
        body, name="mm_nt_row", grid=(R // tm,),
        in_specs=[pl.BlockSpec((tm, nb), lambda i: (i, 0)), resident],
        out_specs=pl.BlockSpec((tm, N_CHIP * kb), lambda i: (i, 0)),
        out_shape=jax.ShapeDtypeStruct((R, N_CHIP * kb), out_dtype),
        compiler_params=_params(48))(d, w)


def mm_tn(a, d, kind, like):
    R = a.shape[0]
    _, kb, nb = like.shape
    tm = _row_tile(R, 1408 if kind == "col" else 768)
    nsteps = R // tm
    contract_rows = (((0,), (0,)), ((), ()))
    out_shape = jax.ShapeDtypeStruct(like.shape, BF16)

    def accumulate(a_ref, d_ref, acc_ref):
        @pl.when(pl.program_id(1) == 0)
        def _():
            acc_ref[...] = jnp.zeros_like(acc_ref)
        acc_ref[...] += lax.dot_general(a_ref[...].astype(BF16), d_ref[...].astype(BF16), contract_rows,
                                        preferred_element_type=F32)
    if kind == "col":
        def body(a_ref, d_ref, o_ref, acc_ref):
            accumulate(a_ref, d_ref, acc_ref)

            @pl.when(pl.program_id(1) == nsteps - 1)
            def _():
                o_ref[...] = acc_ref[...].astype(BF16)
        return pl.pallas_call(
            body, name="mm_tn_col", grid=(N_CHIP, nsteps),
            in_specs=[pl.BlockSpec((tm, kb), lambda q, i: (i, 0)), pl.BlockSpec((tm, nb), lambda q, i: (i, q))],
            out_specs=pl.BlockSpec((None, kb, nb), lambda q, i: (q, 0, 0)), out_shape=out_shape,
            scratch_shapes=[pltpu.VMEM((kb, nb), F32)], compiler_params=_params(48))(a, d)
    tn = 512

    def body(a_ref, d_ref, o_ref, acc_ref):
        accumulate(a_ref, d_ref, acc_ref)

        @pl.when(pl.program_id(1) == nsteps - 1)
        def _():
            o_ref[...] = acc_ref[...].astype(BF16).reshape(N_CHIP, kb, tn)
    return pl.pallas_call(
        body, name="mm_tn_row", grid=(nb // tn, nsteps),
        in_specs=[pl.BlockSpec((tm, N_CHIP * kb), lambda n, i: (i, 0)), pl.BlockSpec((tm, tn), lambda n, i: (i, n))],
        out_specs=pl.BlockSpec((N_CHIP, kb, tn), lambda n, i: (0, 0, n)), out_shape=out_shape,
        scratch_shapes=[pltpu.VMEM((N_CHIP * kb, tn), F32)], compiler_params=_params(48))(a, d)


def _seg(i, T):
    return (i >= T // TMR).astype(jnp.int32)


def norm_mod_fwd(x, nw, mod, k, T):
    R, dm = x.shape

    def body(x_ref, nw_ref, mod_ref, h_ref):
        seg = _seg(pl.program_id(0), T)
        sh = mod_ref[seg, pl.ds(k, 1), :]
        sc = mod_ref[seg, pl.ds(k + 1, 1), :]
        xv = x_ref[...]
        r = lax.rsqrt(jnp.mean(xv * xv, axis=-1, keepdims=True) + EPS)
        h_ref[...] = ((xv * r * nw_ref[...]) * (1.0 + sc) + sh).astype(BF16)
    return pl.pallas_call(
        body, name="norm_mod_fwd", grid=(R // TMR,),
        in_specs=[pl.BlockSpec((TMR, dm), lambda i: (i, 0)),
                  pl.BlockSpec((1, dm), lambda i: (0, 0)),
                  pl.BlockSpec((2, 6, dm), lambda i: (0, 0, 0))],
        out_specs=pl.BlockSpec((TMR, dm), lambda i: (i, 0)),
        out_shape=jax.ShapeDtypeStruct((R, dm), BF16))(x, nw, mod)


def mm_resid_norm_fwd(a, w, x, nw, mod_g, mod_n, kg, kn, T):
    R, dm = x.shape
    _, kb, nb = w.shape

    def body(a_ref, w_ref, x_ref, nw_ref, mg_ref, mn_ref, y_ref, xo_ref, h_ref):
        seg = _seg(pl.program_id(0), T)
        yv = jnp.dot(a_ref[...].astype(BF16), w_ref[...].reshape(N_CHIP * kb, nb), preferred_element_type=F32)
        y_ref[...] = yv
        xv = x_ref[...] + mg_ref[seg, pl.ds(kg, 1), :] * yv
        xo_ref[...] = xv
        r = lax.rsqrt(jnp.mean(xv * xv, axis=-1, keepdims=True) + EPS)
        h_ref[...] = ((xv * r * nw_ref[...]) * (1.0 + mn_ref[seg, pl.ds(kn + 1, 1), :])
                      + mn_ref[seg, pl.ds(kn, 1), :]).astype(BF16)
    tile = pl.BlockSpec((TMR, dm), lambda i: (i, 0))
    modspec = pl.BlockSpec((2, 6, dm), lambda i: (0, 0, 0))
    return pl.pallas_call(
        body, name="mm_resid_norm_fwd", grid=(R // TMR,),
        in_specs=[pl.BlockSpec((TMR, N_CHIP * kb), lambda i: (i, 0)),
                  pl.BlockSpec((N_CHIP, kb, nb), lambda i: (0, 0, 0), pipeline_mode=pl.Buffered(1)),
                  tile, pl.BlockSpec((1, dm), lambda i: (0, 0)), modspec, modspec],
        out_specs=[tile, tile, tile],
        out_shape=[jax.ShapeDtypeStruct((R, dm), F32), jax.ShapeDtypeStruct((R, dm), F32),
                   jax.ShapeDtypeStruct((R, dm), BF16)],
        compiler_params=_params(48))(a, w, x, nw, mod_g, mod_n)


def mm_norm_resid_bwd(d, w, x, nw, mod_n, dxr, y, mod_g, kn, kg, T):
    R, dm = x.shape
    _, kb, nb = w.shape
    with_resid = y is not None
    contract_last = (((1,), (1,)), ((), ()))

    def body(*refs):
        if with_resid:
            d_ref, w_ref, x_ref, nw_ref, mn_ref, dxr_ref, y_ref, mg_ref, dx_ref, dy_ref, dmod_ref, dnw_ref, dg_ref = refs
        else:
            d_ref, w_ref, x_ref, nw_ref, mn_ref, dxr_ref, dx_ref, dmod_ref, dnw_ref = refs
        i = pl.program_id(0)
        seg = _seg(i, T)

        @pl.when(i == 0)
        def _():
            dmod_ref[...] = jnp.zeros_like(dmod_ref)
            dnw_ref[...] = jnp.zeros_like(dnw_ref)
            if with_resid:
                dg_ref[...] = jnp.zeros_like(dg_ref)
        dhv = None
        for q in range(N_CHIP):
            t = lax.dot_general(d_ref[:, q * nb:(q + 1) * nb].astype(BF16), w_ref[q], contract_last,
                                preferred_element_type=F32)
            dhv = t if dhv is None else dhv + t
        sc = mn_ref[seg, pl.ds(kn + 1, 1), :]
        nwv = nw_ref[...]
        xv = x_ref[...]
        r = lax.rsqrt(jnp.mean(xv * xv, axis=-1, keepdims=True) + EPS)
        xh = xv * r
        dmod_ref[seg, 0] += _colsum8(dhv)
        dmod_ref[seg, 1] += _colsum8(dhv * (xh * nwv))
        dn = dhv * (1.0 + sc)
        dnw_ref[...] += _colsum8(dn * xh)
        dxh = dn * nwv
        dx = dxr_ref[...] + r * (dxh - xh * jnp.mean(dxh * xh, axis=-1, keepdims=True))
        dx_ref[...] = dx
        if with_resid:
            dy_ref[...] = (mg_ref[seg, pl.ds(kg, 1), :] * dx).astype(BF16)
            dg_ref[seg] += _colsum8(dx * y_ref[...])
    tile = pl.BlockSpec((TMR, dm), lambda i: (i, 0))
    modspec = pl.BlockSpec((2, 6, dm), lambda i: (0, 0, 0))
    in_specs = [pl.BlockSpec((TMR, N_CHIP * nb), lambda i: (i, 0)),
                pl.BlockSpec((N_CHIP, kb, nb), lambda i: (0, 0, 0), pipeline_mode=pl.Buffered(1)),
                tile, pl.BlockSpec((1, dm), lambda i: (0, 0)), modspec, tile]
    acc_specs = [pl.BlockSpec((2, 2, 8, dm), lambda i: (0, 0, 0, 0)), pl.BlockSpec((8, dm), lambda i: (0, 0))]
    acc_shapes = [jax.ShapeDtypeStruct((2, 2, 8, dm), F32), jax.ShapeDtypeStruct((8, dm), F32)]
    if with_resid:
        return pl.pallas_call(
            body, name="mm_norm_resid_bwd", grid=(R // TMR,),
            in_specs=in_specs + [tile, modspec],
            out_specs=[tile, tile] + acc_specs + [pl.BlockSpec((2, 8, dm), lambda i: (0, 0, 0))],
            out_shape=[jax.ShapeDtypeStruct((R, dm), F32), jax.ShapeDtypeStruct((R, dm), BF16)] + acc_shapes
            + [jax.ShapeDtypeStruct((2, 8, dm), F32)],
            compiler_params=_params(48))(d, w, x, nw, mod_n, dxr, y, mod_g)
    return pl.pallas_call(
        body, name="mm_norm_bwd", grid=(R // TMR,), in_specs=in_specs,
        out_specs=[tile] + acc_specs, out_shape=[jax.ShapeDtypeStruct((R, dm), F32)] + acc_shapes,
        compiler_params=_params(48))(d, w, x, nw, mod_n, dxr)


def resid_bwd(dxn, y, mod, k, T):
    R, dm = dxn.shape

    def body(dx_ref, y_ref, mod_ref, dy_ref, dg_ref):
        i = pl.program_id(0)
        seg = _seg(i, T)

        @pl.when(i == 0)
        def _():
            dg_ref[...] = jnp.zeros_like(dg_ref)
        dxv = dx_ref[...]
        dy_ref[...] = (mod_ref[seg, pl.ds(k, 1), :] * dxv).astype(BF16)
        dg_ref[seg] += _colsum8(dxv * y_ref[...])
    tile = pl.BlockSpec((TMR, dm), lambda i: (i, 0))
    return pl.pallas_call(
        body, name="resid_bwd", grid=(R // TMR,),
        in_specs=[tile, tile, pl.BlockSpec((2, 6, dm), lambda i: (0, 0, 0))],
        out_specs=[tile, pl.BlockSpec((2, 8, dm), lambda i: (0, 0, 0))],
        out_shape=[jax.ShapeDtypeStruct((R, dm), BF16), jax.ShapeDtypeStruct((2, 8, dm), F32)])(dxn, y, mod)


def _halo_specs(width, R):
    nblk = R // HALO
    per = TMR // HALO
    return (pl.BlockSpec((HALO, width), lambda i: (jnp.maximum(i * per - 1, 0), 0)),
            pl.BlockSpec((TMR, width), lambda i: (i, 0)),
            pl.BlockSpec((HALO, width), lambda i: (jnp.minimum((i + 1) * per, nblk - 1), 0)))


def _halo_live(i, T, R):
    nl = T // TMR
    return (i != 0) & (i != nl), (i != nl - 1) & (i != R // TMR - 1)


def _ext(refs, c0, cw, live, halo=HALO):
    pref, ref, nref = refs
    before = jnp.where(live[0], pref[:, c0:c0 + cw].astype(F32)[HALO - halo:], 0.0)
    after = jnp.where(live[1], nref[:, c0:c0 + cw].astype(F32)[:halo], 0.0)
    return jnp.concatenate([before, ref[:, c0:c0 + cw].astype(F32), after], axis=0)


def _at(ext, off, halo=HALO):
    n = ext.shape[0]
    s = (-off) % n
    y = pltpu.roll(ext, s, 0) if s else ext
    return y[halo:halo + TMR]


def ffn_mid_fwd(u, wc, w_down, x, nw, mod_g, mod_n, kg, kn, T):
    R, w2 = u.shape
    dm = x.shape[1]
    _, kb, nb = w_down.shape
    cw = 256
    with_norm = nw is not None

    def body(*refs):
        if with_norm:
            up_ref, u_ref, un_ref, wc_ref, w_ref, x_ref, nw_ref, mg_ref, mn_ref, z_ref, y_ref, xo_ref, h_ref = refs
        else:
            up_ref, u_ref, un_ref, wc_ref, w_ref, x_ref, mg_ref, z_ref, y_ref, xo_ref = refs
        i = pl.program_id(0)
        seg = _seg(i, T)
        live = _halo_live(i, T, R)

        def conv(c0):
            e = _ext((up_ref, u_ref, un_ref), c0, cw, live, 8)
            return (wc_ref[pl.ds(0, 1), c0:c0 + cw] * _at(e, -1, 8) + wc_ref[pl.ds(1, 1), c0:c0 + cw] * _at(e, 0, 8)
                    + wc_ref[pl.ds(2, 1), c0:c0 + cw] * _at(e, 1, 8))
        yv = None
        for j in range(D_FF // cw):
            a = conv(j * cw)
            g = conv(D_FF + j * cw)
            zc = (g * _sigmoid(g) * a).astype(BF16)
            z_ref[:, j * cw:(j + 1) * cw] = zc
            t = jnp.dot(zc, w_ref[j * cw:(j + 1) * cw, :], preferred_element_type=F32)
            yv = t if yv is None else yv + t
        y_ref[...] = yv
        xv = x_ref[...] + mg_ref[seg, pl.ds(kg, 1), :] * yv
        xo_ref[...] = xv
        if with_norm:
            r = lax.rsqrt(jnp.mean(xv * xv, axis=-1, keepdims=True) + EPS)
            h_ref[...] = ((xv * r * nw_ref[...]) * (1.0 + mn_ref[seg, pl.ds(kn + 1, 1), :])
                          + mn_ref[seg, pl.ds(kn, 1), :]).astype(BF16)
    tile = pl.BlockSpec((TMR, dm), lambda i: (i, 0))
    modspec = pl.BlockSpec((2, 6, dm), lambda i: (0, 0, 0))
    w_down = w_down.reshape(N_CHIP * kb, nb)
    in_specs = [*_halo_specs(w2, R), pl.BlockSpec((3, w2), lambda i: (0, 0)),
                pl.BlockSpec((N_CHIP * kb, nb), lambda i: (0, 0), pipeline_mode=pl.Buffered(1)), tile]
    out_specs = [pl.BlockSpec((TMR, D_FF), lambda i: (i, 0)), tile, tile]
    out_shape = [jax.ShapeDtypeStruct((R, D_FF), BF16), jax.ShapeDtypeStruct((R, dm), F32),
                 jax.ShapeDtypeStruct((R, dm), F32)]
    if with_norm:
        return pl.pallas_call(
            body, name="ffn_mid_fwd", grid=(R // TMR,),
            in_specs=in_specs + [pl.BlockSpec((1, dm), lambda i: (0, 0)), modspec, modspec],
            out_specs=out_specs + [tile], out_shape=out_shape + [jax.ShapeDtypeStruct((R, dm), BF16)],
            compiler_params=_params(48))(u, u, u, wc, w_down, x, nw, mod_g, mod_n)
    return pl.pallas_call(
        body, name="ffn_mid_fwd_last", grid=(R // TMR,), in_specs=in_specs + [modspec],
        out_specs=out_specs, out_shape=out_shape, compiler_params=_params(48))(u, u, u, wc, w_down, x, mod_g)


def ffn_mid_bwd(dz, u, wc, T):
    R, w2 = u.shape
    cw = 256

    def body(dz_ref, up_ref, u_ref, un_ref, wc_ref, duc_ref, dwc_ref):
        i = pl.program_id(0)
        live = _halo_live(i, T, R)

        @pl.when(i == 0)
        def _():
            dwc_ref[...] = jnp.zeros_like(dwc_ref)

        def taps(c0):
            e = _ext((up_ref, u_ref, un_ref), c0, cw, live, 8)
            return [_at(e, -1, 8), _at(e, 0, 8), _at(e, 1, 8)]

        def conv(t, c0):
            return (wc_ref[pl.ds(0, 1), c0:c0 + cw] * t[0] + wc_ref[pl.ds(1, 1), c0:c0 + cw] * t[1]
                    + wc_ref[pl.ds(2, 1), c0:c0 + cw] * t[2])
        for j in range(D_FF // cw):
            ca, cg = j * cw, D_FF + j * cw
            dzv = dz_ref[:, ca:ca + cw].astype(F32)
            ta, tg = taps(ca), taps(cg)
            a, g = conv(ta, ca), conv(tg, cg)
            sg = _sigmoid(g)
            da = dzv * (g * sg)
            dg = dzv * a * (sg * (1.0 + g * (1.0 - sg)))
            duc_ref[:, ca:ca + cw] = da.astype(BF16)
            duc_ref[:, cg:cg + cw] = dg.astype(BF16)
            for k in range(3):
                dwc_ref[k, :, ca:ca + cw] += _colsum8(da * ta[k])
                dwc_ref[k, :, cg:cg + cw] += _colsum8(dg * tg[k])
    return pl.pallas_call(
        body, name="ffn_mid_bwd", grid=(R // TMR,),
        in_specs=[pl.BlockSpec((TMR, D_FF), lambda i: (i, 0)), *_halo_specs(w2, R),
                  pl.BlockSpec((3, w2), lambda i: (0, 0))],
        out_specs=[pl.BlockSpec((TMR, w2), lambda i: (i, 0)), pl.BlockSpec((3, 8, w2), lambda i: (0, 0, 0))],
        out_shape=[jax.ShapeDtypeStruct((R, w2), BF16), jax.ShapeDtypeStruct((3, 8, w2), F32)],
        compiler_params=_params(48))(dz, u, u, u, wc)


def ffn_up_bwd(duc, wc, w_up, x, nw, mod_n, dxr, y, mod_g, kn, kg, T):
    R, w2 = duc.shape
    dm = x.shape[1]
    _, kb, nb = w_up.shape
    cw = 128
    contract_last = (((1,), (1,)), ((), ()))

    def body(dp_ref, d_ref, dn_ref, wc_ref, w_ref, x_ref, nw_ref, mn_ref, dxr_ref, y_ref, mg_ref,
             du_ref, dx_ref, dy_ref, dmod_ref, dnw_ref, dg_ref):
        i = pl.program_id(0)
        seg = _seg(i, T)
        live = _halo_live(i, T, R)

        @pl.when(i == 0)
        def _():
            dmod_ref[...] = jnp.zeros_like(dmod_ref)
            dnw_ref[...] = jnp.zeros_like(dnw_ref)
            dg_ref[...] = jnp.zeros_like(dg_ref)
        dhv = None
        for q in range(N_CHIP):
            for j in range(nb // cw):
                c0 = q * nb + j * cw
                e = _ext((dp_ref, d_ref, dn_ref), c0, cw, live, 8)
                du_ref[:, c0:c0 + cw] = (wc_ref[pl.ds(0, 1), c0:c0 + cw] * _at(e, 1, 8)
                                         + wc_ref[pl.ds(1, 1), c0:c0 + cw] * _at(e, 0, 8)
                                         + wc_ref[pl.ds(2, 1), c0:c0 + cw] * _at(e, -1, 8)).astype(BF16)
            t = lax.dot_general(du_ref[:, q * nb:(q + 1) * nb], w_ref[q], contract_last,
                                preferred_element_type=F32)
            dhv = t if dhv is None else dhv + t
        sc = mn_ref[seg, pl.ds(kn + 1, 1), :]
        nwv = nw_ref[...]
        xv = x_ref[...]
        r = lax.rsqrt(jnp.mean(xv * xv, axis=-1, keepdims=True) + EPS)
        xh = xv * r
        dmod_ref[seg, 0] += _colsum8(dhv)
        dmod_ref[seg, 1] += _colsum8(dhv * (xh * nwv))
        dn = dhv * (1.0 + sc)
        dnw_ref[...] += _colsum8(dn * xh)
        dxh = dn * nwv
        dx = dxr_ref[...] + r * (dxh - xh * jnp.mean(dxh * xh, axis=-1, keepdims=True))
        dx_ref[...] = dx
        dy_ref[...] = (mg_ref[seg, pl.ds(kg, 1), :] * dx).astype(BF16)
        dg_ref[seg] += _colsum8(dx * y_ref[...])
    tile = pl.BlockSpec((TMR, dm), lambda i: (i, 0))
    modspec = pl.BlockSpec((2, 6, dm), lambda i: (0, 0, 0))
    return pl.pallas_call(
        body, name="ffn_up_bwd", grid=(R // TMR,),
        in_specs=[*_halo_specs(w2, R), pl.BlockSpec((3, w2), lambda i: (0, 0)),
                  pl.BlockSpec((N_CHIP, kb, nb), lambda i: (0, 0, 0), pipeline_mode=pl.Buffered(1)),
                  tile, pl.BlockSpec((1, dm), lambda i: (0, 0)), modspec, tile, tile, modspec],
        out_specs=[pl.BlockSpec((TMR, w2), lambda i: (i, 0)), tile, tile,
                   pl.BlockSpec((2, 2, 8, dm), lambda i: (0, 0, 0, 0)), pl.BlockSpec((8, dm), lambda i: (0, 0)),
                   pl.BlockSpec((2, 8, dm), lambda i: (0, 0, 0))],
        out_shape=[jax.ShapeDtypeStruct((R, w2), BF16), jax.ShapeDtypeStruct((R, dm), F32),
                   jax.ShapeDtypeStruct((R, dm), BF16), jax.ShapeDtypeStruct((2, 2, 8, dm), F32),
                   jax.ShapeDtypeStruct((8, dm), F32), jax.ShapeDtypeStruct((2, 8, dm), F32)],
        compiler_params=_params(48))(duc, duc, duc, wc, w_up, x, nw, mod_n, dxr, y, mod_g)


_CW = 128


def _mixer_a(prefs, wa_ref, live):
    cin = _ext(prefs, A_W, A_W, live) * _ext(prefs, 2 * A_W, A_W, live)
    ca = (wa_ref[pl.ds(0, 1), :] * _at(cin, -1) + wa_ref[pl.ds(1, 1), :] * _at(cin, 0)
          + wa_ref[pl.ds(2, 1), :] * _at(cin, 1))
    return cin, ca


def _mixer_b(prefs, wb_ref, bias_ref, live, ub_s, ub2_s):
    for cc in range(A_W // _CW):
        c0 = cc * _CW
        ub = _ext(prefs, 3 * A_W + c0, _CW, live) * _sigmoid(_ext(prefs, 4 * A_W + c0, _CW, live))
        ub_s[:, c0:c0 + _CW] = ub
        acc = jnp.zeros((TMR, _CW), F32) + bias_ref[:, c0:c0 + _CW]
        for k in range(B_CONV):
            acc = acc + wb_ref[pl.ds(k, 1), c0:c0 + _CW] * _at(ub, k - B_CONV // 2)
        ub2_s[:, c0:c0 + _CW] = acc


def _layernorm_stats(v):
    mu = jnp.mean(v, axis=-1, keepdims=True)
    xc = v - mu
    rs = lax.rsqrt(jnp.mean(xc * xc, axis=-1, keepdims=True) + EPS)
    return xc * rs, rs


def mixer_fwd(p, wa, wb, bias, lng, lnb, w_out, x, nw, mod, kg, kn, T):
    R, wp = p.shape
    dm = x.shape[1]
    _, kb, nb = w_out.shape

    def body(pp_ref, p_ref, pn_ref, wa_ref, wb_ref, bias_ref, lng_ref, lnb_ref, w_ref, x_ref, nw_ref, mod_ref,
             o_ref, y_ref, xo_ref, h_ref, ub_s, ub2_s):
        i = pl.program_id(0)
        seg = _seg(i, T)
        live = _halo_live(i, T, R)
        prefs = (pp_ref, p_ref, pn_ref)
        _, ca = _mixer_a(prefs, wa_ref, live)
        ya = (p_ref[:, 0:A_W].astype(F32) * ca).astype(BF16)
        o_ref[:, 0:A_W] = ya
        yv = jnp.dot(ya, w_ref[0:A_W, :], preferred_element_type=F32)
        _mixer_b(prefs, wb_ref, bias_ref, live, ub_s, ub2_s)
        xh, _ = _layernorm_stats(ub2_s[...])
        lv = xh * lng_ref[...] + lnb_ref[...]
        yb = (lv * _sigmoid(lv)).astype(BF16)
        o_ref[:, A_W:2 * A_W] = yb
        yv = yv + jnp.dot(yb, w_ref[A_W:2 * A_W, :], preferred_element_type=F32)
        y_ref[...] = yv
        xv = x_ref[...] + mod_ref[seg, pl.ds(kg, 1), :] * yv
        xo_ref[...] = xv
        r = lax.rsqrt(jnp.mean(xv * xv, axis=-1, keepdims=True) + EPS)
        h_ref[...] = ((xv * r * nw_ref[...]) * (1.0 + mod_ref[seg, pl.ds(kn + 1, 1), :])
                      + mod_ref[seg, pl.ds(kn, 1), :]).astype(BF16)
    vec = pl.BlockSpec((1, A_W), lambda i: (0, 0))
    tile = pl.BlockSpec((TMR, dm), lambda i: (i, 0))
    return pl.pallas_call(
        body, name="mixer_fwd", grid=(R // TMR,),
        in_specs=[*_halo_specs(wp, R), pl.BlockSpec((3, A_W), lambda i: (0, 0)),
                  pl.BlockSpec((B_CONV, A_W), lambda i: (0, 0)), vec, vec, vec,
                  pl.BlockSpec((N_CHIP * kb, nb), lambda i: (0, 0), pipeline_mode=pl.Buffered(1)),
                  tile, pl.BlockSpec((1, dm), lambda i: (0, 0)), pl.BlockSpec((2, 6, dm), lambda i: (0, 0, 0))],
        out_specs=[pl.BlockSpec((TMR, 2 * A_W), lambda i: (i, 0)), tile, tile, tile],
        out_shape=[jax.ShapeDtypeStruct((R, 2 * A_W), BF16), jax.ShapeDtypeStruct((R, dm), F32),
                   jax.ShapeDtypeStruct((R, dm), F32), jax.ShapeDtypeStruct((R, dm), BF16)],
        scratch_shapes=[pltpu.VMEM((TMR + 2 * HALO, A_W), F32), pltpu.VMEM((TMR, A_W), F32)],
        compiler_params=_params(48))(p, p, p, wa, wb, bias, lng, lnb, w_out.reshape(N_CHIP * kb, nb), x, nw, mod)


def convmix_bwd1(dyab, p, wa, wb, bias, lng, lnb, T):
    R, wp = p.shape

    def body(dy_ref, pp_ref, p_ref, pn_ref, wa_ref, wb_ref, bias_ref, lng_ref, lnb_ref,
             dmid_ref, dwa_ref, dwb_ref, dvec_ref, ub_s, ub2_s):
        i = pl.program_id(0)
        live = _halo_live(i, T, R)

        @pl.when(i == 0)
        def _():
            dwa_ref[...] = jnp.zeros_like(dwa_ref)
            dwb_ref[...] = jnp.zeros_like(dwb_ref)
            dvec_ref[...] = jnp.zeros_like(dvec_ref)
        prefs = (pp_ref, p_ref, pn_ref)
        cin, ca = _mixer_a(prefs, wa_ref, live)
        dya = dy_ref[:, 0:A_W]
        dmid_ref[:, 0:A_W] = dya * ca
        dca = dya * p_ref[:, 0:A_W].astype(F32)
        dmid_ref[:, A_W:2 * A_W] = dca
        for k in range(3):
            dwa_ref[k] += _colsum8(dca * _at(cin, k - 1))
        _mixer_b(prefs, wb_ref, bias_ref, live, ub_s, ub2_s)
        xh, rs = _layernorm_stats(ub2_s[...])
        gain = lng_ref[...]
        lv = xh * gain + lnb_ref[...]
        sl = _sigmoid(lv)
        dl = dy_ref[:, A_W:2 * A_W] * (sl * (1.0 + lv * (1.0 - sl)))
        dvec_ref[1] += _colsum8(dl * xh)
        dvec_ref[2] += _colsum8(dl)
        dxh = dl * gain
        dub2 = rs * (dxh - jnp.mean(dxh, axis=-1, keepdims=True)
                     - xh * jnp.mean(dxh * xh, axis=-1, keepdims=True))
        dvec_ref[0] += _colsum8(dub2)
        dmid_ref[:, 2 * A_W:3 * A_W] = dub2
        for cc in range(A_W // _CW):
            c0 = cc * _CW
            ub = ub_s[:, c0:c0 + _CW]
            d = dmid_ref[:, 2 * A_W + c0:2 * A_W + c0 + _CW]
            for k in range(B_CONV):
                dwb_ref[k, :, c0:c0 + _CW] += _colsum8(d * _at(ub, k - B_CONV // 2))
    vec = pl.BlockSpec((1, A_W), lambda i: (0, 0))
    return pl.pallas_call(
        body, name="convmix_bwd1", grid=(R // TMR,),
        in_specs=[pl.BlockSpec((TMR, 2 * A_W), lambda i: (i, 0)), *_halo_specs(wp, R),
                  pl.BlockSpec((3, A_W), lambda i: (0, 0)), pl.BlockSpec((B_CONV, A_W), lambda i: (0, 0)),
                  vec, vec, vec],
        out_specs=[pl.BlockSpec((TMR, 3 * A_W), lambda i: (i, 0)),
                   pl.BlockSpec((3, 8, A_W), lambda i: (0, 0, 0)),
                   pl.BlockSpec((B_CONV, 8, A_W), lambda i: (0, 0, 0)),
                   pl.BlockSpec((3, 8, A_W), lambda i: (0, 0, 0))],
        out_shape=[jax.ShapeDtypeStruct((R, 3 * A_W), F32), jax.ShapeDtypeStruct((3, 8, A_W), F32),
                   jax.ShapeDtypeStruct((B_CONV, 8, A_W), F32), jax.ShapeDtypeStruct((3, 8, A_W), F32)],
        scratch_shapes=[pltpu.VMEM((TMR + 2 * HALO, A_W), F32), pltpu.VMEM((TMR, A_W), F32)],
        compiler_params=_params(48))(dyab, p, p, p, wa, wb, bias, lng, lnb)


def mixer_in_bwd(dmid, p, wa, wb, w_in, x, nw, mod_n, dxr, y, mod_g, kn, kg, T):
    R, wp = p.shape
    dm = x.shape[1]
    _, kb, nb = w_in.shape
    with_resid = y is not None
    contract_last = (((1,), (1,)), ((), ()))

    def body(*refs):
        if with_resid:
            (mp_ref, m_ref, mn_ref, p_ref, wa_ref, wb_ref, w_ref, x_ref, nw_ref, mnorm_ref, dxr_ref, y_ref, mg_ref,
             dp_ref, dx_ref, dy_ref, dmod_ref, dnw_ref, dg_ref) = refs
        else:
            (mp_ref, m_ref, mn_ref, p_ref, wa_ref, wb_ref, w_ref, x_ref, nw_ref, mnorm_ref, dxr_ref,
             dp_ref, dx_ref, dmod_ref, dnw_ref) = refs
        i = pl.program_id(0)
        seg = _seg(i, T)
        live = _halo_live(i, T, R)

        @pl.when(i == 0)
        def _():
            dmod_ref[...] = jnp.zeros_like(dmod_ref)
            dnw_ref[...] = jnp.zeros_like(dnw_ref)
            if with_resid:
                dg_ref[...] = jnp.zeros_like(dg_ref)

        def block(q):
            return lax.dot_general(dp_ref[:, q * nb:(q + 1) * nb], w_ref[q], contract_last,
                                   preferred_element_type=F32)
        mrefs = (mp_ref, m_ref, mn_ref)
        dp_ref[:, 0:A_W] = m_ref[:, 0:A_W].astype(BF16)
        dca = _ext(mrefs, A_W, A_W, live)
        dcin = (wa_ref[pl.ds(0, 1), :] * _at(dca, 1) + wa_ref[pl.ds(1, 1), :] * _at(dca, 0)
                + wa_ref[pl.ds(2, 1), :] * _at(dca, -1))
        dp_ref[:, A_W:2 * A_W] = (dcin * p_ref[:, 2 * A_W:3 * A_W].astype(F32)).astype(BF16)
        dp_ref[:, 2 * A_W:3 * A_W] = (dcin * p_ref[:, A_W:2 * A_W].astype(F32)).astype(BF16)
        dhv = block(0) + block(1)
        for cc in range(A_W // _CW):
            c0 = cc * _CW
            d = _ext(mrefs, 2 * A_W + c0, _CW, live)
            dub = jnp.zeros((TMR, _CW), F32)
            for k in range(B_CONV):
                dub = dub + wb_ref[pl.ds(k, 1), c0:c0 + _CW] * _at(d, B_CONV // 2 - k)
            vb = p_ref[:, 3 * A_W + c0:3 * A_W + c0 + _CW].astype(F32)
            s = _sigmoid(p_ref[:, 4 * A_W + c0:4 * A_W + c0 + _CW].astype(F32))
            dp_ref[:, 3 * A_W + c0:3 * A_W + c0 + _CW] = (dub * s).astype(BF16)
            dp_ref[:, 4 * A_W + c0:4 * A_W + c0 + _CW] = (dub * vb * s * (1.0 - s)).astype(BF16)
        dhv = dhv + block(2) + block(3)
        sc = mnorm_ref[seg, pl.ds(kn + 1, 1), :]
        nwv = nw_ref[...]
        xv = x_ref[...]
        r = lax.rsqrt(jnp.mean(xv * xv, axis=-1, keepdims=True) + EPS)
        xh = xv * r
        dmod_ref[seg, 0] += _colsum8(dhv)
        dmod_ref[seg, 1] += _colsum8(dhv * (xh * nwv))
        dn = dhv * (1.0 + sc)
        dnw_ref[...] += _colsum8(dn * xh)
        dxh = dn * nwv
        dx = dxr_ref[...] + r * (dxh - xh * jnp.mean(dxh * xh, axis=-1, keepdims=True))
        dx_ref[...] = dx
        if with_resid:
            dy_ref[...] = (mg_ref[seg, pl.ds(kg, 1), :] * dx).astype(BF16)
            dg_ref[seg] += _colsum8(dx * y_ref[...])
    assert 2 * nb <= 3 * A_W and N_CHIP * nb == wp
    tile = pl.BlockSpec((TMR, dm), lambda i: (i, 0))
    modspec = pl.BlockSpec((2, 6, dm), lambda i: (0, 0, 0))
    in_specs = [*_halo_specs(3 * A_W, R), pl.BlockSpec((TMR, wp), lambda i: (i, 0)),
                pl.BlockSpec((3, A_W), lambda i: (0, 0)), pl.BlockSpec((B_CONV, A_W), lambda i: (0, 0)),
                pl.BlockSpec((N_CHIP, kb, nb), lambda i: (0, 0, 0), pipeline_mode=pl.Buffered(1)),
                tile, pl.BlockSpec((1, dm), lambda i: (0, 0)), modspec, tile]
    dp_spec = pl.BlockSpec((TMR, wp), lambda i: (i, 0))
    acc_specs = [pl.BlockSpec((2, 2, 8, dm), lambda i: (0, 0, 0, 0)), pl.BlockSpec((8, dm), lambda i: (0, 0))]
    acc_shapes = [jax.ShapeDtypeStruct((2, 2, 8, dm), F32), jax.ShapeDtypeStruct((8, dm), F32)]
    dp_shape, dx_shape = jax.ShapeDtypeStruct((R, wp), BF16), jax.ShapeDtypeStruct((R, dm), F32)
    if with_resid:
        return pl.pallas_call(
            body, name="mixer_in_bwd", grid=(R // TMR,), in_specs=in_specs + [tile, modspec],
            out_specs=[dp_spec, tile, tile] + acc_specs + [pl.BlockSpec((2, 8, dm), lambda i: (0, 0, 0))],
            out_shape=[dp_shape, dx_shape, jax.ShapeDtypeStruct((R, dm), BF16)] + acc_shapes
            + [jax.ShapeDtypeStruct((2, 8, dm), F32)],
            compiler_params=_params(48))(dmid, dmid, dmid, p, wa, wb, w_in, x, nw, mod_n, dxr, y, mod_g)
    return pl.pallas_call(
        body, name="mixer_in_bwd_first", grid=(R // TMR,), in_specs=in_specs,
        out_specs=[dp_spec, tile] + acc_specs, out_shape=[dp_shape, dx_shape] + acc_shapes,
        compiler_params=_params(48))(dmid, dmid, dmid, p, wa, wb, w_in, x, nw, mod_n, dxr)


def _rot_half(v):
    w = v.shape[-1]
    lane = lax.broadcasted_iota(jnp.int32, (1, w), 1)
    return jnp.where(lane % HEAD_DIM < HEAD_DIM // 2, pltpu.roll(v, w - HEAD_DIM // 2, 1),
                     pltpu.roll(v, HEAD_DIM // 2, 1))


def mm_qkv_rope(a, w, cs, sn):
    R = a.shape[0]
    _, kb, nb = w.shape
    wq = N_CHIP * nb
    tm = _row_tile(R)
    qw = N_HEADS * HEAD_DIM
    kw = (wq - qw) // 2
    scale = HEAD_DIM ** -0.5

    def body(a_ref, w_ref, cs_ref, sn_ref, o_ref, x_ref):
        av = a_ref[...].astype(BF16)
        for q in range(N_CHIP):
            x_ref[:, q * nb:(q + 1) * nb] = jnp.dot(av, w_ref[q], preferred_element_type=F32)
        c, s = cs_ref[...], sn_ref[...]
        q = x_ref[:, 0:qw]
        o_ref[:, 0:qw] = ((q * jnp.tile(c, (1, qw // 128)) + _rot_half(q) * jnp.tile(s, (1, qw // 128)))
                          * scale).astype(BF16)
        k = x_ref[:, qw:qw + kw]
        o_ref[:, qw:qw + kw] = (k * jnp.tile(c, (1, kw // 128))
                                + _rot_half(k) * jnp.tile(s, (1, kw // 128))).astype(BF16)
        o_ref[:, qw + kw:] = x_ref[:, qw + kw:].astype(BF16)
    tab = pl.BlockSpec((tm, 128), lambda i: (i, 0))
    return pl.pallas_call(
        body, name="mm_qkv_rope", grid=(R // tm,),
        in_specs=[pl.BlockSpec((tm, kb), lambda i: (i, 0)),
                  pl.BlockSpec((N_CHIP, kb, nb), lambda i: (0, 0, 0), pipeline_mode=pl.Buffered(1)), tab, tab],
        out_specs=pl.BlockSpec((tm, wq), lambda i: (i, 0)),
        out_shape=jax.ShapeDtypeStruct((R, wq), BF16), scratch_shapes=[pltpu.VMEM((tm, wq), F32)],
        compiler_params=_params(48))(a, w, cs, sn)


def rope_bwd(dq, dks, dvs, dkc, dvc, cs, sn, T):
    R, qw = dq.shape
    kw = dkc.shape[1]
    nb = R // QB
    nl = T // QB
    scale = HEAD_DIM ** -0.5

    def body(dq_ref, kp_ref, ko_ref, kn_ref, vp_ref, vo_ref, vn_ref, kc_ref, vc_ref, cs_ref, sn_ref, o_ref):
        b = pl.program_id(0)
        c, s = cs_ref[...], sn_ref[...]
        has_next = (b + 1 < nb).astype(F32)
        has_prev = (b >= 1).astype(F32)
        is_ctx = (b >= nl).astype(F32)
        g = dq_ref[...] * scale
        o_ref[:, 0:qw] = (g * jnp.tile(c, (1, qw // 128)) + _rot_half(g * jnp.tile(s, (1, qw // 128)))).astype(BF16)
        g = ko_ref[...] + kp_ref[...] * has_next + kn_ref[...] * has_prev + kc_ref[...] * is_ctx
        o_ref[:, qw:qw + kw] = (g * jnp.tile(c, (1, kw // 128))
                                + _rot_half(g * jnp.tile(s, (1, kw // 128)))).astype(BF16)
        o_ref[:, qw + kw:] = (vo_ref[...] + vp_ref[...] * has_next + vn_ref[...] * has_prev
                              + vc_ref[...] * is_ctx).astype(BF16)
    own = pl.BlockSpec((QB, kw), lambda b: (b, 0))
    from_next = pl.BlockSpec((QB, kw), lambda b: (jnp.minimum(b + 1, nb - 1), 0))
    from_prev = pl.BlockSpec((QB, kw), lambda b: (jnp.maximum(b - 1, 0), 0))
    ctx = pl.BlockSpec((QB, kw), lambda b: (jnp.maximum(b - nl, 0), 0))
    tab = pl.BlockSpec((QB, 128), lambda b: (b, 0))
    return pl.pallas_call(
        body, name="rope_bwd", grid=(nb,),
        in_specs=[pl.BlockSpec((QB, qw), lambda b: (b, 0)), from_next, own, from_prev, from_next, own, from_prev,
                  ctx, ctx, tab, tab],
        out_specs=pl.BlockSpec((QB, qw + 2 * kw), lambda b: (b, 0)),
        out_shape=jax.ShapeDtypeStruct((R, qw + 2 * kw), BF16))(
            dq, dks[0], dks[1], dks[2], dvs[0], dvs[1], dvs[2], dkc, dvc, cs, sn)


def _attn_specs(T, R):
    nl = T // QB
    qcols = N_HEADS * HEAD_DIM // 128
    kcols = 2

    def band(col0, shift):
        return pl.BlockSpec((QB, 128), lambda jj, b: (jnp.clip(b + shift, 0, nl - 1), col0 + jj))

    def ctx(col0):
        return pl.BlockSpec((R - T, 128), lambda jj, b: (T // (R - T), col0 + jj))
    q = pl.BlockSpec((QB, 512), lambda jj, b: (b, jj))
    k0, v0 = qcols, qcols + kcols
    return q, [band(k0, -1), band(k0, 0), band(k0, 1), ctx(k0)], [band(v0, -1), band(v0, 0), band(v0, 1), ctx(v0)]


def _attn_common(T, R):
    nl = T // QB
    nk = 3 * QB + (R - T)

    def low_lanes():
        return lax.broadcasted_iota(jnp.int32, (1, 128), 1) < HEAD_DIM

    def dup(v, par):
        low = low_lanes()
        vf = v.astype(F32)
        r = pltpu.roll(vf, HEAD_DIM, 1)
        return (jnp.where(low, vf, r) if par == 0 else jnp.where(low, r, vf)).astype(BF16)

    def stack(ref, par):
        low = low_lanes()
        pa = ref[:, (2 * par) * 128:(2 * par + 1) * 128].astype(BF16)
        pb = ref[:, (2 * par + 1) * 128:(2 * par + 2) * 128].astype(BF16)
        zero = jnp.zeros_like(pa)
        return jnp.concatenate([jnp.where(low, pa, zero), jnp.where(low, zero, pa),
                                jnp.where(low, pb, zero), jnp.where(low, zero, pb)], axis=0)

    def unstack(v):
        low = low_lanes()
        return (jnp.where(low, v[0:QB], v[QB:2 * QB]), jnp.where(low, v[2 * QB:3 * QB], v[3 * QB:4 * QB]))

    def mask_of(b):
        col = lax.broadcasted_iota(jnp.int32, (1, nk), 1)
        gone = (((col < QB) & (b == 0)) | ((col >= 2 * QB) & (col < 3 * QB) & (b == nl - 1))
                | ((col < 3 * QB) & (b >= nl)))
        return jnp.where(gone, NEG_INF, 0.0)

    def sink_col(sink_ref, first):
        blk = lax.broadcasted_iota(jnp.int32, (4 * QB, 1), 0) // QB
        out = jnp.zeros((4 * QB, 1), F32) + sink_ref[first]
        for h in range(1, 4):
            out = jnp.where(blk == h, sink_ref[first + h], out)
        return out

    def scores(qs, kd, mask, sink):
        s = lax.dot_general(qs, kd, (((1,), (1,)), ((), ())), preferred_element_type=F32) + mask
        m = jnp.maximum(jnp.max(s, axis=-1, keepdims=True), sink)
        e = jnp.exp(s - m)
        es = jnp.exp(sink - m)
        return e, es, 1.0 / (jnp.sum(e, axis=-1, keepdims=True) + es)
    return low_lanes, dup, stack, unstack, mask_of, sink_col, scores


def window_bias(T, R):
    nk = 3 * QB + (R - T)
    row = jnp.arange(QB)[:, None]
    col = jnp.arange(nk)[None, :]
    near = (jnp.abs(col - QB - row) <= WINDOW) | (col >= 3 * QB)
    return jnp.tile(jnp.where(near, 0.0, NEG_INF).astype(F32), (4, 1))


def attn_fwd(qkvr, sinks, bias, T):
    R = qkvr.shape[0]
    qspec, kspecs, vspecs = _attn_specs(T, R)
    _, dup, stack, unstack, mask_of, sink_col, scores = _attn_common(T, R)

    def body(q_ref, kp, ko, kn, kc, vp, vo, vn, vc, sink_ref, bias_ref, o_ref):
        jj, b = pl.program_id(0), pl.program_id(1)
        mask = bias_ref[...] + mask_of(b)
        k_all = jnp.concatenate([kp[...], ko[...], kn[...], kc[...]], axis=0)
        v_all = jnp.concatenate([vp[...], vo[...], vn[...], vc[...]], axis=0)
        for par in range(2):
            kd, vd = dup(k_all, par), dup(v_all, par)
            e, _, rz = scores(stack(q_ref, par), kd, mask, sink_col(sink_ref, jj * 8 + par * 4))
            o = jnp.dot((e * rz).astype(BF16), vd, preferred_element_type=F32)
            pa, pb = unstack(o)
            o_ref[:, (2 * par) * 128:(2 * par + 1) * 128] = pa.astype(BF16)
            o_ref[:, (2 * par + 1) * 128:(2 * par + 2) * 128] = pb.astype(BF16)
    return pl.pallas_call(
        body, name="attn_fwd", grid=(2, R // QB),
        in_specs=[qspec, *kspecs, *vspecs, pl.BlockSpec(memory_space=pltpu.SMEM),
                  pl.BlockSpec(bias.shape, lambda jj, b: (0, 0))],
        out_specs=pl.BlockSpec((QB, 512), lambda jj, b: (b, jj)),
        out_shape=jax.ShapeDtypeStruct((R, N_HEADS * HEAD_DIM), BF16),
        compiler_params=_params(48))(qkvr, *([qkvr] * 8), sinks, bias)


def attn_bwd(qkvr, do, sinks, bias, T):
    R = qkvr.shape[0]
    tc = R - T
    qspec, kspecs, vspecs = _attn_specs(T, R)
    _, dup, stack, unstack, mask_of, sink_col, scores = _attn_common(T, R)
    contract_rows = (((0,), (0,)), ((), ()))
    contract_last = (((1,), (1,)), ((), ()))

    def body(q_ref, kp, ko, kn, kc, vp, vo, vn, vc, do_ref, sink_ref, bias_ref,
             dq_ref, dkp, dko, dkn, dvp, dvo, dvn, dkc_ref, dvc_ref, dsink_ref):
        jj, b = pl.program_id(0), pl.program_id(1)

        @pl.when((jj == 0) & (b == 0))
        def _():
            dsink_ref[...] = jnp.zeros_like(dsink_ref)

        @pl.when(b == 0)
        def _():
            dkc_ref[...] = jnp.zeros_like(dkc_ref)
            dvc_ref[...] = jnp.zeros_like(dvc_ref)
        mask = bias_ref[...] + mask_of(b)
        k_all = jnp.concatenate([kp[...], ko[...], kn[...], kc[...]], axis=0)
        v_all = jnp.concatenate([vp[...], vo[...], vn[...], vc[...]], axis=0)
        lane = lax.broadcasted_iota(jnp.int32, (8, 128), 1)
        srow = lax.broadcasted_iota(jnp.int32, (8, 128), 0)
        dk_fold, dv_fold = [], []
        for par in range(2):
            kd, vd = dup(k_all, par), dup(v_all, par)
            first = jj * 8 + par * 4
            qs, dos = stack(q_ref, par), stack(do_ref, par)
            e, es, rz = scores(qs, kd, mask, sink_col(sink_ref, first))
            p = e * rz
            dp = lax.dot_general(dos, vd, contract_last, preferred_element_type=F32)
            delta = jnp.sum(p * dp, axis=-1, keepdims=True)
            ds = (p * (dp - delta)).astype(BF16)
            t = es * rz * delta
            for h in range(4):
                dsink = -jnp.sum(t[h * QB:(h + 1) * QB])
                dsink_ref[...] += jnp.where((lane == first + h) & (srow == 0), dsink, 0.0)
            pa, pb = unstack(jnp.dot(ds, kd, preferred_element_type=F32))
            dq_ref[:, (2 * par) * 128:(2 * par + 1) * 128] = pa
            dq_ref[:, (2 * par + 1) * 128:(2 * par + 2) * 128] = pb
            dk_t = lax.dot_general(qs, ds, contract_rows, preferred_element_type=F32)
            dv_t = lax.dot_general(dos, p.astype(BF16), contract_rows, preferred_element_type=F32)
            dk_fold.append(dk_t + pltpu.roll(dk_t, HEAD_DIM, 0))
            dv_fold.append(dv_t + pltpu.roll(dv_t, HEAD_DIM, 0))
        low_rows = lax.broadcasted_iota(jnp.int32, (128, 1), 0) < HEAD_DIM
        dk = jnp.where(low_rows, dk_fold[0], dk_fold[1]).T
        dv = jnp.where(low_rows, dv_fold[0], dv_fold[1]).T
        dkp[...], dko[...], dkn[...] = dk[0:QB], dk[QB:2 * QB], dk[2 * QB:3 * QB]
        dvp[...], dvo[...], dvn[...] = dv[0:QB], dv[QB:2 * QB], dv[2 * QB:3 * QB]
        dkc_ref[...] += dk[3 * QB:]
        dvc_ref[...] += dv[3 * QB:]
    blk = pl.BlockSpec((QB, 128), lambda jj, b: (b, jj))
    cblk = pl.BlockSpec((tc, 128), lambda jj, b: (0, jj))
    part = jax.ShapeDtypeStruct((R, 256), F32)
    csum = jax.ShapeDtypeStruct((tc, 256), F32)
    outs = pl.pallas_call(
        body, name="attn_bwd", grid=(2, R // QB),
        in_specs=[qspec, *kspecs, *vspecs, pl.BlockSpec((QB, 512), lambda jj, b: (b, jj)),
                  pl.BlockSpec(memory_space=pltpu.SMEM), pl.BlockSpec(bias.shape, lambda jj, b: (0, 0))],
        out_specs=[pl.BlockSpec((QB, 512), lambda jj, b: (b, jj)), blk, blk, blk, blk, blk, blk, cblk, cblk,
                   pl.BlockSpec((8, 128), lambda jj, b: (0, 0))],
        out_shape=[jax.ShapeDtypeStruct((R, N_HEADS * HEAD_DIM), F32), part, part, part, part, part, part,
                   csum, csum, jax.ShapeDtypeStruct((8, 128), F32)],
        compiler_params=_params(48))(qkvr, *([qkvr] * 8), do, sinks, bias)
    return outs[0], outs[1:4], outs[4:7], outs[7], outs[8], outs[9]


def loss_head(x, nw, target, T):
    R, dm = x.shape
    nl = T // TMR

    def body(x_ref, nw_ref, t_ref, loss_ref, dx_ref, dnw_ref):
        i = pl.program_id(0)

        @pl.when(i == 0)
        def _():
            loss_ref[...] = jnp.zeros_like(loss_ref)
            dnw_ref[...] = jnp.zeros_like(dnw_ref)
        live = (i < nl).astype(F32)
        nwv = nw_ref[...]
        xv = x_ref[...]
        r = lax.rsqrt(jnp.mean(xv * xv, axis=-1, keepdims=True) + EPS)
        xh = xv * r
        err = xh * nwv - t_ref[...]
        per_row = jnp.mean(err * err, axis=-1, keepdims=True)
        loss_ref[...] += 0.5 * live * jnp.sum(per_row)
        dy = err * (live / dm)
        dnw_ref[...] += _colsum8(dy * xh)
        dxh = dy * nwv
        dx_ref[...] = r * (dxh - xh * jnp.mean(dxh * xh, axis=-1, keepdims=True))
    tile = pl.BlockSpec((TMR, dm), lambda i: (i, 0))
    return pl.pallas_call(
        body, name="loss_head", grid=(R // TMR,),
        in_specs=[tile, pl.BlockSpec((1, dm), lambda i: (0, 0)),
                  pl.BlockSpec((TMR, dm), lambda i: (jnp.minimum(i, nl - 1), 0))],
        out_specs=[pl.BlockSpec((8, 128), lambda i: (0, 0)), tile, pl.BlockSpec((8, dm), lambda i: (0, 0))],
        out_shape=[jax.ShapeDtypeStruct((8, 128), F32), jax.ShapeDtypeStruct((R, dm), F32),
                   jax.ShapeDtypeStruct((8, dm), F32)])(x, nw, target)


def adaln_fwd(cond, w_mod, b_mod):
    nl, dm, ns = w_mod.shape

    def body(c_ref, w_ref, b_ref, o_ref):
        cv = c_ref[...]
        s = (cv * _sigmoid(cv)).astype(BF16)
        o_ref[...] = jnp.dot(s, w_ref[...].astype(BF16), preferred_element_type=F32) + b_ref[...]
    return pl.pallas_call(
        body, name="adaln_fwd", grid=(nl,),
        in_specs=[pl.BlockSpec((16, dm), lambda l: (0, 0)), pl.BlockSpec((None, dm, ns), lambda l: (l, 0, 0)),
                  pl.BlockSpec((None, 1, ns), lambda l: (l, 0, 0))],
        out_specs=pl.BlockSpec((None, 16, ns), lambda l: (l, 0, 0)),
        out_shape=jax.ShapeDtypeStruct((nl, 16, ns), F32), compiler_params=_params(48))(cond, w_mod, b_mod)


def adaln_bwd(cond, dmod, w_mod):
    nl, dm, ns = w_mod.shape

    def body(c_ref, d_ref, w_ref, gw_ref, ds_ref):
        l = pl.program_id(0)

        @pl.when(l == 0)
        def _():
            ds_ref[...] = jnp.zeros_like(ds_ref)
        cv = c_ref[...]
        s = (cv * _sigmoid(cv)).astype(BF16)
        dv = d_ref[...].astype(BF16)
        gw_ref[...] = lax.dot_general(s, dv, (((0,), (0,)), ((), ())), preferred_element_type=F32)
        ds_ref[...] += lax.dot_general(dv, w_ref[...].astype(BF16), (((1,), (1,)), ((), ())),
                                       preferred_element_type=F32)
    return pl.pallas_call(
        body, name="adaln_bwd", grid=(nl,),
        in_specs=[pl.BlockSpec((16, dm), lambda l: (0, 0)), pl.BlockSpec((None, 16, ns), lambda l: (l, 0, 0)),
                  pl.BlockSpec((None, dm, ns), lambda l: (l, 0, 0))],
        out_specs=[pl.BlockSpec((None, dm, ns), lambda l: (l, 0, 0)), pl.BlockSpec((16, dm), lambda l: (0, 0))],
        out_shape=[jax.ShapeDtypeStruct((nl, dm, ns), F32), jax.ShapeDtypeStruct((16, dm), F32)],
        compiler_params=_params(48))(cond, dmod, w_mod)


def _me():
    return lax.axis_index("x"), lax.axis_index("y"), lax.axis_index("c")


def allgather8(block):
    m_per, n = block.shape

    def body(x_ref, out_ref, send_sems, recv_sems, local_sem):
        x, y, c = _me()
        me, sibling = (x, y, c), (x, y, 1 - c)
        chips = [(1 - x, y), (x, 1 - y), (1 - x, 1 - y)]

        def rows(px, py, pc):
            return out_ref.at[pl.ds((4 * px + 2 * py + pc) * m_per, m_per), :]

        def copy(k, blk, to, src=None):
            return pltpu.make_async_remote_copy(
                src_ref=rows(*blk) if src is None else src, dst_ref=rows(*blk),
                send_sem=send_sems.at[k], recv_sem=recv_sems.at[k], device_id=to, device_id_type=MESH)
        mine = pltpu.make_async_copy(x_ref, rows(*me), local_sem)
        mine.start()
        first = [copy(0, me, sibling, src=x_ref)]
        first += [copy(1 + j, me, (*chip, c), src=x_ref) for j, chip in enumerate(chips)]
        for cp in first:
            cp.start()
        passed = [copy(4 + j, (*chip, c), sibling) for j, chip in enumerate(chips)]
        for j, chip in enumerate(chips):
            copy(1 + j, (*chip, c), me).wait_recv()
            passed[j].start()
        copy(0, sibling, me).wait_recv()
        for j, chip in enumerate(chips):
            copy(4 + j, (*chip, 1 - c), me).wait_recv()
        for cp in first + passed:
            cp.wait_send()
        mine.wait()
    return pl.pallas_call(
        body, name="allgather8",
        out_shape=jax.ShapeDtypeStruct((N_DEV * m_per, n), block.dtype),
        in_specs=[pl.BlockSpec(memory_space=pltpu.VMEM)],
        out_specs=pl.BlockSpec(memory_space=pltpu.VMEM),
        scratch_shapes=[pltpu.SemaphoreType.DMA((7,)), pltpu.SemaphoreType.DMA((7,)), pltpu.SemaphoreType.DMA],
        compiler_params=_params(48))(block)


def _other_chips(x, y):
    return [(1 - x, y), (x, 1 - y), (1 - x, 1 - y)]


_HBM = pl.BlockSpec(memory_space=pltpu.HBM)
_SEM = pl.BlockSpec(memory_space=pltpu.SEMAPHORE)
_ANY = pl.BlockSpec(memory_space=pl.ANY)
_EFFECT = pltpu.SideEffectType.DATAFLOW_SIDE_EFFECTING


def _in_hbm(v):
    return pltpu.with_memory_space_constraint(v, pltpu.HBM)


def cast_into_slot(w, chip_id):
    kb, nb = w.shape
    tr = _row_tile(kb)

    def body(chip_ref, w_ref, o_ref):
        del chip_ref
        o_ref[...] = w_ref[...].astype(BF16)
    return pl.pallas_call(
        body, name="cast_into_slot",
        grid_spec=pltpu.PrefetchScalarGridSpec(
            num_scalar_prefetch=1, grid=(kb // tr,),
            in_specs=[pl.BlockSpec((tr, nb), lambda i, chip: (i, 0))],
            out_specs=pl.BlockSpec((None, tr, nb), lambda i, chip: (chip[0], i, 0))),
        out_shape=jax.ShapeDtypeStruct((N_CHIP, kb, nb), BF16))(chip_id, w)


def _split_copies(mode, srcs, lands, send_sems, recv_sems):
    x, y, c = _me()
    out = []
    for t in range(len(lands)):
        for k, chip in enumerate(_other_chips(x, y)):
            if mode == "gather":
                src = dst = lands[t].at[2 * x + y]
                landed = lands[t].at[2 * chip[0] + chip[1]]
            else:
                src, dst, landed = srcs[t].at[2 * chip[0] + chip[1]], lands[t].at[k], lands[t].at[k]
            send = pltpu.make_async_remote_copy(src_ref=src, dst_ref=dst, send_sem=send_sems.at[3 * t + k],
                                                recv_sem=recv_sems.at[3 * t + k], device_id=(*chip, c),
                                                device_id_type=MESH)
            recv = pltpu.make_async_remote_copy(src_ref=src, dst_ref=landed, send_sem=send_sems.at[3 * t + k],
                                                recv_sem=recv_sems.at[3 * t + k], device_id=(*chip, c),
                                                device_id_type=MESH)
            out.append((send, recv))
    return out


def exchange_start(name, mode, srcs, lands, after):
    ns, nl = len(srcs), len(lands)
    na = ns + nl

    def body(*refs):
        src_refs, land_refs = refs[:ns], refs[ns:na]
        send_sems, recv_sems = refs[na + 1], refs[na + 2]
        token = refs[-1]
        for send, _ in _split_copies(mode, src_refs, land_refs, send_sems, recv_sems):
            send.start()
        token[...] = jnp.zeros_like(token)
    arrays = list(srcs) + list(lands)
    outs = pl.pallas_call(
        body, name=name,
        out_shape=(pltpu.SemaphoreType.DMA((3 * nl,)), pltpu.SemaphoreType.DMA((3 * nl,)),
                   *[pltpu.HBM(v.shape, v.dtype) for v in arrays], jax.ShapeDtypeStruct((8, 128), F32)),
        in_specs=[_HBM] * na + [_ANY],
        out_specs=(_SEM, _SEM, *[_HBM] * na, pl.BlockSpec(memory_space=pltpu.VMEM)),
        input_output_aliases={i: 2 + i for i in range(na)},
        compiler_params=pltpu.CompilerParams(has_side_effects=_EFFECT))(*[_in_hbm(v) for v in arrays], after)
    return outs[0], outs[1], list(outs[2:2 + ns]), list(outs[2 + ns:2 + na]), outs[-1]


def exchange_wait(name, mode, send_sems, recv_sems, srcs, lands, after):
    ns, nl = len(srcs), len(lands)
    na = ns + nl

    def body(*refs):
        for _, recv in _split_copies(mode, refs[:ns], refs[ns:na], refs[na], refs[na + 1]):
            recv.wait_send()
            recv.wait_recv()
    arrays = list(srcs) + list(lands)
    outs = pl.pallas_call(
        body, name=name,
        out_shape=[pltpu.HBM(v.shape, v.dtype) for v in arrays],
        in_specs=[_HBM] * na + [_SEM, _SEM, _ANY], out_specs=[_HBM] * na,
        input_output_aliases={i: i for i in range(na)},
        compiler_params=pltpu.CompilerParams(has_side_effects=_EFFECT))(*arrays, send_sems, recv_sems, after)
    return list(outs[:ns]), list(outs[ns:])


def swap_with_sibling(v):
    def body(v_ref, out_ref, send_sem, recv_sem):
        x, y, c = _me()
        cp = pltpu.make_async_remote_copy(src_ref=v_ref, dst_ref=out_ref, send_sem=send_sem, recv_sem=recv_sem,
                                          device_id=(x, y, 1 - c), device_id_type=MESH)
        cp.start()
        cp.wait()
    return pl.pallas_call(
        body, name="swap_with_sibling", out_shape=jax.ShapeDtypeStruct(v.shape, v.dtype),
        in_specs=[pl.BlockSpec(memory_space=pl.ANY)], out_specs=pl.BlockSpec(memory_space=pl.ANY),
        scratch_shapes=[pltpu.SemaphoreType.DMA, pltpu.SemaphoreType.DMA])(v)


def sum_slots(parts):
    n, rows, w = parts.shape
    tr = _row_tile(rows)

    def body(p_ref, o_ref):
        acc = p_ref[0].astype(F32)
        for k in range(1, n):
            acc = acc + p_ref[k].astype(F32)
        o_ref[...] = acc
    return pl.pallas_call(
        body, name="sum_slots", grid=(rows // tr,),
        in_specs=[pl.BlockSpec((n, tr, w), lambda i: (0, i, 0))], out_specs=pl.BlockSpec((tr, w), lambda i: (i, 0)),
        out_shape=jax.ShapeDtypeStruct((rows, w), F32), compiler_params=_params(48))(parts)


def sum_landed(landed, own, chip_id, layer, n_layers, buf):
    n, rows, w = landed.shape
    tr = _row_tile(rows)
    base = layer * (rows // tr)

    def compute(l_ref, g_ref, o_ref):
        acc = g_ref[...].astype(F32)
        for k in range(n):
            acc = acc + l_ref[k].astype(F32)
        o_ref[...] = acc
    in_specs = [pl.BlockSpec((n, tr, w), lambda i, chip: (0, i, 0)),
                pl.BlockSpec((None, tr, w), lambda i, chip: (chip[0], i, 0))]
    out_spec = pl.BlockSpec((tr, w), lambda i, chip: (base + i, 0))
    out_shape = jax.ShapeDtypeStruct((n_layers * rows, w), F32)
    if buf is None:
        def body(chip_ref, l_ref, g_ref, o_ref):
            del chip_ref
            compute(l_ref, g_ref, o_ref)
        return pl.pallas_call(
            body, name="sum_landed",
            grid_spec=pltpu.PrefetchScalarGridSpec(num_scalar_prefetch=1, grid=(rows // tr,), in_specs=in_specs,
                                                   out_specs=out_spec),
            out_shape=out_shape, compiler_params=_params(48))(chip_id, landed, own)

    def body(chip_ref, l_ref, g_ref, buf_ref, o_ref):
        del chip_ref, buf_ref
        compute(l_ref, g_ref, o_ref)
    return pl.pallas_call(
        body, name="sum_landed_into",
        grid_spec=pltpu.PrefetchScalarGridSpec(num_scalar_prefetch=1, grid=(rows // tr,),
                                               in_specs=in_specs + [_ANY], out_specs=out_spec),
        out_shape=out_shape, input_output_aliases={3: 0}, compiler_params=_params(48))(chip_id, landed, own, buf)


def adamw(w, ga, gb, m, v):
    rows, wd = w.shape
    tr = min(_row_tile(rows), 128)
    c1 = 1.0 / (1.0 - ADAM_B1 ** ADAM_STEP)
    c2 = 1.0 / (1.0 - ADAM_B2 ** ADAM_STEP)

    def update(wv, g, mv, vv, g_ref, d_ref, m_ref, v_ref):
        mn = ADAM_B1 * mv + (1.0 - ADAM_B1) * g
        vn = ADAM_B2 * vv + (1.0 - ADAM_B2) * (g * g)
        g_ref[...] = g
        m_ref[...] = mn
        v_ref[...] = vn
        d_ref[...] = -ADAM_LR * ((mn * c1) / (jnp.sqrt(vn * c2) + ADAM_EPS) + ADAM_WD * wv)
    tile = pl.BlockSpec((tr, wd), lambda i: (i, 0))
    out = jax.ShapeDtypeStruct((rows, wd), F32)
    if gb is None:
        def body(w_ref, ga_ref, m_ref, v_ref, g_out, d_out, m_out, v_out):
            update(w_ref[...], ga_ref[...], m_ref[...], v_ref[...], g_out, d_out, m_out, v_out)
        return pl.pallas_call(body, name="adamw", grid=(rows // tr,), in_specs=[tile] * 4,
                              out_specs=[tile] * 4, out_shape=[out] * 4)(w, ga, m, v)

    def body(w_ref, ga_ref, gb_ref, m_ref, v_ref, g_out, d_out, m_out, v_out):
        update(w_ref[...], ga_ref[...] + gb_ref[...], m_ref[...], v_ref[...], g_out, d_out, m_out, v_out)
    return pl.pallas_call(body, name="adamw_sum", grid=(rows // tr,), in_specs=[tile] * 5,
                          out_specs=[tile] * 4, out_shape=[out] * 4)(w, ga, gb, m, v)


def _rope_tables(T, R):
    rows = T // GRID_W
    row = jnp.repeat(jnp.arange(rows), GRID_W).astype(F32)
    col = jnp.tile(jnp.arange(GRID_W), rows).astype(F32)
    n_freq = HEAD_DIM // 4
    inv_freq = ROPE_THETA ** (-jnp.arange(n_freq, dtype=F32) / n_freq)
    ang = jnp.concatenate([row[:, None] * inv_freq, col[:, None] * inv_freq], axis=-1)
    cos, sin = jnp.cos(ang), jnp.sin(ang)
    cs = jnp.tile(cos, (1, 4))
    sn = jnp.tile(jnp.concatenate([-sin, sin], axis=-1), (1, 2))
    pad = R - T
    return (jnp.concatenate([cs, jnp.ones((pad, 128), F32)], axis=0),
            jnp.concatenate([sn, jnp.zeros((pad, 128), F32)], axis=0))


def _pack(parts, mult=8 * 128):
    flat = jnp.concatenate([p.reshape(-1).astype(F32) for p in parts])
    pad = (-flat.shape[0]) % mult
    return jnp.pad(flat, (0, pad)).reshape(-1, 128)


def _unpack(buf, shapes):
    flat = buf.reshape(-1)
    out, o = [], 0
    for s in shapes:
        n = 1
        for d in s:
            n *= d
        out.append(flat[o:o + n].reshape(s))
        o += n
    return out


def kernel(x, c, ctx, c_ctx, w_mod, b_mod, norm_mix, norm_ffn, w_in_ab, conv_a, conv_b, conv_b_bias, ln_b_gain, ln_b_bias, w_out_ab, w_qkv, w_o, sinks, w_up, w_conv_ffn, w_down, final_norm, loss_target, m_c_ctx, m_w_mod, m_b_mod, m_norm_mix, m_norm_ffn, m_w_in_ab, m_conv_a, m_conv_b, m_conv_b_bias, m_ln_b_gain, m_ln_b_bias, m_w_out_ab, m_w_qkv, m_w_o, m_sinks, m_w_up, m_w_conv_ffn, m_w_down, m_final_norm, v_c_ctx, v_w_mod, v_b_mod, v_norm_mix, v_norm_ffn, v_w_in_ab, v_conv_a, v_conv_b, v_conv_b_bias, v_ln_b_gain, v_ln_b_bias, v_w_out_ab, v_w_qkv, v_w_o, v_sinks, v_w_up, v_w_conv_ffn, v_w_down, v_final_norm):
    T, dm = x.shape[1], x.shape[2]
    tc = ctx.shape[1]
    R = T + tc
    depth = w_mod.shape[0]
    ax, ay, ac = lax.axis_index("x"), lax.axis_index("y"), lax.axis_index("c")
    chip = 2 * ax + ay
    dev = 4 * ax + 2 * ay + ac

    small_w = [conv_a, conv_b, w_conv_ffn]
    gathered = allgather8(_pack([c] + small_w)).reshape(N_DEV, -1)
    cond8 = gathered[:, :dm]
    off = dm
    full_small = []
    for wsh in small_w:
        n = wsh.size
        per_chip = gathered[0::2, off:off + n].reshape((N_CHIP,) + wsh.shape)
        full_small.append(jnp.concatenate([per_chip[q] for q in range(N_CHIP)], axis=-1))
        off += n
    conv_a_f, conv_b_f, w_conv_ffn_f = full_small
    cond = jnp.concatenate([cond8, c_ctx[None, :], jnp.zeros((7, dm), F32)], axis=0)

    ns_mod = w_mod.shape[2]
    b_mod_sh = lax.dynamic_slice_in_dim(b_mod, chip * ns_mod, ns_mod, axis=1)[:, None, :]
    mod_sh = adaln_fwd(cond, w_mod, b_mod_sh)
    mod_all = allgather8(mod_sh.reshape(depth * 16, ns_mod)).reshape(N_DEV, depth, 16, ns_mod)
    mod_full = jnp.concatenate([mod_all[2 * q] for q in range(N_CHIP)], axis=-1)
    mine = lax.dynamic_index_in_dim(mod_full, dev, axis=1, keepdims=False)
    mods = jnp.stack([mine, mod_full[:, 8]], axis=1).reshape(depth, 2, 6, dm)

    masters = {"w_in_ab": w_in_ab, "w_out_ab": w_out_ab, "w_qkv": w_qkv, "w_o": w_o, "w_up": w_up, "w_down": w_down}
    chip_id = chip.astype(jnp.int32).reshape(1)

    def half_weights(l, half):
        if half == 1:
            return [("w_up", l), ("w_down", l)]
        return [("w_in_ab", l // 2), ("w_out_ab", l // 2)] if l % 2 == 0 else [("w_qkv", l // 2), ("w_o", l // 2)]
    in_flight, after = {}, mods
    for l in range(depth):
        for half in range(2):
            lands = [cast_into_slot(masters[n][j], chip_id) for n, j in half_weights(l, half)]
            send_sems, recv_sems, _, lands, after = exchange_start(f"gather_start_{l}_{half}", "gather", [], lands, after)
            in_flight[l, half] = (send_sems, recv_sems, lands)
    mods = mods + after[0, 0]

    def gathered_weights(l, half, after):
        send_sems, recv_sems, lands = in_flight[l, half]
        _, landed = exchange_wait(f"gather_wait_{l}_{half}", "gather", send_sems, recv_sems, [], lands, after)
        return dict(zip([n for n, _ in half_weights(l, half)], landed))

    cs, sn = _rope_tables(T, R)
    bias = window_bias(T, R)
    sinks_flat = sinks.reshape(-1)

    xs = jnp.concatenate([x[0], ctx[0]], axis=0)
    saved, W = [], []
    h1 = norm_mod_fwd(xs, norm_mix[0][None], mods[0], 0, T)
    for l in range(depth):
        e = l // 2
        wl = gathered_weights(l, 0, h1)
        W.append(wl)
        s = {"x0": xs, "h1": h1}
        if l % 2 == 0:
            p = mm_nn(h1, wl["w_in_ab"], BF16)
            yab, y1, x1, h2 = mixer_fwd(p, conv_a_f[e], conv_b_f[e], conv_b_bias[e][None], ln_b_gain[e][None],
                                        ln_b_bias[e][None], wl["w_out_ab"], xs, norm_ffn[l][None], mods[l], 2, 3, T)
            s.update(p=p, mix=yab)
        else:
            qkvr = mm_qkv_rope(h1, wl["w_qkv"], cs, sn)
            att = attn_fwd(qkvr, sinks_flat[e * N_HEADS:(e + 1) * N_HEADS], bias, T)
            s.update(qkvr=qkvr, mix=att)
            y1, x1, h2 = mm_resid_norm_fwd(att, wl["w_o"], xs, norm_ffn[l][None], mods[l], mods[l], 2, 3, T)
        wl.update(gathered_weights(l, 1, h2))
        u = mm_nn(h2, wl["w_up"], BF16)
        if l + 1 < depth:
            z, y2, xs, h1 = ffn_mid_fwd(u, w_conv_ffn_f[l], wl["w_down"], x1, norm_mix[l + 1][None], mods[l],
                                        mods[l + 1], 5, 0, T)
        else:
            z, y2, xs = ffn_mid_fwd(u, w_conv_ffn_f[l], wl["w_down"], x1, None, mods[l], None, 5, 0, T)
        s.update(y1=y1, x1=x1, h2=h2, u=u, z=z, y2=y2)
        saved.append(s)

    loss_part, dx, d_final = loss_head(xs, final_norm[None], loss_target[0], T)
    loss = lax.psum(loss_part[0, 0], ("x", "y", "c"))

    d_mods, d_norm_mix, d_norm_ffn = [None] * depth, [None] * depth, [None] * depth
    d_conv_a, d_conv_b, d_vecs, d_sinks, d_wc = [None] * 2, [None] * 2, [None] * 2, [None] * 2, [None] * depth
    dss1, dss2, dg1, dg2 = [None] * depth, [None] * depth, [None] * depth, [None] * depth
    scattering = {}

    def scatter(l, half, G, after):
        grads_h = [G[n] for n, _ in half_weights(l, half)]
        lands = [lax.empty((N_CHIP - 1, *g.shape[1:]), g.dtype) for g in grads_h]
        send_sems, recv_sems, grads_h, lands, token = exchange_start(
            f"scatter_start_{l}_{half}", "scatter", grads_h, lands, after)
        scattering[l, half] = (send_sems, recv_sems, grads_h, lands)
        return token

    dy2, dg2[depth - 1] = resid_bwd(dx, saved[depth - 1]["y2"], mods[depth - 1], 5, T)
    pending = 0.0
    for l in reversed(range(depth)):
        e = l // 2
        s, wl = saved[l], W[l]
        G = {}
        G["w_down"] = mm_tn(s["z"], dy2, "row", wl["w_down"])
        duc, d_wc[l] = ffn_mid_bwd(mm_nt(dy2, wl["w_down"], BF16), s["u"], w_conv_ffn_f[l] + pending, T)
        du, dx, dy1, dss2[l], d_norm_ffn[l], dg1[l] = ffn_up_bwd(
            duc, w_conv_ffn_f[l], wl["w_up"], s["x1"], norm_ffn[l][None], mods[l], dx, s["y1"], mods[l], 3, 2, T)
        G["w_up"] = mm_tn(s["h2"], du, "col", wl["w_up"])
        started = scatter(l, 1, G, du)[0, 0]
        if l % 2 == 0:
            G["w_out_ab"] = mm_tn(s["mix"], dy1, "row", wl["w_out_ab"])
            dyab = mm_nt(dy1, wl["w_out_ab"], F32)
            dmid, d_conv_a[e], d_conv_b[e], d_vecs[e] = convmix_bwd1(
                dyab, s["p"], conv_a_f[e] + started, conv_b_f[e], conv_b_bias[e][None], ln_b_gain[e][None],
                ln_b_bias[e][None], T)
            if l > 0:
                dp, dx, dy2, dss1[l], d_norm_mix[l], dg2[l - 1] = mixer_in_bwd(
                    dmid, s["p"], conv_a_f[e], conv_b_f[e], wl["w_in_ab"], s["x0"], norm_mix[l][None], mods[l], dx,
                    saved[l - 1]["y2"], mods[l - 1], 0, 5, T)
            else:
                dp, dx, dss1[l], d_norm_mix[l] = mixer_in_bwd(
                    dmid, s["p"], conv_a_f[e], conv_b_f[e], wl["w_in_ab"], s["x0"], norm_mix[l][None], mods[l], dx,
                    None, None, 0, 0, T)
            G["w_in_ab"] = mm_tn(s["h1"], dp, "col", wl["w_in_ab"])
        else:
            G["w_o"] = mm_tn(s["mix"], dy1, "row", wl["w_o"])
            datt = mm_nt(dy1, wl["w_o"], BF16)
            dq, dks, dvs, dkc, dvc, d_sinks[e] = attn_bwd(
                s["qkvr"], datt, sinks_flat[e * N_HEADS:(e + 1) * N_HEADS] + started, bias, T)
            dqkv = rope_bwd(dq, dks, dvs, dkc, dvc, cs, sn, T)
            G["w_qkv"] = mm_tn(s["h1"], dqkv, "col", wl["w_qkv"])
            dx, dy2, dss1[l], d_norm_mix[l], dg2[l - 1] = mm_norm_resid_bwd(
                dqkv, wl["w_qkv"], s["x0"], norm_mix[l][None], mods[l], dx, saved[l - 1]["y2"], mods[l - 1], 0, 5, T)
        token = scatter(l, 0, G, dx)
        pending = token[0, 0]
    grad_x = dx[:T][None]
    for l in range(depth):
        a1, a2 = dss1[l].sum(2), dss2[l].sum(2)
        d_mods[l] = jnp.stack([a1[:, 0], a1[:, 1], dg1[l].sum(1), a2[:, 0], a2[:, 1], dg2[l].sum(1)], axis=1)

    d_mods = jnp.stack(d_mods)
    summed_parts = [
        d_mods[:, 1],
        jnp.stack(d_norm_mix).sum(1), jnp.stack(d_norm_ffn).sum(1),
        jnp.stack(d_conv_a).sum(2), jnp.stack(d_conv_b).sum(2),
        jnp.stack(d_vecs).sum(2),
        jnp.stack(d_sinks)[:, 0, :N_HEADS],
        jnp.stack(d_wc).sum(2), d_final.sum(0)]
    summed_shapes = [p.shape for p in summed_parts]
    n_own = depth * 6 * dm
    pack = _pack([d_mods[:, 0]] + summed_parts)
    parts = allgather8(pack).reshape(N_DEV, -1, 128)
    total = sum_slots(parts)
    own_rows = parts.reshape(N_DEV, -1)[:, :n_own].reshape(N_DEV, depth, 6 * dm)
    (dmod_ctx, g_norm_mix, g_norm_ffn, g_conv_a, g_conv_b, g_vecs, g_sinks, g_wc, g_final) = _unpack(
        total.reshape(-1)[n_own:], summed_shapes)
    dmod_rows = jnp.concatenate([jnp.moveaxis(own_rows, 0, 1), dmod_ctx.reshape(depth, 1, 6 * dm),
                                 jnp.zeros((depth, 7, 6 * dm), F32)], axis=1)
    g_b_mod = dmod_rows.sum(1)
    dmod_sh = lax.dynamic_slice_in_dim(dmod_rows, chip * ns_mod, ns_mod, axis=2)
    g_w_mod, dsilu = adaln_bwd(cond, dmod_sh, w_mod)
    dsilu_all = allgather8(dsilu[8:16]).reshape(N_DEV, 8, dm)
    dsilu_ctx = sum_slots(dsilu_all[0::2])[0]
    sg = jax.nn.sigmoid(c_ctx)
    g_c_ctx = dsilu_ctx * (sg * (1.0 + c_ctx * (1.0 - sg)))

    def shard_cols(full, width):
        return lax.dynamic_slice_in_dim(full, chip * width, width, axis=full.ndim - 1)
    g_conv_a_s = shard_cols(g_conv_a, conv_a.shape[-1])
    g_conv_b_s = shard_cols(g_conv_b, conv_b.shape[-1])
    g_wc_s = shard_cols(g_wc, w_conv_ffn.shape[-1])

    grads, deltas, new_m, new_v = {}, {}, {}, {}

    def step_2d(name, wv, ga, gb, mv, vv):
        shp = wv.shape
        r2 = lambda t: t.reshape(-1, shp[-1])
        g, d, mn, vn = adamw(r2(wv), r2(ga), None if gb is None else r2(gb), r2(mv), r2(vv))
        grads[name], deltas[name], new_m[name], new_v[name] = (t.reshape(shp) for t in (g, d, mn, vn))

    sums = {n: None for n in masters}
    for l in reversed(range(depth)):
        for half in (1, 0):
            send_sems, recv_sems, grads_h, lands = scattering[l, half]
            grads_h, landed = exchange_wait(f"scatter_wait_{l}_{half}", "scatter", send_sems, recv_sems, grads_h,
                                            lands, token)
            for (n, j), own, arr in zip(half_weights(l, half), grads_h, landed):
                sums[n] = sum_landed(arr, own, chip_id, j, masters[n].shape[0], sums[n])
    moments = {"w_in_ab": (m_w_in_ab, v_w_in_ab), "w_out_ab": (m_w_out_ab, v_w_out_ab),
               "w_qkv": (m_w_qkv, v_w_qkv), "w_o": (m_w_o, v_w_o), "w_up": (m_w_up, v_w_up),
               "w_down": (m_w_down, v_w_down)}
    for name, wv in masters.items():
        other = swap_with_sibling(sums[name])
        step_2d(name, wv, sums[name].reshape(wv.shape), other.reshape(wv.shape), *moments[name])
    step_2d("w_mod", w_mod, g_w_mod, None, m_w_mod, v_w_mod)

    small = [("c_ctx", c_ctx, g_c_ctx, m_c_ctx, v_c_ctx), ("b_mod", b_mod, g_b_mod, m_b_mod, v_b_mod),
             ("norm_mix", norm_mix, g_norm_mix, m_norm_mix, v_norm_mix),
             ("norm_ffn", norm_ffn, g_norm_ffn, m_norm_ffn, v_norm_ffn),
             ("conv_a", conv_a, g_conv_a_s, m_conv_a, v_conv_a), ("conv_b", conv_b, g_conv_b_s, m_conv_b, v_conv_b),
             ("conv_b_bias", conv_b_bias, g_vecs[:, 0], m_conv_b_bias, v_conv_b_bias),
             ("ln_b_gain", ln_b_gain, g_vecs[:, 1], m_ln_b_gain, v_ln_b_gain),
             ("ln_b_bias", ln_b_bias, g_vecs[:, 2], m_ln_b_bias, v_ln_b_bias),
             ("sinks", sinks, g_sinks, m_sinks, v_sinks),
             ("w_conv_ffn", w_conv_ffn, g_wc_s, m_w_conv_ffn, v_w_conv_ffn),
             ("final_norm", final_norm, g_final, m_final_norm, v_final_norm)]
    shapes = [t[1].shape for t in small]
    packed = [_pack([t[k] for t in small]) for k in (1, 2, 3, 4)]
    n_real = sum(t[1].size for t in small)
    lane_id = jnp.arange(packed[3].size).reshape(packed[3].shape)
    packed[3] = jnp.where(lane_id < n_real, packed[3], 1.0)
    outs = adamw(packed[0], packed[1], None, packed[2], packed[3])
    for (name, *_), g, d, mn, vn in zip(small, *[_unpack(o, shapes) for o in outs]):
        grads[name], deltas[name], new_m[name], new_v[name] = g, d, mn, vn

    order = ["c_ctx", "w_mod", "b_mod", "norm_mix", "norm_ffn", "w_in_ab", "conv_a", "conv_b", "conv_b_bias",
             "ln_b_gain", "ln_b_bias", "w_out_ab", "w_qkv", "w_o", "sinks", "w_up", "w_conv_ffn", "w_down",
             "final_norm"]
    return (loss, grad_x, *[grads[n] for n in order], *[deltas[n] for n in order],
            *[new_m[n] for n in order], *[new_v[n] for n in order])
```

```python
import jax
import jax.numpy as jnp
from jax import lax
from jax.experimental import pallas as pl
from jax.experimental.pallas import tpu as pltpu

F32 = jnp.float32
BF16 = jnp.bfloat16
MESH = pl.DeviceIdType.MESH

EPS = 1e-6
NEG_INF = -1e30
GRID_W = 64
HEAD_DIM = 64
N_HEADS = 16
WINDOW = 128
QB = 128
ROPE_THETA = 10000.0
A_W = 512
B_CONV = 31
D_FF = 2816
ADAM_LR, ADAM_B1, ADAM_B2, ADAM_EPS, ADAM_WD, ADAM_STEP = 0.001, 0.9, 0.999, 1e-8, 0.01, 10

TMR = 256
HALO = 16
N_DEV = 8
N_CHIP = 4


def _params(vmem_mb=None):
    if vmem_mb is None:
        return pltpu.CompilerParams()
    return pltpu.CompilerParams(vmem_limit_bytes=vmem_mb * 1024 * 1024)


def _row_tile(rows, cap=768):
    for t in (2816, 1408, 768, 704, 512, 384, 256, 128, 64, 32, 16, 8):
        if t <= cap and rows % t == 0:
            return t
    raise ValueError(rows)


def _colsum8(v):
    r, c = v.shape
    return v.reshape(r // 8, 8, c).sum(axis=0)


def _sigmoid(v):
    return 0.5 * jnp.tanh(0.5 * v) + 0.5


def mm_nn(a, w, out_dtype):
    R = a.shape[0]
    _, kb, nb = w.shape
    tm = _row_tile(R)

    def body(a_ref, w_ref, o_ref):
        av = a_ref[...].astype(BF16)
        for q in range(N_CHIP):
            o_ref[:, q * nb:(q + 1) * nb] = jnp.dot(av, w_ref[q], preferred_element_type=F32).astype(o_ref.dtype)
    return pl.pallas_call(
        body, name="mm_nn_col", grid=(R // tm,),
        in_specs=[pl.BlockSpec((tm, kb), lambda i: (i, 0)),
                  pl.BlockSpec((N_CHIP, kb, nb), lambda i: (0, 0, 0), pipeline_mode=pl.Buffered(1))],
        out_specs=pl.BlockSpec((tm, N_CHIP * nb), lambda i: (i, 0)),
        out_shape=jax.ShapeDtypeStruct((R, N_CHIP * nb), out_dtype),
        compiler_params=_params(48))(a, w)


def mm_nt(d, w, out_dtype):
    R = d.shape[0]
    _, kb, nb = w.shape
    tm = _row_tile(R)
    contract_last = (((1,), (1,)), ((), ()))
    resident = pl.BlockSpec((N_CHIP, kb, nb), lambda i: (0, 0, 0), pipeline_mode=pl.Buffered(1))

    def body(d_ref, w_ref, o_ref):
        wv = w_ref[...].reshape(N_CHIP * kb, nb)
        o_ref[...] = lax.dot_general(d_ref[...].astype(BF16), wv, contract_last,
                                     preferred_element_type=F32).astype(o_ref.dtype)
    return pl.pallas_call(
        body, name="mm_nt_row", grid=(R // tm,),
        in_specs=[pl.BlockSpec((tm, nb), lambda i: (i, 0)), resident],
        out_specs=pl.BlockSpec((tm, N_CHIP * kb), lambda i: (i, 0)),
        out_shape=jax.ShapeDtypeStruct((R, N_CHIP * kb), out_dtype),
        compiler_params=_params(48))(d, w)


def mm_tn(a, d, kind, like):
    R = a.shape[0]
    _, kb, nb = like.shape
    tm = _row_tile(R, 1408 if kind == "col" else 768)
    nsteps = R // tm
    contract_rows = (((0,), (0,)), ((), ()))
    out_shape = jax.ShapeDtypeStruct(like.shape, BF16)

    def accumulate(a_ref, d_ref, acc_ref):
        @pl.when(pl.program_id(1) == 0)
        def _():
            acc_ref[...] = jnp.zeros_like(acc_ref)
        acc_ref[...] += lax.dot_general(a_ref[...].astype(BF16), d_ref[...].astype(BF16), contract_rows,
                                        preferred_element_type=F32)
    if kind == "col":
        def body(a_ref, d_ref, o_ref, acc_ref):
            accumulate(a_ref, d_ref, acc_ref)

            @pl.when(pl.program_id(1) == nsteps - 1)
            def _():
                o_ref[...] = acc_ref[...].astype(BF16)
        return pl.pallas_call(
            body, name="mm_tn_col", grid=(N_CHIP, nsteps),
            in_specs=[pl.BlockSpec((tm, kb), lambda q, i: (i, 0)), pl.BlockSpec((tm, nb), lambda q, i: (i, q))],
            out_specs=pl.BlockSpec((None, kb, nb), lambda q, i: (q, 0, 0)), out_shape=out_shape,
            scratch_shapes=[pltpu.VMEM((kb, nb), F32)], compiler_params=_params(48))(a, d)
    tn = 512

    def body(a_ref, d_ref, o_ref, acc_ref):
        accumulate(a_ref, d_ref, acc_ref)

        @pl.when(pl.program_id(1) == nsteps - 1)
        def _():
            o_ref[...] = acc_ref[...].astype(BF16).reshape(N_CHIP, kb, tn)
    return pl.pallas_call(
        body, name="mm_tn_row", grid=(nb // tn, nsteps),
        in_specs=[pl.BlockSpec((tm, N_CHIP * kb), lambda n, i: (i, 0)), pl.BlockSpec((tm, tn), lambda n, i: (i, n))],
        out_specs=pl.BlockSpec((N_CHIP, kb, tn), lambda n, i: (0, 0, n)), out_shape=out_shape,
        scratch_shapes=[pltpu.VMEM((N_CHIP * kb, tn), F32)], compiler_params=_params(48))(a, d)


def _seg(i, T):
    return (i >= T // TMR).astype(jnp.int32)


def norm_mod_fwd(x, nw, mod, k, T):
    R, dm = x.shape

    def body(x_ref, nw_ref, mod_ref, h_ref):
        seg = _seg(pl.program_id(0), T)
        sh = mod_ref[seg, pl.ds(k, 1), :]
        sc = mod_ref[seg, pl.ds(k + 1, 1), :]
        xv = x_ref[...]
        r = lax.rsqrt(jnp.mean(xv * xv, axis=-1, keepdims=True) + EPS)
        h_ref[...] = ((xv * r * nw_ref[...]) * (1.0 + sc) + sh).astype(BF16)
    return pl.pallas_call(
        body, name="norm_mod_fwd", grid=(R // TMR,),
        in_specs=[pl.BlockSpec((TMR, dm), lambda i: (i, 0)),
                  pl.BlockSpec((1, dm), lambda i: (0, 0)),
                  pl.BlockSpec((2, 6, dm), lambda i: (0, 0, 0))],
        out_specs=pl.BlockSpec((TMR, dm), lambda i: (i, 0)),
        out_shape=jax.ShapeDtypeStruct((R, dm), BF16))(x, nw, mod)


def mm_resid_norm_fwd(a, w, x, nw, mod_g, mod_n, kg, kn, T):
    R, dm = x.shape
    _, kb, nb = w.shape

    def body(a_ref, w_ref, x_ref, nw_ref, mg_ref, mn_ref, y_ref, xo_ref, h_ref):
        seg = _seg(pl.program_id(0), T)
        yv = jnp.dot(a_ref[...].astype(BF16), w_ref[...].reshape(N_CHIP * kb, nb), preferred_element_type=F32)
        y_ref[...] = yv
        xv = x_ref[...] + mg_ref[seg, pl.ds(kg, 1), :] * yv
        xo_ref[...] = xv
        r = lax.rsqrt(jnp.mean(xv * xv, axis=-1, keepdims=True) + EPS)
        h_ref[...] = ((xv * r * nw_ref[...]) * (1.0 + mn_ref[seg, pl.ds(kn + 1, 1), :])
                      + mn_ref[seg, pl.ds(kn, 1), :]).astype(BF16)
    tile = pl.BlockSpec((TMR, dm), lambda i: (i, 0))
    modspec = pl.BlockSpec((2, 6, dm), lambda i: (0, 0, 0))
    return pl.pallas_call(
        body, name="mm_resid_norm_fwd", grid=(R // TMR,),
        in_specs=[pl.BlockSpec((TMR, N_CHIP * kb), lambda i: (i, 0)),
                  pl.BlockSpec((N_CHIP, kb, nb), lambda i: (0, 0, 0), pipeline_mode=pl.Buffered(1)),
                  tile, pl.BlockSpec((1, dm), lambda i: (0, 0)), modspec, modspec],
        out_specs=[tile, tile, tile],
        out_shape=[jax.ShapeDtypeStruct((R, dm), F32), jax.ShapeDtypeStruct((R, dm), F32),
                   jax.ShapeDtypeStruct((R, dm), BF16)],
        compiler_params=_params(48))(a, w, x, nw, mod_g, mod_n)


def mm_norm_resid_bwd(d, w, x, nw, mod_n, dxr, y, mod_g, kn, kg, T):
    R, dm = x.shape
    _, kb, nb = w.shape
    with_resid = y is not None
    contract_last = (((1,), (1,)), ((), ()))

    def body(*refs):
        if with_resid:
            d_ref, w_ref, x_ref, nw_ref, mn_ref, dxr_ref, y_ref, mg_ref, dx_ref, dy_ref, dmod_ref, dnw_ref, dg_ref = refs
        else:
            d_ref, w_ref, x_ref, nw_ref, mn_ref, dxr_ref, dx_ref, dmod_ref, dnw_ref = refs
        i = pl.program_id(0)
        seg = _seg(i, T)

        @pl.when(i == 0)
        def _():
            dmod_ref[...] = jnp.zeros_like(dmod_ref)
            dnw_ref[...] = jnp.zeros_like(dnw_ref)
            if with_resid:
                dg_ref[...] = jnp.zeros_like(dg_ref)
        dhv = None
        for q in range(N_CHIP):
            t = lax.dot_general(d_ref[:, q * nb:(q + 1) * nb].astype(BF16), w_ref[q], contract_last,
                                preferred_element_type=F32)
            dhv = t if dhv is None else dhv + t
        sc = mn_ref[seg, pl.ds(kn + 1, 1), :]
        nwv = nw_ref[...]
        xv = x_ref[...]
        r = lax.rsqrt(jnp.mean(xv * xv, axis=-1, keepdims=True) + EPS)
        xh = xv * r
        dmod_ref[seg, 0] += _colsum8(dhv)
        dmod_ref[seg, 1] += _colsum8(dhv * (xh * nwv))
        dn = dhv * (1.0 + sc)
        dnw_ref[...] += _colsum8(dn * xh)
        dxh = dn * nwv
        dx = dxr_ref[...] + r * (dxh - xh * jnp.mean(dxh * xh, axis=-1, keepdims=True))
        dx_ref[...] = dx
        if with_resid:
            dy_ref[...] = (mg_ref[seg, pl.ds(kg, 1), :] * dx).astype(BF16)
            dg_ref[seg] += _colsum8(dx * y_ref[...])
    tile = pl.BlockSpec((TMR, dm), lambda i: (i, 0))
    modspec = pl.BlockSpec((2, 6, dm), lambda i: (0, 0, 0))
    in_specs = [pl.BlockSpec((TMR, N_CHIP * nb), lambda i: (i, 0)),
                pl.BlockSpec((N_CHIP, kb, nb), lambda i: (0, 0, 0), pipeline_mode=pl.Buffered(1)),
                tile, pl.BlockSpec((1, dm), lambda i: (0, 0)), modspec, tile]
    acc_specs = [pl.BlockSpec((2, 2, 8, dm), lambda i: (0, 0, 0, 0)), pl.BlockSpec((8, dm), lambda i: (0, 0))]
    acc_shapes = [jax.ShapeDtypeStruct((2, 2, 8, dm), F32), jax.ShapeDtypeStruct((8, dm), F32)]
    if with_resid:
        return pl.pallas_call(
            body, name="mm_norm_resid_bwd", grid=(R // TMR,),
            in_specs=in_specs + [tile, modspec],
            out_specs=[tile, tile] + acc_specs + [pl.BlockSpec((2, 8, dm), lambda i: (0, 0, 0))],
            out_shape=[jax.ShapeDtypeStruct((R, dm), F32), jax.ShapeDtypeStruct((R, dm), BF16)] + acc_shapes
            + [jax.ShapeDtypeStruct((2, 8, dm), F32)],
            compiler_params=_params(48))(d, w, x, nw, mod_n, dxr, y, mod_g)
    return pl.pallas_call(
        body, name="mm_norm_bwd", grid=(R // TMR,), in_specs=in_specs,
        out_specs=[tile] + acc_specs, out_shape=[jax.ShapeDtypeStruct((R, dm), F32)] + acc_shapes,
        compiler_params=_params(48))(d, w, x, nw, mod_n, dxr)


def resid_bwd(dxn, y, mod, k, T):
    R, dm = dxn.shape

    def body(dx_ref, y_ref, mod_ref, dy_ref, dg_ref):
        i = pl.program_id(0)
        seg = _seg(i, T)

        @pl.when(i == 0)
        def _():
            dg_ref[...] = jnp.zeros_like(dg_ref)
        dxv = dx_ref[...]
        dy_ref[...] = (mod_ref[seg, pl.ds(k, 1), :] * dxv).astype(BF16)
        dg_ref[seg] += _colsum8(dxv * y_ref[...])
    tile = pl.BlockSpec((TMR, dm), lambda i: (i, 0))
    return pl.pallas_call(
        body, name="resid_bwd", grid=(R // TMR,),
        in_specs=[tile, tile, pl.BlockSpec((2, 6, dm), lambda i: (0, 0, 0))],
        out_specs=[tile, pl.BlockSpec((2, 8, dm), lambda i: (0, 0, 0))],
        out_shape=[jax.ShapeDtypeStruct((R, dm), BF16), jax.ShapeDtypeStruct((2, 8, dm), F32)])(dxn, y, mod)


def _halo_specs(width, R):
    nblk = R // HALO
    per = TMR // HALO
    return (pl.BlockSpec((HALO, width), lambda i: (jnp.maximum(i * per - 1, 0), 0)),
            pl.BlockSpec((TMR, width), lambda i: (i, 0)),
            pl.BlockSpec((HALO, width), lambda i: (jnp.minimum((i + 1) * per, nblk - 1), 0)))


def _halo_live(i, T, R):
    nl = T // TMR
    return (i != 0) & (i != nl), (i != nl - 1) & (i != R // TMR - 1)


def _ext(refs, c0, cw, live, halo=HALO):
    pref, ref, nref = refs
    before = jnp.where(live[0], pref[:, c0:c0 + cw].astype(F32)[HALO - halo:], 0.0)
    after = jnp.where(live[1], nref[:, c0:c0 + cw].astype(F32)[:halo], 0.0)
    return jnp.concatenate([before, ref[:, c0:c0 + cw].astype(F32), after], axis=0)


def _at(ext, off, halo=HALO):
    n = ext.shape[0]
    s = (-off) % n
    y = pltpu.roll(ext, s, 0) if s else ext
    return y[halo:halo + TMR]


def ffn_mid_fwd(u, wc, w_down, x, nw, mod_g, mod_n, kg, kn, T):
    R, w2 = u.shape
    dm = x.shape[1]
    _, kb, nb = w_down.shape
    cw = 256
    with_norm = nw is not None

    def body(*refs):
        if with_norm:
            up_ref, u_ref, un_ref, wc_ref, w_ref, x_ref, nw_ref, mg_ref, mn_ref, z_ref, y_ref, xo_ref, h_ref = refs
        else:
            up_ref, u_ref, un_ref, wc_ref, w_ref, x_ref, mg_ref, z_ref, y_ref, xo_ref = refs
        i = pl.program_id(0)
        seg = _seg(i, T)
        live = _halo_live(i, T, R)

        def conv(c0):
            e = _ext((up_ref, u_ref, un_ref), c0, cw, live, 8)
            return (wc_ref[pl.ds(0, 1), c0:c0 + cw] * _at(e, -1, 8) + wc_ref[pl.ds(1, 1), c0:c0 + cw] * _at(e, 0, 8)
                    + wc_ref[pl.ds(2, 1), c0:c0 + cw] * _at(e, 1, 8))
        yv = None
        for j in range(D_FF // cw):
            a = conv(j * cw)
            g = conv(D_FF + j * cw)
            zc = (g * _sigmoid(g) * a).astype(BF16)
            z_ref[:, j * cw:(j + 1) * cw] = zc
            t = jnp.dot(zc, w_ref[j * cw:(j + 1) * cw, :], preferred_element_type=F32)
            yv = t if yv is None else yv + t
        y_ref[...] = yv
        xv = x_ref[...] + mg_ref[seg, pl.ds(kg, 1), :] * yv
        xo_ref[...] = xv
        if with_norm:
            r = lax.rsqrt(jnp.mean(xv * xv, axis=-1, keepdims=True) + EPS)
            h_ref[...] = ((xv * r * nw_ref[...]) * (1.0 + mn_ref[seg, pl.ds(kn + 1, 1), :])
                          + mn_ref[seg, pl.ds(kn, 1), :]).astype(BF16)
    tile = pl.BlockSpec((TMR, dm), lambda i: (i, 0))
    modspec = pl.BlockSpec((2, 6, dm), lambda i: (0, 0, 0))
    w_down = w_down.reshape(N_CHIP * kb, nb)
    in_specs = [*_halo_specs(w2, R), pl.BlockSpec((3, w2), lambda i: (0, 0)),
                pl.BlockSpec((N_CHIP * kb, nb), lambda i: (0, 0), pipeline_mode=pl.Buffered(1)), tile]
    out_specs = [pl.BlockSpec((TMR, D_FF), lambda i: (i, 0)), tile, tile]
    out_shape = [jax.ShapeDtypeStruct((R, D_FF), BF16), jax.ShapeDtypeStruct((R, dm), F32),
                 jax.ShapeDtypeStruct((R, dm), F32)]
    if with_norm:
        return pl.pallas_call(
            body, name="ffn_mid_fwd", grid=(R // TMR,),
            in_specs=in_specs + [pl.BlockSpec((1, dm), lambda i: (0, 0)), modspec, modspec],
            out_specs=out_specs + [tile], out_shape=out_shape + [jax.ShapeDtypeStruct((R, dm), BF16)],
            compiler_params=_params(48))(u, u, u, wc, w_down, x, nw, mod_g, mod_n)
    return pl.pallas_call(
        body, name="ffn_mid_fwd_last", grid=(R // TMR,), in_specs=in_specs + [modspec],
        out_specs=out_specs, out_shape=out_shape, compiler_params=_params(48))(u, u, u, wc, w_down, x, mod_g)


def ffn_mid_bwd(dz, u, wc, T):
    R, w2 = u.shape
    cw = 256

    def body(dz_ref, up_ref, u_ref, un_ref, wc_ref, duc_ref, dwc_ref):
        i = pl.program_id(0)
        live = _halo_live(i, T, R)

        @pl.when(i == 0)
        def _():
            dwc_ref[...] = jnp.zeros_like(dwc_ref)

        def taps(c0):
            e = _ext((up_ref, u_ref, un_ref), c0, cw, live, 8)
            return [_at(e, -1, 8), _at(e, 0, 8), _at(e, 1, 8)]

        def conv(t, c0):
            return (wc_ref[pl.ds(0, 1), c0:c0 + cw] * t[0] + wc_ref[pl.ds(1, 1), c0:c0 + cw] * t[1]
                    + wc_ref[pl.ds(2, 1), c0:c0 + cw] * t[2])
        for j in range(D_FF // cw):
            ca, cg = j * cw, D_FF + j * cw
            dzv = dz_ref[:, ca:ca + cw].astype(F32)
            ta, tg = taps(ca), taps(cg)
            a, g = conv(ta, ca), conv(tg, cg)
            sg = _sigmoid(g)
            da = dzv * (g * sg)
            dg = dzv * a * (sg * (1.0 + g * (1.0 - sg)))
            duc_ref[:, ca:ca + cw] = da.astype(BF16)
            duc_ref[:, cg:cg + cw] = dg.astype(BF16)
            for k in range(3):
                dwc_ref[k, :, ca:ca + cw] += _colsum8(da * ta[k])
                dwc_ref[k, :, cg:cg + cw] += _colsum8(dg * tg[k])
    return pl.pallas_call(
        body, name="ffn_mid_bwd", grid=(R // TMR,),
        in_specs=[pl.BlockSpec((TMR, D_FF), lambda i: (i, 0)), *_halo_specs(w2, R),
                  pl.BlockSpec((3, w2), lambda i: (0, 0))],
        out_specs=[pl.BlockSpec((TMR, w2), lambda i: (i, 0)), pl.BlockSpec((3, 8, w2), lambda i: (0, 0, 0))],
        out_shape=[jax.ShapeDtypeStruct((R, w2), BF16), jax.ShapeDtypeStruct((3, 8, w2), F32)],
        compiler_params=_params(48))(dz, u, u, u, wc)


def ffn_up_bwd(duc, wc, w_up, x, nw, mod_n, dxr, y, mod_g, kn, kg, T):
    R, w2 = duc.shape
    dm = x.shape[1]
    _, kb, nb = w_up.shape
    cw = 128
    contract_last = (((1,), (1,)), ((), ()))

    def body(dp_ref, d_ref, dn_ref, wc_ref, w_ref, x_ref, nw_ref, mn_ref, dxr_ref, y_ref, mg_ref,
             du_ref, dx_ref, dy_ref, dmod_ref, dnw_ref, dg_ref):
        i = pl.program_id(0)
        seg = _seg(i, T)
        live = _halo_live(i, T, R)

        @pl.when(i == 0)
        def _():
            dmod_ref[...] = jnp.zeros_like(dmod_ref)
            dnw_ref[...] = jnp.zeros_like(dnw_ref)
            dg_ref[...] = jnp.zeros_like(dg_ref)
        dhv = None
        for q in range(N_CHIP):
            for j in range(nb // cw):
                c0 = q * nb + j * cw
                e = _ext((dp_ref, d_ref, dn_ref), c0, cw, live, 8)
                du_ref[:, c0:c0 + cw] = (wc_ref[pl.ds(0, 1), c0:c0 + cw] * _at(e, 1, 8)
                                         + wc_ref[pl.ds(1, 1), c0:c0 + cw] * _at(e, 0, 8)
                                         + wc_ref[pl.ds(2, 1), c0:c0 + cw] * _at(e, -1, 8)).astype(BF16)
            t = lax.dot_general(du_ref[:, q * nb:(q + 1) * nb], w_ref[q], contract_last,
                                preferred_element_type=F32)
            dhv = t if dhv is None else dhv + t
        sc = mn_ref[seg, pl.ds(kn + 1, 1), :]
        nwv = nw_ref[...]
        xv = x_ref[...]
        r = lax.rsqrt(jnp.mean(xv * xv, axis=-1, keepdims=True) + EPS)
        xh = xv * r
        dmod_ref[seg, 0] += _colsum8(dhv)
        dmod_ref[seg, 1] += _colsum8(dhv * (xh * nwv))
        dn = dhv * (1.0 + sc)
        dnw_ref[...] += _colsum8(dn * xh)
        dxh = dn * nwv
        dx = dxr_ref[...] + r * (dxh - xh * jnp.mean(dxh * xh, axis=-1, keepdims=True))
        dx_ref[...] = dx
        dy_ref[...] = (mg_ref[seg, pl.ds(kg, 1), :] * dx).astype(BF16)
        dg_ref[seg] += _colsum8(dx * y_ref[...])
    tile = pl.BlockSpec((TMR, dm), lambda i: (i, 0))
    modspec = pl.BlockSpec((2, 6, dm), lambda i: (0, 0, 0))
    return pl.pallas_call(
        body, name="ffn_up_bwd", grid=(R // TMR,),
        in_specs=[*_halo_specs(w2, R), pl.BlockSpec((3, w2), lambda i: (0, 0)),
                  pl.BlockSpec((N_CHIP, kb, nb), lambda i: (0, 0, 0), pipeline_mode=pl.Buffered(1)),
                  tile, pl.BlockSpec((1, dm), lambda i: (0, 0)), modspec, tile, tile, modspec],
        out_specs=[pl.BlockSpec((TMR, w2), lambda i: (i, 0)), tile, tile,
                   pl.BlockSpec((2, 2, 8, dm), lambda i: (0, 0, 0, 0)), pl.BlockSpec((8, dm), lambda i: (0, 0)),
                   pl.BlockSpec((2, 8, dm), lambda i: (0, 0, 0))],
        out_shape=[jax.ShapeDtypeStruct((R, w2), BF16), jax.ShapeDtypeStruct((R, dm), F32),
                   jax.ShapeDtypeStruct((R, dm), BF16), jax.ShapeDtypeStruct((2, 2, 8, dm), F32),
                   jax.ShapeDtypeStruct((8, dm), F32), jax.ShapeDtypeStruct((2, 8, dm), F32)],
        compiler_params=_params(48))(duc, duc, duc, wc, w_up, x, nw, mod_n, dxr, y, mod_g)


_CW = 128


def _mixer_a(prefs, wa_ref, live):
    cin = _ext(prefs, A_W, A_W, live) * _ext(prefs, 2 * A_W, A_W, live)
    ca = (wa_ref[pl.ds(0, 1), :] * _at(cin, -1) + wa_ref[pl.ds(1, 1), :] * _at(cin, 0)
          + wa_ref[pl.ds(2, 1), :] * _at(cin, 1))
    return cin, ca


def _mixer_b(prefs, wb_ref, bias_ref, live, ub_s, ub2_s):
    for cc in range(A_W // _CW):
        c0 = cc * _CW
        ub = _ext(prefs, 3 * A_W + c0, _CW, live) * _sigmoid(_ext(prefs, 4 * A_W + c0, _CW, live))
        ub_s[:, c0:c0 + _CW] = ub
        acc = jnp.zeros((TMR, _CW), F32) + bias_ref[:, c0:c0 + _CW]
        for k in range(B_CONV):
            acc = acc + wb_ref[pl.ds(k, 1), c0:c0 + _CW] * _at(ub, k - B_CONV // 2)
        ub2_s[:, c0:c0 + _CW] = acc


def _layernorm_stats(v):
    mu = jnp.mean(v, axis=-1, keepdims=True)
    xc = v - mu
    rs = lax.rsqrt(jnp.mean(xc * xc, axis=-1, keepdims=True) + EPS)
    return xc * rs, rs


def mixer_fwd(p, wa, wb, bias, lng, lnb, w_out, x, nw, mod, kg, kn, T):
    R, wp = p.shape
    dm = x.shape[1]
    _, kb, nb = w_out.shape

    def body(pp_ref, p_ref, pn_ref, wa_ref, wb_ref, bias_ref, lng_ref, lnb_ref, w_ref, x_ref, nw_ref, mod_ref,
             o_ref, y_ref, xo_ref, h_ref, ub_s, ub2_s):
        i = pl.program_id(0)
        seg = _seg(i, T)
        live = _halo_live(i, T, R)
        prefs = (pp_ref, p_ref, pn_ref)
        _, ca = _mixer_a(prefs, wa_ref, live)
        ya = (p_ref[:, 0:A_W].astype(F32) * ca).astype(BF16)
        o_ref[:, 0:A_W] = ya
        yv = jnp.dot(ya, w_ref[0:A_W, :], preferred_element_type=F32)
        _mixer_b(prefs, wb_ref, bias_ref, live, ub_s, ub2_s)
        xh, _ = _layernorm_stats(ub2_s[...])
        lv = xh * lng_ref[...] + lnb_ref[...]
        yb = (lv * _sigmoid(lv)).astype(BF16)
        o_ref[:, A_W:2 * A_W] = yb
        yv = yv + jnp.dot(yb, w_ref[A_W:2 * A_W, :], preferred_element_type=F32)
        y_ref[...] = yv
        xv = x_ref[...] + mod_ref[seg, pl.ds(kg, 1), :] * yv
        xo_ref[...] = xv
        r = lax.rsqrt(jnp.mean(xv * xv, axis=-1, keepdims=True) + EPS)
        h_ref[...] = ((xv * r * nw_ref[...]) * (1.0 + mod_ref[seg, pl.ds(kn + 1, 1), :])
                      + mod_ref[seg, pl.ds(kn, 1), :]).astype(BF16)
    vec = pl.BlockSpec((1, A_W), lambda i: (0, 0))
    tile = pl.BlockSpec((TMR, dm), lambda i: (i, 0))
    return pl.pallas_call(
        body, name="mixer_fwd", grid=(R // TMR,),
        in_specs=[*_halo_specs(wp, R), pl.BlockSpec((3, A_W), lambda i: (0, 0)),
                  pl.BlockSpec((B_CONV, A_W), lambda i: (0, 0)), vec, vec, vec,
                  pl.BlockSpec((N_CHIP * kb, nb), lambda i: (0, 0), pipeline_mode=pl.Buffered(1)),
                  tile, pl.BlockSpec((1, dm), lambda i: (0, 0)), pl.BlockSpec((2, 6, dm), lambda i: (0, 0, 0))],
        out_specs=[pl.BlockSpec((TMR, 2 * A_W), lambda i: (i, 0)), tile, tile, tile],
        out_shape=[jax.ShapeDtypeStruct((R, 2 * A_W), BF16), jax.ShapeDtypeStruct((R, dm), F32),
                   jax.ShapeDtypeStruct((R, dm), F32), jax.ShapeDtypeStruct((R, dm), BF16)],
        scratch_shapes=[pltpu.VMEM((TMR + 2 * HALO, A_W), F32), pltpu.VMEM((TMR, A_W), F32)],
        compiler_params=_params(48))(p, p, p, wa, wb, bias, lng, lnb, w_out.reshape(N_CHIP * kb, nb), x, nw, mod)


def convmix_bwd1(dyab, p, wa, wb, bias, lng, lnb, T):
    R, wp = p.shape

    def body(dy_ref, pp_ref, p_ref, pn_ref, wa_ref, wb_ref, bias_ref, lng_ref, lnb_ref,
             dmid_ref, dwa_ref, dwb_ref, dvec_ref, ub_s, ub2_s):
        i = pl.program_id(0)
        live = _halo_live(i, T, R)

        @pl.when(i == 0)
        def _():
            dwa_ref[...] = jnp.zeros_like(dwa_ref)
            dwb_ref[...] = jnp.zeros_like(dwb_ref)
            dvec_ref[...] = jnp.zeros_like(dvec_ref)
        prefs = (pp_ref, p_ref, pn_ref)
        cin, ca = _mixer_a(prefs, wa_ref, live)
        dya = dy_ref[:, 0:A_W]
        dmid_ref[:, 0:A_W] = dya * ca
        dca = dya * p_ref[:, 0:A_W].astype(F32)
        dmid_ref[:, A_W:2 * A_W] = dca
        for k in range(3):
            dwa_ref[k] += _colsum8(dca * _at(cin, k - 1))
        _mixer_b(prefs, wb_ref, bias_ref, live, ub_s, ub2_s)
        xh, rs = _layernorm_stats(ub2_s[...])
        gain = lng_ref[...]
        lv = xh * gain + lnb_ref[...]
        sl = _sigmoid(lv)
        dl = dy_ref[:, A_W:2 * A_W] * (sl * (1.0 + lv * (1.0 - sl)))
        dvec_ref[1] += _colsum8(dl * xh)
        dvec_ref[2] += _colsum8(dl)
        dxh = dl * gain
        dub2 = rs * (dxh - jnp.mean(dxh, axis=-1, keepdims=True)
                     - xh * jnp.mean(dxh * xh, axis=-1, keepdims=True))
        dvec_ref[0] += _colsum8(dub2)
        dmid_ref[:, 2 * A_W:3 * A_W] = dub2
        for cc in range(A_W // _CW):
            c0 = cc * _CW
            ub = ub_s[:, c0:c0 + _CW]
            d = dmid_ref[:, 2 * A_W + c0:2 * A_W + c0 + _CW]
            for k in range(B_CONV):
                dwb_ref[k, :, c0:c0 + _CW] += _colsum8(d * _at(ub, k - B_CONV // 2))
    vec = pl.BlockSpec((1, A_W), lambda i: (0, 0))
    return pl.pallas_call(
        body, name="convmix_bwd1", grid=(R // TMR,),
        in_specs=[pl.BlockSpec((TMR, 2 * A_W), lambda i: (i, 0)), *_halo_specs(wp, R),
                  pl.BlockSpec((3, A_W), lambda i: (0, 0)), pl.BlockSpec((B_CONV, A_W), lambda i: (0, 0)),
                  vec, vec, vec],
        out_specs=[pl.BlockSpec((TMR, 3 * A_W), lambda i: (i, 0)),
                   pl.BlockSpec((3, 8, A_W), lambda i: (0, 0, 0)),
                   pl.BlockSpec((B_CONV, 8, A_W), lambda i: (0, 0, 0)),
                   pl.BlockSpec((3, 8, A_W), lambda i: (0, 0, 0))],
        out_shape=[jax.ShapeDtypeStruct((R, 3 * A_W), F32), jax.ShapeDtypeStruct((3, 8, A_W), F32),
                   jax.ShapeDtypeStruct((B_CONV, 8, A_W), F32), jax.ShapeDtypeStruct((3, 8, A_W), F32)],
        scratch_shapes=[pltpu.VMEM((TMR + 2 * HALO, A_W), F32), pltpu.VMEM((TMR, A_W), F32)],
        compiler_params=_params(48))(dyab, p, p, p, wa, wb, bias, lng, lnb)


def mixer_in_bwd(dmid, p, wa, wb, w_in, x, nw, mod_n, dxr, y, mod_g, kn, kg, T):
    R, wp = p.shape
    dm = x.shape[1]
    _, kb, nb = w_in.shape
    with_resid = y is not None
    contract_last = (((1,), (1,)), ((), ()))

    def body(*refs):
        if with_resid:
            (mp_ref, m_ref, mn_ref, p_ref, wa_ref, wb_ref, w_ref, x_ref, nw_ref, mnorm_ref, dxr_ref, y_ref, mg_ref,
             dp_ref, dx_ref, dy_ref, dmod_ref, dnw_ref, dg_ref) = refs
        else:
            (mp_ref, m_ref, mn_ref, p_ref, wa_ref, wb_ref, w_ref, x_ref, nw_ref, mnorm_ref, dxr_ref,
             dp_ref, dx_ref, dmod_ref, dnw_ref) = refs
        i = pl.program_id(0)
        seg = _seg(i, T)
        live = _halo_live(i, T, R)

        @pl.when(i == 0)
        def _():
            dmod_ref[...] = jnp.zeros_like(dmod_ref)
            dnw_ref[...] = jnp.zeros_like(dnw_ref)
            if with_resid:
                dg_ref[...] = jnp.zeros_like(dg_ref)

        def block(q):
            return lax.dot_general(dp_ref[:, q * nb:(q + 1) * nb], w_ref[q], contract_last,
                                   preferred_element_type=F32)
        mrefs = (mp_ref, m_ref, mn_ref)
        dp_ref[:, 0:A_W] = m_ref[:, 0:A_W].astype(BF16)
        dca = _ext(mrefs, A_W, A_W, live)
        dcin = (wa_ref[pl.ds(0, 1), :] * _at(dca, 1) + wa_ref[pl.ds(1, 1), :] * _at(dca, 0)
                + wa_ref[pl.ds(2, 1), :] * _at(dca, -1))
        dp_ref[:, A_W:2 * A_W] = (dcin * p_ref[:, 2 * A_W:3 * A_W].astype(F32)).astype(BF16)
        dp_ref[:, 2 * A_W:3 * A_W] = (dcin * p_ref[:, A_W:2 * A_W].astype(F32)).astype(BF16)
        dhv = block(0) + block(1)
        for cc in range(A_W // _CW):
            c0 = cc * _CW
            d = _ext(mrefs, 2 * A_W + c0, _CW, live)
            dub = jnp.zeros((TMR, _CW), F32)
            for k in range(B_CONV):
                dub = dub + wb_ref[pl.ds(k, 1), c0:c0 + _CW] * _at(d, B_CONV // 2 - k)
            vb = p_ref[:, 3 * A_W + c0:3 * A_W + c0 + _CW].astype(F32)
            s = _sigmoid(p_ref[:, 4 * A_W + c0:4 * A_W + c0 + _CW].astype(F32))
            dp_ref[:, 3 * A_W + c0:3 * A_W + c0 + _CW] = (dub * s).astype(BF16)
            dp_ref[:, 4 * A_W + c0:4 * A_W + c0 + _CW] = (dub * vb * s * (1.0 - s)).astype(BF16)
        dhv = dhv + block(2) + block(3)
        sc = mnorm_ref[seg, pl.ds(kn + 1, 1), :]
        nwv = nw_ref[...]
        xv = x_ref[...]
        r = lax.rsqrt(jnp.mean(xv * xv, axis=-1, keepdims=True) + EPS)
        xh = xv * r
        dmod_ref[seg, 0] += _colsum8(dhv)
        dmod_ref[seg, 1] += _colsum8(dhv * (xh * nwv))
        dn = dhv * (1.0 + sc)
        dnw_ref[...] += _colsum8(dn * xh)
        dxh = dn * nwv
        dx = dxr_ref[...] + r * (dxh - xh * jnp.mean(dxh * xh, axis=-1, keepdims=True))
        dx_ref[...] = dx
        if with_resid:
            dy_ref[...] = (mg_ref[seg, pl.ds(kg, 1), :] * dx).astype(BF16)
            dg_ref[seg] += _colsum8(dx * y_ref[...])
    assert 2 * nb <= 3 * A_W and N_CHIP * nb == wp
    tile = pl.BlockSpec((TMR, dm), lambda i: (i, 0))
    modspec = pl.BlockSpec((2, 6, dm), lambda i: (0, 0, 0))
    in_specs = [*_halo_specs(3 * A_W, R), pl.BlockSpec((TMR, wp), lambda i: (i, 0)),
                pl.BlockSpec((3, A_W), lambda i: (0, 0)), pl.BlockSpec((B_CONV, A_W), lambda i: (0, 0)),
                pl.BlockSpec((N_CHIP, kb, nb), lambda i: (0, 0, 0), pipeline_mode=pl.Buffered(1)),
                tile, pl.BlockSpec((1, dm), lambda i: (0, 0)), modspec, tile]
    dp_spec = pl.BlockSpec((TMR, wp), lambda i: (i, 0))
    acc_specs = [pl.BlockSpec((2, 2, 8, dm), lambda i: (0, 0, 0, 0)), pl.BlockSpec((8, dm), lambda i: (0, 0))]
    acc_shapes = [jax.ShapeDtypeStruct((2, 2, 8, dm), F32), jax.ShapeDtypeStruct((8, dm), F32)]
    dp_shape, dx_shape = jax.ShapeDtypeStruct((R, wp), BF16), jax.ShapeDtypeStruct((R, dm), F32)
    if with_resid:
        return pl.pallas_call(
            body, name="mixer_in_bwd", grid=(R // TMR,), in_specs=in_specs + [tile, modspec],
            out_specs=[dp_spec, tile, tile] + acc_specs + [pl.BlockSpec((2, 8, dm), lambda i: (0, 0, 0))],
            out_shape=[dp_shape, dx_shape, jax.ShapeDtypeStruct((R, dm), BF16)] + acc_shapes
            + [jax.ShapeDtypeStruct((2, 8, dm), F32)],
            compiler_params=_params(48))(dmid, dmid, dmid, p, wa, wb, w_in, x, nw, mod_n, dxr, y, mod_g)
    return pl.pallas_call(
        body, name="mixer_in_bwd_first", grid=(R // TMR,), in_specs=in_specs,
        out_specs=[dp_spec, tile] + acc_specs, out_shape=[dp_shape, dx_shape] + acc_shapes,
        compiler_params=_params(48))(dmid, dmid, dmid, p, wa, wb, w_in, x, nw, mod_n, dxr)


def _rot_half(v):
    w = v.shape[-1]
    lane = lax.broadcasted_iota(jnp.int32, (1, w), 1)
    return jnp.where(lane % HEAD_DIM < HEAD_DIM // 2, pltpu.roll(v, w - HEAD_DIM // 2, 1),
                     pltpu.roll(v, HEAD_DIM // 2, 1))


def mm_qkv_rope(a, w, cs, sn):
    R = a.shape[0]
    _, kb, nb = w.shape
    wq = N_CHIP * nb
    tm = _row_tile(R)
    qw = N_HEADS * HEAD_DIM
    kw = (wq - qw) // 2
    scale = HEAD_DIM ** -0.5

    def body(a_ref, w_ref, cs_ref, sn_ref, o_ref, x_ref):
        av = a_ref[...].astype(BF16)
        for q in range(N_CHIP):
            x_ref[:, q * nb:(q + 1) * nb] = jnp.dot(av, w_ref[q], preferred_element_type=F32)
        c, s = cs_ref[...], sn_ref[...]
        q = x_ref[:, 0:qw]
        o_ref[:, 0:qw] = ((q * jnp.tile(c, (1, qw // 128)) + _rot_half(q) * jnp.tile(s, (1, qw // 128)))
                          * scale).astype(BF16)
        k = x_ref[:, qw:qw + kw]
        o_ref[:, qw:qw + kw] = (k * jnp.tile(c, (1, kw // 128))
                                + _rot_half(k) * jnp.tile(s, (1, kw // 128))).astype(BF16)
        o_ref[:, qw + kw:] = x_ref[:, qw + kw:].astype(BF16)
    tab = pl.BlockSpec((tm, 128), lambda i: (i, 0))
    return pl.pallas_call(
        body, name="mm_qkv_rope", grid=(R // tm,),
        in_specs=[pl.BlockSpec((tm, kb), lambda i: (i, 0)),
                  pl.BlockSpec((N_CHIP, kb, nb), lambda i: (0, 0, 0), pipeline_mode=pl.Buffered(1)), tab, tab],
        out_specs=pl.BlockSpec((tm, wq), lambda i: (i, 0)),
        out_shape=jax.ShapeDtypeStruct((R, wq), BF16), scratch_shapes=[pltpu.VMEM((tm, wq), F32)],
        compiler_params=_params(48))(a, w, cs, sn)


def rope_bwd(dq, dks, dvs, dkc, dvc, cs, sn, T):
    R, qw = dq.shape
    kw = dkc.shape[1]
    nb = R // QB
    nl = T // QB
    scale = HEAD_DIM ** -0.5

    def body(dq_ref, kp_ref, ko_ref, kn_ref, vp_ref, vo_ref, vn_ref, kc_ref, vc_ref, cs_ref, sn_ref, o_ref):
        b = pl.program_id(0)
        c, s = cs_ref[...], sn_ref[...]
        has_next = (b + 1 < nb).astype(F32)
        has_prev = (b >= 1).astype(F32)
        is_ctx = (b >= nl).astype(F32)
        g = dq_ref[...] * scale
        o_ref[:, 0:qw] = (g * jnp.tile(c, (1, qw // 128)) + _rot_half(g * jnp.tile(s, (1, qw // 128)))).astype(BF16)
        g = ko_ref[...] + kp_ref[...] * has_next + kn_ref[...] * has_prev + kc_ref[...] * is_ctx
        o_ref[:, qw:qw + kw] = (g * jnp.tile(c, (1, kw // 128))
                                + _rot_half(g * jnp.tile(s, (1, kw // 128)))).astype(BF16)
        o_ref[:, qw + kw:] = (vo_ref[...] + vp_ref[...] * has_next + vn_ref[...] * has_prev
                              + vc_ref[...] * is_ctx).astype(BF16)
    own = pl.BlockSpec((QB, kw), lambda b: (b, 0))
    from_next = pl.BlockSpec((QB, kw), lambda b: (jnp.minimum(b + 1, nb - 1), 0))
    from_prev = pl.BlockSpec((QB, kw), lambda b: (jnp.maximum(b - 1, 0), 0))
    ctx = pl.BlockSpec((QB, kw), lambda b: (jnp.maximum(b - nl, 0), 0))
    tab = pl.BlockSpec((QB, 128), lambda b: (b, 0))
    return pl.pallas_call(
        body, name="rope_bwd", grid=(nb,),
        in_specs=[pl.BlockSpec((QB, qw), lambda b: (b, 0)), from_next, own, from_prev, from_next, own, from_prev,
                  ctx, ctx, tab, tab],
        out_specs=pl.BlockSpec((QB, qw + 2 * kw), lambda b: (b, 0)),
        out_shape=jax.ShapeDtypeStruct((R, qw + 2 * kw), BF16))(
            dq, dks[0], dks[1], dks[2], dvs[0], dvs[1], dvs[2], dkc, dvc, cs, sn)


def _attn_specs(T, R):
    nl = T // QB
    qcols = N_HEADS * HEAD_DIM // 128
    kcols = 2

    def band(col0, shift):
        return pl.BlockSpec((QB, 128), lambda jj, b: (jnp.clip(b + shift, 0, nl - 1), col0 + jj))

    def ctx(col0):
        return pl.BlockSpec((R - T, 128), lambda jj, b: (T // (R - T), col0 + jj))
    q = pl.BlockSpec((QB, 512), lambda jj, b: (b, jj))
    k0, v0 = qcols, qcols + kcols
    return q, [band(k0, -1), band(k0, 0), band(k0, 1), ctx(k0)], [band(v0, -1), band(v0, 0), band(v0, 1), ctx(v0)]


def _attn_common(T, R):
    nl = T // QB
    nk = 3 * QB + (R - T)

    def low_lanes():
        return lax.broadcasted_iota(jnp.int32, (1, 128), 1) < HEAD_DIM

    def dup(v, par):
        low = low_lanes()
        vf = v.astype(F32)
        r = pltpu.roll(vf, HEAD_DIM, 1)
        return (jnp.where(low, vf, r) if par == 0 else jnp.where(low, r, vf)).astype(BF16)

    def stack(ref, par):
        low = low_lanes()
        pa = ref[:, (2 * par) * 128:(2 * par + 1) * 128].astype(BF16)
        pb = ref[:, (2 * par + 1) * 128:(2 * par + 2) * 128].astype(BF16)
        zero = jnp.zeros_like(pa)
        return jnp.concatenate([jnp.where(low, pa, zero), jnp.where(low, zero, pa),
                                jnp.where(low, pb, zero), jnp.where(low, zero, pb)], axis=0)

    def unstack(v):
        low = low_lanes()
        return (jnp.where(low, v[0:QB], v[QB:2 * QB]), jnp.where(low, v[2 * QB:3 * QB], v[3 * QB:4 * QB]))

    def mask_of(b):
        col = lax.broadcasted_iota(jnp.int32, (1, nk), 1)
        gone = (((col < QB) & (b == 0)) | ((col >= 2 * QB) & (col < 3 * QB) & (b == nl - 1))
                | ((col < 3 * QB) & (b >= nl)))
        return jnp.where(gone, NEG_INF, 0.0)

    def sink_col(sink_ref, first):
        blk = lax.broadcasted_iota(jnp.int32, (4 * QB, 1), 0) // QB
        out = jnp.zeros((4 * QB, 1), F32) + sink_ref[first]
        for h in range(1, 4):
            out = jnp.where(blk == h, sink_ref[first + h], out)
        return out

    def scores(qs, kd, mask, sink):
        s = lax.dot_general(qs, kd, (((1,), (1,)), ((), ())), preferred_element_type=F32) + mask
        m = jnp.maximum(jnp.max(s, axis=-1, keepdims=True), sink)
        e = jnp.exp(s - m)
        es = jnp.exp(sink - m)
        return e, es, 1.0 / (jnp.sum(e, axis=-1, keepdims=True) + es)
    return low_lanes, dup, stack, unstack, mask_of, sink_col, scores


def window_bias(T, R):
    nk = 3 * QB + (R - T)
    row = jnp.arange(QB)[:, None]
    col = jnp.arange(nk)[None, :]
    near = (jnp.abs(col - QB - row) <= WINDOW) | (col >= 3 * QB)
    return jnp.tile(jnp.where(near, 0.0, NEG_INF).astype(F32), (4, 1))


def _probs_spec(nk):
    return pl.BlockSpec((None, None, 2, 4 * QB, nk + 128), lambda jj, b: (jj, b, 0, 0, 0))


def attn_fwd(qkvr, sinks, bias, T):
    R = qkvr.shape[0]
    nk = bias.shape[1]
    qspec, kspecs, vspecs = _attn_specs(T, R)
    _, dup, stack, unstack, mask_of, sink_col, scores = _attn_common(T, R)

    def body(q_ref, kp, ko, kn, kc, vp, vo, vn, vc, sink_ref, bias_ref, o_ref, p_ref):
        jj, b = pl.program_id(0), pl.program_id(1)
        mask = bias_ref[...] + mask_of(b)
        k_all = jnp.concatenate([kp[...], ko[...], kn[...], kc[...]], axis=0)
        v_all = jnp.concatenate([vp[...], vo[...], vn[...], vc[...]], axis=0)
        for par in range(2):
            kd, vd = dup(k_all, par), dup(v_all, par)
            e, es, rz = scores(stack(q_ref, par), kd, mask, sink_col(sink_ref, jj * 8 + par * 4))
            p = (e * rz).astype(BF16)
            p_ref[par, :, 0:nk] = p
            p_ref[par, :, nk:nk + 128] = jnp.broadcast_to(es * rz, (4 * QB, 128)).astype(BF16)
            o = jnp.dot(p, vd, preferred_element_type=F32)
            pa, pb = unstack(o)
            o_ref[:, (2 * par) * 128:(2 * par + 1) * 128] = pa.astype(BF16)
            o_ref[:, (2 * par + 1) * 128:(2 * par + 2) * 128] = pb.astype(BF16)
    return pl.pallas_call(
        body, name="attn_fwd", grid=(2, R // QB),
        in_specs=[qspec, *kspecs, *vspecs, pl.BlockSpec(memory_space=pltpu.SMEM),
                  pl.BlockSpec(bias.shape, lambda jj, b: (0, 0))],
        out_specs=[pl.BlockSpec((QB, 512), lambda jj, b: (b, jj)), _probs_spec(nk)],
        out_shape=[jax.ShapeDtypeStruct((R, N_HEADS * HEAD_DIM), BF16),
                   jax.ShapeDtypeStruct((2, R // QB, 2, 4 * QB, nk + 128), BF16)],
        compiler_params=_params(48))(qkvr, *([qkvr] * 8), sinks, bias)


def attn_bwd(qkvr, do, probs, T):
    R = qkvr.shape[0]
    tc = R - T
    nk = probs.shape[-1] - 128
    qspec, kspecs, vspecs = _attn_specs(T, R)
    _, dup, stack, unstack, _, _, _ = _attn_common(T, R)
    contract_rows = (((0,), (0,)), ((), ()))
    contract_last = (((1,), (1,)), ((), ()))

    def body(q_ref, kp, ko, kn, kc, vp, vo, vn, vc, do_ref, p_ref,
             dq_ref, dkp, dko, dkn, dvp, dvo, dvn, dkc_ref, dvc_ref, dsink_ref):
        jj, b = pl.program_id(0), pl.program_id(1)

        @pl.when((jj == 0) & (b == 0))
        def _():
            dsink_ref[...] = jnp.zeros_like(dsink_ref)

        @pl.when(b == 0)
        def _():
            dkc_ref[...] = jnp.zeros_like(dkc_ref)
            dvc_ref[...] = jnp.zeros_like(dvc_ref)
        k_all = jnp.concatenate([kp[...], ko[...], kn[...], kc[...]], axis=0)
        v_all = jnp.concatenate([vp[...], vo[...], vn[...], vc[...]], axis=0)
        lane = lax.broadcasted_iota(jnp.int32, (8, 128), 1)
        srow = lax.broadcasted_iota(jnp.int32, (8, 128), 0)
        dk_fold, dv_fold = [], []
        for par in range(2):
            kd, vd = dup(k_all, par), dup(v_all, par)
            first = jj * 8 + par * 4
            qs, dos = stack(q_ref, par), stack(do_ref, par)
            p16 = p_ref[par, :, 0:nk]
            p = p16.astype(F32)
            ps = jnp.max(p_ref[par, :, nk:nk + 128].astype(F32), axis=-1, keepdims=True)
            dp = lax.dot_general(dos, vd, contract_last, preferred_element_type=F32)
            delta = jnp.sum(p * dp, axis=-1, keepdims=True)
            ds = (p * (dp - delta)).astype(BF16)
            t = ps * delta
            for h in range(4):
                dsink = -jnp.sum(t[h * QB:(h + 1) * QB])
                dsink_ref[...] += jnp.where((lane == first + h) & (srow == 0), dsink, 0.0)
            pa, pb = unstack(jnp.dot(ds, kd, preferred_element_type=F32))
            dq_ref[:, (2 * par) * 128:(2 * par + 1) * 128] = pa
            dq_ref[:, (2 * par + 1) * 128:(2 * par + 2) * 128] = pb
            dk_t = lax.dot_general(qs, ds, contract_rows, preferred_element_type=F32)
            dv_t = lax.dot_general(dos, p16, contract_rows, preferred_element_type=F32)
            dk_fold.append(dk_t + pltpu.roll(dk_t, HEAD_DIM, 0))
            dv_fold.append(dv_t + pltpu.roll(dv_t, HEAD_DIM, 0))
        low_rows = lax.broadcasted_iota(jnp.int32, (128, 1), 0) < HEAD_DIM
        dk = jnp.where(low_rows, dk_fold[0], dk_fold[1]).T
        dv = jnp.where(low_rows, dv_fold[0], dv_fold[1]).T
        dkp[...], dko[...], dkn[...] = dk[0:QB], dk[QB:2 * QB], dk[2 * QB:3 * QB]
        dvp[...], dvo[...], dvn[...] = dv[0:QB], dv[QB:2 * QB], dv[2 * QB:3 * QB]
        dkc_ref[...] += dk[3 * QB:]
        dvc_ref[...] += dv[3 * QB:]
    blk = pl.BlockSpec((QB, 128), lambda jj, b: (b, jj))
    cblk = pl.BlockSpec((tc, 128), lambda jj, b: (0, jj))
    part = jax.ShapeDtypeStruct((R, 256), F32)
    csum = jax.ShapeDtypeStruct((tc, 256), F32)
    outs = pl.pallas_call(
        body, name="attn_bwd", grid=(2, R // QB),
        in_specs=[qspec, *kspecs, *vspecs, pl.BlockSpec((QB, 512), lambda jj, b: (b, jj)), _probs_spec(nk)],
        out_specs=[pl.BlockSpec((QB, 512), lambda jj, b: (b, jj)), blk, blk, blk, blk, blk, blk, cblk, cblk,
                   pl.BlockSpec((8, 128), lambda jj, b: (0, 0))],
        out_shape=[jax.ShapeDtypeStruct((R, N_HEADS * HEAD_DIM), F32), part, part, part, part, part, part,
                   csum, csum, jax.ShapeDtypeStruct((8, 128), F32)],
        compiler_params=_params(48))(qkvr, *([qkvr] * 8), do, probs)
    return outs[0], outs[1:4], outs[4:7], outs[7], outs[8], outs[9]


def loss_head(x, nw, target, T):
    R, dm = x.shape
    nl = T // TMR

    def body(x_ref, nw_ref, t_ref, loss_ref, dx_ref, dnw_ref):
        i = pl.program_id(0)

        @pl.when(i == 0)
        def _():
            loss_ref[...] = jnp.zeros_like(loss_ref)
            dnw_ref[...] = jnp.zeros_like(dnw_ref)
        live = (i < nl).astype(F32)
        nwv = nw_ref[...]
        xv = x_ref[...]
        r = lax.rsqrt(jnp.mean(xv * xv, axis=-1, keepdims=True) + EPS)
        xh = xv * r
        err = xh * nwv - t_ref[...]
        per_row = jnp.mean(err * err, axis=-1, keepdims=True)
        loss_ref[...] += 0.5 * live * jnp.sum(per_row)
        dy = err * (live / dm)
        dnw_ref[...] += _colsum8(dy * xh)
        dxh = dy * nwv
        dx_ref[...] = r * (dxh - xh * jnp.mean(dxh * xh, axis=-1, keepdims=True))
    tile = pl.BlockSpec((TMR, dm), lambda i: (i, 0))
    return pl.pallas_call(
        body, name="loss_head", grid=(R // TMR,),
        in_specs=[tile, pl.BlockSpec((1, dm), lambda i: (0, 0)),
                  pl.BlockSpec((TMR, dm), lambda i: (jnp.minimum(i, nl - 1), 0))],
        out_specs=[pl.BlockSpec((8, 128), lambda i: (0, 0)), tile, pl.BlockSpec((8, dm), lambda i: (0, 0))],
        out_shape=[jax.ShapeDtypeStruct((8, 128), F32), jax.ShapeDtypeStruct((R, dm), F32),
                   jax.ShapeDtypeStruct((8, dm), F32)])(x, nw, target)


def adaln_fwd(cond, w_mod, b_mod):
    nl, dm, ns = w_mod.shape

    def body(c_ref, w_ref, b_ref, o_ref):
        cv = c_ref[...]
        s = (cv * _sigmoid(cv)).astype(BF16)
        o_ref[...] = jnp.dot(s, w_ref[...].astype(BF16), preferred_element_type=F32) + b_ref[...]
    return pl.pallas_call(
        body, name="adaln_fwd", grid=(nl,),
        in_specs=[pl.BlockSpec((16, dm), lambda l: (0, 0)), pl.BlockSpec((None, dm, ns), lambda l: (l, 0, 0)),
                  pl.BlockSpec((None, 1, ns), lambda l: (l, 0, 0))],
        out_specs=pl.BlockSpec((None, 16, ns), lambda l: (l, 0, 0)),
        out_shape=jax.ShapeDtypeStruct((nl, 16, ns), F32), compiler_params=_params(48))(cond, w_mod, b_mod)


def adaln_bwd(cond, dmod, w_mod):
    nl, dm, ns = w_mod.shape

    def body(c_ref, d_ref, w_ref, gw_ref, ds_ref):
        l = pl.program_id(0)

        @pl.when(l == 0)
        def _():
            ds_ref[...] = jnp.zeros_like(ds_ref)
        cv = c_ref[...]
        s = (cv * _sigmoid(cv)).astype(BF16)
        dv = d_ref[...].astype(BF16)
        gw_ref[...] = lax.dot_general(s, dv, (((0,), (0,)), ((), ())), preferred_element_type=F32)
        ds_ref[...] += lax.dot_general(dv, w_ref[...].astype(BF16), (((1,), (1,)), ((), ())),
                                       preferred_element_type=F32)
    return pl.pallas_call(
        body, name="adaln_bwd", grid=(nl,),
        in_specs=[pl.BlockSpec((16, dm), lambda l: (0, 0)), pl.BlockSpec((None, 16, ns), lambda l: (l, 0, 0)),
                  pl.BlockSpec((None, dm, ns), lambda l: (l, 0, 0))],
        out_specs=[pl.BlockSpec((None, dm, ns), lambda l: (l, 0, 0)), pl.BlockSpec((16, dm), lambda l: (0, 0))],
        out_shape=[jax.ShapeDtypeStruct((nl, dm, ns), F32), jax.ShapeDtypeStruct((16, dm), F32)],
        compiler_params=_params(48))(cond, dmod, w_mod)


def _me():
    return lax.axis_index("x"), lax.axis_index("y"), lax.axis_index("c")


def allgather8(block):
    m_per, n = block.shape

    def body(x_ref, out_ref, send_sems, recv_sems, local_sem):
        x, y, c = _me()
        me, sibling = (x, y, c), (x, y, 1 - c)
        chips = [(1 - x, y), (x, 1 - y), (1 - x, 1 - y)]

        def rows(px, py, pc):
            return out_ref.at[pl.ds((4 * px + 2 * py + pc) * m_per, m_per), :]

        def copy(k, blk, to, src=None):
            return pltpu.make_async_remote_copy(
                src_ref=rows(*blk) if src is None else src, dst_ref=rows(*blk),
                send_sem=send_sems.at[k], recv_sem=recv_sems.at[k], device_id=to, device_id_type=MESH)
        mine = pltpu.make_async_copy(x_ref, rows(*me), local_sem)
        mine.start()
        first = [copy(0, me, sibling, src=x_ref)]
        first += [copy(1 + j, me, (*chip, c), src=x_ref) for j, chip in enumerate(chips)]
        for cp in first:
            cp.start()
        passed = [copy(4 + j, (*chip, c), sibling) for j, chip in enumerate(chips)]
        for j, chip in enumerate(chips):
            copy(1 + j, (*chip, c), me).wait_recv()
            passed[j].start()
        copy(0, sibling, me).wait_recv()
        for j, chip in enumerate(chips):
            copy(4 + j, (*chip, 1 - c), me).wait_recv()
        for cp in first + passed:
            cp.wait_send()
        mine.wait()
    return pl.pallas_call(
        body, name="allgather8",
        out_shape=jax.ShapeDtypeStruct((N_DEV * m_per, n), block.dtype),
        in_specs=[pl.BlockSpec(memory_space=pltpu.VMEM)],
        out_specs=pl.BlockSpec(memory_space=pltpu.VMEM),
        scratch_shapes=[pltpu.SemaphoreType.DMA((7,)), pltpu.SemaphoreType.DMA((7,)), pltpu.SemaphoreType.DMA],
        compiler_params=_params(48))(block)


def _other_chips(x, y):
    return [(1 - x, y), (x, 1 - y), (1 - x, 1 - y)]


_HBM = pl.BlockSpec(memory_space=pltpu.HBM)
_SEM = pl.BlockSpec(memory_space=pltpu.SEMAPHORE)
_ANY = pl.BlockSpec(memory_space=pl.ANY)
_EFFECT = pltpu.SideEffectType.DATAFLOW_SIDE_EFFECTING


def _in_hbm(v):
    return pltpu.with_memory_space_constraint(v, pltpu.HBM)


def cast_into_slot(w, chip_id):
    kb, nb = w.shape
    tr = _row_tile(kb)

    def body(chip_ref, w_ref, o_ref):
        del chip_ref
        o_ref[...] = w_ref[...].astype(BF16)
    return pl.pallas_call(
        body, name="cast_into_slot",
        grid_spec=pltpu.PrefetchScalarGridSpec(
            num_scalar_prefetch=1, grid=(kb // tr,),
            in_specs=[pl.BlockSpec((tr, nb), lambda i, chip: (i, 0))],
            out_specs=pl.BlockSpec((None, tr, nb), lambda i, chip: (chip[0], i, 0))),
        out_shape=jax.ShapeDtypeStruct((N_CHIP, kb, nb), BF16))(chip_id, w)


def _split_copies(mode, srcs, lands, send_sems, recv_sems):
    x, y, c = _me()
    out = []
    for t in range(len(lands)):
        for k, chip in enumerate(_other_chips(x, y)):
            if mode == "gather":
                src = dst = lands[t].at[2 * x + y]
                landed = lands[t].at[2 * chip[0] + chip[1]]
            else:
                src, dst, landed = srcs[t].at[2 * chip[0] + chip[1]], lands[t].at[k], lands[t].at[k]
            send = pltpu.make_async_remote_copy(src_ref=src, dst_ref=dst, send_sem=send_sems.at[3 * t + k],
                                                recv_sem=recv_sems.at[3 * t + k], device_id=(*chip, c),
                                                device_id_type=MESH)
            recv = pltpu.make_async_remote_copy(src_ref=src, dst_ref=landed, send_sem=send_sems.at[3 * t + k],
                                                recv_sem=recv_sems.at[3 * t + k], device_id=(*chip, c),
                                                device_id_type=MESH)
            out.append((send, recv))
    return out


def exchange_start(name, mode, srcs, lands, after):
    ns, nl = len(srcs), len(lands)
    na = ns + nl

    def body(*refs):
        src_refs, land_refs = refs[:ns], refs[ns:na]
        send_sems, recv_sems = refs[na + 1], refs[na + 2]
        token = refs[-1]
        for send, _ in _split_copies(mode, src_refs, land_refs, send_sems, recv_sems):
            send.start()
        token[...] = jnp.zeros_like(token)
    arrays = list(srcs) + list(lands)
    outs = pl.pallas_call(
        body, name=name,
        out_shape=(pltpu.SemaphoreType.DMA((3 * nl,)), pltpu.SemaphoreType.DMA((3 * nl,)),
                   *[pltpu.HBM(v.shape, v.dtype) for v in arrays], jax.ShapeDtypeStruct((8, 128), F32)),
        in_specs=[_HBM] * na + [_ANY],
        out_specs=(_SEM, _SEM, *[_HBM] * na, pl.BlockSpec(memory_space=pltpu.VMEM)),
        input_output_aliases={i: 2 + i for i in range(na)},
        compiler_params=pltpu.CompilerParams(has_side_effects=_EFFECT))(*[_in_hbm(v) for v in arrays], after)
    return outs[0], outs[1], list(outs[2:2 + ns]), list(outs[2 + ns:2 + na]), outs[-1]


def exchange_wait(name, mode, send_sems, recv_sems, srcs, lands, after):
    ns, nl = len(srcs), len(lands)
    na = ns + nl

    def body(*refs):
        for _, recv in _split_copies(mode, refs[:ns], refs[ns:na], refs[na], refs[na + 1]):
            recv.wait_send()
            recv.wait_recv()
    arrays = list(srcs) + list(lands)
    outs = pl.pallas_call(
        body, name=name,
        out_shape=[pltpu.HBM(v.shape, v.dtype) for v in arrays],
        in_specs=[_HBM] * na + [_SEM, _SEM, _ANY], out_specs=[_HBM] * na,
        input_output_aliases={i: i for i in range(na)},
        compiler_params=pltpu.CompilerParams(has_side_effects=_EFFECT))(*arrays, send_sems, recv_sems, after)
    return list(outs[:ns]), list(outs[ns:])


def swap_with_sibling(v):
    def body(v_ref, out_ref, send_sem, recv_sem):
        x, y, c = _me()
        cp = pltpu.make_async_remote_copy(src_ref=v_ref, dst_ref=out_ref, send_sem=send_sem, recv_sem=recv_sem,
                                          device_id=(x, y, 1 - c), device_id_type=MESH)
        cp.start()
        cp.wait()
    return pl.pallas_call(
        body, name="swap_with_sibling", out_shape=jax.ShapeDtypeStruct(v.shape, v.dtype),
        in_specs=[pl.BlockSpec(memory_space=pl.ANY)], out_specs=pl.BlockSpec(memory_space=pl.ANY),
        scratch_shapes=[pltpu.SemaphoreType.DMA, pltpu.SemaphoreType.DMA])(v)


def sum_slots(parts):
    n, rows, w = parts.shape
    tr = _row_tile(rows)

    def body(p_ref, o_ref):
        acc = p_ref[0].astype(F32)
        for k in range(1, n):
            acc = acc + p_ref[k].astype(F32)
        o_ref[...] = acc
    return pl.pallas_call(
        body, name="sum_slots", grid=(rows // tr,),
        in_specs=[pl.BlockSpec((n, tr, w), lambda i: (0, i, 0))], out_specs=pl.BlockSpec((tr, w), lambda i: (i, 0)),
        out_shape=jax.ShapeDtypeStruct((rows, w), F32), compiler_params=_params(48))(parts)


def sum_landed(landed, own, chip_id, layer, n_layers, buf):
    n, rows, w = landed.shape
    tr = _row_tile(rows)
    base = layer * (rows // tr)

    def compute(l_ref, g_ref, o_ref):
        acc = g_ref[...].astype(F32)
        for k in range(n):
            acc = acc + l_ref[k].astype(F32)
        o_ref[...] = acc
    in_specs = [pl.BlockSpec((n, tr, w), lambda i, chip: (0, i, 0)),
                pl.BlockSpec((None, tr, w), lambda i, chip: (chip[0], i, 0))]
    out_spec = pl.BlockSpec((tr, w), lambda i, chip: (base + i, 0))
    out_shape = jax.ShapeDtypeStruct((n_layers * rows, w), F32)
    if buf is None:
        def body(chip_ref, l_ref, g_ref, o_ref):
            del chip_ref
            compute(l_ref, g_ref, o_ref)
        return pl.pallas_call(
            body, name="sum_landed",
            grid_spec=pltpu.PrefetchScalarGridSpec(num_scalar_prefetch=1, grid=(rows // tr,), in_specs=in_specs,
                                                   out_specs=out_spec),
            out_shape=out_shape, compiler_params=_params(48))(chip_id, landed, own)

    def body(chip_ref, l_ref, g_ref, buf_ref, o_ref):
        del chip_ref, buf_ref
        compute(l_ref, g_ref, o_ref)
    return pl.pallas_call(
        body, name="sum_landed_into",
        grid_spec=pltpu.PrefetchScalarGridSpec(num_scalar_prefetch=1, grid=(rows // tr,),
                                               in_specs=in_specs + [_ANY], out_specs=out_spec),
        out_shape=out_shape, input_output_aliases={3: 0}, compiler_params=_params(48))(chip_id, landed, own, buf)


def adamw(w, ga, gb, m, v):
    rows, wd = w.shape
    tr = min(_row_tile(rows), 128)
    c1 = 1.0 / (1.0 - ADAM_B1 ** ADAM_STEP)
    c2 = 1.0 / (1.0 - ADAM_B2 ** ADAM_STEP)

    def update(wv, g, mv, vv, g_ref, d_ref, m_ref, v_ref):
        mn = ADAM_B1 * mv + (1.0 - ADAM_B1) * g
        vn = ADAM_B2 * vv + (1.0 - ADAM_B2) * (g * g)
        g_ref[...] = g
        m_ref[...] = mn
        v_ref[...] = vn
        d_ref[...] = -ADAM_LR * ((mn * c1) / (jnp.sqrt(vn * c2) + ADAM_EPS) + ADAM_WD * wv)
    tile = pl.BlockSpec((tr, wd), lambda i: (i, 0))
    out = jax.ShapeDtypeStruct((rows, wd), F32)
    if gb is None:
        def body(w_ref, ga_ref, m_ref, v_ref, g_out, d_out, m_out, v_out):
            update(w_ref[...], ga_ref[...], m_ref[...], v_ref[...], g_out, d_out, m_out, v_out)
        return pl.pallas_call(body, name="adamw", grid=(rows // tr,), in_specs=[tile] * 4,
                              out_specs=[tile] * 4, out_shape=[out] * 4)(w, ga, m, v)

    def body(w_ref, ga_ref, gb_ref, m_ref, v_ref, g_out, d_out, m_out, v_out):
        update(w_ref[...], ga_ref[...] + gb_ref[...], m_ref[...], v_ref[...], g_out, d_out, m_out, v_out)
    return pl.pallas_call(body, name="adamw_sum", grid=(rows // tr,), in_specs=[tile] * 5,
                          out_specs=[tile] * 4, out_shape=[out] * 4)(w, ga, gb, m, v)


def _rope_tables(T, R):
    rows = T // GRID_W
    row = jnp.repeat(jnp.arange(rows), GRID_W).astype(F32)
    col = jnp.tile(jnp.arange(GRID_W), rows).astype(F32)
    n_freq = HEAD_DIM // 4
    inv_freq = ROPE_THETA ** (-jnp.arange(n_freq, dtype=F32) / n_freq)
    ang = jnp.concatenate([row[:, None] * inv_freq, col[:, None] * inv_freq], axis=-1)
    cos, sin = jnp.cos(ang), jnp.sin(ang)
    cs = jnp.tile(cos, (1, 4))
    sn = jnp.tile(jnp.concatenate([-sin, sin], axis=-1), (1, 2))
    pad = R - T
    return (jnp.concatenate([cs, jnp.ones((pad, 128), F32)], axis=0),
            jnp.concatenate([sn, jnp.zeros((pad, 128), F32)], axis=0))


def _pack(parts, mult=8 * 128):
    flat = jnp.concatenate([p.reshape(-1).astype(F32) for p in parts])
    pad = (-flat.shape[0]) % mult
    return jnp.pad(flat, (0, pad)).reshape(-1, 128)


def _unpack(buf, shapes):
    flat = buf.reshape(-1)
    out, o = [], 0
    for s in shapes:
        n = 1
        for d in s:
            n *= d
        out.append(flat[o:o + n].reshape(s))
        o += n
    return out


def kernel(x, c, ctx, c_ctx, w_mod, b_mod, norm_mix, norm_ffn, w_in_ab, conv_a, conv_b, conv_b_bias, ln_b_gain, ln_b_bias, w_out_ab, w_qkv, w_o, sinks, w_up, w_conv_ffn, w_down, final_norm, loss_target, m_c_ctx, m_w_mod, m_b_mod, m_norm_mix, m_norm_ffn, m_w_in_ab, m_conv_a, m_conv_b, m_conv_b_bias, m_ln_b_gain, m_ln_b_bias, m_w_out_ab, m_w_qkv, m_w_o, m_sinks, m_w_up, m_w_conv_ffn, m_w_down, m_final_norm, v_c_ctx, v_w_mod, v_b_mod, v_norm_mix, v_norm_ffn, v_w_in_ab, v_conv_a, v_conv_b, v_conv_b_bias, v_ln_b_gain, v_ln_b_bias, v_w_out_ab, v_w_qkv, v_w_o, v_sinks, v_w_up, v_w_conv_ffn, v_w_down, v_final_norm):
    T, dm = x.shape[1], x.shape[2]
    tc = ctx.shape[1]
    R = T + tc
    depth = w_mod.shape[0]
    ax, ay, ac = lax.axis_index("x"), lax.axis_index("y"), lax.axis_index("c")
    chip = 2 * ax + ay
    dev = 4 * ax + 2 * ay + ac

    small_w = [conv_a, conv_b, w_conv_ffn]
    gathered = allgather8(_pack([c] + small_w)).reshape(N_DEV, -1)
    cond8 = gathered[:, :dm]
    off = dm
    full_small = []
    for wsh in small_w:
        n = wsh.size
        per_chip = gathered[0::2, off:off + n].reshape((N_CHIP,) + wsh.shape)
        full_small.append(jnp.concatenate([per_chip[q] for q in range(N_CHIP)], axis=-1))
        off += n
    conv_a_f, conv_b_f, w_conv_ffn_f = full_small
    cond = jnp.concatenate([cond8, c_ctx[None, :], jnp.zeros((7, dm), F32)], axis=0)

    ns_mod = w_mod.shape[2]
    b_mod_sh = lax.dynamic_slice_in_dim(b_mod, chip * ns_mod, ns_mod, axis=1)[:, None, :]
    mod_sh = adaln_fwd(cond, w_mod, b_mod_sh)
    mod_all = allgather8(mod_sh.reshape(depth * 16, ns_mod)).reshape(N_DEV, depth, 16, ns_mod)
    mod_full = jnp.concatenate([mod_all[2 * q] for q in range(N_CHIP)], axis=-1)
    mine = lax.dynamic_index_in_dim(mod_full, dev, axis=1, keepdims=False)
    mods = jnp.stack([mine, mod_full[:, 8]], axis=1).reshape(depth, 2, 6, dm)

    masters = {"w_in_ab": w_in_ab, "w_out_ab": w_out_ab, "w_qkv": w_qkv, "w_o": w_o, "w_up": w_up, "w_down": w_down}
    chip_id = chip.astype(jnp.int32).reshape(1)

    def half_weights(l, half):
        if half == 1:
            return [("w_up", l), ("w_down", l)]
        return [("w_in_ab", l // 2), ("w_out_ab", l // 2)] if l % 2 == 0 else [("w_qkv", l // 2), ("w_o", l // 2)]
    in_flight, after = {}, mods
    for l in range(depth):
        for half in range(2):
            lands = [cast_into_slot(masters[n][j], chip_id) for n, j in half_weights(l, half)]
            send_sems, recv_sems, _, lands, after = exchange_start(f"gather_start_{l}_{half}", "gather", [], lands, after)
            in_flight[l, half] = (send_sems, recv_sems, lands)
    mods = mods + after[0, 0]

    def gathered_weights(l, half, after):
        send_sems, recv_sems, lands = in_flight[l, half]
        _, landed = exchange_wait(f"gather_wait_{l}_{half}", "gather", send_sems, recv_sems, [], lands, after)
        return dict(zip([n for n, _ in half_weights(l, half)], landed))

    cs, sn = _rope_tables(T, R)
    bias = window_bias(T, R)
    sinks_flat = sinks.reshape(-1)

    xs = jnp.concatenate([x[0], ctx[0]], axis=0)
    saved, W = [], []
    h1 = norm_mod_fwd(xs, norm_mix[0][None], mods[0], 0, T)
    for l in range(depth):
        e = l // 2
        wl = gathered_weights(l, 0, h1)
        W.append(wl)
        s = {"x0": xs, "h1": h1}
        if l % 2 == 0:
            p = mm_nn(h1, wl["w_in_ab"], BF16)
            yab, y1, x1, h2 = mixer_fwd(p, conv_a_f[e], conv_b_f[e], conv_b_bias[e][None], ln_b_gain[e][None],
                                        ln_b_bias[e][None], wl["w_out_ab"], xs, norm_ffn[l][None], mods[l], 2, 3, T)
            s.update(p=p, mix=yab)
        else:
            qkvr = mm_qkv_rope(h1, wl["w_qkv"], cs, sn)
            att, probs = attn_fwd(qkvr, sinks_flat[e * N_HEADS:(e + 1) * N_HEADS], bias, T)
            s.update(qkvr=qkvr, mix=att, probs=probs)
            y1, x1, h2 = mm_resid_norm_fwd(att, wl["w_o"], xs, norm_ffn[l][None], mods[l], mods[l], 2, 3, T)
        wl.update(gathered_weights(l, 1, h2))
        u = mm_nn(h2, wl["w_up"], BF16)
        if l + 1 < depth:
            z, y2, xs, h1 = ffn_mid_fwd(u, w_conv_ffn_f[l], wl["w_down"], x1, norm_mix[l + 1][None], mods[l],
                                        mods[l + 1], 5, 0, T)
        else:
            z, y2, xs = ffn_mid_fwd(u, w_conv_ffn_f[l], wl["w_down"], x1, None, mods[l], None, 5, 0, T)
        s.update(y1=y1, x1=x1, h2=h2, u=u, z=z, y2=y2)
        saved.append(s)

    loss_part, dx, d_final = loss_head(xs, final_norm[None], loss_target[0], T)
    loss = lax.psum(loss_part[0, 0], ("x", "y", "c"))

    d_mods, d_norm_mix, d_norm_ffn = [None] * depth, [None] * depth, [None] * depth
    d_conv_a, d_conv_b, d_vecs, d_sinks, d_wc = [None] * 2, [None] * 2, [None] * 2, [None] * 2, [None] * depth
    dss1, dss2, dg1, dg2 = [None] * depth, [None] * depth, [None] * depth, [None] * depth
    scattering = {}

    def scatter(l, half, G, after):
        grads_h = [G[n] for n, _ in half_weights(l, half)]
        lands = [lax.empty((N_CHIP - 1, *g.shape[1:]), g.dtype) for g in grads_h]
        send_sems, recv_sems, grads_h, lands, token = exchange_start(
            f"scatter_start_{l}_{half}", "scatter", grads_h, lands, after)
        scattering[l, half] = (send_sems, recv_sems, grads_h, lands)
        return token

    dy2, dg2[depth - 1] = resid_bwd(dx, saved[depth - 1]["y2"], mods[depth - 1], 5, T)
    pending = 0.0
    for l in reversed(range(depth)):
        e = l // 2
        s, wl = saved[l], W[l]
        G = {}
        G["w_down"] = mm_tn(s["z"], dy2, "row", wl["w_down"])
        duc, d_wc[l] = ffn_mid_bwd(mm_nt(dy2, wl["w_down"], BF16), s["u"], w_conv_ffn_f[l] + pending, T)
        du, dx, dy1, dss2[l], d_norm_ffn[l], dg1[l] = ffn_up_bwd(
            duc, w_conv_ffn_f[l], wl["w_up"], s["x1"], norm_ffn[l][None], mods[l], dx, s["y1"], mods[l], 3, 2, T)
        G["w_up"] = mm_tn(s["h2"], du, "col", wl["w_up"])
        started = scatter(l, 1, G, du)[0, 0]
        if l % 2 == 0:
            G["w_out_ab"] = mm_tn(s["mix"], dy1, "row", wl["w_out_ab"])
            dyab = mm_nt(dy1, wl["w_out_ab"], F32)
            dmid, d_conv_a[e], d_conv_b[e], d_vecs[e] = convmix_bwd1(
                dyab, s["p"], conv_a_f[e] + started, conv_b_f[e], conv_b_bias[e][None], ln_b_gain[e][None],
                ln_b_bias[e][None], T)
            if l > 0:
                dp, dx, dy2, dss1[l], d_norm_mix[l], dg2[l - 1] = mixer_in_bwd(
                    dmid, s["p"], conv_a_f[e], conv_b_f[e], wl["w_in_ab"], s["x0"], norm_mix[l][None], mods[l], dx,
                    saved[l - 1]["y2"], mods[l - 1], 0, 5, T)
            else:
                dp, dx, dss1[l], d_norm_mix[l] = mixer_in_bwd(
                    dmid, s["p"], conv_a_f[e], conv_b_f[e], wl["w_in_ab"], s["x0"], norm_mix[l][None], mods[l], dx,
                    None, None, 0, 0, T)
            G["w_in_ab"] = mm_tn(s["h1"], dp, "col", wl["w_in_ab"])
        else:
            G["w_o"] = mm_tn(s["mix"], dy1, "row", wl["w_o"])
            datt = mm_nt(dy1, wl["w_o"], BF16)
            dq, dks, dvs, dkc, dvc, d_sinks[e] = attn_bwd(s["qkvr"], datt, s["probs"], T)
            dqkv = rope_bwd(dq, dks, dvs, dkc, dvc, cs + started, sn, T)
            G["w_qkv"] = mm_tn(s["h1"], dqkv, "col", wl["w_qkv"])
            dx, dy2, dss1[l], d_norm_mix[l], dg2[l - 1] = mm_norm_resid_bwd(
                dqkv, wl["w_qkv"], s["x0"], norm_mix[l][None], mods[l], dx, saved[l - 1]["y2"], mods[l - 1], 0, 5, T)
        token = scatter(l, 0, G, dx)
        pending = token[0, 0]
    grad_x = dx[:T][None]
    for l in range(depth):
        a1, a2 = dss1[l].sum(2), dss2[l].sum(2)
        d_mods[l] = jnp.stack([a1[:, 0], a1[:, 1], dg1[l].sum(1), a2[:, 0], a2[:, 1], dg2[l].sum(1)], axis=1)

    d_mods = jnp.stack(d_mods)
    summed_parts = [
        d_mods[:, 1],
        jnp.stack(d_norm_mix).sum(1), jnp.stack(d_norm_ffn).sum(1),
        jnp.stack(d_conv_a).sum(2), jnp.stack(d_conv_b).sum(2),
        jnp.stack(d_vecs).sum(2),
        jnp.stack(d_sinks)[:, 0, :N_HEADS],
        jnp.stack(d_wc).sum(2), d_final.sum(0)]
    summed_shapes = [p.shape for p in summed_parts]
    n_own = depth * 6 * dm
    pack = _pack([d_mods[:, 0]] + summed_parts)
    parts = allgather8(pack).reshape(N_DEV, -1, 128)
    total = sum_slots(parts)
    own_rows = parts.reshape(N_DEV, -1)[:, :n_own].reshape(N_DEV, depth, 6 * dm)
    (dmod_ctx, g_norm_mix, g_norm_ffn, g_conv_a, g_conv_b, g_vecs, g_sinks, g_wc, g_final) = _unpack(
        total.reshape(-1)[n_own:], summed_shapes)
    dmod_rows = jnp.concatenate([jnp.moveaxis(own_rows, 0, 1), dmod_ctx.reshape(depth, 1, 6 * dm),
                                 jnp.zeros((depth, 7, 6 * dm), F32)], axis=1)
    g_b_mod = dmod_rows.sum(1)
    dmod_sh = lax.dynamic_slice_in_dim(dmod_rows, chip * ns_mod, ns_mod, axis=2)
    g_w_mod, dsilu = adaln_bwd(cond, dmod_sh, w_mod)
    dsilu_all = allgather8(dsilu[8:16]).reshape(N_DEV, 8, dm)
    dsilu_ctx = sum_slots(dsilu_all[0::2])[0]
    sg = jax.nn.sigmoid(c_ctx)
    g_c_ctx = dsilu_ctx * (sg * (1.0 + c_ctx * (1.0 - sg)))

    def shard_cols(full, width):
        return lax.dynamic_slice_in_dim(full, chip * width, width, axis=full.ndim - 1)
    g_conv_a_s = shard_cols(g_conv_a, conv_a.shape[-1])
    g_conv_b_s = shard_cols(g_conv_b, conv_b.shape[-1])
    g_wc_s = shard_cols(g_wc, w_conv_ffn.shape[-1])

    grads, deltas, new_m, new_v = {}, {}, {}, {}

    def step_2d(name, wv, ga, gb, mv, vv):
        shp = wv.shape
        r2 = lambda t: t.reshape(-1, shp[-1])
        g, d, mn, vn = adamw(r2(wv), r2(ga), None if gb is None else r2(gb), r2(mv), r2(vv))
        grads[name], deltas[name], new_m[name], new_v[name] = (t.reshape(shp) for t in (g, d, mn, vn))

    sums = {n: None for n in masters}
    for l in reversed(range(depth)):
        for half in (1, 0):
            send_sems, recv_sems, grads_h, lands = scattering[l, half]
            grads_h, landed = exchange_wait(f"scatter_wait_{l}_{half}", "scatter", send_sems, recv_sems, grads_h,
                                            lands, token)
            for (n, j), own, arr in zip(half_weights(l, half), grads_h, landed):
                sums[n] = sum_landed(arr, own, chip_id, j, masters[n].shape[0], sums[n])
    moments = {"w_in_ab": (m_w_in_ab, v_w_in_ab), "w_out_ab": (m_w_out_ab, v_w_out_ab),
               "w_qkv": (m_w_qkv, v_w_qkv), "w_o": (m_w_o, v_w_o), "w_up": (m_w_up, v_w_up),
               "w_down": (m_w_down, v_w_down)}
    for name, wv in masters.items():
        other = swap_with_sibling(sums[name])
        step_2d(name, wv, sums[name].reshape(wv.shape), other.reshape(wv.shape), *moments[name])
    step_2d("w_mod", w_mod, g_w_mod, None, m_w_mod, v_w_mod)

    small = [("c_ctx", c_ctx, g_c_ctx, m_c_ctx, v_c_ctx), ("b_mod", b_mod, g_b_mod, m_b_mod, v_b_mod),
             ("norm_mix", norm_mix, g_norm_mix, m_norm_mix, v_norm_mix),
             ("norm_ffn", norm_ffn, g_norm_ffn, m_norm_ffn, v_norm_ffn),
             ("conv_a", conv_a, g_conv_a_s, m_conv_a, v_conv_a), ("conv_b", conv_b, g_conv_b_s, m_conv_b, v_conv_b),
             ("conv_b_bias", conv_b_bias, g_vecs[:, 0], m_conv_b_bias, v_conv_b_bias),
             ("ln_b_gain", ln_b_gain, g_vecs[:, 1], m_ln_b_gain, v_ln_b_gain),
             ("ln_b_bias", ln_b_bias, g_vecs[:, 2], m_ln_b_bias, v_ln_b_bias),
             ("sinks", sinks, g_sinks, m_sinks, v_sinks),
             ("w_conv_ffn", w_conv_ffn, g_wc_s, m_w_conv_ffn, v_w_conv_ffn),
             ("final_norm", final_norm, g_final, m_final_norm, v_final_norm)]
    shapes = [t[1].shape for t in small]
    packed = [_pack([t[k] for t in small]) for k in (1, 2, 3, 4)]
    n_real = sum(t[1].size for t in small)
    lane_id = jnp.arange(packed[3].size).reshape(packed[3].shape)
    packed[3] = jnp.where(lane_id < n_real, packed[3], 1.0)
    outs = adamw(packed[0], packed[1], None, packed[2], packed[3])
    for (name, *_), g, d, mn, vn in zip(small, *[_unpack(o, shapes) for o in outs]):
        grads[name], deltas[name], new_m[name], new_v[name] = g, d, mn, vn

    order = ["c_ctx", "w_mod", "b_mod", "norm_mix", "norm_ffn", "w_in_ab", "conv_a", "conv_b", "conv_b_bias",
             "ln_b_gain", "ln_b_bias", "w_out_ab", "w_qkv", "w_o", "sinks", "w_up", "w_conv_ffn", "w_down",
             "final_norm"]
    return (loss, grad_x, *[grads[n] for n in order], *[deltas[n] for n in order],
            *[new_m[n] for n in order], *[new_v[n] for n in order])
```

```python
import jax
import jax.numpy as jnp
from jax import lax
from jax.experimental import pallas as pl
from jax.experimental.pallas import tpu as pltpu

F32 = jnp.float32
BF16 = jnp.bfloat16
MESH = pl.DeviceIdType.MESH

EPS = 1e-6
NEG_INF = -1e30
GRID_W = 64
HEAD_DIM = 64
N_HEADS = 16
WINDOW = 128
QB = 128
ROPE_THETA = 10000.0
A_W = 512
B_CONV = 31
D_FF = 2816
ADAM_LR, ADAM_B1, ADAM_B2, ADAM_EPS, ADAM_WD, ADAM_STEP = 0.001, 0.9, 0.999, 1e-8, 0.01, 10

TMR = 256
HALO = 16
N_DEV = 8
N_CHIP = 4


def _params(vmem_mb=None):
    if vmem_mb is None:
        return pltpu.CompilerParams()
    return pltpu.CompilerParams(vmem_limit_bytes=vmem_mb * 1024 * 1024)


def _row_tile(rows, cap=768):
    for t in (2816, 1408, 768, 704, 512, 384, 256, 128, 64, 32, 16, 8):
        if t <= cap and rows % t == 0:
            return t
    raise ValueError(rows)


def _colsum8(v):
    r, c = v.shape
    return v.reshape(r // 8, 8, c).sum(axis=0)


def _sigmoid(v):
    return 0.5 * jnp.tanh(0.5 * v) + 0.5


def mm_nn(a, w, out_dtype):
    R = a.shape[0]
    _, kb, nb = w.shape
    tm = _row_tile(R)

    def body(a_ref, w_ref, o_ref):
        av = a_ref[...].astype(BF16)
        for q in range(N_CHIP):
            o_ref[:, q * nb:(q + 1) * nb] = jnp.dot(av, w_ref[q], preferred_element_type=F32).astype(o_ref.dtype)
    return pl.pallas_call(
        body, name="mm_nn_col", grid=(R // tm,),
        in_specs=[pl.BlockSpec((tm, kb), lambda i: (i, 0)),
                  pl.BlockSpec((N_CHIP, kb, nb), lambda i: (0, 0, 0), pipeline_mode=pl.Buffered(1))],
        out_specs=pl.BlockSpec((tm, N_CHIP * nb), lambda i: (i, 0)),
        out_shape=jax.ShapeDtypeStruct((R, N_CHIP * nb), out_dtype),
        compiler_params=_params(48))(a, w)


def mm_nt(d, w, out_dtype):
    R = d.shape[0]
    _, kb, nb = w.shape
    tm = _row_tile(R)
    contract_last = (((1,), (1,)), ((), ()))
    resident = pl.BlockSpec((N_CHIP, kb, nb), lambda i: (0, 0, 0), pipeline_mode=pl.Buffered(1))

    def body(d_ref, w_ref, o_ref):
        wv = w_ref[...].reshape(N_CHIP * kb, nb)
        o_ref[...] = lax.dot_general(d_ref[...].astype(BF16), wv, contract_last,
                                     preferred_element_type=F32).astype(o_ref.dtype)
    return pl.pallas_call(
        body, name="mm_nt_row", grid=(R // tm,),
        in_specs=[pl.BlockSpec((tm, nb), lambda i: (i, 0)), resident],
        out_specs=pl.BlockSpec((tm, N_CHIP * kb), lambda i: (i, 0)),
        out_shape=jax.ShapeDtypeStruct((R, N_CHIP * kb), out_dtype),
        compiler_params=_params(48))(d, w)


def mm_tn(a, d, kind, like):
    R = a.shape[0]
    _, kb, nb = like.shape
    tm = _row_tile(R, 1408 if kind == "col" else 768)
    nsteps = R // tm
    contract_rows = (((0,), (0,)), ((), ()))
    out_shape = jax.ShapeDtypeStruct(like.shape, BF16)

    def accumulate(a_ref, d_ref, acc_ref):
        @pl.when(pl.program_id(1) == 0)
        def _():
            acc_ref[...] = jnp.zeros_like(acc_ref)
        acc_ref[...] += lax.dot_general(a_ref[...].astype(BF16), d_ref[...].astype(BF16), contract_rows,
                                        preferred_element_type=F32)
    if kind == "col":
        def body(a_ref, d_ref, o_ref, acc_ref):
            accumulate(a_ref, d_ref, acc_ref)

            @pl.when(pl.program_id(1) == nsteps - 1)
            def _():
                o_ref[...] = acc_ref[...].astype(BF16)
        return pl.pallas_call(
            body, name="mm_tn_col", grid=(N_CHIP, nsteps),
            in_specs=[pl.BlockSpec((tm, kb), lambda q, i: (i, 0)), pl.BlockSpec((tm, nb), lambda q, i: (i, q))],
            out_specs=pl.BlockSpec((None, kb, nb), lambda q, i: (q, 0, 0)), out_shape=out_shape,
            scratch_shapes=[pltpu.VMEM((kb, nb), F32)], compiler_params=_params(48))(a, d)
    tn = 512

    def body(a_ref, d_ref, o_ref, acc_ref):
        accumulate(a_ref, d_ref, acc_ref)

        @pl.when(pl.program_id(1) == nsteps - 1)
        def _():
            o_ref[...] = acc_ref[...].astype(BF16).reshape(N_CHIP, kb, tn)
    return pl.pallas_call(
        body, name="mm_tn_row", grid=(nb // tn, nsteps),
        in_specs=[pl.BlockSpec((tm, N_CHIP * kb), lambda n, i: (i, 0)), pl.BlockSpec((tm, tn), lambda n, i: (i, n))],
        out_specs=pl.BlockSpec((N_CHIP, kb, tn), lambda n, i: (0, 0, n)), out_shape=out_shape,
        scratch_shapes=[pltpu.VMEM((N_CHIP * kb, tn), F32)], compiler_params=_params(48))(a, d)


def _seg(i, T):
    return (i >= T // TMR).astype(jnp.int32)


def norm_mod_fwd(x, nw, mod, k, T):
    R, dm = x.shape

    def body(x_ref, nw_ref, mod_ref, h_ref):
        seg = _seg(pl.program_id(0), T)
        sh = mod_ref[seg, pl.ds(k, 1), :]
        sc = mod_ref[seg, pl.ds(k + 1, 1), :]
        xv = x_ref[...]
        r = lax.rsqrt(jnp.mean(xv * xv, axis=-1, keepdims=True) + EPS)
        h_ref[...] = ((xv * r * nw_ref[...]) * (1.0 + sc) + sh).astype(BF16)
    return pl.pallas_call(
        body, name="norm_mod_fwd", grid=(R // TMR,),
        in_specs=[pl.BlockSpec((TMR, dm), lambda i: (i, 0)),
                  pl.BlockSpec((1, dm), lambda i: (0, 0)),
                  pl.BlockSpec((2, 6, dm), lambda i: (0, 0, 0))],
        out_specs=pl.BlockSpec((TMR, dm), lambda i: (i, 0)),
        out_shape=jax.ShapeDtypeStruct((R, dm), BF16))(x, nw, mod)


def mm_resid_norm_fwd(a, w, x, nw, mod_g, mod_n, kg, kn, T):
    R, dm = x.shape
    _, kb, nb = w.shape

    def body(a_ref, w_ref, x_ref, nw_ref, mg_ref, mn_ref, y_ref, xo_ref, h_ref):
        seg = _seg(pl.program_id(0), T)
        yv = jnp.dot(a_ref[...].astype(BF16), w_ref[...].reshape(N_CHIP * kb, nb), preferred_element_type=F32)
        y_ref[...] = yv
        xv = x_ref[...] + mg_ref[seg, pl.ds(kg, 1), :] * yv
        xo_ref[...] = xv
        r = lax.rsqrt(jnp.mean(xv * xv, axis=-1, keepdims=True) + EPS)
        h_ref[...] = ((xv * r * nw_ref[...]) * (1.0 + mn_ref[seg, pl.ds(kn + 1, 1), :])
                      + mn_ref[seg, pl.ds(kn, 1), :]).astype(BF16)
    tile = pl.BlockSpec((TMR, dm), lambda i: (i, 0))
    modspec = pl.BlockSpec((2, 6, dm), lambda i: (0, 0, 0))
    return pl.pallas_call(
        body, name="mm_resid_norm_fwd", grid=(R // TMR,),
        in_specs=[pl.BlockSpec((TMR, N_CHIP * kb), lambda i: (i, 0)),
                  pl.BlockSpec((N_CHIP, kb, nb), lambda i: (0, 0, 0), pipeline_mode=pl.Buffered(1)),
                  tile, pl.BlockSpec((1, dm), lambda i: (0, 0)), modspec, modspec],
        out_specs=[tile, tile, tile],
        out_shape=[jax.ShapeDtypeStruct((R, dm), F32), jax.ShapeDtypeStruct((R, dm), F32),
                   jax.ShapeDtypeStruct((R, dm), BF16)],
        compiler_params=_params(48))(a, w, x, nw, mod_g, mod_n)


def resid_bwd(dxn, y, mod, k, T):
    R, dm = dxn.shape

    def body(dx_ref, y_ref, mod_ref, dy_ref, dg_ref):
        i = pl.program_id(0)
        seg = _seg(i, T)

        @pl.when(i == 0)
        def _():
            dg_ref[...] = jnp.zeros_like(dg_ref)
        dxv = dx_ref[...]
        dy_ref[...] = (mod_ref[seg, pl.ds(k, 1), :] * dxv).astype(BF16)
        dg_ref[seg] += _colsum8(dxv * y_ref[...])
    tile = pl.BlockSpec((TMR, dm), lambda i: (i, 0))
    return pl.pallas_call(
        body, name="resid_bwd", grid=(R // TMR,),
        in_specs=[tile, tile, pl.BlockSpec((2, 6, dm), lambda i: (0, 0, 0))],
        out_specs=[tile, pl.BlockSpec((2, 8, dm), lambda i: (0, 0, 0))],
        out_shape=[jax.ShapeDtypeStruct((R, dm), BF16), jax.ShapeDtypeStruct((2, 8, dm), F32)])(dxn, y, mod)


def _halo_specs(width, R):
    nblk = R // HALO
    per = TMR // HALO
    return (pl.BlockSpec((HALO, width), lambda i: (jnp.maximum(i * per - 1, 0), 0)),
            pl.BlockSpec((TMR, width), lambda i: (i, 0)),
            pl.BlockSpec((HALO, width), lambda i: (jnp.minimum((i + 1) * per, nblk - 1), 0)))


def _halo_live(i, T, R):
    nl = T // TMR
    return (i != 0) & (i != nl), (i != nl - 1) & (i != R // TMR - 1)


def _ext(refs, c0, cw, live, halo=HALO):
    pref, ref, nref = refs
    before = jnp.where(live[0], pref[:, c0:c0 + cw].astype(F32)[HALO - halo:], 0.0)
    after = jnp.where(live[1], nref[:, c0:c0 + cw].astype(F32)[:halo], 0.0)
    return jnp.concatenate([before, ref[:, c0:c0 + cw].astype(F32), after], axis=0)


def _at(ext, off, halo=HALO):
    n = ext.shape[0]
    s = (-off) % n
    y = pltpu.roll(ext, s, 0) if s else ext
    return y[halo:halo + TMR]


def ffn_mid_fwd(u, wc, w_down, x, nw, mod_g, mod_n, kg, kn, T):
    R, w2 = u.shape
    dm = x.shape[1]
    _, kb, nb = w_down.shape
    cw = 256
    with_norm = nw is not None

    def body(*refs):
        if with_norm:
            up_ref, u_ref, un_ref, wc_ref, w_ref, x_ref, nw_ref, mg_ref, mn_ref, z_ref, y_ref, xo_ref, h_ref = refs
        else:
            up_ref, u_ref, un_ref, wc_ref, w_ref, x_ref, mg_ref, z_ref, y_ref, xo_ref = refs
        i = pl.program_id(0)
        seg = _seg(i, T)
        live = _halo_live(i, T, R)

        def conv(c0):
            e = _ext((up_ref, u_ref, un_ref), c0, cw, live, 8)
            return (wc_ref[pl.ds(0, 1), c0:c0 + cw] * _at(e, -1, 8) + wc_ref[pl.ds(1, 1), c0:c0 + cw] * _at(e, 0, 8)
                    + wc_ref[pl.ds(2, 1), c0:c0 + cw] * _at(e, 1, 8))
        yv = None
        for j in range(D_FF // cw):
            a = conv(j * cw)
            g = conv(D_FF + j * cw)
            zc = (g * _sigmoid(g) * a).astype(BF16)
            z_ref[:, j * cw:(j + 1) * cw] = zc
            t = jnp.dot(zc, w_ref[j * cw:(j + 1) * cw, :], preferred_element_type=F32)
            yv = t if yv is None else yv + t
        y_ref[...] = yv
        xv = x_ref[...] + mg_ref[seg, pl.ds(kg, 1), :] * yv
        xo_ref[...] = xv
        if with_norm:
            r = lax.rsqrt(jnp.mean(xv * xv, axis=-1, keepdims=True) + EPS)
            h_ref[...] = ((xv * r * nw_ref[...]) * (1.0 + mn_ref[seg, pl.ds(kn + 1, 1), :])
                          + mn_ref[seg, pl.ds(kn, 1), :]).astype(BF16)
    tile = pl.BlockSpec((TMR, dm), lambda i: (i, 0))
    modspec = pl.BlockSpec((2, 6, dm), lambda i: (0, 0, 0))
    w_down = w_down.reshape(N_CHIP * kb, nb)
    in_specs = [*_halo_specs(w2, R), pl.BlockSpec((3, w2), lambda i: (0, 0)),
                pl.BlockSpec((N_CHIP * kb, nb), lambda i: (0, 0), pipeline_mode=pl.Buffered(1)), tile]
    out_specs = [pl.BlockSpec((TMR, D_FF), lambda i: (i, 0)), tile, tile]
    out_shape = [jax.ShapeDtypeStruct((R, D_FF), BF16), jax.ShapeDtypeStruct((R, dm), F32),
                 jax.ShapeDtypeStruct((R, dm), F32)]
    if with_norm:
        return pl.pallas_call(
            body, name="ffn_mid_fwd", grid=(R // TMR,),
            in_specs=in_specs + [pl.BlockSpec((1, dm), lambda i: (0, 0)), modspec, modspec],
            out_specs=out_specs + [tile], out_shape=out_shape + [jax.ShapeDtypeStruct((R, dm), BF16)],
            compiler_params=_params(48))(u, u, u, wc, w_down, x, nw, mod_g, mod_n)
    return pl.pallas_call(
        body, name="ffn_mid_fwd_last", grid=(R // TMR,), in_specs=in_specs + [modspec],
        out_specs=out_specs, out_shape=out_shape, compiler_params=_params(48))(u, u, u, wc, w_down, x, mod_g)


def ffn_mid_bwd(dz, u, wc, T):
    R, w2 = u.shape
    cw = 256

    def body(dz_ref, up_ref, u_ref, un_ref, wc_ref, duc_ref, dwc_ref):
        i = pl.program_id(0)
        live = _halo_live(i, T, R)

        @pl.when(i == 0)
        def _():
            dwc_ref[...] = jnp.zeros_like(dwc_ref)

        def taps(c0):
            e = _ext((up_ref, u_ref, un_ref), c0, cw, live, 8)
            return [_at(e, -1, 8), _at(e, 0, 8), _at(e, 1, 8)]

        def conv(t, c0):
            return (wc_ref[pl.ds(0, 1), c0:c0 + cw] * t[0] + wc_ref[pl.ds(1, 1), c0:c0 + cw] * t[1]
                    + wc_ref[pl.ds(2, 1), c0:c0 + cw] * t[2])
        for j in range(D_FF // cw):
            ca, cg = j * cw, D_FF + j * cw
            dzv = dz_ref[:, ca:ca + cw].astype(F32)
            ta, tg = taps(ca), taps(cg)
            a, g = conv(ta, ca), conv(tg, cg)
            sg = _sigmoid(g)
            da = dzv * (g * sg)
            dg = dzv * a * (sg * (1.0 + g * (1.0 - sg)))
            duc_ref[:, ca:ca + cw] = da.astype(BF16)
            duc_ref[:, cg:cg + cw] = dg.astype(BF16)
            for k in range(3):
                dwc_ref[k, :, ca:ca + cw] += _colsum8(da * ta[k])
                dwc_ref[k, :, cg:cg + cw] += _colsum8(dg * tg[k])
    return pl.pallas_call(
        body, name="ffn_mid_bwd", grid=(R // TMR,),
        in_specs=[pl.BlockSpec((TMR, D_FF), lambda i: (i, 0)), *_halo_specs(w2, R),
                  pl.BlockSpec((3, w2), lambda i: (0, 0))],
        out_specs=[pl.BlockSpec((TMR, w2), lambda i: (i, 0)), pl.BlockSpec((3, 8, w2), lambda i: (0, 0, 0))],
        out_shape=[jax.ShapeDtypeStruct((R, w2), BF16), jax.ShapeDtypeStruct((3, 8, w2), F32)],
        compiler_params=_params(48))(dz, u, u, u, wc)


def ffn_up_bwd(duc, wc, w_up, x, nw, mod_n, dxr, y, mod_g, kn, kg, T):
    R, w2 = duc.shape
    dm = x.shape[1]
    _, kb, nb = w_up.shape
    cw = 128
    contract_last = (((1,), (1,)), ((), ()))

    def body(dp_ref, d_ref, dn_ref, wc_ref, w_ref, x_ref, nw_ref, mn_ref, dxr_ref, y_ref, mg_ref,
             du_ref, dx_ref, dy_ref, dmod_ref, dnw_ref, dg_ref):
        i = pl.program_id(0)
        seg = _seg(i, T)
        live = _halo_live(i, T, R)

        @pl.when(i == 0)
        def _():
            dmod_ref[...] = jnp.zeros_like(dmod_ref)
            dnw_ref[...] = jnp.zeros_like(dnw_ref)
            dg_ref[...] = jnp.zeros_like(dg_ref)
        dhv = None
        for q in range(N_CHIP):
            for j in range(nb // cw):
                c0 = q * nb + j * cw
                e = _ext((dp_ref, d_ref, dn_ref), c0, cw, live, 8)
                du_ref[:, c0:c0 + cw] = (wc_ref[pl.ds(0, 1), c0:c0 + cw] * _at(e, 1, 8)
                                         + wc_ref[pl.ds(1, 1), c0:c0 + cw] * _at(e, 0, 8)
                                         + wc_ref[pl.ds(2, 1), c0:c0 + cw] * _at(e, -1, 8)).astype(BF16)
            t = lax.dot_general(du_ref[:, q * nb:(q + 1) * nb], w_ref[q], contract_last,
                                preferred_element_type=F32)
            dhv = t if dhv is None else dhv + t
        sc = mn_ref[seg, pl.ds(kn + 1, 1), :]
        nwv = nw_ref[...]
        xv = x_ref[...]
        r = lax.rsqrt(jnp.mean(xv * xv, axis=-1, keepdims=True) + EPS)
        xh = xv * r
        dmod_ref[seg, 0] += _colsum8(dhv)
        dmod_ref[seg, 1] += _colsum8(dhv * (xh * nwv))
        dn = dhv * (1.0 + sc)
        dnw_ref[...] += _colsum8(dn * xh)
        dxh = dn * nwv
        dx = dxr_ref[...] + r * (dxh - xh * jnp.mean(dxh * xh, axis=-1, keepdims=True))
        dx_ref[...] = dx
        dy_ref[...] = (mg_ref[seg, pl.ds(kg, 1), :] * dx).astype(BF16)
        dg_ref[seg] += _colsum8(dx * y_ref[...])
    tile = pl.BlockSpec((TMR, dm), lambda i: (i, 0))
    modspec = pl.BlockSpec((2, 6, dm), lambda i: (0, 0, 0))
    return pl.pallas_call(
        body, name="ffn_up_bwd", grid=(R // TMR,),
        in_specs=[*_halo_specs(w2, R), pl.BlockSpec((3, w2), lambda i: (0, 0)),
                  pl.BlockSpec((N_CHIP, kb, nb), lambda i: (0, 0, 0), pipeline_mode=pl.Buffered(1)),
                  tile, pl.BlockSpec((1, dm), lambda i: (0, 0)), modspec, tile, tile, modspec],
        out_specs=[pl.BlockSpec((TMR, w2), lambda i: (i, 0)), tile, tile,
                   pl.BlockSpec((2, 2, 8, dm), lambda i: (0, 0, 0, 0)), pl.BlockSpec((8, dm), lambda i: (0, 0)),
                   pl.BlockSpec((2, 8, dm), lambda i: (0, 0, 0))],
        out_shape=[jax.ShapeDtypeStruct((R, w2), BF16), jax.ShapeDtypeStruct((R, dm), F32),
                   jax.ShapeDtypeStruct((R, dm), BF16), jax.ShapeDtypeStruct((2, 2, 8, dm), F32),
                   jax.ShapeDtypeStruct((8, dm), F32), jax.ShapeDtypeStruct((2, 8, dm), F32)],
        compiler_params=_params(48))(duc, duc, duc, wc, w_up, x, nw, mod_n, dxr, y, mod_g)


_CW = 128


def _mixer_a(prefs, wa_ref, live):
    cin = _ext(prefs, A_W, A_W, live) * _ext(prefs, 2 * A_W, A_W, live)
    ca = (wa_ref[pl.ds(0, 1), :] * _at(cin, -1) + wa_ref[pl.ds(1, 1), :] * _at(cin, 0)
          + wa_ref[pl.ds(2, 1), :] * _at(cin, 1))
    return cin, ca


def _mixer_b(prefs, wb_ref, bias_ref, live, ub_s, ub2_s):
    for cc in range(A_W // _CW):
        c0 = cc * _CW
        ub = _ext(prefs, 3 * A_W + c0, _CW, live) * _sigmoid(_ext(prefs, 4 * A_W + c0, _CW, live))
        ub_s[:, c0:c0 + _CW] = ub
        acc = jnp.zeros((TMR, _CW), F32) + bias_ref[:, c0:c0 + _CW]
        for k in range(B_CONV):
            acc = acc + wb_ref[pl.ds(k, 1), c0:c0 + _CW] * _at(ub, k - B_CONV // 2)
        ub2_s[:, c0:c0 + _CW] = acc


def _layernorm_stats(v):
    mu = jnp.mean(v, axis=-1, keepdims=True)
    xc = v - mu
    rs = lax.rsqrt(jnp.mean(xc * xc, axis=-1, keepdims=True) + EPS)
    return xc * rs, rs


def mixer_fwd(p, wa, wb, bias, lng, lnb, w_out, x, nw, mod, kg, kn, T):
    R, wp = p.shape
    dm = x.shape[1]
    _, kb, nb = w_out.shape

    def body(pp_ref, p_ref, pn_ref, wa_ref, wb_ref, bias_ref, lng_ref, lnb_ref, w_ref, x_ref, nw_ref, mod_ref,
             o_ref, y_ref, xo_ref, h_ref, ub_s, ub2_s):
        i = pl.program_id(0)
        seg = _seg(i, T)
        live = _halo_live(i, T, R)
        prefs = (pp_ref, p_ref, pn_ref)
        _, ca = _mixer_a(prefs, wa_ref, live)
        ya = (p_ref[:, 0:A_W].astype(F32) * ca).astype(BF16)
        o_ref[:, 0:A_W] = ya
        yv = jnp.dot(ya, w_ref[0:A_W, :], preferred_element_type=F32)
        _mixer_b(prefs, wb_ref, bias_ref, live, ub_s, ub2_s)
        xh, _ = _layernorm_stats(ub2_s[...])
        lv = xh * lng_ref[...] + lnb_ref[...]
        yb = (lv * _sigmoid(lv)).astype(BF16)
        o_ref[:, A_W:2 * A_W] = yb
        yv = yv + jnp.dot(yb, w_ref[A_W:2 * A_W, :], preferred_element_type=F32)
        y_ref[...] = yv
        xv = x_ref[...] + mod_ref[seg, pl.ds(kg, 1), :] * yv
        xo_ref[...] = xv
        r = lax.rsqrt(jnp.mean(xv * xv, axis=-1, keepdims=True) + EPS)
        h_ref[...] = ((xv * r * nw_ref[...]) * (1.0 + mod_ref[seg, pl.ds(kn + 1, 1), :])
                      + mod_ref[seg, pl.ds(kn, 1), :]).astype(BF16)
    vec = pl.BlockSpec((1, A_W), lambda i: (0, 0))
    tile = pl.BlockSpec((TMR, dm), lambda i: (i, 0))
    return pl.pallas_call(
        body, name="mixer_fwd", grid=(R // TMR,),
        in_specs=[*_halo_specs(wp, R), pl.BlockSpec((3, A_W), lambda i: (0, 0)),
                  pl.BlockSpec((B_CONV, A_W), lambda i: (0, 0)), vec, vec, vec,
                  pl.BlockSpec((N_CHIP * kb, nb), lambda i: (0, 0), pipeline_mode=pl.Buffered(1)),
                  tile, pl.BlockSpec((1, dm), lambda i: (0, 0)), pl.BlockSpec((2, 6, dm), lambda i: (0, 0, 0))],
        out_specs=[pl.BlockSpec((TMR, 2 * A_W), lambda i: (i, 0)), tile, tile, tile],
        out_shape=[jax.ShapeDtypeStruct((R, 2 * A_W), BF16), jax.ShapeDtypeStruct((R, dm), F32),
                   jax.ShapeDtypeStruct((R, dm), F32), jax.ShapeDtypeStruct((R, dm), BF16)],
        scratch_shapes=[pltpu.VMEM((TMR + 2 * HALO, A_W), F32), pltpu.VMEM((TMR, A_W), F32)],
        compiler_params=_params(48))(p, p, p, wa, wb, bias, lng, lnb, w_out.reshape(N_CHIP * kb, nb), x, nw, mod)


def convmix_bwd1(dyab, p, wa, wb, bias, lng, lnb, T):
    R, wp = p.shape

    def body(dy_ref, pp_ref, p_ref, pn_ref, wa_ref, wb_ref, bias_ref, lng_ref, lnb_ref,
             dmid_ref, dwa_ref, dwb_ref, dvec_ref, ub_s, ub2_s):
        i = pl.program_id(0)
        live = _halo_live(i, T, R)

        @pl.when(i == 0)
        def _():
            dwa_ref[...] = jnp.zeros_like(dwa_ref)
            dwb_ref[...] = jnp.zeros_like(dwb_ref)
            dvec_ref[...] = jnp.zeros_like(dvec_ref)
        prefs = (pp_ref, p_ref, pn_ref)
        cin, ca = _mixer_a(prefs, wa_ref, live)
        dya = dy_ref[:, 0:A_W]
        dmid_ref[:, 0:A_W] = dya * ca
        dca = dya * p_ref[:, 0:A_W].astype(F32)
        dmid_ref[:, A_W:2 * A_W] = dca
        for k in range(3):
            dwa_ref[k] += _colsum8(dca * _at(cin, k - 1))
        _mixer_b(prefs, wb_ref, bias_ref, live, ub_s, ub2_s)
        xh, rs = _layernorm_stats(ub2_s[...])
        gain = lng_ref[...]
        lv = xh * gain + lnb_ref[...]
        sl = _sigmoid(lv)
        dl = dy_ref[:, A_W:2 * A_W] * (sl * (1.0 + lv * (1.0 - sl)))
        dvec_ref[1] += _colsum8(dl * xh)
        dvec_ref[2] += _colsum8(dl)
        dxh = dl * gain
        dub2 = rs * (dxh - jnp.mean(dxh, axis=-1, keepdims=True)
                     - xh * jnp.mean(dxh * xh, axis=-1, keepdims=True))
        dvec_ref[0] += _colsum8(dub2)
        dmid_ref[:, 2 * A_W:3 * A_W] = dub2
        for cc in range(A_W // _CW):
            c0 = cc * _CW
            ub = ub_s[:, c0:c0 + _CW]
            d = dmid_ref[:, 2 * A_W + c0:2 * A_W + c0 + _CW]
            for k in range(B_CONV):
                dwb_ref[k, :, c0:c0 + _CW] += _colsum8(d * _at(ub, k - B_CONV // 2))
    vec = pl.BlockSpec((1, A_W), lambda i: (0, 0))
    return pl.pallas_call(
        body, name="convmix_bwd1", grid=(R // TMR,),
        in_specs=[pl.BlockSpec((TMR, 2 * A_W), lambda i: (i, 0)), *_halo_specs(wp, R),
                  pl.BlockSpec((3, A_W), lambda i: (0, 0)), pl.BlockSpec((B_CONV, A_W), lambda i: (0, 0)),
                  vec, vec, vec],
        out_specs=[pl.BlockSpec((TMR, 3 * A_W), lambda i: (i, 0)),
                   pl.BlockSpec((3, 8, A_W), lambda i: (0, 0, 0)),
                   pl.BlockSpec((B_CONV, 8, A_W), lambda i: (0, 0, 0)),
                   pl.BlockSpec((3, 8, A_W), lambda i: (0, 0, 0))],
        out_shape=[jax.ShapeDtypeStruct((R, 3 * A_W), F32), jax.ShapeDtypeStruct((3, 8, A_W), F32),
                   jax.ShapeDtypeStruct((B_CONV, 8, A_W), F32), jax.ShapeDtypeStruct((3, 8, A_W), F32)],
        scratch_shapes=[pltpu.VMEM((TMR + 2 * HALO, A_W), F32), pltpu.VMEM((TMR, A_W), F32)],
        compiler_params=_params(48))(dyab, p, p, p, wa, wb, bias, lng, lnb)


def mixer_in_bwd(dmid, p, wa, wb, w_in, x, nw, mod_n, dxr, y, mod_g, kn, kg, T):
    R, wp = p.shape
    dm = x.shape[1]
    _, kb, nb = w_in.shape
    with_resid = y is not None
    contract_last = (((1,), (1,)), ((), ()))

    def body(*refs):
        if with_resid:
            (mp_ref, m_ref, mn_ref, p_ref, wa_ref, wb_ref, w_ref, x_ref, nw_ref, mnorm_ref, dxr_ref, y_ref, mg_ref,
             dp_ref, dx_ref, dy_ref, dmod_ref, dnw_ref, dg_ref) = refs
        else:
            (mp_ref, m_ref, mn_ref, p_ref, wa_ref, wb_ref, w_ref, x_ref, nw_ref, mnorm_ref, dxr_ref,
             dp_ref, dx_ref, dmod_ref, dnw_ref) = refs
        i = pl.program_id(0)
        seg = _seg(i, T)
        live = _halo_live(i, T, R)

        @pl.when(i == 0)
        def _():
            dmod_ref[...] = jnp.zeros_like(dmod_ref)
            dnw_ref[...] = jnp.zeros_like(dnw_ref)
            if with_resid:
                dg_ref[...] = jnp.zeros_like(dg_ref)

        def block(q):
            return lax.dot_general(dp_ref[:, q * nb:(q + 1) * nb], w_ref[q], contract_last,
                                   preferred_element_type=F32)
        mrefs = (mp_ref, m_ref, mn_ref)
        dp_ref[:, 0:A_W] = m_ref[:, 0:A_W].astype(BF16)
        dca = _ext(mrefs, A_W, A_W, live)
        dcin = (wa_ref[pl.ds(0, 1), :] * _at(dca, 1) + wa_ref[pl.ds(1, 1), :] * _at(dca, 0)
                + wa_ref[pl.ds(2, 1), :] * _at(dca, -1))
        dp_ref[:, A_W:2 * A_W] = (dcin * p_ref[:, 2 * A_W:3 * A_W].astype(F32)).astype(BF16)
        dp_ref[:, 2 * A_W:3 * A_W] = (dcin * p_ref[:, A_W:2 * A_W].astype(F32)).astype(BF16)
        dhv = block(0) + block(1)
        for cc in range(A_W // _CW):
            c0 = cc * _CW
            d = _ext(mrefs, 2 * A_W + c0, _CW, live)
            dub = jnp.zeros((TMR, _CW), F32)
            for k in range(B_CONV):
                dub = dub + wb_ref[pl.ds(k, 1), c0:c0 + _CW] * _at(d, B_CONV // 2 - k)
            vb = p_ref[:, 3 * A_W + c0:3 * A_W + c0 + _CW].astype(F32)
            s = _sigmoid(p_ref[:, 4 * A_W + c0:4 * A_W + c0 + _CW].astype(F32))
            dp_ref[:, 3 * A_W + c0:3 * A_W + c0 + _CW] = (dub * s).astype(BF16)
            dp_ref[:, 4 * A_W + c0:4 * A_W + c0 + _CW] = (dub * vb * s * (1.0 - s)).astype(BF16)
        dhv = dhv + block(2) + block(3)
        sc = mnorm_ref[seg, pl.ds(kn + 1, 1), :]
        nwv = nw_ref[...]
        xv = x_ref[...]
        r = lax.rsqrt(jnp.mean(xv * xv, axis=-1, keepdims=True) + EPS)
        xh = xv * r
        dmod_ref[seg, 0] += _colsum8(dhv)
        dmod_ref[seg, 1] += _colsum8(dhv * (xh * nwv))
        dn = dhv * (1.0 + sc)
        dnw_ref[...] += _colsum8(dn * xh)
        dxh = dn * nwv
        dx = dxr_ref[...] + r * (dxh - xh * jnp.mean(dxh * xh, axis=-1, keepdims=True))
        dx_ref[...] = dx
        if with_resid:
            dy_ref[...] = (mg_ref[seg, pl.ds(kg, 1), :] * dx).astype(BF16)
            dg_ref[seg] += _colsum8(dx * y_ref[...])
    assert 2 * nb <= 3 * A_W and N_CHIP * nb == wp
    tile = pl.BlockSpec((TMR, dm), lambda i: (i, 0))
    modspec = pl.BlockSpec((2, 6, dm), lambda i: (0, 0, 0))
    in_specs = [*_halo_specs(3 * A_W, R), pl.BlockSpec((TMR, wp), lambda i: (i, 0)),
                pl.BlockSpec((3, A_W), lambda i: (0, 0)), pl.BlockSpec((B_CONV, A_W), lambda i: (0, 0)),
                pl.BlockSpec((N_CHIP, kb, nb), lambda i: (0, 0, 0), pipeline_mode=pl.Buffered(1)),
                tile, pl.BlockSpec((1, dm), lambda i: (0, 0)), modspec, tile]
    dp_spec = pl.BlockSpec((TMR, wp), lambda i: (i, 0))
    acc_specs = [pl.BlockSpec((2, 2, 8, dm), lambda i: (0, 0, 0, 0)), pl.BlockSpec((8, dm), lambda i: (0, 0))]
    acc_shapes = [jax.ShapeDtypeStruct((2, 2, 8, dm), F32), jax.ShapeDtypeStruct((8, dm), F32)]
    dp_shape, dx_shape = jax.ShapeDtypeStruct((R, wp), BF16), jax.ShapeDtypeStruct((R, dm), F32)
    if with_resid:
        return pl.pallas_call(
            body, name="mixer_in_bwd", grid=(R // TMR,), in_specs=in_specs + [tile, modspec],
            out_specs=[dp_spec, tile, tile] + acc_specs + [pl.BlockSpec((2, 8, dm), lambda i: (0, 0, 0))],
            out_shape=[dp_shape, dx_shape, jax.ShapeDtypeStruct((R, dm), BF16)] + acc_shapes
            + [jax.ShapeDtypeStruct((2, 8, dm), F32)],
            compiler_params=_params(48))(dmid, dmid, dmid, p, wa, wb, w_in, x, nw, mod_n, dxr, y, mod_g)
    return pl.pallas_call(
        body, name="mixer_in_bwd_first", grid=(R // TMR,), in_specs=in_specs,
        out_specs=[dp_spec, tile] + acc_specs, out_shape=[dp_shape, dx_shape] + acc_shapes,
        compiler_params=_params(48))(dmid, dmid, dmid, p, wa, wb, w_in, x, nw, mod_n, dxr)


def _rot_half(v):
    w = v.shape[-1]
    lane = lax.broadcasted_iota(jnp.int32, (1, w), 1)
    return jnp.where(lane % HEAD_DIM < HEAD_DIM // 2, pltpu.roll(v, w - HEAD_DIM // 2, 1),
                     pltpu.roll(v, HEAD_DIM // 2, 1))


def mm_qkv_rope(a, w, cs, sn):
    R = a.shape[0]
    _, kb, nb = w.shape
    wq = N_CHIP * nb
    tm = _row_tile(R)
    qw = N_HEADS * HEAD_DIM
    kw = (wq - qw) // 2
    scale = HEAD_DIM ** -0.5

    def body(a_ref, w_ref, cs_ref, sn_ref, o_ref, x_ref):
        av = a_ref[...].astype(BF16)
        for q in range(N_CHIP):
            x_ref[:, q * nb:(q + 1) * nb] = jnp.dot(av, w_ref[q], preferred_element_type=F32)
        c, s = cs_ref[...], sn_ref[...]
        q = x_ref[:, 0:qw]
        o_ref[:, 0:qw] = ((q * jnp.tile(c, (1, qw // 128)) + _rot_half(q) * jnp.tile(s, (1, qw // 128)))
                          * scale).astype(BF16)
        k = x_ref[:, qw:qw + kw]
        o_ref[:, qw:qw + kw] = (k * jnp.tile(c, (1, kw // 128))
                                + _rot_half(k) * jnp.tile(s, (1, kw // 128))).astype(BF16)
        o_ref[:, qw + kw:] = x_ref[:, qw + kw:].astype(BF16)
    tab = pl.BlockSpec((tm, 128), lambda i: (i, 0))
    return pl.pallas_call(
        body, name="mm_qkv_rope", grid=(R // tm,),
        in_specs=[pl.BlockSpec((tm, kb), lambda i: (i, 0)),
                  pl.BlockSpec((N_CHIP, kb, nb), lambda i: (0, 0, 0), pipeline_mode=pl.Buffered(1)), tab, tab],
        out_specs=pl.BlockSpec((tm, wq), lambda i: (i, 0)),
        out_shape=jax.ShapeDtypeStruct((R, wq), BF16), scratch_shapes=[pltpu.VMEM((tm, wq), F32)],
        compiler_params=_params(48))(a, w, cs, sn)


def attn_in_bwd(dq, dks, dvs, dkc, dvc, cs, sn, w, x, nw, mod_n, dxr, y, mod_g, kn, kg, T):
    R, qw = dq.shape
    kw = dkc.shape[1]
    dm = x.shape[1]
    _, kb, nbw = w.shape
    nb = R // QB
    nl = T // QB
    scale = HEAD_DIM ** -0.5
    contract_last = (((1,), (1,)), ((), ()))

    def body(dq_ref, kp_ref, ko_ref, kn_ref, vp_ref, vo_ref, vn_ref, kc_ref, vc_ref, cs_ref, sn_ref,
             w_ref, x_ref, nw_ref, mnorm_ref, dxr_ref, y_ref, mg_ref,
             o_ref, dx_ref, dy_ref, dmod_ref, dnw_ref, dg_ref):
        b = pl.program_id(0)
        seg = (b >= nl).astype(jnp.int32)

        @pl.when(b == 0)
        def _():
            dmod_ref[...] = jnp.zeros_like(dmod_ref)
            dnw_ref[...] = jnp.zeros_like(dnw_ref)
            dg_ref[...] = jnp.zeros_like(dg_ref)
        c, s = cs_ref[...], sn_ref[...]
        has_next = (b + 1 < nb).astype(F32)
        has_prev = (b >= 1).astype(F32)
        is_ctx = (b >= nl).astype(F32)
        g = dq_ref[...] * scale
        o_ref[:, 0:qw] = (g * jnp.tile(c, (1, qw // 128)) + _rot_half(g * jnp.tile(s, (1, qw // 128)))).astype(BF16)
        g = ko_ref[...] + kp_ref[...] * has_next + kn_ref[...] * has_prev + kc_ref[...] * is_ctx
        o_ref[:, qw:qw + kw] = (g * jnp.tile(c, (1, kw // 128))
                                + _rot_half(g * jnp.tile(s, (1, kw // 128)))).astype(BF16)
        o_ref[:, qw + kw:] = (vo_ref[...] + vp_ref[...] * has_next + vn_ref[...] * has_prev
                              + vc_ref[...] * is_ctx).astype(BF16)
        dhv = None
        for q in range(N_CHIP):
            t = lax.dot_general(o_ref[:, q * nbw:(q + 1) * nbw], w_ref[q], contract_last,
                                preferred_element_type=F32)
            dhv = t if dhv is None else dhv + t
        sc = mnorm_ref[seg, pl.ds(kn + 1, 1), :]
        nwv = nw_ref[...]
        xv = x_ref[...]
        r = lax.rsqrt(jnp.mean(xv * xv, axis=-1, keepdims=True) + EPS)
        xh = xv * r
        dmod_ref[seg, 0] += _colsum8(dhv)
        dmod_ref[seg, 1] += _colsum8(dhv * (xh * nwv))
        dn = dhv * (1.0 + sc)
        dnw_ref[...] += _colsum8(dn * xh)
        dxh = dn * nwv
        dx = dxr_ref[...] + r * (dxh - xh * jnp.mean(dxh * xh, axis=-1, keepdims=True))
        dx_ref[...] = dx
        dy_ref[...] = (mg_ref[seg, pl.ds(kg, 1), :] * dx).astype(BF16)
        dg_ref[seg] += _colsum8(dx * y_ref[...])
    own = pl.BlockSpec((QB, kw), lambda b: (b, 0))
    from_next = pl.BlockSpec((QB, kw), lambda b: (jnp.minimum(b + 1, nb - 1), 0))
    from_prev = pl.BlockSpec((QB, kw), lambda b: (jnp.maximum(b - 1, 0), 0))
    ctx = pl.BlockSpec((QB, kw), lambda b: (jnp.maximum(b - nl, 0), 0))
    tab = pl.BlockSpec((QB, 128), lambda b: (b, 0))
    tile = pl.BlockSpec((QB, dm), lambda b: (b, 0))
    modspec = pl.BlockSpec((2, 6, dm), lambda b: (0, 0, 0))
    return pl.pallas_call(
        body, name="attn_in_bwd", grid=(nb,),
        in_specs=[pl.BlockSpec((QB, qw), lambda b: (b, 0)), from_next, own, from_prev, from_next, own, from_prev,
                  ctx, ctx, tab, tab,
                  pl.BlockSpec((N_CHIP, kb, nbw), lambda b: (0, 0, 0), pipeline_mode=pl.Buffered(1)),
                  tile, pl.BlockSpec((1, dm), lambda b: (0, 0)), modspec, tile, tile, modspec],
        out_specs=[pl.BlockSpec((QB, qw + 2 * kw), lambda b: (b, 0)), tile, tile,
                   pl.BlockSpec((2, 2, 8, dm), lambda b: (0, 0, 0, 0)), pl.BlockSpec((8, dm), lambda b: (0, 0)),
                   pl.BlockSpec((2, 8, dm), lambda b: (0, 0, 0))],
        out_shape=[jax.ShapeDtypeStruct((R, qw + 2 * kw), BF16), jax.ShapeDtypeStruct((R, dm), F32),
                   jax.ShapeDtypeStruct((R, dm), BF16), jax.ShapeDtypeStruct((2, 2, 8, dm), F32),
                   jax.ShapeDtypeStruct((8, dm), F32), jax.ShapeDtypeStruct((2, 8, dm), F32)],
        compiler_params=_params(48))(
            dq, dks[0], dks[1], dks[2], dvs[0], dvs[1], dvs[2], dkc, dvc, cs, sn, w, x, nw, mod_n, dxr, y, mod_g)


def _attn_specs(T, R):
    nl = T // QB
    qcols = N_HEADS * HEAD_DIM // 128
    kcols = 2

    def band(col0, shift):
        return pl.BlockSpec((QB, 128), lambda jj, b: (jnp.clip(b + shift, 0, nl - 1), col0 + jj))

    def ctx(col0):
        return pl.BlockSpec((R - T, 128), lambda jj, b: (T // (R - T), col0 + jj))
    q = pl.BlockSpec((QB, 512), lambda jj, b: (b, jj))
    k0, v0 = qcols, qcols + kcols
    return q, [band(k0, -1), band(k0, 0), band(k0, 1), ctx(k0)], [band(v0, -1), band(v0, 0), band(v0, 1), ctx(v0)]


def _attn_common(T, R):
    nl = T // QB
    nk = 3 * QB + (R - T)

    def low_lanes():
        return lax.broadcasted_iota(jnp.int32, (1, 128), 1) < HEAD_DIM

    def dup(v, par):
        low = low_lanes()
        vf = v.astype(F32)
        r = pltpu.roll(vf, HEAD_DIM, 1)
        return (jnp.where(low, vf, r) if par == 0 else jnp.where(low, r, vf)).astype(BF16)

    def stack(ref, par):
        low = low_lanes()
        pa = ref[:, (2 * par) * 128:(2 * par + 1) * 128].astype(BF16)
        pb = ref[:, (2 * par + 1) * 128:(2 * par + 2) * 128].astype(BF16)
        zero = jnp.zeros_like(pa)
        return jnp.concatenate([jnp.where(low, pa, zero), jnp.where(low, zero, pa),
                                jnp.where(low, pb, zero), jnp.where(low, zero, pb)], axis=0)

    def unstack(v):
        low = low_lanes()
        return (jnp.where(low, v[0:QB], v[QB:2 * QB]), jnp.where(low, v[2 * QB:3 * QB], v[3 * QB:4 * QB]))

    def mask_of(b):
        col = lax.broadcasted_iota(jnp.int32, (1, nk), 1)
        gone = (((col < QB) & (b == 0)) | ((col >= 2 * QB) & (col < 3 * QB) & (b == nl - 1))
                | ((col < 3 * QB) & (b >= nl)))
        return jnp.where(gone, NEG_INF, 0.0)

    def sink_col(sink_ref, first):
        blk = lax.broadcasted_iota(jnp.int32, (4 * QB, 1), 0) // QB
        out = jnp.zeros((4 * QB, 1), F32) + sink_ref[first]
        for h in range(1, 4):
            out = jnp.where(blk == h, sink_ref[first + h], out)
        return out

    def scores(qs, kd, mask, sink):
        s = lax.dot_general(qs, kd, (((1,), (1,)), ((), ())), preferred_element_type=F32) + mask
        m = jnp.maximum(jnp.max(s, axis=-1, keepdims=True), sink)
        e = jnp.exp(s - m)
        es = jnp.exp(sink - m)
        return e, es, 1.0 / (jnp.sum(e, axis=-1, keepdims=True) + es)
    return low_lanes, dup, stack, unstack, mask_of, sink_col, scores


def window_bias(T, R):
    nk = 3 * QB + (R - T)
    row = jnp.arange(QB)[:, None]
    col = jnp.arange(nk)[None, :]
    near = (jnp.abs(col - QB - row) <= WINDOW) | (col >= 3 * QB)
    return jnp.tile(jnp.where(near, 0.0, NEG_INF).astype(F32), (4, 1))


def _probs_spec(nk):
    return pl.BlockSpec((None, None, 2, 4 * QB, nk + 128), lambda jj, b: (jj, b, 0, 0, 0))


def attn_fwd(qkvr, sinks, bias, T):
    R = qkvr.shape[0]
    nk = bias.shape[1]
    qspec, kspecs, vspecs = _attn_specs(T, R)
    _, dup, stack, unstack, mask_of, sink_col, scores = _attn_common(T, R)

    def body(q_ref, kp, ko, kn, kc, vp, vo, vn, vc, sink_ref, bias_ref, o_ref, p_ref):
        jj, b = pl.program_id(0), pl.program_id(1)
        mask = bias_ref[...] + mask_of(b)
        k_all = jnp.concatenate([kp[...], ko[...], kn[...], kc[...]], axis=0)
        v_all = jnp.concatenate([vp[...], vo[...], vn[...], vc[...]], axis=0)
        for par in range(2):
            kd, vd = dup(k_all, par), dup(v_all, par)
            e, es, rz = scores(stack(q_ref, par), kd, mask, sink_col(sink_ref, jj * 8 + par * 4))
            p = (e * rz).astype(BF16)
            p_ref[par, :, 0:nk] = p
            p_ref[par, :, nk:nk + 128] = jnp.broadcast_to(es * rz, (4 * QB, 128)).astype(BF16)
            o = jnp.dot(p, vd, preferred_element_type=F32)
            pa, pb = unstack(o)
            o_ref[:, (2 * par) * 128:(2 * par + 1) * 128] = pa.astype(BF16)
            o_ref[:, (2 * par + 1) * 128:(2 * par + 2) * 128] = pb.astype(BF16)
    return pl.pallas_call(
        body, name="attn_fwd", grid=(2, R // QB),
        in_specs=[qspec, *kspecs, *vspecs, pl.BlockSpec(memory_space=pltpu.SMEM),
                  pl.BlockSpec(bias.shape, lambda jj, b: (0, 0))],
        out_specs=[pl.BlockSpec((QB, 512), lambda jj, b: (b, jj)), _probs_spec(nk)],
        out_shape=[jax.ShapeDtypeStruct((R, N_HEADS * HEAD_DIM), BF16),
                   jax.ShapeDtypeStruct((2, R // QB, 2, 4 * QB, nk + 128), BF16)],
        compiler_params=_params(48))(qkvr, *([qkvr] * 8), sinks, bias)


def attn_bwd(qkvr, do, probs, T):
    R = qkvr.shape[0]
    tc = R - T
    nk = probs.shape[-1] - 128
    qspec, kspecs, vspecs = _attn_specs(T, R)
    _, dup, stack, unstack, _, _, _ = _attn_common(T, R)
    contract_rows = (((0,), (0,)), ((), ()))
    contract_last = (((1,), (1,)), ((), ()))

    def body(q_ref, kp, ko, kn, kc, vp, vo, vn, vc, do_ref, p_ref,
             dq_ref, dkp, dko, dkn, dvp, dvo, dvn, dkc_ref, dvc_ref, dsink_ref):
        jj, b = pl.program_id(0), pl.program_id(1)

        @pl.when((jj == 0) & (b == 0))
        def _():
            dsink_ref[...] = jnp.zeros_like(dsink_ref)

        @pl.when(b == 0)
        def _():
            dkc_ref[...] = jnp.zeros_like(dkc_ref)
            dvc_ref[...] = jnp.zeros_like(dvc_ref)
        k_all = jnp.concatenate([kp[...], ko[...], kn[...], kc[...]], axis=0)
        v_all = jnp.concatenate([vp[...], vo[...], vn[...], vc[...]], axis=0)
        lane = lax.broadcasted_iota(jnp.int32, (8, 128), 1)
        srow = lax.broadcasted_iota(jnp.int32, (8, 128), 0)
        dk_fold, dv_fold = [], []
        for par in range(2):
            kd, vd = dup(k_all, par), dup(v_all, par)
            first = jj * 8 + par * 4
            qs, dos = stack(q_ref, par), stack(do_ref, par)
            p16 = p_ref[par, :, 0:nk]
            p = p16.astype(F32)
            ps = jnp.max(p_ref[par, :, nk:nk + 128].astype(F32), axis=-1, keepdims=True)
            dp = lax.dot_general(dos, vd, contract_last, preferred_element_type=F32)
            delta = jnp.sum(p * dp, axis=-1, keepdims=True)
            ds = (p * (dp - delta)).astype(BF16)
            t = ps * delta
            for h in range(4):
                dsink = -jnp.sum(t[h * QB:(h + 1) * QB])
                dsink_ref[...] += jnp.where((lane == first + h) & (srow == 0), dsink, 0.0)
            pa, pb = unstack(jnp.dot(ds, kd, preferred_element_type=F32))
            dq_ref[:, (2 * par) * 128:(2 * par + 1) * 128] = pa
            dq_ref[:, (2 * par + 1) * 128:(2 * par + 2) * 128] = pb
            dk_t = lax.dot_general(qs, ds, contract_rows, preferred_element_type=F32)
            dv_t = lax.dot_general(dos, p16, contract_rows, preferred_element_type=F32)
            dk_fold.append(dk_t + pltpu.roll(dk_t, HEAD_DIM, 0))
            dv_fold.append(dv_t + pltpu.roll(dv_t, HEAD_DIM, 0))
        low_rows = lax.broadcasted_iota(jnp.int32, (128, 1), 0) < HEAD_DIM
        dk = jnp.where(low_rows, dk_fold[0], dk_fold[1]).T
        dv = jnp.where(low_rows, dv_fold[0], dv_fold[1]).T
        dkp[...], dko[...], dkn[...] = dk[0:QB], dk[QB:2 * QB], dk[2 * QB:3 * QB]
        dvp[...], dvo[...], dvn[...] = dv[0:QB], dv[QB:2 * QB], dv[2 * QB:3 * QB]
        dkc_ref[...] += dk[3 * QB:]
        dvc_ref[...] += dv[3 * QB:]
    blk = pl.BlockSpec((QB, 128), lambda jj, b: (b, jj))
    cblk = pl.BlockSpec((tc, 128), lambda jj, b: (0, jj))
    part = jax.ShapeDtypeStruct((R, 256), F32)
    csum = jax.ShapeDtypeStruct((tc, 256), F32)
    outs = pl.pallas_call(
        body, name="attn_bwd", grid=(2, R // QB),
        in_specs=[qspec, *kspecs, *vspecs, pl.BlockSpec((QB, 512), lambda jj, b: (b, jj)), _probs_spec(nk)],
        out_specs=[pl.BlockSpec((QB, 512), lambda jj, b: (b, jj)), blk, blk, blk, blk, blk, blk, cblk, cblk,
                   pl.BlockSpec((8, 128), lambda jj, b: (0, 0))],
        out_shape=[jax.ShapeDtypeStruct((R, N_HEADS * HEAD_DIM), F32), part, part, part, part, part, part,
                   csum, csum, jax.ShapeDtypeStruct((8, 128), F32)],
        compiler_params=_params(48))(qkvr, *([qkvr] * 8), do, probs)
    return outs[0], outs[1:4], outs[4:7], outs[7], outs[8], outs[9]


def loss_head(x, nw, target, T):
    R, dm = x.shape
    nl = T // TMR

    def body(x_ref, nw_ref, t_ref, loss_ref, dx_ref, dnw_ref):
        i = pl.program_id(0)

        @pl.when(i == 0)
        def _():
            loss_ref[...] = jnp.zeros_like(loss_ref)
            dnw_ref[...] = jnp.zeros_like(dnw_ref)
        live = (i < nl).astype(F32)
        nwv = nw_ref[...]
        xv = x_ref[...]
        r = lax.rsqrt(jnp.mean(xv * xv, axis=-1, keepdims=True) + EPS)
        xh = xv * r
        err = xh * nwv - t_ref[...]
        per_row = jnp.mean(err * err, axis=-1, keepdims=True)
        loss_ref[...] += 0.5 * live * jnp.sum(per_row)
        dy = err * (live / dm)
        dnw_ref[...] += _colsum8(dy * xh)
        dxh = dy * nwv
        dx_ref[...] = r * (dxh - xh * jnp.mean(dxh * xh, axis=-1, keepdims=True))
    tile = pl.BlockSpec((TMR, dm), lambda i: (i, 0))
    return pl.pallas_call(
        body, name="loss_head", grid=(R // TMR,),
        in_specs=[tile, pl.BlockSpec((1, dm), lambda i: (0, 0)),
                  pl.BlockSpec((TMR, dm), lambda i: (jnp.minimum(i, nl - 1), 0))],
        out_specs=[pl.BlockSpec((8, 128), lambda i: (0, 0)), tile, pl.BlockSpec((8, dm), lambda i: (0, 0))],
        out_shape=[jax.ShapeDtypeStruct((8, 128), F32), jax.ShapeDtypeStruct((R, dm), F32),
                   jax.ShapeDtypeStruct((8, dm), F32)])(x, nw, target)


def adaln_fwd(cond, w_mod, b_mod):
    nl, dm, ns = w_mod.shape

    def body(c_ref, w_ref, b_ref, o_ref):
        cv = c_ref[...]
        s = (cv * _sigmoid(cv)).astype(BF16)
        o_ref[...] = jnp.dot(s, w_ref[...].astype(BF16), preferred_element_type=F32) + b_ref[...]
    return pl.pallas_call(
        body, name="adaln_fwd", grid=(nl,),
        in_specs=[pl.BlockSpec((16, dm), lambda l: (0, 0)), pl.BlockSpec((None, dm, ns), lambda l: (l, 0, 0)),
                  pl.BlockSpec((None, 1, ns), lambda l: (l, 0, 0))],
        out_specs=pl.BlockSpec((None, 16, ns), lambda l: (l, 0, 0)),
        out_shape=jax.ShapeDtypeStruct((nl, 16, ns), F32), compiler_params=_params(48))(cond, w_mod, b_mod)


def adaln_bwd(cond, dmod, w_mod):
    nl, dm, ns = w_mod.shape

    def body(c_ref, d_ref, w_ref, gw_ref, ds_ref):
        l = pl.program_id(0)

        @pl.when(l == 0)
        def _():
            ds_ref[...] = jnp.zeros_like(ds_ref)
        cv = c_ref[...]
        s = (cv * _sigmoid(cv)).astype(BF16)
        dv = d_ref[...].astype(BF16)
        gw_ref[...] = lax.dot_general(s, dv, (((0,), (0,)), ((), ())), preferred_element_type=F32)
        ds_ref[...] += lax.dot_general(dv, w_ref[...].astype(BF16), (((1,), (1,)), ((), ())),
                                       preferred_element_type=F32)
    return pl.pallas_call(
        body, name="adaln_bwd", grid=(nl,),
        in_specs=[pl.BlockSpec((16, dm), lambda l: (0, 0)), pl.BlockSpec((None, 16, ns), lambda l: (l, 0, 0)),
                  pl.BlockSpec((None, dm, ns), lambda l: (l, 0, 0))],
        out_specs=[pl.BlockSpec((None, dm, ns), lambda l: (l, 0, 0)), pl.BlockSpec((16, dm), lambda l: (0, 0))],
        out_shape=[jax.ShapeDtypeStruct((nl, dm, ns), F32), jax.ShapeDtypeStruct((16, dm), F32)],
        compiler_params=_params(48))(cond, dmod, w_mod)


def _me():
    return lax.axis_index("x"), lax.axis_index("y"), lax.axis_index("c")


def allgather8(block):
    m_per, n = block.shape

    def body(x_ref, out_ref, send_sems, recv_sems, local_sem):
        x, y, c = _me()
        me, sibling = (x, y, c), (x, y, 1 - c)
        chips = [(1 - x, y), (x, 1 - y), (1 - x, 1 - y)]

        def rows(px, py, pc):
            return out_ref.at[pl.ds((4 * px + 2 * py + pc) * m_per, m_per), :]

        def copy(k, blk, to, src=None):
            return pltpu.make_async_remote_copy(
                src_ref=rows(*blk) if src is None else src, dst_ref=rows(*blk),
                send_sem=send_sems.at[k], recv_sem=recv_sems.at[k], device_id=to, device_id_type=MESH)
        mine = pltpu.make_async_copy(x_ref, rows(*me), local_sem)
        mine.start()
        first = [copy(0, me, sibling, src=x_ref)]
        first += [copy(1 + j, me, (*chip, c), src=x_ref) for j, chip in enumerate(chips)]
        for cp in first:
            cp.start()
        passed = [copy(4 + j, (*chip, c), sibling) for j, chip in enumerate(chips)]
        for j, chip in enumerate(chips):
            copy(1 + j, (*chip, c), me).wait_recv()
            passed[j].start()
        copy(0, sibling, me).wait_recv()
        for j, chip in enumerate(chips):
            copy(4 + j, (*chip, 1 - c), me).wait_recv()
        for cp in first + passed:
            cp.wait_send()
        mine.wait()
    return pl.pallas_call(
        body, name="allgather8",
        out_shape=jax.ShapeDtypeStruct((N_DEV * m_per, n), block.dtype),
        in_specs=[pl.BlockSpec(memory_space=pltpu.VMEM)],
        out_specs=pl.BlockSpec(memory_space=pltpu.VMEM),
        scratch_shapes=[pltpu.SemaphoreType.DMA((7,)), pltpu.SemaphoreType.DMA((7,)), pltpu.SemaphoreType.DMA],
        compiler_params=_params(48))(block)


def _other_chips(x, y):
    return [(1 - x, y), (x, 1 - y), (1 - x, 1 - y)]


_HBM = pl.BlockSpec(memory_space=pltpu.HBM)
_SEM = pl.BlockSpec(memory_space=pltpu.SEMAPHORE)
_ANY = pl.BlockSpec(memory_space=pl.ANY)
_EFFECT = pltpu.SideEffectType.DATAFLOW_SIDE_EFFECTING


def _in_hbm(v):
    return pltpu.with_memory_space_constraint(v, pltpu.HBM)


def cast_into_slot(w, chip_id):
    kb, nb = w.shape
    tr = _row_tile(kb)

    def body(chip_ref, w_ref, o_ref):
        del chip_ref
        o_ref[...] = w_ref[...].astype(BF16)
    return pl.pallas_call(
        body, name="cast_into_slot",
        grid_spec=pltpu.PrefetchScalarGridSpec(
            num_scalar_prefetch=1, grid=(kb // tr,),
            in_specs=[pl.BlockSpec((tr, nb), lambda i, chip: (i, 0))],
            out_specs=pl.BlockSpec((None, tr, nb), lambda i, chip: (chip[0], i, 0))),
        out_shape=jax.ShapeDtypeStruct((N_CHIP, kb, nb), BF16))(chip_id, w)


def _split_copies(mode, srcs, lands, send_sems, recv_sems):
    x, y, c = _me()
    out = []
    for t in range(len(lands)):
        for k, chip in enumerate(_other_chips(x, y)):
            if mode == "gather":
                src = dst = lands[t].at[2 * x + y]
                landed = lands[t].at[2 * chip[0] + chip[1]]
            else:
                src, dst, landed = srcs[t].at[2 * chip[0] + chip[1]], lands[t].at[k], lands[t].at[k]
            send = pltpu.make_async_remote_copy(src_ref=src, dst_ref=dst, send_sem=send_sems.at[3 * t + k],
                                                recv_sem=recv_sems.at[3 * t + k], device_id=(*chip, c),
                                                device_id_type=MESH)
            recv = pltpu.make_async_remote_copy(src_ref=src, dst_ref=landed, send_sem=send_sems.at[3 * t + k],
                                                recv_sem=recv_sems.at[3 * t + k], device_id=(*chip, c),
                                                device_id_type=MESH)
            out.append((send, recv))
    return out


def exchange_start(name, mode, srcs, lands, after):
    ns, nl = len(srcs), len(lands)
    na = ns + nl

    def body(*refs):
        src_refs, land_refs = refs[:ns], refs[ns:na]
        send_sems, recv_sems = refs[na + 1], refs[na + 2]
        token = refs[-1]
        for send, _ in _split_copies(mode, src_refs, land_refs, send_sems, recv_sems):
            send.start()
        token[...] = jnp.zeros_like(token)
    arrays = list(srcs) + list(lands)
    outs = pl.pallas_call(
        body, name=name,
        out_shape=(pltpu.SemaphoreType.DMA((3 * nl,)), pltpu.SemaphoreType.DMA((3 * nl,)),
                   *[pltpu.HBM(v.shape, v.dtype) for v in arrays], jax.ShapeDtypeStruct((8, 128), F32)),
        in_specs=[_HBM] * na + [_ANY],
        out_specs=(_SEM, _SEM, *[_HBM] * na, pl.BlockSpec(memory_space=pltpu.VMEM)),
        input_output_aliases={i: 2 + i for i in range(na)},
        compiler_params=pltpu.CompilerParams(has_side_effects=_EFFECT))(*[_in_hbm(v) for v in arrays], after)
    return outs[0], outs[1], list(outs[2:2 + ns]), list(outs[2 + ns:2 + na]), outs[-1]


def exchange_wait(name, mode, send_sems, recv_sems, srcs, lands, after):
    ns, nl = len(srcs), len(lands)
    na = ns + nl

    def body(*refs):
        for _, recv in _split_copies(mode, refs[:ns], refs[ns:na], refs[na], refs[na + 1]):
            recv.wait_send()
            recv.wait_recv()
    arrays = list(srcs) + list(lands)
    outs = pl.pallas_call(
        body, name=name,
        out_shape=[pltpu.HBM(v.shape, v.dtype) for v in arrays],
        in_specs=[_HBM] * na + [_SEM, _SEM, _ANY], out_specs=[_HBM] * na,
        input_output_aliases={i: i for i in range(na)},
        compiler_params=pltpu.CompilerParams(has_side_effects=_EFFECT))(*arrays, send_sems, recv_sems, after)
    return list(outs[:ns]), list(outs[ns:])


def swap_with_sibling(v):
    def body(v_ref, out_ref, send_sem, recv_sem):
        x, y, c = _me()
        cp = pltpu.make_async_remote_copy(src_ref=v_ref, dst_ref=out_ref, send_sem=send_sem, recv_sem=recv_sem,
                                          device_id=(x, y, 1 - c), device_id_type=MESH)
        cp.start()
        cp.wait()
    return pl.pallas_call(
        body, name="swap_with_sibling", out_shape=jax.ShapeDtypeStruct(v.shape, v.dtype),
        in_specs=[pl.BlockSpec(memory_space=pl.ANY)], out_specs=pl.BlockSpec(memory_space=pl.ANY),
        scratch_shapes=[pltpu.SemaphoreType.DMA, pltpu.SemaphoreType.DMA])(v)


def sum_slots(parts):
    n, rows, w = parts.shape
    tr = _row_tile(rows)

    def body(p_ref, o_ref):
        acc = p_ref[0].astype(F32)
        for k in range(1, n):
            acc = acc + p_ref[k].astype(F32)
        o_ref[...] = acc
    return pl.pallas_call(
        body, name="sum_slots", grid=(rows // tr,),
        in_specs=[pl.BlockSpec((n, tr, w), lambda i: (0, i, 0))], out_specs=pl.BlockSpec((tr, w), lambda i: (i, 0)),
        out_shape=jax.ShapeDtypeStruct((rows, w), F32), compiler_params=_params(48))(parts)


def sum_landed(landed, own, chip_id, layer, n_layers, buf):
    n, rows, w = landed.shape
    tr = _row_tile(rows)
    base = layer * (rows // tr)

    def compute(l_ref, g_ref, o_ref):
        acc = g_ref[...].astype(F32)
        for k in range(n):
            acc = acc + l_ref[k].astype(F32)
        o_ref[...] = acc
    in_specs = [pl.BlockSpec((n, tr, w), lambda i, chip: (0, i, 0)),
                pl.BlockSpec((None, tr, w), lambda i, chip: (chip[0], i, 0))]
    out_spec = pl.BlockSpec((tr, w), lambda i, chip: (base + i, 0))
    out_shape = jax.ShapeDtypeStruct((n_layers * rows, w), F32)
    if buf is None:
        def body(chip_ref, l_ref, g_ref, o_ref):
            del chip_ref
            compute(l_ref, g_ref, o_ref)
        return pl.pallas_call(
            body, name="sum_landed",
            grid_spec=pltpu.PrefetchScalarGridSpec(num_scalar_prefetch=1, grid=(rows // tr,), in_specs=in_specs,
                                                   out_specs=out_spec),
            out_shape=out_shape, compiler_params=_params(48))(chip_id, landed, own)

    def body(chip_ref, l_ref, g_ref, buf_ref, o_ref):
        del chip_ref, buf_ref
        compute(l_ref, g_ref, o_ref)
    return pl.pallas_call(
        body, name="sum_landed_into",
        grid_spec=pltpu.PrefetchScalarGridSpec(num_scalar_prefetch=1, grid=(rows // tr,),
                                               in_specs=in_specs + [_ANY], out_specs=out_spec),
        out_shape=out_shape, input_output_aliases={3: 0}, compiler_params=_params(48))(chip_id, landed, own, buf)


def adamw(w, ga, gb, m, v):
    rows, wd = w.shape
    tr = min(_row_tile(rows), 128)
    c1 = 1.0 / (1.0 - ADAM_B1 ** ADAM_STEP)
    c2 = 1.0 / (1.0 - ADAM_B2 ** ADAM_STEP)

    def update(wv, g, mv, vv, g_ref, d_ref, m_ref, v_ref):
        mn = ADAM_B1 * mv + (1.0 - ADAM_B1) * g
        vn = ADAM_B2 * vv + (1.0 - ADAM_B2) * (g * g)
        g_ref[...] = g
        m_ref[...] = mn
        v_ref[...] = vn
        d_ref[...] = -ADAM_LR * ((mn * c1) / (jnp.sqrt(vn * c2) + ADAM_EPS) + ADAM_WD * wv)
    tile = pl.BlockSpec((tr, wd), lambda i: (i, 0))
    out = jax.ShapeDtypeStruct((rows, wd), F32)
    if gb is None:
        def body(w_ref, ga_ref, m_ref, v_ref, g_out, d_out, m_out, v_out):
            update(w_ref[...], ga_ref[...], m_ref[...], v_ref[...], g_out, d_out, m_out, v_out)
        return pl.pallas_call(body, name="adamw", grid=(rows // tr,), in_specs=[tile] * 4,
                              out_specs=[tile] * 4, out_shape=[out] * 4)(w, ga, m, v)

    def body(w_ref, ga_ref, gb_ref, m_ref, v_ref, g_out, d_out, m_out, v_out):
        update(w_ref[...], ga_ref[...] + gb_ref[...], m_ref[...], v_ref[...], g_out, d_out, m_out, v_out)
    return pl.pallas_call(body, name="adamw_sum", grid=(rows // tr,), in_specs=[tile] * 5,
                          out_specs=[tile] * 4, out_shape=[out] * 4)(w, ga, gb, m, v)


def _rope_tables(T, R):
    rows = T // GRID_W
    row = jnp.repeat(jnp.arange(rows), GRID_W).astype(F32)
    col = jnp.tile(jnp.arange(GRID_W), rows).astype(F32)
    n_freq = HEAD_DIM // 4
    inv_freq = ROPE_THETA ** (-jnp.arange(n_freq, dtype=F32) / n_freq)
    ang = jnp.concatenate([row[:, None] * inv_freq, col[:, None] * inv_freq], axis=-1)
    cos, sin = jnp.cos(ang), jnp.sin(ang)
    cs = jnp.tile(cos, (1, 4))
    sn = jnp.tile(jnp.concatenate([-sin, sin], axis=-1), (1, 2))
    pad = R - T
    return (jnp.concatenate([cs, jnp.ones((pad, 128), F32)], axis=0),
            jnp.concatenate([sn, jnp.zeros((pad, 128), F32)], axis=0))


def _pack(parts, mult=8 * 128):
    flat = jnp.concatenate([p.reshape(-1).astype(F32) for p in parts])
    pad = (-flat.shape[0]) % mult
    return jnp.pad(flat, (0, pad)).reshape(-1, 128)


def _unpack(buf, shapes):
    flat = buf.reshape(-1)
    out, o = [], 0
    for s in shapes:
        n = 1
        for d in s:
            n *= d
        out.append(flat[o:o + n].reshape(s))
        o += n
    return out


def kernel(x, c, ctx, c_ctx, w_mod, b_mod, norm_mix, norm_ffn, w_in_ab, conv_a, conv_b, conv_b_bias, ln_b_gain, ln_b_bias, w_out_ab, w_qkv, w_o, sinks, w_up, w_conv_ffn, w_down, final_norm, loss_target, m_c_ctx, m_w_mod, m_b_mod, m_norm_mix, m_norm_ffn, m_w_in_ab, m_conv_a, m_conv_b, m_conv_b_bias, m_ln_b_gain, m_ln_b_bias, m_w_out_ab, m_w_qkv, m_w_o, m_sinks, m_w_up, m_w_conv_ffn, m_w_down, m_final_norm, v_c_ctx, v_w_mod, v_b_mod, v_norm_mix, v_norm_ffn, v_w_in_ab, v_conv_a, v_conv_b, v_conv_b_bias, v_ln_b_gain, v_ln_b_bias, v_w_out_ab, v_w_qkv, v_w_o, v_sinks, v_w_up, v_w_conv_ffn, v_w_down, v_final_norm):
    T, dm = x.shape[1], x.shape[2]
    tc = ctx.shape[1]
    R = T + tc
    depth = w_mod.shape[0]
    ax, ay, ac = lax.axis_index("x"), lax.axis_index("y"), lax.axis_index("c")
    chip = 2 * ax + ay
    dev = 4 * ax + 2 * ay + ac

    small_w = [conv_a, conv_b, w_conv_ffn]
    gathered = allgather8(_pack([c] + small_w)).reshape(N_DEV, -1)
    cond8 = gathered[:, :dm]
    off = dm
    full_small = []
    for wsh in small_w:
        n = wsh.size
        per_chip = gathered[0::2, off:off + n].reshape((N_CHIP,) + wsh.shape)
        full_small.append(jnp.concatenate([per_chip[q] for q in range(N_CHIP)], axis=-1))
        off += n
    conv_a_f, conv_b_f, w_conv_ffn_f = full_small
    cond = jnp.concatenate([cond8, c_ctx[None, :], jnp.zeros((7, dm), F32)], axis=0)

    ns_mod = w_mod.shape[2]
    b_mod_sh = lax.dynamic_slice_in_dim(b_mod, chip * ns_mod, ns_mod, axis=1)[:, None, :]
    mod_sh = adaln_fwd(cond, w_mod, b_mod_sh)
    mod_all = allgather8(mod_sh.reshape(depth * 16, ns_mod)).reshape(N_DEV, depth, 16, ns_mod)
    mod_full = jnp.concatenate([mod_all[2 * q] for q in range(N_CHIP)], axis=-1)
    mine = lax.dynamic_index_in_dim(mod_full, dev, axis=1, keepdims=False)
    mods = jnp.stack([mine, mod_full[:, 8]], axis=1).reshape(depth, 2, 6, dm)

    masters = {"w_in_ab": w_in_ab, "w_out_ab": w_out_ab, "w_qkv": w_qkv, "w_o": w_o, "w_up": w_up, "w_down": w_down}
    chip_id = chip.astype(jnp.int32).reshape(1)

    def half_weights(l, half):
        if half == 1:
            return [("w_up", l), ("w_down", l)]
        return [("w_in_ab", l // 2), ("w_out_ab", l // 2)] if l % 2 == 0 else [("w_qkv", l // 2), ("w_o", l // 2)]
    in_flight, after = {}, mods
    for l in range(depth):
        for half in range(2):
            lands = [cast_into_slot(masters[n][j], chip_id) for n, j in half_weights(l, half)]
            send_sems, recv_sems, _, lands, after = exchange_start(f"gather_start_{l}_{half}", "gather", [], lands, after)
            in_flight[l, half] = (send_sems, recv_sems, lands)
    mods = mods + after[0, 0]

    def gathered_weights(l, half, after):
        send_sems, recv_sems, lands = in_flight[l, half]
        _, landed = exchange_wait(f"gather_wait_{l}_{half}", "gather", send_sems, recv_sems, [], lands, after)
        return dict(zip([n for n, _ in half_weights(l, half)], landed))

    cs, sn = _rope_tables(T, R)
    bias = window_bias(T, R)
    sinks_flat = sinks.reshape(-1)

    xs = jnp.concatenate([x[0], ctx[0]], axis=0)
    saved, W = [], []
    h1 = norm_mod_fwd(xs, norm_mix[0][None], mods[0], 0, T)
    for l in range(depth):
        e = l // 2
        wl = gathered_weights(l, 0, h1)
        W.append(wl)
        s = {"x0": xs, "h1": h1}
        if l % 2 == 0:
            p = mm_nn(h1, wl["w_in_ab"], BF16)
            yab, y1, x1, h2 = mixer_fwd(p, conv_a_f[e], conv_b_f[e], conv_b_bias[e][None], ln_b_gain[e][None],
                                        ln_b_bias[e][None], wl["w_out_ab"], xs, norm_ffn[l][None], mods[l], 2, 3, T)
            s.update(p=p, mix=yab)
        else:
            qkvr = mm_qkv_rope(h1, wl["w_qkv"], cs, sn)
            att, probs = attn_fwd(qkvr, sinks_flat[e * N_HEADS:(e + 1) * N_HEADS], bias, T)
            s.update(qkvr=qkvr, mix=att, probs=probs)
            y1, x1, h2 = mm_resid_norm_fwd(att, wl["w_o"], xs, norm_ffn[l][None], mods[l], mods[l], 2, 3, T)
        wl.update(gathered_weights(l, 1, h2))
        u = mm_nn(h2, wl["w_up"], BF16)
        if l + 1 < depth:
            z, y2, xs, h1 = ffn_mid_fwd(u, w_conv_ffn_f[l], wl["w_down"], x1, norm_mix[l + 1][None], mods[l],
                                        mods[l + 1], 5, 0, T)
        else:
            z, y2, xs = ffn_mid_fwd(u, w_conv_ffn_f[l], wl["w_down"], x1, None, mods[l], None, 5, 0, T)
        s.update(y1=y1, x1=x1, h2=h2, u=u, z=z, y2=y2)
        saved.append(s)

    loss_part, dx, d_final = loss_head(xs, final_norm[None], loss_target[0], T)
    loss = lax.psum(loss_part[0, 0], ("x", "y", "c"))

    d_mods, d_norm_mix, d_norm_ffn = [None] * depth, [None] * depth, [None] * depth
    d_conv_a, d_conv_b, d_vecs, d_sinks, d_wc = [None] * 2, [None] * 2, [None] * 2, [None] * 2, [None] * depth
    dss1, dss2, dg1, dg2 = [None] * depth, [None] * depth, [None] * depth, [None] * depth
    scattering = {}

    def scatter(l, half, G, after):
        grads_h = [G[n] for n, _ in half_weights(l, half)]
        lands = [lax.empty((N_CHIP - 1, *g.shape[1:]), g.dtype) for g in grads_h]
        send_sems, recv_sems, grads_h, lands, token = exchange_start(
            f"scatter_start_{l}_{half}", "scatter", grads_h, lands, after)
        scattering[l, half] = (send_sems, recv_sems, grads_h, lands)
        return token

    dy2, dg2[depth - 1] = resid_bwd(dx, saved[depth - 1]["y2"], mods[depth - 1], 5, T)
    pending = 0.0
    for l in reversed(range(depth)):
        e = l // 2
        s, wl = saved[l], W[l]
        G = {}
        G["w_down"] = mm_tn(s["z"], dy2, "row", wl["w_down"])
        duc, d_wc[l] = ffn_mid_bwd(mm_nt(dy2, wl["w_down"], BF16), s["u"], w_conv_ffn_f[l] + pending, T)
        du, dx, dy1, dss2[l], d_norm_ffn[l], dg1[l] = ffn_up_bwd(
            duc, w_conv_ffn_f[l], wl["w_up"], s["x1"], norm_ffn[l][None], mods[l], dx, s["y1"], mods[l], 3, 2, T)
        G["w_up"] = mm_tn(s["h2"], du, "col", wl["w_up"])
        started = scatter(l, 1, G, du)[0, 0]
        if l % 2 == 0:
            G["w_out_ab"] = mm_tn(s["mix"], dy1, "row", wl["w_out_ab"])
            dyab = mm_nt(dy1, wl["w_out_ab"], F32)
            dmid, d_conv_a[e], d_conv_b[e], d_vecs[e] = convmix_bwd1(
                dyab, s["p"], conv_a_f[e] + started, conv_b_f[e], conv_b_bias[e][None], ln_b_gain[e][None],
                ln_b_bias[e][None], T)
            if l > 0:
                dp, dx, dy2, dss1[l], d_norm_mix[l], dg2[l - 1] = mixer_in_bwd(
                    dmid, s["p"], conv_a_f[e], conv_b_f[e], wl["w_in_ab"], s["x0"], norm_mix[l][None], mods[l], dx,
                    saved[l - 1]["y2"], mods[l - 1], 0, 5, T)
            else:
                dp, dx, dss1[l], d_norm_mix[l] = mixer_in_bwd(
                    dmid, s["p"], conv_a_f[e], conv_b_f[e], wl["w_in_ab"], s["x0"], norm_mix[l][None], mods[l], dx,
                    None, None, 0, 0, T)
            G["w_in_ab"] = mm_tn(s["h1"], dp, "col", wl["w_in_ab"])
        else:
            G["w_o"] = mm_tn(s["mix"], dy1, "row", wl["w_o"])
            datt = mm_nt(dy1, wl["w_o"], BF16)
            dq, dks, dvs, dkc, dvc, d_sinks[e] = attn_bwd(s["qkvr"], datt, s["probs"], T)
            dqkv, dx, dy2, dss1[l], d_norm_mix[l], dg2[l - 1] = attn_in_bwd(
                dq, dks, dvs, dkc, dvc, cs + started, sn, wl["w_qkv"], s["x0"], norm_mix[l][None], mods[l], dx,
                saved[l - 1]["y2"], mods[l - 1], 0, 5, T)
            G["w_qkv"] = mm_tn(s["h1"], dqkv, "col", wl["w_qkv"])
        token = scatter(l, 0, G, dx)
        pending = token[0, 0]
    grad_x = dx[:T][None]
    for l in range(depth):
        a1, a2 = dss1[l].sum(2), dss2[l].sum(2)
        d_mods[l] = jnp.stack([a1[:, 0], a1[:, 1], dg1[l].sum(1), a2[:, 0], a2[:, 1], dg2[l].sum(1)], axis=1)

    d_mods = jnp.stack(d_mods)
    summed_parts = [
        d_mods[:, 1],
        jnp.stack(d_norm_mix).sum(1), jnp.stack(d_norm_ffn).sum(1),
        jnp.stack(d_conv_a).sum(2), jnp.stack(d_conv_b).sum(2),
        jnp.stack(d_vecs).sum(2),
        jnp.stack(d_sinks)[:, 0, :N_HEADS],
        jnp.stack(d_wc).sum(2), d_final.sum(0)]
    summed_shapes = [p.shape for p in summed_parts]
    n_own = depth * 6 * dm
    pack = _pack([d_mods[:, 0]] + summed_parts)
    parts = allgather8(pack).reshape(N_DEV, -1, 128)
    total = sum_slots(parts)
    own_rows = parts.reshape(N_DEV, -1)[:, :n_own].reshape(N_DEV, depth, 6 * dm)
    (dmod_ctx, g_norm_mix, g_norm_ffn, g_conv_a, g_conv_b, g_vecs, g_sinks, g_wc, g_final) = _unpack(
        total.reshape(-1)[n_own:], summed_shapes)
    dmod_rows = jnp.concatenate([jnp.moveaxis(own_rows, 0, 1), dmod_ctx.reshape(depth, 1, 6 * dm),
                                 jnp.zeros((depth, 7, 6 * dm), F32)], axis=1)
    g_b_mod = dmod_rows.sum(1)
    dmod_sh = lax.dynamic_slice_in_dim(dmod_rows, chip * ns_mod, ns_mod, axis=2)
    g_w_mod, dsilu = adaln_bwd(cond, dmod_sh, w_mod)
    dsilu_all = allgather8(dsilu[8:16]).reshape(N_DEV, 8, dm)
    dsilu_ctx = sum_slots(dsilu_all[0::2])[0]
    sg = jax.nn.sigmoid(c_ctx)
    g_c_ctx = dsilu_ctx * (sg * (1.0 + c_ctx * (1.0 - sg)))

    def shard_cols(full, width):
        return lax.dynamic_slice_in_dim(full, chip * width, width, axis=full.ndim - 1)
    g_conv_a_s = shard_cols(g_conv_a, conv_a.shape[-1])
    g_conv_b_s = shard_cols(g_conv_b, conv_b.shape[-1])
    g_wc_s = shard_cols(g_wc, w_conv_ffn.shape[-1])

    grads, deltas, new_m, new_v = {}, {}, {}, {}

    def step_2d(name, wv, ga, gb, mv, vv):
        shp = wv.shape
        r2 = lambda t: t.reshape(-1, shp[-1])
        g, d, mn, vn = adamw(r2(wv), r2(ga), None if gb is None else r2(gb), r2(mv), r2(vv))
        grads[name], deltas[name], new_m[name], new_v[name] = (t.reshape(shp) for t in (g, d, mn, vn))

    sums = {n: None for n in masters}
    for l in reversed(range(depth)):
        for half in (1, 0):
            send_sems, recv_sems, grads_h, lands = scattering[l, half]
            grads_h, landed = exchange_wait(f"scatter_wait_{l}_{half}", "scatter", send_sems, recv_sems, grads_h,
                                            lands, token)
            for (n, j), own, arr in zip(half_weights(l, half), grads_h, landed):
                sums[n] = sum_landed(arr, own, chip_id, j, masters[n].shape[0], sums[n])
    moments = {"w_in_ab": (m_w_in_ab, v_w_in_ab), "w_out_ab": (m_w_out_ab, v_w_out_ab),
               "w_qkv": (m_w_qkv, v_w_qkv), "w_o": (m_w_o, v_w_o), "w_up": (m_w_up, v_w_up),
               "w_down": (m_w_down, v_w_down)}
    for name, wv in masters.items():
        other = swap_with_sibling(sums[name])
        step_2d(name, wv, sums[name].reshape(wv.shape), other.reshape(wv.shape), *moments[name])
    step_2d("w_mod", w_mod, g_w_mod, None, m_w_mod, v_w_mod)

    small = [("c_ctx", c_ctx, g_c_ctx, m_c_ctx, v_c_ctx), ("b_mod", b_mod, g_b_mod, m_b_mod, v_b_mod),
             ("norm_mix", norm_mix, g_norm_mix, m_norm_mix, v_norm_mix),
             ("norm_ffn", norm_ffn, g_norm_ffn, m_norm_ffn, v_norm_ffn),
             ("conv_a", conv_a, g_conv_a_s, m_conv_a, v_conv_a), ("conv_b", conv_b, g_conv_b_s, m_conv_b, v_conv_b),
             ("conv_b_bias", conv_b_bias, g_vecs[:, 0], m_conv_b_bias, v_conv_b_bias),
             ("ln_b_gain", ln_b_gain, g_vecs[:, 1], m_ln_b_gain, v_ln_b_gain),
             ("ln_b_bias", ln_b_bias, g_vecs[:, 2], m_ln_b_bias, v_ln_b_bias),
             ("sinks", sinks, g_sinks, m_sinks, v_sinks),
             ("w_conv_ffn", w_conv_ffn, g_wc_s, m_w_conv_ffn, v_w_conv_ffn),
             ("final_norm", final_norm, g_final, m_final_norm, v_final_norm)]
    shapes = [t[1].shape for t in small]
    packed = [_pack([t[k] for t in small]) for k in (1, 2, 3, 4)]
    n_real = sum(t[1].size for t in small)
    lane_id = jnp.arange(packed[3].size).reshape(packed[3].shape)
    packed[3] = jnp.where(lane_id < n_real, packed[3], 1.0)
    outs = adamw(packed[0], packed[1], None, packed[2], packed[3])
    for (name, *_), g, d, mn, vn in zip(small, *[_unpack(o, shapes) for o in outs]):
        grads[name], deltas[name], new_m[name], new_v[name] = g, d, mn, vn

    order = ["c_ctx", "w_mod", "b_mod", "norm_mix", "norm_ffn", "w_in_ab", "conv_a", "conv_b", "conv_b_bias",
             "ln_b_gain", "ln_b_bias", "w_out_ab", "w_qkv", "w_o", "sinks", "w_up", "w_conv_ffn", "w_down",
             "final_norm"]
    return (loss, grad_x, *[grads[n] for n in order], *[deltas[n] for n in order],
            *[new_m[n] for n in order], *[new_v[n] for n in order])
```

```python
import jax
import jax.numpy as jnp
from jax import lax
from jax.experimental import pallas as pl
from jax.experimental.pallas import tpu as pltpu

F32 = jnp.float32
BF16 = jnp.bfloat16
MESH = pl.DeviceIdType.MESH

EPS = 1e-6
NEG_INF = -1e30
GRID_W = 64
HEAD_DIM = 64
N_HEADS = 16
WINDOW = 128
QB = 128
ROPE_THETA = 10000.0
A_W = 512
B_CONV = 31
D_FF = 2816
ADAM_LR, ADAM_B1, ADAM_B2, ADAM_EPS, ADAM_WD, ADAM_STEP = 0.001, 0.9, 0.999, 1e-8, 0.01, 10

TMR = 256
HALO = 16
N_DEV = 8
N_CHIP = 4


def _params(vmem_mb=None):
    if vmem_mb is None:
        return pltpu.CompilerParams()
    return pltpu.CompilerParams(vmem_limit_bytes=vmem_mb * 1024 * 1024)


def _row_tile(rows, cap=768):
    for t in (2816, 1408, 768, 704, 512, 384, 256, 128, 64, 32, 16, 8):
        if t <= cap and rows % t == 0:
            return t
    raise ValueError(rows)


def _colsum8(v):
    r, c = v.shape
    return v.reshape(r // 8, 8, c).sum(axis=0)


def _sigmoid(v):
    return 0.5 * jnp.tanh(0.5 * v) + 0.5


def mm_nn(a, w, out_dtype):
    R = a.shape[0]
    _, kb, nb = w.shape
    tm = _row_tile(R)

    def body(a_ref, w_ref, o_ref):
        av = a_ref[...].astype(BF16)
        for q in range(N_CHIP):
            o_ref[:, q * nb:(q + 1) * nb] = jnp.dot(av, w_ref[q], preferred_element_type=F32).astype(o_ref.dtype)
    return pl.pallas_call(
        body, name="mm_nn_col", grid=(R // tm,),
        in_specs=[pl.BlockSpec((tm, kb), lambda i: (i, 0)),
                  pl.BlockSpec((N_CHIP, kb, nb), lambda i: (0, 0, 0), pipeline_mode=pl.Buffered(1))],
        out_specs=pl.BlockSpec((tm, N_CHIP * nb), lambda i: (i, 0)),
        out_shape=jax.ShapeDtypeStruct((R, N_CHIP * nb), out_dtype),
        compiler_params=_params(48))(a, w)


def mm_nt(d, w, out_dtype):
    R = d.shape[0]
    _, kb, nb = w.shape
    tm = _row_tile(R)
    contract_last = (((1,), (1,)), ((), ()))
    resident = pl.BlockSpec((N_CHIP, kb, nb), lambda i: (0, 0, 0), pipeline_mode=pl.Buffered(1))

    def body(d_ref, w_ref, o_ref):
        wv = w_ref[...].reshape(N_CHIP * kb, nb)
        o_ref[...] = lax.dot_general(d_ref[...].astype(BF16), wv, contract_last,
                                     preferred_element_type=F32).astype(o_ref.dtype)
    return pl.pallas_call(
        body, name="mm_nt_row", grid=(R // tm,),
        in_specs=[pl.BlockSpec((tm, nb), lambda i: (i, 0)), resident],
        out_specs=pl.BlockSpec((tm, N_CHIP * kb), lambda i: (i, 0)),
        out_shape=jax.ShapeDtypeStruct((R, N_CHIP * kb), out_dtype),
        compiler_params=_params(48))(d, w)


def mm_tn(a, d, kind, like):
    R = a.shape[0]
    _, kb, nb = like.shape
    tm = _row_tile(R, 1408 if kind == "col" else 768)
    nsteps = R // tm
    contract_rows = (((0,), (0,)), ((), ()))
    out_shape = jax.ShapeDtypeStruct(like.shape, BF16)

    def accumulate(a_ref, d_ref, acc_ref):
        @pl.when(pl.program_id(1) == 0)
        def _():
            acc_ref[...] = jnp.zeros_like(acc_ref)
        acc_ref[...] += lax.dot_general(a_ref[...].astype(BF16), d_ref[...].astype(BF16), contract_rows,
                                        preferred_element_type=F32)
    if kind == "col":
        def body(a_ref, d_ref, o_ref, acc_ref):
            accumulate(a_ref, d_ref, acc_ref)

            @pl.when(pl.program_id(1) == nsteps - 1)
            def _():
                o_ref[...] = acc_ref[...].astype(BF16)
        return pl.pallas_call(
            body, name="mm_tn_col", grid=(N_CHIP, nsteps),
            in_specs=[pl.BlockSpec((tm, kb), lambda q, i: (i, 0)), pl.BlockSpec((tm, nb), lambda q, i: (i, q))],
            out_specs=pl.BlockSpec((None, kb, nb), lambda q, i: (q, 0, 0)), out_shape=out_shape,
            scratch_shapes=[pltpu.VMEM((kb, nb), F32)], compiler_params=_params(48))(a, d)
    tn = 512

    def body(a_ref, d_ref, o_ref, acc_ref):
        accumulate(a_ref, d_ref, acc_ref)

        @pl.when(pl.program_id(1) == nsteps - 1)
        def _():
            o_ref[...] = acc_ref[...].astype(BF16).reshape(N_CHIP, kb, tn)
    return pl.pallas_call(
        body, name="mm_tn_row", grid=(nb // tn, nsteps),
        in_specs=[pl.BlockSpec((tm, N_CHIP * kb), lambda n, i: (i, 0)), pl.BlockSpec((tm, tn), lambda n, i: (i, n))],
        out_specs=pl.BlockSpec((N_CHIP, kb, tn), lambda n, i: (0, 0, n)), out_shape=out_shape,
        scratch_shapes=[pltpu.VMEM((N_CHIP * kb, tn), F32)], compiler_params=_params(48))(a, d)


def _seg(i, T):
    return (i >= T // TMR).astype(jnp.int32)


def norm_mod_fwd(x, nw, mod, k, T):
    R, dm = x.shape

    def body(x_ref, nw_ref, mod_ref, h_ref):
        seg = _seg(pl.program_id(0), T)
        sh = mod_ref[seg, pl.ds(k, 1), :]
        sc = mod_ref[seg, pl.ds(k + 1, 1), :]
        xv = x_ref[...]
        r = lax.rsqrt(jnp.mean(xv * xv, axis=-1, keepdims=True) + EPS)
        h_ref[...] = ((xv * r * nw_ref[...]) * (1.0 + sc) + sh).astype(BF16)
    return pl.pallas_call(
        body, name="norm_mod_fwd", grid=(R // TMR,),
        in_specs=[pl.BlockSpec((TMR, dm), lambda i: (i, 0)),
                  pl.BlockSpec((1, dm), lambda i: (0, 0)),
                  pl.BlockSpec((2, 6, dm), lambda i: (0, 0, 0))],
        out_specs=pl.BlockSpec((TMR, dm), lambda i: (i, 0)),
        out_shape=jax.ShapeDtypeStruct((R, dm), BF16))(x, nw, mod)


def mm_resid_norm_fwd(a, w, x, nw, mod_g, mod_n, kg, kn, T):
    R, dm = x.shape
    _, kb, nb = w.shape

    def body(a_ref, w_ref, x_ref, nw_ref, mg_ref, mn_ref, y_ref, xo_ref, h_ref):
        seg = _seg(pl.program_id(0), T)
        yv = jnp.dot(a_ref[...].astype(BF16), w_ref[...].reshape(N_CHIP * kb, nb), preferred_element_type=F32)
        y_ref[...] = yv
        xv = x_ref[...] + mg_ref[seg, pl.ds(kg, 1), :] * yv
        xo_ref[...] = xv
        r = lax.rsqrt(jnp.mean(xv * xv, axis=-1, keepdims=True) + EPS)
        h_ref[...] = ((xv * r * nw_ref[...]) * (1.0 + mn_ref[seg, pl.ds(kn + 1, 1), :])
                      + mn_ref[seg, pl.ds(kn, 1), :]).astype(BF16)
    tile = pl.BlockSpec((TMR, dm), lambda i: (i, 0))
    modspec = pl.BlockSpec((2, 6, dm), lambda i: (0, 0, 0))
    return pl.pallas_call(
        body, name="mm_resid_norm_fwd", grid=(R // TMR,),
        in_specs=[pl.BlockSpec((TMR, N_CHIP * kb), lambda i: (i, 0)),
                  pl.BlockSpec((N_CHIP, kb, nb), lambda i: (0, 0, 0), pipeline_mode=pl.Buffered(1)),
                  tile, pl.BlockSpec((1, dm), lambda i: (0, 0)), modspec, modspec],
        out_specs=[tile, tile, tile],
        out_shape=[jax.ShapeDtypeStruct((R, dm), F32), jax.ShapeDtypeStruct((R, dm), F32),
                   jax.ShapeDtypeStruct((R, dm), BF16)],
        compiler_params=_params(48))(a, w, x, nw, mod_g, mod_n)


def resid_bwd(dxn, y, mod, k, T):
    R, dm = dxn.shape

    def body(dx_ref, y_ref, mod_ref, dy_ref, dg_ref):
        i = pl.program_id(0)
        seg = _seg(i, T)

        @pl.when(i == 0)
        def _():
            dg_ref[...] = jnp.zeros_like(dg_ref)
        dxv = dx_ref[...]
        dy_ref[...] = (mod_ref[seg, pl.ds(k, 1), :] * dxv).astype(BF16)
        dg_ref[seg] += _colsum8(dxv * y_ref[...])
    tile = pl.BlockSpec((TMR, dm), lambda i: (i, 0))
    return pl.pallas_call(
        body, name="resid_bwd", grid=(R // TMR,),
        in_specs=[tile, tile, pl.BlockSpec((2, 6, dm), lambda i: (0, 0, 0))],
        out_specs=[tile, pl.BlockSpec((2, 8, dm), lambda i: (0, 0, 0))],
        out_shape=[jax.ShapeDtypeStruct((R, dm), BF16), jax.ShapeDtypeStruct((2, 8, dm), F32)])(dxn, y, mod)


def _halo_specs(width, R):
    nblk = R // HALO
    per = TMR // HALO
    return (pl.BlockSpec((HALO, width), lambda i: (jnp.maximum(i * per - 1, 0), 0)),
            pl.BlockSpec((TMR, width), lambda i: (i, 0)),
            pl.BlockSpec((HALO, width), lambda i: (jnp.minimum((i + 1) * per, nblk - 1), 0)))


def _halo_live(i, T, R):
    nl = T // TMR
    return (i != 0) & (i != nl), (i != nl - 1) & (i != R // TMR - 1)


def _ext(refs, c0, cw, live, halo=HALO):
    pref, ref, nref = refs
    before = jnp.where(live[0], pref[:, c0:c0 + cw].astype(F32)[HALO - halo:], 0.0)
    after = jnp.where(live[1], nref[:, c0:c0 + cw].astype(F32)[:halo], 0.0)
    return jnp.concatenate([before, ref[:, c0:c0 + cw].astype(F32), after], axis=0)


def _at(ext, off, halo=HALO):
    n = ext.shape[0]
    s = (-off) % n
    y = pltpu.roll(ext, s, 0) if s else ext
    return y[halo:halo + TMR]


def ffn_mid_fwd(u, wc, w_down, x, nw, mod_g, mod_n, kg, kn, T):
    R, w2 = u.shape
    dm = x.shape[1]
    _, kb, nb = w_down.shape
    cw = 256
    with_norm = nw is not None

    def body(*refs):
        if with_norm:
            up_ref, u_ref, un_ref, wc_ref, w_ref, x_ref, nw_ref, mg_ref, mn_ref, z_ref, y_ref, xo_ref, h_ref = refs
        else:
            up_ref, u_ref, un_ref, wc_ref, w_ref, x_ref, mg_ref, z_ref, y_ref, xo_ref = refs
        i = pl.program_id(0)
        seg = _seg(i, T)
        live = _halo_live(i, T, R)

        def conv(c0):
            e = _ext((up_ref, u_ref, un_ref), c0, cw, live, 8)
            return (wc_ref[pl.ds(0, 1), c0:c0 + cw] * _at(e, -1, 8) + wc_ref[pl.ds(1, 1), c0:c0 + cw] * _at(e, 0, 8)
                    + wc_ref[pl.ds(2, 1), c0:c0 + cw] * _at(e, 1, 8))
        yv = None
        for j in range(D_FF // cw):
            a = conv(j * cw)
            g = conv(D_FF + j * cw)
            zc = (g * _sigmoid(g) * a).astype(BF16)
            z_ref[:, j * cw:(j + 1) * cw] = zc
            t = jnp.dot(zc, w_ref[j * cw:(j + 1) * cw, :], preferred_element_type=F32)
            yv = t if yv is None else yv + t
        y_ref[...] = yv
        xv = x_ref[...] + mg_ref[seg, pl.ds(kg, 1), :] * yv
        xo_ref[...] = xv
        if with_norm:
            r = lax.rsqrt(jnp.mean(xv * xv, axis=-1, keepdims=True) + EPS)
            h_ref[...] = ((xv * r * nw_ref[...]) * (1.0 + mn_ref[seg, pl.ds(kn + 1, 1), :])
                          + mn_ref[seg, pl.ds(kn, 1), :]).astype(BF16)
    tile = pl.BlockSpec((TMR, dm), lambda i: (i, 0))
    modspec = pl.BlockSpec((2, 6, dm), lambda i: (0, 0, 0))
    w_down = w_down.reshape(N_CHIP * kb, nb)
    in_specs = [*_halo_specs(w2, R), pl.BlockSpec((3, w2), lambda i: (0, 0)),
                pl.BlockSpec((N_CHIP * kb, nb), lambda i: (0, 0), pipeline_mode=pl.Buffered(1)), tile]
    out_specs = [pl.BlockSpec((TMR, D_FF), lambda i: (i, 0)), tile, tile]
    out_shape = [jax.ShapeDtypeStruct((R, D_FF), BF16), jax.ShapeDtypeStruct((R, dm), F32),
                 jax.ShapeDtypeStruct((R, dm), F32)]
    if with_norm:
        return pl.pallas_call(
            body, name="ffn_mid_fwd", grid=(R // TMR,),
            in_specs=in_specs + [pl.BlockSpec((1, dm), lambda i: (0, 0)), modspec, modspec],
            out_specs=out_specs + [tile], out_shape=out_shape + [jax.ShapeDtypeStruct((R, dm), BF16)],
            compiler_params=_params(48))(u, u, u, wc, w_down, x, nw, mod_g, mod_n)
    return pl.pallas_call(
        body, name="ffn_mid_fwd_last", grid=(R // TMR,), in_specs=in_specs + [modspec],
        out_specs=out_specs, out_shape=out_shape, compiler_params=_params(48))(u, u, u, wc, w_down, x, mod_g)


def ffn_mid_bwd(dz, u, wc, T):
    R, w2 = u.shape
    cw = 256

    def body(dz_ref, up_ref, u_ref, un_ref, wc_ref, duc_ref, dwc_ref):
        i = pl.program_id(0)
        live = _halo_live(i, T, R)

        @pl.when(i == 0)
        def _():
            dwc_ref[...] = jnp.zeros_like(dwc_ref)

        def taps(c0):
            e = _ext((up_ref, u_ref, un_ref), c0, cw, live, 8)
            return [_at(e, -1, 8), _at(e, 0, 8), _at(e, 1, 8)]

        def conv(t, c0):
            return (wc_ref[pl.ds(0, 1), c0:c0 + cw] * t[0] + wc_ref[pl.ds(1, 1), c0:c0 + cw] * t[1]
                    + wc_ref[pl.ds(2, 1), c0:c0 + cw] * t[2])
        for j in range(D_FF // cw):
            ca, cg = j * cw, D_FF + j * cw
            dzv = dz_ref[:, ca:ca + cw].astype(F32)
            ta, tg = taps(ca), taps(cg)
            a, g = conv(ta, ca), conv(tg, cg)
            sg = _sigmoid(g)
            da = dzv * (g * sg)
            dg = dzv * a * (sg * (1.0 + g * (1.0 - sg)))
            duc_ref[:, ca:ca + cw] = da.astype(BF16)
            duc_ref[:, cg:cg + cw] = dg.astype(BF16)
            for k in range(3):
                dwc_ref[k, :, ca:ca + cw] += _colsum8(da * ta[k])
                dwc_ref[k, :, cg:cg + cw] += _colsum8(dg * tg[k])
    return pl.pallas_call(
        body, name="ffn_mid_bwd", grid=(R // TMR,),
        in_specs=[pl.BlockSpec((TMR, D_FF), lambda i: (i, 0)), *_halo_specs(w2, R),
                  pl.BlockSpec((3, w2), lambda i: (0, 0))],
        out_specs=[pl.BlockSpec((TMR, w2), lambda i: (i, 0)), pl.BlockSpec((3, 8, w2), lambda i: (0, 0, 0))],
        out_shape=[jax.ShapeDtypeStruct((R, w2), BF16), jax.ShapeDtypeStruct((3, 8, w2), F32)],
        compiler_params=_params(48))(dz, u, u, u, wc)


def ffn_up_bwd(duc, wc, w_up, x, nw, mod_n, dxr, y, mod_g, kn, kg, T):
    R, w2 = duc.shape
    dm = x.shape[1]
    _, kb, nb = w_up.shape
    cw = 128
    contract_last = (((1,), (1,)), ((), ()))

    def body(dp_ref, d_ref, dn_ref, wc_ref, w_ref, x_ref, nw_ref, mn_ref, dxr_ref, y_ref, mg_ref,
             du_ref, dx_ref, dy_ref, dmod_ref, dnw_ref, dg_ref):
        i = pl.program_id(0)
        seg = _seg(i, T)
        live = _halo_live(i, T, R)

        @pl.when(i == 0)
        def _():
            dmod_ref[...] = jnp.zeros_like(dmod_ref)
            dnw_ref[...] = jnp.zeros_like(dnw_ref)
            dg_ref[...] = jnp.zeros_like(dg_ref)
        dhv = None
        for q in range(N_CHIP):
            for j in range(nb // cw):
                c0 = q * nb + j * cw
                e = _ext((dp_ref, d_ref, dn_ref), c0, cw, live, 8)
                du_ref[:, c0:c0 + cw] = (wc_ref[pl.ds(0, 1), c0:c0 + cw] * _at(e, 1, 8)
                                         + wc_ref[pl.ds(1, 1), c0:c0 + cw] * _at(e, 0, 8)
                                         + wc_ref[pl.ds(2, 1), c0:c0 + cw] * _at(e, -1, 8)).astype(BF16)
            t = lax.dot_general(du_ref[:, q * nb:(q + 1) * nb], w_ref[q], contract_last,
                                preferred_element_type=F32)
            dhv = t if dhv is None else dhv + t
        sc = mn_ref[seg, pl.ds(kn + 1, 1), :]
        nwv = nw_ref[...]
        xv = x_ref[...]
        r = lax.rsqrt(jnp.mean(xv * xv, axis=-1, keepdims=True) + EPS)
        xh = xv * r
        dmod_ref[seg, 0] += _colsum8(dhv)
        dmod_ref[seg, 1] += _colsum8(dhv * (xh * nwv))
        dn = dhv * (1.0 + sc)
        dnw_ref[...] += _colsum8(dn * xh)
        dxh = dn * nwv
        dx = dxr_ref[...] + r * (dxh - xh * jnp.mean(dxh * xh, axis=-1, keepdims=True))
        dx_ref[...] = dx
        dy_ref[...] = (mg_ref[seg, pl.ds(kg, 1), :] * dx).astype(BF16)
        dg_ref[seg] += _colsum8(dx * y_ref[...])
    tile = pl.BlockSpec((TMR, dm), lambda i: (i, 0))
    modspec = pl.BlockSpec((2, 6, dm), lambda i: (0, 0, 0))
    return pl.pallas_call(
        body, name="ffn_up_bwd", grid=(R // TMR,),
        in_specs=[*_halo_specs(w2, R), pl.BlockSpec((3, w2), lambda i: (0, 0)),
                  pl.BlockSpec((N_CHIP, kb, nb), lambda i: (0, 0, 0), pipeline_mode=pl.Buffered(1)),
                  tile, pl.BlockSpec((1, dm), lambda i: (0, 0)), modspec, tile, tile, modspec],
        out_specs=[pl.BlockSpec((TMR, w2), lambda i: (i, 0)), tile, tile,
                   pl.BlockSpec((2, 2, 8, dm), lambda i: (0, 0, 0, 0)), pl.BlockSpec((8, dm), lambda i: (0, 0)),
                   pl.BlockSpec((2, 8, dm), lambda i: (0, 0, 0))],
        out_shape=[jax.ShapeDtypeStruct((R, w2), BF16), jax.ShapeDtypeStruct((R, dm), F32),
                   jax.ShapeDtypeStruct((R, dm), BF16), jax.ShapeDtypeStruct((2, 2, 8, dm), F32),
                   jax.ShapeDtypeStruct((8, dm), F32), jax.ShapeDtypeStruct((2, 8, dm), F32)],
        compiler_params=_params(48))(duc, duc, duc, wc, w_up, x, nw, mod_n, dxr, y, mod_g)


_CW = 128


def _mixer_a(prefs, wa_ref, live):
    cin = _ext(prefs, A_W, A_W, live) * _ext(prefs, 2 * A_W, A_W, live)
    ca = (wa_ref[pl.ds(0, 1), :] * _at(cin, -1) + wa_ref[pl.ds(1, 1), :] * _at(cin, 0)
          + wa_ref[pl.ds(2, 1), :] * _at(cin, 1))
    return cin, ca


def _mixer_b(prefs, wb_ref, bias_ref, live, ub_s, ub2_s):
    for cc in range(A_W // _CW):
        c0 = cc * _CW
        ub = _ext(prefs, 3 * A_W + c0, _CW, live) * _sigmoid(_ext(prefs, 4 * A_W + c0, _CW, live))
        ub_s[:, c0:c0 + _CW] = ub
        acc = jnp.zeros((TMR, _CW), F32) + bias_ref[:, c0:c0 + _CW]
        for k in range(B_CONV):
            acc = acc + wb_ref[pl.ds(k, 1), c0:c0 + _CW] * _at(ub, k - B_CONV // 2)
        ub2_s[:, c0:c0 + _CW] = acc


def _layernorm_stats(v):
    mu = jnp.mean(v, axis=-1, keepdims=True)
    xc = v - mu
    rs = lax.rsqrt(jnp.mean(xc * xc, axis=-1, keepdims=True) + EPS)
    return xc * rs, rs


def mixer_fwd(p, wa, wb, bias, lng, lnb, w_out, x, nw, mod, kg, kn, T):
    R, wp = p.shape
    dm = x.shape[1]
    _, kb, nb = w_out.shape

    def body(pp_ref, p_ref, pn_ref, wa_ref, wb_ref, bias_ref, lng_ref, lnb_ref, w_ref, x_ref, nw_ref, mod_ref,
             o_ref, y_ref, xo_ref, h_ref, ub_s, ub2_s):
        i = pl.program_id(0)
        seg = _seg(i, T)
        live = _halo_live(i, T, R)
        prefs = (pp_ref, p_ref, pn_ref)
        _, ca = _mixer_a(prefs, wa_ref, live)
        ya = (p_ref[:, 0:A_W].astype(F32) * ca).astype(BF16)
        o_ref[:, 0:A_W] = ya
        yv = jnp.dot(ya, w_ref[0:A_W, :], preferred_element_type=F32)
        _mixer_b(prefs, wb_ref, bias_ref, live, ub_s, ub2_s)
        xh, _ = _layernorm_stats(ub2_s[...])
        lv = xh * lng_ref[...] + lnb_ref[...]
        yb = (lv * _sigmoid(lv)).astype(BF16)
        o_ref[:, A_W:2 * A_W] = yb
        yv = yv + jnp.dot(yb, w_ref[A_W:2 * A_W, :], preferred_element_type=F32)
        y_ref[...] = yv
        xv = x_ref[...] + mod_ref[seg, pl.ds(kg, 1), :] * yv
        xo_ref[...] = xv
        r = lax.rsqrt(jnp.mean(xv * xv, axis=-1, keepdims=True) + EPS)
        h_ref[...] = ((xv * r * nw_ref[...]) * (1.0 + mod_ref[seg, pl.ds(kn + 1, 1), :])
                      + mod_ref[seg, pl.ds(kn, 1), :]).astype(BF16)
    vec = pl.BlockSpec((1, A_W), lambda i: (0, 0))
    tile = pl.BlockSpec((TMR, dm), lambda i: (i, 0))
    return pl.pallas_call(
        body, name="mixer_fwd", grid=(R // TMR,),
        in_specs=[*_halo_specs(wp, R), pl.BlockSpec((3, A_W), lambda i: (0, 0)),
                  pl.BlockSpec((B_CONV, A_W), lambda i: (0, 0)), vec, vec, vec,
                  pl.BlockSpec((N_CHIP * kb, nb), lambda i: (0, 0), pipeline_mode=pl.Buffered(1)),
                  tile, pl.BlockSpec((1, dm), lambda i: (0, 0)), pl.BlockSpec((2, 6, dm), lambda i: (0, 0, 0))],
        out_specs=[pl.BlockSpec((TMR, 2 * A_W), lambda i: (i, 0)), tile, tile, tile],
        out_shape=[jax.ShapeDtypeStruct((R, 2 * A_W), BF16), jax.ShapeDtypeStruct((R, dm), F32),
                   jax.ShapeDtypeStruct((R, dm), F32), jax.ShapeDtypeStruct((R, dm), BF16)],
        scratch_shapes=[pltpu.VMEM((TMR + 2 * HALO, A_W), F32), pltpu.VMEM((TMR, A_W), F32)],
        compiler_params=_params(48))(p, p, p, wa, wb, bias, lng, lnb, w_out.reshape(N_CHIP * kb, nb), x, nw, mod)


def convmix_bwd1(dyab, p, wa, wb, bias, lng, lnb, T):
    R, wp = p.shape

    def body(dy_ref, pp_ref, p_ref, pn_ref, wa_ref, wb_ref, bias_ref, lng_ref, lnb_ref,
             dmid_ref, dwa_ref, dwb_ref, dvec_ref, ub_s, ub2_s):
        i = pl.program_id(0)
        live = _halo_live(i, T, R)

        @pl.when(i == 0)
        def _():
            dwa_ref[...] = jnp.zeros_like(dwa_ref)
            dwb_ref[...] = jnp.zeros_like(dwb_ref)
            dvec_ref[...] = jnp.zeros_like(dvec_ref)
        prefs = (pp_ref, p_ref, pn_ref)
        cin, ca = _mixer_a(prefs, wa_ref, live)
        dya = dy_ref[:, 0:A_W]
        dmid_ref[:, 0:A_W] = dya * ca
        dca = dya * p_ref[:, 0:A_W].astype(F32)
        dmid_ref[:, A_W:2 * A_W] = dca
        for k in range(3):
            dwa_ref[k] += _colsum8(dca * _at(cin, k - 1))
        _mixer_b(prefs, wb_ref, bias_ref, live, ub_s, ub2_s)
        xh, rs = _layernorm_stats(ub2_s[...])
        gain = lng_ref[...]
        lv = xh * gain + lnb_ref[...]
        sl = _sigmoid(lv)
        dl = dy_ref[:, A_W:2 * A_W] * (sl * (1.0 + lv * (1.0 - sl)))
        dvec_ref[1] += _colsum8(dl * xh)
        dvec_ref[2] += _colsum8(dl)
        dxh = dl * gain
        dub2 = rs * (dxh - jnp.mean(dxh, axis=-1, keepdims=True)
                     - xh * jnp.mean(dxh * xh, axis=-1, keepdims=True))
        dvec_ref[0] += _colsum8(dub2)
        dmid_ref[:, 2 * A_W:3 * A_W] = dub2
        for cc in range(A_W // _CW):
            c0 = cc * _CW
            ub = ub_s[:, c0:c0 + _CW]
            d = dmid_ref[:, 2 * A_W + c0:2 * A_W + c0 + _CW]
            for k in range(B_CONV):
                dwb_ref[k, :, c0:c0 + _CW] += _colsum8(d * _at(ub, k - B_CONV // 2))
    vec = pl.BlockSpec((1, A_W), lambda i: (0, 0))
    return pl.pallas_call(
        body, name="convmix_bwd1", grid=(R // TMR,),
        in_specs=[pl.BlockSpec((TMR, 2 * A_W), lambda i: (i, 0)), *_halo_specs(wp, R),
                  pl.BlockSpec((3, A_W), lambda i: (0, 0)), pl.BlockSpec((B_CONV, A_W), lambda i: (0, 0)),
                  vec, vec, vec],
        out_specs=[pl.BlockSpec((TMR, 3 * A_W), lambda i: (i, 0)),
                   pl.BlockSpec((3, 8, A_W), lambda i: (0, 0, 0)),
                   pl.BlockSpec((B_CONV, 8, A_W), lambda i: (0, 0, 0)),
                   pl.BlockSpec((3, 8, A_W), lambda i: (0, 0, 0))],
        out_shape=[jax.ShapeDtypeStruct((R, 3 * A_W), F32), jax.ShapeDtypeStruct((3, 8, A_W), F32),
                   jax.ShapeDtypeStruct((B_CONV, 8, A_W), F32), jax.ShapeDtypeStruct((3, 8, A_W), F32)],
        scratch_shapes=[pltpu.VMEM((TMR + 2 * HALO, A_W), F32), pltpu.VMEM((TMR, A_W), F32)],
        compiler_params=_params(48))(dyab, p, p, p, wa, wb, bias, lng, lnb)


def mixer_in_bwd(dmid, p, wa, wb, w_in, x, nw, mod_n, dxr, y, mod_g, kn, kg, T):
    R, wp = p.shape
    dm = x.shape[1]
    _, kb, nb = w_in.shape
    with_resid = y is not None
    contract_last = (((1,), (1,)), ((), ()))

    def body(*refs):
        if with_resid:
            (mp_ref, m_ref, mn_ref, p_ref, wa_ref, wb_ref, w_ref, x_ref, nw_ref, mnorm_ref, dxr_ref, y_ref, mg_ref,
             dp_ref, dx_ref, dy_ref, dmod_ref, dnw_ref, dg_ref) = refs
        else:
            (mp_ref, m_ref, mn_ref, p_ref, wa_ref, wb_ref, w_ref, x_ref, nw_ref, mnorm_ref, dxr_ref,
             dp_ref, dx_ref, dmod_ref, dnw_ref) = refs
        i = pl.program_id(0)
        seg = _seg(i, T)
        live = _halo_live(i, T, R)

        @pl.when(i == 0)
        def _():
            dmod_ref[...] = jnp.zeros_like(dmod_ref)
            dnw_ref[...] = jnp.zeros_like(dnw_ref)
            if with_resid:
                dg_ref[...] = jnp.zeros_like(dg_ref)

        def block(q):
            return lax.dot_general(dp_ref[:, q * nb:(q + 1) * nb], w_ref[q], contract_last,
                                   preferred_element_type=F32)
        mrefs = (mp_ref, m_ref, mn_ref)
        dp_ref[:, 0:A_W] = m_ref[:, 0:A_W].astype(BF16)
        dca = _ext(mrefs, A_W, A_W, live)
        dcin = (wa_ref[pl.ds(0, 1), :] * _at(dca, 1) + wa_ref[pl.ds(1, 1), :] * _at(dca, 0)
                + wa_ref[pl.ds(2, 1), :] * _at(dca, -1))
        dp_ref[:, A_W:2 * A_W] = (dcin * p_ref[:, 2 * A_W:3 * A_W].astype(F32)).astype(BF16)
        dp_ref[:, 2 * A_W:3 * A_W] = (dcin * p_ref[:, A_W:2 * A_W].astype(F32)).astype(BF16)
        dhv = block(0) + block(1)
        for cc in range(A_W // _CW):
            c0 = cc * _CW
            d = _ext(mrefs, 2 * A_W + c0, _CW, live)
            dub = jnp.zeros((TMR, _CW), F32)
            for k in range(B_CONV):
                dub = dub + wb_ref[pl.ds(k, 1), c0:c0 + _CW] * _at(d, B_CONV // 2 - k)
            vb = p_ref[:, 3 * A_W + c0:3 * A_W + c0 + _CW].astype(F32)
            s = _sigmoid(p_ref[:, 4 * A_W + c0:4 * A_W + c0 + _CW].astype(F32))
            dp_ref[:, 3 * A_W + c0:3 * A_W + c0 + _CW] = (dub * s).astype(BF16)
            dp_ref[:, 4 * A_W + c0:4 * A_W + c0 + _CW] = (dub * vb * s * (1.0 - s)).astype(BF16)
        dhv = dhv + block(2) + block(3)
        sc = mnorm_ref[seg, pl.ds(kn + 1, 1), :]
        nwv = nw_ref[...]
        xv = x_ref[...]
        r = lax.rsqrt(jnp.mean(xv * xv, axis=-1, keepdims=True) + EPS)
        xh = xv * r
        dmod_ref[seg, 0] += _colsum8(dhv)
        dmod_ref[seg, 1] += _colsum8(dhv * (xh * nwv))
        dn = dhv * (1.0 + sc)
        dnw_ref[...] += _colsum8(dn * xh)
        dxh = dn * nwv
        dx = dxr_ref[...] + r * (dxh - xh * jnp.mean(dxh * xh, axis=-1, keepdims=True))
        dx_ref[...] = dx
        if with_resid:
            dy_ref[...] = (mg_ref[seg, pl.ds(kg, 1), :] * dx).astype(BF16)
            dg_ref[seg] += _colsum8(dx * y_ref[...])
    assert 2 * nb <= 3 * A_W and N_CHIP * nb == wp
    tile = pl.BlockSpec((TMR, dm), lambda i: (i, 0))
    modspec = pl.BlockSpec((2, 6, dm), lambda i: (0, 0, 0))
    in_specs = [*_halo_specs(3 * A_W, R), pl.BlockSpec((TMR, wp), lambda i: (i, 0)),
                pl.BlockSpec((3, A_W), lambda i: (0, 0)), pl.BlockSpec((B_CONV, A_W), lambda i: (0, 0)),
                pl.BlockSpec((N_CHIP, kb, nb), lambda i: (0, 0, 0), pipeline_mode=pl.Buffered(1)),
                tile, pl.BlockSpec((1, dm), lambda i: (0, 0)), modspec, tile]
    dp_spec = pl.BlockSpec((TMR, wp), lambda i: (i, 0))
    acc_specs = [pl.BlockSpec((2, 2, 8, dm), lambda i: (0, 0, 0, 0)), pl.BlockSpec((8, dm), lambda i: (0, 0))]
    acc_shapes = [jax.ShapeDtypeStruct((2, 2, 8, dm), F32), jax.ShapeDtypeStruct((8, dm), F32)]
    dp_shape, dx_shape = jax.ShapeDtypeStruct((R, wp), BF16), jax.ShapeDtypeStruct((R, dm), F32)
    if with_resid:
        return pl.pallas_call(
            body, name="mixer_in_bwd", grid=(R // TMR,), in_specs=in_specs + [tile, modspec],
            out_specs=[dp_spec, tile, tile] + acc_specs + [pl.BlockSpec((2, 8, dm), lambda i: (0, 0, 0))],
            out_shape=[dp_shape, dx_shape, jax.ShapeDtypeStruct((R, dm), BF16)] + acc_shapes
            + [jax.ShapeDtypeStruct((2, 8, dm), F32)],
            compiler_params=_params(48))(dmid, dmid, dmid, p, wa, wb, w_in, x, nw, mod_n, dxr, y, mod_g)
    return pl.pallas_call(
        body, name="mixer_in_bwd_first", grid=(R // TMR,), in_specs=in_specs,
        out_specs=[dp_spec, tile] + acc_specs, out_shape=[dp_shape, dx_shape] + acc_shapes,
        compiler_params=_params(48))(dmid, dmid, dmid, p, wa, wb, w_in, x, nw, mod_n, dxr)


def _rot_half(v):
    w = v.shape[-1]
    lane = lax.broadcasted_iota(jnp.int32, (1, w), 1)
    return jnp.where(lane % HEAD_DIM < HEAD_DIM // 2, pltpu.roll(v, w - HEAD_DIM // 2, 1),
                     pltpu.roll(v, HEAD_DIM // 2, 1))


def mm_qkv_rope(a, w, cs, sn):
    R = a.shape[0]
    _, kb, nb = w.shape
    wq = N_CHIP * nb
    tm = _row_tile(R)
    qw = N_HEADS * HEAD_DIM
    kw = (wq - qw) // 2
    scale = HEAD_DIM ** -0.5

    def body(a_ref, w_ref, cs_ref, sn_ref, o_ref, x_ref):
        av = a_ref[...].astype(BF16)
        for q in range(N_CHIP):
            x_ref[:, q * nb:(q + 1) * nb] = jnp.dot(av, w_ref[q], preferred_element_type=F32)
        c, s = cs_ref[...], sn_ref[...]
        q = x_ref[:, 0:qw]
        o_ref[:, 0:qw] = ((q * jnp.tile(c, (1, qw // 128)) + _rot_half(q) * jnp.tile(s, (1, qw // 128)))
                          * scale).astype(BF16)
        k = x_ref[:, qw:qw + kw]
        o_ref[:, qw:qw + kw] = (k * jnp.tile(c, (1, kw // 128))
                                + _rot_half(k) * jnp.tile(s, (1, kw // 128))).astype(BF16)
        o_ref[:, qw + kw:] = x_ref[:, qw + kw:].astype(BF16)
    tab = pl.BlockSpec((tm, 128), lambda i: (i, 0))
    return pl.pallas_call(
        body, name="mm_qkv_rope", grid=(R // tm,),
        in_specs=[pl.BlockSpec((tm, kb), lambda i: (i, 0)),
                  pl.BlockSpec((N_CHIP, kb, nb), lambda i: (0, 0, 0), pipeline_mode=pl.Buffered(1)), tab, tab],
        out_specs=pl.BlockSpec((tm, wq), lambda i: (i, 0)),
        out_shape=jax.ShapeDtypeStruct((R, wq), BF16), scratch_shapes=[pltpu.VMEM((tm, wq), F32)],
        compiler_params=_params(48))(a, w, cs, sn)


def attn_in_bwd(dq, dks, dvs, dkc, dvc, cs, sn, w, x, nw, mod_n, dxr, y, mod_g, kn, kg, T):
    R, qw = dq.shape
    kw = dkc.shape[1]
    dm = x.shape[1]
    _, kb, nbw = w.shape
    nb = R // QB
    nl = T // QB
    scale = HEAD_DIM ** -0.5
    contract_last = (((1,), (1,)), ((), ()))

    def body(dq_ref, kp_ref, ko_ref, kn_ref, vp_ref, vo_ref, vn_ref, kc_ref, vc_ref, cs_ref, sn_ref,
             w_ref, x_ref, nw_ref, mnorm_ref, dxr_ref, y_ref, mg_ref,
             o_ref, dx_ref, dy_ref, dmod_ref, dnw_ref, dg_ref):
        b = pl.program_id(0)
        seg = (b >= nl).astype(jnp.int32)

        @pl.when(b == 0)
        def _():
            dmod_ref[...] = jnp.zeros_like(dmod_ref)
            dnw_ref[...] = jnp.zeros_like(dnw_ref)
            dg_ref[...] = jnp.zeros_like(dg_ref)
        c, s = cs_ref[...], sn_ref[...]
        has_next = (b + 1 < nb).astype(F32)
        has_prev = (b >= 1).astype(F32)
        is_ctx = (b >= nl).astype(F32)
        g = dq_ref[...] * scale
        o_ref[:, 0:qw] = (g * jnp.tile(c, (1, qw // 128)) + _rot_half(g * jnp.tile(s, (1, qw // 128)))).astype(BF16)
        g = ko_ref[...] + kp_ref[...] * has_next + kn_ref[...] * has_prev + kc_ref[...] * is_ctx
        o_ref[:, qw:qw + kw] = (g * jnp.tile(c, (1, kw // 128))
                                + _rot_half(g * jnp.tile(s, (1, kw // 128)))).astype(BF16)
        o_ref[:, qw + kw:] = (vo_ref[...] + vp_ref[...] * has_next + vn_ref[...] * has_prev
                              + vc_ref[...] * is_ctx).astype(BF16)
        dhv = None
        for q in range(N_CHIP):
            t = lax.dot_general(o_ref[:, q * nbw:(q + 1) * nbw], w_ref[q], contract_last,
                                preferred_element_type=F32)
            dhv = t if dhv is None else dhv + t
        sc = mnorm_ref[seg, pl.ds(kn + 1, 1), :]
        nwv = nw_ref[...]
        xv = x_ref[...]
        r = lax.rsqrt(jnp.mean(xv * xv, axis=-1, keepdims=True) + EPS)
        xh = xv * r
        dmod_ref[seg, 0] += _colsum8(dhv)
        dmod_ref[seg, 1] += _colsum8(dhv * (xh * nwv))
        dn = dhv * (1.0 + sc)
        dnw_ref[...] += _colsum8(dn * xh)
        dxh = dn * nwv
        dx = dxr_ref[...] + r * (dxh - xh * jnp.mean(dxh * xh, axis=-1, keepdims=True))
        dx_ref[...] = dx
        dy_ref[...] = (mg_ref[seg, pl.ds(kg, 1), :] * dx).astype(BF16)
        dg_ref[seg] += _colsum8(dx * y_ref[...])
    own = pl.BlockSpec((QB, kw), lambda b: (b, 0))
    from_next = pl.BlockSpec((QB, kw), lambda b: (jnp.minimum(b + 1, nb - 1), 0))
    from_prev = pl.BlockSpec((QB, kw), lambda b: (jnp.maximum(b - 1, 0), 0))
    ctx = pl.BlockSpec((QB, kw), lambda b: (jnp.maximum(b - nl, 0), 0))
    tab = pl.BlockSpec((QB, 128), lambda b: (b, 0))
    tile = pl.BlockSpec((QB, dm), lambda b: (b, 0))
    modspec = pl.BlockSpec((2, 6, dm), lambda b: (0, 0, 0))
    return pl.pallas_call(
        body, name="attn_in_bwd", grid=(nb,),
        in_specs=[pl.BlockSpec((QB, qw), lambda b: (b, 0)), from_next, own, from_prev, from_next, own, from_prev,
                  ctx, ctx, tab, tab,
                  pl.BlockSpec((N_CHIP, kb, nbw), lambda b: (0, 0, 0), pipeline_mode=pl.Buffered(1)),
                  tile, pl.BlockSpec((1, dm), lambda b: (0, 0)), modspec, tile, tile, modspec],
        out_specs=[pl.BlockSpec((QB, qw + 2 * kw), lambda b: (b, 0)), tile, tile,
                   pl.BlockSpec((2, 2, 8, dm), lambda b: (0, 0, 0, 0)), pl.BlockSpec((8, dm), lambda b: (0, 0)),
                   pl.BlockSpec((2, 8, dm), lambda b: (0, 0, 0))],
        out_shape=[jax.ShapeDtypeStruct((R, qw + 2 * kw), BF16), jax.ShapeDtypeStruct((R, dm), F32),
                   jax.ShapeDtypeStruct((R, dm), BF16), jax.ShapeDtypeStruct((2, 2, 8, dm), F32),
                   jax.ShapeDtypeStruct((8, dm), F32), jax.ShapeDtypeStruct((2, 8, dm), F32)],
        compiler_params=_params(48))(
            dq, dks[0], dks[1], dks[2], dvs[0], dvs[1], dvs[2], dkc, dvc, cs, sn, w, x, nw, mod_n, dxr, y, mod_g)


def _attn_specs(T, R):
    nl = T // QB
    qcols = N_HEADS * HEAD_DIM // 128
    kcols = 2

    def band(col0, shift):
        return pl.BlockSpec((QB, 128), lambda jj, b: (jnp.clip(b + shift, 0, nl - 1), col0 + jj))

    def ctx(col0):
        return pl.BlockSpec((R - T, 128), lambda jj, b: (T // (R - T), col0 + jj))
    q = pl.BlockSpec((QB, 512), lambda jj, b: (b, jj))
    k0, v0 = qcols, qcols + kcols
    return q, [band(k0, -1), band(k0, 0), band(k0, 1), ctx(k0)], [band(v0, -1), band(v0, 0), band(v0, 1), ctx(v0)]


def _attn_common(T, R):
    nl = T // QB
    nk = 3 * QB + (R - T)

    def low_lanes():
        return lax.broadcasted_iota(jnp.int32, (1, 128), 1) < HEAD_DIM

    def dup(v, par):
        low = low_lanes()
        vf = v.astype(F32)
        r = pltpu.roll(vf, HEAD_DIM, 1)
        return (jnp.where(low, vf, r) if par == 0 else jnp.where(low, r, vf)).astype(BF16)

    def stack(ref, par):
        low = low_lanes()
        pa = ref[:, (2 * par) * 128:(2 * par + 1) * 128].astype(BF16)
        pb = ref[:, (2 * par + 1) * 128:(2 * par + 2) * 128].astype(BF16)
        zero = jnp.zeros_like(pa)
        return jnp.concatenate([jnp.where(low, pa, zero), jnp.where(low, zero, pa),
                                jnp.where(low, pb, zero), jnp.where(low, zero, pb)], axis=0)

    def unstack(v):
        low = low_lanes()
        return (jnp.where(low, v[0:QB], v[QB:2 * QB]), jnp.where(low, v[2 * QB:3 * QB], v[3 * QB:4 * QB]))

    def mask_of(b):
        col = lax.broadcasted_iota(jnp.int32, (1, nk), 1)
        gone = (((col < QB) & (b == 0)) | ((col >= 2 * QB) & (col < 3 * QB) & (b == nl - 1))
                | ((col < 3 * QB) & (b >= nl)))
        return jnp.where(gone, NEG_INF, 0.0)

    def sink_col(sink_ref, first):
        blk = lax.broadcasted_iota(jnp.int32, (4 * QB, 1), 0) // QB
        out = jnp.zeros((4 * QB, 1), F32) + sink_ref[first]
        for h in range(1, 4):
            out = jnp.where(blk == h, sink_ref[first + h], out)
        return out

    def scores(qs, kd, mask, sink):
        s = lax.dot_general(qs, kd, (((1,), (1,)), ((), ())), preferred_element_type=F32) + mask
        m = jnp.maximum(jnp.max(s, axis=-1, keepdims=True), sink)
        e = jnp.exp(s - m)
        es = jnp.exp(sink - m)
        return e, es, 1.0 / (jnp.sum(e, axis=-1, keepdims=True) + es)
    return low_lanes, dup, stack, unstack, mask_of, sink_col, scores


def window_bias(T, R):
    nk = 3 * QB + (R - T)
    row = jnp.arange(QB)[:, None]
    col = jnp.arange(nk)[None, :]
    near = (jnp.abs(col - QB - row) <= WINDOW) | (col >= 3 * QB)
    return jnp.tile(jnp.where(near, 0.0, NEG_INF).astype(F32), (4, 1))


def _probs_spec(nk):
    return pl.BlockSpec((None, None, 2, 4 * QB, nk + 128), lambda jj, b: (jj, b, 0, 0, 0))


def attn_fwd(qkvr, sinks, bias, T):
    R = qkvr.shape[0]
    nk = bias.shape[1]
    qspec, kspecs, vspecs = _attn_specs(T, R)
    _, dup, stack, unstack, mask_of, sink_col, scores = _attn_common(T, R)

    def body(q_ref, kp, ko, kn, kc, vp, vo, vn, vc, sink_ref, bias_ref, o_ref, p_ref):
        jj, b = pl.program_id(0), pl.program_id(1)
        mask = bias_ref[...] + mask_of(b)
        k_all = jnp.concatenate([kp[...], ko[...], kn[...], kc[...]], axis=0)
        v_all = jnp.concatenate([vp[...], vo[...], vn[...], vc[...]], axis=0)
        for par in range(2):
            kd, vd = dup(k_all, par), dup(v_all, par)
            e, es, rz = scores(stack(q_ref, par), kd, mask, sink_col(sink_ref, jj * 8 + par * 4))
            p = (e * rz).astype(BF16)
            p_ref[par, :, 0:nk] = p
            p_ref[par, :, nk:nk + 128] = jnp.broadcast_to(es * rz, (4 * QB, 128)).astype(BF16)
            o = jnp.dot(p, vd, preferred_element_type=F32)
            pa, pb = unstack(o)
            o_ref[:, (2 * par) * 128:(2 * par + 1) * 128] = pa.astype(BF16)
            o_ref[:, (2 * par + 1) * 128:(2 * par + 2) * 128] = pb.astype(BF16)
    return pl.pallas_call(
        body, name="attn_fwd", grid=(2, R // QB),
        in_specs=[qspec, *kspecs, *vspecs, pl.BlockSpec(memory_space=pltpu.SMEM),
                  pl.BlockSpec(bias.shape, lambda jj, b: (0, 0))],
        out_specs=[pl.BlockSpec((QB, 512), lambda jj, b: (b, jj)), _probs_spec(nk)],
        out_shape=[jax.ShapeDtypeStruct((R, N_HEADS * HEAD_DIM), BF16),
                   jax.ShapeDtypeStruct((2, R // QB, 2, 4 * QB, nk + 128), BF16)],
        compiler_params=_params(48))(qkvr, *([qkvr] * 8), sinks, bias)


def attn_bwd(qkvr, do, probs, T):
    R = qkvr.shape[0]
    tc = R - T
    nk = probs.shape[-1] - 128
    qspec, kspecs, vspecs = _attn_specs(T, R)
    _, dup, stack, unstack, _, _, _ = _attn_common(T, R)
    contract_rows = (((0,), (0,)), ((), ()))
    contract_last = (((1,), (1,)), ((), ()))

    def body(q_ref, kp, ko, kn, kc, vp, vo, vn, vc, do_ref, p_ref,
             dq_ref, dkp, dko, dkn, dvp, dvo, dvn, dkc_ref, dvc_ref, dsink_ref):
        jj, b = pl.program_id(0), pl.program_id(1)

        @pl.when((jj == 0) & (b == 0))
        def _():
            dsink_ref[...] = jnp.zeros_like(dsink_ref)

        @pl.when(b == 0)
        def _():
            dkc_ref[...] = jnp.zeros_like(dkc_ref)
            dvc_ref[...] = jnp.zeros_like(dvc_ref)
        k_all = jnp.concatenate([kp[...], ko[...], kn[...], kc[...]], axis=0)
        v_all = jnp.concatenate([vp[...], vo[...], vn[...], vc[...]], axis=0)
        lane = lax.broadcasted_iota(jnp.int32, (8, 128), 1)
        srow = lax.broadcasted_iota(jnp.int32, (8, 128), 0)
        dk_fold, dv_fold = [], []
        for par in range(2):
            kd, vd = dup(k_all, par), dup(v_all, par)
            first = jj * 8 + par * 4
            qs, dos = stack(q_ref, par), stack(do_ref, par)
            p16 = p_ref[par, :, 0:nk]
            p = p16.astype(F32)
            ps = jnp.max(p_ref[par, :, nk:nk + 128].astype(F32), axis=-1, keepdims=True)
            dp = lax.dot_general(dos, vd, contract_last, preferred_element_type=F32)
            delta = jnp.sum(p * dp, axis=-1, keepdims=True)
            ds = (p * (dp - delta)).astype(BF16)
            t = ps * delta
            for h in range(4):
                dsink = -jnp.sum(t[h * QB:(h + 1) * QB])
                dsink_ref[...] += jnp.where((lane == first + h) & (srow == 0), dsink, 0.0)
            pa, pb = unstack(jnp.dot(ds, kd, preferred_element_type=F32))
            dq_ref[:, (2 * par) * 128:(2 * par + 1) * 128] = pa
            dq_ref[:, (2 * par + 1) * 128:(2 * par + 2) * 128] = pb
            dk_t = lax.dot_general(qs, ds, contract_rows, preferred_element_type=F32)
            dv_t = lax.dot_general(dos, p16, contract_rows, preferred_element_type=F32)
            dk_fold.append(dk_t + pltpu.roll(dk_t, HEAD_DIM, 0))
            dv_fold.append(dv_t + pltpu.roll(dv_t, HEAD_DIM, 0))
        low_rows = lax.broadcasted_iota(jnp.int32, (128, 1), 0) < HEAD_DIM
        dk = jnp.where(low_rows, dk_fold[0], dk_fold[1]).T
        dv = jnp.where(low_rows, dv_fold[0], dv_fold[1]).T
        dkp[...], dko[...], dkn[...] = dk[0:QB], dk[QB:2 * QB], dk[2 * QB:3 * QB]
        dvp[...], dvo[...], dvn[...] = dv[0:QB], dv[QB:2 * QB], dv[2 * QB:3 * QB]
        dkc_ref[...] += dk[3 * QB:]
        dvc_ref[...] += dv[3 * QB:]
    blk = pl.BlockSpec((QB, 128), lambda jj, b: (b, jj))
    cblk = pl.BlockSpec((tc, 128), lambda jj, b: (0, jj))
    part = jax.ShapeDtypeStruct((R, 256), F32)
    csum = jax.ShapeDtypeStruct((tc, 256), F32)
    outs = pl.pallas_call(
        body, name="attn_bwd", grid=(2, R // QB),
        in_specs=[qspec, *kspecs, *vspecs, pl.BlockSpec((QB, 512), lambda jj, b: (b, jj)), _probs_spec(nk)],
        out_specs=[pl.BlockSpec((QB, 512), lambda jj, b: (b, jj)), blk, blk, blk, blk, blk, blk, cblk, cblk,
                   pl.BlockSpec((8, 128), lambda jj, b: (0, 0))],
        out_shape=[jax.ShapeDtypeStruct((R, N_HEADS * HEAD_DIM), F32), part, part, part, part, part, part,
                   csum, csum, jax.ShapeDtypeStruct((8, 128), F32)],
        compiler_params=_params(48))(qkvr, *([qkvr] * 8), do, probs)
    return outs[0], outs[1:4], outs[4:7], outs[7], outs[8], outs[9]


def loss_head(x, nw, target, T):
    R, dm = x.shape
    nl = T // TMR

    def body(x_ref, nw_ref, t_ref, loss_ref, dx_ref, dnw_ref):
        i = pl.program_id(0)

        @pl.when(i == 0)
        def _():
            loss_ref[...] = jnp.zeros_like(loss_ref)
            dnw_ref[...] = jnp.zeros_like(dnw_ref)
        live = (i < nl).astype(F32)
        nwv = nw_ref[...]
        xv = x_ref[...]
        r = lax.rsqrt(jnp.mean(xv * xv, axis=-1, keepdims=True) + EPS)
        xh = xv * r
        err = xh * nwv - t_ref[...]
        per_row = jnp.mean(err * err, axis=-1, keepdims=True)
        loss_ref[...] += 0.5 * live * jnp.sum(per_row)
        dy = err * (live / dm)
        dnw_ref[...] += _colsum8(dy * xh)
        dxh = dy * nwv
        dx_ref[...] = r * (dxh - xh * jnp.mean(dxh * xh, axis=-1, keepdims=True))
    tile = pl.BlockSpec((TMR, dm), lambda i: (i, 0))
    return pl.pallas_call(
        body, name="loss_head", grid=(R // TMR,),
        in_specs=[tile, pl.BlockSpec((1, dm), lambda i: (0, 0)),
                  pl.BlockSpec((TMR, dm), lambda i: (jnp.minimum(i, nl - 1), 0))],
        out_specs=[pl.BlockSpec((8, 128), lambda i: (0, 0)), tile, pl.BlockSpec((8, dm), lambda i: (0, 0))],
        out_shape=[jax.ShapeDtypeStruct((8, 128), F32), jax.ShapeDtypeStruct((R, dm), F32),
                   jax.ShapeDtypeStruct((8, dm), F32)])(x, nw, target)


def adaln_fwd(cond, w_mod, b_mod):
    nl, dm, ns = w_mod.shape

    def body(c_ref, w_ref, b_ref, o_ref):
        cv = c_ref[...]
        s = (cv * _sigmoid(cv)).astype(BF16)
        o_ref[...] = jnp.dot(s, w_ref[...].astype(BF16), preferred_element_type=F32) + b_ref[...]
    return pl.pallas_call(
        body, name="adaln_fwd", grid=(nl,),
        in_specs=[pl.BlockSpec((16, dm), lambda l: (0, 0)), pl.BlockSpec((None, dm, ns), lambda l: (l, 0, 0)),
                  pl.BlockSpec((None, 1, ns), lambda l: (l, 0, 0))],
        out_specs=pl.BlockSpec((None, 16, ns), lambda l: (l, 0, 0)),
        out_shape=jax.ShapeDtypeStruct((nl, 16, ns), F32), compiler_params=_params(48))(cond, w_mod, b_mod)


def adaln_bwd(cond, dmod, w_mod):
    nl, dm, ns = w_mod.shape

    def body(c_ref, d_ref, w_ref, gw_ref, ds_ref):
        l = pl.program_id(0)

        @pl.when(l == 0)
        def _():
            ds_ref[...] = jnp.zeros_like(ds_ref)
        cv = c_ref[...]
        s = (cv * _sigmoid(cv)).astype(BF16)
        dv = d_ref[...].astype(BF16)
        gw_ref[...] = lax.dot_general(s, dv, (((0,), (0,)), ((), ())), preferred_element_type=F32)
        ds_ref[...] += lax.dot_general(dv, w_ref[...].astype(BF16), (((1,), (1,)), ((), ())),
                                       preferred_element_type=F32)
    return pl.pallas_call(
        body, name="adaln_bwd", grid=(nl,),
        in_specs=[pl.BlockSpec((16, dm), lambda l: (0, 0)), pl.BlockSpec((None, 16, ns), lambda l: (l, 0, 0)),
                  pl.BlockSpec((None, dm, ns), lambda l: (l, 0, 0))],
        out_specs=[pl.BlockSpec((None, dm, ns), lambda l: (l, 0, 0)), pl.BlockSpec((16, dm), lambda l: (0, 0))],
        out_shape=[jax.ShapeDtypeStruct((nl, dm, ns), F32), jax.ShapeDtypeStruct((16, dm), F32)],
        compiler_params=_params(48))(cond, dmod, w_mod)


def _me():
    return lax.axis_index("x"), lax.axis_index("y"), lax.axis_index("c")


def allgather8(block):
    m_per, n = block.shape

    def body(x_ref, out_ref, send_sems, recv_sems, local_sem):
        x, y, c = _me()
        me, sibling = (x, y, c), (x, y, 1 - c)
        chips = [(1 - x, y), (x, 1 - y), (1 - x, 1 - y)]

        def rows(px, py, pc):
            return out_ref.at[pl.ds((4 * px + 2 * py + pc) * m_per, m_per), :]

        def copy(k, blk, to, src=None):
            return pltpu.make_async_remote_copy(
                src_ref=rows(*blk) if src is None else src, dst_ref=rows(*blk),
                send_sem=send_sems.at[k], recv_sem=recv_sems.at[k], device_id=to, device_id_type=MESH)
        mine = pltpu.make_async_copy(x_ref, rows(*me), local_sem)
        mine.start()
        first = [copy(0, me, sibling, src=x_ref)]
        first += [copy(1 + j, me, (*chip, c), src=x_ref) for j, chip in enumerate(chips)]
        for cp in first:
            cp.start()
        passed = [copy(4 + j, (*chip, c), sibling) for j, chip in enumerate(chips)]
        for j, chip in enumerate(chips):
            copy(1 + j, (*chip, c), me).wait_recv()
            passed[j].start()
        copy(0, sibling, me).wait_recv()
        for j, chip in enumerate(chips):
            copy(4 + j, (*chip, 1 - c), me).wait_recv()
        for cp in first + passed:
            cp.wait_send()
        mine.wait()
    return pl.pallas_call(
        body, name="allgather8",
        out_shape=jax.ShapeDtypeStruct((N_DEV * m_per, n), block.dtype),
        in_specs=[pl.BlockSpec(memory_space=pltpu.VMEM)],
        out_specs=pl.BlockSpec(memory_space=pltpu.VMEM),
        scratch_shapes=[pltpu.SemaphoreType.DMA((7,)), pltpu.SemaphoreType.DMA((7,)), pltpu.SemaphoreType.DMA],
        compiler_params=_params(48))(block)


def _other_chips(x, y):
    return [(1 - x, y), (x, 1 - y), (1 - x, 1 - y)]


_HBM = pl.BlockSpec(memory_space=pltpu.HBM)
_SEM = pl.BlockSpec(memory_space=pltpu.SEMAPHORE)
_ANY = pl.BlockSpec(memory_space=pl.ANY)
_EFFECT = pltpu.SideEffectType.DATAFLOW_SIDE_EFFECTING


def _in_hbm(v):
    return pltpu.with_memory_space_constraint(v, pltpu.HBM)


def cast_into_slot(w, chip_id):
    kb, nb = w.shape
    tr = _row_tile(kb)

    def body(chip_ref, w_ref, o_ref):
        del chip_ref
        o_ref[...] = w_ref[...].astype(BF16)
    return pl.pallas_call(
        body, name="cast_into_slot",
        grid_spec=pltpu.PrefetchScalarGridSpec(
            num_scalar_prefetch=1, grid=(kb // tr,),
            in_specs=[pl.BlockSpec((tr, nb), lambda i, chip: (i, 0))],
            out_specs=pl.BlockSpec((None, tr, nb), lambda i, chip: (chip[0], i, 0))),
        out_shape=jax.ShapeDtypeStruct((N_CHIP, kb, nb), BF16))(chip_id, w)


def _split_copies(mode, srcs, lands, send_sems, recv_sems):
    x, y, c = _me()
    out = []
    for t in range(len(lands)):
        for k, chip in enumerate(_other_chips(x, y)):
            if mode == "gather":
                src = dst = lands[t].at[2 * x + y]
                landed = lands[t].at[2 * chip[0] + chip[1]]
            else:
                src, dst, landed = srcs[t].at[2 * chip[0] + chip[1]], lands[t].at[k], lands[t].at[k]
            send = pltpu.make_async_remote_copy(src_ref=src, dst_ref=dst, send_sem=send_sems.at[3 * t + k],
                                                recv_sem=recv_sems.at[3 * t + k], device_id=(*chip, c),
                                                device_id_type=MESH)
            recv = pltpu.make_async_remote_copy(src_ref=src, dst_ref=landed, send_sem=send_sems.at[3 * t + k],
                                                recv_sem=recv_sems.at[3 * t + k], device_id=(*chip, c),
                                                device_id_type=MESH)
            out.append((send, recv))
    return out


def exchange_start(name, mode, srcs, lands, after):
    ns, nl = len(srcs), len(lands)
    na = ns + nl

    def body(*refs):
        src_refs, land_refs = refs[:ns], refs[ns:na]
        send_sems, recv_sems = refs[na + 1], refs[na + 2]
        token = refs[-1]
        for send, _ in _split_copies(mode, src_refs, land_refs, send_sems, recv_sems):
            send.start()
        token[...] = jnp.zeros_like(token)
    arrays = list(srcs) + list(lands)
    outs = pl.pallas_call(
        body, name=name,
        out_shape=(pltpu.SemaphoreType.DMA((3 * nl,)), pltpu.SemaphoreType.DMA((3 * nl,)),
                   *[pltpu.HBM(v.shape, v.dtype) for v in arrays], jax.ShapeDtypeStruct((8, 128), F32)),
        in_specs=[_HBM] * na + [_ANY],
        out_specs=(_SEM, _SEM, *[_HBM] * na, pl.BlockSpec(memory_space=pltpu.VMEM)),
        input_output_aliases={i: 2 + i for i in range(na)},
        compiler_params=pltpu.CompilerParams(has_side_effects=_EFFECT))(*[_in_hbm(v) for v in arrays], after)
    return outs[0], outs[1], list(outs[2:2 + ns]), list(outs[2 + ns:2 + na]), outs[-1]


def exchange_wait(name, mode, send_sems, recv_sems, srcs, lands, after):
    ns, nl = len(srcs), len(lands)
    na = ns + nl

    def body(*refs):
        for _, recv in _split_copies(mode, refs[:ns], refs[ns:na], refs[na], refs[na + 1]):
            recv.wait_send()
            recv.wait_recv()
    arrays = list(srcs) + list(lands)
    outs = pl.pallas_call(
        body, name=name,
        out_shape=[pltpu.HBM(v.shape, v.dtype) for v in arrays],
        in_specs=[_HBM] * na + [_SEM, _SEM, _ANY], out_specs=[_HBM] * na,
        input_output_aliases={i: i for i in range(na)},
        compiler_params=pltpu.CompilerParams(has_side_effects=_EFFECT))(*arrays, send_sems, recv_sems, after)
    return list(outs[:ns]), list(outs[ns:])


def swap_with_sibling(vs):
    n = len(vs)

    def body(*refs):
        v_refs, out_refs, send_sems, recv_sems = refs[:n], refs[n:2 * n], refs[2 * n], refs[2 * n + 1]
        x, y, c = _me()
        cps = [pltpu.make_async_remote_copy(src_ref=v_refs[t], dst_ref=out_refs[t], send_sem=send_sems.at[t],
                                            recv_sem=recv_sems.at[t], device_id=(x, y, 1 - c), device_id_type=MESH)
               for t in range(n)]
        for cp in cps:
            cp.start()
        for cp in cps:
            cp.wait()
    return pl.pallas_call(
        body, name="swap_with_sibling", out_shape=[jax.ShapeDtypeStruct(v.shape, v.dtype) for v in vs],
        in_specs=[_ANY] * n, out_specs=[_ANY] * n,
        scratch_shapes=[pltpu.SemaphoreType.DMA((n,)), pltpu.SemaphoreType.DMA((n,))])(*vs)


def sum_slots(parts):
    n, rows, w = parts.shape
    tr = _row_tile(rows)

    def body(p_ref, o_ref):
        acc = p_ref[0].astype(F32)
        for k in range(1, n):
            acc = acc + p_ref[k].astype(F32)
        o_ref[...] = acc
    return pl.pallas_call(
        body, name="sum_slots", grid=(rows // tr,),
        in_specs=[pl.BlockSpec((n, tr, w), lambda i: (0, i, 0))], out_specs=pl.BlockSpec((tr, w), lambda i: (i, 0)),
        out_shape=jax.ShapeDtypeStruct((rows, w), F32), compiler_params=_params(48))(parts)


def sum_landed(landed, own, chip_id, layer, n_layers, buf):
    n, rows, w = landed.shape
    tr = _row_tile(rows)
    base = layer * (rows // tr)

    def compute(l_ref, g_ref, o_ref):
        acc = g_ref[...].astype(F32)
        for k in range(n):
            acc = acc + l_ref[k].astype(F32)
        o_ref[...] = acc
    in_specs = [pl.BlockSpec((n, tr, w), lambda i, chip: (0, i, 0)),
                pl.BlockSpec((None, tr, w), lambda i, chip: (chip[0], i, 0))]
    out_spec = pl.BlockSpec((tr, w), lambda i, chip: (base + i, 0))
    out_shape = jax.ShapeDtypeStruct((n_layers * rows, w), F32)
    if buf is None:
        def body(chip_ref, l_ref, g_ref, o_ref):
            del chip_ref
            compute(l_ref, g_ref, o_ref)
        return pl.pallas_call(
            body, name="sum_landed",
            grid_spec=pltpu.PrefetchScalarGridSpec(num_scalar_prefetch=1, grid=(rows // tr,), in_specs=in_specs,
                                                   out_specs=out_spec),
            out_shape=out_shape, compiler_params=_params(48))(chip_id, landed, own)

    def body(chip_ref, l_ref, g_ref, buf_ref, o_ref):
        del chip_ref, buf_ref
        compute(l_ref, g_ref, o_ref)
    return pl.pallas_call(
        body, name="sum_landed_into",
        grid_spec=pltpu.PrefetchScalarGridSpec(num_scalar_prefetch=1, grid=(rows // tr,),
                                               in_specs=in_specs + [_ANY], out_specs=out_spec),
        out_shape=out_shape, input_output_aliases={3: 0}, compiler_params=_params(48))(chip_id, landed, own, buf)


def adamw(w, ga, gb, m, v):
    rows, wd = w.shape
    tr = min(_row_tile(rows), 128)
    c1 = 1.0 / (1.0 - ADAM_B1 ** ADAM_STEP)
    c2 = 1.0 / (1.0 - ADAM_B2 ** ADAM_STEP)

    def update(wv, g, mv, vv, g_ref, d_ref, m_ref, v_ref):
        mn = ADAM_B1 * mv + (1.0 - ADAM_B1) * g
        vn = ADAM_B2 * vv + (1.0 - ADAM_B2) * (g * g)
        g_ref[...] = g
        m_ref[...] = mn
        v_ref[...] = vn
        d_ref[...] = -ADAM_LR * ((mn * c1) / (jnp.sqrt(vn * c2) + ADAM_EPS) + ADAM_WD * wv)
    tile = pl.BlockSpec((tr, wd), lambda i: (i, 0))
    out = jax.ShapeDtypeStruct((rows, wd), F32)
    if gb is None:
        def body(w_ref, ga_ref, m_ref, v_ref, g_out, d_out, m_out, v_out):
            update(w_ref[...], ga_ref[...], m_ref[...], v_ref[...], g_out, d_out, m_out, v_out)
        return pl.pallas_call(body, name="adamw", grid=(rows // tr,), in_specs=[tile] * 4,
                              out_specs=[tile] * 4, out_shape=[out] * 4)(w, ga, m, v)

    def body(w_ref, ga_ref, gb_ref, m_ref, v_ref, g_out, d_out, m_out, v_out):
        update(w_ref[...], ga_ref[...] + gb_ref[...], m_ref[...], v_ref[...], g_out, d_out, m_out, v_out)
    return pl.pallas_call(body, name="adamw_sum", grid=(rows // tr,), in_specs=[tile] * 5,
                          out_specs=[tile] * 4, out_shape=[out] * 4)(w, ga, gb, m, v)


def _rope_tables(T, R):
    rows = T // GRID_W
    row = jnp.repeat(jnp.arange(rows), GRID_W).astype(F32)
    col = jnp.tile(jnp.arange(GRID_W), rows).astype(F32)
    n_freq = HEAD_DIM // 4
    inv_freq = ROPE_THETA ** (-jnp.arange(n_freq, dtype=F32) / n_freq)
    ang = jnp.concatenate([row[:, None] * inv_freq, col[:, None] * inv_freq], axis=-1)
    cos, sin = jnp.cos(ang), jnp.sin(ang)
    cs = jnp.tile(cos, (1, 4))
    sn = jnp.tile(jnp.concatenate([-sin, sin], axis=-1), (1, 2))
    pad = R - T
    return (jnp.concatenate([cs, jnp.ones((pad, 128), F32)], axis=0),
            jnp.concatenate([sn, jnp.zeros((pad, 128), F32)], axis=0))


def _pack(parts, mult=8 * 128):
    flat = jnp.concatenate([p.reshape(-1).astype(F32) for p in parts])
    pad = (-flat.shape[0]) % mult
    return jnp.pad(flat, (0, pad)).reshape(-1, 128)


def _unpack(buf, shapes):
    flat = buf.reshape(-1)
    out, o = [], 0
    for s in shapes:
        n = 1
        for d in s:
            n *= d
        out.append(flat[o:o + n].reshape(s))
        o += n
    return out


def kernel(x, c, ctx, c_ctx, w_mod, b_mod, norm_mix, norm_ffn, w_in_ab, conv_a, conv_b, conv_b_bias, ln_b_gain, ln_b_bias, w_out_ab, w_qkv, w_o, sinks, w_up, w_conv_ffn, w_down, final_norm, loss_target, m_c_ctx, m_w_mod, m_b_mod, m_norm_mix, m_norm_ffn, m_w_in_ab, m_conv_a, m_conv_b, m_conv_b_bias, m_ln_b_gain, m_ln_b_bias, m_w_out_ab, m_w_qkv, m_w_o, m_sinks, m_w_up, m_w_conv_ffn, m_w_down, m_final_norm, v_c_ctx, v_w_mod, v_b_mod, v_norm_mix, v_norm_ffn, v_w_in_ab, v_conv_a, v_conv_b, v_conv_b_bias, v_ln_b_gain, v_ln_b_bias, v_w_out_ab, v_w_qkv, v_w_o, v_sinks, v_w_up, v_w_conv_ffn, v_w_down, v_final_norm):
    T, dm = x.shape[1], x.shape[2]
    tc = ctx.shape[1]
    R = T + tc
    depth = w_mod.shape[0]
    ax, ay, ac = lax.axis_index("x"), lax.axis_index("y"), lax.axis_index("c")
    chip = 2 * ax + ay
    dev = 4 * ax + 2 * ay + ac

    small_w = [conv_a, conv_b, w_conv_ffn]
    gathered = allgather8(_pack([c] + small_w)).reshape(N_DEV, -1)
    cond8 = gathered[:, :dm]
    off = dm
    full_small = []
    for wsh in small_w:
        n = wsh.size
        per_chip = gathered[0::2, off:off + n].reshape((N_CHIP,) + wsh.shape)
        full_small.append(jnp.concatenate([per_chip[q] for q in range(N_CHIP)], axis=-1))
        off += n
    conv_a_f, conv_b_f, w_conv_ffn_f = full_small
    cond = jnp.concatenate([cond8, c_ctx[None, :], jnp.zeros((7, dm), F32)], axis=0)

    ns_mod = w_mod.shape[2]
    b_mod_sh = lax.dynamic_slice_in_dim(b_mod, chip * ns_mod, ns_mod, axis=1)[:, None, :]
    mod_sh = adaln_fwd(cond, w_mod, b_mod_sh)
    mod_all = allgather8(mod_sh.reshape(depth * 16, ns_mod)).reshape(N_DEV, depth, 16, ns_mod)
    mod_full = jnp.concatenate([mod_all[2 * q] for q in range(N_CHIP)], axis=-1)
    mine = lax.dynamic_index_in_dim(mod_full, dev, axis=1, keepdims=False)
    mods = jnp.stack([mine, mod_full[:, 8]], axis=1).reshape(depth, 2, 6, dm)

    masters = {"w_in_ab": w_in_ab, "w_out_ab": w_out_ab, "w_qkv": w_qkv, "w_o": w_o, "w_up": w_up, "w_down": w_down}
    chip_id = chip.astype(jnp.int32).reshape(1)

    def half_weights(l, half):
        if half == 1:
            return [("w_up", l), ("w_down", l)]
        return [("w_in_ab", l // 2), ("w_out_ab", l // 2)] if l % 2 == 0 else [("w_qkv", l // 2), ("w_o", l // 2)]
    in_flight, after = {}, mods
    for l in range(depth):
        for half in range(2):
            lands = [cast_into_slot(masters[n][j], chip_id) for n, j in half_weights(l, half)]
            send_sems, recv_sems, _, lands, after = exchange_start(f"gather_start_{l}_{half}", "gather", [], lands, after)
            in_flight[l, half] = (send_sems, recv_sems, lands)
    mods = mods + after[0, 0]

    def gathered_weights(l, half, after):
        send_sems, recv_sems, lands = in_flight[l, half]
        _, landed = exchange_wait(f"gather_wait_{l}_{half}", "gather", send_sems, recv_sems, [], lands, after)
        return dict(zip([n for n, _ in half_weights(l, half)], landed))

    cs, sn = _rope_tables(T, R)
    bias = window_bias(T, R)
    sinks_flat = sinks.reshape(-1)

    xs = jnp.concatenate([x[0], ctx[0]], axis=0)
    saved, W = [], []
    h1 = norm_mod_fwd(xs, norm_mix[0][None], mods[0], 0, T)
    for l in range(depth):
        e = l // 2
        wl = gathered_weights(l, 0, h1)
        W.append(wl)
        s = {"x0": xs, "h1": h1}
        if l % 2 == 0:
            p = mm_nn(h1, wl["w_in_ab"], BF16)
            yab, y1, x1, h2 = mixer_fwd(p, conv_a_f[e], conv_b_f[e], conv_b_bias[e][None], ln_b_gain[e][None],
                                        ln_b_bias[e][None], wl["w_out_ab"], xs, norm_ffn[l][None], mods[l], 2, 3, T)
            s.update(p=p, mix=yab)
        else:
            qkvr = mm_qkv_rope(h1, wl["w_qkv"], cs, sn)
            att, probs = attn_fwd(qkvr, sinks_flat[e * N_HEADS:(e + 1) * N_HEADS], bias, T)
            s.update(qkvr=qkvr, mix=att, probs=probs)
            y1, x1, h2 = mm_resid_norm_fwd(att, wl["w_o"], xs, norm_ffn[l][None], mods[l], mods[l], 2, 3, T)
        wl.update(gathered_weights(l, 1, h2))
        u = mm_nn(h2, wl["w_up"], BF16)
        if l + 1 < depth:
            z, y2, xs, h1 = ffn_mid_fwd(u, w_conv_ffn_f[l], wl["w_down"], x1, norm_mix[l + 1][None], mods[l],
                                        mods[l + 1], 5, 0, T)
        else:
            z, y2, xs = ffn_mid_fwd(u, w_conv_ffn_f[l], wl["w_down"], x1, None, mods[l], None, 5, 0, T)
        s.update(y1=y1, x1=x1, h2=h2, u=u, z=z, y2=y2)
        saved.append(s)

    loss_part, dx, d_final = loss_head(xs, final_norm[None], loss_target[0], T)
    loss = lax.psum(loss_part[0, 0], ("x", "y", "c"))

    d_mods, d_norm_mix, d_norm_ffn = [None] * depth, [None] * depth, [None] * depth
    d_conv_a, d_conv_b, d_vecs, d_sinks, d_wc = [None] * 2, [None] * 2, [None] * 2, [None] * 2, [None] * depth
    dss1, dss2, dg1, dg2 = [None] * depth, [None] * depth, [None] * depth, [None] * depth
    scattering = {}

    def scatter(l, half, G, after):
        grads_h = [G[n] for n, _ in half_weights(l, half)]
        lands = [lax.empty((N_CHIP - 1, *g.shape[1:]), g.dtype) for g in grads_h]
        send_sems, recv_sems, grads_h, lands, token = exchange_start(
            f"scatter_start_{l}_{half}", "scatter", grads_h, lands, after)
        scattering[l, half] = (send_sems, recv_sems, grads_h, lands)
        return token

    dy2, dg2[depth - 1] = resid_bwd(dx, saved[depth - 1]["y2"], mods[depth - 1], 5, T)
    pending = 0.0
    for l in reversed(range(depth)):
        e = l // 2
        s, wl = saved[l], W[l]
        G = {}
        G["w_down"] = mm_tn(s["z"], dy2, "row", wl["w_down"])
        duc, d_wc[l] = ffn_mid_bwd(mm_nt(dy2, wl["w_down"], BF16), s["u"], w_conv_ffn_f[l] + pending, T)
        du, dx, dy1, dss2[l], d_norm_ffn[l], dg1[l] = ffn_up_bwd(
            duc, w_conv_ffn_f[l], wl["w_up"], s["x1"], norm_ffn[l][None], mods[l], dx, s["y1"], mods[l], 3, 2, T)
        G["w_up"] = mm_tn(s["h2"], du, "col", wl["w_up"])
        started = scatter(l, 1, G, du)[0, 0]
        if l % 2 == 0:
            G["w_out_ab"] = mm_tn(s["mix"], dy1, "row", wl["w_out_ab"])
            dyab = mm_nt(dy1, wl["w_out_ab"], F32)
            dmid, d_conv_a[e], d_conv_b[e], d_vecs[e] = convmix_bwd1(
                dyab, s["p"], conv_a_f[e] + started, conv_b_f[e], conv_b_bias[e][None], ln_b_gain[e][None],
                ln_b_bias[e][None], T)
            if l > 0:
                dp, dx, dy2, dss1[l], d_norm_mix[l], dg2[l - 1] = mixer_in_bwd(
                    dmid, s["p"], conv_a_f[e], conv_b_f[e], wl["w_in_ab"], s["x0"], norm_mix[l][None], mods[l], dx,
                    saved[l - 1]["y2"], mods[l - 1], 0, 5, T)
            else:
                dp, dx, dss1[l], d_norm_mix[l] = mixer_in_bwd(
                    dmid, s["p"], conv_a_f[e], conv_b_f[e], wl["w_in_ab"], s["x0"], norm_mix[l][None], mods[l], dx,
                    None, None, 0, 0, T)
            G["w_in_ab"] = mm_tn(s["h1"], dp, "col", wl["w_in_ab"])
        else:
            G["w_o"] = mm_tn(s["mix"], dy1, "row", wl["w_o"])
            datt = mm_nt(dy1, wl["w_o"], BF16)
            dq, dks, dvs, dkc, dvc, d_sinks[e] = attn_bwd(s["qkvr"], datt, s["probs"], T)
            dqkv, dx, dy2, dss1[l], d_norm_mix[l], dg2[l - 1] = attn_in_bwd(
                dq, dks, dvs, dkc, dvc, cs + started, sn, wl["w_qkv"], s["x0"], norm_mix[l][None], mods[l], dx,
                saved[l - 1]["y2"], mods[l - 1], 0, 5, T)
            G["w_qkv"] = mm_tn(s["h1"], dqkv, "col", wl["w_qkv"])
        token = scatter(l, 0, G, dx)
        pending = token[0, 0]
    grad_x = dx[:T][None]
    for l in range(depth):
        a1, a2 = dss1[l].sum(2), dss2[l].sum(2)
        d_mods[l] = jnp.stack([a1[:, 0], a1[:, 1], dg1[l].sum(1), a2[:, 0], a2[:, 1], dg2[l].sum(1)], axis=1)

    d_mods = jnp.stack(d_mods)
    summed_parts = [
        d_mods[:, 1],
        jnp.stack(d_norm_mix).sum(1), jnp.stack(d_norm_ffn).sum(1),
        jnp.stack(d_conv_a).sum(2), jnp.stack(d_conv_b).sum(2),
        jnp.stack(d_vecs).sum(2),
        jnp.stack(d_sinks)[:, 0, :N_HEADS],
        jnp.stack(d_wc).sum(2), d_final.sum(0) + pending]
    summed_shapes = [p.shape for p in summed_parts]
    n_own = depth * 6 * dm
    pack = _pack([d_mods[:, 0]] + summed_parts)
    parts = allgather8(pack).reshape(N_DEV, -1, 128)
    total = sum_slots(parts)
    own_rows = parts.reshape(N_DEV, -1)[:, :n_own].reshape(N_DEV, depth, 6 * dm)
    (dmod_ctx, g_norm_mix, g_norm_ffn, g_conv_a, g_conv_b, g_vecs, g_sinks, g_wc, g_final) = _unpack(
        total.reshape(-1)[n_own:], summed_shapes)
    dmod_rows = jnp.concatenate([jnp.moveaxis(own_rows, 0, 1), dmod_ctx.reshape(depth, 1, 6 * dm),
                                 jnp.zeros((depth, 7, 6 * dm), F32)], axis=1)
    g_b_mod = dmod_rows.sum(1)
    dmod_sh = lax.dynamic_slice_in_dim(dmod_rows, chip * ns_mod, ns_mod, axis=2)
    g_w_mod, dsilu = adaln_bwd(cond, dmod_sh, w_mod)
    dsilu_all = allgather8(dsilu[8:16]).reshape(N_DEV, 8, dm)
    dsilu_ctx = sum_slots(dsilu_all[0::2])[0]
    sg = jax.nn.sigmoid(c_ctx)
    g_c_ctx = dsilu_ctx * (sg * (1.0 + c_ctx * (1.0 - sg)))

    def shard_cols(full, width):
        return lax.dynamic_slice_in_dim(full, chip * width, width, axis=full.ndim - 1)
    g_conv_a_s = shard_cols(g_conv_a, conv_a.shape[-1])
    g_conv_b_s = shard_cols(g_conv_b, conv_b.shape[-1])
    g_wc_s = shard_cols(g_wc, w_conv_ffn.shape[-1])

    grads, deltas, new_m, new_v = {}, {}, {}, {}

    def step_2d(name, wv, ga, gb, mv, vv):
        shp = wv.shape
        r2 = lambda t: t.reshape(-1, shp[-1])
        g, d, mn, vn = adamw(r2(wv), r2(ga), None if gb is None else r2(gb), r2(mv), r2(vv))
        grads[name], deltas[name], new_m[name], new_v[name] = (t.reshape(shp) for t in (g, d, mn, vn))

    step_2d("w_mod", w_mod, g_w_mod, None, m_w_mod, v_w_mod)
    sums = {n: None for n in masters}
    for l in reversed(range(depth)):
        for half in (1, 0):
            send_sems, recv_sems, grads_h, lands = scattering[l, half]
            grads_h, landed = exchange_wait(f"scatter_wait_{l}_{half}", "scatter", send_sems, recv_sems, grads_h,
                                            lands, deltas["w_mod"])
            for (n, j), own, arr in zip(half_weights(l, half), grads_h, landed):
                sums[n] = sum_landed(arr, own, chip_id, j, masters[n].shape[0], sums[n])
    moments = {"w_in_ab": (m_w_in_ab, v_w_in_ab), "w_out_ab": (m_w_out_ab, v_w_out_ab),
               "w_qkv": (m_w_qkv, v_w_qkv), "w_o": (m_w_o, v_w_o), "w_up": (m_w_up, v_w_up),
               "w_down": (m_w_down, v_w_down)}
    others = swap_with_sibling([sums[name] for name in masters])
    for (name, wv), other in zip(masters.items(), others):
        step_2d(name, wv, sums[name].reshape(wv.shape), other.reshape(wv.shape), *moments[name])

    small = [("c_ctx", c_ctx, g_c_ctx, m_c_ctx, v_c_ctx), ("b_mod", b_mod, g_b_mod, m_b_mod, v_b_mod),
             ("norm_mix", norm_mix, g_norm_mix, m_norm_mix, v_norm_mix),
             ("norm_ffn", norm_ffn, g_norm_ffn, m_norm_ffn, v_norm_ffn),
             ("conv_a", conv_a, g_conv_a_s, m_conv_a, v_conv_a), ("conv_b", conv_b, g_conv_b_s, m_conv_b, v_conv_b),
             ("conv_b_bias", conv_b_bias, g_vecs[:, 0], m_conv_b_bias, v_conv_b_bias),
             ("ln_b_gain", ln_b_gain, g_vecs[:, 1], m_ln_b_gain, v_ln_b_gain),
             ("ln_b_bias", ln_b_bias, g_vecs[:, 2], m_ln_b_bias, v_ln_b_bias),
             ("sinks", sinks, g_sinks, m_sinks, v_sinks),
             ("w_conv_ffn", w_conv_ffn, g_wc_s, m_w_conv_ffn, v_w_conv_ffn),
             ("final_norm", final_norm, g_final, m_final_norm, v_final_norm)]
    shapes = [t[1].shape for t in small]
    packed = [_pack([t[k] for t in small]) for k in (1, 2, 3, 4)]
    n_real = sum(t[1].size for t in small)
    lane_id = jnp.arange(packed[3].size).reshape(packed[3].shape)
    packed[3] = jnp.where(lane_id < n_real, packed[3], 1.0)
    outs = adamw(packed[0], packed[1], None, packed[2], packed[3])
    for (name, *_), g, d, mn, vn in zip(small, *[_unpack(o, shapes) for o in outs]):
        grads[name], deltas[name], new_m[name], new_v[name] = g, d, mn, vn

    order = ["c_ctx", "w_mod", "b_mod", "norm_mix", "norm_ffn", "w_in_ab", "conv_a", "conv_b", "conv_b_bias",
             "ln_b_gain", "ln_b_bias", "w_out_ab", "w_qkv", "w_o", "sinks", "w_up", "w_conv_ffn", "w_down",
             "final_norm"]
    return (loss, grad_x, *[grads[n] for n in order], *[deltas[n] for n in order],
            *[new_m[n] for n in order], *[new_v[n] for n in order])
```

```python
import jax
import jax.numpy as jnp
from jax import lax
from jax.experimental import pallas as pl
from jax.experimental.pallas import tpu as pltpu

F32 = jnp.float32
BF16 = jnp.bfloat16
MESH = pl.DeviceIdType.MESH

EPS = 1e-6
NEG_INF = -1e30
GRID_W = 64
HEAD_DIM = 64
N_HEADS = 16
WINDOW = 128
QB = 128
ROPE_THETA = 10000.0
A_W = 512
B_CONV = 31
D_FF = 2816
ADAM_LR, ADAM_B1, ADAM_B2, ADAM_EPS, ADAM_WD, ADAM_STEP = 0.001, 0.9, 0.999, 1e-8, 0.01, 10

TMR = 256
HALO = 16
N_DEV = 8
N_CHIP = 4


def _params(vmem_mb=None):
    if vmem_mb is None:
        return pltpu.CompilerParams()
    return pltpu.CompilerParams(vmem_limit_bytes=vmem_mb * 1024 * 1024)


def _row_tile(rows, cap=768):
    for t in (2816, 1408, 768, 704, 512, 384, 256, 128, 64, 32, 16, 8):
        if t <= cap and rows % t == 0:
            return t
    raise ValueError(rows)


def _colsum8(v):
    r, c = v.shape
    return v.reshape(r // 8, 8, c).sum(axis=0)


def _sigmoid(v):
    return 0.5 * jnp.tanh(0.5 * v) + 0.5


def mm_nn(a, w, out_dtype):
    R = a.shape[0]
    _, kb, nb = w.shape
    tm = _row_tile(R)

    def body(a_ref, w_ref, o_ref):
        av = a_ref[...].astype(BF16)
        for q in range(N_CHIP):
            o_ref[:, q * nb:(q + 1) * nb] = jnp.dot(av, w_ref[q], preferred_element_type=F32).astype(o_ref.dtype)
    return pl.pallas_call(
        body, name="mm_nn_col", grid=(R // tm,),
        in_specs=[pl.BlockSpec((tm, kb), lambda i: (i, 0)),
                  pl.BlockSpec((N_CHIP, kb, nb), lambda i: (0, 0, 0), pipeline_mode=pl.Buffered(1))],
        out_specs=pl.BlockSpec((tm, N_CHIP * nb), lambda i: (i, 0)),
        out_shape=jax.ShapeDtypeStruct((R, N_CHIP * nb), out_dtype),
        compiler_params=_params(48))(a, w)


def mm_nt(d, w, out_dtype):
    R = d.shape[0]
    _, kb, nb = w.shape
    tm = _row_tile(R)
    contract_last = (((1,), (1,)), ((), ()))
    resident = pl.BlockSpec((N_CHIP, kb, nb), lambda i: (0, 0, 0), pipeline_mode=pl.Buffered(1))

    def body(d_ref, w_ref, o_ref):
        wv = w_ref[...].reshape(N_CHIP * kb, nb)
        o_ref[...] = lax.dot_general(d_ref[...].astype(BF16), wv, contract_last,
                                     preferred_element_type=F32).astype(o_ref.dtype)
    return pl.pallas_call(
        body, name="mm_nt_row", grid=(R // tm,),
        in_specs=[pl.BlockSpec((tm, nb), lambda i: (i, 0)), resident],
        out_specs=pl.BlockSpec((tm, N_CHIP * kb), lambda i: (i, 0)),
        out_shape=jax.ShapeDtypeStruct((R, N_CHIP * kb), out_dtype),
        compiler_params=_params(48))(d, w)


def mm_tn(a, d, kind, like):
    R = a.shape[0]
    _, kb, nb = like.shape
    tm = _row_tile(R, 1408 if kind == "col" else 768)
    nsteps = R // tm
    contract_rows = (((0,), (0,)), ((), ()))
    out_shape = jax.ShapeDtypeStruct(like.shape, BF16)

    def accumulate(a_ref, d_ref, acc_ref):
        @pl.when(pl.program_id(1) == 0)
        def _():
            acc_ref[...] = jnp.zeros_like(acc_ref)
        acc_ref[...] += lax.dot_general(a_ref[...].astype(BF16), d_ref[...].astype(BF16), contract_rows,
                                        preferred_element_type=F32)
    if kind == "col":
        def body(a_ref, d_ref, o_ref, acc_ref):
            accumulate(a_ref, d_ref, acc_ref)

            @pl.when(pl.program_id(1) == nsteps - 1)
            def _():
                o_ref[...] = acc_ref[...].astype(BF16)
        return pl.pallas_call(
            body, name="mm_tn_col", grid=(N_CHIP, nsteps),
            in_specs=[pl.BlockSpec((tm, kb), lambda q, i: (i, 0)), pl.BlockSpec((tm, nb), lambda q, i: (i, q))],
            out_specs=pl.BlockSpec((None, kb, nb), lambda q, i: (q, 0, 0)), out_shape=out_shape,
            scratch_shapes=[pltpu.VMEM((kb, nb), F32)], compiler_params=_params(48))(a, d)
    tn = 512

    def body(a_ref, d_ref, o_ref, acc_ref):
        accumulate(a_ref, d_ref, acc_ref)

        @pl.when(pl.program_id(1) == nsteps - 1)
        def _():
            o_ref[...] = acc_ref[...].astype(BF16).reshape(N_CHIP, kb, tn)
    return pl.pallas_call(
        body, name="mm_tn_row", grid=(nb // tn, nsteps),
        in_specs=[pl.BlockSpec((tm, N_CHIP * kb), lambda n, i: (i, 0)), pl.BlockSpec((tm, tn), lambda n, i: (i, n))],
        out_specs=pl.BlockSpec((N_CHIP, kb, tn), lambda n, i: (0, 0, n)), out_shape=out_shape,
        scratch_shapes=[pltpu.VMEM((N_CHIP * kb, tn), F32)], compiler_params=_params(48))(a, d)


def _seg(i, T):
    return (i >= T // TMR).astype(jnp.int32)


def norm_mod_fwd(x, nw, mod, k, T):
    R, dm = x.shape

    def body(x_ref, nw_ref, mod_ref, h_ref):
        seg = _seg(pl.program_id(0), T)
        sh = mod_ref[seg, pl.ds(k, 1), :]
        sc = mod_ref[seg, pl.ds(k + 1, 1), :]
        xv = x_ref[...]
        r = lax.rsqrt(jnp.mean(xv * xv, axis=-1, keepdims=True) + EPS)
        h_ref[...] = ((xv * r * nw_ref[...]) * (1.0 + sc) + sh).astype(BF16)
    return pl.pallas_call(
        body, name="norm_mod_fwd", grid=(R // TMR,),
        in_specs=[pl.BlockSpec((TMR, dm), lambda i: (i, 0)),
                  pl.BlockSpec((1, dm), lambda i: (0, 0)),
                  pl.BlockSpec((2, 6, dm), lambda i: (0, 0, 0))],
        out_specs=pl.BlockSpec((TMR, dm), lambda i: (i, 0)),
        out_shape=jax.ShapeDtypeStruct((R, dm), BF16))(x, nw, mod)


def mm_resid_norm_fwd(a, w, x, nw, mod_g, mod_n, kg, kn, T):
    R, dm = x.shape
    _, kb, nb = w.shape

    def body(a_ref, w_ref, x_ref, nw_ref, mg_ref, mn_ref, y_ref, xo_ref, h_ref):
        seg = _seg(pl.program_id(0), T)
        yv = jnp.dot(a_ref[...].astype(BF16), w_ref[...].reshape(N_CHIP * kb, nb), preferred_element_type=F32)
        y_ref[...] = yv
        xv = x_ref[...] + mg_ref[seg, pl.ds(kg, 1), :] * yv
        xo_ref[...] = xv
        r = lax.rsqrt(jnp.mean(xv * xv, axis=-1, keepdims=True) + EPS)
        h_ref[...] = ((xv * r * nw_ref[...]) * (1.0 + mn_ref[seg, pl.ds(kn + 1, 1), :])
                      + mn_ref[seg, pl.ds(kn, 1), :]).astype(BF16)
    tile = pl.BlockSpec((TMR, dm), lambda i: (i, 0))
    modspec = pl.BlockSpec((2, 6, dm), lambda i: (0, 0, 0))
    return pl.pallas_call(
        body, name="mm_resid_norm_fwd", grid=(R // TMR,),
        in_specs=[pl.BlockSpec((TMR, N_CHIP * kb), lambda i: (i, 0)),
                  pl.BlockSpec((N_CHIP, kb, nb), lambda i: (0, 0, 0), pipeline_mode=pl.Buffered(1)),
                  tile, pl.BlockSpec((1, dm), lambda i: (0, 0)), modspec, modspec],
        out_specs=[tile, tile, tile],
        out_shape=[jax.ShapeDtypeStruct((R, dm), F32), jax.ShapeDtypeStruct((R, dm), F32),
                   jax.ShapeDtypeStruct((R, dm), BF16)],
        compiler_params=_params(48))(a, w, x, nw, mod_g, mod_n)


def _halo_specs(width, R):
    nblk = R // HALO
    per = TMR // HALO
    return (pl.BlockSpec((HALO, width), lambda i: (jnp.maximum(i * per - 1, 0), 0)),
            pl.BlockSpec((TMR, width), lambda i: (i, 0)),
            pl.BlockSpec((HALO, width), lambda i: (jnp.minimum((i + 1) * per, nblk - 1), 0)))


def _halo_live(i, T, R):
    nl = T // TMR
    return (i != 0) & (i != nl), (i != nl - 1) & (i != R // TMR - 1)


def _ext(refs, c0, cw, live, halo=HALO):
    pref, ref, nref = refs
    before = jnp.where(live[0], pref[:, c0:c0 + cw].astype(F32)[HALO - halo:], 0.0)
    after = jnp.where(live[1], nref[:, c0:c0 + cw].astype(F32)[:halo], 0.0)
    return jnp.concatenate([before, ref[:, c0:c0 + cw].astype(F32), after], axis=0)


def _at(ext, off, halo=HALO):
    n = ext.shape[0]
    s = (-off) % n
    y = pltpu.roll(ext, s, 0) if s else ext
    return y[halo:halo + TMR]


def ffn_mid_fwd(u, wc, w_down, x, nw, mod_g, mod_n, kg, kn, T):
    R, w2 = u.shape
    dm = x.shape[1]
    _, kb, nb = w_down.shape
    cw = 256
    with_norm = nw is not None

    def body(*refs):
        if with_norm:
            up_ref, u_ref, un_ref, wc_ref, w_ref, x_ref, nw_ref, mg_ref, mn_ref, z_ref, y_ref, xo_ref, h_ref = refs
        else:
            up_ref, u_ref, un_ref, wc_ref, w_ref, x_ref, mg_ref, z_ref, y_ref, xo_ref = refs
        i = pl.program_id(0)
        seg = _seg(i, T)
        live = _halo_live(i, T, R)

        def conv(c0):
            e = _ext((up_ref, u_ref, un_ref), c0, cw, live, 8)
            return (wc_ref[pl.ds(0, 1), c0:c0 + cw] * _at(e, -1, 8) + wc_ref[pl.ds(1, 1), c0:c0 + cw] * _at(e, 0, 8)
                    + wc_ref[pl.ds(2, 1), c0:c0 + cw] * _at(e, 1, 8))
        yv = None
        for j in range(D_FF // cw):
            a = conv(j * cw)
            g = conv(D_FF + j * cw)
            zc = (g * _sigmoid(g) * a).astype(BF16)
            z_ref[:, j * cw:(j + 1) * cw] = zc
            t = jnp.dot(zc, w_ref[j * cw:(j + 1) * cw, :], preferred_element_type=F32)
            yv = t if yv is None else yv + t
        y_ref[...] = yv
        xv = x_ref[...] + mg_ref[seg, pl.ds(kg, 1), :] * yv
        xo_ref[...] = xv
        if with_norm:
            r = lax.rsqrt(jnp.mean(xv * xv, axis=-1, keepdims=True) + EPS)
            h_ref[...] = ((xv * r * nw_ref[...]) * (1.0 + mn_ref[seg, pl.ds(kn + 1, 1), :])
                          + mn_ref[seg, pl.ds(kn, 1), :]).astype(BF16)
    tile = pl.BlockSpec((TMR, dm), lambda i: (i, 0))
    modspec = pl.BlockSpec((2, 6, dm), lambda i: (0, 0, 0))
    w_down = w_down.reshape(N_CHIP * kb, nb)
    in_specs = [*_halo_specs(w2, R), pl.BlockSpec((3, w2), lambda i: (0, 0)),
                pl.BlockSpec((N_CHIP * kb, nb), lambda i: (0, 0), pipeline_mode=pl.Buffered(1)), tile]
    out_specs = [pl.BlockSpec((TMR, D_FF), lambda i: (i, 0)), tile, tile]
    out_shape = [jax.ShapeDtypeStruct((R, D_FF), BF16), jax.ShapeDtypeStruct((R, dm), F32),
                 jax.ShapeDtypeStruct((R, dm), F32)]
    if with_norm:
        return pl.pallas_call(
            body, name="ffn_mid_fwd", grid=(R // TMR,),
            in_specs=in_specs + [pl.BlockSpec((1, dm), lambda i: (0, 0)), modspec, modspec],
            out_specs=out_specs + [tile], out_shape=out_shape + [jax.ShapeDtypeStruct((R, dm), BF16)],
            compiler_params=_params(48))(u, u, u, wc, w_down, x, nw, mod_g, mod_n)
    return pl.pallas_call(
        body, name="ffn_mid_fwd_last", grid=(R // TMR,), in_specs=in_specs + [modspec],
        out_specs=out_specs, out_shape=out_shape, compiler_params=_params(48))(u, u, u, wc, w_down, x, mod_g)


def ffn_mid_bwd(dz, u, wc, T):
    R, w2 = u.shape
    cw = 256

    def body(dz_ref, up_ref, u_ref, un_ref, wc_ref, duc_ref, dwc_ref):
        i = pl.program_id(0)
        live = _halo_live(i, T, R)

        @pl.when(i == 0)
        def _():
            dwc_ref[...] = jnp.zeros_like(dwc_ref)

        def taps(c0):
            e = _ext((up_ref, u_ref, un_ref), c0, cw, live, 8)
            return [_at(e, -1, 8), _at(e, 0, 8), _at(e, 1, 8)]

        def conv(t, c0):
            return (wc_ref[pl.ds(0, 1), c0:c0 + cw] * t[0] + wc_ref[pl.ds(1, 1), c0:c0 + cw] * t[1]
                    + wc_ref[pl.ds(2, 1), c0:c0 + cw] * t[2])
        for j in range(D_FF // cw):
            ca, cg = j * cw, D_FF + j * cw
            dzv = dz_ref[:, ca:ca + cw].astype(F32)
            ta, tg = taps(ca), taps(cg)
            a, g = conv(ta, ca), conv(tg, cg)
            sg = _sigmoid(g)
            da = dzv * (g * sg)
            dg = dzv * a * (sg * (1.0 + g * (1.0 - sg)))
            duc_ref[:, ca:ca + cw] = da.astype(BF16)
            duc_ref[:, cg:cg + cw] = dg.astype(BF16)
            for k in range(3):
                dwc_ref[k, :, ca:ca + cw] += _colsum8(da * ta[k])
                dwc_ref[k, :, cg:cg + cw] += _colsum8(dg * tg[k])
    return pl.pallas_call(
        body, name="ffn_mid_bwd", grid=(R // TMR,),
        in_specs=[pl.BlockSpec((TMR, D_FF), lambda i: (i, 0)), *_halo_specs(w2, R),
                  pl.BlockSpec((3, w2), lambda i: (0, 0))],
        out_specs=[pl.BlockSpec((TMR, w2), lambda i: (i, 0)), pl.BlockSpec((3, 8, w2), lambda i: (0, 0, 0))],
        out_shape=[jax.ShapeDtypeStruct((R, w2), BF16), jax.ShapeDtypeStruct((3, 8, w2), F32)],
        compiler_params=_params(48))(dz, u, u, u, wc)


def ffn_up_bwd(duc, wc, w_up, x, nw, mod_n, dxr, y, mod_g, kn, kg, T):
    R, w2 = duc.shape
    dm = x.shape[1]
    _, kb, nb = w_up.shape
    cw = 128
    contract_last = (((1,), (1,)), ((), ()))

    def body(dp_ref, d_ref, dn_ref, wc_ref, w_ref, x_ref, nw_ref, mn_ref, dxr_ref, y_ref, mg_ref,
             du_ref, dx_ref, dy_ref, dmod_ref, dnw_ref, dg_ref):
        i = pl.program_id(0)
        seg = _seg(i, T)
        live = _halo_live(i, T, R)

        @pl.when(i == 0)
        def _():
            dmod_ref[...] = jnp.zeros_like(dmod_ref)
            dnw_ref[...] = jnp.zeros_like(dnw_ref)
            dg_ref[...] = jnp.zeros_like(dg_ref)
        dhv = None
        for q in range(N_CHIP):
            for j in range(nb // cw):
                c0 = q * nb + j * cw
                e = _ext((dp_ref, d_ref, dn_ref), c0, cw, live, 8)
                du_ref[:, c0:c0 + cw] = (wc_ref[pl.ds(0, 1), c0:c0 + cw] * _at(e, 1, 8)
                                         + wc_ref[pl.ds(1, 1), c0:c0 + cw] * _at(e, 0, 8)
                                         + wc_ref[pl.ds(2, 1), c0:c0 + cw] * _at(e, -1, 8)).astype(BF16)
            t = lax.dot_general(du_ref[:, q * nb:(q + 1) * nb], w_ref[q], contract_last,
                                preferred_element_type=F32)
            dhv = t if dhv is None else dhv + t
        sc = mn_ref[seg, pl.ds(kn + 1, 1), :]
        nwv = nw_ref[...]
        xv = x_ref[...]
        r = lax.rsqrt(jnp.mean(xv * xv, axis=-1, keepdims=True) + EPS)
        xh = xv * r
        dmod_ref[seg, 0] += _colsum8(dhv)
        dmod_ref[seg, 1] += _colsum8(dhv * (xh * nwv))
        dn = dhv * (1.0 + sc)
        dnw_ref[...] += _colsum8(dn * xh)
        dxh = dn * nwv
        dx = dxr_ref[...] + r * (dxh - xh * jnp.mean(dxh * xh, axis=-1, keepdims=True))
        dx_ref[...] = dx
        dy_ref[...] = (mg_ref[seg, pl.ds(kg, 1), :] * dx).astype(BF16)
        dg_ref[seg] += _colsum8(dx * y_ref[...])
    tile = pl.BlockSpec((TMR, dm), lambda i: (i, 0))
    modspec = pl.BlockSpec((2, 6, dm), lambda i: (0, 0, 0))
    return pl.pallas_call(
        body, name="ffn_up_bwd", grid=(R // TMR,),
        in_specs=[*_halo_specs(w2, R), pl.BlockSpec((3, w2), lambda i: (0, 0)),
                  pl.BlockSpec((N_CHIP, kb, nb), lambda i: (0, 0, 0), pipeline_mode=pl.Buffered(1)),
                  tile, pl.BlockSpec((1, dm), lambda i: (0, 0)), modspec, tile, tile, modspec],
        out_specs=[pl.BlockSpec((TMR, w2), lambda i: (i, 0)), tile, tile,
                   pl.BlockSpec((2, 2, 8, dm), lambda i: (0, 0, 0, 0)), pl.BlockSpec((8, dm), lambda i: (0, 0)),
                   pl.BlockSpec((2, 8, dm), lambda i: (0, 0, 0))],
        out_shape=[jax.ShapeDtypeStruct((R, w2), BF16), jax.ShapeDtypeStruct((R, dm), F32),
                   jax.ShapeDtypeStruct((R, dm), BF16), jax.ShapeDtypeStruct((2, 2, 8, dm), F32),
                   jax.ShapeDtypeStruct((8, dm), F32), jax.ShapeDtypeStruct((2, 8, dm), F32)],
        compiler_params=_params(48))(duc, duc, duc, wc, w_up, x, nw, mod_n, dxr, y, mod_g)


_CW = 128


def _mixer_a(prefs, wa_ref, live):
    cin = _ext(prefs, A_W, A_W, live) * _ext(prefs, 2 * A_W, A_W, live)
    ca = (wa_ref[pl.ds(0, 1), :] * _at(cin, -1) + wa_ref[pl.ds(1, 1), :] * _at(cin, 0)
          + wa_ref[pl.ds(2, 1), :] * _at(cin, 1))
    return cin, ca


def _mixer_b(prefs, wb_ref, bias_ref, live, ub_s, ub2_s):
    for cc in range(A_W // _CW):
        c0 = cc * _CW
        ub = _ext(prefs, 3 * A_W + c0, _CW, live) * _sigmoid(_ext(prefs, 4 * A_W + c0, _CW, live))
        ub_s[:, c0:c0 + _CW] = ub
        acc = jnp.zeros((TMR, _CW), F32) + bias_ref[:, c0:c0 + _CW]
        for k in range(B_CONV):
            acc = acc + wb_ref[pl.ds(k, 1), c0:c0 + _CW] * _at(ub, k - B_CONV // 2)
        ub2_s[:, c0:c0 + _CW] = acc


def _layernorm_stats(v):
    mu = jnp.mean(v, axis=-1, keepdims=True)
    xc = v - mu
    rs = lax.rsqrt(jnp.mean(xc * xc, axis=-1, keepdims=True) + EPS)
    return xc * rs, rs


def mixer_fwd(p, wa, wb, bias, lng, lnb, w_out, x, nw, mod, kg, kn, T):
    R, wp = p.shape
    dm = x.shape[1]
    _, kb, nb = w_out.shape

    def body(pp_ref, p_ref, pn_ref, wa_ref, wb_ref, bias_ref, lng_ref, lnb_ref, w_ref, x_ref, nw_ref, mod_ref,
             o_ref, y_ref, xo_ref, h_ref, ub_s, ub2_s):
        i = pl.program_id(0)
        seg = _seg(i, T)
        live = _halo_live(i, T, R)
        prefs = (pp_ref, p_ref, pn_ref)
        _, ca = _mixer_a(prefs, wa_ref, live)
        ya = (p_ref[:, 0:A_W].astype(F32) * ca).astype(BF16)
        o_ref[:, 0:A_W] = ya
        yv = jnp.dot(ya, w_ref[0:A_W, :], preferred_element_type=F32)
        _mixer_b(prefs, wb_ref, bias_ref, live, ub_s, ub2_s)
        xh, _ = _layernorm_stats(ub2_s[...])
        lv = xh * lng_ref[...] + lnb_ref[...]
        yb = (lv * _sigmoid(lv)).astype(BF16)
        o_ref[:, A_W:2 * A_W] = yb
        yv = yv + jnp.dot(yb, w_ref[A_W:2 * A_W, :], preferred_element_type=F32)
        y_ref[...] = yv
        xv = x_ref[...] + mod_ref[seg, pl.ds(kg, 1), :] * yv
        xo_ref[...] = xv
        r = lax.rsqrt(jnp.mean(xv * xv, axis=-1, keepdims=True) + EPS)
        h_ref[...] = ((xv * r * nw_ref[...]) * (1.0 + mod_ref[seg, pl.ds(kn + 1, 1), :])
                      + mod_ref[seg, pl.ds(kn, 1), :]).astype(BF16)
    vec = pl.BlockSpec((1, A_W), lambda i: (0, 0))
    tile = pl.BlockSpec((TMR, dm), lambda i: (i, 0))
    return pl.pallas_call(
        body, name="mixer_fwd", grid=(R // TMR,),
        in_specs=[*_halo_specs(wp, R), pl.BlockSpec((3, A_W), lambda i: (0, 0)),
                  pl.BlockSpec((B_CONV, A_W), lambda i: (0, 0)), vec, vec, vec,
                  pl.BlockSpec((N_CHIP * kb, nb), lambda i: (0, 0), pipeline_mode=pl.Buffered(1)),
                  tile, pl.BlockSpec((1, dm), lambda i: (0, 0)), pl.BlockSpec((2, 6, dm), lambda i: (0, 0, 0))],
        out_specs=[pl.BlockSpec((TMR, 2 * A_W), lambda i: (i, 0)), tile, tile, tile],
        out_shape=[jax.ShapeDtypeStruct((R, 2 * A_W), BF16), jax.ShapeDtypeStruct((R, dm), F32),
                   jax.ShapeDtypeStruct((R, dm), F32), jax.ShapeDtypeStruct((R, dm), BF16)],
        scratch_shapes=[pltpu.VMEM((TMR + 2 * HALO, A_W), F32), pltpu.VMEM((TMR, A_W), F32)],
        compiler_params=_params(48))(p, p, p, wa, wb, bias, lng, lnb, w_out.reshape(N_CHIP * kb, nb), x, nw, mod)


def convmix_bwd1(dyab, p, wa, wb, bias, lng, lnb, T):
    R, wp = p.shape

    def body(dy_ref, pp_ref, p_ref, pn_ref, wa_ref, wb_ref, bias_ref, lng_ref, lnb_ref,
             dmid_ref, dwa_ref, dwb_ref, dvec_ref, ub_s, ub2_s):
        i = pl.program_id(0)
        live = _halo_live(i, T, R)

        @pl.when(i == 0)
        def _():
            dwa_ref[...] = jnp.zeros_like(dwa_ref)
            dwb_ref[...] = jnp.zeros_like(dwb_ref)
            dvec_ref[...] = jnp.zeros_like(dvec_ref)
        prefs = (pp_ref, p_ref, pn_ref)
        cin, ca = _mixer_a(prefs, wa_ref, live)
        dya = dy_ref[:, 0:A_W]
        dmid_ref[:, 0:A_W] = dya * ca
        dca = dya * p_ref[:, 0:A_W].astype(F32)
        dmid_ref[:, A_W:2 * A_W] = dca
        for k in range(3):
            dwa_ref[k] += _colsum8(dca * _at(cin, k - 1))
        _mixer_b(prefs, wb_ref, bias_ref, live, ub_s, ub2_s)
        xh, rs = _layernorm_stats(ub2_s[...])
        gain = lng_ref[...]
        lv = xh * gain + lnb_ref[...]
        sl = _sigmoid(lv)
        dl = dy_ref[:, A_W:2 * A_W] * (sl * (1.0 + lv * (1.0 - sl)))
        dvec_ref[1] += _colsum8(dl * xh)
        dvec_ref[2] += _colsum8(dl)
        dxh = dl * gain
        dub2 = rs * (dxh - jnp.mean(dxh, axis=-1, keepdims=True)
                     - xh * jnp.mean(dxh * xh, axis=-1, keepdims=True))
        dvec_ref[0] += _colsum8(dub2)
        dmid_ref[:, 2 * A_W:3 * A_W] = dub2
        for cc in range(A_W // _CW):
            c0 = cc * _CW
            ub = ub_s[:, c0:c0 + _CW]
            d = dmid_ref[:, 2 * A_W + c0:2 * A_W + c0 + _CW]
            for k in range(B_CONV):
                dwb_ref[k, :, c0:c0 + _CW] += _colsum8(d * _at(ub, k - B_CONV // 2))
    vec = pl.BlockSpec((1, A_W), lambda i: (0, 0))
    return pl.pallas_call(
        body, name="convmix_bwd1", grid=(R // TMR,),
        in_specs=[pl.BlockSpec((TMR, 2 * A_W), lambda i: (i, 0)), *_halo_specs(wp, R),
                  pl.BlockSpec((3, A_W), lambda i: (0, 0)), pl.BlockSpec((B_CONV, A_W), lambda i: (0, 0)),
                  vec, vec, vec],
        out_specs=[pl.BlockSpec((TMR, 3 * A_W), lambda i: (i, 0)),
                   pl.BlockSpec((3, 8, A_W), lambda i: (0, 0, 0)),
                   pl.BlockSpec((B_CONV, 8, A_W), lambda i: (0, 0, 0)),
                   pl.BlockSpec((3, 8, A_W), lambda i: (0, 0, 0))],
        out_shape=[jax.ShapeDtypeStruct((R, 3 * A_W), F32), jax.ShapeDtypeStruct((3, 8, A_W), F32),
                   jax.ShapeDtypeStruct((B_CONV, 8, A_W), F32), jax.ShapeDtypeStruct((3, 8, A_W), F32)],
        scratch_shapes=[pltpu.VMEM((TMR + 2 * HALO, A_W), F32), pltpu.VMEM((TMR, A_W), F32)],
        compiler_params=_params(48))(dyab, p, p, p, wa, wb, bias, lng, lnb)


def mixer_in_bwd(dmid, p, wa, wb, w_in, x, nw, mod_n, dxr, y, mod_g, kn, kg, T):
    R, wp = p.shape
    dm = x.shape[1]
    _, kb, nb = w_in.shape
    with_resid = y is not None
    contract_last = (((1,), (1,)), ((), ()))

    def body(*refs):
        if with_resid:
            (mp_ref, m_ref, mn_ref, p_ref, wa_ref, wb_ref, w_ref, x_ref, nw_ref, mnorm_ref, dxr_ref, y_ref, mg_ref,
             dp_ref, dx_ref, dy_ref, dmod_ref, dnw_ref, dg_ref) = refs
        else:
            (mp_ref, m_ref, mn_ref, p_ref, wa_ref, wb_ref, w_ref, x_ref, nw_ref, mnorm_ref, dxr_ref,
             dp_ref, dx_ref, dmod_ref, dnw_ref) = refs
        i = pl.program_id(0)
        seg = _seg(i, T)
        live = _halo_live(i, T, R)

        @pl.when(i == 0)
        def _():
            dmod_ref[...] = jnp.zeros_like(dmod_ref)
            dnw_ref[...] = jnp.zeros_like(dnw_ref)
            if with_resid:
                dg_ref[...] = jnp.zeros_like(dg_ref)

        def block(q):
            return lax.dot_general(dp_ref[:, q * nb:(q + 1) * nb], w_ref[q], contract_last,
                                   preferred_element_type=F32)
        mrefs = (mp_ref, m_ref, mn_ref)
        dp_ref[:, 0:A_W] = m_ref[:, 0:A_W].astype(BF16)
        dca = _ext(mrefs, A_W, A_W, live)
        dcin = (wa_ref[pl.ds(0, 1), :] * _at(dca, 1) + wa_ref[pl.ds(1, 1), :] * _at(dca, 0)
                + wa_ref[pl.ds(2, 1), :] * _at(dca, -1))
        dp_ref[:, A_W:2 * A_W] = (dcin * p_ref[:, 2 * A_W:3 * A_W].astype(F32)).astype(BF16)
        dp_ref[:, 2 * A_W:3 * A_W] = (dcin * p_ref[:, A_W:2 * A_W].astype(F32)).astype(BF16)
        dhv = block(0) + block(1)
        for cc in range(A_W // _CW):
            c0 = cc * _CW
            d = _ext(mrefs, 2 * A_W + c0, _CW, live)
            dub = jnp.zeros((TMR, _CW), F32)
            for k in range(B_CONV):
                dub = dub + wb_ref[pl.ds(k, 1), c0:c0 + _CW] * _at(d, B_CONV // 2 - k)
            vb = p_ref[:, 3 * A_W + c0:3 * A_W + c0 + _CW].astype(F32)
            s = _sigmoid(p_ref[:, 4 * A_W + c0:4 * A_W + c0 + _CW].astype(F32))
            dp_ref[:, 3 * A_W + c0:3 * A_W + c0 + _CW] = (dub * s).astype(BF16)
            dp_ref[:, 4 * A_W + c0:4 * A_W + c0 + _CW] = (dub * vb * s * (1.0 - s)).astype(BF16)
        dhv = dhv + block(2) + block(3)
        sc = mnorm_ref[seg, pl.ds(kn + 1, 1), :]
        nwv = nw_ref[...]
        xv = x_ref[...]
        r = lax.rsqrt(jnp.mean(xv * xv, axis=-1, keepdims=True) + EPS)
        xh = xv * r
        dmod_ref[seg, 0] += _colsum8(dhv)
        dmod_ref[seg, 1] += _colsum8(dhv * (xh * nwv))
        dn = dhv * (1.0 + sc)
        dnw_ref[...] += _colsum8(dn * xh)
        dxh = dn * nwv
        dx = dxr_ref[...] + r * (dxh - xh * jnp.mean(dxh * xh, axis=-1, keepdims=True))
        dx_ref[...] = dx
        if with_resid:
            dy_ref[...] = (mg_ref[seg, pl.ds(kg, 1), :] * dx).astype(BF16)
            dg_ref[seg] += _colsum8(dx * y_ref[...])
    assert 2 * nb <= 3 * A_W and N_CHIP * nb == wp
    tile = pl.BlockSpec((TMR, dm), lambda i: (i, 0))
    modspec = pl.BlockSpec((2, 6, dm), lambda i: (0, 0, 0))
    in_specs = [*_halo_specs(3 * A_W, R), pl.BlockSpec((TMR, wp), lambda i: (i, 0)),
                pl.BlockSpec((3, A_W), lambda i: (0, 0)), pl.BlockSpec((B_CONV, A_W), lambda i: (0, 0)),
                pl.BlockSpec((N_CHIP, kb, nb), lambda i: (0, 0, 0), pipeline_mode=pl.Buffered(1)),
                tile, pl.BlockSpec((1, dm), lambda i: (0, 0)), modspec, tile]
    dp_spec = pl.BlockSpec((TMR, wp), lambda i: (i, 0))
    acc_specs = [pl.BlockSpec((2, 2, 8, dm), lambda i: (0, 0, 0, 0)), pl.BlockSpec((8, dm), lambda i: (0, 0))]
    acc_shapes = [jax.ShapeDtypeStruct((2, 2, 8, dm), F32), jax.ShapeDtypeStruct((8, dm), F32)]
    dp_shape, dx_shape = jax.ShapeDtypeStruct((R, wp), BF16), jax.ShapeDtypeStruct((R, dm), F32)
    if with_resid:
        return pl.pallas_call(
            body, name="mixer_in_bwd", grid=(R // TMR,), in_specs=in_specs + [tile, modspec],
            out_specs=[dp_spec, tile, tile] + acc_specs + [pl.BlockSpec((2, 8, dm), lambda i: (0, 0, 0))],
            out_shape=[dp_shape, dx_shape, jax.ShapeDtypeStruct((R, dm), BF16)] + acc_shapes
            + [jax.ShapeDtypeStruct((2, 8, dm), F32)],
            compiler_params=_params(48))(dmid, dmid, dmid, p, wa, wb, w_in, x, nw, mod_n, dxr, y, mod_g)
    return pl.pallas_call(
        body, name="mixer_in_bwd_first", grid=(R // TMR,), in_specs=in_specs,
        out_specs=[dp_spec, tile] + acc_specs, out_shape=[dp_shape, dx_shape] + acc_shapes,
        compiler_params=_params(48))(dmid, dmid, dmid, p, wa, wb, w_in, x, nw, mod_n, dxr)


def _rot_half(v):
    w = v.shape[-1]
    lane = lax.broadcasted_iota(jnp.int32, (1, w), 1)
    return jnp.where(lane % HEAD_DIM < HEAD_DIM // 2, pltpu.roll(v, w - HEAD_DIM // 2, 1),
                     pltpu.roll(v, HEAD_DIM // 2, 1))


def mm_qkv_rope(a, w, cs, sn):
    R = a.shape[0]
    _, kb, nb = w.shape
    wq = N_CHIP * nb
    tm = _row_tile(R)
    qw = N_HEADS * HEAD_DIM
    kw = (wq - qw) // 2
    scale = HEAD_DIM ** -0.5

    def body(a_ref, w_ref, cs_ref, sn_ref, o_ref, x_ref):
        av = a_ref[...].astype(BF16)
        for q in range(N_CHIP):
            x_ref[:, q * nb:(q + 1) * nb] = jnp.dot(av, w_ref[q], preferred_element_type=F32)
        c, s = cs_ref[...], sn_ref[...]
        q = x_ref[:, 0:qw]
        o_ref[:, 0:qw] = ((q * jnp.tile(c, (1, qw // 128)) + _rot_half(q) * jnp.tile(s, (1, qw // 128)))
                          * scale).astype(BF16)
        k = x_ref[:, qw:qw + kw]
        o_ref[:, qw:qw + kw] = (k * jnp.tile(c, (1, kw // 128))
                                + _rot_half(k) * jnp.tile(s, (1, kw // 128))).astype(BF16)
        o_ref[:, qw + kw:] = x_ref[:, qw + kw:].astype(BF16)
    tab = pl.BlockSpec((tm, 128), lambda i: (i, 0))
    return pl.pallas_call(
        body, name="mm_qkv_rope", grid=(R // tm,),
        in_specs=[pl.BlockSpec((tm, kb), lambda i: (i, 0)),
                  pl.BlockSpec((N_CHIP, kb, nb), lambda i: (0, 0, 0), pipeline_mode=pl.Buffered(1)), tab, tab],
        out_specs=pl.BlockSpec((tm, wq), lambda i: (i, 0)),
        out_shape=jax.ShapeDtypeStruct((R, wq), BF16), scratch_shapes=[pltpu.VMEM((tm, wq), F32)],
        compiler_params=_params(48))(a, w, cs, sn)


def attn_in_bwd(dq, dks, dvs, dkc, dvc, cs, sn, w, x, nw, mod_n, dxr, y, mod_g, kn, kg, T):
    R, qw = dq.shape
    kw = dkc.shape[1]
    dm = x.shape[1]
    _, kb, nbw = w.shape
    nb = R // QB
    nl = T // QB
    scale = HEAD_DIM ** -0.5
    contract_last = (((1,), (1,)), ((), ()))

    def body(dq_ref, kp_ref, ko_ref, kn_ref, vp_ref, vo_ref, vn_ref, kc_ref, vc_ref, cs_ref, sn_ref,
             w_ref, x_ref, nw_ref, mnorm_ref, dxr_ref, y_ref, mg_ref,
             o_ref, dx_ref, dy_ref, dmod_ref, dnw_ref, dg_ref):
        b = pl.program_id(0)
        seg = (b >= nl).astype(jnp.int32)

        @pl.when(b == 0)
        def _():
            dmod_ref[...] = jnp.zeros_like(dmod_ref)
            dnw_ref[...] = jnp.zeros_like(dnw_ref)
            dg_ref[...] = jnp.zeros_like(dg_ref)
        c, s = cs_ref[...], sn_ref[...]
        has_next = (b + 1 < nb).astype(F32)
        has_prev = (b >= 1).astype(F32)
        is_ctx = (b >= nl).astype(F32)
        g = dq_ref[...] * scale
        o_ref[:, 0:qw] = (g * jnp.tile(c, (1, qw // 128)) + _rot_half(g * jnp.tile(s, (1, qw // 128)))).astype(BF16)
        g = ko_ref[...] + kp_ref[...] * has_next + kn_ref[...] * has_prev + kc_ref[...] * is_ctx
        o_ref[:, qw:qw + kw] = (g * jnp.tile(c, (1, kw // 128))
                                + _rot_half(g * jnp.tile(s, (1, kw // 128)))).astype(BF16)
        o_ref[:, qw + kw:] = (vo_ref[...] + vp_ref[...] * has_next + vn_ref[...] * has_prev
                              + vc_ref[...] * is_ctx).astype(BF16)
        dhv = None
        for q in range(N_CHIP):
            t = lax.dot_general(o_ref[:, q * nbw:(q + 1) * nbw], w_ref[q], contract_last,
                                preferred_element_type=F32)
            dhv = t if dhv is None else dhv + t
        sc = mnorm_ref[seg, pl.ds(kn + 1, 1), :]
        nwv = nw_ref[...]
        xv = x_ref[...]
        r = lax.rsqrt(jnp.mean(xv * xv, axis=-1, keepdims=True) + EPS)
        xh = xv * r
        dmod_ref[seg, 0] += _colsum8(dhv)
        dmod_ref[seg, 1] += _colsum8(dhv * (xh * nwv))
        dn = dhv * (1.0 + sc)
        dnw_ref[...] += _colsum8(dn * xh)
        dxh = dn * nwv
        dx = dxr_ref[...] + r * (dxh - xh * jnp.mean(dxh * xh, axis=-1, keepdims=True))
        dx_ref[...] = dx
        dy_ref[...] = (mg_ref[seg, pl.ds(kg, 1), :] * dx).astype(BF16)
        dg_ref[seg] += _colsum8(dx * y_ref[...])
    own = pl.BlockSpec((QB, kw), lambda b: (b, 0))
    from_next = pl.BlockSpec((QB, kw), lambda b: (jnp.minimum(b + 1, nb - 1), 0))
    from_prev = pl.BlockSpec((QB, kw), lambda b: (jnp.maximum(b - 1, 0), 0))
    ctx = pl.BlockSpec((QB, kw), lambda b: (jnp.maximum(b - nl, 0), 0))
    tab = pl.BlockSpec((QB, 128), lambda b: (b, 0))
    tile = pl.BlockSpec((QB, dm), lambda b: (b, 0))
    modspec = pl.BlockSpec((2, 6, dm), lambda b: (0, 0, 0))
    return pl.pallas_call(
        body, name="attn_in_bwd", grid=(nb,),
        in_specs=[pl.BlockSpec((QB, qw), lambda b: (b, 0)), from_next, own, from_prev, from_next, own, from_prev,
                  ctx, ctx, tab, tab,
                  pl.BlockSpec((N_CHIP, kb, nbw), lambda b: (0, 0, 0), pipeline_mode=pl.Buffered(1)),
                  tile, pl.BlockSpec((1, dm), lambda b: (0, 0)), modspec, tile, tile, modspec],
        out_specs=[pl.BlockSpec((QB, qw + 2 * kw), lambda b: (b, 0)), tile, tile,
                   pl.BlockSpec((2, 2, 8, dm), lambda b: (0, 0, 0, 0)), pl.BlockSpec((8, dm), lambda b: (0, 0)),
                   pl.BlockSpec((2, 8, dm), lambda b: (0, 0, 0))],
        out_shape=[jax.ShapeDtypeStruct((R, qw + 2 * kw), BF16), jax.ShapeDtypeStruct((R, dm), F32),
                   jax.ShapeDtypeStruct((R, dm), BF16), jax.ShapeDtypeStruct((2, 2, 8, dm), F32),
                   jax.ShapeDtypeStruct((8, dm), F32), jax.ShapeDtypeStruct((2, 8, dm), F32)],
        compiler_params=_params(48))(
            dq, dks[0], dks[1], dks[2], dvs[0], dvs[1], dvs[2], dkc, dvc, cs, sn, w, x, nw, mod_n, dxr, y, mod_g)


def _attn_specs(T, R):
    nl = T // QB
    qcols = N_HEADS * HEAD_DIM // 128
    kcols = 2

    def band(col0, shift):
        return pl.BlockSpec((QB, 128), lambda jj, b: (jnp.clip(b + shift, 0, nl - 1), col0 + jj))

    def ctx(col0):
        return pl.BlockSpec((R - T, 128), lambda jj, b: (T // (R - T), col0 + jj))
    q = pl.BlockSpec((QB, 512), lambda jj, b: (b, jj))
    k0, v0 = qcols, qcols + kcols
    return q, [band(k0, -1), band(k0, 0), band(k0, 1), ctx(k0)], [band(v0, -1), band(v0, 0), band(v0, 1), ctx(v0)]


def _attn_common(T, R):
    nl = T // QB
    nk = 3 * QB + (R - T)

    def low_lanes():
        return lax.broadcasted_iota(jnp.int32, (1, 128), 1) < HEAD_DIM

    def dup(v, par):
        low = low_lanes()
        vf = v.astype(F32)
        r = pltpu.roll(vf, HEAD_DIM, 1)
        return (jnp.where(low, vf, r) if par == 0 else jnp.where(low, r, vf)).astype(BF16)

    def stack(ref, par):
        low = low_lanes()
        pa = ref[:, (2 * par) * 128:(2 * par + 1) * 128].astype(BF16)
        pb = ref[:, (2 * par + 1) * 128:(2 * par + 2) * 128].astype(BF16)
        zero = jnp.zeros_like(pa)
        return jnp.concatenate([jnp.where(low, pa, zero), jnp.where(low, zero, pa),
                                jnp.where(low, pb, zero), jnp.where(low, zero, pb)], axis=0)

    def unstack(v):
        low = low_lanes()
        return (jnp.where(low, v[0:QB], v[QB:2 * QB]), jnp.where(low, v[2 * QB:3 * QB], v[3 * QB:4 * QB]))

    def mask_of(b):
        col = lax.broadcasted_iota(jnp.int32, (1, nk), 1)
        gone = (((col < QB) & (b == 0)) | ((col >= 2 * QB) & (col < 3 * QB) & (b == nl - 1))
                | ((col < 3 * QB) & (b >= nl)))
        return jnp.where(gone, NEG_INF, 0.0)

    def sink_col(sink_ref, first):
        blk = lax.broadcasted_iota(jnp.int32, (4 * QB, 1), 0) // QB
        out = jnp.zeros((4 * QB, 1), F32) + sink_ref[first]
        for h in range(1, 4):
            out = jnp.where(blk == h, sink_ref[first + h], out)
        return out

    def scores(qs, kd, mask, sink):
        s = lax.dot_general(qs, kd, (((1,), (1,)), ((), ())), preferred_element_type=F32) + mask
        m = jnp.maximum(jnp.max(s, axis=-1, keepdims=True), sink)
        e = jnp.exp(s - m)
        es = jnp.exp(sink - m)
        return e, es, 1.0 / (jnp.sum(e, axis=-1, keepdims=True) + es)
    return low_lanes, dup, stack, unstack, mask_of, sink_col, scores


def window_bias(T, R):
    nk = 3 * QB + (R - T)
    row = jnp.arange(QB)[:, None]
    col = jnp.arange(nk)[None, :]
    near = (jnp.abs(col - QB - row) <= WINDOW) | (col >= 3 * QB)
    return jnp.tile(jnp.where(near, 0.0, NEG_INF).astype(F32), (4, 1))


def _probs_spec(nk):
    return pl.BlockSpec((None, None, 2, 4 * QB, nk + 128), lambda jj, b: (jj, b, 0, 0, 0))


def attn_fwd(qkvr, sinks, bias, T):
    R = qkvr.shape[0]
    nk = bias.shape[1]
    qspec, kspecs, vspecs = _attn_specs(T, R)
    _, dup, stack, unstack, mask_of, sink_col, scores = _attn_common(T, R)

    def body(q_ref, kp, ko, kn, kc, vp, vo, vn, vc, sink_ref, bias_ref, o_ref, p_ref):
        jj, b = pl.program_id(0), pl.program_id(1)
        mask = bias_ref[...] + mask_of(b)
        k_all = jnp.concatenate([kp[...], ko[...], kn[...], kc[...]], axis=0)
        v_all = jnp.concatenate([vp[...], vo[...], vn[...], vc[...]], axis=0)
        for par in range(2):
            kd, vd = dup(k_all, par), dup(v_all, par)
            e, es, rz = scores(stack(q_ref, par), kd, mask, sink_col(sink_ref, jj * 8 + par * 4))
            p = (e * rz).astype(BF16)
            p_ref[par, :, 0:nk] = p
            p_ref[par, :, nk:nk + 128] = jnp.broadcast_to(es * rz, (4 * QB, 128)).astype(BF16)
            o = jnp.dot(p, vd, preferred_element_type=F32)
            pa, pb = unstack(o)
            o_ref[:, (2 * par) * 128:(2 * par + 1) * 128] = pa.astype(BF16)
            o_ref[:, (2 * par + 1) * 128:(2 * par + 2) * 128] = pb.astype(BF16)
    return pl.pallas_call(
        body, name="attn_fwd", grid=(2, R // QB),
        in_specs=[qspec, *kspecs, *vspecs, pl.BlockSpec(memory_space=pltpu.SMEM),
                  pl.BlockSpec(bias.shape, lambda jj, b: (0, 0))],
        out_specs=[pl.BlockSpec((QB, 512), lambda jj, b: (b, jj)), _probs_spec(nk)],
        out_shape=[jax.ShapeDtypeStruct((R, N_HEADS * HEAD_DIM), BF16),
                   jax.ShapeDtypeStruct((2, R // QB, 2, 4 * QB, nk + 128), BF16)],
        compiler_params=_params(48))(qkvr, *([qkvr] * 8), sinks, bias)


def attn_bwd(qkvr, do, probs, T):
    R = qkvr.shape[0]
    tc = R - T
    nk = probs.shape[-1] - 128
    qspec, kspecs, vspecs = _attn_specs(T, R)
    _, dup, stack, unstack, _, _, _ = _attn_common(T, R)
    contract_rows = (((0,), (0,)), ((), ()))
    contract_last = (((1,), (1,)), ((), ()))

    def body(q_ref, kp, ko, kn, kc, vp, vo, vn, vc, do_ref, p_ref,
             dq_ref, dkp, dko, dkn, dvp, dvo, dvn, dkc_ref, dvc_ref, dsink_ref):
        jj, b = pl.program_id(0), pl.program_id(1)

        @pl.when((jj == 0) & (b == 0))
        def _():
            dsink_ref[...] = jnp.zeros_like(dsink_ref)

        @pl.when(b == 0)
        def _():
            dkc_ref[...] = jnp.zeros_like(dkc_ref)
            dvc_ref[...] = jnp.zeros_like(dvc_ref)
        k_all = jnp.concatenate([kp[...], ko[...], kn[...], kc[...]], axis=0)
        v_all = jnp.concatenate([vp[...], vo[...], vn[...], vc[...]], axis=0)
        lane = lax.broadcasted_iota(jnp.int32, (8, 128), 1)
        srow = lax.broadcasted_iota(jnp.int32, (8, 128), 0)
        dk_fold, dv_fold = [], []
        for par in range(2):
            kd, vd = dup(k_all, par), dup(v_all, par)
            first = jj * 8 + par * 4
            qs, dos = stack(q_ref, par), stack(do_ref, par)
            p16 = p_ref[par, :, 0:nk]
            p = p16.astype(F32)
            ps = jnp.max(p_ref[par, :, nk:nk + 128].astype(F32), axis=-1, keepdims=True)
            dp = lax.dot_general(dos, vd, contract_last, preferred_element_type=F32)
            delta = jnp.sum(p * dp, axis=-1, keepdims=True)
            ds = (p * (dp - delta)).astype(BF16)
            t = ps * delta
            for h in range(4):
                dsink = -jnp.sum(t[h * QB:(h + 1) * QB])
                dsink_ref[...] += jnp.where((lane == first + h) & (srow == 0), dsink, 0.0)
            pa, pb = unstack(jnp.dot(ds, kd, preferred_element_type=F32))
            dq_ref[:, (2 * par) * 128:(2 * par + 1) * 128] = pa
            dq_ref[:, (2 * par + 1) * 128:(2 * par + 2) * 128] = pb
            dk_t = lax.dot_general(qs, ds, contract_rows, preferred_element_type=F32)
            dv_t = lax.dot_general(dos, p16, contract_rows, preferred_element_type=F32)
            dk_fold.append(dk_t + pltpu.roll(dk_t, HEAD_DIM, 0))
            dv_fold.append(dv_t + pltpu.roll(dv_t, HEAD_DIM, 0))
        low_rows = lax.broadcasted_iota(jnp.int32, (128, 1), 0) < HEAD_DIM
        dk = jnp.where(low_rows, dk_fold[0], dk_fold[1]).T
        dv = jnp.where(low_rows, dv_fold[0], dv_fold[1]).T
        dkp[...], dko[...], dkn[...] = dk[0:QB], dk[QB:2 * QB], dk[2 * QB:3 * QB]
        dvp[...], dvo[...], dvn[...] = dv[0:QB], dv[QB:2 * QB], dv[2 * QB:3 * QB]
        dkc_ref[...] += dk[3 * QB:]
        dvc_ref[...] += dv[3 * QB:]
    blk = pl.BlockSpec((QB, 128), lambda jj, b: (b, jj))
    cblk = pl.BlockSpec((tc, 128), lambda jj, b: (0, jj))
    part = jax.ShapeDtypeStruct((R, 256), F32)
    csum = jax.ShapeDtypeStruct((tc, 256), F32)
    outs = pl.pallas_call(
        body, name="attn_bwd", grid=(2, R // QB),
        in_specs=[qspec, *kspecs, *vspecs, pl.BlockSpec((QB, 512), lambda jj, b: (b, jj)), _probs_spec(nk)],
        out_specs=[pl.BlockSpec((QB, 512), lambda jj, b: (b, jj)), blk, blk, blk, blk, blk, blk, cblk, cblk,
                   pl.BlockSpec((8, 128), lambda jj, b: (0, 0))],
        out_shape=[jax.ShapeDtypeStruct((R, N_HEADS * HEAD_DIM), F32), part, part, part, part, part, part,
                   csum, csum, jax.ShapeDtypeStruct((8, 128), F32)],
        compiler_params=_params(48))(qkvr, *([qkvr] * 8), do, probs)
    return outs[0], outs[1:4], outs[4:7], outs[7], outs[8], outs[9]


def loss_head(x, nw, target, y, mod, kg, T):
    R, dm = x.shape
    nl = T // TMR

    def body(x_ref, nw_ref, t_ref, y_ref, mod_ref, loss_ref, dx_ref, dnw_ref, dy_ref, dg_ref):
        i = pl.program_id(0)
        seg = _seg(i, T)

        @pl.when(i == 0)
        def _():
            loss_ref[...] = jnp.zeros_like(loss_ref)
            dnw_ref[...] = jnp.zeros_like(dnw_ref)
            dg_ref[...] = jnp.zeros_like(dg_ref)
        live = (i < nl).astype(F32)
        nwv = nw_ref[...]
        xv = x_ref[...]
        r = lax.rsqrt(jnp.mean(xv * xv, axis=-1, keepdims=True) + EPS)
        xh = xv * r
        err = xh * nwv - t_ref[...]
        per_row = jnp.mean(err * err, axis=-1, keepdims=True)
        loss_ref[...] += 0.5 * live * jnp.sum(per_row)
        dy = err * (live / dm)
        dnw_ref[...] += _colsum8(dy * xh)
        dxh = dy * nwv
        dx = r * (dxh - xh * jnp.mean(dxh * xh, axis=-1, keepdims=True))
        dx_ref[...] = dx
        dy_ref[...] = (mod_ref[seg, pl.ds(kg, 1), :] * dx).astype(BF16)
        dg_ref[seg] += _colsum8(dx * y_ref[...])
    tile = pl.BlockSpec((TMR, dm), lambda i: (i, 0))
    return pl.pallas_call(
        body, name="loss_head", grid=(R // TMR,),
        in_specs=[tile, pl.BlockSpec((1, dm), lambda i: (0, 0)),
                  pl.BlockSpec((TMR, dm), lambda i: (jnp.minimum(i, nl - 1), 0)), tile,
                  pl.BlockSpec((2, 6, dm), lambda i: (0, 0, 0))],
        out_specs=[pl.BlockSpec((8, 128), lambda i: (0, 0)), tile, pl.BlockSpec((8, dm), lambda i: (0, 0)), tile,
                   pl.BlockSpec((2, 8, dm), lambda i: (0, 0, 0))],
        out_shape=[jax.ShapeDtypeStruct((8, 128), F32), jax.ShapeDtypeStruct((R, dm), F32),
                   jax.ShapeDtypeStruct((8, dm), F32), jax.ShapeDtypeStruct((R, dm), BF16),
                   jax.ShapeDtypeStruct((2, 8, dm), F32)])(x, nw, target, y, mod)


def adaln_fwd(cond, w_mod, b_mod):
    nl, dm, ns = w_mod.shape

    def body(c_ref, w_ref, b_ref, o_ref):
        cv = c_ref[...]
        s = (cv * _sigmoid(cv)).astype(BF16)
        o_ref[...] = jnp.dot(s, w_ref[...].astype(BF16), preferred_element_type=F32) + b_ref[...]
    return pl.pallas_call(
        body, name="adaln_fwd", grid=(nl,),
        in_specs=[pl.BlockSpec((16, dm), lambda l: (0, 0)), pl.BlockSpec((None, dm, ns), lambda l: (l, 0, 0)),
                  pl.BlockSpec((None, 1, ns), lambda l: (l, 0, 0))],
        out_specs=pl.BlockSpec((None, 16, ns), lambda l: (l, 0, 0)),
        out_shape=jax.ShapeDtypeStruct((nl, 16, ns), F32), compiler_params=_params(48))(cond, w_mod, b_mod)


def adaln_bwd(cond, dmod, w_mod):
    nl, dm, ns = w_mod.shape

    def body(c_ref, d_ref, w_ref, gw_ref, ds_ref):
        l = pl.program_id(0)

        @pl.when(l == 0)
        def _():
            ds_ref[...] = jnp.zeros_like(ds_ref)
        cv = c_ref[...]
        s = (cv * _sigmoid(cv)).astype(BF16)
        dv = d_ref[...].astype(BF16)
        gw_ref[...] = lax.dot_general(s, dv, (((0,), (0,)), ((), ())), preferred_element_type=F32)
        ds_ref[...] += lax.dot_general(dv, w_ref[...].astype(BF16), (((1,), (1,)), ((), ())),
                                       preferred_element_type=F32)
    return pl.pallas_call(
        body, name="adaln_bwd", grid=(nl,),
        in_specs=[pl.BlockSpec((16, dm), lambda l: (0, 0)), pl.BlockSpec((None, 16, ns), lambda l: (l, 0, 0)),
                  pl.BlockSpec((None, dm, ns), lambda l: (l, 0, 0))],
        out_specs=[pl.BlockSpec((None, dm, ns), lambda l: (l, 0, 0)), pl.BlockSpec((16, dm), lambda l: (0, 0))],
        out_shape=[jax.ShapeDtypeStruct((nl, dm, ns), F32), jax.ShapeDtypeStruct((16, dm), F32)],
        compiler_params=_params(48))(cond, dmod, w_mod)


def _me():
    return lax.axis_index("x"), lax.axis_index("y"), lax.axis_index("c")


def allgather8(block):
    m_per, n = block.shape

    def body(x_ref, out_ref, send_sems, recv_sems, local_sem):
        x, y, c = _me()
        me, sibling = (x, y, c), (x, y, 1 - c)
        chips = [(1 - x, y), (x, 1 - y), (1 - x, 1 - y)]

        def rows(px, py, pc):
            return out_ref.at[pl.ds((4 * px + 2 * py + pc) * m_per, m_per), :]

        def copy(k, blk, to, src=None):
            return pltpu.make_async_remote_copy(
                src_ref=rows(*blk) if src is None else src, dst_ref=rows(*blk),
                send_sem=send_sems.at[k], recv_sem=recv_sems.at[k], device_id=to, device_id_type=MESH)
        mine = pltpu.make_async_copy(x_ref, rows(*me), local_sem)
        mine.start()
        first = [copy(0, me, sibling, src=x_ref)]
        first += [copy(1 + j, me, (*chip, c), src=x_ref) for j, chip in enumerate(chips)]
        for cp in first:
            cp.start()
        passed = [copy(4 + j, (*chip, c), sibling) for j, chip in enumerate(chips)]
        for j, chip in enumerate(chips):
            copy(1 + j, (*chip, c), me).wait_recv()
            passed[j].start()
        copy(0, sibling, me).wait_recv()
        for j, chip in enumerate(chips):
            copy(4 + j, (*chip, 1 - c), me).wait_recv()
        for cp in first + passed:
            cp.wait_send()
        mine.wait()
    return pl.pallas_call(
        body, name="allgather8",
        out_shape=jax.ShapeDtypeStruct((N_DEV * m_per, n), block.dtype),
        in_specs=[pl.BlockSpec(memory_space=pltpu.VMEM)],
        out_specs=pl.BlockSpec(memory_space=pltpu.VMEM),
        scratch_shapes=[pltpu.SemaphoreType.DMA((7,)), pltpu.SemaphoreType.DMA((7,)), pltpu.SemaphoreType.DMA],
        compiler_params=_params(48))(block)


def _other_chips(x, y):
    return [(1 - x, y), (x, 1 - y), (1 - x, 1 - y)]


_HBM = pl.BlockSpec(memory_space=pltpu.HBM)
_SEM = pl.BlockSpec(memory_space=pltpu.SEMAPHORE)
_ANY = pl.BlockSpec(memory_space=pl.ANY)
_EFFECT = pltpu.SideEffectType.DATAFLOW_SIDE_EFFECTING


def _in_hbm(v):
    return pltpu.with_memory_space_constraint(v, pltpu.HBM)


def cast_into_slot(w, layer, chip_id):
    _, kb, nb = w.shape
    tr = _row_tile(kb)

    def body(chip_ref, w_ref, o_ref):
        del chip_ref
        o_ref[...] = w_ref[...].astype(BF16)
    return pl.pallas_call(
        body, name="cast_into_slot",
        grid_spec=pltpu.PrefetchScalarGridSpec(
            num_scalar_prefetch=1, grid=(kb // tr,),
            in_specs=[pl.BlockSpec((None, tr, nb), lambda i, chip: (layer, i, 0))],
            out_specs=pl.BlockSpec((None, tr, nb), lambda i, chip: (chip[0], i, 0))),
        out_shape=jax.ShapeDtypeStruct((N_CHIP, kb, nb), BF16))(chip_id, w)


def _split_copies(mode, srcs, lands, send_sems, recv_sems):
    x, y, c = _me()
    out = []
    for t in range(len(lands)):
        for k, chip in enumerate(_other_chips(x, y)):
            if mode == "gather":
                src = dst = lands[t].at[2 * x + y]
                landed = lands[t].at[2 * chip[0] + chip[1]]
            else:
                src, dst, landed = srcs[t].at[2 * chip[0] + chip[1]], lands[t].at[k], lands[t].at[k]
            send = pltpu.make_async_remote_copy(src_ref=src, dst_ref=dst, send_sem=send_sems.at[3 * t + k],
                                                recv_sem=recv_sems.at[3 * t + k], device_id=(*chip, c),
                                                device_id_type=MESH)
            recv = pltpu.make_async_remote_copy(src_ref=src, dst_ref=landed, send_sem=send_sems.at[3 * t + k],
                                                recv_sem=recv_sems.at[3 * t + k], device_id=(*chip, c),
                                                device_id_type=MESH)
            out.append((send, recv))
    return out


def exchange_start(name, mode, srcs, lands, after):
    ns, nl = len(srcs), len(lands)
    na = ns + nl

    def body(*refs):
        src_refs, land_refs = refs[:ns], refs[ns:na]
        send_sems, recv_sems = refs[na + 1], refs[na + 2]
        token = refs[-1]
        for send, _ in _split_copies(mode, src_refs, land_refs, send_sems, recv_sems):
            send.start()
        token[...] = jnp.zeros_like(token)
    arrays = list(srcs) + list(lands)
    outs = pl.pallas_call(
        body, name=name,
        out_shape=(pltpu.SemaphoreType.DMA((3 * nl,)), pltpu.SemaphoreType.DMA((3 * nl,)),
                   *[pltpu.HBM(v.shape, v.dtype) for v in arrays], jax.ShapeDtypeStruct((8, 128), F32)),
        in_specs=[_HBM] * na + [_ANY],
        out_specs=(_SEM, _SEM, *[_HBM] * na, pl.BlockSpec(memory_space=pltpu.VMEM)),
        input_output_aliases={i: 2 + i for i in range(na)},
        compiler_params=pltpu.CompilerParams(has_side_effects=_EFFECT))(*[_in_hbm(v) for v in arrays], after)
    return outs[0], outs[1], list(outs[2:2 + ns]), list(outs[2 + ns:2 + na]), outs[-1]


def exchange_wait(name, mode, send_sems, recv_sems, srcs, lands, after):
    ns, nl = len(srcs), len(lands)
    na = ns + nl

    def body(*refs):
        for _, recv in _split_copies(mode, refs[:ns], refs[ns:na], refs[na], refs[na + 1]):
            recv.wait_send()
            recv.wait_recv()
    arrays = list(srcs) + list(lands)
    outs = pl.pallas_call(
        body, name=name,
        out_shape=[pltpu.HBM(v.shape, v.dtype) for v in arrays],
        in_specs=[_HBM] * na + [_SEM, _SEM, _ANY], out_specs=[_HBM] * na,
        input_output_aliases={i: i for i in range(na)},
        compiler_params=pltpu.CompilerParams(has_side_effects=_EFFECT))(*arrays, send_sems, recv_sems, after)
    return list(outs[:ns]), list(outs[ns:])


def swap_with_sibling(vs):
    n = len(vs)

    def body(*refs):
        v_refs, out_refs, send_sems, recv_sems = refs[:n], refs[n:2 * n], refs[2 * n], refs[2 * n + 1]
        x, y, c = _me()
        cps = [pltpu.make_async_remote_copy(src_ref=v_refs[t], dst_ref=out_refs[t], send_sem=send_sems.at[t],
                                            recv_sem=recv_sems.at[t], device_id=(x, y, 1 - c), device_id_type=MESH)
               for t in range(n)]
        for cp in cps:
            cp.start()
        for cp in cps:
            cp.wait()
    return pl.pallas_call(
        body, name="swap_with_sibling", out_shape=[jax.ShapeDtypeStruct(v.shape, v.dtype) for v in vs],
        in_specs=[_ANY] * n, out_specs=[_ANY] * n,
        scratch_shapes=[pltpu.SemaphoreType.DMA((n,)), pltpu.SemaphoreType.DMA((n,))])(*vs)


def sum_slots(parts):
    n, rows, w = parts.shape
    tr = _row_tile(rows)

    def body(p_ref, o_ref):
        acc = p_ref[0].astype(F32)
        for k in range(1, n):
            acc = acc + p_ref[k].astype(F32)
        o_ref[...] = acc
    return pl.pallas_call(
        body, name="sum_slots", grid=(rows // tr,),
        in_specs=[pl.BlockSpec((n, tr, w), lambda i: (0, i, 0))], out_specs=pl.BlockSpec((tr, w), lambda i: (i, 0)),
        out_shape=jax.ShapeDtypeStruct((rows, w), F32), compiler_params=_params(48))(parts)


def sum_landed(landed, own, chip_id, layer, n_layers, buf):
    n, rows, w = landed.shape
    tr = _row_tile(rows)
    base = layer * (rows // tr)

    def compute(l_ref, g_ref, o_ref):
        acc = g_ref[...].astype(F32)
        for k in range(n):
            acc = acc + l_ref[k].astype(F32)
        o_ref[...] = acc
    in_specs = [pl.BlockSpec((n, tr, w), lambda i, chip: (0, i, 0)),
                pl.BlockSpec((None, tr, w), lambda i, chip: (chip[0], i, 0))]
    out_spec = pl.BlockSpec((tr, w), lambda i, chip: (base + i, 0))
    out_shape = jax.ShapeDtypeStruct((n_layers * rows, w), F32)
    if buf is None:
        def body(chip_ref, l_ref, g_ref, o_ref):
            del chip_ref
            compute(l_ref, g_ref, o_ref)
        return pl.pallas_call(
            body, name="sum_landed",
            grid_spec=pltpu.PrefetchScalarGridSpec(num_scalar_prefetch=1, grid=(rows // tr,), in_specs=in_specs,
                                                   out_specs=out_spec),
            out_shape=out_shape, compiler_params=_params(48))(chip_id, landed, own)

    def body(chip_ref, l_ref, g_ref, buf_ref, o_ref):
        del chip_ref, buf_ref
        compute(l_ref, g_ref, o_ref)
    return pl.pallas_call(
        body, name="sum_landed_into",
        grid_spec=pltpu.PrefetchScalarGridSpec(num_scalar_prefetch=1, grid=(rows // tr,),
                                               in_specs=in_specs + [_ANY], out_specs=out_spec),
        out_shape=out_shape, input_output_aliases={3: 0}, compiler_params=_params(48))(chip_id, landed, own, buf)


def adamw(w, ga, gb, m, v):
    rows, wd = w.shape
    tr = min(_row_tile(rows), 128)
    c1 = 1.0 / (1.0 - ADAM_B1 ** ADAM_STEP)
    c2 = 1.0 / (1.0 - ADAM_B2 ** ADAM_STEP)

    def update(wv, g, mv, vv, g_ref, d_ref, m_ref, v_ref):
        mn = ADAM_B1 * mv + (1.0 - ADAM_B1) * g
        vn = ADAM_B2 * vv + (1.0 - ADAM_B2) * (g * g)
        g_ref[...] = g
        m_ref[...] = mn
        v_ref[...] = vn
        d_ref[...] = -ADAM_LR * ((mn * c1) / (jnp.sqrt(vn * c2) + ADAM_EPS) + ADAM_WD * wv)
    tile = pl.BlockSpec((tr, wd), lambda i: (i, 0))
    out = jax.ShapeDtypeStruct((rows, wd), F32)
    if gb is None:
        def body(w_ref, ga_ref, m_ref, v_ref, g_out, d_out, m_out, v_out):
            update(w_ref[...], ga_ref[...], m_ref[...], v_ref[...], g_out, d_out, m_out, v_out)
        return pl.pallas_call(body, name="adamw", grid=(rows // tr,), in_specs=[tile] * 4,
                              out_specs=[tile] * 4, out_shape=[out] * 4)(w, ga, m, v)

    def body(w_ref, ga_ref, gb_ref, m_ref, v_ref, g_out, d_out, m_out, v_out):
        update(w_ref[...], ga_ref[...] + gb_ref[...], m_ref[...], v_ref[...], g_out, d_out, m_out, v_out)
    return pl.pallas_call(body, name="adamw_sum", grid=(rows // tr,), in_specs=[tile] * 5,
                          out_specs=[tile] * 4, out_shape=[out] * 4)(w, ga, gb, m, v)


def _rope_tables(T, R):
    rows = T // GRID_W
    row = jnp.repeat(jnp.arange(rows), GRID_W).astype(F32)
    col = jnp.tile(jnp.arange(GRID_W), rows).astype(F32)
    n_freq = HEAD_DIM // 4
    inv_freq = ROPE_THETA ** (-jnp.arange(n_freq, dtype=F32) / n_freq)
    ang = jnp.concatenate([row[:, None] * inv_freq, col[:, None] * inv_freq], axis=-1)
    cos, sin = jnp.cos(ang), jnp.sin(ang)
    cs = jnp.tile(cos, (1, 4))
    sn = jnp.tile(jnp.concatenate([-sin, sin], axis=-1), (1, 2))
    pad = R - T
    return (jnp.concatenate([cs, jnp.ones((pad, 128), F32)], axis=0),
            jnp.concatenate([sn, jnp.zeros((pad, 128), F32)], axis=0))


def _pack(parts, mult=8 * 128):
    flat = jnp.concatenate([p.reshape(-1).astype(F32) for p in parts])
    pad = (-flat.shape[0]) % mult
    return jnp.pad(flat, (0, pad)).reshape(-1, 128)


def _unpack(buf, shapes):
    flat = buf.reshape(-1)
    out, o = [], 0
    for s in shapes:
        n = 1
        for d in s:
            n *= d
        out.append(flat[o:o + n].reshape(s))
        o += n
    return out


def kernel(x, c, ctx, c_ctx, w_mod, b_mod, norm_mix, norm_ffn, w_in_ab, conv_a, conv_b, conv_b_bias, ln_b_gain, ln_b_bias, w_out_ab, w_qkv, w_o, sinks, w_up, w_conv_ffn, w_down, final_norm, loss_target, m_c_ctx, m_w_mod, m_b_mod, m_norm_mix, m_norm_ffn, m_w_in_ab, m_conv_a, m_conv_b, m_conv_b_bias, m_ln_b_gain, m_ln_b_bias, m_w_out_ab, m_w_qkv, m_w_o, m_sinks, m_w_up, m_w_conv_ffn, m_w_down, m_final_norm, v_c_ctx, v_w_mod, v_b_mod, v_norm_mix, v_norm_ffn, v_w_in_ab, v_conv_a, v_conv_b, v_conv_b_bias, v_ln_b_gain, v_ln_b_bias, v_w_out_ab, v_w_qkv, v_w_o, v_sinks, v_w_up, v_w_conv_ffn, v_w_down, v_final_norm):
    T, dm = x.shape[1], x.shape[2]
    tc = ctx.shape[1]
    R = T + tc
    depth = w_mod.shape[0]
    ax, ay, ac = lax.axis_index("x"), lax.axis_index("y"), lax.axis_index("c")
    chip = 2 * ax + ay
    dev = 4 * ax + 2 * ay + ac

    small_w = [conv_a, conv_b, w_conv_ffn]
    gathered = allgather8(_pack([c] + small_w)).reshape(N_DEV, -1)
    cond8 = gathered[:, :dm]
    off = dm
    full_small = []
    for wsh in small_w:
        n = wsh.size
        per_chip = gathered[0::2, off:off + n].reshape((N_CHIP,) + wsh.shape)
        full_small.append(jnp.concatenate([per_chip[q] for q in range(N_CHIP)], axis=-1))
        off += n
    conv_a_f, conv_b_f, w_conv_ffn_f = full_small
    cond = jnp.concatenate([cond8, c_ctx[None, :], jnp.zeros((7, dm), F32)], axis=0)

    ns_mod = w_mod.shape[2]
    b_mod_sh = lax.dynamic_slice_in_dim(b_mod, chip * ns_mod, ns_mod, axis=1)[:, None, :]
    mod_sh = adaln_fwd(cond, w_mod, b_mod_sh)
    mod_all = allgather8(mod_sh.reshape(depth * 16, ns_mod)).reshape(N_DEV, depth, 16, ns_mod)
    mod_full = jnp.concatenate([mod_all[2 * q] for q in range(N_CHIP)], axis=-1)
    mine = lax.dynamic_index_in_dim(mod_full, dev, axis=1, keepdims=False)
    mods = jnp.stack([mine, mod_full[:, 8]], axis=1).reshape(depth, 2, 6, dm)

    masters = {"w_in_ab": w_in_ab, "w_out_ab": w_out_ab, "w_qkv": w_qkv, "w_o": w_o, "w_up": w_up, "w_down": w_down}
    chip_id = chip.astype(jnp.int32).reshape(1)

    def half_weights(l, half):
        if half == 1:
            return [("w_up", l), ("w_down", l)]
        return [("w_in_ab", l // 2), ("w_out_ab", l // 2)] if l % 2 == 0 else [("w_qkv", l // 2), ("w_o", l // 2)]
    in_flight, after = {}, mods
    for l in range(depth):
        for half in range(2):
            lands = [cast_into_slot(masters[n], j, chip_id) for n, j in half_weights(l, half)]
            send_sems, recv_sems, _, lands, after = exchange_start(f"gather_start_{l}_{half}", "gather", [], lands, after)
            in_flight[l, half] = (send_sems, recv_sems, lands)
    mods = mods + after[0, 0]

    def gathered_weights(l, half, after):
        send_sems, recv_sems, lands = in_flight[l, half]
        _, landed = exchange_wait(f"gather_wait_{l}_{half}", "gather", send_sems, recv_sems, [], lands, after)
        return dict(zip([n for n, _ in half_weights(l, half)], landed))

    cs, sn = _rope_tables(T, R)
    bias = window_bias(T, R)
    sinks_flat = sinks.reshape(-1)

    xs = jnp.concatenate([x[0], ctx[0]], axis=0)
    saved, W = [], []
    h1 = norm_mod_fwd(xs, norm_mix[0][None], mods[0], 0, T)
    for l in range(depth):
        e = l // 2
        wl = gathered_weights(l, 0, h1)
        W.append(wl)
        s = {"x0": xs, "h1": h1}
        if l % 2 == 0:
            p = mm_nn(h1, wl["w_in_ab"], BF16)
            yab, y1, x1, h2 = mixer_fwd(p, conv_a_f[e], conv_b_f[e], conv_b_bias[e][None], ln_b_gain[e][None],
                                        ln_b_bias[e][None], wl["w_out_ab"], xs, norm_ffn[l][None], mods[l], 2, 3, T)
            s.update(p=p, mix=yab)
        else:
            qkvr = mm_qkv_rope(h1, wl["w_qkv"], cs, sn)
            att, probs = attn_fwd(qkvr, sinks_flat[e * N_HEADS:(e + 1) * N_HEADS], bias, T)
            s.update(qkvr=qkvr, mix=att, probs=probs)
            y1, x1, h2 = mm_resid_norm_fwd(att, wl["w_o"], xs, norm_ffn[l][None], mods[l], mods[l], 2, 3, T)
        wl.update(gathered_weights(l, 1, h2))
        u = mm_nn(h2, wl["w_up"], BF16)
        if l + 1 < depth:
            z, y2, xs, h1 = ffn_mid_fwd(u, w_conv_ffn_f[l], wl["w_down"], x1, norm_mix[l + 1][None], mods[l],
                                        mods[l + 1], 5, 0, T)
        else:
            z, y2, xs = ffn_mid_fwd(u, w_conv_ffn_f[l], wl["w_down"], x1, None, mods[l], None, 5, 0, T)
        s.update(y1=y1, x1=x1, h2=h2, u=u, z=z, y2=y2)
        saved.append(s)

    loss_part, dx, d_final, dy2, dg2_last = loss_head(xs, final_norm[None], loss_target[0], saved[depth - 1]["y2"],
                                                      mods[depth - 1], 5, T)
    loss = lax.psum(loss_part[0, 0], ("x", "y", "c"))

    d_mods, d_norm_mix, d_norm_ffn = [None] * depth, [None] * depth, [None] * depth
    d_conv_a, d_conv_b, d_vecs, d_sinks, d_wc = [None] * 2, [None] * 2, [None] * 2, [None] * 2, [None] * depth
    dss1, dss2, dg1, dg2 = [None] * depth, [None] * depth, [None] * depth, [None] * depth
    scattering = {}

    def scatter(l, half, G, after):
        grads_h = [G[n] for n, _ in half_weights(l, half)]
        lands = [lax.empty((N_CHIP - 1, *g.shape[1:]), g.dtype) for g in grads_h]
        send_sems, recv_sems, grads_h, lands, token = exchange_start(
            f"scatter_start_{l}_{half}", "scatter", grads_h, lands, after)
        scattering[l, half] = (send_sems, recv_sems, grads_h, lands)
        return token

    dg2[depth - 1] = dg2_last
    pending = 0.0
    for l in reversed(range(depth)):
        e = l // 2
        s, wl = saved[l], W[l]
        G = {}
        G["w_down"] = mm_tn(s["z"], dy2, "row", wl["w_down"])
        duc, d_wc[l] = ffn_mid_bwd(mm_nt(dy2, wl["w_down"], BF16), s["u"], w_conv_ffn_f[l] + pending, T)
        du, dx, dy1, dss2[l], d_norm_ffn[l], dg1[l] = ffn_up_bwd(
            duc, w_conv_ffn_f[l], wl["w_up"], s["x1"], norm_ffn[l][None], mods[l], dx, s["y1"], mods[l], 3, 2, T)
        G["w_up"] = mm_tn(s["h2"], du, "col", wl["w_up"])
        started = scatter(l, 1, G, du)[0, 0]
        if l % 2 == 0:
            G["w_out_ab"] = mm_tn(s["mix"], dy1, "row", wl["w_out_ab"])
            dyab = mm_nt(dy1, wl["w_out_ab"], F32)
            dmid, d_conv_a[e], d_conv_b[e], d_vecs[e] = convmix_bwd1(
                dyab, s["p"], conv_a_f[e] + started, conv_b_f[e], conv_b_bias[e][None], ln_b_gain[e][None],
                ln_b_bias[e][None], T)
            if l > 0:
                dp, dx, dy2, dss1[l], d_norm_mix[l], dg2[l - 1] = mixer_in_bwd(
                    dmid, s["p"], conv_a_f[e], conv_b_f[e], wl["w_in_ab"], s["x0"], norm_mix[l][None], mods[l], dx,
                    saved[l - 1]["y2"], mods[l - 1], 0, 5, T)
            else:
                dp, dx, dss1[l], d_norm_mix[l] = mixer_in_bwd(
                    dmid, s["p"], conv_a_f[e], conv_b_f[e], wl["w_in_ab"], s["x0"], norm_mix[l][None], mods[l], dx,
                    None, None, 0, 0, T)
            G["w_in_ab"] = mm_tn(s["h1"], dp, "col", wl["w_in_ab"])
        else:
            G["w_o"] = mm_tn(s["mix"], dy1, "row", wl["w_o"])
            datt = mm_nt(dy1, wl["w_o"], BF16)
            dq, dks, dvs, dkc, dvc, d_sinks[e] = attn_bwd(s["qkvr"], datt, s["probs"], T)
            dqkv, dx, dy2, dss1[l], d_norm_mix[l], dg2[l - 1] = attn_in_bwd(
                dq, dks, dvs, dkc, dvc, cs + started, sn, wl["w_qkv"], s["x0"], norm_mix[l][None], mods[l], dx,
                saved[l - 1]["y2"], mods[l - 1], 0, 5, T)
            G["w_qkv"] = mm_tn(s["h1"], dqkv, "col", wl["w_qkv"])
        token = scatter(l, 0, G, dx)
        pending = token[0, 0]
    grad_x = dx[:T][None]
    for l in range(depth):
        a1, a2 = dss1[l].sum(2), dss2[l].sum(2)
        d_mods[l] = jnp.stack([a1[:, 0], a1[:, 1], dg1[l].sum(1), a2[:, 0], a2[:, 1], dg2[l].sum(1)], axis=1)

    d_mods = jnp.stack(d_mods)
    summed_parts = [
        d_mods[:, 1],
        jnp.stack(d_norm_mix).sum(1), jnp.stack(d_norm_ffn).sum(1),
        jnp.stack(d_conv_a).sum(2), jnp.stack(d_conv_b).sum(2),
        jnp.stack(d_vecs).sum(2),
        jnp.stack(d_sinks)[:, 0, :N_HEADS],
        jnp.stack(d_wc).sum(2), d_final.sum(0) + pending]
    summed_shapes = [p.shape for p in summed_parts]
    n_own = depth * 6 * dm
    pack = _pack([d_mods[:, 0]] + summed_parts)
    parts = allgather8(pack).reshape(N_DEV, -1, 128)
    total = sum_slots(parts)
    own_rows = parts.reshape(N_DEV, -1)[:, :n_own].reshape(N_DEV, depth, 6 * dm)
    (dmod_ctx, g_norm_mix, g_norm_ffn, g_conv_a, g_conv_b, g_vecs, g_sinks, g_wc, g_final) = _unpack(
        total.reshape(-1)[n_own:], summed_shapes)
    dmod_rows = jnp.concatenate([jnp.moveaxis(own_rows, 0, 1), dmod_ctx.reshape(depth, 1, 6 * dm),
                                 jnp.zeros((depth, 7, 6 * dm), F32)], axis=1)
    g_b_mod = dmod_rows.sum(1)
    dmod_sh = lax.dynamic_slice_in_dim(dmod_rows, chip * ns_mod, ns_mod, axis=2)
    g_w_mod, dsilu = adaln_bwd(cond, dmod_sh, w_mod)
    dsilu_all = allgather8(dsilu[8:16]).reshape(N_DEV, 8, dm)
    dsilu_ctx = sum_slots(dsilu_all[0::2])[0]
    sg = jax.nn.sigmoid(c_ctx)
    g_c_ctx = dsilu_ctx * (sg * (1.0 + c_ctx * (1.0 - sg)))

    def shard_cols(full, width):
        return lax.dynamic_slice_in_dim(full, chip * width, width, axis=full.ndim - 1)
    g_conv_a_s = shard_cols(g_conv_a, conv_a.shape[-1])
    g_conv_b_s = shard_cols(g_conv_b, conv_b.shape[-1])
    g_wc_s = shard_cols(g_wc, w_conv_ffn.shape[-1])

    grads, deltas, new_m, new_v = {}, {}, {}, {}

    def step_2d(name, wv, ga, gb, mv, vv):
        shp = wv.shape
        r2 = lambda t: t.reshape(-1, shp[-1])
        g, d, mn, vn = adamw(r2(wv), r2(ga), None if gb is None else r2(gb), r2(mv), r2(vv))
        grads[name], deltas[name], new_m[name], new_v[name] = (t.reshape(shp) for t in (g, d, mn, vn))

    step_2d("w_mod", w_mod, g_w_mod, None, m_w_mod, v_w_mod)
    sums = {n: None for n in masters}
    for l in reversed(range(depth)):
        for half in (1, 0):
            send_sems, recv_sems, grads_h, lands = scattering[l, half]
            grads_h, landed = exchange_wait(f"scatter_wait_{l}_{half}", "scatter", send_sems, recv_sems, grads_h,
                                            lands, deltas["w_mod"])
            for (n, j), own, arr in zip(half_weights(l, half), grads_h, landed):
                sums[n] = sum_landed(arr, own, chip_id, j, masters[n].shape[0], sums[n])
    moments = {"w_in_ab": (m_w_in_ab, v_w_in_ab), "w_out_ab": (m_w_out_ab, v_w_out_ab),
               "w_qkv": (m_w_qkv, v_w_qkv), "w_o": (m_w_o, v_w_o), "w_up": (m_w_up, v_w_up),
               "w_down": (m_w_down, v_w_down)}
    others = swap_with_sibling([sums[name] for name in masters])
    for (name, wv), other in zip(masters.items(), others):
        step_2d(name, wv, sums[name].reshape(wv.shape), other.reshape(wv.shape), *moments[name])

    small = [("c_ctx", c_ctx, g_c_ctx, m_c_ctx, v_c_ctx), ("b_mod", b_mod, g_b_mod, m_b_mod, v_b_mod),
             ("norm_mix", norm_mix, g_norm_mix, m_norm_mix, v_norm_mix),
             ("norm_ffn", norm_ffn, g_norm_ffn, m_norm_ffn, v_norm_ffn),
             ("conv_a", conv_a, g_conv_a_s, m_conv_a, v_conv_a), ("conv_b", conv_b, g_conv_b_s, m_conv_b, v_conv_b),
             ("conv_b_bias", conv_b_bias, g_vecs[:, 0], m_conv_b_bias, v_conv_b_bias),
             ("ln_b_gain", ln_b_gain, g_vecs[:, 1], m_ln_b_gain, v_ln_b_gain),
             ("ln_b_bias", ln_b_bias, g_vecs[:, 2], m_ln_b_bias, v_ln_b_bias),
             ("sinks", sinks, g_sinks, m_sinks, v_sinks),
             ("w_conv_ffn", w_conv_ffn, g_wc_s, m_w_conv_ffn, v_w_conv_ffn),
             ("final_norm", final_norm, g_final, m_final_norm, v_final_norm)]
    shapes = [t[1].shape for t in small]
    packed = [_pack([t[k] for t in small]) for k in (1, 2, 3, 4)]
    n_real = sum(t[1].size for t in small)
    lane_id = jnp.arange(packed[3].size).reshape(packed[3].shape)
    packed[3] = jnp.where(lane_id < n_real, packed[3], 1.0)
    outs = adamw(packed[0], packed[1], None, packed[2], packed[3])
    for (name, *_), g, d, mn, vn in zip(small, *[_unpack(o, shapes) for o in outs]):
        grads[name], deltas[name], new_m[name], new_v[name] = g, d, mn, vn

    order = ["c_ctx", "w_mod", "b_mod", "norm_mix", "norm_ffn", "w_in_ab", "conv_a", "conv_b", "conv_b_bias",
             "ln_b_gain", "ln_b_bias", "w_out_ab", "w_qkv", "w_o", "sinks", "w_up", "w_conv_ffn", "w_down",
             "final_norm"]
    return (loss, grad_x, *[grads[n] for n in order], *[deltas[n] for n in order],
            *[new_m[n] for n in order], *[new_v[n] for n in order])
```

```python
import jax
import jax.numpy as jnp
from jax import lax
from jax.experimental import pallas as pl
from jax.experimental.pallas import tpu as pltpu

F32 = jnp.float32
BF16 = jnp.bfloat16
MESH = pl.DeviceIdType.MESH

EPS = 1e-6
NEG_INF = -1e30
GRID_W = 64
HEAD_DIM = 64
N_HEADS = 16
WINDOW = 128
QB = 128
ROPE_THETA = 10000.0
A_W = 512
B_CONV = 31
D_FF = 2816
ADAM_LR, ADAM_B1, ADAM_B2, ADAM_EPS, ADAM_WD, ADAM_STEP = 0.001, 0.9, 0.999, 1e-8, 0.01, 10

TMR = 256
HALO = 16
N_DEV = 8
N_CHIP = 4


def _params(vmem_mb=None):
    if vmem_mb is None:
        return pltpu.CompilerParams()
    return pltpu.CompilerParams(vmem_limit_bytes=vmem_mb * 1024 * 1024)


def _row_tile(rows, cap=768):
    for t in (2816, 1408, 768, 704, 512, 384, 256, 128, 64, 32, 16, 8):
        if t <= cap and rows % t == 0:
            return t
    raise ValueError(rows)


def _colsum8(v):
    r, c = v.shape
    return v.reshape(r // 8, 8, c).sum(axis=0)


def _sigmoid(v):
    return 0.5 * jnp.tanh(0.5 * v) + 0.5


def mm_nn(a, w, out_dtype):
    R = a.shape[0]
    _, kb, nb = w.shape
    tm = _row_tile(R)

    def body(a_ref, w_ref, o_ref):
        av = a_ref[...].astype(BF16)
        for q in range(N_CHIP):
            o_ref[:, q * nb:(q + 1) * nb] = jnp.dot(av, w_ref[q], preferred_element_type=F32).astype(o_ref.dtype)
    return pl.pallas_call(
        body, name="mm_nn_col", grid=(R // tm,),
        in_specs=[pl.BlockSpec((tm, kb), lambda i: (i, 0)),
                  pl.BlockSpec((N_CHIP, kb, nb), lambda i: (0, 0, 0), pipeline_mode=pl.Buffered(1))],
        out_specs=pl.BlockSpec((tm, N_CHIP * nb), lambda i: (i, 0)),
        out_shape=jax.ShapeDtypeStruct((R, N_CHIP * nb), out_dtype),
        compiler_params=_params(48))(a, w)


def mm_nt(d, w, out_dtype):
    R = d.shape[0]
    _, kb, nb = w.shape
    tm = _row_tile(R)
    contract_last = (((1,), (1,)), ((), ()))
    resident = pl.BlockSpec((N_CHIP, kb, nb), lambda i: (0, 0, 0), pipeline_mode=pl.Buffered(1))

    def body(d_ref, w_ref, o_ref):
        wv = w_ref[...].reshape(N_CHIP * kb, nb)
        o_ref[...] = lax.dot_general(d_ref[...].astype(BF16), wv, contract_last,
                                     preferred_element_type=F32).astype(o_ref.dtype)
    return pl.pallas_call(
        body, name="mm_nt_row", grid=(R // tm,),
        in_specs=[pl.BlockSpec((tm, nb), lambda i: (i, 0)), resident],
        out_specs=pl.BlockSpec((tm, N_CHIP * kb), lambda i: (i, 0)),
        out_shape=jax.ShapeDtypeStruct((R, N_CHIP * kb), out_dtype),
        compiler_params=_params(48))(d, w)


def mm_tn(a, d, kind, like):
    R = a.shape[0]
    _, kb, nb = like.shape
    tm = _row_tile(R, 1408 if kind == "col" else 768)
    nsteps = R // tm
    contract_rows = (((0,), (0,)), ((), ()))
    out_shape = jax.ShapeDtypeStruct(like.shape, BF16)

    def accumulate(a_ref, d_ref, acc_ref):
        @pl.when(pl.program_id(1) == 0)
        def _():
            acc_ref[...] = jnp.zeros_like(acc_ref)
        acc_ref[...] += lax.dot_general(a_ref[...].astype(BF16), d_ref[...].astype(BF16), contract_rows,
                                        preferred_element_type=F32)
    if kind == "col":
        def body(a_ref, d_ref, o_ref, acc_ref):
            accumulate(a_ref, d_ref, acc_ref)

            @pl.when(pl.program_id(1) == nsteps - 1)
            def _():
                o_ref[...] = acc_ref[...].astype(BF16)
        return pl.pallas_call(
            body, name="mm_tn_col", grid=(N_CHIP, nsteps),
            in_specs=[pl.BlockSpec((tm, kb), lambda q, i: (i, 0)), pl.BlockSpec((tm, nb), lambda q, i: (i, q))],
            out_specs=pl.BlockSpec((None, kb, nb), lambda q, i: (q, 0, 0)), out_shape=out_shape,
            scratch_shapes=[pltpu.VMEM((kb, nb), F32)], compiler_params=_params(48))(a, d)
    tn = 512

    def body(a_ref, d_ref, o_ref, acc_ref):
        accumulate(a_ref, d_ref, acc_ref)

        @pl.when(pl.program_id(1) == nsteps - 1)
        def _():
            o_ref[...] = acc_ref[...].astype(BF16).reshape(N_CHIP, kb, tn)
    return pl.pallas_call(
        body, name="mm_tn_row", grid=(nb // tn, nsteps),
        in_specs=[pl.BlockSpec((tm, N_CHIP * kb), lambda n, i: (i, 0)), pl.BlockSpec((tm, tn), lambda n, i: (i, n))],
        out_specs=pl.BlockSpec((N_CHIP, kb, tn), lambda n, i: (0, 0, n)), out_shape=out_shape,
        scratch_shapes=[pltpu.VMEM((N_CHIP * kb, tn), F32)], compiler_params=_params(48))(a, d)


def _seg(i, T):
    return (i >= T // TMR).astype(jnp.int32)


def norm_mod_fwd(x, nw, mod, k, T):
    R, dm = x.shape

    def body(x_ref, nw_ref, mod_ref, h_ref):
        seg = _seg(pl.program_id(0), T)
        sh = mod_ref[seg, pl.ds(k, 1), :]
        sc = mod_ref[seg, pl.ds(k + 1, 1), :]
        xv = x_ref[...]
        r = lax.rsqrt(jnp.mean(xv * xv, axis=-1, keepdims=True) + EPS)
        h_ref[...] = ((xv * r * nw_ref[...]) * (1.0 + sc) + sh).astype(BF16)
    return pl.pallas_call(
        body, name="norm_mod_fwd", grid=(R // TMR,),
        in_specs=[pl.BlockSpec((TMR, dm), lambda i: (i, 0)),
                  pl.BlockSpec((1, dm), lambda i: (0, 0)),
                  pl.BlockSpec((2, 6, dm), lambda i: (0, 0, 0))],
        out_specs=pl.BlockSpec((TMR, dm), lambda i: (i, 0)),
        out_shape=jax.ShapeDtypeStruct((R, dm), BF16))(x, nw, mod)


def mm_resid_norm_fwd(a, w, x, nw, mod_g, mod_n, kg, kn, T):
    R, dm = x.shape
    _, kb, nb = w.shape

    def body(a_ref, w_ref, x_ref, nw_ref, mg_ref, mn_ref, y_ref, xo_ref, h_ref):
        seg = _seg(pl.program_id(0), T)
        yv = jnp.dot(a_ref[...].astype(BF16), w_ref[...].reshape(N_CHIP * kb, nb), preferred_element_type=F32)
        y_ref[...] = yv
        xv = x_ref[...] + mg_ref[seg, pl.ds(kg, 1), :] * yv
        xo_ref[...] = xv
        r = lax.rsqrt(jnp.mean(xv * xv, axis=-1, keepdims=True) + EPS)
        h_ref[...] = ((xv * r * nw_ref[...]) * (1.0 + mn_ref[seg, pl.ds(kn + 1, 1), :])
                      + mn_ref[seg, pl.ds(kn, 1), :]).astype(BF16)
    tile = pl.BlockSpec((TMR, dm), lambda i: (i, 0))
    modspec = pl.BlockSpec((2, 6, dm), lambda i: (0, 0, 0))
    return pl.pallas_call(
        body, name="mm_resid_norm_fwd", grid=(R // TMR,),
        in_specs=[pl.BlockSpec((TMR, N_CHIP * kb), lambda i: (i, 0)),
                  pl.BlockSpec((N_CHIP, kb, nb), lambda i: (0, 0, 0), pipeline_mode=pl.Buffered(1)),
                  tile, pl.BlockSpec((1, dm), lambda i: (0, 0)), modspec, modspec],
        out_specs=[tile, tile, tile],
        out_shape=[jax.ShapeDtypeStruct((R, dm), F32), jax.ShapeDtypeStruct((R, dm), F32),
                   jax.ShapeDtypeStruct((R, dm), BF16)],
        compiler_params=_params(48))(a, w, x, nw, mod_g, mod_n)


def _halo_specs(width, R):
    nblk = R // HALO
    per = TMR // HALO
    return (pl.BlockSpec((HALO, width), lambda i: (jnp.maximum(i * per - 1, 0), 0)),
            pl.BlockSpec((TMR, width), lambda i: (i, 0)),
            pl.BlockSpec((HALO, width), lambda i: (jnp.minimum((i + 1) * per, nblk - 1), 0)))


def _halo_live(i, T, R):
    nl = T // TMR
    return (i != 0) & (i != nl), (i != nl - 1) & (i != R // TMR - 1)


def _ext(refs, c0, cw, live, halo=HALO):
    pref, ref, nref = refs
    before = jnp.where(live[0], pref[:, c0:c0 + cw].astype(F32)[HALO - halo:], 0.0)
    after = jnp.where(live[1], nref[:, c0:c0 + cw].astype(F32)[:halo], 0.0)
    return jnp.concatenate([before, ref[:, c0:c0 + cw].astype(F32), after], axis=0)


def _at(ext, off, halo=HALO):
    n = ext.shape[0]
    s = (-off) % n
    y = pltpu.roll(ext, s, 0) if s else ext
    return y[halo:halo + TMR]


def ffn_mid_fwd(u, wc, w_down, x, nw, mod_g, mod_n, kg, kn, T):
    R, w2 = u.shape
    dm = x.shape[1]
    _, kb, nb = w_down.shape
    cw = 256
    with_norm = nw is not None

    def body(*refs):
        if with_norm:
            up_ref, u_ref, un_ref, wc_ref, w_ref, x_ref, nw_ref, mg_ref, mn_ref, z_ref, y_ref, xo_ref, h_ref = refs
        else:
            up_ref, u_ref, un_ref, wc_ref, w_ref, x_ref, mg_ref, z_ref, y_ref, xo_ref = refs
        i = pl.program_id(0)
        seg = _seg(i, T)
        live = _halo_live(i, T, R)

        def conv(c0):
            e = _ext((up_ref, u_ref, un_ref), c0, cw, live, 8)
            return (wc_ref[pl.ds(0, 1), c0:c0 + cw] * _at(e, -1, 8) + wc_ref[pl.ds(1, 1), c0:c0 + cw] * _at(e, 0, 8)
                    + wc_ref[pl.ds(2, 1), c0:c0 + cw] * _at(e, 1, 8))
        yv = None
        for j in range(D_FF // cw):
            a = conv(j * cw)
            g = conv(D_FF + j * cw)
            zc = (g * _sigmoid(g) * a).astype(BF16)
            z_ref[:, j * cw:(j + 1) * cw] = zc
            t = jnp.dot(zc, w_ref[j * cw:(j + 1) * cw, :], preferred_element_type=F32)
            yv = t if yv is None else yv + t
        y_ref[...] = yv
        xv = x_ref[...] + mg_ref[seg, pl.ds(kg, 1), :] * yv
        xo_ref[...] = xv
        if with_norm:
            r = lax.rsqrt(jnp.mean(xv * xv, axis=-1, keepdims=True) + EPS)
            h_ref[...] = ((xv * r * nw_ref[...]) * (1.0 + mn_ref[seg, pl.ds(kn + 1, 1), :])
                          + mn_ref[seg, pl.ds(kn, 1), :]).astype(BF16)
    tile = pl.BlockSpec((TMR, dm), lambda i: (i, 0))
    modspec = pl.BlockSpec((2, 6, dm), lambda i: (0, 0, 0))
    w_down = w_down.reshape(N_CHIP * kb, nb)
    in_specs = [*_halo_specs(w2, R), pl.BlockSpec((3, w2), lambda i: (0, 0)),
                pl.BlockSpec((N_CHIP * kb, nb), lambda i: (0, 0), pipeline_mode=pl.Buffered(1)), tile]
    out_specs = [pl.BlockSpec((TMR, D_FF), lambda i: (i, 0)), tile, tile]
    out_shape = [jax.ShapeDtypeStruct((R, D_FF), BF16), jax.ShapeDtypeStruct((R, dm), F32),
                 jax.ShapeDtypeStruct((R, dm), F32)]
    if with_norm:
        return pl.pallas_call(
            body, name="ffn_mid_fwd", grid=(R // TMR,),
            in_specs=in_specs + [pl.BlockSpec((1, dm), lambda i: (0, 0)), modspec, modspec],
            out_specs=out_specs + [tile], out_shape=out_shape + [jax.ShapeDtypeStruct((R, dm), BF16)],
            compiler_params=_params(48))(u, u, u, wc, w_down, x, nw, mod_g, mod_n)
    return pl.pallas_call(
        body, name="ffn_mid_fwd_last", grid=(R // TMR,), in_specs=in_specs + [modspec],
        out_specs=out_specs, out_shape=out_shape, compiler_params=_params(48))(u, u, u, wc, w_down, x, mod_g)


def ffn_mid_bwd(dz, u, wc, T):
    R, w2 = u.shape
    cw = 256

    def body(dz_ref, up_ref, u_ref, un_ref, wc_ref, duc_ref, dwc_ref):
        i = pl.program_id(0)
        live = _halo_live(i, T, R)

        @pl.when(i == 0)
        def _():
            dwc_ref[...] = jnp.zeros_like(dwc_ref)

        def taps(c0):
            e = _ext((up_ref, u_ref, un_ref), c0, cw, live, 8)
            return [_at(e, -1, 8), _at(e, 0, 8), _at(e, 1, 8)]

        def conv(t, c0):
            return (wc_ref[pl.ds(0, 1), c0:c0 + cw] * t[0] + wc_ref[pl.ds(1, 1), c0:c0 + cw] * t[1]
                    + wc_ref[pl.ds(2, 1), c0:c0 + cw] * t[2])
        for j in range(D_FF // cw):
            ca, cg = j * cw, D_FF + j * cw
            dzv = dz_ref[:, ca:ca + cw].astype(F32)
            ta, tg = taps(ca), taps(cg)
            a, g = conv(ta, ca), conv(tg, cg)
            sg = _sigmoid(g)
            da = dzv * (g * sg)
            dg = dzv * a * (sg * (1.0 + g * (1.0 - sg)))
            duc_ref[:, ca:ca + cw] = da.astype(BF16)
            duc_ref[:, cg:cg + cw] = dg.astype(BF16)
            for k in range(3):
                dwc_ref[k, :, ca:ca + cw] += _colsum8(da * ta[k])
                dwc_ref[k, :, cg:cg + cw] += _colsum8(dg * tg[k])
    return pl.pallas_call(
        body, name="ffn_mid_bwd", grid=(R // TMR,),
        in_specs=[pl.BlockSpec((TMR, D_FF), lambda i: (i, 0)), *_halo_specs(w2, R),
                  pl.BlockSpec((3, w2), lambda i: (0, 0))],
        out_specs=[pl.BlockSpec((TMR, w2), lambda i: (i, 0)), pl.BlockSpec((3, 8, w2), lambda i: (0, 0, 0))],
        out_shape=[jax.ShapeDtypeStruct((R, w2), BF16), jax.ShapeDtypeStruct((3, 8, w2), F32)],
        compiler_params=_params(48))(dz, u, u, u, wc)


def ffn_up_bwd(duc, wc, w_up, x, nw, mod_n, dxr, y, mod_g, kn, kg, T):
    R, w2 = duc.shape
    dm = x.shape[1]
    _, kb, nb = w_up.shape
    cw = 128
    contract_last = (((1,), (1,)), ((), ()))

    def body(dp_ref, d_ref, dn_ref, wc_ref, w_ref, x_ref, nw_ref, mn_ref, dxr_ref, y_ref, mg_ref,
             du_ref, dx_ref, dy_ref, dmod_ref, dnw_ref, dg_ref):
        i = pl.program_id(0)
        seg = _seg(i, T)
        live = _halo_live(i, T, R)

        @pl.when(i == 0)
        def _():
            dmod_ref[...] = jnp.zeros_like(dmod_ref)
            dnw_ref[...] = jnp.zeros_like(dnw_ref)
            dg_ref[...] = jnp.zeros_like(dg_ref)
        dhv = None
        for q in range(N_CHIP):
            for j in range(nb // cw):
                c0 = q * nb + j * cw
                e = _ext((dp_ref, d_ref, dn_ref), c0, cw, live, 8)
                du_ref[:, c0:c0 + cw] = (wc_ref[pl.ds(0, 1), c0:c0 + cw] * _at(e, 1, 8)
                                         + wc_ref[pl.ds(1, 1), c0:c0 + cw] * _at(e, 0, 8)
                                         + wc_ref[pl.ds(2, 1), c0:c0 + cw] * _at(e, -1, 8)).astype(BF16)
            t = lax.dot_general(du_ref[:, q * nb:(q + 1) * nb], w_ref[q], contract_last,
                                preferred_element_type=F32)
            dhv = t if dhv is None else dhv + t
        sc = mn_ref[seg, pl.ds(kn + 1, 1), :]
        nwv = nw_ref[...]
        xv = x_ref[...]
        r = lax.rsqrt(jnp.mean(xv * xv, axis=-1, keepdims=True) + EPS)
        xh = xv * r
        dmod_ref[seg, 0] += _colsum8(dhv)
        dmod_ref[seg, 1] += _colsum8(dhv * (xh * nwv))
        dn = dhv * (1.0 + sc)
        dnw_ref[...] += _colsum8(dn * xh)
        dxh = dn * nwv
        dx = dxr_ref[...] + r * (dxh - xh * jnp.mean(dxh * xh, axis=-1, keepdims=True))
        dx_ref[...] = dx
        dy_ref[...] = (mg_ref[seg, pl.ds(kg, 1), :] * dx).astype(BF16)
        dg_ref[seg] += _colsum8(dx * y_ref[...])
    tile = pl.BlockSpec((TMR, dm), lambda i: (i, 0))
    modspec = pl.BlockSpec((2, 6, dm), lambda i: (0, 0, 0))
    return pl.pallas_call(
        body, name="ffn_up_bwd", grid=(R // TMR,),
        in_specs=[*_halo_specs(w2, R), pl.BlockSpec((3, w2), lambda i: (0, 0)),
                  pl.BlockSpec((N_CHIP, kb, nb), lambda i: (0, 0, 0), pipeline_mode=pl.Buffered(1)),
                  tile, pl.BlockSpec((1, dm), lambda i: (0, 0)), modspec, tile, tile, modspec],
        out_specs=[pl.BlockSpec((TMR, w2), lambda i: (i, 0)), tile, tile,
                   pl.BlockSpec((2, 2, 8, dm), lambda i: (0, 0, 0, 0)), pl.BlockSpec((8, dm), lambda i: (0, 0)),
                   pl.BlockSpec((2, 8, dm), lambda i: (0, 0, 0))],
        out_shape=[jax.ShapeDtypeStruct((R, w2), BF16), jax.ShapeDtypeStruct((R, dm), F32),
                   jax.ShapeDtypeStruct((R, dm), BF16), jax.ShapeDtypeStruct((2, 2, 8, dm), F32),
                   jax.ShapeDtypeStruct((8, dm), F32), jax.ShapeDtypeStruct((2, 8, dm), F32)],
        compiler_params=_params(48))(duc, duc, duc, wc, w_up, x, nw, mod_n, dxr, y, mod_g)


_CW = 128


def _mixer_a(prefs, wa_ref, live):
    cin = _ext(prefs, A_W, A_W, live) * _ext(prefs, 2 * A_W, A_W, live)
    ca = (wa_ref[pl.ds(0, 1), :] * _at(cin, -1) + wa_ref[pl.ds(1, 1), :] * _at(cin, 0)
          + wa_ref[pl.ds(2, 1), :] * _at(cin, 1))
    return cin, ca


def _mixer_b(prefs, wb_ref, bias_ref, live, ub_s, ub2_s):
    for cc in range(A_W // _CW):
        c0 = cc * _CW
        ub = _ext(prefs, 3 * A_W + c0, _CW, live) * _sigmoid(_ext(prefs, 4 * A_W + c0, _CW, live))
        ub_s[:, c0:c0 + _CW] = ub
        acc = jnp.zeros((TMR, _CW), F32) + bias_ref[:, c0:c0 + _CW]
        for k in range(B_CONV):
            acc = acc + wb_ref[pl.ds(k, 1), c0:c0 + _CW] * _at(ub, k - B_CONV // 2)
        ub2_s[:, c0:c0 + _CW] = acc


def _layernorm_stats(v):
    mu = jnp.mean(v, axis=-1, keepdims=True)
    xc = v - mu
    rs = lax.rsqrt(jnp.mean(xc * xc, axis=-1, keepdims=True) + EPS)
    return xc * rs, rs


def mixer_fwd(p, wa, wb, bias, lng, lnb, w_out, x, nw, mod, kg, kn, T):
    R, wp = p.shape
    dm = x.shape[1]
    _, kb, nb = w_out.shape

    def body(pp_ref, p_ref, pn_ref, wa_ref, wb_ref, bias_ref, lng_ref, lnb_ref, w_ref, x_ref, nw_ref, mod_ref,
             o_ref, y_ref, xo_ref, h_ref, ub_s, ub2_s):
        i = pl.program_id(0)
        seg = _seg(i, T)
        live = _halo_live(i, T, R)
        prefs = (pp_ref, p_ref, pn_ref)
        _, ca = _mixer_a(prefs, wa_ref, live)
        ya = (p_ref[:, 0:A_W].astype(F32) * ca).astype(BF16)
        o_ref[:, 0:A_W] = ya
        yv = jnp.dot(ya, w_ref[0:A_W, :], preferred_element_type=F32)
        _mixer_b(prefs, wb_ref, bias_ref, live, ub_s, ub2_s)
        xh, _ = _layernorm_stats(ub2_s[...])
        lv = xh * lng_ref[...] + lnb_ref[...]
        yb = (lv * _sigmoid(lv)).astype(BF16)
        o_ref[:, A_W:2 * A_W] = yb
        yv = yv + jnp.dot(yb, w_ref[A_W:2 * A_W, :], preferred_element_type=F32)
        y_ref[...] = yv
        xv = x_ref[...] + mod_ref[seg, pl.ds(kg, 1), :] * yv
        xo_ref[...] = xv
        r = lax.rsqrt(jnp.mean(xv * xv, axis=-1, keepdims=True) + EPS)
        h_ref[...] = ((xv * r * nw_ref[...]) * (1.0 + mod_ref[seg, pl.ds(kn + 1, 1), :])
                      + mod_ref[seg, pl.ds(kn, 1), :]).astype(BF16)
    vec = pl.BlockSpec((1, A_W), lambda i: (0, 0))
    tile = pl.BlockSpec((TMR, dm), lambda i: (i, 0))
    return pl.pallas_call(
        body, name="mixer_fwd", grid=(R // TMR,),
        in_specs=[*_halo_specs(wp, R), pl.BlockSpec((3, A_W), lambda i: (0, 0)),
                  pl.BlockSpec((B_CONV, A_W), lambda i: (0, 0)), vec, vec, vec,
                  pl.BlockSpec((N_CHIP * kb, nb), lambda i: (0, 0), pipeline_mode=pl.Buffered(1)),
                  tile, pl.BlockSpec((1, dm), lambda i: (0, 0)), pl.BlockSpec((2, 6, dm), lambda i: (0, 0, 0))],
        out_specs=[pl.BlockSpec((TMR, 2 * A_W), lambda i: (i, 0)), tile, tile, tile],
        out_shape=[jax.ShapeDtypeStruct((R, 2 * A_W), BF16), jax.ShapeDtypeStruct((R, dm), F32),
                   jax.ShapeDtypeStruct((R, dm), F32), jax.ShapeDtypeStruct((R, dm), BF16)],
        scratch_shapes=[pltpu.VMEM((TMR + 2 * HALO, A_W), F32), pltpu.VMEM((TMR, A_W), F32)],
        compiler_params=_params(48))(p, p, p, wa, wb, bias, lng, lnb, w_out.reshape(N_CHIP * kb, nb), x, nw, mod)


def convmix_bwd1(dyab, p, wa, wb, bias, lng, lnb, T):
    R, wp = p.shape

    def body(dy_ref, pp_ref, p_ref, pn_ref, wa_ref, wb_ref, bias_ref, lng_ref, lnb_ref,
             dmid_ref, dwa_ref, dwb_ref, dvec_ref, ub_s, ub2_s):
        i = pl.program_id(0)
        live = _halo_live(i, T, R)

        @pl.when(i == 0)
        def _():
            dwa_ref[...] = jnp.zeros_like(dwa_ref)
            dwb_ref[...] = jnp.zeros_like(dwb_ref)
            dvec_ref[...] = jnp.zeros_like(dvec_ref)
        prefs = (pp_ref, p_ref, pn_ref)
        cin, ca = _mixer_a(prefs, wa_ref, live)
        dya = dy_ref[:, 0:A_W]
        dmid_ref[:, 0:A_W] = dya * ca
        dca = dya * p_ref[:, 0:A_W].astype(F32)
        dmid_ref[:, A_W:2 * A_W] = dca
        for k in range(3):
            dwa_ref[k] += _colsum8(dca * _at(cin, k - 1))
        _mixer_b(prefs, wb_ref, bias_ref, live, ub_s, ub2_s)
        xh, rs = _layernorm_stats(ub2_s[...])
        gain = lng_ref[...]
        lv = xh * gain + lnb_ref[...]
        sl = _sigmoid(lv)
        dl = dy_ref[:, A_W:2 * A_W] * (sl * (1.0 + lv * (1.0 - sl)))
        dvec_ref[1] += _colsum8(dl * xh)
        dvec_ref[2] += _colsum8(dl)
        dxh = dl * gain
        dub2 = rs * (dxh - jnp.mean(dxh, axis=-1, keepdims=True)
                     - xh * jnp.mean(dxh * xh, axis=-1, keepdims=True))
        dvec_ref[0] += _colsum8(dub2)
        dmid_ref[:, 2 * A_W:3 * A_W] = dub2
        for cc in range(A_W // _CW):
            c0 = cc * _CW
            ub = ub_s[:, c0:c0 + _CW]
            d = dmid_ref[:, 2 * A_W + c0:2 * A_W + c0 + _CW]
            for k in range(B_CONV):
                dwb_ref[k, :, c0:c0 + _CW] += _colsum8(d * _at(ub, k - B_CONV // 2))
    vec = pl.BlockSpec((1, A_W), lambda i: (0, 0))
    return pl.pallas_call(
        body, name="convmix_bwd1", grid=(R // TMR,),
        in_specs=[pl.BlockSpec((TMR, 2 * A_W), lambda i: (i, 0)), *_halo_specs(wp, R),
                  pl.BlockSpec((3, A_W), lambda i: (0, 0)), pl.BlockSpec((B_CONV, A_W), lambda i: (0, 0)),
                  vec, vec, vec],
        out_specs=[pl.BlockSpec((TMR, 3 * A_W), lambda i: (i, 0)),
                   pl.BlockSpec((3, 8, A_W), lambda i: (0, 0, 0)),
                   pl.BlockSpec((B_CONV, 8, A_W), lambda i: (0, 0, 0)),
                   pl.BlockSpec((3, 8, A_W), lambda i: (0, 0, 0))],
        out_shape=[jax.ShapeDtypeStruct((R, 3 * A_W), F32), jax.ShapeDtypeStruct((3, 8, A_W), F32),
                   jax.ShapeDtypeStruct((B_CONV, 8, A_W), F32), jax.ShapeDtypeStruct((3, 8, A_W), F32)],
        scratch_shapes=[pltpu.VMEM((TMR + 2 * HALO, A_W), F32), pltpu.VMEM((TMR, A_W), F32)],
        compiler_params=_params(48))(dyab, p, p, p, wa, wb, bias, lng, lnb)


def mixer_in_bwd(dmid, p, wa, wb, w_in, x, nw, mod_n, dxr, y, mod_g, w_dz, kn, kg, T):
    R, wp = p.shape
    dm = x.shape[1]
    _, kb, nb = w_in.shape
    with_resid = y is not None
    contract_last = (((1,), (1,)), ((), ()))

    def body(*refs):
        if with_resid:
            (mp_ref, m_ref, mn_ref, p_ref, wa_ref, wb_ref, w_ref, x_ref, nw_ref, mnorm_ref, dxr_ref, y_ref, mg_ref,
             wdz_ref, dp_ref, dx_ref, dy_ref, dz_ref, dmod_ref, dnw_ref, dg_ref) = refs
        else:
            (mp_ref, m_ref, mn_ref, p_ref, wa_ref, wb_ref, w_ref, x_ref, nw_ref, mnorm_ref, dxr_ref,
             dp_ref, dx_ref, dmod_ref, dnw_ref) = refs
        i = pl.program_id(0)
        seg = _seg(i, T)
        live = _halo_live(i, T, R)

        @pl.when(i == 0)
        def _():
            dmod_ref[...] = jnp.zeros_like(dmod_ref)
            dnw_ref[...] = jnp.zeros_like(dnw_ref)
            if with_resid:
                dg_ref[...] = jnp.zeros_like(dg_ref)

        def block(q):
            return lax.dot_general(dp_ref[:, q * nb:(q + 1) * nb], w_ref[q], contract_last,
                                   preferred_element_type=F32)
        mrefs = (mp_ref, m_ref, mn_ref)
        dp_ref[:, 0:A_W] = m_ref[:, 0:A_W].astype(BF16)
        dca = _ext(mrefs, A_W, A_W, live)
        dcin = (wa_ref[pl.ds(0, 1), :] * _at(dca, 1) + wa_ref[pl.ds(1, 1), :] * _at(dca, 0)
                + wa_ref[pl.ds(2, 1), :] * _at(dca, -1))
        dp_ref[:, A_W:2 * A_W] = (dcin * p_ref[:, 2 * A_W:3 * A_W].astype(F32)).astype(BF16)
        dp_ref[:, 2 * A_W:3 * A_W] = (dcin * p_ref[:, A_W:2 * A_W].astype(F32)).astype(BF16)
        dhv = block(0) + block(1)
        for cc in range(A_W // _CW):
            c0 = cc * _CW
            d = _ext(mrefs, 2 * A_W + c0, _CW, live)
            dub = jnp.zeros((TMR, _CW), F32)
            for k in range(B_CONV):
                dub = dub + wb_ref[pl.ds(k, 1), c0:c0 + _CW] * _at(d, B_CONV // 2 - k)
            vb = p_ref[:, 3 * A_W + c0:3 * A_W + c0 + _CW].astype(F32)
            s = _sigmoid(p_ref[:, 4 * A_W + c0:4 * A_W + c0 + _CW].astype(F32))
            dp_ref[:, 3 * A_W + c0:3 * A_W + c0 + _CW] = (dub * s).astype(BF16)
            dp_ref[:, 4 * A_W + c0:4 * A_W + c0 + _CW] = (dub * vb * s * (1.0 - s)).astype(BF16)
        dhv = dhv + block(2) + block(3)
        sc = mnorm_ref[seg, pl.ds(kn + 1, 1), :]
        nwv = nw_ref[...]
        xv = x_ref[...]
        r = lax.rsqrt(jnp.mean(xv * xv, axis=-1, keepdims=True) + EPS)
        xh = xv * r
        dmod_ref[seg, 0] += _colsum8(dhv)
        dmod_ref[seg, 1] += _colsum8(dhv * (xh * nwv))
        dn = dhv * (1.0 + sc)
        dnw_ref[...] += _colsum8(dn * xh)
        dxh = dn * nwv
        dx = dxr_ref[...] + r * (dxh - xh * jnp.mean(dxh * xh, axis=-1, keepdims=True))
        dx_ref[...] = dx
        if with_resid:
            dyv = (mg_ref[seg, pl.ds(kg, 1), :] * dx).astype(BF16)
            dy_ref[...] = dyv
            dz_ref[...] = lax.dot_general(dyv, wdz_ref[...], contract_last, preferred_element_type=F32).astype(BF16)
            dg_ref[seg] += _colsum8(dx * y_ref[...])
    assert 2 * nb <= 3 * A_W and N_CHIP * nb == wp
    tile = pl.BlockSpec((TMR, dm), lambda i: (i, 0))
    modspec = pl.BlockSpec((2, 6, dm), lambda i: (0, 0, 0))
    in_specs = [*_halo_specs(3 * A_W, R), pl.BlockSpec((TMR, wp), lambda i: (i, 0)),
                pl.BlockSpec((3, A_W), lambda i: (0, 0)), pl.BlockSpec((B_CONV, A_W), lambda i: (0, 0)),
                pl.BlockSpec((N_CHIP, kb, nb), lambda i: (0, 0, 0), pipeline_mode=pl.Buffered(1)),
                tile, pl.BlockSpec((1, dm), lambda i: (0, 0)), modspec, tile]
    dp_spec = pl.BlockSpec((TMR, wp), lambda i: (i, 0))
    acc_specs = [pl.BlockSpec((2, 2, 8, dm), lambda i: (0, 0, 0, 0)), pl.BlockSpec((8, dm), lambda i: (0, 0))]
    acc_shapes = [jax.ShapeDtypeStruct((2, 2, 8, dm), F32), jax.ShapeDtypeStruct((8, dm), F32)]
    dp_shape, dx_shape = jax.ShapeDtypeStruct((R, wp), BF16), jax.ShapeDtypeStruct((R, dm), F32)
    if with_resid:
        wdz = w_dz.reshape(-1, w_dz.shape[-1])
        return pl.pallas_call(
            body, name="mixer_in_bwd", grid=(R // TMR,),
            in_specs=in_specs + [tile, modspec, pl.BlockSpec(wdz.shape, lambda i: (0, 0), pipeline_mode=pl.Buffered(1))],
            out_specs=[dp_spec, tile, tile, pl.BlockSpec((TMR, wdz.shape[0]), lambda i: (i, 0))] + acc_specs
            + [pl.BlockSpec((2, 8, dm), lambda i: (0, 0, 0))],
            out_shape=[dp_shape, dx_shape, jax.ShapeDtypeStruct((R, dm), BF16),
                       jax.ShapeDtypeStruct((R, wdz.shape[0]), BF16)] + acc_shapes
            + [jax.ShapeDtypeStruct((2, 8, dm), F32)],
            compiler_params=_params(48))(dmid, dmid, dmid, p, wa, wb, w_in, x, nw, mod_n, dxr, y, mod_g, wdz)
    return pl.pallas_call(
        body, name="mixer_in_bwd_first", grid=(R // TMR,), in_specs=in_specs,
        out_specs=[dp_spec, tile] + acc_specs, out_shape=[dp_shape, dx_shape] + acc_shapes,
        compiler_params=_params(48))(dmid, dmid, dmid, p, wa, wb, w_in, x, nw, mod_n, dxr)


def _rot_half(v):
    w = v.shape[-1]
    lane = lax.broadcasted_iota(jnp.int32, (1, w), 1)
    return jnp.where(lane % HEAD_DIM < HEAD_DIM // 2, pltpu.roll(v, w - HEAD_DIM // 2, 1),
                     pltpu.roll(v, HEAD_DIM // 2, 1))


def mm_qkv_rope(a, w, cs, sn):
    R = a.shape[0]
    _, kb, nb = w.shape
    wq = N_CHIP * nb
    tm = _row_tile(R)
    qw = N_HEADS * HEAD_DIM
    kw = (wq - qw) // 2
    scale = HEAD_DIM ** -0.5

    def body(a_ref, w_ref, cs_ref, sn_ref, o_ref, x_ref):
        av = a_ref[...].astype(BF16)
        for q in range(N_CHIP):
            x_ref[:, q * nb:(q + 1) * nb] = jnp.dot(av, w_ref[q], preferred_element_type=F32)
        c, s = cs_ref[...], sn_ref[...]
        q = x_ref[:, 0:qw]
        o_ref[:, 0:qw] = ((q * jnp.tile(c, (1, qw // 128)) + _rot_half(q) * jnp.tile(s, (1, qw // 128)))
                          * scale).astype(BF16)
        k = x_ref[:, qw:qw + kw]
        o_ref[:, qw:qw + kw] = (k * jnp.tile(c, (1, kw // 128))
                                + _rot_half(k) * jnp.tile(s, (1, kw // 128))).astype(BF16)
        o_ref[:, qw + kw:] = x_ref[:, qw + kw:].astype(BF16)
    tab = pl.BlockSpec((tm, 128), lambda i: (i, 0))
    return pl.pallas_call(
        body, name="mm_qkv_rope", grid=(R // tm,),
        in_specs=[pl.BlockSpec((tm, kb), lambda i: (i, 0)),
                  pl.BlockSpec((N_CHIP, kb, nb), lambda i: (0, 0, 0), pipeline_mode=pl.Buffered(1)), tab, tab],
        out_specs=pl.BlockSpec((tm, wq), lambda i: (i, 0)),
        out_shape=jax.ShapeDtypeStruct((R, wq), BF16), scratch_shapes=[pltpu.VMEM((tm, wq), F32)],
        compiler_params=_params(48))(a, w, cs, sn)


def attn_in_bwd(dq, dks, dvs, dkc, dvc, cs, sn, w, x, nw, mod_n, dxr, y, mod_g, w_dz, kn, kg, T):
    wdz = w_dz.reshape(-1, w_dz.shape[-1])
    R, qw = dq.shape
    kw = dkc.shape[1]
    dm = x.shape[1]
    _, kb, nbw = w.shape
    nb = R // QB
    nl = T // QB
    scale = HEAD_DIM ** -0.5
    contract_last = (((1,), (1,)), ((), ()))

    def body(dq_ref, kp_ref, ko_ref, kn_ref, vp_ref, vo_ref, vn_ref, kc_ref, vc_ref, cs_ref, sn_ref,
             w_ref, x_ref, nw_ref, mnorm_ref, dxr_ref, y_ref, mg_ref, wdz_ref,
             o_ref, dx_ref, dy_ref, dz_ref, dmod_ref, dnw_ref, dg_ref):
        b = pl.program_id(0)
        seg = (b >= nl).astype(jnp.int32)

        @pl.when(b == 0)
        def _():
            dmod_ref[...] = jnp.zeros_like(dmod_ref)
            dnw_ref[...] = jnp.zeros_like(dnw_ref)
            dg_ref[...] = jnp.zeros_like(dg_ref)
        c, s = cs_ref[...], sn_ref[...]
        has_next = (b + 1 < nb).astype(F32)
        has_prev = (b >= 1).astype(F32)
        is_ctx = (b >= nl).astype(F32)
        g = dq_ref[...] * scale
        o_ref[:, 0:qw] = (g * jnp.tile(c, (1, qw // 128)) + _rot_half(g * jnp.tile(s, (1, qw // 128)))).astype(BF16)
        g = ko_ref[...] + kp_ref[...] * has_next + kn_ref[...] * has_prev + kc_ref[...] * is_ctx
        o_ref[:, qw:qw + kw] = (g * jnp.tile(c, (1, kw // 128))
                                + _rot_half(g * jnp.tile(s, (1, kw // 128)))).astype(BF16)
        o_ref[:, qw + kw:] = (vo_ref[...] + vp_ref[...] * has_next + vn_ref[...] * has_prev
                              + vc_ref[...] * is_ctx).astype(BF16)
        dhv = None
        for q in range(N_CHIP):
            t = lax.dot_general(o_ref[:, q * nbw:(q + 1) * nbw], w_ref[q], contract_last,
                                preferred_element_type=F32)
            dhv = t if dhv is None else dhv + t
        sc = mnorm_ref[seg, pl.ds(kn + 1, 1), :]
        nwv = nw_ref[...]
        xv = x_ref[...]
        r = lax.rsqrt(jnp.mean(xv * xv, axis=-1, keepdims=True) + EPS)
        xh = xv * r
        dmod_ref[seg, 0] += _colsum8(dhv)
        dmod_ref[seg, 1] += _colsum8(dhv * (xh * nwv))
        dn = dhv * (1.0 + sc)
        dnw_ref[...] += _colsum8(dn * xh)
        dxh = dn * nwv
        dx = dxr_ref[...] + r * (dxh - xh * jnp.mean(dxh * xh, axis=-1, keepdims=True))
        dx_ref[...] = dx
        dyv = (mg_ref[seg, pl.ds(kg, 1), :] * dx).astype(BF16)
        dy_ref[...] = dyv
        dz_ref[...] = lax.dot_general(dyv, wdz_ref[...], contract_last, preferred_element_type=F32).astype(BF16)
        dg_ref[seg] += _colsum8(dx * y_ref[...])
    own = pl.BlockSpec((QB, kw), lambda b: (b, 0))
    from_next = pl.BlockSpec((QB, kw), lambda b: (jnp.minimum(b + 1, nb - 1), 0))
    from_prev = pl.BlockSpec((QB, kw), lambda b: (jnp.maximum(b - 1, 0), 0))
    ctx = pl.BlockSpec((QB, kw), lambda b: (jnp.maximum(b - nl, 0), 0))
    tab = pl.BlockSpec((QB, 128), lambda b: (b, 0))
    tile = pl.BlockSpec((QB, dm), lambda b: (b, 0))
    modspec = pl.BlockSpec((2, 6, dm), lambda b: (0, 0, 0))
    return pl.pallas_call(
        body, name="attn_in_bwd", grid=(nb,),
        in_specs=[pl.BlockSpec((QB, qw), lambda b: (b, 0)), from_next, own, from_prev, from_next, own, from_prev,
                  ctx, ctx, tab, tab,
                  pl.BlockSpec((N_CHIP, kb, nbw), lambda b: (0, 0, 0), pipeline_mode=pl.Buffered(1)),
                  tile, pl.BlockSpec((1, dm), lambda b: (0, 0)), modspec, tile, tile, modspec,
                  pl.BlockSpec(wdz.shape, lambda b: (0, 0), pipeline_mode=pl.Buffered(1))],
        out_specs=[pl.BlockSpec((QB, qw + 2 * kw), lambda b: (b, 0)), tile, tile,
                   pl.BlockSpec((QB, wdz.shape[0]), lambda b: (b, 0)),
                   pl.BlockSpec((2, 2, 8, dm), lambda b: (0, 0, 0, 0)), pl.BlockSpec((8, dm), lambda b: (0, 0)),
                   pl.BlockSpec((2, 8, dm), lambda b: (0, 0, 0))],
        out_shape=[jax.ShapeDtypeStruct((R, qw + 2 * kw), BF16), jax.ShapeDtypeStruct((R, dm), F32),
                   jax.ShapeDtypeStruct((R, dm), BF16), jax.ShapeDtypeStruct((R, wdz.shape[0]), BF16),
                   jax.ShapeDtypeStruct((2, 2, 8, dm), F32), jax.ShapeDtypeStruct((8, dm), F32),
                   jax.ShapeDtypeStruct((2, 8, dm), F32)],
        compiler_params=_params(48))(
            dq, dks[0], dks[1], dks[2], dvs[0], dvs[1], dvs[2], dkc, dvc, cs, sn, w, x, nw, mod_n, dxr, y, mod_g, wdz)


def _attn_specs(T, R):
    nl = T // QB
    qcols = N_HEADS * HEAD_DIM // 128
    kcols = 2

    def band(col0, shift):
        return pl.BlockSpec((QB, 128), lambda jj, b: (jnp.clip(b + shift, 0, nl - 1), col0 + jj))

    def ctx(col0):
        return pl.BlockSpec((R - T, 128), lambda jj, b: (T // (R - T), col0 + jj))
    q = pl.BlockSpec((QB, 512), lambda jj, b: (b, jj))
    k0, v0 = qcols, qcols + kcols
    return q, [band(k0, -1), band(k0, 0), band(k0, 1), ctx(k0)], [band(v0, -1), band(v0, 0), band(v0, 1), ctx(v0)]


def _attn_common(T, R):
    nl = T // QB
    nk = 3 * QB + (R - T)

    def low_lanes():
        return lax.broadcasted_iota(jnp.int32, (1, 128), 1) < HEAD_DIM

    def dup(v, par):
        low = low_lanes()
        vf = v.astype(F32)
        r = pltpu.roll(vf, HEAD_DIM, 1)
        return (jnp.where(low, vf, r) if par == 0 else jnp.where(low, r, vf)).astype(BF16)

    def stack(ref, par):
        low = low_lanes()
        pa = ref[:, (2 * par) * 128:(2 * par + 1) * 128].astype(BF16)
        pb = ref[:, (2 * par + 1) * 128:(2 * par + 2) * 128].astype(BF16)
        zero = jnp.zeros_like(pa)
        return jnp.concatenate([jnp.where(low, pa, zero), jnp.where(low, zero, pa),
                                jnp.where(low, pb, zero), jnp.where(low, zero, pb)], axis=0)

    def unstack(v):
        low = low_lanes()
        return (jnp.where(low, v[0:QB], v[QB:2 * QB]), jnp.where(low, v[2 * QB:3 * QB], v[3 * QB:4 * QB]))

    def mask_of(b):
        col = lax.broadcasted_iota(jnp.int32, (1, nk), 1)
        gone = (((col < QB) & (b == 0)) | ((col >= 2 * QB) & (col < 3 * QB) & (b == nl - 1))
                | ((col < 3 * QB) & (b >= nl)))
        return jnp.where(gone, NEG_INF, 0.0)

    def sink_col(sink_ref, first):
        blk = lax.broadcasted_iota(jnp.int32, (4 * QB, 1), 0) // QB
        out = jnp.zeros((4 * QB, 1), F32) + sink_ref[first]
        for h in range(1, 4):
            out = jnp.where(blk == h, sink_ref[first + h], out)
        return out

    def scores(qs, kd, mask, sink):
        s = lax.dot_general(qs, kd, (((1,), (1,)), ((), ())), preferred_element_type=F32) + mask
        m = jnp.maximum(jnp.max(s, axis=-1, keepdims=True), sink)
        e = jnp.exp(s - m)
        es = jnp.exp(sink - m)
        return e, es, 1.0 / (jnp.sum(e, axis=-1, keepdims=True) + es)
    return low_lanes, dup, stack, unstack, mask_of, sink_col, scores


def window_bias(T, R):
    nk = 3 * QB + (R - T)
    row = jnp.arange(QB)[:, None]
    col = jnp.arange(nk)[None, :]
    near = (jnp.abs(col - QB - row) <= WINDOW) | (col >= 3 * QB)
    return jnp.tile(jnp.where(near, 0.0, NEG_INF).astype(F32), (4, 1))


def _probs_spec(nk):
    return pl.BlockSpec((None, None, 2, 4 * QB, nk + 128), lambda jj, b: (jj, b, 0, 0, 0))


def attn_fwd(qkvr, sinks, bias, T):
    R = qkvr.shape[0]
    nk = bias.shape[1]
    qspec, kspecs, vspecs = _attn_specs(T, R)
    _, dup, stack, unstack, mask_of, sink_col, scores = _attn_common(T, R)

    def body(q_ref, kp, ko, kn, kc, vp, vo, vn, vc, sink_ref, bias_ref, o_ref, p_ref):
        jj, b = pl.program_id(0), pl.program_id(1)
        mask = bias_ref[...] + mask_of(b)
        k_all = jnp.concatenate([kp[...], ko[...], kn[...], kc[...]], axis=0)
        v_all = jnp.concatenate([vp[...], vo[...], vn[...], vc[...]], axis=0)
        for par in range(2):
            kd, vd = dup(k_all, par), dup(v_all, par)
            e, es, rz = scores(stack(q_ref, par), kd, mask, sink_col(sink_ref, jj * 8 + par * 4))
            p = (e * rz).astype(BF16)
            p_ref[par, :, 0:nk] = p
            p_ref[par, :, nk:nk + 128] = jnp.broadcast_to(es * rz, (4 * QB, 128)).astype(BF16)
            o = jnp.dot(p, vd, preferred_element_type=F32)
            pa, pb = unstack(o)
            o_ref[:, (2 * par) * 128:(2 * par + 1) * 128] = pa.astype(BF16)
            o_ref[:, (2 * par + 1) * 128:(2 * par + 2) * 128] = pb.astype(BF16)
    return pl.pallas_call(
        body, name="attn_fwd", grid=(2, R // QB),
        in_specs=[qspec, *kspecs, *vspecs, pl.BlockSpec(memory_space=pltpu.SMEM),
                  pl.BlockSpec(bias.shape, lambda jj, b: (0, 0))],
        out_specs=[pl.BlockSpec((QB, 512), lambda jj, b: (b, jj)), _probs_spec(nk)],
        out_shape=[jax.ShapeDtypeStruct((R, N_HEADS * HEAD_DIM), BF16),
                   jax.ShapeDtypeStruct((2, R // QB, 2, 4 * QB, nk + 128), BF16)],
        compiler_params=_params(48))(qkvr, *([qkvr] * 8), sinks, bias)


def attn_bwd(qkvr, do, probs, T):
    R = qkvr.shape[0]
    tc = R - T
    nk = probs.shape[-1] - 128
    qspec, kspecs, vspecs = _attn_specs(T, R)
    _, dup, stack, unstack, _, _, _ = _attn_common(T, R)
    contract_rows = (((0,), (0,)), ((), ()))
    contract_last = (((1,), (1,)), ((), ()))

    def body(q_ref, kp, ko, kn, kc, vp, vo, vn, vc, do_ref, p_ref,
             dq_ref, dkp, dko, dkn, dvp, dvo, dvn, dkc_ref, dvc_ref, dsink_ref):
        jj, b = pl.program_id(0), pl.program_id(1)

        @pl.when((jj == 0) & (b == 0))
        def _():
            dsink_ref[...] = jnp.zeros_like(dsink_ref)

        @pl.when(b == 0)
        def _():
            dkc_ref[...] = jnp.zeros_like(dkc_ref)
            dvc_ref[...] = jnp.zeros_like(dvc_ref)
        k_all = jnp.concatenate([kp[...], ko[...], kn[...], kc[...]], axis=0)
        v_all = jnp.concatenate([vp[...], vo[...], vn[...], vc[...]], axis=0)
        lane = lax.broadcasted_iota(jnp.int32, (8, 128), 1)
        srow = lax.broadcasted_iota(jnp.int32, (8, 128), 0)
        dk_fold, dv_fold = [], []
        for par in range(2):
            kd, vd = dup(k_all, par), dup(v_all, par)
            first = jj * 8 + par * 4
            qs, dos = stack(q_ref, par), stack(do_ref, par)
            p16 = p_ref[par, :, 0:nk]
            p = p16.astype(F32)
            ps = jnp.max(p_ref[par, :, nk:nk + 128].astype(F32), axis=-1, keepdims=True)
            dp = lax.dot_general(dos, vd, contract_last, preferred_element_type=F32)
            delta = jnp.sum(p * dp, axis=-1, keepdims=True)
            ds = (p * (dp - delta)).astype(BF16)
            t = ps * delta
            for h in range(4):
                dsink = -jnp.sum(t[h * QB:(h + 1) * QB])
                dsink_ref[...] += jnp.where((lane == first + h) & (srow == 0), dsink, 0.0)
            pa, pb = unstack(jnp.dot(ds, kd, preferred_element_type=F32))
            dq_ref[:, (2 * par) * 128:(2 * par + 1) * 128] = pa
            dq_ref[:, (2 * par + 1) * 128:(2 * par + 2) * 128] = pb
            dk_t = lax.dot_general(qs, ds, contract_rows, preferred_element_type=F32)
            dv_t = lax.dot_general(dos, p16, contract_rows, preferred_element_type=F32)
            dk_fold.append(dk_t + pltpu.roll(dk_t, HEAD_DIM, 0))
            dv_fold.append(dv_t + pltpu.roll(dv_t, HEAD_DIM, 0))
        low_rows = lax.broadcasted_iota(jnp.int32, (128, 1), 0) < HEAD_DIM
        dk = jnp.where(low_rows, dk_fold[0], dk_fold[1]).T
        dv = jnp.where(low_rows, dv_fold[0], dv_fold[1]).T
        dkp[...], dko[...], dkn[...] = dk[0:QB], dk[QB:2 * QB], dk[2 * QB:3 * QB]
        dvp[...], dvo[...], dvn[...] = dv[0:QB], dv[QB:2 * QB], dv[2 * QB:3 * QB]
        dkc_ref[...] += dk[3 * QB:]
        dvc_ref[...] += dv[3 * QB:]
    blk = pl.BlockSpec((QB, 128), lambda jj, b: (b, jj))
    cblk = pl.BlockSpec((tc, 128), lambda jj, b: (0, jj))
    part = jax.ShapeDtypeStruct((R, 256), F32)
    csum = jax.ShapeDtypeStruct((tc, 256), F32)
    outs = pl.pallas_call(
        body, name="attn_bwd", grid=(2, R // QB),
        in_specs=[qspec, *kspecs, *vspecs, pl.BlockSpec((QB, 512), lambda jj, b: (b, jj)), _probs_spec(nk)],
        out_specs=[pl.BlockSpec((QB, 512), lambda jj, b: (b, jj)), blk, blk, blk, blk, blk, blk, cblk, cblk,
                   pl.BlockSpec((8, 128), lambda jj, b: (0, 0))],
        out_shape=[jax.ShapeDtypeStruct((R, N_HEADS * HEAD_DIM), F32), part, part, part, part, part, part,
                   csum, csum, jax.ShapeDtypeStruct((8, 128), F32)],
        compiler_params=_params(48))(qkvr, *([qkvr] * 8), do, probs)
    return outs[0], outs[1:4], outs[4:7], outs[7], outs[8], outs[9]


def loss_head(x, nw, target, y, mod, w_dz, kg, T):
    R, dm = x.shape
    nl = T // TMR
    wdz = w_dz.reshape(-1, w_dz.shape[-1])

    def body(x_ref, nw_ref, t_ref, y_ref, mod_ref, wdz_ref, loss_ref, dx_ref, dnw_ref, dy_ref, dz_ref, dg_ref):
        i = pl.program_id(0)
        seg = _seg(i, T)

        @pl.when(i == 0)
        def _():
            loss_ref[...] = jnp.zeros_like(loss_ref)
            dnw_ref[...] = jnp.zeros_like(dnw_ref)
            dg_ref[...] = jnp.zeros_like(dg_ref)
        live = (i < nl).astype(F32)
        nwv = nw_ref[...]
        xv = x_ref[...]
        r = lax.rsqrt(jnp.mean(xv * xv, axis=-1, keepdims=True) + EPS)
        xh = xv * r
        err = xh * nwv - t_ref[...]
        per_row = jnp.mean(err * err, axis=-1, keepdims=True)
        loss_ref[...] += 0.5 * live * jnp.sum(per_row)
        dy = err * (live / dm)
        dnw_ref[...] += _colsum8(dy * xh)
        dxh = dy * nwv
        dx = r * (dxh - xh * jnp.mean(dxh * xh, axis=-1, keepdims=True))
        dx_ref[...] = dx
        dyv = (mod_ref[seg, pl.ds(kg, 1), :] * dx).astype(BF16)
        dy_ref[...] = dyv
        dz_ref[...] = lax.dot_general(dyv, wdz_ref[...], (((1,), (1,)), ((), ())),
                                      preferred_element_type=F32).astype(BF16)
        dg_ref[seg] += _colsum8(dx * y_ref[...])
    tile = pl.BlockSpec((TMR, dm), lambda i: (i, 0))
    return pl.pallas_call(
        body, name="loss_head", grid=(R // TMR,),
        in_specs=[tile, pl.BlockSpec((1, dm), lambda i: (0, 0)),
                  pl.BlockSpec((TMR, dm), lambda i: (jnp.minimum(i, nl - 1), 0)), tile,
                  pl.BlockSpec((2, 6, dm), lambda i: (0, 0, 0)),
                  pl.BlockSpec(wdz.shape, lambda i: (0, 0), pipeline_mode=pl.Buffered(1))],
        out_specs=[pl.BlockSpec((8, 128), lambda i: (0, 0)), tile, pl.BlockSpec((8, dm), lambda i: (0, 0)), tile,
                   pl.BlockSpec((TMR, wdz.shape[0]), lambda i: (i, 0)), pl.BlockSpec((2, 8, dm), lambda i: (0, 0, 0))],
        out_shape=[jax.ShapeDtypeStruct((8, 128), F32), jax.ShapeDtypeStruct((R, dm), F32),
                   jax.ShapeDtypeStruct((8, dm), F32), jax.ShapeDtypeStruct((R, dm), BF16),
                   jax.ShapeDtypeStruct((R, wdz.shape[0]), BF16), jax.ShapeDtypeStruct((2, 8, dm), F32)],
        compiler_params=_params(48))(x, nw, target, y, mod, wdz)


def adaln_fwd(cond, w_mod, b_mod):
    nl, dm, ns = w_mod.shape

    def body(c_ref, w_ref, b_ref, o_ref):
        cv = c_ref[...]
        s = (cv * _sigmoid(cv)).astype(BF16)
        o_ref[...] = jnp.dot(s, w_ref[...].astype(BF16), preferred_element_type=F32) + b_ref[...]
    return pl.pallas_call(
        body, name="adaln_fwd", grid=(nl,),
        in_specs=[pl.BlockSpec((16, dm), lambda l: (0, 0)), pl.BlockSpec((None, dm, ns), lambda l: (l, 0, 0)),
                  pl.BlockSpec((None, 1, ns), lambda l: (l, 0, 0))],
        out_specs=pl.BlockSpec((None, 16, ns), lambda l: (l, 0, 0)),
        out_shape=jax.ShapeDtypeStruct((nl, 16, ns), F32), compiler_params=_params(48))(cond, w_mod, b_mod)


def adaln_bwd(cond, dmod, w_mod):
    nl, dm, ns = w_mod.shape

    def body(c_ref, d_ref, w_ref, gw_ref, ds_ref):
        l = pl.program_id(0)

        @pl.when(l == 0)
        def _():
            ds_ref[...] = jnp.zeros_like(ds_ref)
        cv = c_ref[...]
        s = (cv * _sigmoid(cv)).astype(BF16)
        dv = d_ref[...].astype(BF16)
        gw_ref[...] = lax.dot_general(s, dv, (((0,), (0,)), ((), ())), preferred_element_type=F32)
        ds_ref[...] += lax.dot_general(dv, w_ref[...].astype(BF16), (((1,), (1,)), ((), ())),
                                       preferred_element_type=F32)
    return pl.pallas_call(
        body, name="adaln_bwd", grid=(nl,),
        in_specs=[pl.BlockSpec((16, dm), lambda l: (0, 0)), pl.BlockSpec((None, 16, ns), lambda l: (l, 0, 0)),
                  pl.BlockSpec((None, dm, ns), lambda l: (l, 0, 0))],
        out_specs=[pl.BlockSpec((None, dm, ns), lambda l: (l, 0, 0)), pl.BlockSpec((16, dm), lambda l: (0, 0))],
        out_shape=[jax.ShapeDtypeStruct((nl, dm, ns), F32), jax.ShapeDtypeStruct((16, dm), F32)],
        compiler_params=_params(48))(cond, dmod, w_mod)


def _me():
    return lax.axis_index("x"), lax.axis_index("y"), lax.axis_index("c")


def allgather8(block):
    m_per, n = block.shape

    def body(x_ref, out_ref, send_sems, recv_sems, local_sem):
        x, y, c = _me()
        me, sibling = (x, y, c), (x, y, 1 - c)
        chips = [(1 - x, y), (x, 1 - y), (1 - x, 1 - y)]

        def rows(px, py, pc):
            return out_ref.at[pl.ds((4 * px + 2 * py + pc) * m_per, m_per), :]

        def copy(k, blk, to, src=None):
            return pltpu.make_async_remote_copy(
                src_ref=rows(*blk) if src is None else src, dst_ref=rows(*blk),
                send_sem=send_sems.at[k], recv_sem=recv_sems.at[k], device_id=to, device_id_type=MESH)
        mine = pltpu.make_async_copy(x_ref, rows(*me), local_sem)
        mine.start()
        first = [copy(0, me, sibling, src=x_ref)]
        first += [copy(1 + j, me, (*chip, c), src=x_ref) for j, chip in enumerate(chips)]
        for cp in first:
            cp.start()
        passed = [copy(4 + j, (*chip, c), sibling) for j, chip in enumerate(chips)]
        for j, chip in enumerate(chips):
            copy(1 + j, (*chip, c), me).wait_recv()
            passed[j].start()
        copy(0, sibling, me).wait_recv()
        for j, chip in enumerate(chips):
            copy(4 + j, (*chip, 1 - c), me).wait_recv()
        for cp in first + passed:
            cp.wait_send()
        mine.wait()
    return pl.pallas_call(
        body, name="allgather8",
        out_shape=jax.ShapeDtypeStruct((N_DEV * m_per, n), block.dtype),
        in_specs=[pl.BlockSpec(memory_space=pltpu.VMEM)],
        out_specs=pl.BlockSpec(memory_space=pltpu.VMEM),
        scratch_shapes=[pltpu.SemaphoreType.DMA((7,)), pltpu.SemaphoreType.DMA((7,)), pltpu.SemaphoreType.DMA],
        compiler_params=_params(48))(block)


def _other_chips(x, y):
    return [(1 - x, y), (x, 1 - y), (1 - x, 1 - y)]


_HBM = pl.BlockSpec(memory_space=pltpu.HBM)
_SEM = pl.BlockSpec(memory_space=pltpu.SEMAPHORE)
_ANY = pl.BlockSpec(memory_space=pl.ANY)
_EFFECT = pltpu.SideEffectType.DATAFLOW_SIDE_EFFECTING


def _in_hbm(v):
    return pltpu.with_memory_space_constraint(v, pltpu.HBM)


def cast_into_slot(w, layer, chip_id):
    _, kb, nb = w.shape
    tr = _row_tile(kb)

    def body(chip_ref, w_ref, o_ref):
        del chip_ref
        o_ref[...] = w_ref[...].astype(BF16)
    return pl.pallas_call(
        body, name="cast_into_slot",
        grid_spec=pltpu.PrefetchScalarGridSpec(
            num_scalar_prefetch=1, grid=(kb // tr,),
            in_specs=[pl.BlockSpec((None, tr, nb), lambda i, chip: (layer, i, 0))],
            out_specs=pl.BlockSpec((None, tr, nb), lambda i, chip: (chip[0], i, 0))),
        out_shape=jax.ShapeDtypeStruct((N_CHIP, kb, nb), BF16))(chip_id, w)


def _split_copies(mode, srcs, lands, send_sems, recv_sems):
    x, y, c = _me()
    out = []
    for t in range(len(lands)):
        for k, chip in enumerate(_other_chips(x, y)):
            if mode == "gather":
                src = dst = lands[t].at[2 * x + y]
                landed = lands[t].at[2 * chip[0] + chip[1]]
            else:
                src, dst, landed = srcs[t].at[2 * chip[0] + chip[1]], lands[t].at[k], lands[t].at[k]
            send = pltpu.make_async_remote_copy(src_ref=src, dst_ref=dst, send_sem=send_sems.at[3 * t + k],
                                                recv_sem=recv_sems.at[3 * t + k], device_id=(*chip, c),
                                                device_id_type=MESH)
            recv = pltpu.make_async_remote_copy(src_ref=src, dst_ref=landed, send_sem=send_sems.at[3 * t + k],
                                                recv_sem=recv_sems.at[3 * t + k], device_id=(*chip, c),
                                                device_id_type=MESH)
            out.append((send, recv))
    return out


def exchange_start(name, mode, srcs, lands, after):
    ns, nl = len(srcs), len(lands)
    na = ns + nl

    def body(*refs):
        src_refs, land_refs = refs[:ns], refs[ns:na]
        send_sems, recv_sems = refs[na + 1], refs[na + 2]
        token = refs[-1]
        for send, _ in _split_copies(mode, src_refs, land_refs, send_sems, recv_sems):
            send.start()
        token[...] = jnp.zeros_like(token)
    arrays = list(srcs) + list(lands)
    outs = pl.pallas_call(
        body, name=name,
        out_shape=(pltpu.SemaphoreType.DMA((3 * nl,)), pltpu.SemaphoreType.DMA((3 * nl,)),
                   *[pltpu.HBM(v.shape, v.dtype) for v in arrays], jax.ShapeDtypeStruct((8, 128), F32)),
        in_specs=[_HBM] * na + [_ANY],
        out_specs=(_SEM, _SEM, *[_HBM] * na, pl.BlockSpec(memory_space=pltpu.VMEM)),
        input_output_aliases={i: 2 + i for i in range(na)},
        compiler_params=pltpu.CompilerParams(has_side_effects=_EFFECT))(*[_in_hbm(v) for v in arrays], after)
    return outs[0], outs[1], list(outs[2:2 + ns]), list(outs[2 + ns:2 + na]), outs[-1]


def exchange_wait(name, mode, send_sems, recv_sems, srcs, lands, after):
    ns, nl = len(srcs), len(lands)
    na = ns + nl

    def body(*refs):
        for _, recv in _split_copies(mode, refs[:ns], refs[ns:na], refs[na], refs[na + 1]):
            recv.wait_send()
            recv.wait_recv()
    arrays = list(srcs) + list(lands)
    outs = pl.pallas_call(
        body, name=name,
        out_shape=[pltpu.HBM(v.shape, v.dtype) for v in arrays],
        in_specs=[_HBM] * na + [_SEM, _SEM, _ANY], out_specs=[_HBM] * na,
        input_output_aliases={i: i for i in range(na)},
        compiler_params=pltpu.CompilerParams(has_side_effects=_EFFECT))(*arrays, send_sems, recv_sems, after)
    return list(outs[:ns]), list(outs[ns:])


def swap_with_sibling(vs):
    n = len(vs)

    def body(*refs):
        v_refs, out_refs, send_sems, recv_sems = refs[:n], refs[n:2 * n], refs[2 * n], refs[2 * n + 1]
        x, y, c = _me()
        cps = [pltpu.make_async_remote_copy(src_ref=v_refs[t], dst_ref=out_refs[t], send_sem=send_sems.at[t],
                                            recv_sem=recv_sems.at[t], device_id=(x, y, 1 - c), device_id_type=MESH)
               for t in range(n)]
        for cp in cps:
            cp.start()
        for cp in cps:
            cp.wait()
    return pl.pallas_call(
        body, name="swap_with_sibling", out_shape=[jax.ShapeDtypeStruct(v.shape, v.dtype) for v in vs],
        in_specs=[_ANY] * n, out_specs=[_ANY] * n,
        scratch_shapes=[pltpu.SemaphoreType.DMA((n,)), pltpu.SemaphoreType.DMA((n,))])(*vs)


def sum_slots(parts):
    n, rows, w = parts.shape
    tr = _row_tile(rows)

    def body(p_ref, o_ref):
        acc = p_ref[0].astype(F32)
        for k in range(1, n):
            acc = acc + p_ref[k].astype(F32)
        o_ref[...] = acc
    return pl.pallas_call(
        body, name="sum_slots", grid=(rows // tr,),
        in_specs=[pl.BlockSpec((n, tr, w), lambda i: (0, i, 0))], out_specs=pl.BlockSpec((tr, w), lambda i: (i, 0)),
        out_shape=jax.ShapeDtypeStruct((rows, w), F32), compiler_params=_params(48))(parts)


def sum_landed(landed, own, chip_id, layer, n_layers, buf):
    n, rows, w = landed.shape
    tr = _row_tile(rows)
    base = layer * (rows // tr)

    def compute(l_ref, g_ref, o_ref):
        acc = g_ref[...].astype(F32)
        for k in range(n):
            acc = acc + l_ref[k].astype(F32)
        o_ref[...] = acc
    in_specs = [pl.BlockSpec((n, tr, w), lambda i, chip: (0, i, 0)),
                pl.BlockSpec((None, tr, w), lambda i, chip: (chip[0], i, 0))]
    out_spec = pl.BlockSpec((tr, w), lambda i, chip: (base + i, 0))
    out_shape = jax.ShapeDtypeStruct((n_layers * rows, w), F32)
    if buf is None:
        def body(chip_ref, l_ref, g_ref, o_ref):
            del chip_ref
            compute(l_ref, g_ref, o_ref)
        return pl.pallas_call(
            body, name="sum_landed",
            grid_spec=pltpu.PrefetchScalarGridSpec(num_scalar_prefetch=1, grid=(rows // tr,), in_specs=in_specs,
                                                   out_specs=out_spec),
            out_shape=out_shape, compiler_params=_params(48))(chip_id, landed, own)

    def body(chip_ref, l_ref, g_ref, buf_ref, o_ref):
        del chip_ref, buf_ref
        compute(l_ref, g_ref, o_ref)
    return pl.pallas_call(
        body, name="sum_landed_into",
        grid_spec=pltpu.PrefetchScalarGridSpec(num_scalar_prefetch=1, grid=(rows // tr,),
                                               in_specs=in_specs + [_ANY], out_specs=out_spec),
        out_shape=out_shape, input_output_aliases={3: 0}, compiler_params=_params(48))(chip_id, landed, own, buf)


def adamw(w, ga, gb, m, v):
    rows, wd = w.shape
    tr = min(_row_tile(rows), 128)
    c1 = 1.0 / (1.0 - ADAM_B1 ** ADAM_STEP)
    c2 = 1.0 / (1.0 - ADAM_B2 ** ADAM_STEP)

    def update(wv, g, mv, vv, g_ref, d_ref, m_ref, v_ref):
        mn = ADAM_B1 * mv + (1.0 - ADAM_B1) * g
        vn = ADAM_B2 * vv + (1.0 - ADAM_B2) * (g * g)
        g_ref[...] = g
        m_ref[...] = mn
        v_ref[...] = vn
        d_ref[...] = -ADAM_LR * ((mn * c1) / (jnp.sqrt(vn * c2) + ADAM_EPS) + ADAM_WD * wv)
    tile = pl.BlockSpec((tr, wd), lambda i: (i, 0))
    out = jax.ShapeDtypeStruct((rows, wd), F32)
    if gb is None:
        def body(w_ref, ga_ref, m_ref, v_ref, g_out, d_out, m_out, v_out):
            update(w_ref[...], ga_ref[...], m_ref[...], v_ref[...], g_out, d_out, m_out, v_out)
        return pl.pallas_call(body, name="adamw", grid=(rows // tr,), in_specs=[tile] * 4,
                              out_specs=[tile] * 4, out_shape=[out] * 4)(w, ga, m, v)

    def body(w_ref, ga_ref, gb_ref, m_ref, v_ref, g_out, d_out, m_out, v_out):
        update(w_ref[...], ga_ref[...] + gb_ref[...], m_ref[...], v_ref[...], g_out, d_out, m_out, v_out)
    return pl.pallas_call(body, name="adamw_sum", grid=(rows // tr,), in_specs=[tile] * 5,
                          out_specs=[tile] * 4, out_shape=[out] * 4)(w, ga, gb, m, v)


def _rope_tables(T, R):
    rows = T // GRID_W
    row = jnp.repeat(jnp.arange(rows), GRID_W).astype(F32)
    col = jnp.tile(jnp.arange(GRID_W), rows).astype(F32)
    n_freq = HEAD_DIM // 4
    inv_freq = ROPE_THETA ** (-jnp.arange(n_freq, dtype=F32) / n_freq)
    ang = jnp.concatenate([row[:, None] * inv_freq, col[:, None] * inv_freq], axis=-1)
    cos, sin = jnp.cos(ang), jnp.sin(ang)
    cs = jnp.tile(cos, (1, 4))
    sn = jnp.tile(jnp.concatenate([-sin, sin], axis=-1), (1, 2))
    pad = R - T
    return (jnp.concatenate([cs, jnp.ones((pad, 128), F32)], axis=0),
            jnp.concatenate([sn, jnp.zeros((pad, 128), F32)], axis=0))


def _pack(parts, mult=8 * 128):
    flat = jnp.concatenate([p.reshape(-1).astype(F32) for p in parts])
    pad = (-flat.shape[0]) % mult
    return jnp.pad(flat, (0, pad)).reshape(-1, 128)


def _unpack(buf, shapes):
    flat = buf.reshape(-1)
    out, o = [], 0
    for s in shapes:
        n = 1
        for d in s:
            n *= d
        out.append(flat[o:o + n].reshape(s))
        o += n
    return out


def kernel(x, c, ctx, c_ctx, w_mod, b_mod, norm_mix, norm_ffn, w_in_ab, conv_a, conv_b, conv_b_bias, ln_b_gain, ln_b_bias, w_out_ab, w_qkv, w_o, sinks, w_up, w_conv_ffn, w_down, final_norm, loss_target, m_c_ctx, m_w_mod, m_b_mod, m_norm_mix, m_norm_ffn, m_w_in_ab, m_conv_a, m_conv_b, m_conv_b_bias, m_ln_b_gain, m_ln_b_bias, m_w_out_ab, m_w_qkv, m_w_o, m_sinks, m_w_up, m_w_conv_ffn, m_w_down, m_final_norm, v_c_ctx, v_w_mod, v_b_mod, v_norm_mix, v_norm_ffn, v_w_in_ab, v_conv_a, v_conv_b, v_conv_b_bias, v_ln_b_gain, v_ln_b_bias, v_w_out_ab, v_w_qkv, v_w_o, v_sinks, v_w_up, v_w_conv_ffn, v_w_down, v_final_norm):
    T, dm = x.shape[1], x.shape[2]
    tc = ctx.shape[1]
    R = T + tc
    depth = w_mod.shape[0]
    ax, ay, ac = lax.axis_index("x"), lax.axis_index("y"), lax.axis_index("c")
    chip = 2 * ax + ay
    dev = 4 * ax + 2 * ay + ac

    small_w = [conv_a, conv_b, w_conv_ffn]
    gathered = allgather8(_pack([c] + small_w)).reshape(N_DEV, -1)
    cond8 = gathered[:, :dm]
    off = dm
    full_small = []
    for wsh in small_w:
        n = wsh.size
        per_chip = gathered[0::2, off:off + n].reshape((N_CHIP,) + wsh.shape)
        full_small.append(jnp.concatenate([per_chip[q] for q in range(N_CHIP)], axis=-1))
        off += n
    conv_a_f, conv_b_f, w_conv_ffn_f = full_small
    cond = jnp.concatenate([cond8, c_ctx[None, :], jnp.zeros((7, dm), F32)], axis=0)

    ns_mod = w_mod.shape[2]
    b_mod_sh = lax.dynamic_slice_in_dim(b_mod, chip * ns_mod, ns_mod, axis=1)[:, None, :]
    mod_sh = adaln_fwd(cond, w_mod, b_mod_sh)
    mod_all = allgather8(mod_sh.reshape(depth * 16, ns_mod)).reshape(N_DEV, depth, 16, ns_mod)
    mod_full = jnp.concatenate([mod_all[2 * q] for q in range(N_CHIP)], axis=-1)
    mine = lax.dynamic_index_in_dim(mod_full, dev, axis=1, keepdims=False)
    mods = jnp.stack([mine, mod_full[:, 8]], axis=1).reshape(depth, 2, 6, dm)

    masters = {"w_in_ab": w_in_ab, "w_out_ab": w_out_ab, "w_qkv": w_qkv, "w_o": w_o, "w_up": w_up, "w_down": w_down}
    chip_id = chip.astype(jnp.int32).reshape(1)

    def half_weights(l, half):
        if half == 1:
            return [("w_up", l), ("w_down", l)]
        return [("w_in_ab", l // 2), ("w_out_ab", l // 2)] if l % 2 == 0 else [("w_qkv", l // 2), ("w_o", l // 2)]
    in_flight, after = {}, mods
    for l in range(depth):
        for half in range(2):
            lands = [cast_into_slot(masters[n], j, chip_id) for n, j in half_weights(l, half)]
            send_sems, recv_sems, _, lands, after = exchange_start(f"gather_start_{l}_{half}", "gather", [], lands, after)
            in_flight[l, half] = (send_sems, recv_sems, lands)
    mods = mods + after[0, 0]

    def gathered_weights(l, half, after):
        send_sems, recv_sems, lands = in_flight[l, half]
        _, landed = exchange_wait(f"gather_wait_{l}_{half}", "gather", send_sems, recv_sems, [], lands, after)
        return dict(zip([n for n, _ in half_weights(l, half)], landed))

    cs, sn = _rope_tables(T, R)
    bias = window_bias(T, R)
    sinks_flat = sinks.reshape(-1)

    xs = jnp.concatenate([x[0], ctx[0]], axis=0)
    saved, W = [], []
    h1 = norm_mod_fwd(xs, norm_mix[0][None], mods[0], 0, T)
    for l in range(depth):
        e = l // 2
        wl = gathered_weights(l, 0, h1)
        W.append(wl)
        s = {"x0": xs, "h1": h1}
        if l % 2 == 0:
            p = mm_nn(h1, wl["w_in_ab"], BF16)
            yab, y1, x1, h2 = mixer_fwd(p, conv_a_f[e], conv_b_f[e], conv_b_bias[e][None], ln_b_gain[e][None],
                                        ln_b_bias[e][None], wl["w_out_ab"], xs, norm_ffn[l][None], mods[l], 2, 3, T)
            s.update(p=p, mix=yab)
        else:
            qkvr = mm_qkv_rope(h1, wl["w_qkv"], cs, sn)
            att, probs = attn_fwd(qkvr, sinks_flat[e * N_HEADS:(e + 1) * N_HEADS], bias, T)
            s.update(qkvr=qkvr, mix=att, probs=probs)
            y1, x1, h2 = mm_resid_norm_fwd(att, wl["w_o"], xs, norm_ffn[l][None], mods[l], mods[l], 2, 3, T)
        wl.update(gathered_weights(l, 1, h2))
        u = mm_nn(h2, wl["w_up"], BF16)
        if l + 1 < depth:
            z, y2, xs, h1 = ffn_mid_fwd(u, w_conv_ffn_f[l], wl["w_down"], x1, norm_mix[l + 1][None], mods[l],
                                        mods[l + 1], 5, 0, T)
        else:
            z, y2, xs = ffn_mid_fwd(u, w_conv_ffn_f[l], wl["w_down"], x1, None, mods[l], None, 5, 0, T)
        s.update(y1=y1, x1=x1, h2=h2, u=u, z=z, y2=y2)
        saved.append(s)

    loss_part, dx, d_final, dy2, dz, dg2_last = loss_head(
        xs, final_norm[None], loss_target[0], saved[depth - 1]["y2"], mods[depth - 1], W[depth - 1]["w_down"], 5, T)
    loss = lax.psum(loss_part[0, 0], ("x", "y", "c"))

    d_mods, d_norm_mix, d_norm_ffn = [None] * depth, [None] * depth, [None] * depth
    d_conv_a, d_conv_b, d_vecs, d_sinks, d_wc = [None] * 2, [None] * 2, [None] * 2, [None] * 2, [None] * depth
    dss1, dss2, dg1, dg2 = [None] * depth, [None] * depth, [None] * depth, [None] * depth
    scattering = {}

    def scatter(l, half, G, after):
        grads_h = [G[n] for n, _ in half_weights(l, half)]
        lands = [lax.empty((N_CHIP - 1, *g.shape[1:]), g.dtype) for g in grads_h]
        send_sems, recv_sems, grads_h, lands, token = exchange_start(
            f"scatter_start_{l}_{half}", "scatter", grads_h, lands, after)
        scattering[l, half] = (send_sems, recv_sems, grads_h, lands)
        return token

    dg2[depth - 1] = dg2_last
    pending = 0.0
    for l in reversed(range(depth)):
        e = l // 2
        s, wl = saved[l], W[l]
        G = {}
        G["w_down"] = mm_tn(s["z"], dy2, "row", wl["w_down"])
        duc, d_wc[l] = ffn_mid_bwd(dz, s["u"], w_conv_ffn_f[l] + pending, T)
        du, dx, dy1, dss2[l], d_norm_ffn[l], dg1[l] = ffn_up_bwd(
            duc, w_conv_ffn_f[l], wl["w_up"], s["x1"], norm_ffn[l][None], mods[l], dx, s["y1"], mods[l], 3, 2, T)
        G["w_up"] = mm_tn(s["h2"], du, "col", wl["w_up"])
        started = scatter(l, 1, G, du)[0, 0]
        if l % 2 == 0:
            G["w_out_ab"] = mm_tn(s["mix"], dy1, "row", wl["w_out_ab"])
            dyab = mm_nt(dy1, wl["w_out_ab"], F32)
            dmid, d_conv_a[e], d_conv_b[e], d_vecs[e] = convmix_bwd1(
                dyab, s["p"], conv_a_f[e] + started, conv_b_f[e], conv_b_bias[e][None], ln_b_gain[e][None],
                ln_b_bias[e][None], T)
            if l > 0:
                dp, dx, dy2, dz, dss1[l], d_norm_mix[l], dg2[l - 1] = mixer_in_bwd(
                    dmid, s["p"], conv_a_f[e], conv_b_f[e], wl["w_in_ab"], s["x0"], norm_mix[l][None], mods[l], dx,
                    saved[l - 1]["y2"], mods[l - 1], W[l - 1]["w_down"], 0, 5, T)
            else:
                dp, dx, dss1[l], d_norm_mix[l] = mixer_in_bwd(
                    dmid, s["p"], conv_a_f[e], conv_b_f[e], wl["w_in_ab"], s["x0"], norm_mix[l][None], mods[l], dx,
                    None, None, None, 0, 0, T)
            G["w_in_ab"] = mm_tn(s["h1"], dp, "col", wl["w_in_ab"])
        else:
            G["w_o"] = mm_tn(s["mix"], dy1, "row", wl["w_o"])
            datt = mm_nt(dy1, wl["w_o"], BF16)
            dq, dks, dvs, dkc, dvc, d_sinks[e] = attn_bwd(s["qkvr"], datt, s["probs"], T)
            dqkv, dx, dy2, dz, dss1[l], d_norm_mix[l], dg2[l - 1] = attn_in_bwd(
                dq, dks, dvs, dkc, dvc, cs + started, sn, wl["w_qkv"], s["x0"], norm_mix[l][None], mods[l], dx,
                saved[l - 1]["y2"], mods[l - 1], W[l - 1]["w_down"], 0, 5, T)
            G["w_qkv"] = mm_tn(s["h1"], dqkv, "col", wl["w_qkv"])
        token = scatter(l, 0, G, dx)
        pending = token[0, 0]
    grad_x = dx[:T][None]
    for l in range(depth):
        a1, a2 = dss1[l].sum(2), dss2[l].sum(2)
        d_mods[l] = jnp.stack([a1[:, 0], a1[:, 1], dg1[l].sum(1), a2[:, 0], a2[:, 1], dg2[l].sum(1)], axis=1)

    d_mods = jnp.stack(d_mods)
    summed_parts = [
        d_mods[:, 1],
        jnp.stack(d_norm_mix).sum(1), jnp.stack(d_norm_ffn).sum(1),
        jnp.stack(d_conv_a).sum(2), jnp.stack(d_conv_b).sum(2),
        jnp.stack(d_vecs).sum(2),
        jnp.stack(d_sinks)[:, 0, :N_HEADS],
        jnp.stack(d_wc).sum(2), d_final.sum(0) + pending]
    summed_shapes = [p.shape for p in summed_parts]
    n_own = depth * 6 * dm
    pack = _pack([d_mods[:, 0]] + summed_parts)
    parts = allgather8(pack).reshape(N_DEV, -1, 128)
    total = sum_slots(parts)
    own_rows = parts.reshape(N_DEV, -1)[:, :n_own].reshape(N_DEV, depth, 6 * dm)
    (dmod_ctx, g_norm_mix, g_norm_ffn, g_conv_a, g_conv_b, g_vecs, g_sinks, g_wc, g_final) = _unpack(
        total.reshape(-1)[n_own:], summed_shapes)
    dmod_rows = jnp.concatenate([jnp.moveaxis(own_rows, 0, 1), dmod_ctx.reshape(depth, 1, 6 * dm),
                                 jnp.zeros((depth, 7, 6 * dm), F32)], axis=1)
    g_b_mod = dmod_rows.sum(1)
    dmod_sh = lax.dynamic_slice_in_dim(dmod_rows, chip * ns_mod, ns_mod, axis=2)
    g_w_mod, dsilu = adaln_bwd(cond, dmod_sh, w_mod)
    dsilu_all = allgather8(dsilu[8:16]).reshape(N_DEV, 8, dm)
    dsilu_ctx = sum_slots(dsilu_all[0::2])[0]
    sg = jax.nn.sigmoid(c_ctx)
    g_c_ctx = dsilu_ctx * (sg * (1.0 + c_ctx * (1.0 - sg)))

    def shard_cols(full, width):
        return lax.dynamic_slice_in_dim(full, chip * width, width, axis=full.ndim - 1)
    g_conv_a_s = shard_cols(g_conv_a, conv_a.shape[-1])
    g_conv_b_s = shard_cols(g_conv_b, conv_b.shape[-1])
    g_wc_s = shard_cols(g_wc, w_conv_ffn.shape[-1])

    grads, deltas, new_m, new_v = {}, {}, {}, {}

    def step_2d(name, wv, ga, gb, mv, vv):
        shp = wv.shape
        r2 = lambda t: t.reshape(-1, shp[-1])
        g, d, mn, vn = adamw(r2(wv), r2(ga), None if gb is None else r2(gb), r2(mv), r2(vv))
        grads[name], deltas[name], new_m[name], new_v[name] = (t.reshape(shp) for t in (g, d, mn, vn))

    step_2d("w_mod", w_mod, g_w_mod, None, m_w_mod, v_w_mod)
    sums = {n: None for n in masters}
    for l in reversed(range(depth)):
        for half in (1, 0):
            send_sems, recv_sems, grads_h, lands = scattering[l, half]
            grads_h, landed = exchange_wait(f"scatter_wait_{l}_{half}", "scatter", send_sems, recv_sems, grads_h,
                                            lands, deltas["w_mod"])
            for (n, j), own, arr in zip(half_weights(l, half), grads_h, landed):
                sums[n] = sum_landed(arr, own, chip_id, j, masters[n].shape[0], sums[n])
    moments = {"w_in_ab": (m_w_in_ab, v_w_in_ab), "w_out_ab": (m_w_out_ab, v_w_out_ab),
               "w_qkv": (m_w_qkv, v_w_qkv), "w_o": (m_w_o, v_w_o), "w_up": (m_w_up, v_w_up),
               "w_down": (m_w_down, v_w_down)}
    others = swap_with_sibling([sums[name] for name in masters])
    for (name, wv), other in zip(masters.items(), others):
        step_2d(name, wv, sums[name].reshape(wv.shape), other.reshape(wv.shape), *moments[name])

    small = [("c_ctx", c_ctx, g_c_ctx, m_c_ctx, v_c_ctx), ("b_mod", b_mod, g_b_mod, m_b_mod, v_b_mod),
             ("norm_mix", norm_mix, g_norm_mix, m_norm_mix, v_norm_mix),
             ("norm_ffn", norm_ffn, g_norm_ffn, m_norm_ffn, v_norm_ffn),
             ("conv_a", conv_a, g_conv_a_s, m_conv_a, v_conv_a), ("conv_b", conv_b, g_conv_b_s, m_conv_b, v_conv_b),
             ("conv_b_bias", conv_b_bias, g_vecs[:, 0], m_conv_b_bias, v_conv_b_bias),
             ("ln_b_gain", ln_b_gain, g_vecs[:, 1], m_ln_b_gain, v_ln_b_gain),
             ("ln_b_bias", ln_b_bias, g_vecs[:, 2], m_ln_b_bias, v_ln_b_bias),
             ("sinks", sinks, g_sinks, m_sinks, v_sinks),
             ("w_conv_ffn", w_conv_ffn, g_wc_s, m_w_conv_ffn, v_w_conv_ffn),
             ("final_norm", final_norm, g_final, m_final_norm, v_final_norm)]
    shapes = [t[1].shape for t in small]
    packed = [_pack([t[k] for t in small]) for k in (1, 2, 3, 4)]
    n_real = sum(t[1].size for t in small)
    lane_id = jnp.arange(packed[3].size).reshape(packed[3].shape)
    packed[3] = jnp.where(lane_id < n_real, packed[3], 1.0)
    outs = adamw(packed[0], packed[1], None, packed[2], packed[3])
    for (name, *_), g, d, mn, vn in zip(small, *[_unpack(o, shapes) for o in outs]):
        grads[name], deltas[name], new_m[name], new_v[name] = g, d, mn, vn

    order = ["c_ctx", "w_mod", "b_mod", "norm_mix", "norm_ffn", "w_in_ab", "conv_a", "conv_b", "conv_b_bias",
             "ln_b_gain", "ln_b_bias", "w_out_ab", "w_qkv", "w_o", "sinks", "w_up", "w_conv_ffn", "w_down",
             "final_norm"]
    return (loss, grad_x, *[grads[n] for n in order], *[deltas[n] for n in order],
            *[new_m[n] for n in order], *[new_v[n] for n in order])
```

```python
import jax
import jax.numpy as jnp
from jax import lax
from jax.experimental import pallas as pl
from jax.experimental.pallas import tpu as pltpu

F32 = jnp.float32
BF16 = jnp.bfloat16
MESH = pl.DeviceIdType.MESH

EPS = 1e-6
NEG_INF = -1e30
GRID_W = 64
HEAD_DIM = 64
N_HEADS = 16
WINDOW = 128
QB = 128
ROPE_THETA = 10000.0
A_W = 512
B_CONV = 31
D_FF = 2816
ADAM_LR, ADAM_B1, ADAM_B2, ADAM_EPS, ADAM_WD, ADAM_STEP = 0.001, 0.9, 0.999, 1e-8, 0.01, 10

TMR = 256
HALO = 16
N_DEV = 8
N_CHIP = 4


def _params(vmem_mb=None):
    if vmem_mb is None:
        return pltpu.CompilerParams()
    return pltpu.CompilerParams(vmem_limit_bytes=vmem_mb * 1024 * 1024)


def _row_tile(rows, cap=768):
    for t in (2816, 1408, 768, 704, 512, 384, 256, 128, 64, 32, 16, 8):
        if t <= cap and rows % t == 0:
            return t
    raise ValueError(rows)


def _colsum8(v):
    r, c = v.shape
    return v.reshape(r // 8, 8, c).sum(axis=0)


def _sigmoid(v):
    return 0.5 * jnp.tanh(0.5 * v) + 0.5


def mm_nn(a, w, out_dtype):
    R = a.shape[0]
    _, kb, nb = w.shape
    tm = _row_tile(R)

    def body(a_ref, w_ref, o_ref):
        av = a_ref[...].astype(BF16)
        for q in range(N_CHIP):
            o_ref[:, q * nb:(q + 1) * nb] = jnp.dot(av, w_ref[q], preferred_element_type=F32).astype(o_ref.dtype)
    return pl.pallas_call(
        body, name="mm_nn_col", grid=(R // tm,),
        in_specs=[pl.BlockSpec((tm, kb), lambda i: (i, 0)),
                  pl.BlockSpec((N_CHIP, kb, nb), lambda i: (0, 0, 0), pipeline_mode=pl.Buffered(1))],
        out_specs=pl.BlockSpec((tm, N_CHIP * nb), lambda i: (i, 0)),
        out_shape=jax.ShapeDtypeStruct((R, N_CHIP * nb), out_dtype),
        compiler_params=_params(48))(a, w)


def mm_nt(d, w, out_dtype):
    R = d.shape[0]
    _, kb, nb = w.shape
    tm = _row_tile(R)
    contract_last = (((1,), (1,)), ((), ()))
    resident = pl.BlockSpec((N_CHIP, kb, nb), lambda i: (0, 0, 0), pipeline_mode=pl.Buffered(1))

    def body(d_ref, w_ref, o_ref):
        wv = w_ref[...].reshape(N_CHIP * kb, nb)
        o_ref[...] = lax.dot_general(d_ref[...].astype(BF16), wv, contract_last,
                                     preferred_element_type=F32).astype(o_ref.dtype)
    return pl.pallas_call(
        body, name="mm_nt_row", grid=(R // tm,),
        in_specs=[pl.BlockSpec((tm, nb), lambda i: (i, 0)), resident],
        out_specs=pl.BlockSpec((tm, N_CHIP * kb), lambda i: (i, 0)),
        out_shape=jax.ShapeDtypeStruct((R, N_CHIP * kb), out_dtype),
        compiler_params=_params(48))(d, w)


def mm_tn(a, d, kind, like):
    R = a.shape[0]
    _, kb, nb = like.shape
    tm = _row_tile(R, 1408 if kind == "col" else 768)
    nsteps = R // tm
    contract_rows = (((0,), (0,)), ((), ()))
    out_shape = jax.ShapeDtypeStruct(like.shape, BF16)

    def accumulate(a_ref, d_ref, acc_ref):
        @pl.when(pl.program_id(1) == 0)
        def _():
            acc_ref[...] = jnp.zeros_like(acc_ref)
        acc_ref[...] += lax.dot_general(a_ref[...].astype(BF16), d_ref[...].astype(BF16), contract_rows,
                                        preferred_element_type=F32)
    if kind == "col":
        def body(a_ref, d_ref, o_ref, acc_ref):
            accumulate(a_ref, d_ref, acc_ref)

            @pl.when(pl.program_id(1) == nsteps - 1)
            def _():
                o_ref[...] = acc_ref[...].astype(BF16)
        return pl.pallas_call(
            body, name="mm_tn_col", grid=(N_CHIP, nsteps),
            in_specs=[pl.BlockSpec((tm, kb), lambda q, i: (i, 0)), pl.BlockSpec((tm, nb), lambda q, i: (i, q))],
            out_specs=pl.BlockSpec((None, kb, nb), lambda q, i: (q, 0, 0)), out_shape=out_shape,
            scratch_shapes=[pltpu.VMEM((kb, nb), F32)], compiler_params=_params(48))(a, d)
    tn = 512

    def body(a_ref, d_ref, o_ref, acc_ref):
        accumulate(a_ref, d_ref, acc_ref)

        @pl.when(pl.program_id(1) == nsteps - 1)
        def _():
            o_ref[...] = acc_ref[...].astype(BF16).reshape(N_CHIP, kb, tn)
    return pl.pallas_call(
        body, name="mm_tn_row", grid=(nb // tn, nsteps),
        in_specs=[pl.BlockSpec((tm, N_CHIP * kb), lambda n, i: (i, 0)), pl.BlockSpec((tm, tn), lambda n, i: (i, n))],
        out_specs=pl.BlockSpec((N_CHIP, kb, tn), lambda n, i: (0, 0, n)), out_shape=out_shape,
        scratch_shapes=[pltpu.VMEM((N_CHIP * kb, tn), F32)], compiler_params=_params(48))(a, d)


def _seg(i, T):
    return (i >= T // TMR).astype(jnp.int32)


def norm_mod_fwd(x, nw, mod, k, T):
    R, dm = x.shape

    def body(x_ref, nw_ref, mod_ref, h_ref):
        seg = _seg(pl.program_id(0), T)
        sh = mod_ref[seg, pl.ds(k, 1), :]
        sc = mod_ref[seg, pl.ds(k + 1, 1), :]
        xv = x_ref[...]
        r = lax.rsqrt(jnp.mean(xv * xv, axis=-1, keepdims=True) + EPS)
        h_ref[...] = ((xv * r * nw_ref[...]) * (1.0 + sc) + sh).astype(BF16)
    return pl.pallas_call(
        body, name="norm_mod_fwd", grid=(R // TMR,),
        in_specs=[pl.BlockSpec((TMR, dm), lambda i: (i, 0)),
                  pl.BlockSpec((1, dm), lambda i: (0, 0)),
                  pl.BlockSpec((2, 6, dm), lambda i: (0, 0, 0))],
        out_specs=pl.BlockSpec((TMR, dm), lambda i: (i, 0)),
        out_shape=jax.ShapeDtypeStruct((R, dm), BF16))(x, nw, mod)


def mm_resid_norm_fwd(a, w, x, nw, mod_g, mod_n, kg, kn, T):
    R, dm = x.shape
    _, kb, nb = w.shape

    def body(a_ref, w_ref, x_ref, nw_ref, mg_ref, mn_ref, y_ref, xo_ref, h_ref):
        seg = _seg(pl.program_id(0), T)
        yv = jnp.dot(a_ref[...].astype(BF16), w_ref[...].reshape(N_CHIP * kb, nb), preferred_element_type=F32)
        y_ref[...] = yv
        xv = x_ref[...] + mg_ref[seg, pl.ds(kg, 1), :] * yv
        xo_ref[...] = xv
        r = lax.rsqrt(jnp.mean(xv * xv, axis=-1, keepdims=True) + EPS)
        h_ref[...] = ((xv * r * nw_ref[...]) * (1.0 + mn_ref[seg, pl.ds(kn + 1, 1), :])
                      + mn_ref[seg, pl.ds(kn, 1), :]).astype(BF16)
    tile = pl.BlockSpec((TMR, dm), lambda i: (i, 0))
    modspec = pl.BlockSpec((2, 6, dm), lambda i: (0, 0, 0))
    return pl.pallas_call(
        body, name="mm_resid_norm_fwd", grid=(R // TMR,),
        in_specs=[pl.BlockSpec((TMR, N_CHIP * kb), lambda i: (i, 0)),
                  pl.BlockSpec((N_CHIP, kb, nb), lambda i: (0, 0, 0), pipeline_mode=pl.Buffered(1)),
                  tile, pl.BlockSpec((1, dm), lambda i: (0, 0)), modspec, modspec],
        out_specs=[tile, tile, tile],
        out_shape=[jax.ShapeDtypeStruct((R, dm), F32), jax.ShapeDtypeStruct((R, dm), F32),
                   jax.ShapeDtypeStruct((R, dm), BF16)],
        compiler_params=_params(48))(a, w, x, nw, mod_g, mod_n)


def _halo_specs(width, R):
    nblk = R // HALO
    per = TMR // HALO
    return (pl.BlockSpec((HALO, width), lambda i: (jnp.maximum(i * per - 1, 0), 0)),
            pl.BlockSpec((TMR, width), lambda i: (i, 0)),
            pl.BlockSpec((HALO, width), lambda i: (jnp.minimum((i + 1) * per, nblk - 1), 0)))


def _halo_live(i, T, R):
    nl = T // TMR
    return (i != 0) & (i != nl), (i != nl - 1) & (i != R // TMR - 1)


def _ext(refs, c0, cw, live, halo=HALO):
    pref, ref, nref = refs
    before = jnp.where(live[0], pref[:, c0:c0 + cw].astype(F32)[HALO - halo:], 0.0)
    after = jnp.where(live[1], nref[:, c0:c0 + cw].astype(F32)[:halo], 0.0)
    return jnp.concatenate([before, ref[:, c0:c0 + cw].astype(F32), after], axis=0)


def _at(ext, off, halo=HALO):
    n = ext.shape[0]
    s = (-off) % n
    y = pltpu.roll(ext, s, 0) if s else ext
    return y[halo:halo + TMR]


def ffn_mid_fwd(u, wc, w_down, x, nw, mod_g, mod_n, kg, kn, T):
    R, w2 = u.shape
    dm = x.shape[1]
    _, kb, nb = w_down.shape
    cw = 256
    with_norm = nw is not None

    def body(*refs):
        if with_norm:
            up_ref, u_ref, un_ref, wc_ref, w_ref, x_ref, nw_ref, mg_ref, mn_ref, z_ref, y_ref, xo_ref, h_ref = refs
        else:
            up_ref, u_ref, un_ref, wc_ref, w_ref, x_ref, mg_ref, z_ref, y_ref, xo_ref = refs
        i = pl.program_id(0)
        seg = _seg(i, T)
        live = _halo_live(i, T, R)

        def conv(c0):
            e = _ext((up_ref, u_ref, un_ref), c0, cw, live, 8)
            return (wc_ref[pl.ds(0, 1), c0:c0 + cw] * _at(e, -1, 8) + wc_ref[pl.ds(1, 1), c0:c0 + cw] * _at(e, 0, 8)
                    + wc_ref[pl.ds(2, 1), c0:c0 + cw] * _at(e, 1, 8))
        yv = None
        for j in range(D_FF // cw):
            a = conv(j * cw)
            g = conv(D_FF + j * cw)
            zc = (g * _sigmoid(g) * a).astype(BF16)
            z_ref[:, j * cw:(j + 1) * cw] = zc
            t = jnp.dot(zc, w_ref[j * cw:(j + 1) * cw, :], preferred_element_type=F32)
            yv = t if yv is None else yv + t
        y_ref[...] = yv
        xv = x_ref[...] + mg_ref[seg, pl.ds(kg, 1), :] * yv
        xo_ref[...] = xv
        if with_norm:
            r = lax.rsqrt(jnp.mean(xv * xv, axis=-1, keepdims=True) + EPS)
            h_ref[...] = ((xv * r * nw_ref[...]) * (1.0 + mn_ref[seg, pl.ds(kn + 1, 1), :])
                          + mn_ref[seg, pl.ds(kn, 1), :]).astype(BF16)
    tile = pl.BlockSpec((TMR, dm), lambda i: (i, 0))
    modspec = pl.BlockSpec((2, 6, dm), lambda i: (0, 0, 0))
    w_down = w_down.reshape(N_CHIP * kb, nb)
    in_specs = [*_halo_specs(w2, R), pl.BlockSpec((3, w2), lambda i: (0, 0)),
                pl.BlockSpec((N_CHIP * kb, nb), lambda i: (0, 0), pipeline_mode=pl.Buffered(1)), tile]
    out_specs = [pl.BlockSpec((TMR, D_FF), lambda i: (i, 0)), tile, tile]
    out_shape = [jax.ShapeDtypeStruct((R, D_FF), BF16), jax.ShapeDtypeStruct((R, dm), F32),
                 jax.ShapeDtypeStruct((R, dm), F32)]
    if with_norm:
        return pl.pallas_call(
            body, name="ffn_mid_fwd", grid=(R // TMR,),
            in_specs=in_specs + [pl.BlockSpec((1, dm), lambda i: (0, 0)), modspec, modspec],
            out_specs=out_specs + [tile], out_shape=out_shape + [jax.ShapeDtypeStruct((R, dm), BF16)],
            compiler_params=_params(48))(u, u, u, wc, w_down, x, nw, mod_g, mod_n)
    return pl.pallas_call(
        body, name="ffn_mid_fwd_last", grid=(R // TMR,), in_specs=in_specs + [modspec],
        out_specs=out_specs, out_shape=out_shape, compiler_params=_params(48))(u, u, u, wc, w_down, x, mod_g)


def ffn_mid_bwd(dz, u, wc, T):
    R, w2 = u.shape
    cw = 256

    def body(dz_ref, up_ref, u_ref, un_ref, wc_ref, duc_ref, dwc_ref):
        i = pl.program_id(0)
        live = _halo_live(i, T, R)

        @pl.when(i == 0)
        def _():
            dwc_ref[...] = jnp.zeros_like(dwc_ref)

        def taps(c0):
            e = _ext((up_ref, u_ref, un_ref), c0, cw, live, 8)
            return [_at(e, -1, 8), _at(e, 0, 8), _at(e, 1, 8)]

        def conv(t, c0):
            return (wc_ref[pl.ds(0, 1), c0:c0 + cw] * t[0] + wc_ref[pl.ds(1, 1), c0:c0 + cw] * t[1]
                    + wc_ref[pl.ds(2, 1), c0:c0 + cw] * t[2])
        for j in range(D_FF // cw):
            ca, cg = j * cw, D_FF + j * cw
            dzv = dz_ref[:, ca:ca + cw].astype(F32)
            ta, tg = taps(ca), taps(cg)
            a, g = conv(ta, ca), conv(tg, cg)
            sg = _sigmoid(g)
            da = dzv * (g * sg)
            dg = dzv * a * (sg * (1.0 + g * (1.0 - sg)))
            duc_ref[:, ca:ca + cw] = da.astype(BF16)
            duc_ref[:, cg:cg + cw] = dg.astype(BF16)
            for k in range(3):
                dwc_ref[k, :, ca:ca + cw] += _colsum8(da * ta[k])
                dwc_ref[k, :, cg:cg + cw] += _colsum8(dg * tg[k])
    return pl.pallas_call(
        body, name="ffn_mid_bwd", grid=(R // TMR,),
        in_specs=[pl.BlockSpec((TMR, D_FF), lambda i: (i, 0)), *_halo_specs(w2, R),
                  pl.BlockSpec((3, w2), lambda i: (0, 0))],
        out_specs=[pl.BlockSpec((TMR, w2), lambda i: (i, 0)), pl.BlockSpec((3, 8, w2), lambda i: (0, 0, 0))],
        out_shape=[jax.ShapeDtypeStruct((R, w2), BF16), jax.ShapeDtypeStruct((3, 8, w2), F32)],
        compiler_params=_params(48))(dz, u, u, u, wc)


def ffn_up_bwd(duc, wc, w_up, x, nw, mod_n, dxr, y, mod_g, kn, kg, T):
    R, w2 = duc.shape
    dm = x.shape[1]
    _, kb, nb = w_up.shape
    cw = 128
    contract_last = (((1,), (1,)), ((), ()))

    def body(dp_ref, d_ref, dn_ref, wc_ref, w_ref, x_ref, nw_ref, mn_ref, dxr_ref, y_ref, mg_ref,
             du_ref, dx_ref, dy_ref, dmod_ref, dnw_ref, dg_ref):
        i = pl.program_id(0)
        seg = _seg(i, T)
        live = _halo_live(i, T, R)

        @pl.when(i == 0)
        def _():
            dmod_ref[...] = jnp.zeros_like(dmod_ref)
            dnw_ref[...] = jnp.zeros_like(dnw_ref)
            dg_ref[...] = jnp.zeros_like(dg_ref)
        dhv = None
        for q in range(N_CHIP):
            for j in range(nb // cw):
                c0 = q * nb + j * cw
                e = _ext((dp_ref, d_ref, dn_ref), c0, cw, live, 8)
                du_ref[:, c0:c0 + cw] = (wc_ref[pl.ds(0, 1), c0:c0 + cw] * _at(e, 1, 8)
                                         + wc_ref[pl.ds(1, 1), c0:c0 + cw] * _at(e, 0, 8)
                                         + wc_ref[pl.ds(2, 1), c0:c0 + cw] * _at(e, -1, 8)).astype(BF16)
            t = lax.dot_general(du_ref[:, q * nb:(q + 1) * nb], w_ref[q], contract_last,
                                preferred_element_type=F32)
            dhv = t if dhv is None else dhv + t
        sc = mn_ref[seg, pl.ds(kn + 1, 1), :]
        nwv = nw_ref[...]
        xv = x_ref[...]
        r = lax.rsqrt(jnp.mean(xv * xv, axis=-1, keepdims=True) + EPS)
        xh = xv * r
        dmod_ref[seg, 0] += _colsum8(dhv)
        dmod_ref[seg, 1] += _colsum8(dhv * (xh * nwv))
        dn = dhv * (1.0 + sc)
        dnw_ref[...] += _colsum8(dn * xh)
        dxh = dn * nwv
        dx = dxr_ref[...] + r * (dxh - xh * jnp.mean(dxh * xh, axis=-1, keepdims=True))
        dx_ref[...] = dx
        dy_ref[...] = (mg_ref[seg, pl.ds(kg, 1), :] * dx).astype(BF16)
        dg_ref[seg] += _colsum8(dx * y_ref[...])
    tile = pl.BlockSpec((TMR, dm), lambda i: (i, 0))
    modspec = pl.BlockSpec((2, 6, dm), lambda i: (0, 0, 0))
    return pl.pallas_call(
        body, name="ffn_up_bwd", grid=(R // TMR,),
        in_specs=[*_halo_specs(w2, R), pl.BlockSpec((3, w2), lambda i: (0, 0)),
                  pl.BlockSpec((N_CHIP, kb, nb), lambda i: (0, 0, 0), pipeline_mode=pl.Buffered(1)),
                  tile, pl.BlockSpec((1, dm), lambda i: (0, 0)), modspec, tile, tile, modspec],
        out_specs=[pl.BlockSpec((TMR, w2), lambda i: (i, 0)), tile, tile,
                   pl.BlockSpec((2, 2, 8, dm), lambda i: (0, 0, 0, 0)), pl.BlockSpec((8, dm), lambda i: (0, 0)),
                   pl.BlockSpec((2, 8, dm), lambda i: (0, 0, 0))],
        out_shape=[jax.ShapeDtypeStruct((R, w2), BF16), jax.ShapeDtypeStruct((R, dm), F32),
                   jax.ShapeDtypeStruct((R, dm), BF16), jax.ShapeDtypeStruct((2, 2, 8, dm), F32),
                   jax.ShapeDtypeStruct((8, dm), F32), jax.ShapeDtypeStruct((2, 8, dm), F32)],
        compiler_params=_params(48))(duc, duc, duc, wc, w_up, x, nw, mod_n, dxr, y, mod_g)


_CW = 128


def _mixer_a(prefs, wa_ref, live):
    cin = _ext(prefs, A_W, A_W, live) * _ext(prefs, 2 * A_W, A_W, live)
    ca = (wa_ref[pl.ds(0, 1), :] * _at(cin, -1) + wa_ref[pl.ds(1, 1), :] * _at(cin, 0)
          + wa_ref[pl.ds(2, 1), :] * _at(cin, 1))
    return cin, ca


def _mixer_b(prefs, wb_ref, bias_ref, live, ub_s, ub2_s):
    for cc in range(A_W // _CW):
        c0 = cc * _CW
        ub = _ext(prefs, 3 * A_W + c0, _CW, live) * _sigmoid(_ext(prefs, 4 * A_W + c0, _CW, live))
        ub_s[:, c0:c0 + _CW] = ub
        acc = jnp.zeros((TMR, _CW), F32) + bias_ref[:, c0:c0 + _CW]
        for k in range(B_CONV):
            acc = acc + wb_ref[pl.ds(k, 1), c0:c0 + _CW] * _at(ub, k - B_CONV // 2)
        ub2_s[:, c0:c0 + _CW] = acc


def _layernorm_stats(v):
    mu = jnp.mean(v, axis=-1, keepdims=True)
    xc = v - mu
    rs = lax.rsqrt(jnp.mean(xc * xc, axis=-1, keepdims=True) + EPS)
    return xc * rs, rs


def mixer_fwd(p, wa, wb, bias, lng, lnb, w_out, x, nw, mod, kg, kn, T):
    R, wp = p.shape
    dm = x.shape[1]
    _, kb, nb = w_out.shape

    def body(pp_ref, p_ref, pn_ref, wa_ref, wb_ref, bias_ref, lng_ref, lnb_ref, w_ref, x_ref, nw_ref, mod_ref,
             o_ref, y_ref, xo_ref, h_ref, ub_s, ub2_s):
        i = pl.program_id(0)
        seg = _seg(i, T)
        live = _halo_live(i, T, R)
        prefs = (pp_ref, p_ref, pn_ref)
        _, ca = _mixer_a(prefs, wa_ref, live)
        ya = (p_ref[:, 0:A_W].astype(F32) * ca).astype(BF16)
        o_ref[:, 0:A_W] = ya
        yv = jnp.dot(ya, w_ref[0:A_W, :], preferred_element_type=F32)
        _mixer_b(prefs, wb_ref, bias_ref, live, ub_s, ub2_s)
        xh, _ = _layernorm_stats(ub2_s[...])
        lv = xh * lng_ref[...] + lnb_ref[...]
        yb = (lv * _sigmoid(lv)).astype(BF16)
        o_ref[:, A_W:2 * A_W] = yb
        yv = yv + jnp.dot(yb, w_ref[A_W:2 * A_W, :], preferred_element_type=F32)
        y_ref[...] = yv
        xv = x_ref[...] + mod_ref[seg, pl.ds(kg, 1), :] * yv
        xo_ref[...] = xv
        r = lax.rsqrt(jnp.mean(xv * xv, axis=-1, keepdims=True) + EPS)
        h_ref[...] = ((xv * r * nw_ref[...]) * (1.0 + mod_ref[seg, pl.ds(kn + 1, 1), :])
                      + mod_ref[seg, pl.ds(kn, 1), :]).astype(BF16)
    vec = pl.BlockSpec((1, A_W), lambda i: (0, 0))
    tile = pl.BlockSpec((TMR, dm), lambda i: (i, 0))
    return pl.pallas_call(
        body, name="mixer_fwd", grid=(R // TMR,),
        in_specs=[*_halo_specs(wp, R), pl.BlockSpec((3, A_W), lambda i: (0, 0)),
                  pl.BlockSpec((B_CONV, A_W), lambda i: (0, 0)), vec, vec, vec,
                  pl.BlockSpec((N_CHIP * kb, nb), lambda i: (0, 0), pipeline_mode=pl.Buffered(1)),
                  tile, pl.BlockSpec((1, dm), lambda i: (0, 0)), pl.BlockSpec((2, 6, dm), lambda i: (0, 0, 0))],
        out_specs=[pl.BlockSpec((TMR, 2 * A_W), lambda i: (i, 0)), tile, tile, tile],
        out_shape=[jax.ShapeDtypeStruct((R, 2 * A_W), BF16), jax.ShapeDtypeStruct((R, dm), F32),
                   jax.ShapeDtypeStruct((R, dm), F32), jax.ShapeDtypeStruct((R, dm), BF16)],
        scratch_shapes=[pltpu.VMEM((TMR + 2 * HALO, A_W), F32), pltpu.VMEM((TMR, A_W), F32)],
        compiler_params=_params(48))(p, p, p, wa, wb, bias, lng, lnb, w_out.reshape(N_CHIP * kb, nb), x, nw, mod)


def convmix_bwd1(dyab, p, wa, wb, bias, lng, lnb, T):
    R, wp = p.shape

    def body(dy_ref, pp_ref, p_ref, pn_ref, wa_ref, wb_ref, bias_ref, lng_ref, lnb_ref,
             dmid_ref, dwa_ref, dwb_ref, dvec_ref, ub_s, ub2_s):
        i = pl.program_id(0)
        live = _halo_live(i, T, R)

        @pl.when(i == 0)
        def _():
            dwa_ref[...] = jnp.zeros_like(dwa_ref)
            dwb_ref[...] = jnp.zeros_like(dwb_ref)
            dvec_ref[...] = jnp.zeros_like(dvec_ref)
        prefs = (pp_ref, p_ref, pn_ref)
        cin, ca = _mixer_a(prefs, wa_ref, live)
        dya = dy_ref[:, 0:A_W]
        dmid_ref[:, 0:A_W] = dya * ca
        dca = dya * p_ref[:, 0:A_W].astype(F32)
        dmid_ref[:, A_W:2 * A_W] = dca
        for k in range(3):
            dwa_ref[k] += _colsum8(dca * _at(cin, k - 1))
        _mixer_b(prefs, wb_ref, bias_ref, live, ub_s, ub2_s)
        xh, rs = _layernorm_stats(ub2_s[...])
        gain = lng_ref[...]
        lv = xh * gain + lnb_ref[...]
        sl = _sigmoid(lv)
        dl = dy_ref[:, A_W:2 * A_W] * (sl * (1.0 + lv * (1.0 - sl)))
        dvec_ref[1] += _colsum8(dl * xh)
        dvec_ref[2] += _colsum8(dl)
        dxh = dl * gain
        dub2 = rs * (dxh - jnp.mean(dxh, axis=-1, keepdims=True)
                     - xh * jnp.mean(dxh * xh, axis=-1, keepdims=True))
        dvec_ref[0] += _colsum8(dub2)
        dmid_ref[:, 2 * A_W:3 * A_W] = dub2
        for cc in range(A_W // _CW):
            c0 = cc * _CW
            ub = ub_s[:, c0:c0 + _CW]
            d = dmid_ref[:, 2 * A_W + c0:2 * A_W + c0 + _CW]
            for k in range(B_CONV):
                dwb_ref[k, :, c0:c0 + _CW] += _colsum8(d * _at(ub, k - B_CONV // 2))
    vec = pl.BlockSpec((1, A_W), lambda i: (0, 0))
    return pl.pallas_call(
        body, name="convmix_bwd1", grid=(R // TMR,),
        in_specs=[pl.BlockSpec((TMR, 2 * A_W), lambda i: (i, 0)), *_halo_specs(wp, R),
                  pl.BlockSpec((3, A_W), lambda i: (0, 0)), pl.BlockSpec((B_CONV, A_W), lambda i: (0, 0)),
                  vec, vec, vec],
        out_specs=[pl.BlockSpec((TMR, 3 * A_W), lambda i: (i, 0)),
                   pl.BlockSpec((3, 8, A_W), lambda i: (0, 0, 0)),
                   pl.BlockSpec((B_CONV, 8, A_W), lambda i: (0, 0, 0)),
                   pl.BlockSpec((3, 8, A_W), lambda i: (0, 0, 0))],
        out_shape=[jax.ShapeDtypeStruct((R, 3 * A_W), F32), jax.ShapeDtypeStruct((3, 8, A_W), F32),
                   jax.ShapeDtypeStruct((B_CONV, 8, A_W), F32), jax.ShapeDtypeStruct((3, 8, A_W), F32)],
        scratch_shapes=[pltpu.VMEM((TMR + 2 * HALO, A_W), F32), pltpu.VMEM((TMR, A_W), F32)],
        compiler_params=_params(48))(dyab, p, p, p, wa, wb, bias, lng, lnb)


def mixer_in_bwd(dmid, p, wa, wb, w_in, x, nw, mod_n, dxr, y, mod_g, kn, kg, T):
    R, wp = p.shape
    dm = x.shape[1]
    _, kb, nb = w_in.shape
    with_resid = y is not None
    contract_last = (((1,), (1,)), ((), ()))

    def body(*refs):
        if with_resid:
            (mp_ref, m_ref, mn_ref, p_ref, wa_ref, wb_ref, w_ref, x_ref, nw_ref, mnorm_ref, dxr_ref, y_ref, mg_ref,
             dp_ref, dx_ref, dy_ref, dmod_ref, dnw_ref, dg_ref) = refs
        else:
            (mp_ref, m_ref, mn_ref, p_ref, wa_ref, wb_ref, w_ref, x_ref, nw_ref, mnorm_ref, dxr_ref,
             dp_ref, dx_ref, dmod_ref, dnw_ref) = refs
        i = pl.program_id(0)
        seg = _seg(i, T)
        live = _halo_live(i, T, R)

        @pl.when(i == 0)
        def _():
            dmod_ref[...] = jnp.zeros_like(dmod_ref)
            dnw_ref[...] = jnp.zeros_like(dnw_ref)
            if with_resid:
                dg_ref[...] = jnp.zeros_like(dg_ref)

        def block(q):
            return lax.dot_general(dp_ref[:, q * nb:(q + 1) * nb], w_ref[q], contract_last,
                                   preferred_element_type=F32)
        mrefs = (mp_ref, m_ref, mn_ref)
        dp_ref[:, 0:A_W] = m_ref[:, 0:A_W].astype(BF16)
        dca = _ext(mrefs, A_W, A_W, live)
        dcin = (wa_ref[pl.ds(0, 1), :] * _at(dca, 1) + wa_ref[pl.ds(1, 1), :] * _at(dca, 0)
                + wa_ref[pl.ds(2, 1), :] * _at(dca, -1))
        dp_ref[:, A_W:2 * A_W] = (dcin * p_ref[:, 2 * A_W:3 * A_W].astype(F32)).astype(BF16)
        dp_ref[:, 2 * A_W:3 * A_W] = (dcin * p_ref[:, A_W:2 * A_W].astype(F32)).astype(BF16)
        dhv = block(0) + block(1)
        for cc in range(A_W // _CW):
            c0 = cc * _CW
            d = _ext(mrefs, 2 * A_W + c0, _CW, live)
            dub = jnp.zeros((TMR, _CW), F32)
            for k in range(B_CONV):
                dub = dub + wb_ref[pl.ds(k, 1), c0:c0 + _CW] * _at(d, B_CONV // 2 - k)
            vb = p_ref[:, 3 * A_W + c0:3 * A_W + c0 + _CW].astype(F32)
            s = _sigmoid(p_ref[:, 4 * A_W + c0:4 * A_W + c0 + _CW].astype(F32))
            dp_ref[:, 3 * A_W + c0:3 * A_W + c0 + _CW] = (dub * s).astype(BF16)
            dp_ref[:, 4 * A_W + c0:4 * A_W + c0 + _CW] = (dub * vb * s * (1.0 - s)).astype(BF16)
        dhv = dhv + block(2) + block(3)
        sc = mnorm_ref[seg, pl.ds(kn + 1, 1), :]
        nwv = nw_ref[...]
        xv = x_ref[...]
        r = lax.rsqrt(jnp.mean(xv * xv, axis=-1, keepdims=True) + EPS)
        xh = xv * r
        dmod_ref[seg, 0] += _colsum8(dhv)
        dmod_ref[seg, 1] += _colsum8(dhv * (xh * nwv))
        dn = dhv * (1.0 + sc)
        dnw_ref[...] += _colsum8(dn * xh)
        dxh = dn * nwv
        dx = dxr_ref[...] + r * (dxh - xh * jnp.mean(dxh * xh, axis=-1, keepdims=True))
        dx_ref[...] = dx
        if with_resid:
            dy_ref[...] = (mg_ref[seg, pl.ds(kg, 1), :] * dx).astype(BF16)
            dg_ref[seg] += _colsum8(dx * y_ref[...])
    assert 2 * nb <= 3 * A_W and N_CHIP * nb == wp
    tile = pl.BlockSpec((TMR, dm), lambda i: (i, 0))
    modspec = pl.BlockSpec((2, 6, dm), lambda i: (0, 0, 0))
    in_specs = [*_halo_specs(3 * A_W, R), pl.BlockSpec((TMR, wp), lambda i: (i, 0)),
                pl.BlockSpec((3, A_W), lambda i: (0, 0)), pl.BlockSpec((B_CONV, A_W), lambda i: (0, 0)),
                pl.BlockSpec((N_CHIP, kb, nb), lambda i: (0, 0, 0), pipeline_mode=pl.Buffered(1)),
                tile, pl.BlockSpec((1, dm), lambda i: (0, 0)), modspec, tile]
    dp_spec = pl.BlockSpec((TMR, wp), lambda i: (i, 0))
    acc_specs = [pl.BlockSpec((2, 2, 8, dm), lambda i: (0, 0, 0, 0)), pl.BlockSpec((8, dm), lambda i: (0, 0))]
    acc_shapes = [jax.ShapeDtypeStruct((2, 2, 8, dm), F32), jax.ShapeDtypeStruct((8, dm), F32)]
    dp_shape, dx_shape = jax.ShapeDtypeStruct((R, wp), BF16), jax.ShapeDtypeStruct((R, dm), F32)
    if with_resid:
        return pl.pallas_call(
            body, name="mixer_in_bwd", grid=(R // TMR,), in_specs=in_specs + [tile, modspec],
            out_specs=[dp_spec, tile, tile] + acc_specs + [pl.BlockSpec((2, 8, dm), lambda i: (0, 0, 0))],
            out_shape=[dp_shape, dx_shape, jax.ShapeDtypeStruct((R, dm), BF16)] + acc_shapes
            + [jax.ShapeDtypeStruct((2, 8, dm), F32)],
            compiler_params=_params(48))(dmid, dmid, dmid, p, wa, wb, w_in, x, nw, mod_n, dxr, y, mod_g)
    return pl.pallas_call(
        body, name="mixer_in_bwd_first", grid=(R // TMR,), in_specs=in_specs,
        out_specs=[dp_spec, tile] + acc_specs, out_shape=[dp_shape, dx_shape] + acc_shapes,
        compiler_params=_params(48))(dmid, dmid, dmid, p, wa, wb, w_in, x, nw, mod_n, dxr)


def _rot_half(v):
    w = v.shape[-1]
    lane = lax.broadcasted_iota(jnp.int32, (1, w), 1)
    return jnp.where(lane % HEAD_DIM < HEAD_DIM // 2, pltpu.roll(v, w - HEAD_DIM // 2, 1),
                     pltpu.roll(v, HEAD_DIM // 2, 1))


def mm_qkv_rope(a, w, cs, sn):
    R = a.shape[0]
    _, kb, nb = w.shape
    wq = N_CHIP * nb
    tm = _row_tile(R)
    qw = N_HEADS * HEAD_DIM
    kw = (wq - qw) // 2
    scale = HEAD_DIM ** -0.5

    def body(a_ref, w_ref, cs_ref, sn_ref, o_ref, x_ref):
        av = a_ref[...].astype(BF16)
        for q in range(N_CHIP):
            x_ref[:, q * nb:(q + 1) * nb] = jnp.dot(av, w_ref[q], preferred_element_type=F32)
        c, s = cs_ref[...], sn_ref[...]
        q = x_ref[:, 0:qw]
        o_ref[:, 0:qw] = ((q * jnp.tile(c, (1, qw // 128)) + _rot_half(q) * jnp.tile(s, (1, qw // 128)))
                          * scale).astype(BF16)
        k = x_ref[:, qw:qw + kw]
        o_ref[:, qw:qw + kw] = (k * jnp.tile(c, (1, kw // 128))
                                + _rot_half(k) * jnp.tile(s, (1, kw // 128))).astype(BF16)
        o_ref[:, qw + kw:] = x_ref[:, qw + kw:].astype(BF16)
    tab = pl.BlockSpec((tm, 128), lambda i: (i, 0))
    return pl.pallas_call(
        body, name="mm_qkv_rope", grid=(R // tm,),
        in_specs=[pl.BlockSpec((tm, kb), lambda i: (i, 0)),
                  pl.BlockSpec((N_CHIP, kb, nb), lambda i: (0, 0, 0), pipeline_mode=pl.Buffered(1)), tab, tab],
        out_specs=pl.BlockSpec((tm, wq), lambda i: (i, 0)),
        out_shape=jax.ShapeDtypeStruct((R, wq), BF16), scratch_shapes=[pltpu.VMEM((tm, wq), F32)],
        compiler_params=_params(48))(a, w, cs, sn)


def attn_in_bwd(dq, dks, dvs, dkc, dvc, cs, sn, w, x, nw, mod_n, dxr, y, mod_g, kn, kg, T):
    R, qw = dq.shape
    kw = dkc.shape[1]
    dm = x.shape[1]
    _, kb, nbw = w.shape
    nb = R // QB
    nl = T // QB
    scale = HEAD_DIM ** -0.5
    contract_last = (((1,), (1,)), ((), ()))

    def body(dq_ref, kp_ref, ko_ref, kn_ref, vp_ref, vo_ref, vn_ref, kc_ref, vc_ref, cs_ref, sn_ref,
             w_ref, x_ref, nw_ref, mnorm_ref, dxr_ref, y_ref, mg_ref,
             o_ref, dx_ref, dy_ref, dmod_ref, dnw_ref, dg_ref):
        b = pl.program_id(0)
        seg = (b >= nl).astype(jnp.int32)

        @pl.when(b == 0)
        def _():
            dmod_ref[...] = jnp.zeros_like(dmod_ref)
            dnw_ref[...] = jnp.zeros_like(dnw_ref)
            dg_ref[...] = jnp.zeros_like(dg_ref)
        c, s = cs_ref[...], sn_ref[...]
        has_next = (b + 1 < nb).astype(F32)
        has_prev = (b >= 1).astype(F32)
        is_ctx = (b >= nl).astype(F32)
        g = dq_ref[...] * scale
        o_ref[:, 0:qw] = (g * jnp.tile(c, (1, qw // 128)) + _rot_half(g * jnp.tile(s, (1, qw // 128)))).astype(BF16)
        g = ko_ref[...] + kp_ref[...] * has_next + kn_ref[...] * has_prev + kc_ref[...] * is_ctx
        o_ref[:, qw:qw + kw] = (g * jnp.tile(c, (1, kw // 128))
                                + _rot_half(g * jnp.tile(s, (1, kw // 128)))).astype(BF16)
        o_ref[:, qw + kw:] = (vo_ref[...] + vp_ref[...] * has_next + vn_ref[...] * has_prev
                              + vc_ref[...] * is_ctx).astype(BF16)
        dhv = None
        for q in range(N_CHIP):
            t = lax.dot_general(o_ref[:, q * nbw:(q + 1) * nbw], w_ref[q], contract_last,
                                preferred_element_type=F32)
            dhv = t if dhv is None else dhv + t
        sc = mnorm_ref[seg, pl.ds(kn + 1, 1), :]
        nwv = nw_ref[...]
        xv = x_ref[...]
        r = lax.rsqrt(jnp.mean(xv * xv, axis=-1, keepdims=True) + EPS)
        xh = xv * r
        dmod_ref[seg, 0] += _colsum8(dhv)
        dmod_ref[seg, 1] += _colsum8(dhv * (xh * nwv))
        dn = dhv * (1.0 + sc)
        dnw_ref[...] += _colsum8(dn * xh)
        dxh = dn * nwv
        dx = dxr_ref[...] + r * (dxh - xh * jnp.mean(dxh * xh, axis=-1, keepdims=True))
        dx_ref[...] = dx
        dy_ref[...] = (mg_ref[seg, pl.ds(kg, 1), :] * dx).astype(BF16)
        dg_ref[seg] += _colsum8(dx * y_ref[...])
    own = pl.BlockSpec((QB, kw), lambda b: (b, 0))
    from_next = pl.BlockSpec((QB, kw), lambda b: (jnp.minimum(b + 1, nb - 1), 0))
    from_prev = pl.BlockSpec((QB, kw), lambda b: (jnp.maximum(b - 1, 0), 0))
    ctx = pl.BlockSpec((QB, kw), lambda b: (jnp.maximum(b - nl, 0), 0))
    tab = pl.BlockSpec((QB, 128), lambda b: (b, 0))
    tile = pl.BlockSpec((QB, dm), lambda b: (b, 0))
    modspec = pl.BlockSpec((2, 6, dm), lambda b: (0, 0, 0))
    return pl.pallas_call(
        body, name="attn_in_bwd", grid=(nb,),
        in_specs=[pl.BlockSpec((QB, qw), lambda b: (b, 0)), from_next, own, from_prev, from_next, own, from_prev,
                  ctx, ctx, tab, tab,
                  pl.BlockSpec((N_CHIP, kb, nbw), lambda b: (0, 0, 0), pipeline_mode=pl.Buffered(1)),
                  tile, pl.BlockSpec((1, dm), lambda b: (0, 0)), modspec, tile, tile, modspec],
        out_specs=[pl.BlockSpec((QB, qw + 2 * kw), lambda b: (b, 0)), tile, tile,
                   pl.BlockSpec((2, 2, 8, dm), lambda b: (0, 0, 0, 0)), pl.BlockSpec((8, dm), lambda b: (0, 0)),
                   pl.BlockSpec((2, 8, dm), lambda b: (0, 0, 0))],
        out_shape=[jax.ShapeDtypeStruct((R, qw + 2 * kw), BF16), jax.ShapeDtypeStruct((R, dm), F32),
                   jax.ShapeDtypeStruct((R, dm), BF16), jax.ShapeDtypeStruct((2, 2, 8, dm), F32),
                   jax.ShapeDtypeStruct((8, dm), F32), jax.ShapeDtypeStruct((2, 8, dm), F32)],
        compiler_params=_params(48))(
            dq, dks[0], dks[1], dks[2], dvs[0], dvs[1], dvs[2], dkc, dvc, cs, sn, w, x, nw, mod_n, dxr, y, mod_g)


def _attn_specs(T, R):
    nl = T // QB
    qcols = N_HEADS * HEAD_DIM // 128
    kcols = 2

    def band(col0, shift):
        return pl.BlockSpec((QB, 128), lambda jj, b: (jnp.clip(b + shift, 0, nl - 1), col0 + jj))

    def ctx(col0):
        return pl.BlockSpec((R - T, 128), lambda jj, b: (T // (R - T), col0 + jj))
    q = pl.BlockSpec((QB, 512), lambda jj, b: (b, jj))
    k0, v0 = qcols, qcols + kcols
    return q, [band(k0, -1), band(k0, 0), band(k0, 1), ctx(k0)], [band(v0, -1), band(v0, 0), band(v0, 1), ctx(v0)]


def _attn_common(T, R):
    nl = T // QB
    nk = 3 * QB + (R - T)

    def low_lanes():
        return lax.broadcasted_iota(jnp.int32, (1, 128), 1) < HEAD_DIM

    def dup(v, par):
        low = low_lanes()
        vf = v.astype(F32)
        r = pltpu.roll(vf, HEAD_DIM, 1)
        return (jnp.where(low, vf, r) if par == 0 else jnp.where(low, r, vf)).astype(BF16)

    def stack(ref, par):
        low = low_lanes()
        pa = ref[:, (2 * par) * 128:(2 * par + 1) * 128].astype(BF16)
        pb = ref[:, (2 * par + 1) * 128:(2 * par + 2) * 128].astype(BF16)
        zero = jnp.zeros_like(pa)
        return jnp.concatenate([jnp.where(low, pa, zero), jnp.where(low, zero, pa),
                                jnp.where(low, pb, zero), jnp.where(low, zero, pb)], axis=0)

    def unstack(v):
        low = low_lanes()
        return (jnp.where(low, v[0:QB], v[QB:2 * QB]), jnp.where(low, v[2 * QB:3 * QB], v[3 * QB:4 * QB]))

    def mask_of(b):
        col = lax.broadcasted_iota(jnp.int32, (1, nk), 1)
        gone = (((col < QB) & (b == 0)) | ((col >= 2 * QB) & (col < 3 * QB) & (b == nl - 1))
                | ((col < 3 * QB) & (b >= nl)))
        return jnp.where(gone, NEG_INF, 0.0)

    def sink_col(sink_ref, first):
        blk = lax.broadcasted_iota(jnp.int32, (4 * QB, 1), 0) // QB
        out = jnp.zeros((4 * QB, 1), F32) + sink_ref[first]
        for h in range(1, 4):
            out = jnp.where(blk == h, sink_ref[first + h], out)
        return out

    def scores(qs, kd, mask, sink):
        s = lax.dot_general(qs, kd, (((1,), (1,)), ((), ())), preferred_element_type=F32) + mask
        m = jnp.maximum(jnp.max(s, axis=-1, keepdims=True), sink)
        e = jnp.exp(s - m)
        es = jnp.exp(sink - m)
        return e, es, 1.0 / (jnp.sum(e, axis=-1, keepdims=True) + es)
    return low_lanes, dup, stack, unstack, mask_of, sink_col, scores


def window_bias(T, R):
    nk = 3 * QB + (R - T)
    row = jnp.arange(QB)[:, None]
    col = jnp.arange(nk)[None, :]
    near = (jnp.abs(col - QB - row) <= WINDOW) | (col >= 3 * QB)
    return jnp.tile(jnp.where(near, 0.0, NEG_INF).astype(F32), (4, 1))


def _probs_spec(nk):
    return pl.BlockSpec((None, None, 2, 4 * QB, nk + 128), lambda jj, b: (jj, b, 0, 0, 0))


def attn_fwd(qkvr, sinks, bias, T):
    R = qkvr.shape[0]
    nk = bias.shape[1]
    qspec, kspecs, vspecs = _attn_specs(T, R)
    _, dup, stack, unstack, mask_of, sink_col, scores = _attn_common(T, R)

    def body(q_ref, kp, ko, kn, kc, vp, vo, vn, vc, sink_ref, bias_ref, o_ref, p_ref):
        jj, b = pl.program_id(0), pl.program_id(1)
        mask = bias_ref[...] + mask_of(b)
        k_all = jnp.concatenate([kp[...], ko[...], kn[...], kc[...]], axis=0)
        v_all = jnp.concatenate([vp[...], vo[...], vn[...], vc[...]], axis=0)
        for par in range(2):
            kd, vd = dup(k_all, par), dup(v_all, par)
            e, es, rz = scores(stack(q_ref, par), kd, mask, sink_col(sink_ref, jj * 8 + par * 4))
            p = (e * rz).astype(BF16)
            p_ref[par, :, 0:nk] = p
            p_ref[par, :, nk:nk + 128] = jnp.broadcast_to(es * rz, (4 * QB, 128)).astype(BF16)
            o = jnp.dot(p, vd, preferred_element_type=F32)
            pa, pb = unstack(o)
            o_ref[:, (2 * par) * 128:(2 * par + 1) * 128] = pa.astype(BF16)
            o_ref[:, (2 * par + 1) * 128:(2 * par + 2) * 128] = pb.astype(BF16)
    return pl.pallas_call(
        body, name="attn_fwd", grid=(2, R // QB),
        in_specs=[qspec, *kspecs, *vspecs, pl.BlockSpec(memory_space=pltpu.SMEM),
                  pl.BlockSpec(bias.shape, lambda jj, b: (0, 0))],
        out_specs=[pl.BlockSpec((QB, 512), lambda jj, b: (b, jj)), _probs_spec(nk)],
        out_shape=[jax.ShapeDtypeStruct((R, N_HEADS * HEAD_DIM), BF16),
                   jax.ShapeDtypeStruct((2, R // QB, 2, 4 * QB, nk + 128), BF16)],
        compiler_params=_params(48))(qkvr, *([qkvr] * 8), sinks, bias)


def attn_bwd(qkvr, do, probs, T):
    R = qkvr.shape[0]
    tc = R - T
    nk = probs.shape[-1] - 128
    qspec, kspecs, vspecs = _attn_specs(T, R)
    _, dup, stack, unstack, _, _, _ = _attn_common(T, R)
    contract_rows = (((0,), (0,)), ((), ()))
    contract_last = (((1,), (1,)), ((), ()))

    def body(q_ref, kp, ko, kn, kc, vp, vo, vn, vc, do_ref, p_ref,
             dq_ref, dkp, dko, dkn, dvp, dvo, dvn, dkc_ref, dvc_ref, dsink_ref):
        jj, b = pl.program_id(0), pl.program_id(1)

        @pl.when((jj == 0) & (b == 0))
        def _():
            dsink_ref[...] = jnp.zeros_like(dsink_ref)

        @pl.when(b == 0)
        def _():
            dkc_ref[...] = jnp.zeros_like(dkc_ref)
            dvc_ref[...] = jnp.zeros_like(dvc_ref)
        k_all = jnp.concatenate([kp[...], ko[...], kn[...], kc[...]], axis=0)
        v_all = jnp.concatenate([vp[...], vo[...], vn[...], vc[...]], axis=0)
        lane = lax.broadcasted_iota(jnp.int32, (8, 128), 1)
        srow = lax.broadcasted_iota(jnp.int32, (8, 128), 0)
        dk_fold, dv_fold = [], []
        for par in range(2):
            kd, vd = dup(k_all, par), dup(v_all, par)
            first = jj * 8 + par * 4
            qs, dos = stack(q_ref, par), stack(do_ref, par)
            p16 = p_ref[par, :, 0:nk]
            p = p16.astype(F32)
            ps = jnp.max(p_ref[par, :, nk:nk + 128].astype(F32), axis=-1, keepdims=True)
            dp = lax.dot_general(dos, vd, contract_last, preferred_element_type=F32)
            delta = jnp.sum(p * dp, axis=-1, keepdims=True)
            ds = (p * (dp - delta)).astype(BF16)
            t = ps * delta
            for h in range(4):
                dsink = -jnp.sum(t[h * QB:(h + 1) * QB])
                dsink_ref[...] += jnp.where((lane == first + h) & (srow == 0), dsink, 0.0)
            pa, pb = unstack(jnp.dot(ds, kd, preferred_element_type=F32))
            dq_ref[:, (2 * par) * 128:(2 * par + 1) * 128] = pa
            dq_ref[:, (2 * par + 1) * 128:(2 * par + 2) * 128] = pb
            dk_t = lax.dot_general(qs, ds, contract_rows, preferred_element_type=F32)
            dv_t = lax.dot_general(dos, p16, contract_rows, preferred_element_type=F32)
            dk_fold.append(dk_t + pltpu.roll(dk_t, HEAD_DIM, 0))
            dv_fold.append(dv_t + pltpu.roll(dv_t, HEAD_DIM, 0))
        low_rows = lax.broadcasted_iota(jnp.int32, (128, 1), 0) < HEAD_DIM
        dk = jnp.where(low_rows, dk_fold[0], dk_fold[1]).T
        dv = jnp.where(low_rows, dv_fold[0], dv_fold[1]).T
        dkp[...], dko[...], dkn[...] = dk[0:QB], dk[QB:2 * QB], dk[2 * QB:3 * QB]
        dvp[...], dvo[...], dvn[...] = dv[0:QB], dv[QB:2 * QB], dv[2 * QB:3 * QB]
        dkc_ref[...] += dk[3 * QB:]
        dvc_ref[...] += dv[3 * QB:]
    blk = pl.BlockSpec((QB, 128), lambda jj, b: (b, jj))
    cblk = pl.BlockSpec((tc, 128), lambda jj, b: (0, jj))
    part = jax.ShapeDtypeStruct((R, 256), F32)
    csum = jax.ShapeDtypeStruct((tc, 256), F32)
    outs = pl.pallas_call(
        body, name="attn_bwd", grid=(2, R // QB),
        in_specs=[qspec, *kspecs, *vspecs, pl.BlockSpec((QB, 512), lambda jj, b: (b, jj)), _probs_spec(nk)],
        out_specs=[pl.BlockSpec((QB, 512), lambda jj, b: (b, jj)), blk, blk, blk, blk, blk, blk, cblk, cblk,
                   pl.BlockSpec((8, 128), lambda jj, b: (0, 0))],
        out_shape=[jax.ShapeDtypeStruct((R, N_HEADS * HEAD_DIM), F32), part, part, part, part, part, part,
                   csum, csum, jax.ShapeDtypeStruct((8, 128), F32)],
        compiler_params=_params(48))(qkvr, *([qkvr] * 8), do, probs)
    return outs[0], outs[1:4], outs[4:7], outs[7], outs[8], outs[9]


def loss_head(x, nw, target, y, mod, kg, T):
    R, dm = x.shape
    nl = T // TMR

    def body(x_ref, nw_ref, t_ref, y_ref, mod_ref, loss_ref, dx_ref, dnw_ref, dy_ref, dg_ref):
        i = pl.program_id(0)
        seg = _seg(i, T)

        @pl.when(i == 0)
        def _():
            loss_ref[...] = jnp.zeros_like(loss_ref)
            dnw_ref[...] = jnp.zeros_like(dnw_ref)
            dg_ref[...] = jnp.zeros_like(dg_ref)
        live = (i < nl).astype(F32)
        nwv = nw_ref[...]
        xv = x_ref[...]
        r = lax.rsqrt(jnp.mean(xv * xv, axis=-1, keepdims=True) + EPS)
        xh = xv * r
        err = xh * nwv - t_ref[...]
        per_row = jnp.mean(err * err, axis=-1, keepdims=True)
        loss_ref[...] += 0.5 * live * jnp.sum(per_row)
        dy = err * (live / dm)
        dnw_ref[...] += _colsum8(dy * xh)
        dxh = dy * nwv
        dx = r * (dxh - xh * jnp.mean(dxh * xh, axis=-1, keepdims=True))
        dx_ref[...] = dx
        dy_ref[...] = (mod_ref[seg, pl.ds(kg, 1), :] * dx).astype(BF16)
        dg_ref[seg] += _colsum8(dx * y_ref[...])
    tile = pl.BlockSpec((TMR, dm), lambda i: (i, 0))
    return pl.pallas_call(
        body, name="loss_head", grid=(R // TMR,),
        in_specs=[tile, pl.BlockSpec((1, dm), lambda i: (0, 0)),
                  pl.BlockSpec((TMR, dm), lambda i: (jnp.minimum(i, nl - 1), 0)), tile,
                  pl.BlockSpec((2, 6, dm), lambda i: (0, 0, 0))],
        out_specs=[pl.BlockSpec((8, 128), lambda i: (0, 0)), tile, pl.BlockSpec((8, dm), lambda i: (0, 0)), tile,
                   pl.BlockSpec((2, 8, dm), lambda i: (0, 0, 0))],
        out_shape=[jax.ShapeDtypeStruct((8, 128), F32), jax.ShapeDtypeStruct((R, dm), F32),
                   jax.ShapeDtypeStruct((8, dm), F32), jax.ShapeDtypeStruct((R, dm), BF16),
                   jax.ShapeDtypeStruct((2, 8, dm), F32)])(x, nw, target, y, mod)


def adaln_fwd(cond, w_mod, b_mod):
    nl, dm, ns = w_mod.shape

    def body(c_ref, w_ref, b_ref, o_ref):
        cv = c_ref[...]
        s = (cv * _sigmoid(cv)).astype(BF16)
        o_ref[...] = jnp.dot(s, w_ref[...].astype(BF16), preferred_element_type=F32) + b_ref[...]
    return pl.pallas_call(
        body, name="adaln_fwd", grid=(nl,),
        in_specs=[pl.BlockSpec((16, dm), lambda l: (0, 0)), pl.BlockSpec((None, dm, ns), lambda l: (l, 0, 0)),
                  pl.BlockSpec((None, 1, ns), lambda l: (l, 0, 0))],
        out_specs=pl.BlockSpec((None, 16, ns), lambda l: (l, 0, 0)),
        out_shape=jax.ShapeDtypeStruct((nl, 16, ns), F32), compiler_params=_params(48))(cond, w_mod, b_mod)


def adaln_bwd(cond, dmod, w_mod):
    nl, dm, ns = w_mod.shape

    def body(c_ref, d_ref, w_ref, gw_ref, ds_ref):
        l = pl.program_id(0)

        @pl.when(l == 0)
        def _():
            ds_ref[...] = jnp.zeros_like(ds_ref)
        cv = c_ref[...]
        s = (cv * _sigmoid(cv)).astype(BF16)
        dv = d_ref[...].astype(BF16)
        gw_ref[...] = lax.dot_general(s, dv, (((0,), (0,)), ((), ())), preferred_element_type=F32)
        ds_ref[...] += lax.dot_general(dv, w_ref[...].astype(BF16), (((1,), (1,)), ((), ())),
                                       preferred_element_type=F32)
    return pl.pallas_call(
        body, name="adaln_bwd", grid=(nl,),
        in_specs=[pl.BlockSpec((16, dm), lambda l: (0, 0)), pl.BlockSpec((None, 16, ns), lambda l: (l, 0, 0)),
                  pl.BlockSpec((None, dm, ns), lambda l: (l, 0, 0))],
        out_specs=[pl.BlockSpec((None, dm, ns), lambda l: (l, 0, 0)), pl.BlockSpec((16, dm), lambda l: (0, 0))],
        out_shape=[jax.ShapeDtypeStruct((nl, dm, ns), F32), jax.ShapeDtypeStruct((16, dm), F32)],
        compiler_params=_params(48))(cond, dmod, w_mod)


def _me():
    return lax.axis_index("x"), lax.axis_index("y"), lax.axis_index("c")


def allgather8(block):
    m_per, n = block.shape

    def body(x_ref, out_ref, send_sems, recv_sems, local_sem):
        x, y, c = _me()
        me, sibling = (x, y, c), (x, y, 1 - c)
        chips = [(1 - x, y), (x, 1 - y), (1 - x, 1 - y)]

        def rows(px, py, pc):
            return out_ref.at[pl.ds((4 * px + 2 * py + pc) * m_per, m_per), :]

        def copy(k, blk, to, src=None):
            return pltpu.make_async_remote_copy(
                src_ref=rows(*blk) if src is None else src, dst_ref=rows(*blk),
                send_sem=send_sems.at[k], recv_sem=recv_sems.at[k], device_id=to, device_id_type=MESH)
        mine = pltpu.make_async_copy(x_ref, rows(*me), local_sem)
        mine.start()
        first = [copy(0, me, sibling, src=x_ref)]
        first += [copy(1 + j, me, (*chip, c), src=x_ref) for j, chip in enumerate(chips)]
        for cp in first:
            cp.start()
        passed = [copy(4 + j, (*chip, c), sibling) for j, chip in enumerate(chips)]
        for j, chip in enumerate(chips):
            copy(1 + j, (*chip, c), me).wait_recv()
            passed[j].start()
        copy(0, sibling, me).wait_recv()
        for j, chip in enumerate(chips):
            copy(4 + j, (*chip, 1 - c), me).wait_recv()
        for cp in first + passed:
            cp.wait_send()
        mine.wait()
    return pl.pallas_call(
        body, name="allgather8",
        out_shape=jax.ShapeDtypeStruct((N_DEV * m_per, n), block.dtype),
        in_specs=[pl.BlockSpec(memory_space=pltpu.VMEM)],
        out_specs=pl.BlockSpec(memory_space=pltpu.VMEM),
        scratch_shapes=[pltpu.SemaphoreType.DMA((7,)), pltpu.SemaphoreType.DMA((7,)), pltpu.SemaphoreType.DMA],
        compiler_params=_params(48))(block)


def _other_chips(x, y):
    return [(1 - x, y), (x, 1 - y), (1 - x, 1 - y)]


_HBM = pl.BlockSpec(memory_space=pltpu.HBM)
_SEM = pl.BlockSpec(memory_space=pltpu.SEMAPHORE)
_ANY = pl.BlockSpec(memory_space=pl.ANY)
_EFFECT = pltpu.SideEffectType.DATAFLOW_SIDE_EFFECTING


def _in_hbm(v):
    return pltpu.with_memory_space_constraint(v, pltpu.HBM)


def cast_into_slot(w, layer, chip_id):
    _, kb, nb = w.shape
    tr = _row_tile(kb)

    def body(chip_ref, w_ref, o_ref):
        del chip_ref
        o_ref[...] = w_ref[...].astype(BF16)
    return pl.pallas_call(
        body, name="cast_into_slot",
        grid_spec=pltpu.PrefetchScalarGridSpec(
            num_scalar_prefetch=1, grid=(kb // tr,),
            in_specs=[pl.BlockSpec((None, tr, nb), lambda i, chip: (layer, i, 0))],
            out_specs=pl.BlockSpec((None, tr, nb), lambda i, chip: (chip[0], i, 0))),
        out_shape=jax.ShapeDtypeStruct((N_CHIP, kb, nb), BF16))(chip_id, w)


def _split_copies(mode, srcs, lands, send_sems, recv_sems):
    x, y, c = _me()
    out = []
    for t in range(len(lands)):
        for k, chip in enumerate(_other_chips(x, y)):
            if mode == "gather":
                src = dst = lands[t].at[2 * x + y]
                landed = lands[t].at[2 * chip[0] + chip[1]]
            else:
                src, dst, landed = srcs[t].at[2 * chip[0] + chip[1]], lands[t].at[k], lands[t].at[k]
            send = pltpu.make_async_remote_copy(src_ref=src, dst_ref=dst, send_sem=send_sems.at[3 * t + k],
                                                recv_sem=recv_sems.at[3 * t + k], device_id=(*chip, c),
                                                device_id_type=MESH)
            recv = pltpu.make_async_remote_copy(src_ref=src, dst_ref=landed, send_sem=send_sems.at[3 * t + k],
                                                recv_sem=recv_sems.at[3 * t + k], device_id=(*chip, c),
                                                device_id_type=MESH)
            out.append((send, recv))
    return out


def exchange_start(name, mode, srcs, lands, after):
    ns, nl = len(srcs), len(lands)
    na = ns + nl

    def body(*refs):
        src_refs, land_refs = refs[:ns], refs[ns:na]
        send_sems, recv_sems = refs[na + 1], refs[na + 2]
        token = refs[-1]
        for send, _ in _split_copies(mode, src_refs, land_refs, send_sems, recv_sems):
            send.start()
        token[...] = jnp.zeros_like(token)
    arrays = list(srcs) + list(lands)
    outs = pl.pallas_call(
        body, name=name,
        out_shape=(pltpu.SemaphoreType.DMA((3 * nl,)), pltpu.SemaphoreType.DMA((3 * nl,)),
                   *[pltpu.HBM(v.shape, v.dtype) for v in arrays], jax.ShapeDtypeStruct((8, 128), F32)),
        in_specs=[_HBM] * na + [_ANY],
        out_specs=(_SEM, _SEM, *[_HBM] * na, pl.BlockSpec(memory_space=pltpu.VMEM)),
        input_output_aliases={i: 2 + i for i in range(na)},
        compiler_params=pltpu.CompilerParams(has_side_effects=_EFFECT))(*[_in_hbm(v) for v in arrays], after)
    return outs[0], outs[1], list(outs[2:2 + ns]), list(outs[2 + ns:2 + na]), outs[-1]


def exchange_wait(name, mode, send_sems, recv_sems, srcs, lands, after):
    ns, nl = len(srcs), len(lands)
    na = ns + nl

    def body(*refs):
        for _, recv in _split_copies(mode, refs[:ns], refs[ns:na], refs[na], refs[na + 1]):
            recv.wait_send()
            recv.wait_recv()
    arrays = list(srcs) + list(lands)
    outs = pl.pallas_call(
        body, name=name,
        out_shape=[pltpu.HBM(v.shape, v.dtype) for v in arrays],
        in_specs=[_HBM] * na + [_SEM, _SEM, _ANY], out_specs=[_HBM] * na,
        input_output_aliases={i: i for i in range(na)},
        compiler_params=pltpu.CompilerParams(has_side_effects=_EFFECT))(*arrays, send_sems, recv_sems, after)
    return list(outs[:ns]), list(outs[ns:])


def swap_with_sibling(vs):
    n = len(vs)

    def body(*refs):
        v_refs, out_refs, send_sems, recv_sems = refs[:n], refs[n:2 * n], refs[2 * n], refs[2 * n + 1]
        x, y, c = _me()
        cps = [pltpu.make_async_remote_copy(src_ref=v_refs[t], dst_ref=out_refs[t], send_sem=send_sems.at[t],
                                            recv_sem=recv_sems.at[t], device_id=(x, y, 1 - c), device_id_type=MESH)
               for t in range(n)]
        for cp in cps:
            cp.start()
        for cp in cps:
            cp.wait()
    return pl.pallas_call(
        body, name="swap_with_sibling", out_shape=[jax.ShapeDtypeStruct(v.shape, v.dtype) for v in vs],
        in_specs=[_ANY] * n, out_specs=[_ANY] * n,
        scratch_shapes=[pltpu.SemaphoreType.DMA((n,)), pltpu.SemaphoreType.DMA((n,))])(*vs)


def sum_slots(parts):
    n, rows, w = parts.shape
    tr = _row_tile(rows)

    def body(p_ref, o_ref):
        acc = p_ref[0].astype(F32)
        for k in range(1, n):
            acc = acc + p_ref[k].astype(F32)
        o_ref[...] = acc
    return pl.pallas_call(
        body, name="sum_slots", grid=(rows // tr,),
        in_specs=[pl.BlockSpec((n, tr, w), lambda i: (0, i, 0))], out_specs=pl.BlockSpec((tr, w), lambda i: (i, 0)),
        out_shape=jax.ShapeDtypeStruct((rows, w), F32), compiler_params=_params(48))(parts)


def sum_landed(landed, own, chip_id, layer, n_layers, buf):
    n, rows, w = landed.shape
    tr = _row_tile(rows)
    base = layer * (rows // tr)

    def compute(l_ref, g_ref, o_ref):
        acc = g_ref[...].astype(F32)
        for k in range(n):
            acc = acc + l_ref[k].astype(F32)
        o_ref[...] = acc
    in_specs = [pl.BlockSpec((n, tr, w), lambda i, chip: (0, i, 0)),
                pl.BlockSpec((None, tr, w), lambda i, chip: (chip[0], i, 0))]
    out_spec = pl.BlockSpec((tr, w), lambda i, chip: (base + i, 0))
    out_shape = jax.ShapeDtypeStruct((n_layers * rows, w), F32)
    if buf is None:
        def body(chip_ref, l_ref, g_ref, o_ref):
            del chip_ref
            compute(l_ref, g_ref, o_ref)
        return pl.pallas_call(
            body, name="sum_landed",
            grid_spec=pltpu.PrefetchScalarGridSpec(num_scalar_prefetch=1, grid=(rows // tr,), in_specs=in_specs,
                                                   out_specs=out_spec),
            out_shape=out_shape, compiler_params=_params(48))(chip_id, landed, own)

    def body(chip_ref, l_ref, g_ref, buf_ref, o_ref):
        del chip_ref, buf_ref
        compute(l_ref, g_ref, o_ref)
    return pl.pallas_call(
        body, name="sum_landed_into",
        grid_spec=pltpu.PrefetchScalarGridSpec(num_scalar_prefetch=1, grid=(rows // tr,),
                                               in_specs=in_specs + [_ANY], out_specs=out_spec),
        out_shape=out_shape, input_output_aliases={3: 0}, compiler_params=_params(48))(chip_id, landed, own, buf)


def adamw(w, ga, gb, m, v):
    rows, wd = w.shape
    tr = min(_row_tile(rows), 128)
    c1 = 1.0 / (1.0 - ADAM_B1 ** ADAM_STEP)
    c2 = 1.0 / (1.0 - ADAM_B2 ** ADAM_STEP)

    def update(wv, g, mv, vv, g_ref, d_ref, m_ref, v_ref):
        mn = ADAM_B1 * mv + (1.0 - ADAM_B1) * g
        vn = ADAM_B2 * vv + (1.0 - ADAM_B2) * (g * g)
        g_ref[...] = g
        m_ref[...] = mn
        v_ref[...] = vn
        d_ref[...] = -ADAM_LR * ((mn * c1) / (jnp.sqrt(vn * c2) + ADAM_EPS) + ADAM_WD * wv)
    tile = pl.BlockSpec((tr, wd), lambda i: (i, 0))
    out = jax.ShapeDtypeStruct((rows, wd), F32)
    if gb is None:
        def body(w_ref, ga_ref, m_ref, v_ref, g_out, d_out, m_out, v_out):
            update(w_ref[...], ga_ref[...], m_ref[...], v_ref[...], g_out, d_out, m_out, v_out)
        return pl.pallas_call(body, name="adamw", grid=(rows // tr,), in_specs=[tile] * 4,
                              out_specs=[tile] * 4, out_shape=[out] * 4)(w, ga, m, v)

    def body(w_ref, ga_ref, gb_ref, m_ref, v_ref, g_out, d_out, m_out, v_out):
        update(w_ref[...], ga_ref[...] + gb_ref[...], m_ref[...], v_ref[...], g_out, d_out, m_out, v_out)
    return pl.pallas_call(body, name="adamw_sum", grid=(rows // tr,), in_specs=[tile] * 5,
                          out_specs=[tile] * 4, out_shape=[out] * 4)(w, ga, gb, m, v)


def _rope_tables(T, R):
    rows = T // GRID_W
    row = jnp.repeat(jnp.arange(rows), GRID_W).astype(F32)
    col = jnp.tile(jnp.arange(GRID_W), rows).astype(F32)
    n_freq = HEAD_DIM // 4
    inv_freq = ROPE_THETA ** (-jnp.arange(n_freq, dtype=F32) / n_freq)
    ang = jnp.concatenate([row[:, None] * inv_freq, col[:, None] * inv_freq], axis=-1)
    cos, sin = jnp.cos(ang), jnp.sin(ang)
    cs = jnp.tile(cos, (1, 4))
    sn = jnp.tile(jnp.concatenate([-sin, sin], axis=-1), (1, 2))
    pad = R - T
    return (jnp.concatenate([cs, jnp.ones((pad, 128), F32)], axis=0),
            jnp.concatenate([sn, jnp.zeros((pad, 128), F32)], axis=0))


def _pack(parts, mult=8 * 128):
    flat = jnp.concatenate([p.reshape(-1).astype(F32) for p in parts])
    pad = (-flat.shape[0]) % mult
    return jnp.pad(flat, (0, pad)).reshape(-1, 128)


def _unpack(buf, shapes):
    flat = buf.reshape(-1)
    out, o = [], 0
    for s in shapes:
        n = 1
        for d in s:
            n *= d
        out.append(flat[o:o + n].reshape(s))
        o += n
    return out


def kernel(x, c, ctx, c_ctx, w_mod, b_mod, norm_mix, norm_ffn, w_in_ab, conv_a, conv_b, conv_b_bias, ln_b_gain, ln_b_bias, w_out_ab, w_qkv, w_o, sinks, w_up, w_conv_ffn, w_down, final_norm, loss_target, m_c_ctx, m_w_mod, m_b_mod, m_norm_mix, m_norm_ffn, m_w_in_ab, m_conv_a, m_conv_b, m_conv_b_bias, m_ln_b_gain, m_ln_b_bias, m_w_out_ab, m_w_qkv, m_w_o, m_sinks, m_w_up, m_w_conv_ffn, m_w_down, m_final_norm, v_c_ctx, v_w_mod, v_b_mod, v_norm_mix, v_norm_ffn, v_w_in_ab, v_conv_a, v_conv_b, v_conv_b_bias, v_ln_b_gain, v_ln_b_bias, v_w_out_ab, v_w_qkv, v_w_o, v_sinks, v_w_up, v_w_conv_ffn, v_w_down, v_final_norm):
    T, dm = x.shape[1], x.shape[2]
    tc = ctx.shape[1]
    R = T + tc
    depth = w_mod.shape[0]
    ax, ay, ac = lax.axis_index("x"), lax.axis_index("y"), lax.axis_index("c")
    chip = 2 * ax + ay
    dev = 4 * ax + 2 * ay + ac

    small_w = [conv_a, conv_b, w_conv_ffn]
    gathered = allgather8(_pack([c] + small_w)).reshape(N_DEV, -1)
    cond8 = gathered[:, :dm]
    off = dm
    full_small = []
    for wsh in small_w:
        n = wsh.size
        per_chip = gathered[0::2, off:off + n].reshape((N_CHIP,) + wsh.shape)
        full_small.append(jnp.concatenate([per_chip[q] for q in range(N_CHIP)], axis=-1))
        off += n
    conv_a_f, conv_b_f, w_conv_ffn_f = full_small
    cond = jnp.concatenate([cond8, c_ctx[None, :], jnp.zeros((7, dm), F32)], axis=0)

    ns_mod = w_mod.shape[2]
    b_mod_sh = lax.dynamic_slice_in_dim(b_mod, chip * ns_mod, ns_mod, axis=1)[:, None, :]
    mod_sh = adaln_fwd(cond, w_mod, b_mod_sh)
    mod_all = allgather8(mod_sh.reshape(depth * 16, ns_mod)).reshape(N_DEV, depth, 16, ns_mod)
    mod_full = jnp.concatenate([mod_all[2 * q] for q in range(N_CHIP)], axis=-1)
    mine = lax.dynamic_index_in_dim(mod_full, dev, axis=1, keepdims=False)
    mods = jnp.stack([mine, mod_full[:, 8]], axis=1).reshape(depth, 2, 6, dm)

    masters = {"w_in_ab": w_in_ab, "w_out_ab": w_out_ab, "w_qkv": w_qkv, "w_o": w_o, "w_up": w_up, "w_down": w_down}
    chip_id = chip.astype(jnp.int32).reshape(1)

    def half_weights(l, half):
        if half == 1:
            return [("w_up", l), ("w_down", l)]
        return [("w_in_ab", l // 2), ("w_out_ab", l // 2)] if l % 2 == 0 else [("w_qkv", l // 2), ("w_o", l // 2)]
    in_flight, after = {}, mods
    for l in range(depth):
        for half in range(2):
            lands = [cast_into_slot(masters[n], j, chip_id) for n, j in half_weights(l, half)]
            send_sems, recv_sems, _, lands, after = exchange_start(f"gather_start_{l}_{half}", "gather", [], lands, after)
            in_flight[l, half] = (send_sems, recv_sems, lands)
    mods = mods + after[0, 0]

    def gathered_weights(l, half, after):
        send_sems, recv_sems, lands = in_flight[l, half]
        _, landed = exchange_wait(f"gather_wait_{l}_{half}", "gather", send_sems, recv_sems, [], lands, after)
        return dict(zip([n for n, _ in half_weights(l, half)], landed))

    cs, sn = _rope_tables(T, R)
    bias = window_bias(T, R)
    sinks_flat = sinks.reshape(-1)

    xs = jnp.concatenate([x[0], ctx[0]], axis=0)
    saved, W = [], []
    h1 = norm_mod_fwd(xs, norm_mix[0][None], mods[0], 0, T)
    for l in range(depth):
        e = l // 2
        wl = gathered_weights(l, 0, h1)
        W.append(wl)
        s = {"x0": xs, "h1": h1}
        if l % 2 == 0:
            p = mm_nn(h1, wl["w_in_ab"], BF16)
            yab, y1, x1, h2 = mixer_fwd(p, conv_a_f[e], conv_b_f[e], conv_b_bias[e][None], ln_b_gain[e][None],
                                        ln_b_bias[e][None], wl["w_out_ab"], xs, norm_ffn[l][None], mods[l], 2, 3, T)
            s.update(p=p, mix=yab)
        else:
            qkvr = mm_qkv_rope(h1, wl["w_qkv"], cs, sn)
            att, probs = attn_fwd(qkvr, sinks_flat[e * N_HEADS:(e + 1) * N_HEADS], bias, T)
            s.update(qkvr=qkvr, mix=att, probs=probs)
            y1, x1, h2 = mm_resid_norm_fwd(att, wl["w_o"], xs, norm_ffn[l][None], mods[l], mods[l], 2, 3, T)
        wl.update(gathered_weights(l, 1, h2))
        u = mm_nn(h2, wl["w_up"], BF16)
        if l + 1 < depth:
            z, y2, xs, h1 = ffn_mid_fwd(u, w_conv_ffn_f[l], wl["w_down"], x1, norm_mix[l + 1][None], mods[l],
                                        mods[l + 1], 5, 0, T)
        else:
            z, y2, xs = ffn_mid_fwd(u, w_conv_ffn_f[l], wl["w_down"], x1, None, mods[l], None, 5, 0, T)
        s.update(y1=y1, x1=x1, h2=h2, u=u, z=z, y2=y2)
        saved.append(s)

    loss_part, dx, d_final, dy2, dg2_last = loss_head(xs, final_norm[None], loss_target[0], saved[depth - 1]["y2"],
                                                      mods[depth - 1], 5, T)
    loss = lax.psum(loss_part[0, 0], ("x", "y", "c"))

    d_mods, d_norm_mix, d_norm_ffn = [None] * depth, [None] * depth, [None] * depth
    d_conv_a, d_conv_b, d_vecs, d_sinks, d_wc = [None] * 2, [None] * 2, [None] * 2, [None] * 2, [None] * depth
    dss1, dss2, dg1, dg2 = [None] * depth, [None] * depth, [None] * depth, [None] * depth
    scattering = {}

    def scatter(l, half, G, after):
        grads_h = [G[n] for n, _ in half_weights(l, half)]
        lands = [lax.empty((N_CHIP - 1, *g.shape[1:]), g.dtype) for g in grads_h]
        send_sems, recv_sems, grads_h, lands, token = exchange_start(
            f"scatter_start_{l}_{half}", "scatter", grads_h, lands, after)
        scattering[l, half] = (send_sems, recv_sems, grads_h, lands)
        return token

    dg2[depth - 1] = dg2_last
    pending = 0.0
    for l in reversed(range(depth)):
        e = l // 2
        s, wl = saved[l], W[l]
        G = {}
        G["w_down"] = mm_tn(s["z"], dy2, "row", wl["w_down"])
        duc, d_wc[l] = ffn_mid_bwd(mm_nt(dy2, wl["w_down"], BF16), s["u"], w_conv_ffn_f[l] + pending, T)
        du, dx, dy1, dss2[l], d_norm_ffn[l], dg1[l] = ffn_up_bwd(
            duc, w_conv_ffn_f[l], wl["w_up"], s["x1"], norm_ffn[l][None], mods[l], dx, s["y1"], mods[l], 3, 2, T)
        G["w_up"] = mm_tn(s["h2"], du, "col", wl["w_up"])
        started = scatter(l, 1, G, du)[0, 0]
        if l % 2 == 0:
            G["w_out_ab"] = mm_tn(s["mix"], dy1, "row", wl["w_out_ab"])
            dyab = mm_nt(dy1, wl["w_out_ab"], F32)
            dmid, d_conv_a[e], d_conv_b[e], d_vecs[e] = convmix_bwd1(
                dyab, s["p"], conv_a_f[e] + started, conv_b_f[e], conv_b_bias[e][None], ln_b_gain[e][None],
                ln_b_bias[e][None], T)
            if l > 0:
                dp, dx, dy2, dss1[l], d_norm_mix[l], dg2[l - 1] = mixer_in_bwd(
                    dmid, s["p"], conv_a_f[e], conv_b_f[e], wl["w_in_ab"], s["x0"], norm_mix[l][None], mods[l], dx,
                    saved[l - 1]["y2"], mods[l - 1], 0, 5, T)
            else:
                dp, dx, dss1[l], d_norm_mix[l] = mixer_in_bwd(
                    dmid, s["p"], conv_a_f[e], conv_b_f[e], wl["w_in_ab"], s["x0"], norm_mix[l][None], mods[l], dx,
                    None, None, 0, 0, T)
            G["w_in_ab"] = mm_tn(s["h1"], dp, "col", wl["w_in_ab"])
        else:
            G["w_o"] = mm_tn(s["mix"], dy1, "row", wl["w_o"])
            datt = mm_nt(dy1, wl["w_o"], BF16)
            dq, dks, dvs, dkc, dvc, d_sinks[e] = attn_bwd(s["qkvr"], datt, s["probs"], T)
            dqkv, dx, dy2, dss1[l], d_norm_mix[l], dg2[l - 1] = attn_in_bwd(
                dq, dks, dvs, dkc, dvc, cs + started, sn, wl["w_qkv"], s["x0"], norm_mix[l][None], mods[l], dx,
                saved[l - 1]["y2"], mods[l - 1], 0, 5, T)
            G["w_qkv"] = mm_tn(s["h1"], dqkv, "col", wl["w_qkv"])
        if l > 0:
            pending = scatter(l, 0, G, dx)[0, 0]
    grad_x = dx[:T][None]
    for l in range(depth):
        a1, a2 = dss1[l].sum(2), dss2[l].sum(2)
        d_mods[l] = jnp.stack([a1[:, 0], a1[:, 1], dg1[l].sum(1), a2[:, 0], a2[:, 1], dg2[l].sum(1)], axis=1)

    d_mods = jnp.stack(d_mods)
    summed_parts = [
        d_mods[:, 1],
        jnp.stack(d_norm_mix).sum(1), jnp.stack(d_norm_ffn).sum(1),
        jnp.stack(d_conv_a).sum(2), jnp.stack(d_conv_b).sum(2),
        jnp.stack(d_vecs).sum(2),
        jnp.stack(d_sinks)[:, 0, :N_HEADS],
        jnp.stack(d_wc).sum(2), d_final.sum(0) + pending]
    summed_shapes = [p.shape for p in summed_parts]
    n_own = depth * 6 * dm
    pack = _pack([d_mods[:, 0]] + summed_parts)
    parts = allgather8(pack).reshape(N_DEV, -1, 128)
    scatter(0, 0, G, parts)
    total = sum_slots(parts)
    own_rows = parts.reshape(N_DEV, -1)[:, :n_own].reshape(N_DEV, depth, 6 * dm)
    (dmod_ctx, g_norm_mix, g_norm_ffn, g_conv_a, g_conv_b, g_vecs, g_sinks, g_wc, g_final) = _unpack(
        total.reshape(-1)[n_own:], summed_shapes)
    dmod_rows = jnp.concatenate([jnp.moveaxis(own_rows, 0, 1), dmod_ctx.reshape(depth, 1, 6 * dm),
                                 jnp.zeros((depth, 7, 6 * dm), F32)], axis=1)
    g_b_mod = dmod_rows.sum(1)
    dmod_sh = lax.dynamic_slice_in_dim(dmod_rows, chip * ns_mod, ns_mod, axis=2)
    g_w_mod, dsilu = adaln_bwd(cond, dmod_sh, w_mod)
    dsilu_all = allgather8(dsilu[8:16]).reshape(N_DEV, 8, dm)
    dsilu_ctx = sum_slots(dsilu_all[0::2])[0]
    sg = jax.nn.sigmoid(c_ctx)
    g_c_ctx = dsilu_ctx * (sg * (1.0 + c_ctx * (1.0 - sg)))

    def shard_cols(full, width):
        return lax.dynamic_slice_in_dim(full, chip * width, width, axis=full.ndim - 1)
    g_conv_a_s = shard_cols(g_conv_a, conv_a.shape[-1])
    g_conv_b_s = shard_cols(g_conv_b, conv_b.shape[-1])
    g_wc_s = shard_cols(g_wc, w_conv_ffn.shape[-1])

    grads, deltas, new_m, new_v = {}, {}, {}, {}

    def step_2d(name, wv, ga, gb, mv, vv):
        shp = wv.shape
        r2 = lambda t: t.reshape(-1, shp[-1])
        g, d, mn, vn = adamw(r2(wv), r2(ga), None if gb is None else r2(gb), r2(mv), r2(vv))
        grads[name], deltas[name], new_m[name], new_v[name] = (t.reshape(shp) for t in (g, d, mn, vn))

    step_2d("w_mod", w_mod, g_w_mod, None, m_w_mod, v_w_mod)
    sums = {n: None for n in masters}
    for l in reversed(range(depth)):
        for half in (1, 0):
            send_sems, recv_sems, grads_h, lands = scattering[l, half]
            grads_h, landed = exchange_wait(f"scatter_wait_{l}_{half}", "scatter", send_sems, recv_sems, grads_h,
                                            lands, deltas["w_mod"])
            for (n, j), own, arr in zip(half_weights(l, half), grads_h, landed):
                sums[n] = sum_landed(arr, own, chip_id, j, masters[n].shape[0], sums[n])
    moments = {"w_in_ab": (m_w_in_ab, v_w_in_ab), "w_out_ab": (m_w_out_ab, v_w_out_ab),
               "w_qkv": (m_w_qkv, v_w_qkv), "w_o": (m_w_o, v_w_o), "w_up": (m_w_up, v_w_up),
               "w_down": (m_w_down, v_w_down)}
    others = swap_with_sibling([sums[name] for name in masters])
    for (name, wv), other in zip(masters.items(), others):
        step_2d(name, wv, sums[name].reshape(wv.shape), other.reshape(wv.shape), *moments[name])

    small = [("c_ctx", c_ctx, g_c_ctx, m_c_ctx, v_c_ctx), ("b_mod", b_mod, g_b_mod, m_b_mod, v_b_mod),
             ("norm_mix", norm_mix, g_norm_mix, m_norm_mix, v_norm_mix),
             ("norm_ffn", norm_ffn, g_norm_ffn, m_norm_ffn, v_norm_ffn),
             ("conv_a", conv_a, g_conv_a_s, m_conv_a, v_conv_a), ("conv_b", conv_b, g_conv_b_s, m_conv_b, v_conv_b),
             ("conv_b_bias", conv_b_bias, g_vecs[:, 0], m_conv_b_bias, v_conv_b_bias),
             ("ln_b_gain", ln_b_gain, g_vecs[:, 1], m_ln_b_gain, v_ln_b_gain),
             ("ln_b_bias", ln_b_bias, g_vecs[:, 2], m_ln_b_bias, v_ln_b_bias),
             ("sinks", sinks, g_sinks, m_sinks, v_sinks),
             ("w_conv_ffn", w_conv_ffn, g_wc_s, m_w_conv_ffn, v_w_conv_ffn),
             ("final_norm", final_norm, g_final, m_final_norm, v_final_norm)]
    shapes = [t[1].shape for t in small]
    packed = [_pack([t[k] for t in small]) for k in (1, 2, 3, 4)]
    n_real = sum(t[1].size for t in small)
    lane_id = jnp.arange(packed[3].size).reshape(packed[3].shape)
    packed[3] = jnp.where(lane_id < n_real, packed[3], 1.0)
    outs = adamw(packed[0], packed[1], None, packed[2], packed[3])
    for (name, *_), g, d, mn, vn in zip(small, *[_unpack(o, shapes) for o in outs]):
        grads[name], deltas[name], new_m[name], new_v[name] = g, d, mn, vn

    order = ["c_ctx", "w_mod", "b_mod", "norm_mix", "norm_ffn", "w_in_ab", "conv_a", "conv_b", "conv_b_bias",
             "ln_b_gain", "ln_b_bias", "w_out_ab", "w_qkv", "w_o", "sinks", "w_up", "w_conv_ffn", "w_down",
             "final_norm"]
    return (loss, grad_x, *[grads[n] for n in order], *[deltas[n] for n in order],
            *[new_m[n] for n in order], *[new_v[n] for n in order])
```

```python
import jax
import jax.numpy as jnp
from jax import lax
from jax.experimental import pallas as pl
from jax.experimental.pallas import tpu as pltpu

F32 = jnp.float32
BF16 = jnp.bfloat16
MESH = pl.DeviceIdType.MESH

EPS = 1e-6
NEG_INF = -1e30
GRID_W = 64
HEAD_DIM = 64
N_HEADS = 16
WINDOW = 128
QB = 128
ROPE_THETA = 10000.0
A_W = 512
B_CONV = 31
D_FF = 2816
ADAM_LR, ADAM_B1, ADAM_B2, ADAM_EPS, ADAM_WD, ADAM_STEP = 0.001, 0.9, 0.999, 1e-8, 0.01, 10

TMR = 256
HALO = 16
N_DEV = 8
N_CHIP = 4


def _params(vmem_mb=None):
    if vmem_mb is None:
        return pltpu.CompilerParams()
    return pltpu.CompilerParams(vmem_limit_bytes=vmem_mb * 1024 * 1024)


def _row_tile(rows, cap=768):
    for t in (2816, 1408, 768, 704, 512, 384, 256, 128, 64, 32, 16, 8):
        if t <= cap and rows % t == 0:
            return t
    raise ValueError(rows)


def _colsum8(v):
    r, c = v.shape
    return v.reshape(r // 8, 8, c).sum(axis=0)


def _sigmoid(v):
    return 0.5 * jnp.tanh(0.5 * v) + 0.5


def mm_nn(a, w, out_dtype):
    R = a.shape[0]
    _, kb, nb = w.shape
    tm = _row_tile(R)

    def body(a_ref, w_ref, o_ref):
        av = a_ref[...].astype(BF16)
        for q in range(N_CHIP):
            o_ref[:, q * nb:(q + 1) * nb] = jnp.dot(av, w_ref[q], preferred_element_type=F32).astype(o_ref.dtype)
    return pl.pallas_call(
        body, name="mm_nn_col", grid=(R // tm,),
        in_specs=[pl.BlockSpec((tm, kb), lambda i: (i, 0)),
                  pl.BlockSpec((N_CHIP, kb, nb), lambda i: (0, 0, 0), pipeline_mode=pl.Buffered(1))],
        out_specs=pl.BlockSpec((tm, N_CHIP * nb), lambda i: (i, 0)),
        out_shape=jax.ShapeDtypeStruct((R, N_CHIP * nb), out_dtype),
        compiler_params=_params(48))(a, w)


def mm_nt(d, w, out_dtype):
    R = d.shape[0]
    _, kb, nb = w.shape
    tm = _row_tile(R)
    contract_last = (((1,), (1,)), ((), ()))
    resident = pl.BlockSpec((N_CHIP, kb, nb), lambda i: (0, 0, 0), pipeline_mode=pl.Buffered(1))

    def body(d_ref, w_ref, o_ref):
        wv = w_ref[...].reshape(N_CHIP * kb, nb)
        o_ref[...] = lax.dot_general(d_ref[...].astype(BF16), wv, contract_last,
                                     preferred_element_type=F32).astype(o_ref.dtype)
    return pl.pallas_call(
        body, name="mm_nt_row", grid=(R // tm,),
        in_specs=[pl.BlockSpec((tm, nb), lambda i: (i, 0)), resident],
        out_specs=pl.BlockSpec((tm, N_CHIP * kb), lambda i: (i, 0)),
        out_shape=jax.ShapeDtypeStruct((R, N_CHIP * kb), out_dtype),
        compiler_params=_params(48))(d, w)


def mm_tn(a, d, kind, like):
    R = a.shape[0]
    _, kb, nb = like.shape
    tm = _row_tile(R, 1408 if kind == "col" else 768)
    nsteps = R // tm
    contract_rows = (((0,), (0,)), ((), ()))
    out_shape = jax.ShapeDtypeStruct(like.shape, BF16)

    def accumulate(a_ref, d_ref, acc_ref):
        @pl.when(pl.program_id(1) == 0)
        def _():
            acc_ref[...] = jnp.zeros_like(acc_ref)
        acc_ref[...] += lax.dot_general(a_ref[...].astype(BF16), d_ref[...].astype(BF16), contract_rows,
                                        preferred_element_type=F32)
    if kind == "col":
        def body(a_ref, d_ref, o_ref, acc_ref):
            accumulate(a_ref, d_ref, acc_ref)

            @pl.when(pl.program_id(1) == nsteps - 1)
            def _():
                o_ref[...] = acc_ref[...].astype(BF16)
        return pl.pallas_call(
            body, name="mm_tn_col", grid=(N_CHIP, nsteps),
            in_specs=[pl.BlockSpec((tm, kb), lambda q, i: (i, 0)), pl.BlockSpec((tm, nb), lambda q, i: (i, q))],
            out_specs=pl.BlockSpec((None, kb, nb), lambda q, i: (q, 0, 0)), out_shape=out_shape,
            scratch_shapes=[pltpu.VMEM((kb, nb), F32)], compiler_params=_params(48))(a, d)
    tn = 512

    def body(a_ref, d_ref, o_ref, acc_ref):
        accumulate(a_ref, d_ref, acc_ref)

        @pl.when(pl.program_id(1) == nsteps - 1)
        def _():
            o_ref[...] = acc_ref[...].astype(BF16).reshape(N_CHIP, kb, tn)
    return pl.pallas_call(
        body, name="mm_tn_row", grid=(nb // tn, nsteps),
        in_specs=[pl.BlockSpec((tm, N_CHIP * kb), lambda n, i: (i, 0)), pl.BlockSpec((tm, tn), lambda n, i: (i, n))],
        out_specs=pl.BlockSpec((N_CHIP, kb, tn), lambda n, i: (0, 0, n)), out_shape=out_shape,
        scratch_shapes=[pltpu.VMEM((N_CHIP * kb, tn), F32)], compiler_params=_params(48))(a, d)


def _seg(i, T):
    return (i >= T // TMR).astype(jnp.int32)


def norm_mod_fwd(x, nw, mod, k, T):
    R, dm = x.shape

    def body(x_ref, nw_ref, mod_ref, h_ref):
        seg = _seg(pl.program_id(0), T)
        sh = mod_ref[seg, pl.ds(k, 1), :]
        sc = mod_ref[seg, pl.ds(k + 1, 1), :]
        xv = x_ref[...]
        r = lax.rsqrt(jnp.mean(xv * xv, axis=-1, keepdims=True) + EPS)
        h_ref[...] = ((xv * r * nw_ref[...]) * (1.0 + sc) + sh).astype(BF16)
    return pl.pallas_call(
        body, name="norm_mod_fwd", grid=(R // TMR,),
        in_specs=[pl.BlockSpec((TMR, dm), lambda i: (i, 0)),
                  pl.BlockSpec((1, dm), lambda i: (0, 0)),
                  pl.BlockSpec((2, 6, dm), lambda i: (0, 0, 0))],
        out_specs=pl.BlockSpec((TMR, dm), lambda i: (i, 0)),
        out_shape=jax.ShapeDtypeStruct((R, dm), BF16))(x, nw, mod)


def mm_resid_norm_fwd(a, w, x, nw, mod_g, mod_n, kg, kn, T):
    R, dm = x.shape
    _, kb, nb = w.shape

    def body(a_ref, w_ref, x_ref, nw_ref, mg_ref, mn_ref, y_ref, xo_ref, h_ref):
        seg = _seg(pl.program_id(0), T)
        yv = jnp.dot(a_ref[...].astype(BF16), w_ref[...].reshape(N_CHIP * kb, nb), preferred_element_type=F32)
        y_ref[...] = yv
        xv = x_ref[...] + mg_ref[seg, pl.ds(kg, 1), :] * yv
        xo_ref[...] = xv
        r = lax.rsqrt(jnp.mean(xv * xv, axis=-1, keepdims=True) + EPS)
        h_ref[...] = ((xv * r * nw_ref[...]) * (1.0 + mn_ref[seg, pl.ds(kn + 1, 1), :])
                      + mn_ref[seg, pl.ds(kn, 1), :]).astype(BF16)
    tile = pl.BlockSpec((TMR, dm), lambda i: (i, 0))
    modspec = pl.BlockSpec((2, 6, dm), lambda i: (0, 0, 0))
    return pl.pallas_call(
        body, name="mm_resid_norm_fwd", grid=(R // TMR,),
        in_specs=[pl.BlockSpec((TMR, N_CHIP * kb), lambda i: (i, 0)),
                  pl.BlockSpec((N_CHIP, kb, nb), lambda i: (0, 0, 0), pipeline_mode=pl.Buffered(1)),
                  tile, pl.BlockSpec((1, dm), lambda i: (0, 0)), modspec, modspec],
        out_specs=[tile, tile, tile],
        out_shape=[jax.ShapeDtypeStruct((R, dm), F32), jax.ShapeDtypeStruct((R, dm), F32),
                   jax.ShapeDtypeStruct((R, dm), BF16)],
        compiler_params=_params(48))(a, w, x, nw, mod_g, mod_n)


def _halo_specs(width, R):
    nblk = R // HALO
    per = TMR // HALO
    return (pl.BlockSpec((HALO, width), lambda i: (jnp.maximum(i * per - 1, 0), 0)),
            pl.BlockSpec((TMR, width), lambda i: (i, 0)),
            pl.BlockSpec((HALO, width), lambda i: (jnp.minimum((i + 1) * per, nblk - 1), 0)))


def _halo_live(i, T, R):
    nl = T // TMR
    return (i != 0) & (i != nl), (i != nl - 1) & (i != R // TMR - 1)


def _ext(refs, c0, cw, live, halo=HALO):
    pref, ref, nref = refs
    before = jnp.where(live[0], pref[:, c0:c0 + cw].astype(F32)[HALO - halo:], 0.0)
    after = jnp.where(live[1], nref[:, c0:c0 + cw].astype(F32)[:halo], 0.0)
    return jnp.concatenate([before, ref[:, c0:c0 + cw].astype(F32), after], axis=0)


def _at(ext, off, halo=HALO):
    n = ext.shape[0]
    s = (-off) % n
    y = pltpu.roll(ext, s, 0) if s else ext
    return y[halo:halo + TMR]


def ffn_mid_fwd(u, wc, w_down, x, nw, mod_g, mod_n, kg, kn, T):
    R, w2 = u.shape
    dm = x.shape[1]
    _, kb, nb = w_down.shape
    cw = 256
    with_norm = nw is not None

    def body(*refs):
        if with_norm:
            up_ref, u_ref, un_ref, wc_ref, w_ref, x_ref, nw_ref, mg_ref, mn_ref, z_ref, y_ref, xo_ref, h_ref = refs
        else:
            up_ref, u_ref, un_ref, wc_ref, w_ref, x_ref, mg_ref, z_ref, y_ref, xo_ref = refs
        i = pl.program_id(0)
        seg = _seg(i, T)
        live = _halo_live(i, T, R)

        def conv(c0):
            e = _ext((up_ref, u_ref, un_ref), c0, cw, live, 8)
            return (wc_ref[pl.ds(0, 1), c0:c0 + cw] * _at(e, -1, 8) + wc_ref[pl.ds(1, 1), c0:c0 + cw] * _at(e, 0, 8)
                    + wc_ref[pl.ds(2, 1), c0:c0 + cw] * _at(e, 1, 8))
        yv = None
        for j in range(D_FF // cw):
            a = conv(j * cw)
            g = conv(D_FF + j * cw)
            zc = (g * _sigmoid(g) * a).astype(BF16)
            z_ref[:, j * cw:(j + 1) * cw] = zc
            t = jnp.dot(zc, w_ref[j * cw:(j + 1) * cw, :], preferred_element_type=F32)
            yv = t if yv is None else yv + t
        y_ref[...] = yv
        xv = x_ref[...] + mg_ref[seg, pl.ds(kg, 1), :] * yv
        xo_ref[...] = xv
        if with_norm:
            r = lax.rsqrt(jnp.mean(xv * xv, axis=-1, keepdims=True) + EPS)
            h_ref[...] = ((xv * r * nw_ref[...]) * (1.0 + mn_ref[seg, pl.ds(kn + 1, 1), :])
                          + mn_ref[seg, pl.ds(kn, 1), :]).astype(BF16)
    tile = pl.BlockSpec((TMR, dm), lambda i: (i, 0))
    modspec = pl.BlockSpec((2, 6, dm), lambda i: (0, 0, 0))
    w_down = w_down.reshape(N_CHIP * kb, nb)
    in_specs = [*_halo_specs(w2, R), pl.BlockSpec((3, w2), lambda i: (0, 0)),
                pl.BlockSpec((N_CHIP * kb, nb), lambda i: (0, 0), pipeline_mode=pl.Buffered(1)), tile]
    out_specs = [pl.BlockSpec((TMR, D_FF), lambda i: (i, 0)), tile, tile]
    out_shape = [jax.ShapeDtypeStruct((R, D_FF), BF16), jax.ShapeDtypeStruct((R, dm), F32),
                 jax.ShapeDtypeStruct((R, dm), F32)]
    if with_norm:
        return pl.pallas_call(
            body, name="ffn_mid_fwd", grid=(R // TMR,),
            in_specs=in_specs + [pl.BlockSpec((1, dm), lambda i: (0, 0)), modspec, modspec],
            out_specs=out_specs + [tile], out_shape=out_shape + [jax.ShapeDtypeStruct((R, dm), BF16)],
            compiler_params=_params(48))(u, u, u, wc, w_down, x, nw, mod_g, mod_n)
    return pl.pallas_call(
        body, name="ffn_mid_fwd_last", grid=(R // TMR,), in_specs=in_specs + [modspec],
        out_specs=out_specs, out_shape=out_shape, compiler_params=_params(48))(u, u, u, wc, w_down, x, mod_g)


def ffn_mid_bwd(dz, u, wc, T):
    R, w2 = u.shape
    cw = 256

    def body(dz_ref, up_ref, u_ref, un_ref, wc_ref, duc_ref, dwc_ref):
        i = pl.program_id(0)
        live = _halo_live(i, T, R)

        @pl.when(i == 0)
        def _():
            dwc_ref[...] = jnp.zeros_like(dwc_ref)

        def taps(c0):
            e = _ext((up_ref, u_ref, un_ref), c0, cw, live, 8)
            return [_at(e, -1, 8), _at(e, 0, 8), _at(e, 1, 8)]

        def conv(t, c0):
            return (wc_ref[pl.ds(0, 1), c0:c0 + cw] * t[0] + wc_ref[pl.ds(1, 1), c0:c0 + cw] * t[1]
                    + wc_ref[pl.ds(2, 1), c0:c0 + cw] * t[2])
        for j in range(D_FF // cw):
            ca, cg = j * cw, D_FF + j * cw
            dzv = dz_ref[:, ca:ca + cw].astype(F32)
            ta, tg = taps(ca), taps(cg)
            a, g = conv(ta, ca), conv(tg, cg)
            sg = _sigmoid(g)
            da = dzv * (g * sg)
            dg = dzv * a * (sg * (1.0 + g * (1.0 - sg)))
            duc_ref[:, ca:ca + cw] = da.astype(BF16)
            duc_ref[:, cg:cg + cw] = dg.astype(BF16)
            for k in range(3):
                dwc_ref[k, :, ca:ca + cw] += _colsum8(da * ta[k])
                dwc_ref[k, :, cg:cg + cw] += _colsum8(dg * tg[k])
    return pl.pallas_call(
        body, name="ffn_mid_bwd", grid=(R // TMR,),
        in_specs=[pl.BlockSpec((TMR, D_FF), lambda i: (i, 0)), *_halo_specs(w2, R),
                  pl.BlockSpec((3, w2), lambda i: (0, 0))],
        out_specs=[pl.BlockSpec((TMR, w2), lambda i: (i, 0)), pl.BlockSpec((3, 8, w2), lambda i: (0, 0, 0))],
        out_shape=[jax.ShapeDtypeStruct((R, w2), BF16), jax.ShapeDtypeStruct((3, 8, w2), F32)],
        compiler_params=_params(48))(dz, u, u, u, wc)


def ffn_up_bwd(duc, wc, w_up, x, nw, mod_n, dxr, y, mod_g, kn, kg, T):
    R, w2 = duc.shape
    dm = x.shape[1]
    _, kb, nb = w_up.shape
    cw = 128
    contract_last = (((1,), (1,)), ((), ()))

    def body(dp_ref, d_ref, dn_ref, wc_ref, w_ref, x_ref, nw_ref, mn_ref, dxr_ref, y_ref, mg_ref,
             du_ref, dx_ref, dy_ref, dmod_ref, dnw_ref, dg_ref):
        i = pl.program_id(0)
        seg = _seg(i, T)
        live = _halo_live(i, T, R)

        @pl.when(i == 0)
        def _():
            dmod_ref[...] = jnp.zeros_like(dmod_ref)
            dnw_ref[...] = jnp.zeros_like(dnw_ref)
            dg_ref[...] = jnp.zeros_like(dg_ref)
        dhv = None
        for q in range(N_CHIP):
            for j in range(nb // cw):
                c0 = q * nb + j * cw
                e = _ext((dp_ref, d_ref, dn_ref), c0, cw, live, 8)
                du_ref[:, c0:c0 + cw] = (wc_ref[pl.ds(0, 1), c0:c0 + cw] * _at(e, 1, 8)
                                         + wc_ref[pl.ds(1, 1), c0:c0 + cw] * _at(e, 0, 8)
                                         + wc_ref[pl.ds(2, 1), c0:c0 + cw] * _at(e, -1, 8)).astype(BF16)
            t = lax.dot_general(du_ref[:, q * nb:(q + 1) * nb], w_ref[q], contract_last,
                                preferred_element_type=F32)
            dhv = t if dhv is None else dhv + t
        sc = mn_ref[seg, pl.ds(kn + 1, 1), :]
        nwv = nw_ref[...]
        xv = x_ref[...]
        r = lax.rsqrt(jnp.mean(xv * xv, axis=-1, keepdims=True) + EPS)
        xh = xv * r
        dmod_ref[seg, 0] += _colsum8(dhv)
        dmod_ref[seg, 1] += _colsum8(dhv * (xh * nwv))
        dn = dhv * (1.0 + sc)
        dnw_ref[...] += _colsum8(dn * xh)
        dxh = dn * nwv
        dx = dxr_ref[...] + r * (dxh - xh * jnp.mean(dxh * xh, axis=-1, keepdims=True))
        dx_ref[...] = dx
        dy_ref[...] = (mg_ref[seg, pl.ds(kg, 1), :] * dx).astype(BF16)
        dg_ref[seg] += _colsum8(dx * y_ref[...])
    tile = pl.BlockSpec((TMR, dm), lambda i: (i, 0))
    modspec = pl.BlockSpec((2, 6, dm), lambda i: (0, 0, 0))
    return pl.pallas_call(
        body, name="ffn_up_bwd", grid=(R // TMR,),
        in_specs=[*_halo_specs(w2, R), pl.BlockSpec((3, w2), lambda i: (0, 0)),
                  pl.BlockSpec((N_CHIP, kb, nb), lambda i: (0, 0, 0), pipeline_mode=pl.Buffered(1)),
                  tile, pl.BlockSpec((1, dm), lambda i: (0, 0)), modspec, tile, tile, modspec],
        out_specs=[pl.BlockSpec((TMR, w2), lambda i: (i, 0)), tile, tile,
                   pl.BlockSpec((2, 2, 8, dm), lambda i: (0, 0, 0, 0)), pl.BlockSpec((8, dm), lambda i: (0, 0)),
                   pl.BlockSpec((2, 8, dm), lambda i: (0, 0, 0))],
        out_shape=[jax.ShapeDtypeStruct((R, w2), BF16), jax.ShapeDtypeStruct((R, dm), F32),
                   jax.ShapeDtypeStruct((R, dm), BF16), jax.ShapeDtypeStruct((2, 2, 8, dm), F32),
                   jax.ShapeDtypeStruct((8, dm), F32), jax.ShapeDtypeStruct((2, 8, dm), F32)],
        compiler_params=_params(48))(duc, duc, duc, wc, w_up, x, nw, mod_n, dxr, y, mod_g)


_CW = 128


def _mixer_a(prefs, wa_ref, live):
    cin = _ext(prefs, A_W, A_W, live) * _ext(prefs, 2 * A_W, A_W, live)
    ca = (wa_ref[pl.ds(0, 1), :] * _at(cin, -1) + wa_ref[pl.ds(1, 1), :] * _at(cin, 0)
          + wa_ref[pl.ds(2, 1), :] * _at(cin, 1))
    return cin, ca


def _mixer_b(prefs, wb_ref, bias_ref, live, ub_s, ub2_s):
    for cc in range(A_W // _CW):
        c0 = cc * _CW
        ub = _ext(prefs, 3 * A_W + c0, _CW, live) * _sigmoid(_ext(prefs, 4 * A_W + c0, _CW, live))
        ub_s[:, c0:c0 + _CW] = ub
        acc = jnp.zeros((TMR, _CW), F32) + bias_ref[:, c0:c0 + _CW]
        for k in range(B_CONV):
            acc = acc + wb_ref[pl.ds(k, 1), c0:c0 + _CW] * _at(ub, k - B_CONV // 2)
        ub2_s[:, c0:c0 + _CW] = acc


def _layernorm_stats(v):
    mu = jnp.mean(v, axis=-1, keepdims=True)
    xc = v - mu
    rs = lax.rsqrt(jnp.mean(xc * xc, axis=-1, keepdims=True) + EPS)
    return xc * rs, rs


def mixer_fwd(p, wa, wb, bias, lng, lnb, w_out, x, nw, mod, kg, kn, T):
    R, wp = p.shape
    dm = x.shape[1]
    _, kb, nb = w_out.shape

    def body(pp_ref, p_ref, pn_ref, wa_ref, wb_ref, bias_ref, lng_ref, lnb_ref, w_ref, x_ref, nw_ref, mod_ref,
             o_ref, y_ref, xo_ref, h_ref, ub_s, ub2_s):
        i = pl.program_id(0)
        seg = _seg(i, T)
        live = _halo_live(i, T, R)
        prefs = (pp_ref, p_ref, pn_ref)
        _, ca = _mixer_a(prefs, wa_ref, live)
        ya = (p_ref[:, 0:A_W].astype(F32) * ca).astype(BF16)
        o_ref[:, 0:A_W] = ya
        yv = jnp.dot(ya, w_ref[0:A_W, :], preferred_element_type=F32)
        _mixer_b(prefs, wb_ref, bias_ref, live, ub_s, ub2_s)
        xh, _ = _layernorm_stats(ub2_s[...])
        lv = xh * lng_ref[...] + lnb_ref[...]
        yb = (lv * _sigmoid(lv)).astype(BF16)
        o_ref[:, A_W:2 * A_W] = yb
        yv = yv + jnp.dot(yb, w_ref[A_W:2 * A_W, :], preferred_element_type=F32)
        y_ref[...] = yv
        xv = x_ref[...] + mod_ref[seg, pl.ds(kg, 1), :] * yv
        xo_ref[...] = xv
        r = lax.rsqrt(jnp.mean(xv * xv, axis=-1, keepdims=True) + EPS)
        h_ref[...] = ((xv * r * nw_ref[...]) * (1.0 + mod_ref[seg, pl.ds(kn + 1, 1), :])
                      + mod_ref[seg, pl.ds(kn, 1), :]).astype(BF16)
    vec = pl.BlockSpec((1, A_W), lambda i: (0, 0))
    tile = pl.BlockSpec((TMR, dm), lambda i: (i, 0))
    return pl.pallas_call(
        body, name="mixer_fwd", grid=(R // TMR,),
        in_specs=[*_halo_specs(wp, R), pl.BlockSpec((3, A_W), lambda i: (0, 0)),
                  pl.BlockSpec((B_CONV, A_W), lambda i: (0, 0)), vec, vec, vec,
                  pl.BlockSpec((N_CHIP * kb, nb), lambda i: (0, 0), pipeline_mode=pl.Buffered(1)),
                  tile, pl.BlockSpec((1, dm), lambda i: (0, 0)), pl.BlockSpec((2, 6, dm), lambda i: (0, 0, 0))],
        out_specs=[pl.BlockSpec((TMR, 2 * A_W), lambda i: (i, 0)), tile, tile, tile],
        out_shape=[jax.ShapeDtypeStruct((R, 2 * A_W), BF16), jax.ShapeDtypeStruct((R, dm), F32),
                   jax.ShapeDtypeStruct((R, dm), F32), jax.ShapeDtypeStruct((R, dm), BF16)],
        scratch_shapes=[pltpu.VMEM((TMR + 2 * HALO, A_W), F32), pltpu.VMEM((TMR, A_W), F32)],
        compiler_params=_params(48))(p, p, p, wa, wb, bias, lng, lnb, w_out.reshape(N_CHIP * kb, nb), x, nw, mod)


def convmix_bwd1(dyab, p, wa, wb, bias, lng, lnb, T):
    R, wp = p.shape

    def body(dy_ref, pp_ref, p_ref, pn_ref, wa_ref, wb_ref, bias_ref, lng_ref, lnb_ref,
             dmid_ref, dwa_ref, dwb_ref, dvec_ref, ub_s, ub2_s):
        i = pl.program_id(0)
        live = _halo_live(i, T, R)

        @pl.when(i == 0)
        def _():
            dwa_ref[...] = jnp.zeros_like(dwa_ref)
            dwb_ref[...] = jnp.zeros_like(dwb_ref)
            dvec_ref[...] = jnp.zeros_like(dvec_ref)
        prefs = (pp_ref, p_ref, pn_ref)
        cin, ca = _mixer_a(prefs, wa_ref, live)
        dya = dy_ref[:, 0:A_W]
        dmid_ref[:, 0:A_W] = dya * ca
        dca = dya * p_ref[:, 0:A_W].astype(F32)
        dmid_ref[:, A_W:2 * A_W] = dca
        for k in range(3):
            dwa_ref[k] += _colsum8(dca * _at(cin, k - 1))
        _mixer_b(prefs, wb_ref, bias_ref, live, ub_s, ub2_s)
        xh, rs = _layernorm_stats(ub2_s[...])
        gain = lng_ref[...]
        lv = xh * gain + lnb_ref[...]
        sl = _sigmoid(lv)
        dl = dy_ref[:, A_W:2 * A_W] * (sl * (1.0 + lv * (1.0 - sl)))
        dvec_ref[1] += _colsum8(dl * xh)
        dvec_ref[2] += _colsum8(dl)
        dxh = dl * gain
        dub2 = rs * (dxh - jnp.mean(dxh, axis=-1, keepdims=True)
                     - xh * jnp.mean(dxh * xh, axis=-1, keepdims=True))
        dvec_ref[0] += _colsum8(dub2)
        dmid_ref[:, 2 * A_W:3 * A_W] = dub2
        for cc in range(A_W // _CW):
            c0 = cc * _CW
            ub = ub_s[:, c0:c0 + _CW]
            d = dmid_ref[:, 2 * A_W + c0:2 * A_W + c0 + _CW]
            for k in range(B_CONV):
                dwb_ref[k, :, c0:c0 + _CW] += _colsum8(d * _at(ub, k - B_CONV // 2))
    vec = pl.BlockSpec((1, A_W), lambda i: (0, 0))
    return pl.pallas_call(
        body, name="convmix_bwd1", grid=(R // TMR,),
        in_specs=[pl.BlockSpec((TMR, 2 * A_W), lambda i: (i, 0)), *_halo_specs(wp, R),
                  pl.BlockSpec((3, A_W), lambda i: (0, 0)), pl.BlockSpec((B_CONV, A_W), lambda i: (0, 0)),
                  vec, vec, vec],
        out_specs=[pl.BlockSpec((TMR, 3 * A_W), lambda i: (i, 0)),
                   pl.BlockSpec((3, 8, A_W), lambda i: (0, 0, 0)),
                   pl.BlockSpec((B_CONV, 8, A_W), lambda i: (0, 0, 0)),
                   pl.BlockSpec((3, 8, A_W), lambda i: (0, 0, 0))],
        out_shape=[jax.ShapeDtypeStruct((R, 3 * A_W), F32), jax.ShapeDtypeStruct((3, 8, A_W), F32),
                   jax.ShapeDtypeStruct((B_CONV, 8, A_W), F32), jax.ShapeDtypeStruct((3, 8, A_W), F32)],
        scratch_shapes=[pltpu.VMEM((TMR + 2 * HALO, A_W), F32), pltpu.VMEM((TMR, A_W), F32)],
        compiler_params=_params(48))(dyab, p, p, p, wa, wb, bias, lng, lnb)


def mixer_in_bwd(dmid, p, wa, wb, w_in, x, nw, mod_n, dxr, y, mod_g, kn, kg, T):
    R, wp = p.shape
    dm = x.shape[1]
    _, kb, nb = w_in.shape
    with_resid = y is not None
    contract_last = (((1,), (1,)), ((), ()))

    def body(*refs):
        if with_resid:
            (mp_ref, m_ref, mn_ref, p_ref, wa_ref, wb_ref, w_ref, x_ref, nw_ref, mnorm_ref, dxr_ref, y_ref, mg_ref,
             dp_ref, dx_ref, dy_ref, dmod_ref, dnw_ref, dg_ref) = refs
        else:
            (mp_ref, m_ref, mn_ref, p_ref, wa_ref, wb_ref, w_ref, x_ref, nw_ref, mnorm_ref, dxr_ref,
             dp_ref, dx_ref, dmod_ref, dnw_ref) = refs
        i = pl.program_id(0)
        seg = _seg(i, T)
        live = _halo_live(i, T, R)

        @pl.when(i == 0)
        def _():
            dmod_ref[...] = jnp.zeros_like(dmod_ref)
            dnw_ref[...] = jnp.zeros_like(dnw_ref)
            if with_resid:
                dg_ref[...] = jnp.zeros_like(dg_ref)

        def block(q):
            return lax.dot_general(dp_ref[:, q * nb:(q + 1) * nb], w_ref[q], contract_last,
                                   preferred_element_type=F32)
        mrefs = (mp_ref, m_ref, mn_ref)
        dp_ref[:, 0:A_W] = m_ref[:, 0:A_W].astype(BF16)
        dca = _ext(mrefs, A_W, A_W, live)
        dcin = (wa_ref[pl.ds(0, 1), :] * _at(dca, 1) + wa_ref[pl.ds(1, 1), :] * _at(dca, 0)
                + wa_ref[pl.ds(2, 1), :] * _at(dca, -1))
        dp_ref[:, A_W:2 * A_W] = (dcin * p_ref[:, 2 * A_W:3 * A_W].astype(F32)).astype(BF16)
        dp_ref[:, 2 * A_W:3 * A_W] = (dcin * p_ref[:, A_W:2 * A_W].astype(F32)).astype(BF16)
        dhv = block(0) + block(1)
        for cc in range(A_W // _CW):
            c0 = cc * _CW
            d = _ext(mrefs, 2 * A_W + c0, _CW, live)
            dub = jnp.zeros((TMR, _CW), F32)
            for k in range(B_CONV):
                dub = dub + wb_ref[pl.ds(k, 1), c0:c0 + _CW] * _at(d, B_CONV // 2 - k)
            vb = p_ref[:, 3 * A_W + c0:3 * A_W + c0 + _CW].astype(F32)
            s = _sigmoid(p_ref[:, 4 * A_W + c0:4 * A_W + c0 + _CW].astype(F32))
            dp_ref[:, 3 * A_W + c0:3 * A_W + c0 + _CW] = (dub * s).astype(BF16)
            dp_ref[:, 4 * A_W + c0:4 * A_W + c0 + _CW] = (dub * vb * s * (1.0 - s)).astype(BF16)
        dhv = dhv + block(2) + block(3)
        sc = mnorm_ref[seg, pl.ds(kn + 1, 1), :]
        nwv = nw_ref[...]
        xv = x_ref[...]
        r = lax.rsqrt(jnp.mean(xv * xv, axis=-1, keepdims=True) + EPS)
        xh = xv * r
        dmod_ref[seg, 0] += _colsum8(dhv)
        dmod_ref[seg, 1] += _colsum8(dhv * (xh * nwv))
        dn = dhv * (1.0 + sc)
        dnw_ref[...] += _colsum8(dn * xh)
        dxh = dn * nwv
        dx = dxr_ref[...] + r * (dxh - xh * jnp.mean(dxh * xh, axis=-1, keepdims=True))
        dx_ref[...] = dx
        if with_resid:
            dy_ref[...] = (mg_ref[seg, pl.ds(kg, 1), :] * dx).astype(BF16)
            dg_ref[seg] += _colsum8(dx * y_ref[...])
    assert 2 * nb <= 3 * A_W and N_CHIP * nb == wp
    tile = pl.BlockSpec((TMR, dm), lambda i: (i, 0))
    modspec = pl.BlockSpec((2, 6, dm), lambda i: (0, 0, 0))
    in_specs = [*_halo_specs(3 * A_W, R), pl.BlockSpec((TMR, wp), lambda i: (i, 0)),
                pl.BlockSpec((3, A_W), lambda i: (0, 0)), pl.BlockSpec((B_CONV, A_W), lambda i: (0, 0)),
                pl.BlockSpec((N_CHIP, kb, nb), lambda i: (0, 0, 0), pipeline_mode=pl.Buffered(1)),
                tile, pl.BlockSpec((1, dm), lambda i: (0, 0)), modspec, tile]
    dp_spec = pl.BlockSpec((TMR, wp), lambda i: (i, 0))
    acc_specs = [pl.BlockSpec((2, 2, 8, dm), lambda i: (0, 0, 0, 0)), pl.BlockSpec((8, dm), lambda i: (0, 0))]
    acc_shapes = [jax.ShapeDtypeStruct((2, 2, 8, dm), F32), jax.ShapeDtypeStruct((8, dm), F32)]
    dp_shape, dx_shape = jax.ShapeDtypeStruct((R, wp), BF16), jax.ShapeDtypeStruct((R, dm), F32)
    if with_resid:
        return pl.pallas_call(
            body, name="mixer_in_bwd", grid=(R // TMR,), in_specs=in_specs + [tile, modspec],
            out_specs=[dp_spec, tile, tile] + acc_specs + [pl.BlockSpec((2, 8, dm), lambda i: (0, 0, 0))],
            out_shape=[dp_shape, dx_shape, jax.ShapeDtypeStruct((R, dm), BF16)] + acc_shapes
            + [jax.ShapeDtypeStruct((2, 8, dm), F32)],
            compiler_params=_params(48))(dmid, dmid, dmid, p, wa, wb, w_in, x, nw, mod_n, dxr, y, mod_g)
    return pl.pallas_call(
        body, name="mixer_in_bwd_first", grid=(R // TMR,), in_specs=in_specs,
        out_specs=[dp_spec, tile] + acc_specs, out_shape=[dp_shape, dx_shape] + acc_shapes,
        compiler_params=_params(48))(dmid, dmid, dmid, p, wa, wb, w_in, x, nw, mod_n, dxr)


def _rot_half(v):
    w = v.shape[-1]
    lane = lax.broadcasted_iota(jnp.int32, (1, w), 1)
    return jnp.where(lane % HEAD_DIM < HEAD_DIM // 2, pltpu.roll(v, w - HEAD_DIM // 2, 1),
                     pltpu.roll(v, HEAD_DIM // 2, 1))


def mm_qkv_rope(a, w, cs, sn):
    R = a.shape[0]
    _, kb, nb = w.shape
    wq = N_CHIP * nb
    tm = _row_tile(R)
    qw = N_HEADS * HEAD_DIM
    kw = (wq - qw) // 2
    scale = HEAD_DIM ** -0.5

    def body(a_ref, w_ref, cs_ref, sn_ref, o_ref, x_ref):
        av = a_ref[...].astype(BF16)
        for q in range(N_CHIP):
            x_ref[:, q * nb:(q + 1) * nb] = jnp.dot(av, w_ref[q], preferred_element_type=F32)
        c, s = cs_ref[...], sn_ref[...]
        q = x_ref[:, 0:qw]
        o_ref[:, 0:qw] = ((q * jnp.tile(c, (1, qw // 128)) + _rot_half(q) * jnp.tile(s, (1, qw // 128)))
                          * scale).astype(BF16)
        k = x_ref[:, qw:qw + kw]
        o_ref[:, qw:qw + kw] = (k * jnp.tile(c, (1, kw // 128))
                                + _rot_half(k) * jnp.tile(s, (1, kw // 128))).astype(BF16)
        o_ref[:, qw + kw:] = x_ref[:, qw + kw:].astype(BF16)
    tab = pl.BlockSpec((tm, 128), lambda i: (i, 0))
    return pl.pallas_call(
        body, name="mm_qkv_rope", grid=(R // tm,),
        in_specs=[pl.BlockSpec((tm, kb), lambda i: (i, 0)),
                  pl.BlockSpec((N_CHIP, kb, nb), lambda i: (0, 0, 0), pipeline_mode=pl.Buffered(1)), tab, tab],
        out_specs=pl.BlockSpec((tm, wq), lambda i: (i, 0)),
        out_shape=jax.ShapeDtypeStruct((R, wq), BF16), scratch_shapes=[pltpu.VMEM((tm, wq), F32)],
        compiler_params=_params(48))(a, w, cs, sn)


def attn_in_bwd(dq, dks, dvs, dkc, dvc, cs, sn, w, x, nw, mod_n, dxr, y, mod_g, kn, kg, T):
    R, qw = dq.shape
    kw = dkc.shape[1]
    dm = x.shape[1]
    _, kb, nbw = w.shape
    nb = R // QB
    nl = T // QB
    scale = HEAD_DIM ** -0.5
    contract_last = (((1,), (1,)), ((), ()))

    def body(dq_ref, kp_ref, ko_ref, kn_ref, vp_ref, vo_ref, vn_ref, kc_ref, vc_ref, cs_ref, sn_ref,
             w_ref, x_ref, nw_ref, mnorm_ref, dxr_ref, y_ref, mg_ref,
             o_ref, dx_ref, dy_ref, dmod_ref, dnw_ref, dg_ref):
        b = pl.program_id(0)
        seg = (b >= nl).astype(jnp.int32)

        @pl.when(b == 0)
        def _():
            dmod_ref[...] = jnp.zeros_like(dmod_ref)
            dnw_ref[...] = jnp.zeros_like(dnw_ref)
            dg_ref[...] = jnp.zeros_like(dg_ref)
        c, s = cs_ref[...], sn_ref[...]
        has_next = (b + 1 < nb).astype(F32)
        has_prev = (b >= 1).astype(F32)
        is_ctx = (b >= nl).astype(F32)
        g = dq_ref[...] * scale
        o_ref[:, 0:qw] = (g * jnp.tile(c, (1, qw // 128)) + _rot_half(g * jnp.tile(s, (1, qw // 128)))).astype(BF16)
        g = ko_ref[...] + kp_ref[...] * has_next + kn_ref[...] * has_prev + kc_ref[...] * is_ctx
        o_ref[:, qw:qw + kw] = (g * jnp.tile(c, (1, kw // 128))
                                + _rot_half(g * jnp.tile(s, (1, kw // 128)))).astype(BF16)
        o_ref[:, qw + kw:] = (vo_ref[...] + vp_ref[...] * has_next + vn_ref[...] * has_prev
                              + vc_ref[...] * is_ctx).astype(BF16)
        dhv = None
        for q in range(N_CHIP):
            t = lax.dot_general(o_ref[:, q * nbw:(q + 1) * nbw], w_ref[q], contract_last,
                                preferred_element_type=F32)
            dhv = t if dhv is None else dhv + t
        sc = mnorm_ref[seg, pl.ds(kn + 1, 1), :]
        nwv = nw_ref[...]
        xv = x_ref[...]
        r = lax.rsqrt(jnp.mean(xv * xv, axis=-1, keepdims=True) + EPS)
        xh = xv * r
        dmod_ref[seg, 0] += _colsum8(dhv)
        dmod_ref[seg, 1] += _colsum8(dhv * (xh * nwv))
        dn = dhv * (1.0 + sc)
        dnw_ref[...] += _colsum8(dn * xh)
        dxh = dn * nwv
        dx = dxr_ref[...] + r * (dxh - xh * jnp.mean(dxh * xh, axis=-1, keepdims=True))
        dx_ref[...] = dx
        dy_ref[...] = (mg_ref[seg, pl.ds(kg, 1), :] * dx).astype(BF16)
        dg_ref[seg] += _colsum8(dx * y_ref[...])
    own = pl.BlockSpec((QB, kw), lambda b: (b, 0))
    from_next = pl.BlockSpec((QB, kw), lambda b: (jnp.minimum(b + 1, nb - 1), 0))
    from_prev = pl.BlockSpec((QB, kw), lambda b: (jnp.maximum(b - 1, 0), 0))
    ctx = pl.BlockSpec((QB, kw), lambda b: (jnp.maximum(b - nl, 0), 0))
    tab = pl.BlockSpec((QB, 128), lambda b: (b, 0))
    tile = pl.BlockSpec((QB, dm), lambda b: (b, 0))
    modspec = pl.BlockSpec((2, 6, dm), lambda b: (0, 0, 0))
    return pl.pallas_call(
        body, name="attn_in_bwd", grid=(nb,),
        in_specs=[pl.BlockSpec((QB, qw), lambda b: (b, 0)), from_next, own, from_prev, from_next, own, from_prev,
                  ctx, ctx, tab, tab,
                  pl.BlockSpec((N_CHIP, kb, nbw), lambda b: (0, 0, 0), pipeline_mode=pl.Buffered(1)),
                  tile, pl.BlockSpec((1, dm), lambda b: (0, 0)), modspec, tile, tile, modspec],
        out_specs=[pl.BlockSpec((QB, qw + 2 * kw), lambda b: (b, 0)), tile, tile,
                   pl.BlockSpec((2, 2, 8, dm), lambda b: (0, 0, 0, 0)), pl.BlockSpec((8, dm), lambda b: (0, 0)),
                   pl.BlockSpec((2, 8, dm), lambda b: (0, 0, 0))],
        out_shape=[jax.ShapeDtypeStruct((R, qw + 2 * kw), BF16), jax.ShapeDtypeStruct((R, dm), F32),
                   jax.ShapeDtypeStruct((R, dm), BF16), jax.ShapeDtypeStruct((2, 2, 8, dm), F32),
                   jax.ShapeDtypeStruct((8, dm), F32), jax.ShapeDtypeStruct((2, 8, dm), F32)],
        compiler_params=_params(48))(
            dq, dks[0], dks[1], dks[2], dvs[0], dvs[1], dvs[2], dkc, dvc, cs, sn, w, x, nw, mod_n, dxr, y, mod_g)


def _attn_specs(T, R):
    nl = T // QB
    kblk = N_HEADS * HEAD_DIM // 256

    def band(col, shift):
        return pl.BlockSpec((QB, 256), lambda b: (jnp.clip(b + shift, 0, nl - 1), col))

    def ctx(col):
        return pl.BlockSpec((R - T, 256), lambda b: (T // (R - T), col))
    q = pl.BlockSpec((QB, N_HEADS * HEAD_DIM), lambda b: (b, 0))
    return (q, [band(kblk, -1), band(kblk, 0), band(kblk, 1), ctx(kblk)],
            [band(kblk + 1, -1), band(kblk + 1, 0), band(kblk + 1, 1), ctx(kblk + 1)])


def _attn_common(T, R):
    nl = T // QB
    nk = 3 * QB + (R - T)

    def low_lanes():
        return lax.broadcasted_iota(jnp.int32, (1, 128), 1) < HEAD_DIM

    def dup(v, par):
        low = low_lanes()
        vf = v.astype(F32)
        r = pltpu.roll(vf, HEAD_DIM, 1)
        return (jnp.where(low, vf, r) if par == 0 else jnp.where(low, r, vf)).astype(BF16)

    def stack(ref, par, base):
        low = low_lanes()
        pa = ref[:, base + (2 * par) * 128:base + (2 * par + 1) * 128].astype(BF16)
        pb = ref[:, base + (2 * par + 1) * 128:base + (2 * par + 2) * 128].astype(BF16)
        zero = jnp.zeros_like(pa)
        return jnp.concatenate([jnp.where(low, pa, zero), jnp.where(low, zero, pa),
                                jnp.where(low, pb, zero), jnp.where(low, zero, pb)], axis=0)

    def unstack(v):
        low = low_lanes()
        return (jnp.where(low, v[0:QB], v[QB:2 * QB]), jnp.where(low, v[2 * QB:3 * QB], v[3 * QB:4 * QB]))

    def mask_of(b):
        col = lax.broadcasted_iota(jnp.int32, (1, nk), 1)
        gone = (((col < QB) & (b == 0)) | ((col >= 2 * QB) & (col < 3 * QB) & (b == nl - 1))
                | ((col < 3 * QB) & (b >= nl)))
        return jnp.where(gone, NEG_INF, 0.0)

    def sink_col(sink_ref, first):
        blk = lax.broadcasted_iota(jnp.int32, (4 * QB, 1), 0) // QB
        out = jnp.zeros((4 * QB, 1), F32) + sink_ref[first]
        for h in range(1, 4):
            out = jnp.where(blk == h, sink_ref[first + h], out)
        return out

    def scores(qs, kd, mask, sink):
        s = lax.dot_general(qs, kd, (((1,), (1,)), ((), ())), preferred_element_type=F32) + mask
        m = jnp.maximum(jnp.max(s, axis=-1, keepdims=True), sink)
        e = jnp.exp(s - m)
        es = jnp.exp(sink - m)
        return e, es, 1.0 / (jnp.sum(e, axis=-1, keepdims=True) + es)
    return low_lanes, dup, stack, unstack, mask_of, sink_col, scores


def window_bias(T, R):
    nk = 3 * QB + (R - T)
    row = jnp.arange(QB)[:, None]
    col = jnp.arange(nk)[None, :]
    near = (jnp.abs(col - QB - row) <= WINDOW) | (col >= 3 * QB)
    return jnp.tile(jnp.where(near, 0.0, NEG_INF).astype(F32), (4, 1))


def _probs_spec(nk):
    return pl.BlockSpec((2, None, 2, 4 * QB, nk + 128), lambda b: (0, b, 0, 0, 0))


def attn_fwd(qkvr, sinks, bias, T):
    R = qkvr.shape[0]
    nk = bias.shape[1]
    qspec, kspecs, vspecs = _attn_specs(T, R)
    _, dup, stack, unstack, mask_of, sink_col, scores = _attn_common(T, R)

    def body(q_ref, kp, ko, kn, kc, vp, vo, vn, vc, sink_ref, bias_ref, o_ref, p_ref):
        mask = bias_ref[...] + mask_of(pl.program_id(0))
        for jj in range(2):
            kv = slice(jj * 128, (jj + 1) * 128)
            k_all = jnp.concatenate([kp[:, kv], ko[:, kv], kn[:, kv], kc[:, kv]], axis=0)
            v_all = jnp.concatenate([vp[:, kv], vo[:, kv], vn[:, kv], vc[:, kv]], axis=0)
            for par in range(2):
                kd, vd = dup(k_all, par), dup(v_all, par)
                e, es, rz = scores(stack(q_ref, par, jj * 512), kd, mask, sink_col(sink_ref, jj * 8 + par * 4))
                p = (e * rz).astype(BF16)
                p_ref[jj, par, :, 0:nk] = p
                p_ref[jj, par, :, nk:nk + 128] = jnp.broadcast_to(es * rz, (4 * QB, 128)).astype(BF16)
                o = jnp.dot(p, vd, preferred_element_type=F32)
                pa, pb = unstack(o)
                c0 = jj * 512 + 2 * par * 128
                o_ref[:, c0:c0 + 128] = pa.astype(BF16)
                o_ref[:, c0 + 128:c0 + 256] = pb.astype(BF16)
    return pl.pallas_call(
        body, name="attn_fwd", grid=(R // QB,),
        in_specs=[qspec, *kspecs, *vspecs, pl.BlockSpec(memory_space=pltpu.SMEM),
                  pl.BlockSpec(bias.shape, lambda b: (0, 0))],
        out_specs=[pl.BlockSpec((QB, N_HEADS * HEAD_DIM), lambda b: (b, 0)), _probs_spec(nk)],
        out_shape=[jax.ShapeDtypeStruct((R, N_HEADS * HEAD_DIM), BF16),
                   jax.ShapeDtypeStruct((2, R // QB, 2, 4 * QB, nk + 128), BF16)],
        compiler_params=_params(48))(qkvr, *([qkvr] * 8), sinks, bias)


def attn_bwd(qkvr, do, probs, T):
    R = qkvr.shape[0]
    tc = R - T
    nk = probs.shape[-1] - 128
    qspec, kspecs, vspecs = _attn_specs(T, R)
    _, dup, stack, unstack, _, _, _ = _attn_common(T, R)
    contract_rows = (((0,), (0,)), ((), ()))
    contract_last = (((1,), (1,)), ((), ()))

    def body(q_ref, kp, ko, kn, kc, vp, vo, vn, vc, do_ref, p_ref,
             dq_ref, dkp, dko, dkn, dvp, dvo, dvn, dkc_ref, dvc_ref, dsink_ref):
        @pl.when(pl.program_id(0) == 0)
        def _():
            dsink_ref[...] = jnp.zeros_like(dsink_ref)
            dkc_ref[...] = jnp.zeros_like(dkc_ref)
            dvc_ref[...] = jnp.zeros_like(dvc_ref)
        lane = lax.broadcasted_iota(jnp.int32, (8, 128), 1)
        srow = lax.broadcasted_iota(jnp.int32, (8, 128), 0)
        low_rows = lax.broadcasted_iota(jnp.int32, (128, 1), 0) < HEAD_DIM
        for jj in range(2):
            kv = slice(jj * 128, (jj + 1) * 128)
            k_all = jnp.concatenate([kp[:, kv], ko[:, kv], kn[:, kv], kc[:, kv]], axis=0)
            v_all = jnp.concatenate([vp[:, kv], vo[:, kv], vn[:, kv], vc[:, kv]], axis=0)
            dk_fold, dv_fold = [], []
            for par in range(2):
                kd, vd = dup(k_all, par), dup(v_all, par)
                first = jj * 8 + par * 4
                qs, dos = stack(q_ref, par, jj * 512), stack(do_ref, par, jj * 512)
                p16 = p_ref[jj, par, :, 0:nk]
                p = p16.astype(F32)
                ps = jnp.max(p_ref[jj, par, :, nk:nk + 128].astype(F32), axis=-1, keepdims=True)
                dp = lax.dot_general(dos, vd, contract_last, preferred_element_type=F32)
                delta = jnp.sum(p * dp, axis=-1, keepdims=True)
                ds = (p * (dp - delta)).astype(BF16)
                t = ps * delta
                for h in range(4):
                    dsink = -jnp.sum(t[h * QB:(h + 1) * QB])
                    dsink_ref[...] += jnp.where((lane == first + h) & (srow == 0), dsink, 0.0)
                pa, pb = unstack(jnp.dot(ds, kd, preferred_element_type=F32))
                c0 = jj * 512 + 2 * par * 128
                dq_ref[:, c0:c0 + 128] = pa
                dq_ref[:, c0 + 128:c0 + 256] = pb
                dk_t = lax.dot_general(qs, ds, contract_rows, preferred_element_type=F32)
                dv_t = lax.dot_general(dos, p16, contract_rows, preferred_element_type=F32)
                dk_fold.append(dk_t + pltpu.roll(dk_t, HEAD_DIM, 0))
                dv_fold.append(dv_t + pltpu.roll(dv_t, HEAD_DIM, 0))
            dk = jnp.where(low_rows, dk_fold[0], dk_fold[1]).T
            dv = jnp.where(low_rows, dv_fold[0], dv_fold[1]).T
            dkp[:, kv], dko[:, kv], dkn[:, kv] = dk[0:QB], dk[QB:2 * QB], dk[2 * QB:3 * QB]
            dvp[:, kv], dvo[:, kv], dvn[:, kv] = dv[0:QB], dv[QB:2 * QB], dv[2 * QB:3 * QB]
            dkc_ref[:, kv] += dk[3 * QB:]
            dvc_ref[:, kv] += dv[3 * QB:]
    blk = pl.BlockSpec((QB, 256), lambda b: (b, 0))
    cblk = pl.BlockSpec((tc, 256), lambda b: (0, 0))
    part = jax.ShapeDtypeStruct((R, 256), F32)
    csum = jax.ShapeDtypeStruct((tc, 256), F32)
    full = pl.BlockSpec((QB, N_HEADS * HEAD_DIM), lambda b: (b, 0))
    outs = pl.pallas_call(
        body, name="attn_bwd", grid=(R // QB,),
        in_specs=[qspec, *kspecs, *vspecs, full, _probs_spec(nk)],
        out_specs=[full, blk, blk, blk, blk, blk, blk, cblk, cblk, pl.BlockSpec((8, 128), lambda b: (0, 0))],
        out_shape=[jax.ShapeDtypeStruct((R, N_HEADS * HEAD_DIM), F32), part, part, part, part, part, part,
                   csum, csum, jax.ShapeDtypeStruct((8, 128), F32)],
        compiler_params=_params(48))(qkvr, *([qkvr] * 8), do, probs)
    return outs[0], outs[1:4], outs[4:7], outs[7], outs[8], outs[9]


def loss_head(x, nw, target, y, mod, kg, T):
    R, dm = x.shape
    nl = T // TMR

    def body(x_ref, nw_ref, t_ref, y_ref, mod_ref, loss_ref, dx_ref, dnw_ref, dy_ref, dg_ref):
        i = pl.program_id(0)
        seg = _seg(i, T)

        @pl.when(i == 0)
        def _():
            loss_ref[...] = jnp.zeros_like(loss_ref)
            dnw_ref[...] = jnp.zeros_like(dnw_ref)
            dg_ref[...] = jnp.zeros_like(dg_ref)
        live = (i < nl).astype(F32)
        nwv = nw_ref[...]
        xv = x_ref[...]
        r = lax.rsqrt(jnp.mean(xv * xv, axis=-1, keepdims=True) + EPS)
        xh = xv * r
        err = xh * nwv - t_ref[...]
        per_row = jnp.mean(err * err, axis=-1, keepdims=True)
        loss_ref[...] += 0.5 * live * jnp.sum(per_row)
        dy = err * (live / dm)
        dnw_ref[...] += _colsum8(dy * xh)
        dxh = dy * nwv
        dx = r * (dxh - xh * jnp.mean(dxh * xh, axis=-1, keepdims=True))
        dx_ref[...] = dx
        dy_ref[...] = (mod_ref[seg, pl.ds(kg, 1), :] * dx).astype(BF16)
        dg_ref[seg] += _colsum8(dx * y_ref[...])
    tile = pl.BlockSpec((TMR, dm), lambda i: (i, 0))
    return pl.pallas_call(
        body, name="loss_head", grid=(R // TMR,),
        in_specs=[tile, pl.BlockSpec((1, dm), lambda i: (0, 0)),
                  pl.BlockSpec((TMR, dm), lambda i: (jnp.minimum(i, nl - 1), 0)), tile,
                  pl.BlockSpec((2, 6, dm), lambda i: (0, 0, 0))],
        out_specs=[pl.BlockSpec((8, 128), lambda i: (0, 0)), tile, pl.BlockSpec((8, dm), lambda i: (0, 0)), tile,
                   pl.BlockSpec((2, 8, dm), lambda i: (0, 0, 0))],
        out_shape=[jax.ShapeDtypeStruct((8, 128), F32), jax.ShapeDtypeStruct((R, dm), F32),
                   jax.ShapeDtypeStruct((8, dm), F32), jax.ShapeDtypeStruct((R, dm), BF16),
                   jax.ShapeDtypeStruct((2, 8, dm), F32)])(x, nw, target, y, mod)


def adaln_fwd(cond, w_mod, b_mod):
    nl, dm, ns = w_mod.shape

    def body(c_ref, w_ref, b_ref, o_ref):
        cv = c_ref[...]
        s = (cv * _sigmoid(cv)).astype(BF16)
        o_ref[...] = jnp.dot(s, w_ref[...].astype(BF16), preferred_element_type=F32) + b_ref[...]
    return pl.pallas_call(
        body, name="adaln_fwd", grid=(nl,),
        in_specs=[pl.BlockSpec((16, dm), lambda l: (0, 0)), pl.BlockSpec((None, dm, ns), lambda l: (l, 0, 0)),
                  pl.BlockSpec((None, 1, ns), lambda l: (l, 0, 0))],
        out_specs=pl.BlockSpec((None, 16, ns), lambda l: (l, 0, 0)),
        out_shape=jax.ShapeDtypeStruct((nl, 16, ns), F32), compiler_params=_params(48))(cond, w_mod, b_mod)


def adaln_bwd(cond, dmod, w_mod):
    nl, dm, ns = w_mod.shape

    def body(c_ref, d_ref, w_ref, gw_ref, ds_ref):
        l = pl.program_id(0)

        @pl.when(l == 0)
        def _():
            ds_ref[...] = jnp.zeros_like(ds_ref)
        cv = c_ref[...]
        s = (cv * _sigmoid(cv)).astype(BF16)
        dv = d_ref[...].astype(BF16)
        gw_ref[...] = lax.dot_general(s, dv, (((0,), (0,)), ((), ())), preferred_element_type=F32)
        ds_ref[...] += lax.dot_general(dv, w_ref[...].astype(BF16), (((1,), (1,)), ((), ())),
                                       preferred_element_type=F32)
    return pl.pallas_call(
        body, name="adaln_bwd", grid=(nl,),
        in_specs=[pl.BlockSpec((16, dm), lambda l: (0, 0)), pl.BlockSpec((None, 16, ns), lambda l: (l, 0, 0)),
                  pl.BlockSpec((None, dm, ns), lambda l: (l, 0, 0))],
        out_specs=[pl.BlockSpec((None, dm, ns), lambda l: (l, 0, 0)), pl.BlockSpec((16, dm), lambda l: (0, 0))],
        out_shape=[jax.ShapeDtypeStruct((nl, dm, ns), F32), jax.ShapeDtypeStruct((16, dm), F32)],
        compiler_params=_params(48))(cond, dmod, w_mod)


def _me():
    return lax.axis_index("x"), lax.axis_index("y"), lax.axis_index("c")


def allgather8(block):
    m_per, n = block.shape

    def body(x_ref, out_ref, send_sems, recv_sems, local_sem):
        x, y, c = _me()
        me, sibling = (x, y, c), (x, y, 1 - c)
        chips = [(1 - x, y), (x, 1 - y), (1 - x, 1 - y)]

        def rows(px, py, pc):
            return out_ref.at[pl.ds((4 * px + 2 * py + pc) * m_per, m_per), :]

        def copy(k, blk, to, src=None):
            return pltpu.make_async_remote_copy(
                src_ref=rows(*blk) if src is None else src, dst_ref=rows(*blk),
                send_sem=send_sems.at[k], recv_sem=recv_sems.at[k], device_id=to, device_id_type=MESH)
        mine = pltpu.make_async_copy(x_ref, rows(*me), local_sem)
        mine.start()
        first = [copy(0, me, sibling, src=x_ref)]
        first += [copy(1 + j, me, (*chip, c), src=x_ref) for j, chip in enumerate(chips)]
        for cp in first:
            cp.start()
        passed = [copy(4 + j, (*chip, c), sibling) for j, chip in enumerate(chips)]
        for j, chip in enumerate(chips):
            copy(1 + j, (*chip, c), me).wait_recv()
            passed[j].start()
        copy(0, sibling, me).wait_recv()
        for j, chip in enumerate(chips):
            copy(4 + j, (*chip, 1 - c), me).wait_recv()
        for cp in first + passed:
            cp.wait_send()
        mine.wait()
    return pl.pallas_call(
        body, name="allgather8",
        out_shape=jax.ShapeDtypeStruct((N_DEV * m_per, n), block.dtype),
        in_specs=[pl.BlockSpec(memory_space=pltpu.VMEM)],
        out_specs=pl.BlockSpec(memory_space=pltpu.VMEM),
        scratch_shapes=[pltpu.SemaphoreType.DMA((7,)), pltpu.SemaphoreType.DMA((7,)), pltpu.SemaphoreType.DMA],
        compiler_params=_params(48))(block)


def _other_chips(x, y):
    return [(1 - x, y), (x, 1 - y), (1 - x, 1 - y)]


_HBM = pl.BlockSpec(memory_space=pltpu.HBM)
_SEM = pl.BlockSpec(memory_space=pltpu.SEMAPHORE)
_ANY = pl.BlockSpec(memory_space=pl.ANY)
_EFFECT = pltpu.SideEffectType.DATAFLOW_SIDE_EFFECTING


def _in_hbm(v):
    return pltpu.with_memory_space_constraint(v, pltpu.HBM)


def cast_into_slot(w, layer, chip_id):
    _, kb, nb = w.shape
    tr = _row_tile(kb)

    def body(chip_ref, w_ref, o_ref):
        del chip_ref
        o_ref[...] = w_ref[...].astype(BF16)
    return pl.pallas_call(
        body, name="cast_into_slot",
        grid_spec=pltpu.PrefetchScalarGridSpec(
            num_scalar_prefetch=1, grid=(kb // tr,),
            in_specs=[pl.BlockSpec((None, tr, nb), lambda i, chip: (layer, i, 0))],
            out_specs=pl.BlockSpec((None, tr, nb), lambda i, chip: (chip[0], i, 0))),
        out_shape=jax.ShapeDtypeStruct((N_CHIP, kb, nb), BF16))(chip_id, w)


def _split_copies(mode, srcs, lands, send_sems, recv_sems):
    x, y, c = _me()
    out = []
    for t in range(len(lands)):
        for k, chip in enumerate(_other_chips(x, y)):
            if mode == "gather":
                src = dst = lands[t].at[2 * x + y]
                landed = lands[t].at[2 * chip[0] + chip[1]]
            else:
                src, dst, landed = srcs[t].at[2 * chip[0] + chip[1]], lands[t].at[k], lands[t].at[k]
            send = pltpu.make_async_remote_copy(src_ref=src, dst_ref=dst, send_sem=send_sems.at[3 * t + k],
                                                recv_sem=recv_sems.at[3 * t + k], device_id=(*chip, c),
                                                device_id_type=MESH)
            recv = pltpu.make_async_remote_copy(src_ref=src, dst_ref=landed, send_sem=send_sems.at[3 * t + k],
                                                recv_sem=recv_sems.at[3 * t + k], device_id=(*chip, c),
                                                device_id_type=MESH)
            out.append((send, recv))
    return out


def exchange_start(name, mode, srcs, lands, after):
    ns, nl = len(srcs), len(lands)
    na = ns + nl

    def body(*refs):
        src_refs, land_refs = refs[:ns], refs[ns:na]
        send_sems, recv_sems = refs[na + 1], refs[na + 2]
        token = refs[-1]
        for send, _ in _split_copies(mode, src_refs, land_refs, send_sems, recv_sems):
            send.start()
        token[...] = jnp.zeros_like(token)
    arrays = list(srcs) + list(lands)
    outs = pl.pallas_call(
        body, name=name,
        out_shape=(pltpu.SemaphoreType.DMA((3 * nl,)), pltpu.SemaphoreType.DMA((3 * nl,)),
                   *[pltpu.HBM(v.shape, v.dtype) for v in arrays], jax.ShapeDtypeStruct((8, 128), F32)),
        in_specs=[_HBM] * na + [_ANY],
        out_specs=(_SEM, _SEM, *[_HBM] * na, pl.BlockSpec(memory_space=pltpu.VMEM)),
        input_output_aliases={i: 2 + i for i in range(na)},
        compiler_params=pltpu.CompilerParams(has_side_effects=_EFFECT))(*[_in_hbm(v) for v in arrays], after)
    return outs[0], outs[1], list(outs[2:2 + ns]), list(outs[2 + ns:2 + na]), outs[-1]


def exchange_wait(name, mode, send_sems, recv_sems, srcs, lands, after):
    ns, nl = len(srcs), len(lands)
    na = ns + nl

    def body(*refs):
        for _, recv in _split_copies(mode, refs[:ns], refs[ns:na], refs[na], refs[na + 1]):
            recv.wait_send()
            recv.wait_recv()
    arrays = list(srcs) + list(lands)
    outs = pl.pallas_call(
        body, name=name,
        out_shape=[pltpu.HBM(v.shape, v.dtype) for v in arrays],
        in_specs=[_HBM] * na + [_SEM, _SEM, _ANY], out_specs=[_HBM] * na,
        input_output_aliases={i: i for i in range(na)},
        compiler_params=pltpu.CompilerParams(has_side_effects=_EFFECT))(*arrays, send_sems, recv_sems, after)
    return list(outs[:ns]), list(outs[ns:])


def swap_with_sibling(vs):
    n = len(vs)

    def body(*refs):
        v_refs, out_refs, send_sems, recv_sems = refs[:n], refs[n:2 * n], refs[2 * n], refs[2 * n + 1]
        x, y, c = _me()
        cps = [pltpu.make_async_remote_copy(src_ref=v_refs[t], dst_ref=out_refs[t], send_sem=send_sems.at[t],
                                            recv_sem=recv_sems.at[t], device_id=(x, y, 1 - c), device_id_type=MESH)
               for t in range(n)]
        for cp in cps:
            cp.start()
        for cp in cps:
            cp.wait()
    return pl.pallas_call(
        body, name="swap_with_sibling", out_shape=[jax.ShapeDtypeStruct(v.shape, v.dtype) for v in vs],
        in_specs=[_ANY] * n, out_specs=[_ANY] * n,
        scratch_shapes=[pltpu.SemaphoreType.DMA((n,)), pltpu.SemaphoreType.DMA((n,))])(*vs)


def sum_slots(parts):
    n, rows, w = parts.shape
    tr = _row_tile(rows)

    def body(p_ref, o_ref):
        acc = p_ref[0].astype(F32)
        for k in range(1, n):
            acc = acc + p_ref[k].astype(F32)
        o_ref[...] = acc
    return pl.pallas_call(
        body, name="sum_slots", grid=(rows // tr,),
        in_specs=[pl.BlockSpec((n, tr, w), lambda i: (0, i, 0))], out_specs=pl.BlockSpec((tr, w), lambda i: (i, 0)),
        out_shape=jax.ShapeDtypeStruct((rows, w), F32), compiler_params=_params(48))(parts)


def sum_landed(landed, own, chip_id, layer, n_layers, buf):
    n, rows, w = landed.shape
    tr = _row_tile(rows)
    base = layer * (rows // tr)

    def compute(l_ref, g_ref, o_ref):
        acc = g_ref[...].astype(F32)
        for k in range(n):
            acc = acc + l_ref[k].astype(F32)
        o_ref[...] = acc
    in_specs = [pl.BlockSpec((n, tr, w), lambda i, chip: (0, i, 0)),
                pl.BlockSpec((None, tr, w), lambda i, chip: (chip[0], i, 0))]
    out_spec = pl.BlockSpec((tr, w), lambda i, chip: (base + i, 0))
    out_shape = jax.ShapeDtypeStruct((n_layers * rows, w), F32)
    if buf is None:
        def body(chip_ref, l_ref, g_ref, o_ref):
            del chip_ref
            compute(l_ref, g_ref, o_ref)
        return pl.pallas_call(
            body, name="sum_landed",
            grid_spec=pltpu.PrefetchScalarGridSpec(num_scalar_prefetch=1, grid=(rows // tr,), in_specs=in_specs,
                                                   out_specs=out_spec),
            out_shape=out_shape, compiler_params=_params(48))(chip_id, landed, own)

    def body(chip_ref, l_ref, g_ref, buf_ref, o_ref):
        del chip_ref, buf_ref
        compute(l_ref, g_ref, o_ref)
    return pl.pallas_call(
        body, name="sum_landed_into",
        grid_spec=pltpu.PrefetchScalarGridSpec(num_scalar_prefetch=1, grid=(rows // tr,),
                                               in_specs=in_specs + [_ANY], out_specs=out_spec),
        out_shape=out_shape, input_output_aliases={3: 0}, compiler_params=_params(48))(chip_id, landed, own, buf)


def adamw(w, ga, gb, m, v):
    rows, wd = w.shape
    tr = min(_row_tile(rows), 128)
    c1 = 1.0 / (1.0 - ADAM_B1 ** ADAM_STEP)
    c2 = 1.0 / (1.0 - ADAM_B2 ** ADAM_STEP)

    def update(wv, g, mv, vv, g_ref, d_ref, m_ref, v_ref):
        mn = ADAM_B1 * mv + (1.0 - ADAM_B1) * g
        vn = ADAM_B2 * vv + (1.0 - ADAM_B2) * (g * g)
        g_ref[...] = g
        m_ref[...] = mn
        v_ref[...] = vn
        d_ref[...] = -ADAM_LR * ((mn * c1) / (jnp.sqrt(vn * c2) + ADAM_EPS) + ADAM_WD * wv)
    tile = pl.BlockSpec((tr, wd), lambda i: (i, 0))
    out = jax.ShapeDtypeStruct((rows, wd), F32)
    if gb is None:
        def body(w_ref, ga_ref, m_ref, v_ref, g_out, d_out, m_out, v_out):
            update(w_ref[...], ga_ref[...], m_ref[...], v_ref[...], g_out, d_out, m_out, v_out)
        return pl.pallas_call(body, name="adamw", grid=(rows // tr,), in_specs=[tile] * 4,
                              out_specs=[tile] * 4, out_shape=[out] * 4)(w, ga, m, v)

    def body(w_ref, ga_ref, gb_ref, m_ref, v_ref, g_out, d_out, m_out, v_out):
        update(w_ref[...], ga_ref[...] + gb_ref[...], m_ref[...], v_ref[...], g_out, d_out, m_out, v_out)
    return pl.pallas_call(body, name="adamw_sum", grid=(rows // tr,), in_specs=[tile] * 5,
                          out_specs=[tile] * 4, out_shape=[out] * 4)(w, ga, gb, m, v)


def _rope_tables(T, R):
    rows = T // GRID_W
    row = jnp.repeat(jnp.arange(rows), GRID_W).astype(F32)
    col = jnp.tile(jnp.arange(GRID_W), rows).astype(F32)
    n_freq = HEAD_DIM // 4
    inv_freq = ROPE_THETA ** (-jnp.arange(n_freq, dtype=F32) / n_freq)
    ang = jnp.concatenate([row[:, None] * inv_freq, col[:, None] * inv_freq], axis=-1)
    cos, sin = jnp.cos(ang), jnp.sin(ang)
    cs = jnp.tile(cos, (1, 4))
    sn = jnp.tile(jnp.concatenate([-sin, sin], axis=-1), (1, 2))
    pad = R - T
    return (jnp.concatenate([cs, jnp.ones((pad, 128), F32)], axis=0),
            jnp.concatenate([sn, jnp.zeros((pad, 128), F32)], axis=0))


def _pack(parts, mult=8 * 128):
    flat = jnp.concatenate([p.reshape(-1).astype(F32) for p in parts])
    pad = (-flat.shape[0]) % mult
    return jnp.pad(flat, (0, pad)).reshape(-1, 128)


def _unpack(buf, shapes):
    flat = buf.reshape(-1)
    out, o = [], 0
    for s in shapes:
        n = 1
        for d in s:
            n *= d
        out.append(flat[o:o + n].reshape(s))
        o += n
    return out


def kernel(x, c, ctx, c_ctx, w_mod, b_mod, norm_mix, norm_ffn, w_in_ab, conv_a, conv_b, conv_b_bias, ln_b_gain, ln_b_bias, w_out_ab, w_qkv, w_o, sinks, w_up, w_conv_ffn, w_down, final_norm, loss_target, m_c_ctx, m_w_mod, m_b_mod, m_norm_mix, m_norm_ffn, m_w_in_ab, m_conv_a, m_conv_b, m_conv_b_bias, m_ln_b_gain, m_ln_b_bias, m_w_out_ab, m_w_qkv, m_w_o, m_sinks, m_w_up, m_w_conv_ffn, m_w_down, m_final_norm, v_c_ctx, v_w_mod, v_b_mod, v_norm_mix, v_norm_ffn, v_w_in_ab, v_conv_a, v_conv_b, v_conv_b_bias, v_ln_b_gain, v_ln_b_bias, v_w_out_ab, v_w_qkv, v_w_o, v_sinks, v_w_up, v_w_conv_ffn, v_w_down, v_final_norm):
    T, dm = x.shape[1], x.shape[2]
    tc = ctx.shape[1]
    R = T + tc
    depth = w_mod.shape[0]
    ax, ay, ac = lax.axis_index("x"), lax.axis_index("y"), lax.axis_index("c")
    chip = 2 * ax + ay
    dev = 4 * ax + 2 * ay + ac

    small_w = [conv_a, conv_b, w_conv_ffn]
    gathered = allgather8(_pack([c] + small_w)).reshape(N_DEV, -1)
    cond8 = gathered[:, :dm]
    off = dm
    full_small = []
    for wsh in small_w:
        n = wsh.size
        per_chip = gathered[0::2, off:off + n].reshape((N_CHIP,) + wsh.shape)
        full_small.append(jnp.concatenate([per_chip[q] for q in range(N_CHIP)], axis=-1))
        off += n
    conv_a_f, conv_b_f, w_conv_ffn_f = full_small
    cond = jnp.concatenate([cond8, c_ctx[None, :], jnp.zeros((7, dm), F32)], axis=0)

    ns_mod = w_mod.shape[2]
    b_mod_sh = lax.dynamic_slice_in_dim(b_mod, chip * ns_mod, ns_mod, axis=1)[:, None, :]
    mod_sh = adaln_fwd(cond, w_mod, b_mod_sh)
    mod_all = allgather8(mod_sh.reshape(depth * 16, ns_mod)).reshape(N_DEV, depth, 16, ns_mod)
    mod_full = jnp.concatenate([mod_all[2 * q] for q in range(N_CHIP)], axis=-1)
    mine = lax.dynamic_index_in_dim(mod_full, dev, axis=1, keepdims=False)
    mods = jnp.stack([mine, mod_full[:, 8]], axis=1).reshape(depth, 2, 6, dm)

    masters = {"w_in_ab": w_in_ab, "w_out_ab": w_out_ab, "w_qkv": w_qkv, "w_o": w_o, "w_up": w_up, "w_down": w_down}
    chip_id = chip.astype(jnp.int32).reshape(1)

    def half_weights(l, half):
        if half == 1:
            return [("w_up", l), ("w_down", l)]
        return [("w_in_ab", l // 2), ("w_out_ab", l // 2)] if l % 2 == 0 else [("w_qkv", l // 2), ("w_o", l // 2)]
    in_flight, after = {}, mods
    for l in range(depth):
        for half in range(2):
            lands = [cast_into_slot(masters[n], j, chip_id) for n, j in half_weights(l, half)]
            send_sems, recv_sems, _, lands, after = exchange_start(f"gather_start_{l}_{half}", "gather", [], lands, after)
            in_flight[l, half] = (send_sems, recv_sems, lands)
    mods = mods + after[0, 0]

    def gathered_weights(l, half, after):
        send_sems, recv_sems, lands = in_flight[l, half]
        _, landed = exchange_wait(f"gather_wait_{l}_{half}", "gather", send_sems, recv_sems, [], lands, after)
        return dict(zip([n for n, _ in half_weights(l, half)], landed))

    cs, sn = _rope_tables(T, R)
    bias = window_bias(T, R)
    sinks_flat = sinks.reshape(-1)

    xs = jnp.concatenate([x[0], ctx[0]], axis=0)
    saved, W = [], []
    h1 = norm_mod_fwd(xs, norm_mix[0][None], mods[0], 0, T)
    for l in range(depth):
        e = l // 2
        wl = gathered_weights(l, 0, h1)
        W.append(wl)
        s = {"x0": xs, "h1": h1}
        if l % 2 == 0:
            p = mm_nn(h1, wl["w_in_ab"], BF16)
            yab, y1, x1, h2 = mixer_fwd(p, conv_a_f[e], conv_b_f[e], conv_b_bias[e][None], ln_b_gain[e][None],
                                        ln_b_bias[e][None], wl["w_out_ab"], xs, norm_ffn[l][None], mods[l], 2, 3, T)
            s.update(p=p, mix=yab)
        else:
            qkvr = mm_qkv_rope(h1, wl["w_qkv"], cs, sn)
            att, probs = attn_fwd(qkvr, sinks_flat[e * N_HEADS:(e + 1) * N_HEADS], bias, T)
            s.update(qkvr=qkvr, mix=att, probs=probs)
            y1, x1, h2 = mm_resid_norm_fwd(att, wl["w_o"], xs, norm_ffn[l][None], mods[l], mods[l], 2, 3, T)
        wl.update(gathered_weights(l, 1, h2))
        u = mm_nn(h2, wl["w_up"], BF16)
        if l + 1 < depth:
            z, y2, xs, h1 = ffn_mid_fwd(u, w_conv_ffn_f[l], wl["w_down"], x1, norm_mix[l + 1][None], mods[l],
                                        mods[l + 1], 5, 0, T)
        else:
            z, y2, xs = ffn_mid_fwd(u, w_conv_ffn_f[l], wl["w_down"], x1, None, mods[l], None, 5, 0, T)
        s.update(y1=y1, x1=x1, h2=h2, u=u, z=z, y2=y2)
        saved.append(s)

    loss_part, dx, d_final, dy2, dg2_last = loss_head(xs, final_norm[None], loss_target[0], saved[depth - 1]["y2"],
                                                      mods[depth - 1], 5, T)
    loss = lax.psum(loss_part[0, 0], ("x", "y", "c"))

    d_mods, d_norm_mix, d_norm_ffn = [None] * depth, [None] * depth, [None] * depth
    d_conv_a, d_conv_b, d_vecs, d_sinks, d_wc = [None] * 2, [None] * 2, [None] * 2, [None] * 2, [None] * depth
    dss1, dss2, dg1, dg2 = [None] * depth, [None] * depth, [None] * depth, [None] * depth
    scattering = {}

    def scatter(l, half, G, after):
        grads_h = [G[n] for n, _ in half_weights(l, half)]
        lands = [lax.empty((N_CHIP - 1, *g.shape[1:]), g.dtype) for g in grads_h]
        send_sems, recv_sems, grads_h, lands, token = exchange_start(
            f"scatter_start_{l}_{half}", "scatter", grads_h, lands, after)
        scattering[l, half] = (send_sems, recv_sems, grads_h, lands)
        return token

    dg2[depth - 1] = dg2_last
    pending = 0.0
    for l in reversed(range(depth)):
        e = l // 2
        s, wl = saved[l], W[l]
        G = {}
        G["w_down"] = mm_tn(s["z"], dy2, "row", wl["w_down"])
        duc, d_wc[l] = ffn_mid_bwd(mm_nt(dy2, wl["w_down"], BF16), s["u"], w_conv_ffn_f[l] + pending, T)
        du, dx, dy1, dss2[l], d_norm_ffn[l], dg1[l] = ffn_up_bwd(
            duc, w_conv_ffn_f[l], wl["w_up"], s["x1"], norm_ffn[l][None], mods[l], dx, s["y1"], mods[l], 3, 2, T)
        G["w_up"] = mm_tn(s["h2"], du, "col", wl["w_up"])
        started = scatter(l, 1, G, du)[0, 0]
        if l % 2 == 0:
            G["w_out_ab"] = mm_tn(s["mix"], dy1, "row", wl["w_out_ab"])
            dyab = mm_nt(dy1, wl["w_out_ab"], F32)
            dmid, d_conv_a[e], d_conv_b[e], d_vecs[e] = convmix_bwd1(
                dyab, s["p"], conv_a_f[e] + started, conv_b_f[e], conv_b_bias[e][None], ln_b_gain[e][None],
                ln_b_bias[e][None], T)
            if l > 0:
                dp, dx, dy2, dss1[l], d_norm_mix[l], dg2[l - 1] = mixer_in_bwd(
                    dmid, s["p"], conv_a_f[e], conv_b_f[e], wl["w_in_ab"], s["x0"], norm_mix[l][None], mods[l], dx,
                    saved[l - 1]["y2"], mods[l - 1], 0, 5, T)
            else:
                dp, dx, dss1[l], d_norm_mix[l] = mixer_in_bwd(
                    dmid, s["p"], conv_a_f[e], conv_b_f[e], wl["w_in_ab"], s["x0"], norm_mix[l][None], mods[l], dx,
                    None, None, 0, 0, T)
            G["w_in_ab"] = mm_tn(s["h1"], dp, "col", wl["w_in_ab"])
        else:
            G["w_o"] = mm_tn(s["mix"], dy1, "row", wl["w_o"])
            datt = mm_nt(dy1, wl["w_o"], BF16)
            dq, dks, dvs, dkc, dvc, d_sinks[e] = attn_bwd(s["qkvr"], datt, s["probs"], T)
            dqkv, dx, dy2, dss1[l], d_norm_mix[l], dg2[l - 1] = attn_in_bwd(
                dq, dks, dvs, dkc, dvc, cs + started, sn, wl["w_qkv"], s["x0"], norm_mix[l][None], mods[l], dx,
                saved[l - 1]["y2"], mods[l - 1], 0, 5, T)
            G["w_qkv"] = mm_tn(s["h1"], dqkv, "col", wl["w_qkv"])
        token = scatter(l, 0, G, dx)
        pending = token[0, 0]
    grad_x = dx[:T][None]
    for l in range(depth):
        a1, a2 = dss1[l].sum(2), dss2[l].sum(2)
        d_mods[l] = jnp.stack([a1[:, 0], a1[:, 1], dg1[l].sum(1), a2[:, 0], a2[:, 1], dg2[l].sum(1)], axis=1)

    d_mods = jnp.stack(d_mods)
    summed_parts = [
        d_mods[:, 1],
        jnp.stack(d_norm_mix).sum(1), jnp.stack(d_norm_ffn).sum(1),
        jnp.stack(d_conv_a).sum(2), jnp.stack(d_conv_b).sum(2),
        jnp.stack(d_vecs).sum(2),
        jnp.stack(d_sinks)[:, 0, :N_HEADS],
        jnp.stack(d_wc).sum(2), d_final.sum(0) + pending]
    summed_shapes = [p.shape for p in summed_parts]
    n_own = depth * 6 * dm
    pack = _pack([d_mods[:, 0]] + summed_parts)
    parts = allgather8(pack).reshape(N_DEV, -1, 128)
    total = sum_slots(parts)
    own_rows = parts.reshape(N_DEV, -1)[:, :n_own].reshape(N_DEV, depth, 6 * dm)
    (dmod_ctx, g_norm_mix, g_norm_ffn, g_conv_a, g_conv_b, g_vecs, g_sinks, g_wc, g_final) = _unpack(
        total.reshape(-1)[n_own:], summed_shapes)
    dmod_rows = jnp.concatenate([jnp.moveaxis(own_rows, 0, 1), dmod_ctx.reshape(depth, 1, 6 * dm),
                                 jnp.zeros((depth, 7, 6 * dm), F32)], axis=1)
    g_b_mod = dmod_rows.sum(1)
    dmod_sh = lax.dynamic_slice_in_dim(dmod_rows, chip * ns_mod, ns_mod, axis=2)
    g_w_mod, dsilu = adaln_bwd(cond, dmod_sh, w_mod)
    dsilu_all = allgather8(dsilu[8:16]).reshape(N_DEV, 8, dm)
    dsilu_ctx = sum_slots(dsilu_all[0::2])[0]
    sg = jax.nn.sigmoid(c_ctx)
    g_c_ctx = dsilu_ctx * (sg * (1.0 + c_ctx * (1.0 - sg)))

    def shard_cols(full, width):
        return lax.dynamic_slice_in_dim(full, chip * width, width, axis=full.ndim - 1)
    g_conv_a_s = shard_cols(g_conv_a, conv_a.shape[-1])
    g_conv_b_s = shard_cols(g_conv_b, conv_b.shape[-1])
    g_wc_s = shard_cols(g_wc, w_conv_ffn.shape[-1])

    grads, deltas, new_m, new_v = {}, {}, {}, {}

    def step_2d(name, wv, ga, gb, mv, vv):
        shp = wv.shape
        r2 = lambda t: t.reshape(-1, shp[-1])
        g, d, mn, vn = adamw(r2(wv), r2(ga), None if gb is None else r2(gb), r2(mv), r2(vv))
        grads[name], deltas[name], new_m[name], new_v[name] = (t.reshape(shp) for t in (g, d, mn, vn))

    step_2d("w_mod", w_mod, g_w_mod, None, m_w_mod, v_w_mod)
    sums = {n: None for n in masters}
    for l in reversed(range(depth)):
        for half in (1, 0):
            send_sems, recv_sems, grads_h, lands = scattering[l, half]
            grads_h, landed = exchange_wait(f"scatter_wait_{l}_{half}", "scatter", send_sems, recv_sems, grads_h,
                                            lands, deltas["w_mod"])
            for (n, j), own, arr in zip(half_weights(l, half), grads_h, landed):
                sums[n] = sum_landed(arr, own, chip_id, j, masters[n].shape[0], sums[n])
    moments = {"w_in_ab": (m_w_in_ab, v_w_in_ab), "w_out_ab": (m_w_out_ab, v_w_out_ab),
               "w_qkv": (m_w_qkv, v_w_qkv), "w_o": (m_w_o, v_w_o), "w_up": (m_w_up, v_w_up),
               "w_down": (m_w_down, v_w_down)}
    others = swap_with_sibling([sums[name] for name in masters])
    for (name, wv), other in zip(masters.items(), others):
        step_2d(name, wv, sums[name].reshape(wv.shape), other.reshape(wv.shape), *moments[name])

    small = [("c_ctx", c_ctx, g_c_ctx, m_c_ctx, v_c_ctx), ("b_mod", b_mod, g_b_mod, m_b_mod, v_b_mod),
             ("norm_mix", norm_mix, g_norm_mix, m_norm_mix, v_norm_mix),
             ("norm_ffn", norm_ffn, g_norm_ffn, m_norm_ffn, v_norm_ffn),
             ("conv_a", conv_a, g_conv_a_s, m_conv_a, v_conv_a), ("conv_b", conv_b, g_conv_b_s, m_conv_b, v_conv_b),
             ("conv_b_bias", conv_b_bias, g_vecs[:, 0], m_conv_b_bias, v_conv_b_bias),
             ("ln_b_gain", ln_b_gain, g_vecs[:, 1], m_ln_b_gain, v_ln_b_gain),
             ("ln_b_bias", ln_b_bias, g_vecs[:, 2], m_ln_b_bias, v_ln_b_bias),
             ("sinks", sinks, g_sinks, m_sinks, v_sinks),
             ("w_conv_ffn", w_conv_ffn, g_wc_s, m_w_conv_ffn, v_w_conv_ffn),
             ("final_norm", final_norm, g_final, m_final_norm, v_final_norm)]
    shapes = [t[1].shape for t in small]
    packed = [_pack([t[k] for t in small]) for k in (1, 2, 3, 4)]
    n_real = sum(t[1].size for t in small)
    lane_id = jnp.arange(packed[3].size).reshape(packed[3].shape)
    packed[3] = jnp.where(lane_id < n_real, packed[3], 1.0)
    outs = adamw(packed[0], packed[1], None, packed[2], packed[3])
    for (name, *_), g, d, mn, vn in zip(small, *[_unpack(o, shapes) for o in outs]):
        grads[name], deltas[name], new_m[name], new_v[name] = g, d, mn, vn

    order = ["c_ctx", "w_mod", "b_mod", "norm_mix", "norm_ffn", "w_in_ab", "conv_a", "conv_b", "conv_b_bias",
             "ln_b_gain", "ln_b_bias", "w_out_ab", "w_qkv", "w_o", "sinks", "w_up", "w_conv_ffn", "w_down",
             "final_norm"]
    return (loss, grad_x, *[grads[n] for n in order], *[deltas[n] for n in order],
            *[new_m[n] for n in order], *[new_v[n] for n in order])
```

```python
import jax
import jax.numpy as jnp
from jax import lax
from jax.experimental import pallas as pl
from jax.experimental.pallas import tpu as pltpu

F32 = jnp.float32
BF16 = jnp.bfloat16
MESH = pl.DeviceIdType.MESH

EPS = 1e-6
NEG_INF = -1e30
GRID_W = 64
HEAD_DIM = 64
N_HEADS = 16
WINDOW = 128
QB = 128
ROPE_THETA = 10000.0
A_W = 512
B_CONV = 31
D_FF = 2816
ADAM_LR, ADAM_B1, ADAM_B2, ADAM_EPS, ADAM_WD, ADAM_STEP = 0.001, 0.9, 0.999, 1e-8, 0.01, 10

TMR = 256
HALO = 16
N_DEV = 8
N_CHIP = 4


def _params(vmem_mb=None):
    if vmem_mb is None:
        return pltpu.CompilerParams()
    return pltpu.CompilerParams(vmem_limit_bytes=vmem_mb * 1024 * 1024)


def _row_tile(rows, cap=768):
    for t in (2816, 1408, 768, 704, 512, 384, 256, 128, 64, 32, 16, 8):
        if t <= cap and rows % t == 0:
            return t
    raise ValueError(rows)


def _colsum8(v):
    r, c = v.shape
    return v.reshape(r // 8, 8, c).sum(axis=0)


def _sigmoid(v):
    return 0.5 * jnp.tanh(0.5 * v) + 0.5


def mm_nn(a, w, out_dtype):
    R = a.shape[0]
    _, kb, nb = w.shape
    tm = _row_tile(R)

    def body(a_ref, w_ref, o_ref):
        av = a_ref[...].astype(BF16)
        for q in range(N_CHIP):
            o_ref[:, q * nb:(q + 1) * nb] = jnp.dot(av, w_ref[q], preferred_element_type=F32).astype(o_ref.dtype)
    return pl.pallas_call(
        body, name="mm_nn_col", grid=(R // tm,),
        in_specs=[pl.BlockSpec((tm, kb), lambda i: (i, 0)),
                  pl.BlockSpec((N_CHIP, kb, nb), lambda i: (0, 0, 0), pipeline_mode=pl.Buffered(1))],
        out_specs=pl.BlockSpec((tm, N_CHIP * nb), lambda i: (i, 0)),
        out_shape=jax.ShapeDtypeStruct((R, N_CHIP * nb), out_dtype),
        compiler_params=_params(48))(a, w)


def mm_nt(d, w, out_dtype):
    R = d.shape[0]
    _, kb, nb = w.shape
    tm = _row_tile(R)
    contract_last = (((1,), (1,)), ((), ()))
    resident = pl.BlockSpec((N_CHIP, kb, nb), lambda i: (0, 0, 0), pipeline_mode=pl.Buffered(1))

    def body(d_ref, w_ref, o_ref):
        wv = w_ref[...].reshape(N_CHIP * kb, nb)
        o_ref[...] = lax.dot_general(d_ref[...].astype(BF16), wv, contract_last,
                                     preferred_element_type=F32).astype(o_ref.dtype)
    return pl.pallas_call(
        body, name="mm_nt_row", grid=(R // tm,),
        in_specs=[pl.BlockSpec((tm, nb), lambda i: (i, 0)), resident],
        out_specs=pl.BlockSpec((tm, N_CHIP * kb), lambda i: (i, 0)),
        out_shape=jax.ShapeDtypeStruct((R, N_CHIP * kb), out_dtype),
        compiler_params=_params(48))(d, w)


def mm_tn(a, d, kind, like):
    R = a.shape[0]
    _, kb, nb = like.shape
    tm = _row_tile(R, 1408 if kind == "col" else 768)
    nsteps = R // tm
    contract_rows = (((0,), (0,)), ((), ()))
    out_shape = jax.ShapeDtypeStruct(like.shape, BF16)

    def accumulate(a_ref, d_ref, acc_ref):
        @pl.when(pl.program_id(1) == 0)
        def _():
            acc_ref[...] = jnp.zeros_like(acc_ref)
        acc_ref[...] += lax.dot_general(a_ref[...].astype(BF16), d_ref[...].astype(BF16), contract_rows,
                                        preferred_element_type=F32)
    if kind == "col":
        def body(a_ref, d_ref, o_ref, acc_ref):
            accumulate(a_ref, d_ref, acc_ref)

            @pl.when(pl.program_id(1) == nsteps - 1)
            def _():
                o_ref[...] = acc_ref[...].astype(BF16)
        return pl.pallas_call(
            body, name="mm_tn_col", grid=(N_CHIP, nsteps),
            in_specs=[pl.BlockSpec((tm, kb), lambda q, i: (i, 0)), pl.BlockSpec((tm, nb), lambda q, i: (i, q))],
            out_specs=pl.BlockSpec((None, kb, nb), lambda q, i: (q, 0, 0)), out_shape=out_shape,
            scratch_shapes=[pltpu.VMEM((kb, nb), F32)], compiler_params=_params(48))(a, d)
    tn = 512

    def body(a_ref, d_ref, o_ref, acc_ref):
        accumulate(a_ref, d_ref, acc_ref)

        @pl.when(pl.program_id(1) == nsteps - 1)
        def _():
            o_ref[...] = acc_ref[...].astype(BF16).reshape(N_CHIP, kb, tn)
    return pl.pallas_call(
        body, name="mm_tn_row", grid=(nb // tn, nsteps),
        in_specs=[pl.BlockSpec((tm, N_CHIP * kb), lambda n, i: (i, 0)), pl.BlockSpec((tm, tn), lambda n, i: (i, n))],
        out_specs=pl.BlockSpec((N_CHIP, kb, tn), lambda n, i: (0, 0, n)), out_shape=out_shape,
        scratch_shapes=[pltpu.VMEM((N_CHIP * kb, tn), F32)], compiler_params=_params(48))(a, d)


def _seg(i, T):
    return (i >= T // TMR).astype(jnp.int32)


def norm_mod_fwd(x, nw, mod, k, T):
    R, dm = x.shape

    def body(x_ref, nw_ref, mod_ref, h_ref):
        seg = _seg(pl.program_id(0), T)
        sh = mod_ref[seg, pl.ds(k, 1), :]
        sc = mod_ref[seg, pl.ds(k + 1, 1), :]
        xv = x_ref[...]
        r = lax.rsqrt(jnp.mean(xv * xv, axis=-1, keepdims=True) + EPS)
        h_ref[...] = ((xv * r * nw_ref[...]) * (1.0 + sc) + sh).astype(BF16)
    return pl.pallas_call(
        body, name="norm_mod_fwd", grid=(R // TMR,),
        in_specs=[pl.BlockSpec((TMR, dm), lambda i: (i, 0)),
                  pl.BlockSpec((1, dm), lambda i: (0, 0)),
                  pl.BlockSpec((2, 6, dm), lambda i: (0, 0, 0))],
        out_specs=pl.BlockSpec((TMR, dm), lambda i: (i, 0)),
        out_shape=jax.ShapeDtypeStruct((R, dm), BF16))(x, nw, mod)


def mm_resid_norm_fwd(a, w, x, nw, mod_g, mod_n, kg, kn, T):
    R, dm = x.shape
    _, kb, nb = w.shape

    def body(a_ref, w_ref, x_ref, nw_ref, mg_ref, mn_ref, y_ref, xo_ref, h_ref):
        seg = _seg(pl.program_id(0), T)
        yv = jnp.dot(a_ref[...].astype(BF16), w_ref[...].reshape(N_CHIP * kb, nb), preferred_element_type=F32)
        y_ref[...] = yv
        xv = x_ref[...] + mg_ref[seg, pl.ds(kg, 1), :] * yv
        xo_ref[...] = xv
        r = lax.rsqrt(jnp.mean(xv * xv, axis=-1, keepdims=True) + EPS)
        h_ref[...] = ((xv * r * nw_ref[...]) * (1.0 + mn_ref[seg, pl.ds(kn + 1, 1), :])
                      + mn_ref[seg, pl.ds(kn, 1), :]).astype(BF16)
    tile = pl.BlockSpec((TMR, dm), lambda i: (i, 0))
    modspec = pl.BlockSpec((2, 6, dm), lambda i: (0, 0, 0))
    return pl.pallas_call(
        body, name="mm_resid_norm_fwd", grid=(R // TMR,),
        in_specs=[pl.BlockSpec((TMR, N_CHIP * kb), lambda i: (i, 0)),
                  pl.BlockSpec((N_CHIP, kb, nb), lambda i: (0, 0, 0), pipeline_mode=pl.Buffered(1)),
                  tile, pl.BlockSpec((1, dm), lambda i: (0, 0)), modspec, modspec],
        out_specs=[tile, tile, tile],
        out_shape=[jax.ShapeDtypeStruct((R, dm), F32), jax.ShapeDtypeStruct((R, dm), F32),
                   jax.ShapeDtypeStruct((R, dm), BF16)],
        compiler_params=_params(48))(a, w, x, nw, mod_g, mod_n)


def _halo_specs(width, R):
    nblk = R // HALO
    per = TMR // HALO
    return (pl.BlockSpec((HALO, width), lambda i: (jnp.maximum(i * per - 1, 0), 0)),
            pl.BlockSpec((TMR, width), lambda i: (i, 0)),
            pl.BlockSpec((HALO, width), lambda i: (jnp.minimum((i + 1) * per, nblk - 1), 0)))


def _halo_live(i, T, R):
    nl = T // TMR
    return (i != 0) & (i != nl), (i != nl - 1) & (i != R // TMR - 1)


def _ext(refs, c0, cw, live, halo=HALO):
    pref, ref, nref = refs
    before = jnp.where(live[0], pref[:, c0:c0 + cw].astype(F32)[HALO - halo:], 0.0)
    after = jnp.where(live[1], nref[:, c0:c0 + cw].astype(F32)[:halo], 0.0)
    return jnp.concatenate([before, ref[:, c0:c0 + cw].astype(F32), after], axis=0)


def _at(ext, off, halo=HALO):
    n = ext.shape[0]
    s = (-off) % n
    y = pltpu.roll(ext, s, 0) if s else ext
    return y[halo:halo + TMR]


def ffn_mid_fwd(u, wc, w_down, x, nw, mod_g, mod_n, kg, kn, T):
    R, w2 = u.shape
    dm = x.shape[1]
    _, kb, nb = w_down.shape
    cw = 256
    with_norm = nw is not None

    def body(*refs):
        if with_norm:
            up_ref, u_ref, un_ref, wc_ref, w_ref, x_ref, nw_ref, mg_ref, mn_ref, z_ref, y_ref, xo_ref, h_ref = refs
        else:
            up_ref, u_ref, un_ref, wc_ref, w_ref, x_ref, mg_ref, z_ref, y_ref, xo_ref = refs
        i = pl.program_id(0)
        seg = _seg(i, T)
        live = _halo_live(i, T, R)

        def conv(c0):
            e = _ext((up_ref, u_ref, un_ref), c0, cw, live, 8)
            return (wc_ref[pl.ds(0, 1), c0:c0 + cw] * _at(e, -1, 8) + wc_ref[pl.ds(1, 1), c0:c0 + cw] * _at(e, 0, 8)
                    + wc_ref[pl.ds(2, 1), c0:c0 + cw] * _at(e, 1, 8))
        yv = None
        for j in range(D_FF // cw):
            a = conv(j * cw)
            g = conv(D_FF + j * cw)
            zc = (g * _sigmoid(g) * a).astype(BF16)
            z_ref[:, j * cw:(j + 1) * cw] = zc
            t = jnp.dot(zc, w_ref[j * cw:(j + 1) * cw, :], preferred_element_type=F32)
            yv = t if yv is None else yv + t
        y_ref[...] = yv
        xv = x_ref[...] + mg_ref[seg, pl.ds(kg, 1), :] * yv
        xo_ref[...] = xv
        if with_norm:
            r = lax.rsqrt(jnp.mean(xv * xv, axis=-1, keepdims=True) + EPS)
            h_ref[...] = ((xv * r * nw_ref[...]) * (1.0 + mn_ref[seg, pl.ds(kn + 1, 1), :])
                          + mn_ref[seg, pl.ds(kn, 1), :]).astype(BF16)
    tile = pl.BlockSpec((TMR, dm), lambda i: (i, 0))
    modspec = pl.BlockSpec((2, 6, dm), lambda i: (0, 0, 0))
    w_down = w_down.reshape(N_CHIP * kb, nb)
    in_specs = [*_halo_specs(w2, R), pl.BlockSpec((3, w2), lambda i: (0, 0)),
                pl.BlockSpec((N_CHIP * kb, nb), lambda i: (0, 0), pipeline_mode=pl.Buffered(1)), tile]
    out_specs = [pl.BlockSpec((TMR, D_FF), lambda i: (i, 0)), tile, tile]
    out_shape = [jax.ShapeDtypeStruct((R, D_FF), BF16), jax.ShapeDtypeStruct((R, dm), F32),
                 jax.ShapeDtypeStruct((R, dm), F32)]
    if with_norm:
        return pl.pallas_call(
            body, name="ffn_mid_fwd", grid=(R // TMR,),
            in_specs=in_specs + [pl.BlockSpec((1, dm), lambda i: (0, 0)), modspec, modspec],
            out_specs=out_specs + [tile], out_shape=out_shape + [jax.ShapeDtypeStruct((R, dm), BF16)],
            compiler_params=_params(48))(u, u, u, wc, w_down, x, nw, mod_g, mod_n)
    return pl.pallas_call(
        body, name="ffn_mid_fwd_last", grid=(R // TMR,), in_specs=in_specs + [modspec],
        out_specs=out_specs, out_shape=out_shape, compiler_params=_params(48))(u, u, u, wc, w_down, x, mod_g)


def ffn_mid_bwd(dz, u, wc, T):
    R, w2 = u.shape
    cw = 256

    def body(dz_ref, up_ref, u_ref, un_ref, wc_ref, duc_ref, dwc_ref):
        i = pl.program_id(0)
        live = _halo_live(i, T, R)

        @pl.when(i == 0)
        def _():
            dwc_ref[...] = jnp.zeros_like(dwc_ref)

        def taps(c0):
            e = _ext((up_ref, u_ref, un_ref), c0, cw, live, 8)
            return [_at(e, -1, 8), _at(e, 0, 8), _at(e, 1, 8)]

        def conv(t, c0):
            return (wc_ref[pl.ds(0, 1), c0:c0 + cw] * t[0] + wc_ref[pl.ds(1, 1), c0:c0 + cw] * t[1]
                    + wc_ref[pl.ds(2, 1), c0:c0 + cw] * t[2])
        for j in range(D_FF // cw):
            ca, cg = j * cw, D_FF + j * cw
            dzv = dz_ref[:, ca:ca + cw].astype(F32)
            ta, tg = taps(ca), taps(cg)
            a, g = conv(ta, ca), conv(tg, cg)
            sg = _sigmoid(g)
            da = dzv * (g * sg)
            dg = dzv * a * (sg * (1.0 + g * (1.0 - sg)))
            duc_ref[:, ca:ca + cw] = da.astype(BF16)
            duc_ref[:, cg:cg + cw] = dg.astype(BF16)
            for k in range(3):
                dwc_ref[k, :, ca:ca + cw] += _colsum8(da * ta[k])
                dwc_ref[k, :, cg:cg + cw] += _colsum8(dg * tg[k])
    return pl.pallas_call(
        body, name="ffn_mid_bwd", grid=(R // TMR,),
        in_specs=[pl.BlockSpec((TMR, D_FF), lambda i: (i, 0)), *_halo_specs(w2, R),
                  pl.BlockSpec((3, w2), lambda i: (0, 0))],
        out_specs=[pl.BlockSpec((TMR, w2), lambda i: (i, 0)), pl.BlockSpec((3, 8, w2), lambda i: (0, 0, 0))],
        out_shape=[jax.ShapeDtypeStruct((R, w2), BF16), jax.ShapeDtypeStruct((3, 8, w2), F32)],
        compiler_params=_params(48))(dz, u, u, u, wc)


def ffn_up_bwd(duc, wc, w_up, x, nw, mod_n, dxr, y, mod_g, kn, kg, T):
    R, w2 = duc.shape
    dm = x.shape[1]
    _, kb, nb = w_up.shape
    cw = 128
    contract_last = (((1,), (1,)), ((), ()))

    def body(dp_ref, d_ref, dn_ref, wc_ref, w_ref, x_ref, nw_ref, mn_ref, dxr_ref, y_ref, mg_ref,
             du_ref, dx_ref, dy_ref, dmod_ref, dnw_ref, dg_ref):
        i = pl.program_id(0)
        seg = _seg(i, T)
        live = _halo_live(i, T, R)

        @pl.when(i == 0)
        def _():
            dmod_ref[...] = jnp.zeros_like(dmod_ref)
            dnw_ref[...] = jnp.zeros_like(dnw_ref)
            dg_ref[...] = jnp.zeros_like(dg_ref)
        dhv = None
        for q in range(N_CHIP):
            for j in range(nb // cw):
                c0 = q * nb + j * cw
                e = _ext((dp_ref, d_ref, dn_ref), c0, cw, live, 8)
                du_ref[:, c0:c0 + cw] = (wc_ref[pl.ds(0, 1), c0:c0 + cw] * _at(e, 1, 8)
                                         + wc_ref[pl.ds(1, 1), c0:c0 + cw] * _at(e, 0, 8)
                                         + wc_ref[pl.ds(2, 1), c0:c0 + cw] * _at(e, -1, 8)).astype(BF16)
            t = lax.dot_general(du_ref[:, q * nb:(q + 1) * nb], w_ref[q], contract_last,
                                preferred_element_type=F32)
            dhv = t if dhv is None else dhv + t
        sc = mn_ref[seg, pl.ds(kn + 1, 1), :]
        nwv = nw_ref[...]
        xv = x_ref[...]
        r = lax.rsqrt(jnp.mean(xv * xv, axis=-1, keepdims=True) + EPS)
        xh = xv * r
        dmod_ref[seg, 0] += _colsum8(dhv)
        dmod_ref[seg, 1] += _colsum8(dhv * (xh * nwv))
        dn = dhv * (1.0 + sc)
        dnw_ref[...] += _colsum8(dn * xh)
        dxh = dn * nwv
        dx = dxr_ref[...] + r * (dxh - xh * jnp.mean(dxh * xh, axis=-1, keepdims=True))
        dx_ref[...] = dx
        dy_ref[...] = (mg_ref[seg, pl.ds(kg, 1), :] * dx).astype(BF16)
        dg_ref[seg] += _colsum8(dx * y_ref[...])
    tile = pl.BlockSpec((TMR, dm), lambda i: (i, 0))
    modspec = pl.BlockSpec((2, 6, dm), lambda i: (0, 0, 0))
    return pl.pallas_call(
        body, name="ffn_up_bwd", grid=(R // TMR,),
        in_specs=[*_halo_specs(w2, R), pl.BlockSpec((3, w2), lambda i: (0, 0)),
                  pl.BlockSpec((N_CHIP, kb, nb), lambda i: (0, 0, 0), pipeline_mode=pl.Buffered(1)),
                  tile, pl.BlockSpec((1, dm), lambda i: (0, 0)), modspec, tile, tile, modspec],
        out_specs=[pl.BlockSpec((TMR, w2), lambda i: (i, 0)), tile, tile,
                   pl.BlockSpec((2, 2, 8, dm), lambda i: (0, 0, 0, 0)), pl.BlockSpec((8, dm), lambda i: (0, 0)),
                   pl.BlockSpec((2, 8, dm), lambda i: (0, 0, 0))],
        out_shape=[jax.ShapeDtypeStruct((R, w2), BF16), jax.ShapeDtypeStruct((R, dm), F32),
                   jax.ShapeDtypeStruct((R, dm), BF16), jax.ShapeDtypeStruct((2, 2, 8, dm), F32),
                   jax.ShapeDtypeStruct((8, dm), F32), jax.ShapeDtypeStruct((2, 8, dm), F32)],
        compiler_params=_params(48))(duc, duc, duc, wc, w_up, x, nw, mod_n, dxr, y, mod_g)


_CW = 128


def _mixer_a(prefs, wa_ref, live):
    cin = _ext(prefs, A_W, A_W, live) * _ext(prefs, 2 * A_W, A_W, live)
    ca = (wa_ref[pl.ds(0, 1), :] * _at(cin, -1) + wa_ref[pl.ds(1, 1), :] * _at(cin, 0)
          + wa_ref[pl.ds(2, 1), :] * _at(cin, 1))
    return cin, ca


def _mixer_b(prefs, wb_ref, bias_ref, live, ub_s, ub2_s):
    for cc in range(A_W // _CW):
        c0 = cc * _CW
        ub = _ext(prefs, 3 * A_W + c0, _CW, live) * _sigmoid(_ext(prefs, 4 * A_W + c0, _CW, live))
        ub_s[:, c0:c0 + _CW] = ub
        acc = jnp.zeros((TMR, _CW), F32) + bias_ref[:, c0:c0 + _CW]
        for k in range(B_CONV):
            acc = acc + wb_ref[pl.ds(k, 1), c0:c0 + _CW] * _at(ub, k - B_CONV // 2)
        ub2_s[:, c0:c0 + _CW] = acc


def _layernorm_stats(v):
    mu = jnp.mean(v, axis=-1, keepdims=True)
    xc = v - mu
    rs = lax.rsqrt(jnp.mean(xc * xc, axis=-1, keepdims=True) + EPS)
    return xc * rs, rs


def mixer_fwd(p, wa, wb, bias, lng, lnb, w_out, x, nw, mod, kg, kn, T):
    R, wp = p.shape
    dm = x.shape[1]
    _, kb, nb = w_out.shape

    def body(pp_ref, p_ref, pn_ref, wa_ref, wb_ref, bias_ref, lng_ref, lnb_ref, w_ref, x_ref, nw_ref, mod_ref,
             o_ref, y_ref, xo_ref, h_ref, ub_s, ub2_s):
        i = pl.program_id(0)
        seg = _seg(i, T)
        live = _halo_live(i, T, R)
        prefs = (pp_ref, p_ref, pn_ref)
        _, ca = _mixer_a(prefs, wa_ref, live)
        ya = (p_ref[:, 0:A_W].astype(F32) * ca).astype(BF16)
        o_ref[:, 0:A_W] = ya
        yv = jnp.dot(ya, w_ref[0:A_W, :], preferred_element_type=F32)
        _mixer_b(prefs, wb_ref, bias_ref, live, ub_s, ub2_s)
        xh, _ = _layernorm_stats(ub2_s[...])
        lv = xh * lng_ref[...] + lnb_ref[...]
        yb = (lv * _sigmoid(lv)).astype(BF16)
        o_ref[:, A_W:2 * A_W] = yb
        yv = yv + jnp.dot(yb, w_ref[A_W:2 * A_W, :], preferred_element_type=F32)
        y_ref[...] = yv
        xv = x_ref[...] + mod_ref[seg, pl.ds(kg, 1), :] * yv
        xo_ref[...] = xv
        r = lax.rsqrt(jnp.mean(xv * xv, axis=-1, keepdims=True) + EPS)
        h_ref[...] = ((xv * r * nw_ref[...]) * (1.0 + mod_ref[seg, pl.ds(kn + 1, 1), :])
                      + mod_ref[seg, pl.ds(kn, 1), :]).astype(BF16)
    vec = pl.BlockSpec((1, A_W), lambda i: (0, 0))
    tile = pl.BlockSpec((TMR, dm), lambda i: (i, 0))
    return pl.pallas_call(
        body, name="mixer_fwd", grid=(R // TMR,),
        in_specs=[*_halo_specs(wp, R), pl.BlockSpec((3, A_W), lambda i: (0, 0)),
                  pl.BlockSpec((B_CONV, A_W), lambda i: (0, 0)), vec, vec, vec,
                  pl.BlockSpec((N_CHIP * kb, nb), lambda i: (0, 0), pipeline_mode=pl.Buffered(1)),
                  tile, pl.BlockSpec((1, dm), lambda i: (0, 0)), pl.BlockSpec((2, 6, dm), lambda i: (0, 0, 0))],
        out_specs=[pl.BlockSpec((TMR, 2 * A_W), lambda i: (i, 0)), tile, tile, tile],
        out_shape=[jax.ShapeDtypeStruct((R, 2 * A_W), BF16), jax.ShapeDtypeStruct((R, dm), F32),
                   jax.ShapeDtypeStruct((R, dm), F32), jax.ShapeDtypeStruct((R, dm), BF16)],
        scratch_shapes=[pltpu.VMEM((TMR + 2 * HALO, A_W), F32), pltpu.VMEM((TMR, A_W), F32)],
        compiler_params=_params(48))(p, p, p, wa, wb, bias, lng, lnb, w_out.reshape(N_CHIP * kb, nb), x, nw, mod)


def convmix_bwd1(dyab, p, wa, wb, bias, lng, lnb, T):
    R, wp = p.shape

    def body(dy_ref, pp_ref, p_ref, pn_ref, wa_ref, wb_ref, bias_ref, lng_ref, lnb_ref,
             dmid_ref, dwa_ref, dwb_ref, dvec_ref, ub_s, ub2_s):
        i = pl.program_id(0)
        live = _halo_live(i, T, R)

        @pl.when(i == 0)
        def _():
            dwa_ref[...] = jnp.zeros_like(dwa_ref)
            dwb_ref[...] = jnp.zeros_like(dwb_ref)
            dvec_ref[...] = jnp.zeros_like(dvec_ref)
        prefs = (pp_ref, p_ref, pn_ref)
        cin, ca = _mixer_a(prefs, wa_ref, live)
        dya = dy_ref[:, 0:A_W]
        dmid_ref[:, 0:A_W] = dya * ca
        dca = dya * p_ref[:, 0:A_W].astype(F32)
        dmid_ref[:, A_W:2 * A_W] = dca
        for k in range(3):
            dwa_ref[k] += _colsum8(dca * _at(cin, k - 1))
        _mixer_b(prefs, wb_ref, bias_ref, live, ub_s, ub2_s)
        xh, rs = _layernorm_stats(ub2_s[...])
        gain = lng_ref[...]
        lv = xh * gain + lnb_ref[...]
        sl = _sigmoid(lv)
        dl = dy_ref[:, A_W:2 * A_W] * (sl * (1.0 + lv * (1.0 - sl)))
        dvec_ref[1] += _colsum8(dl * xh)
        dvec_ref[2] += _colsum8(dl)
        dxh = dl * gain
        dub2 = rs * (dxh - jnp.mean(dxh, axis=-1, keepdims=True)
                     - xh * jnp.mean(dxh * xh, axis=-1, keepdims=True))
        dvec_ref[0] += _colsum8(dub2)
        dmid_ref[:, 2 * A_W:3 * A_W] = dub2
        for cc in range(A_W // _CW):
            c0 = cc * _CW
            ub = ub_s[:, c0:c0 + _CW]
            d = dmid_ref[:, 2 * A_W + c0:2 * A_W + c0 + _CW]
            for k in range(B_CONV):
                dwb_ref[k, :, c0:c0 + _CW] += _colsum8(d * _at(ub, k - B_CONV // 2))
    vec = pl.BlockSpec((1, A_W), lambda i: (0, 0))
    return pl.pallas_call(
        body, name="convmix_bwd1", grid=(R // TMR,),
        in_specs=[pl.BlockSpec((TMR, 2 * A_W), lambda i: (i, 0)), *_halo_specs(wp, R),
                  pl.BlockSpec((3, A_W), lambda i: (0, 0)), pl.BlockSpec((B_CONV, A_W), lambda i: (0, 0)),
                  vec, vec, vec],
        out_specs=[pl.BlockSpec((TMR, 3 * A_W), lambda i: (i, 0)),
                   pl.BlockSpec((3, 8, A_W), lambda i: (0, 0, 0)),
                   pl.BlockSpec((B_CONV, 8, A_W), lambda i: (0, 0, 0)),
                   pl.BlockSpec((3, 8, A_W), lambda i: (0, 0, 0))],
        out_shape=[jax.ShapeDtypeStruct((R, 3 * A_W), F32), jax.ShapeDtypeStruct((3, 8, A_W), F32),
                   jax.ShapeDtypeStruct((B_CONV, 8, A_W), F32), jax.ShapeDtypeStruct((3, 8, A_W), F32)],
        scratch_shapes=[pltpu.VMEM((TMR + 2 * HALO, A_W), F32), pltpu.VMEM((TMR, A_W), F32)],
        compiler_params=_params(48))(dyab, p, p, p, wa, wb, bias, lng, lnb)


def mixer_in_bwd(dmid, p, wa, wb, w_in, x, nw, mod_n, dxr, y, mod_g, kn, kg, T):
    R, wp = p.shape
    dm = x.shape[1]
    _, kb, nb = w_in.shape
    with_resid = y is not None
    contract_last = (((1,), (1,)), ((), ()))

    def body(*refs):
        if with_resid:
            (mp_ref, m_ref, mn_ref, p_ref, wa_ref, wb_ref, w_ref, x_ref, nw_ref, mnorm_ref, dxr_ref, y_ref, mg_ref,
             dp_ref, dx_ref, dy_ref, dmod_ref, dnw_ref, dg_ref) = refs
        else:
            (mp_ref, m_ref, mn_ref, p_ref, wa_ref, wb_ref, w_ref, x_ref, nw_ref, mnorm_ref, dxr_ref,
             dp_ref, dx_ref, dmod_ref, dnw_ref) = refs
        i = pl.program_id(0)
        seg = _seg(i, T)
        live = _halo_live(i, T, R)

        @pl.when(i == 0)
        def _():
            dmod_ref[...] = jnp.zeros_like(dmod_ref)
            dnw_ref[...] = jnp.zeros_like(dnw_ref)
            if with_resid:
                dg_ref[...] = jnp.zeros_like(dg_ref)

        def block(q):
            return lax.dot_general(dp_ref[:, q * nb:(q + 1) * nb], w_ref[q], contract_last,
                                   preferred_element_type=F32)
        mrefs = (mp_ref, m_ref, mn_ref)
        dp_ref[:, 0:A_W] = m_ref[:, 0:A_W].astype(BF16)
        dca = _ext(mrefs, A_W, A_W, live)
        dcin = (wa_ref[pl.ds(0, 1), :] * _at(dca, 1) + wa_ref[pl.ds(1, 1), :] * _at(dca, 0)
                + wa_ref[pl.ds(2, 1), :] * _at(dca, -1))
        dp_ref[:, A_W:2 * A_W] = (dcin * p_ref[:, 2 * A_W:3 * A_W].astype(F32)).astype(BF16)
        dp_ref[:, 2 * A_W:3 * A_W] = (dcin * p_ref[:, A_W:2 * A_W].astype(F32)).astype(BF16)
        dhv = block(0) + block(1)
        for cc in range(A_W // _CW):
            c0 = cc * _CW
            d = _ext(mrefs, 2 * A_W + c0, _CW, live)
            dub = jnp.zeros((TMR, _CW), F32)
            for k in range(B_CONV):
                dub = dub + wb_ref[pl.ds(k, 1), c0:c0 + _CW] * _at(d, B_CONV // 2 - k)
            vb = p_ref[:, 3 * A_W + c0:3 * A_W + c0 + _CW].astype(F32)
            s = _sigmoid(p_ref[:, 4 * A_W + c0:4 * A_W + c0 + _CW].astype(F32))
            dp_ref[:, 3 * A_W + c0:3 * A_W + c0 + _CW] = (dub * s).astype(BF16)
            dp_ref[:, 4 * A_W + c0:4 * A_W + c0 + _CW] = (dub * vb * s * (1.0 - s)).astype(BF16)
        dhv = dhv + block(2) + block(3)
        sc = mnorm_ref[seg, pl.ds(kn + 1, 1), :]
        nwv = nw_ref[...]
        xv = x_ref[...]
        r = lax.rsqrt(jnp.mean(xv * xv, axis=-1, keepdims=True) + EPS)
        xh = xv * r
        dmod_ref[seg, 0] += _colsum8(dhv)
        dmod_ref[seg, 1] += _colsum8(dhv * (xh * nwv))
        dn = dhv * (1.0 + sc)
        dnw_ref[...] += _colsum8(dn * xh)
        dxh = dn * nwv
        dx = dxr_ref[...] + r * (dxh - xh * jnp.mean(dxh * xh, axis=-1, keepdims=True))
        dx_ref[...] = dx
        if with_resid:
            dy_ref[...] = (mg_ref[seg, pl.ds(kg, 1), :] * dx).astype(BF16)
            dg_ref[seg] += _colsum8(dx * y_ref[...])
    assert 2 * nb <= 3 * A_W and N_CHIP * nb == wp
    tile = pl.BlockSpec((TMR, dm), lambda i: (i, 0))
    modspec = pl.BlockSpec((2, 6, dm), lambda i: (0, 0, 0))
    in_specs = [*_halo_specs(3 * A_W, R), pl.BlockSpec((TMR, wp), lambda i: (i, 0)),
                pl.BlockSpec((3, A_W), lambda i: (0, 0)), pl.BlockSpec((B_CONV, A_W), lambda i: (0, 0)),
                pl.BlockSpec((N_CHIP, kb, nb), lambda i: (0, 0, 0), pipeline_mode=pl.Buffered(1)),
                tile, pl.BlockSpec((1, dm), lambda i: (0, 0)), modspec, tile]
    dp_spec = pl.BlockSpec((TMR, wp), lambda i: (i, 0))
    acc_specs = [pl.BlockSpec((2, 2, 8, dm), lambda i: (0, 0, 0, 0)), pl.BlockSpec((8, dm), lambda i: (0, 0))]
    acc_shapes = [jax.ShapeDtypeStruct((2, 2, 8, dm), F32), jax.ShapeDtypeStruct((8, dm), F32)]
    dp_shape, dx_shape = jax.ShapeDtypeStruct((R, wp), BF16), jax.ShapeDtypeStruct((R, dm), F32)
    if with_resid:
        return pl.pallas_call(
            body, name="mixer_in_bwd", grid=(R // TMR,), in_specs=in_specs + [tile, modspec],
            out_specs=[dp_spec, tile, tile] + acc_specs + [pl.BlockSpec((2, 8, dm), lambda i: (0, 0, 0))],
            out_shape=[dp_shape, dx_shape, jax.ShapeDtypeStruct((R, dm), BF16)] + acc_shapes
            + [jax.ShapeDtypeStruct((2, 8, dm), F32)],
            compiler_params=_params(48))(dmid, dmid, dmid, p, wa, wb, w_in, x, nw, mod_n, dxr, y, mod_g)
    return pl.pallas_call(
        body, name="mixer_in_bwd_first", grid=(R // TMR,), in_specs=in_specs,
        out_specs=[dp_spec, tile] + acc_specs, out_shape=[dp_shape, dx_shape] + acc_shapes,
        compiler_params=_params(48))(dmid, dmid, dmid, p, wa, wb, w_in, x, nw, mod_n, dxr)


def _rot_half(v):
    w = v.shape[-1]
    lane = lax.broadcasted_iota(jnp.int32, (1, w), 1)
    return jnp.where(lane % HEAD_DIM < HEAD_DIM // 2, pltpu.roll(v, w - HEAD_DIM // 2, 1),
                     pltpu.roll(v, HEAD_DIM // 2, 1))


def mm_qkv_rope(a, w, cs, sn):
    R = a.shape[0]
    _, kb, nb = w.shape
    wq = N_CHIP * nb
    tm = _row_tile(R)
    qw = N_HEADS * HEAD_DIM
    kw = (wq - qw) // 2
    scale = HEAD_DIM ** -0.5

    def body(a_ref, w_ref, cs_ref, sn_ref, o_ref, x_ref):
        av = a_ref[...].astype(BF16)
        for q in range(N_CHIP):
            x_ref[:, q * nb:(q + 1) * nb] = jnp.dot(av, w_ref[q], preferred_element_type=F32)
        c, s = cs_ref[...], sn_ref[...]
        q = x_ref[:, 0:qw]
        o_ref[:, 0:qw] = ((q * jnp.tile(c, (1, qw // 128)) + _rot_half(q) * jnp.tile(s, (1, qw // 128)))
                          * scale).astype(BF16)
        k = x_ref[:, qw:qw + kw]
        o_ref[:, qw:qw + kw] = (k * jnp.tile(c, (1, kw // 128))
                                + _rot_half(k) * jnp.tile(s, (1, kw // 128))).astype(BF16)
        o_ref[:, qw + kw:] = x_ref[:, qw + kw:].astype(BF16)
    tab = pl.BlockSpec((tm, 128), lambda i: (i, 0))
    return pl.pallas_call(
        body, name="mm_qkv_rope", grid=(R // tm,),
        in_specs=[pl.BlockSpec((tm, kb), lambda i: (i, 0)),
                  pl.BlockSpec((N_CHIP, kb, nb), lambda i: (0, 0, 0), pipeline_mode=pl.Buffered(1)), tab, tab],
        out_specs=pl.BlockSpec((tm, wq), lambda i: (i, 0)),
        out_shape=jax.ShapeDtypeStruct((R, wq), BF16), scratch_shapes=[pltpu.VMEM((tm, wq), F32)],
        compiler_params=_params(48))(a, w, cs, sn)


def attn_in_bwd(dq, dks, dvs, dkc, dvc, cs, sn, w, x, nw, mod_n, dxr, y, mod_g, kn, kg, T):
    R, qw = dq.shape
    kw = dkc.shape[1]
    dm = x.shape[1]
    _, kb, nbw = w.shape
    nb = R // QB
    nl = T // QB
    scale = HEAD_DIM ** -0.5
    contract_last = (((1,), (1,)), ((), ()))

    def body(dq_ref, kp0, kp1, ko_ref, kn0, kn1, vp0, vp1, vo_ref, vn0, vn1, kc_ref, vc_ref, cs_ref, sn_ref,
             w_ref, x_ref, nw_ref, mnorm_ref, dxr_ref, y_ref, mg_ref,
             o_ref, dx_ref, dy_ref, dmod_ref, dnw_ref, dg_ref):
        kp_ref, kn_ref, vp_ref, vn_ref = (kp0, kp1), (kn0, kn1), (vp0, vp1), (vn0, vn1)
        i = pl.program_id(0)
        seg = _seg(i, T)

        @pl.when(i == 0)
        def _():
            dmod_ref[...] = jnp.zeros_like(dmod_ref)
            dnw_ref[...] = jnp.zeros_like(dnw_ref)
            dg_ref[...] = jnp.zeros_like(dg_ref)
        c, s = cs_ref[...], sn_ref[...]
        has_next = (2 * i + 2 < nb).astype(F32)
        has_prev = (i >= 1).astype(F32)
        is_ctx = (i >= T // TMR).astype(F32)

        def band(own, from_next, from_prev, ctx_sum):
            return (own[...] + jnp.concatenate([from_next[0][...], from_next[1][...] * has_next], axis=0)
                    + jnp.concatenate([from_prev[0][...] * has_prev, from_prev[1][...]], axis=0)
                    + ctx_sum[...] * is_ctx)
        g = dq_ref[...] * scale
        o_ref[:, 0:qw] = (g * jnp.tile(c, (1, qw // 128)) + _rot_half(g * jnp.tile(s, (1, qw // 128)))).astype(BF16)
        g = band(ko_ref, kp_ref, kn_ref, kc_ref)
        o_ref[:, qw:qw + kw] = (g * jnp.tile(c, (1, kw // 128))
                                + _rot_half(g * jnp.tile(s, (1, kw // 128)))).astype(BF16)
        o_ref[:, qw + kw:] = band(vo_ref, vp_ref, vn_ref, vc_ref).astype(BF16)
        dhv = None
        for q in range(N_CHIP):
            t = lax.dot_general(o_ref[:, q * nbw:(q + 1) * nbw], w_ref[q], contract_last,
                                preferred_element_type=F32)
            dhv = t if dhv is None else dhv + t
        sc = mnorm_ref[seg, pl.ds(kn + 1, 1), :]
        nwv = nw_ref[...]
        xv = x_ref[...]
        r = lax.rsqrt(jnp.mean(xv * xv, axis=-1, keepdims=True) + EPS)
        xh = xv * r
        dmod_ref[seg, 0] += _colsum8(dhv)
        dmod_ref[seg, 1] += _colsum8(dhv * (xh * nwv))
        dn = dhv * (1.0 + sc)
        dnw_ref[...] += _colsum8(dn * xh)
        dxh = dn * nwv
        dx = dxr_ref[...] + r * (dxh - xh * jnp.mean(dxh * xh, axis=-1, keepdims=True))
        dx_ref[...] = dx
        dy_ref[...] = (mg_ref[seg, pl.ds(kg, 1), :] * dx).astype(BF16)
        dg_ref[seg] += _colsum8(dx * y_ref[...])
    own = pl.BlockSpec((TMR, kw), lambda i: (i, 0))
    from_next = [pl.BlockSpec((QB, kw), lambda i: (2 * i + 1, 0)),
                 pl.BlockSpec((QB, kw), lambda i: (jnp.minimum(2 * i + 2, nb - 1), 0))]
    from_prev = [pl.BlockSpec((QB, kw), lambda i: (jnp.maximum(2 * i - 1, 0), 0)),
                 pl.BlockSpec((QB, kw), lambda i: (2 * i, 0))]
    ctx = pl.BlockSpec((TMR, kw), lambda i: (jnp.maximum(i - T // TMR, 0), 0))
    tab = pl.BlockSpec((TMR, 128), lambda b: (b, 0))
    tile = pl.BlockSpec((TMR, dm), lambda b: (b, 0))
    modspec = pl.BlockSpec((2, 6, dm), lambda b: (0, 0, 0))
    return pl.pallas_call(
        body, name="attn_in_bwd", grid=(R // TMR,),
        in_specs=[pl.BlockSpec((TMR, qw), lambda b: (b, 0)), *from_next, own, *from_prev, *from_next, own, *from_prev,
                  ctx, ctx, tab, tab,
                  pl.BlockSpec((N_CHIP, kb, nbw), lambda b: (0, 0, 0), pipeline_mode=pl.Buffered(1)),
                  tile, pl.BlockSpec((1, dm), lambda b: (0, 0)), modspec, tile, tile, modspec],
        out_specs=[pl.BlockSpec((TMR, qw + 2 * kw), lambda b: (b, 0)), tile, tile,
                   pl.BlockSpec((2, 2, 8, dm), lambda b: (0, 0, 0, 0)), pl.BlockSpec((8, dm), lambda b: (0, 0)),
                   pl.BlockSpec((2, 8, dm), lambda b: (0, 0, 0))],
        out_shape=[jax.ShapeDtypeStruct((R, qw + 2 * kw), BF16), jax.ShapeDtypeStruct((R, dm), F32),
                   jax.ShapeDtypeStruct((R, dm), BF16), jax.ShapeDtypeStruct((2, 2, 8, dm), F32),
                   jax.ShapeDtypeStruct((8, dm), F32), jax.ShapeDtypeStruct((2, 8, dm), F32)],
        compiler_params=_params(48))(
            dq, dks[0], dks[0], dks[1], dks[2], dks[2], dvs[0], dvs[0], dvs[1], dvs[2], dvs[2], dkc, dvc, cs, sn,
            w, x, nw, mod_n, dxr, y, mod_g)


def _attn_specs(T, R):
    nl = T // QB
    kblk = N_HEADS * HEAD_DIM // 256

    def band(col, shift):
        return pl.BlockSpec((QB, 256), lambda b: (jnp.clip(b + shift, 0, nl - 1), col))

    def ctx(col):
        return pl.BlockSpec((R - T, 256), lambda b: (T // (R - T), col))
    q = pl.BlockSpec((QB, N_HEADS * HEAD_DIM), lambda b: (b, 0))
    return (q, [band(kblk, -1), band(kblk, 0), band(kblk, 1), ctx(kblk)],
            [band(kblk + 1, -1), band(kblk + 1, 0), band(kblk + 1, 1), ctx(kblk + 1)])


def _attn_common(T, R):
    nl = T // QB
    nk = 3 * QB + (R - T)

    def low_lanes():
        return lax.broadcasted_iota(jnp.int32, (1, 128), 1) < HEAD_DIM

    def dup(v, par):
        low = low_lanes()
        vf = v.astype(F32)
        r = pltpu.roll(vf, HEAD_DIM, 1)
        return (jnp.where(low, vf, r) if par == 0 else jnp.where(low, r, vf)).astype(BF16)

    def stack(ref, par, base):
        low = low_lanes()
        pa = ref[:, base + (2 * par) * 128:base + (2 * par + 1) * 128].astype(BF16)
        pb = ref[:, base + (2 * par + 1) * 128:base + (2 * par + 2) * 128].astype(BF16)
        zero = jnp.zeros_like(pa)
        return jnp.concatenate([jnp.where(low, pa, zero), jnp.where(low, zero, pa),
                                jnp.where(low, pb, zero), jnp.where(low, zero, pb)], axis=0)

    def unstack(v):
        low = low_lanes()
        return (jnp.where(low, v[0:QB], v[QB:2 * QB]), jnp.where(low, v[2 * QB:3 * QB], v[3 * QB:4 * QB]))

    def mask_of(b):
        col = lax.broadcasted_iota(jnp.int32, (1, nk), 1)
        gone = (((col < QB) & (b == 0)) | ((col >= 2 * QB) & (col < 3 * QB) & (b == nl - 1))
                | ((col < 3 * QB) & (b >= nl)))
        return jnp.where(gone, NEG_INF, 0.0)

    def sink_col(sink_ref, first):
        blk = lax.broadcasted_iota(jnp.int32, (4 * QB, 1), 0) // QB
        out = jnp.zeros((4 * QB, 1), F32) + sink_ref[first]
        for h in range(1, 4):
            out = jnp.where(blk == h, sink_ref[first + h], out)
        return out

    def scores(qs, kd, mask, sink):
        s = lax.dot_general(qs, kd, (((1,), (1,)), ((), ())), preferred_element_type=F32) + mask
        m = jnp.maximum(jnp.max(s, axis=-1, keepdims=True), sink)
        e = jnp.exp(s - m)
        es = jnp.exp(sink - m)
        return e, es, 1.0 / (jnp.sum(e, axis=-1, keepdims=True) + es)
    return low_lanes, dup, stack, unstack, mask_of, sink_col, scores


def window_bias(T, R):
    nk = 3 * QB + (R - T)
    row = jnp.arange(QB)[:, None]
    col = jnp.arange(nk)[None, :]
    near = (jnp.abs(col - QB - row) <= WINDOW) | (col >= 3 * QB)
    return jnp.tile(jnp.where(near, 0.0, NEG_INF).astype(F32), (4, 1))


def _probs_spec(nk):
    return pl.BlockSpec((2, None, 2, 4 * QB, nk + 128), lambda b: (0, b, 0, 0, 0))


def attn_fwd(qkvr, sinks, bias, T):
    R = qkvr.shape[0]
    nk = bias.shape[1]
    qspec, kspecs, vspecs = _attn_specs(T, R)
    _, dup, stack, unstack, mask_of, sink_col, scores = _attn_common(T, R)

    def body(q_ref, kp, ko, kn, kc, vp, vo, vn, vc, sink_ref, bias_ref, o_ref, p_ref):
        mask = bias_ref[...] + mask_of(pl.program_id(0))
        for jj in range(2):
            kv = slice(jj * 128, (jj + 1) * 128)
            k_all = jnp.concatenate([kp[:, kv], ko[:, kv], kn[:, kv], kc[:, kv]], axis=0)
            v_all = jnp.concatenate([vp[:, kv], vo[:, kv], vn[:, kv], vc[:, kv]], axis=0)
            for par in range(2):
                kd, vd = dup(k_all, par), dup(v_all, par)
                e, es, rz = scores(stack(q_ref, par, jj * 512), kd, mask, sink_col(sink_ref, jj * 8 + par * 4))
                p = (e * rz).astype(BF16)
                p_ref[jj, par, :, 0:nk] = p
                p_ref[jj, par, :, nk:nk + 128] = jnp.broadcast_to(es * rz, (4 * QB, 128)).astype(BF16)
                o = jnp.dot(p, vd, preferred_element_type=F32)
                pa, pb = unstack(o)
                c0 = jj * 512 + 2 * par * 128
                o_ref[:, c0:c0 + 128] = pa.astype(BF16)
                o_ref[:, c0 + 128:c0 + 256] = pb.astype(BF16)
    return pl.pallas_call(
        body, name="attn_fwd", grid=(R // QB,),
        in_specs=[qspec, *kspecs, *vspecs, pl.BlockSpec(memory_space=pltpu.SMEM),
                  pl.BlockSpec(bias.shape, lambda b: (0, 0))],
        out_specs=[pl.BlockSpec((QB, N_HEADS * HEAD_DIM), lambda b: (b, 0)), _probs_spec(nk)],
        out_shape=[jax.ShapeDtypeStruct((R, N_HEADS * HEAD_DIM), BF16),
                   jax.ShapeDtypeStruct((2, R // QB, 2, 4 * QB, nk + 128), BF16)],
        compiler_params=_params(48))(qkvr, *([qkvr] * 8), sinks, bias)


def attn_bwd(qkvr, do, probs, T):
    R = qkvr.shape[0]
    tc = R - T
    nk = probs.shape[-1] - 128
    qspec, kspecs, vspecs = _attn_specs(T, R)
    _, dup, stack, unstack, _, _, _ = _attn_common(T, R)
    contract_rows = (((0,), (0,)), ((), ()))
    contract_last = (((1,), (1,)), ((), ()))

    def body(q_ref, kp, ko, kn, kc, vp, vo, vn, vc, do_ref, p_ref,
             dq_ref, dkp, dko, dkn, dvp, dvo, dvn, dkc_ref, dvc_ref, dsink_ref):
        @pl.when(pl.program_id(0) == 0)
        def _():
            dsink_ref[...] = jnp.zeros_like(dsink_ref)
            dkc_ref[...] = jnp.zeros_like(dkc_ref)
            dvc_ref[...] = jnp.zeros_like(dvc_ref)
        lane = lax.broadcasted_iota(jnp.int32, (8, 128), 1)
        srow = lax.broadcasted_iota(jnp.int32, (8, 128), 0)
        low_rows = lax.broadcasted_iota(jnp.int32, (128, 1), 0) < HEAD_DIM
        for jj in range(2):
            kv = slice(jj * 128, (jj + 1) * 128)
            k_all = jnp.concatenate([kp[:, kv], ko[:, kv], kn[:, kv], kc[:, kv]], axis=0)
            v_all = jnp.concatenate([vp[:, kv], vo[:, kv], vn[:, kv], vc[:, kv]], axis=0)
            dk_fold, dv_fold = [], []
            for par in range(2):
                kd, vd = dup(k_all, par), dup(v_all, par)
                first = jj * 8 + par * 4
                qs, dos = stack(q_ref, par, jj * 512), stack(do_ref, par, jj * 512)
                p16 = p_ref[jj, par, :, 0:nk]
                p = p16.astype(F32)
                ps = jnp.max(p_ref[jj, par, :, nk:nk + 128].astype(F32), axis=-1, keepdims=True)
                dp = lax.dot_general(dos, vd, contract_last, preferred_element_type=F32)
                delta = jnp.sum(p * dp, axis=-1, keepdims=True)
                ds = (p * (dp - delta)).astype(BF16)
                t = ps * delta
                for h in range(4):
                    dsink = -jnp.sum(t[h * QB:(h + 1) * QB])
                    dsink_ref[...] += jnp.where((lane == first + h) & (srow == 0), dsink, 0.0)
                pa, pb = unstack(jnp.dot(ds, kd, preferred_element_type=F32))
                c0 = jj * 512 + 2 * par * 128
                dq_ref[:, c0:c0 + 128] = pa
                dq_ref[:, c0 + 128:c0 + 256] = pb
                dk_t = lax.dot_general(qs, ds, contract_rows, preferred_element_type=F32)
                dv_t = lax.dot_general(dos, p16, contract_rows, preferred_element_type=F32)
                dk_fold.append(dk_t + pltpu.roll(dk_t, HEAD_DIM, 0))
                dv_fold.append(dv_t + pltpu.roll(dv_t, HEAD_DIM, 0))
            dk = jnp.where(low_rows, dk_fold[0], dk_fold[1]).T
            dv = jnp.where(low_rows, dv_fold[0], dv_fold[1]).T
            dkp[:, kv], dko[:, kv], dkn[:, kv] = dk[0:QB], dk[QB:2 * QB], dk[2 * QB:3 * QB]
            dvp[:, kv], dvo[:, kv], dvn[:, kv] = dv[0:QB], dv[QB:2 * QB], dv[2 * QB:3 * QB]
            dkc_ref[:, kv] += dk[3 * QB:]
            dvc_ref[:, kv] += dv[3 * QB:]
    blk = pl.BlockSpec((QB, 256), lambda b: (b, 0))
    cblk = pl.BlockSpec((tc, 256), lambda b: (0, 0))
    part = jax.ShapeDtypeStruct((R, 256), F32)
    csum = jax.ShapeDtypeStruct((tc, 256), F32)
    full = pl.BlockSpec((QB, N_HEADS * HEAD_DIM), lambda b: (b, 0))
    outs = pl.pallas_call(
        body, name="attn_bwd", grid=(R // QB,),
        in_specs=[qspec, *kspecs, *vspecs, full, _probs_spec(nk)],
        out_specs=[full, blk, blk, blk, blk, blk, blk, cblk, cblk, pl.BlockSpec((8, 128), lambda b: (0, 0))],
        out_shape=[jax.ShapeDtypeStruct((R, N_HEADS * HEAD_DIM), F32), part, part, part, part, part, part,
                   csum, csum, jax.ShapeDtypeStruct((8, 128), F32)],
        compiler_params=_params(48))(qkvr, *([qkvr] * 8), do, probs)
    return outs[0], outs[1:4], outs[4:7], outs[7], outs[8], outs[9]


def loss_head(x, nw, target, y, mod, kg, T):
    R, dm = x.shape
    nl = T // TMR

    def body(x_ref, nw_ref, t_ref, y_ref, mod_ref, loss_ref, dx_ref, dnw_ref, dy_ref, dg_ref):
        i = pl.program_id(0)
        seg = _seg(i, T)

        @pl.when(i == 0)
        def _():
            loss_ref[...] = jnp.zeros_like(loss_ref)
            dnw_ref[...] = jnp.zeros_like(dnw_ref)
            dg_ref[...] = jnp.zeros_like(dg_ref)
        live = (i < nl).astype(F32)
        nwv = nw_ref[...]
        xv = x_ref[...]
        r = lax.rsqrt(jnp.mean(xv * xv, axis=-1, keepdims=True) + EPS)
        xh = xv * r
        err = xh * nwv - t_ref[...]
        per_row = jnp.mean(err * err, axis=-1, keepdims=True)
        loss_ref[...] += 0.5 * live * jnp.sum(per_row)
        dy = err * (live / dm)
        dnw_ref[...] += _colsum8(dy * xh)
        dxh = dy * nwv
        dx = r * (dxh - xh * jnp.mean(dxh * xh, axis=-1, keepdims=True))
        dx_ref[...] = dx
        dy_ref[...] = (mod_ref[seg, pl.ds(kg, 1), :] * dx).astype(BF16)
        dg_ref[seg] += _colsum8(dx * y_ref[...])
    tile = pl.BlockSpec((TMR, dm), lambda i: (i, 0))
    return pl.pallas_call(
        body, name="loss_head", grid=(R // TMR,),
        in_specs=[tile, pl.BlockSpec((1, dm), lambda i: (0, 0)),
                  pl.BlockSpec((TMR, dm), lambda i: (jnp.minimum(i, nl - 1), 0)), tile,
                  pl.BlockSpec((2, 6, dm), lambda i: (0, 0, 0))],
        out_specs=[pl.BlockSpec((8, 128), lambda i: (0, 0)), tile, pl.BlockSpec((8, dm), lambda i: (0, 0)), tile,
                   pl.BlockSpec((2, 8, dm), lambda i: (0, 0, 0))],
        out_shape=[jax.ShapeDtypeStruct((8, 128), F32), jax.ShapeDtypeStruct((R, dm), F32),
                   jax.ShapeDtypeStruct((8, dm), F32), jax.ShapeDtypeStruct((R, dm), BF16),
                   jax.ShapeDtypeStruct((2, 8, dm), F32)])(x, nw, target, y, mod)


def adaln_fwd(cond, w_mod, b_mod):
    nl, dm, ns = w_mod.shape

    def body(c_ref, w_ref, b_ref, o_ref):
        cv = c_ref[...]
        s = (cv * _sigmoid(cv)).astype(BF16)
        o_ref[...] = jnp.dot(s, w_ref[...].astype(BF16), preferred_element_type=F32) + b_ref[...]
    return pl.pallas_call(
        body, name="adaln_fwd", grid=(nl,),
        in_specs=[pl.BlockSpec((16, dm), lambda l: (0, 0)), pl.BlockSpec((None, dm, ns), lambda l: (l, 0, 0)),
                  pl.BlockSpec((None, 1, ns), lambda l: (l, 0, 0))],
        out_specs=pl.BlockSpec((None, 16, ns), lambda l: (l, 0, 0)),
        out_shape=jax.ShapeDtypeStruct((nl, 16, ns), F32), compiler_params=_params(48))(cond, w_mod, b_mod)


def adaln_bwd(cond, dmod, w_mod):
    nl, dm, ns = w_mod.shape

    def body(c_ref, d_ref, w_ref, gw_ref, ds_ref):
        l = pl.program_id(0)

        @pl.when(l == 0)
        def _():
            ds_ref[...] = jnp.zeros_like(ds_ref)
        cv = c_ref[...]
        s = (cv * _sigmoid(cv)).astype(BF16)
        dv = d_ref[...].astype(BF16)
        gw_ref[...] = lax.dot_general(s, dv, (((0,), (0,)), ((), ())), preferred_element_type=F32)
        ds_ref[...] += lax.dot_general(dv, w_ref[...].astype(BF16), (((1,), (1,)), ((), ())),
                                       preferred_element_type=F32)
    return pl.pallas_call(
        body, name="adaln_bwd", grid=(nl,),
        in_specs=[pl.BlockSpec((16, dm), lambda l: (0, 0)), pl.BlockSpec((None, 16, ns), lambda l: (l, 0, 0)),
                  pl.BlockSpec((None, dm, ns), lambda l: (l, 0, 0))],
        out_specs=[pl.BlockSpec((None, dm, ns), lambda l: (l, 0, 0)), pl.BlockSpec((16, dm), lambda l: (0, 0))],
        out_shape=[jax.ShapeDtypeStruct((nl, dm, ns), F32), jax.ShapeDtypeStruct((16, dm), F32)],
        compiler_params=_params(48))(cond, dmod, w_mod)


def _me():
    return lax.axis_index("x"), lax.axis_index("y"), lax.axis_index("c")


def allgather8(block):
    m_per, n = block.shape

    def body(x_ref, out_ref, send_sems, recv_sems, local_sem):
        x, y, c = _me()
        me, sibling = (x, y, c), (x, y, 1 - c)
        chips = [(1 - x, y), (x, 1 - y), (1 - x, 1 - y)]

        def rows(px, py, pc):
            return out_ref.at[pl.ds((4 * px + 2 * py + pc) * m_per, m_per), :]

        def copy(k, blk, to, src=None):
            return pltpu.make_async_remote_copy(
                src_ref=rows(*blk) if src is None else src, dst_ref=rows(*blk),
                send_sem=send_sems.at[k], recv_sem=recv_sems.at[k], device_id=to, device_id_type=MESH)
        mine = pltpu.make_async_copy(x_ref, rows(*me), local_sem)
        mine.start()
        first = [copy(0, me, sibling, src=x_ref)]
        first += [copy(1 + j, me, (*chip, c), src=x_ref) for j, chip in enumerate(chips)]
        for cp in first:
            cp.start()
        passed = [copy(4 + j, (*chip, c), sibling) for j, chip in enumerate(chips)]
        for j, chip in enumerate(chips):
            copy(1 + j, (*chip, c), me).wait_recv()
            passed[j].start()
        copy(0, sibling, me).wait_recv()
        for j, chip in enumerate(chips):
            copy(4 + j, (*chip, 1 - c), me).wait_recv()
        for cp in first + passed:
            cp.wait_send()
        mine.wait()
    return pl.pallas_call(
        body, name="allgather8",
        out_shape=jax.ShapeDtypeStruct((N_DEV * m_per, n), block.dtype),
        in_specs=[pl.BlockSpec(memory_space=pltpu.VMEM)],
        out_specs=pl.BlockSpec(memory_space=pltpu.VMEM),
        scratch_shapes=[pltpu.SemaphoreType.DMA((7,)), pltpu.SemaphoreType.DMA((7,)), pltpu.SemaphoreType.DMA],
        compiler_params=_params(48))(block)


def _other_chips(x, y):
    return [(1 - x, y), (x, 1 - y), (1 - x, 1 - y)]


_HBM = pl.BlockSpec(memory_space=pltpu.HBM)
_SEM = pl.BlockSpec(memory_space=pltpu.SEMAPHORE)
_ANY = pl.BlockSpec(memory_space=pl.ANY)
_EFFECT = pltpu.SideEffectType.DATAFLOW_SIDE_EFFECTING


def _in_hbm(v):
    return pltpu.with_memory_space_constraint(v, pltpu.HBM)


def cast_into_slot(w, layer, chip_id):
    _, kb, nb = w.shape
    tr = _row_tile(kb)

    def body(chip_ref, w_ref, o_ref):
        del chip_ref
        o_ref[...] = w_ref[...].astype(BF16)
    return pl.pallas_call(
        body, name="cast_into_slot",
        grid_spec=pltpu.PrefetchScalarGridSpec(
            num_scalar_prefetch=1, grid=(kb // tr,),
            in_specs=[pl.BlockSpec((None, tr, nb), lambda i, chip: (layer, i, 0))],
            out_specs=pl.BlockSpec((None, tr, nb), lambda i, chip: (chip[0], i, 0))),
        out_shape=jax.ShapeDtypeStruct((N_CHIP, kb, nb), BF16))(chip_id, w)


def _split_copies(mode, srcs, lands, send_sems, recv_sems):
    x, y, c = _me()
    out = []
    for t in range(len(lands)):
        for k, chip in enumerate(_other_chips(x, y)):
            if mode == "gather":
                src = dst = lands[t].at[2 * x + y]
                landed = lands[t].at[2 * chip[0] + chip[1]]
            else:
                src, dst, landed = srcs[t].at[2 * chip[0] + chip[1]], lands[t].at[k], lands[t].at[k]
            send = pltpu.make_async_remote_copy(src_ref=src, dst_ref=dst, send_sem=send_sems.at[3 * t + k],
                                                recv_sem=recv_sems.at[3 * t + k], device_id=(*chip, c),
                                                device_id_type=MESH)
            recv = pltpu.make_async_remote_copy(src_ref=src, dst_ref=landed, send_sem=send_sems.at[3 * t + k],
                                                recv_sem=recv_sems.at[3 * t + k], device_id=(*chip, c),
                                                device_id_type=MESH)
            out.append((send, recv))
    return out


def exchange_start(name, mode, srcs, lands, after):
    ns, nl = len(srcs), len(lands)
    na = ns + nl

    def body(*refs):
        src_refs, land_refs = refs[:ns], refs[ns:na]
        send_sems, recv_sems = refs[na + 1], refs[na + 2]
        token = refs[-1]
        for send, _ in _split_copies(mode, src_refs, land_refs, send_sems, recv_sems):
            send.start()
        token[...] = jnp.zeros_like(token)
    arrays = list(srcs) + list(lands)
    outs = pl.pallas_call(
        body, name=name,
        out_shape=(pltpu.SemaphoreType.DMA((3 * nl,)), pltpu.SemaphoreType.DMA((3 * nl,)),
                   *[pltpu.HBM(v.shape, v.dtype) for v in arrays], jax.ShapeDtypeStruct((8, 128), F32)),
        in_specs=[_HBM] * na + [_ANY],
        out_specs=(_SEM, _SEM, *[_HBM] * na, pl.BlockSpec(memory_space=pltpu.VMEM)),
        input_output_aliases={i: 2 + i for i in range(na)},
        compiler_params=pltpu.CompilerParams(has_side_effects=_EFFECT))(*[_in_hbm(v) for v in arrays], after)
    return outs[0], outs[1], list(outs[2:2 + ns]), list(outs[2 + ns:2 + na]), outs[-1]


def exchange_wait(name, mode, send_sems, recv_sems, srcs, lands, after):
    ns, nl = len(srcs), len(lands)
    na = ns + nl

    def body(*refs):
        for _, recv in _split_copies(mode, refs[:ns], refs[ns:na], refs[na], refs[na + 1]):
            recv.wait_send()
            recv.wait_recv()
    arrays = list(srcs) + list(lands)
    outs = pl.pallas_call(
        body, name=name,
        out_shape=[pltpu.HBM(v.shape, v.dtype) for v in arrays],
        in_specs=[_HBM] * na + [_SEM, _SEM, _ANY], out_specs=[_HBM] * na,
        input_output_aliases={i: i for i in range(na)},
        compiler_params=pltpu.CompilerParams(has_side_effects=_EFFECT))(*arrays, send_sems, recv_sems, after)
    return list(outs[:ns]), list(outs[ns:])


def swap_with_sibling(vs):
    n = len(vs)

    def body(*refs):
        v_refs, out_refs, send_sems, recv_sems = refs[:n], refs[n:2 * n], refs[2 * n], refs[2 * n + 1]
        x, y, c = _me()
        cps = [pltpu.make_async_remote_copy(src_ref=v_refs[t], dst_ref=out_refs[t], send_sem=send_sems.at[t],
                                            recv_sem=recv_sems.at[t], device_id=(x, y, 1 - c), device_id_type=MESH)
               for t in range(n)]
        for cp in cps:
            cp.start()
        for cp in cps:
            cp.wait()
    return pl.pallas_call(
        body, name="swap_with_sibling", out_shape=[jax.ShapeDtypeStruct(v.shape, v.dtype) for v in vs],
        in_specs=[_ANY] * n, out_specs=[_ANY] * n,
        scratch_shapes=[pltpu.SemaphoreType.DMA((n,)), pltpu.SemaphoreType.DMA((n,))])(*vs)


def sum_slots(parts):
    n, rows, w = parts.shape
    tr = _row_tile(rows)

    def body(p_ref, o_ref):
        acc = p_ref[0].astype(F32)
        for k in range(1, n):
            acc = acc + p_ref[k].astype(F32)
        o_ref[...] = acc
    return pl.pallas_call(
        body, name="sum_slots", grid=(rows // tr,),
        in_specs=[pl.BlockSpec((n, tr, w), lambda i: (0, i, 0))], out_specs=pl.BlockSpec((tr, w), lambda i: (i, 0)),
        out_shape=jax.ShapeDtypeStruct((rows, w), F32), compiler_params=_params(48))(parts)


def sum_landed(landed, own, chip_id, layer, n_layers, buf):
    n, rows, w = landed.shape
    tr = _row_tile(rows)
    base = layer * (rows // tr)

    def compute(l_ref, g_ref, o_ref):
        acc = g_ref[...].astype(F32)
        for k in range(n):
            acc = acc + l_ref[k].astype(F32)
        o_ref[...] = acc
    in_specs = [pl.BlockSpec((n, tr, w), lambda i, chip: (0, i, 0)),
                pl.BlockSpec((None, tr, w), lambda i, chip: (chip[0], i, 0))]
    out_spec = pl.BlockSpec((tr, w), lambda i, chip: (base + i, 0))
    out_shape = jax.ShapeDtypeStruct((n_layers * rows, w), F32)
    if buf is None:
        def body(chip_ref, l_ref, g_ref, o_ref):
            del chip_ref
            compute(l_ref, g_ref, o_ref)
        return pl.pallas_call(
            body, name="sum_landed",
            grid_spec=pltpu.PrefetchScalarGridSpec(num_scalar_prefetch=1, grid=(rows // tr,), in_specs=in_specs,
                                                   out_specs=out_spec),
            out_shape=out_shape, compiler_params=_params(48))(chip_id, landed, own)

    def body(chip_ref, l_ref, g_ref, buf_ref, o_ref):
        del chip_ref, buf_ref
        compute(l_ref, g_ref, o_ref)
    return pl.pallas_call(
        body, name="sum_landed_into",
        grid_spec=pltpu.PrefetchScalarGridSpec(num_scalar_prefetch=1, grid=(rows // tr,),
                                               in_specs=in_specs + [_ANY], out_specs=out_spec),
        out_shape=out_shape, input_output_aliases={3: 0}, compiler_params=_params(48))(chip_id, landed, own, buf)


def adamw(w, ga, gb, m, v):
    rows, wd = w.shape
    tr = min(_row_tile(rows), 128)
    c1 = 1.0 / (1.0 - ADAM_B1 ** ADAM_STEP)
    c2 = 1.0 / (1.0 - ADAM_B2 ** ADAM_STEP)

    def update(wv, g, mv, vv, g_ref, d_ref, m_ref, v_ref):
        mn = ADAM_B1 * mv + (1.0 - ADAM_B1) * g
        vn = ADAM_B2 * vv + (1.0 - ADAM_B2) * (g * g)
        g_ref[...] = g
        m_ref[...] = mn
        v_ref[...] = vn
        d_ref[...] = -ADAM_LR * ((mn * c1) / (jnp.sqrt(vn * c2) + ADAM_EPS) + ADAM_WD * wv)
    tile = pl.BlockSpec((tr, wd), lambda i: (i, 0))
    out = jax.ShapeDtypeStruct((rows, wd), F32)
    if gb is None:
        def body(w_ref, ga_ref, m_ref, v_ref, g_out, d_out, m_out, v_out):
            update(w_ref[...], ga_ref[...], m_ref[...], v_ref[...], g_out, d_out, m_out, v_out)
        return pl.pallas_call(body, name="adamw", grid=(rows // tr,), in_specs=[tile] * 4,
                              out_specs=[tile] * 4, out_shape=[out] * 4)(w, ga, m, v)

    def body(w_ref, ga_ref, gb_ref, m_ref, v_ref, g_out, d_out, m_out, v_out):
        update(w_ref[...], ga_ref[...] + gb_ref[...], m_ref[...], v_ref[...], g_out, d_out, m_out, v_out)
    return pl.pallas_call(body, name="adamw_sum", grid=(rows // tr,), in_specs=[tile] * 5,
                          out_specs=[tile] * 4, out_shape=[out] * 4)(w, ga, gb, m, v)


def _rope_tables(T, R):
    rows = T // GRID_W
    row = jnp.repeat(jnp.arange(rows), GRID_W).astype(F32)
    col = jnp.tile(jnp.arange(GRID_W), rows).astype(F32)
    n_freq = HEAD_DIM // 4
    inv_freq = ROPE_THETA ** (-jnp.arange(n_freq, dtype=F32) / n_freq)
    ang = jnp.concatenate([row[:, None] * inv_freq, col[:, None] * inv_freq], axis=-1)
    cos, sin = jnp.cos(ang), jnp.sin(ang)
    cs = jnp.tile(cos, (1, 4))
    sn = jnp.tile(jnp.concatenate([-sin, sin], axis=-1), (1, 2))
    pad = R - T
    return (jnp.concatenate([cs, jnp.ones((pad, 128), F32)], axis=0),
            jnp.concatenate([sn, jnp.zeros((pad, 128), F32)], axis=0))


def _pack(parts, mult=8 * 128):
    flat = jnp.concatenate([p.reshape(-1).astype(F32) for p in parts])
    pad = (-flat.shape[0]) % mult
    return jnp.pad(flat, (0, pad)).reshape(-1, 128)


def _unpack(buf, shapes):
    flat = buf.reshape(-1)
    out, o = [], 0
    for s in shapes:
        n = 1
        for d in s:
            n *= d
        out.append(flat[o:o + n].reshape(s))
        o += n
    return out


def kernel(x, c, ctx, c_ctx, w_mod, b_mod, norm_mix, norm_ffn, w_in_ab, conv_a, conv_b, conv_b_bias, ln_b_gain, ln_b_bias, w_out_ab, w_qkv, w_o, sinks, w_up, w_conv_ffn, w_down, final_norm, loss_target, m_c_ctx, m_w_mod, m_b_mod, m_norm_mix, m_norm_ffn, m_w_in_ab, m_conv_a, m_conv_b, m_conv_b_bias, m_ln_b_gain, m_ln_b_bias, m_w_out_ab, m_w_qkv, m_w_o, m_sinks, m_w_up, m_w_conv_ffn, m_w_down, m_final_norm, v_c_ctx, v_w_mod, v_b_mod, v_norm_mix, v_norm_ffn, v_w_in_ab, v_conv_a, v_conv_b, v_conv_b_bias, v_ln_b_gain, v_ln_b_bias, v_w_out_ab, v_w_qkv, v_w_o, v_sinks, v_w_up, v_w_conv_ffn, v_w_down, v_final_norm):
    T, dm = x.shape[1], x.shape[2]
    tc = ctx.shape[1]
    R = T + tc
    depth = w_mod.shape[0]
    ax, ay, ac = lax.axis_index("x"), lax.axis_index("y"), lax.axis_index("c")
    chip = 2 * ax + ay
    dev = 4 * ax + 2 * ay + ac

    small_w = [conv_a, conv_b, w_conv_ffn]
    gathered = allgather8(_pack([c] + small_w)).reshape(N_DEV, -1)
    cond8 = gathered[:, :dm]
    off = dm
    full_small = []
    for wsh in small_w:
        n = wsh.size
        per_chip = gathered[0::2, off:off + n].reshape((N_CHIP,) + wsh.shape)
        full_small.append(jnp.concatenate([per_chip[q] for q in range(N_CHIP)], axis=-1))
        off += n
    conv_a_f, conv_b_f, w_conv_ffn_f = full_small
    cond = jnp.concatenate([cond8, c_ctx[None, :], jnp.zeros((7, dm), F32)], axis=0)

    ns_mod = w_mod.shape[2]
    b_mod_sh = lax.dynamic_slice_in_dim(b_mod, chip * ns_mod, ns_mod, axis=1)[:, None, :]
    mod_sh = adaln_fwd(cond, w_mod, b_mod_sh)
    mod_all = allgather8(mod_sh.reshape(depth * 16, ns_mod)).reshape(N_DEV, depth, 16, ns_mod)
    mod_full = jnp.concatenate([mod_all[2 * q] for q in range(N_CHIP)], axis=-1)
    mine = lax.dynamic_index_in_dim(mod_full, dev, axis=1, keepdims=False)
    mods = jnp.stack([mine, mod_full[:, 8]], axis=1).reshape(depth, 2, 6, dm)

    masters = {"w_in_ab": w_in_ab, "w_out_ab": w_out_ab, "w_qkv": w_qkv, "w_o": w_o, "w_up": w_up, "w_down": w_down}
    chip_id = chip.astype(jnp.int32).reshape(1)

    def half_weights(l, half):
        if half == 1:
            return [("w_up", l), ("w_down", l)]
        return [("w_in_ab", l // 2), ("w_out_ab", l // 2)] if l % 2 == 0 else [("w_qkv", l // 2), ("w_o", l // 2)]
    in_flight, after = {}, mods
    for l in range(depth):
        for half in range(2):
            lands = [cast_into_slot(masters[n], j, chip_id) for n, j in half_weights(l, half)]
            send_sems, recv_sems, _, lands, after = exchange_start(f"gather_start_{l}_{half}", "gather", [], lands, after)
            in_flight[l, half] = (send_sems, recv_sems, lands)
    mods = mods + after[0, 0]

    def gathered_weights(l, half, after):
        send_sems, recv_sems, lands = in_flight[l, half]
        _, landed = exchange_wait(f"gather_wait_{l}_{half}", "gather", send_sems, recv_sems, [], lands, after)
        return dict(zip([n for n, _ in half_weights(l, half)], landed))

    cs, sn = _rope_tables(T, R)
    bias = window_bias(T, R)
    sinks_flat = sinks.reshape(-1)

    xs = jnp.concatenate([x[0], ctx[0]], axis=0)
    saved, W = [], []
    h1 = norm_mod_fwd(xs, norm_mix[0][None], mods[0], 0, T)
    for l in range(depth):
        e = l // 2
        wl = gathered_weights(l, 0, h1)
        W.append(wl)
        s = {"x0": xs, "h1": h1}
        if l % 2 == 0:
            p = mm_nn(h1, wl["w_in_ab"], BF16)
            yab, y1, x1, h2 = mixer_fwd(p, conv_a_f[e], conv_b_f[e], conv_b_bias[e][None], ln_b_gain[e][None],
                                        ln_b_bias[e][None], wl["w_out_ab"], xs, norm_ffn[l][None], mods[l], 2, 3, T)
            s.update(p=p, mix=yab)
        else:
            qkvr = mm_qkv_rope(h1, wl["w_qkv"], cs, sn)
            att, probs = attn_fwd(qkvr, sinks_flat[e * N_HEADS:(e + 1) * N_HEADS], bias, T)
            s.update(qkvr=qkvr, mix=att, probs=probs)
            y1, x1, h2 = mm_resid_norm_fwd(att, wl["w_o"], xs, norm_ffn[l][None], mods[l], mods[l], 2, 3, T)
        wl.update(gathered_weights(l, 1, h2))
        u = mm_nn(h2, wl["w_up"], BF16)
        if l + 1 < depth:
            z, y2, xs, h1 = ffn_mid_fwd(u, w_conv_ffn_f[l], wl["w_down"], x1, norm_mix[l + 1][None], mods[l],
                                        mods[l + 1], 5, 0, T)
        else:
            z, y2, xs = ffn_mid_fwd(u, w_conv_ffn_f[l], wl["w_down"], x1, None, mods[l], None, 5, 0, T)
        s.update(y1=y1, x1=x1, h2=h2, u=u, z=z, y2=y2)
        saved.append(s)

    loss_part, dx, d_final, dy2, dg2_last = loss_head(xs, final_norm[None], loss_target[0], saved[depth - 1]["y2"],
                                                      mods[depth - 1], 5, T)
    loss = lax.psum(loss_part[0, 0], ("x", "y", "c"))

    d_mods, d_norm_mix, d_norm_ffn = [None] * depth, [None] * depth, [None] * depth
    d_conv_a, d_conv_b, d_vecs, d_sinks, d_wc = [None] * 2, [None] * 2, [None] * 2, [None] * 2, [None] * depth
    dss1, dss2, dg1, dg2 = [None] * depth, [None] * depth, [None] * depth, [None] * depth
    scattering = {}

    def scatter(l, half, G, after):
        grads_h = [G[n] for n, _ in half_weights(l, half)]
        lands = [lax.empty((N_CHIP - 1, *g.shape[1:]), g.dtype) for g in grads_h]
        send_sems, recv_sems, grads_h, lands, token = exchange_start(
            f"scatter_start_{l}_{half}", "scatter", grads_h, lands, after)
        scattering[l, half] = (send_sems, recv_sems, grads_h, lands)
        return token

    dg2[depth - 1] = dg2_last
    pending = 0.0
    for l in reversed(range(depth)):
        e = l // 2
        s, wl = saved[l], W[l]
        G = {}
        G["w_down"] = mm_tn(s["z"], dy2, "row", wl["w_down"])
        duc, d_wc[l] = ffn_mid_bwd(mm_nt(dy2, wl["w_down"], BF16), s["u"], w_conv_ffn_f[l] + pending, T)
        du, dx, dy1, dss2[l], d_norm_ffn[l], dg1[l] = ffn_up_bwd(
            duc, w_conv_ffn_f[l], wl["w_up"], s["x1"], norm_ffn[l][None], mods[l], dx, s["y1"], mods[l], 3, 2, T)
        G["w_up"] = mm_tn(s["h2"], du, "col", wl["w_up"])
        started = scatter(l, 1, G, du)[0, 0]
        if l % 2 == 0:
            G["w_out_ab"] = mm_tn(s["mix"], dy1, "row", wl["w_out_ab"])
            dyab = mm_nt(dy1, wl["w_out_ab"], F32)
            dmid, d_conv_a[e], d_conv_b[e], d_vecs[e] = convmix_bwd1(
                dyab, s["p"], conv_a_f[e] + started, conv_b_f[e], conv_b_bias[e][None], ln_b_gain[e][None],
                ln_b_bias[e][None], T)
            if l > 0:
                dp, dx, dy2, dss1[l], d_norm_mix[l], dg2[l - 1] = mixer_in_bwd(
                    dmid, s["p"], conv_a_f[e], conv_b_f[e], wl["w_in_ab"], s["x0"], norm_mix[l][None], mods[l], dx,
                    saved[l - 1]["y2"], mods[l - 1], 0, 5, T)
            else:
                dp, dx, dss1[l], d_norm_mix[l] = mixer_in_bwd(
                    dmid, s["p"], conv_a_f[e], conv_b_f[e], wl["w_in_ab"], s["x0"], norm_mix[l][None], mods[l], dx,
                    None, None, 0, 0, T)
            G["w_in_ab"] = mm_tn(s["h1"], dp, "col", wl["w_in_ab"])
        else:
            G["w_o"] = mm_tn(s["mix"], dy1, "row", wl["w_o"])
            datt = mm_nt(dy1, wl["w_o"], BF16)
            dq, dks, dvs, dkc, dvc, d_sinks[e] = attn_bwd(s["qkvr"], datt, s["probs"], T)
            dqkv, dx, dy2, dss1[l], d_norm_mix[l], dg2[l - 1] = attn_in_bwd(
                dq, dks, dvs, dkc, dvc, cs + started, sn, wl["w_qkv"], s["x0"], norm_mix[l][None], mods[l], dx,
                saved[l - 1]["y2"], mods[l - 1], 0, 5, T)
            G["w_qkv"] = mm_tn(s["h1"], dqkv, "col", wl["w_qkv"])
        token = scatter(l, 0, G, dx)
        pending = token[0, 0]
    grad_x = dx[:T][None]
    for l in range(depth):
        a1, a2 = dss1[l].sum(2), dss2[l].sum(2)
        d_mods[l] = jnp.stack([a1[:, 0], a1[:, 1], dg1[l].sum(1), a2[:, 0], a2[:, 1], dg2[l].sum(1)], axis=1)

    d_mods = jnp.stack(d_mods)
    summed_parts = [
        d_mods[:, 1],
        jnp.stack(d_norm_mix).sum(1), jnp.stack(d_norm_ffn).sum(1),
        jnp.stack(d_conv_a).sum(2), jnp.stack(d_conv_b).sum(2),
        jnp.stack(d_vecs).sum(2),
        jnp.stack(d_sinks)[:, 0, :N_HEADS],
        jnp.stack(d_wc).sum(2), d_final.sum(0) + pending]
    summed_shapes = [p.shape for p in summed_parts]
    n_own = depth * 6 * dm
    pack = _pack([d_mods[:, 0]] + summed_parts)
    parts = allgather8(pack).reshape(N_DEV, -1, 128)
    total = sum_slots(parts)
    own_rows = parts.reshape(N_DEV, -1)[:, :n_own].reshape(N_DEV, depth, 6 * dm)
    (dmod_ctx, g_norm_mix, g_norm_ffn, g_conv_a, g_conv_b, g_vecs, g_sinks, g_wc, g_final) = _unpack(
        total.reshape(-1)[n_own:], summed_shapes)
    dmod_rows = jnp.concatenate([jnp.moveaxis(own_rows, 0, 1), dmod_ctx.reshape(depth, 1, 6 * dm),
                                 jnp.zeros((depth, 7, 6 * dm), F32)], axis=1)
    g_b_mod = dmod_rows.sum(1)
    dmod_sh = lax.dynamic_slice_in_dim(dmod_rows, chip * ns_mod, ns_mod, axis=2)
    g_w_mod, dsilu = adaln_bwd(cond, dmod_sh, w_mod)
    dsilu_all = allgather8(dsilu[8:16]).reshape(N_DEV, 8, dm)
    dsilu_ctx = sum_slots(dsilu_all[0::2])[0]
    sg = jax.nn.sigmoid(c_ctx)
    g_c_ctx = dsilu_ctx * (sg * (1.0 + c_ctx * (1.0 - sg)))

    def shard_cols(full, width):
        return lax.dynamic_slice_in_dim(full, chip * width, width, axis=full.ndim - 1)
    g_conv_a_s = shard_cols(g_conv_a, conv_a.shape[-1])
    g_conv_b_s = shard_cols(g_conv_b, conv_b.shape[-1])
    g_wc_s = shard_cols(g_wc, w_conv_ffn.shape[-1])

    grads, deltas, new_m, new_v = {}, {}, {}, {}

    def step_2d(name, wv, ga, gb, mv, vv):
        shp = wv.shape
        r2 = lambda t: t.reshape(-1, shp[-1])
        g, d, mn, vn = adamw(r2(wv), r2(ga), None if gb is None else r2(gb), r2(mv), r2(vv))
        grads[name], deltas[name], new_m[name], new_v[name] = (t.reshape(shp) for t in (g, d, mn, vn))

    step_2d("w_mod", w_mod, g_w_mod, None, m_w_mod, v_w_mod)
    sums = {n: None for n in masters}
    for l in reversed(range(depth)):
        for half in (1, 0):
            send_sems, recv_sems, grads_h, lands = scattering[l, half]
            grads_h, landed = exchange_wait(f"scatter_wait_{l}_{half}", "scatter", send_sems, recv_sems, grads_h,
                                            lands, deltas["w_mod"])
            for (n, j), own, arr in zip(half_weights(l, half), grads_h, landed):
                sums[n] = sum_landed(arr, own, chip_id, j, masters[n].shape[0], sums[n])
    moments = {"w_in_ab": (m_w_in_ab, v_w_in_ab), "w_out_ab": (m_w_out_ab, v_w_out_ab),
               "w_qkv": (m_w_qkv, v_w_qkv), "w_o": (m_w_o, v_w_o), "w_up": (m_w_up, v_w_up),
               "w_down": (m_w_down, v_w_down)}
    others = swap_with_sibling([sums[name] for name in masters])
    for (name, wv), other in zip(masters.items(), others):
        step_2d(name, wv, sums[name].reshape(wv.shape), other.reshape(wv.shape), *moments[name])

    small = [("c_ctx", c_ctx, g_c_ctx, m_c_ctx, v_c_ctx), ("b_mod", b_mod, g_b_mod, m_b_mod, v_b_mod),
             ("norm_mix", norm_mix, g_norm_mix, m_norm_mix, v_norm_mix),
             ("norm_ffn", norm_ffn, g_norm_ffn, m_norm_ffn, v_norm_ffn),
             ("conv_a", conv_a, g_conv_a_s, m_conv_a, v_conv_a), ("conv_b", conv_b, g_conv_b_s, m_conv_b, v_conv_b),
             ("conv_b_bias", conv_b_bias, g_vecs[:, 0], m_conv_b_bias, v_conv_b_bias),
             ("ln_b_gain", ln_b_gain, g_vecs[:, 1], m_ln_b_gain, v_ln_b_gain),
             ("ln_b_bias", ln_b_bias, g_vecs[:, 2], m_ln_b_bias, v_ln_b_bias),
             ("sinks", sinks, g_sinks, m_sinks, v_sinks),
             ("w_conv_ffn", w_conv_ffn, g_wc_s, m_w_conv_ffn, v_w_conv_ffn),
             ("final_norm", final_norm, g_final, m_final_norm, v_final_norm)]
    shapes = [t[1].shape for t in small]
    packed = [_pack([t[k] for t in small]) for k in (1, 2, 3, 4)]
    n_real = sum(t[1].size for t in small)
    lane_id = jnp.arange(packed[3].size).reshape(packed[3].shape)
    packed[3] = jnp.where(lane_id < n_real, packed[3], 1.0)
    outs = adamw(packed[0], packed[1], None, packed[2], packed[3])
    for (name, *_), g, d, mn, vn in zip(small, *[_unpack(o, shapes) for o in outs]):
        grads[name], deltas[name], new_m[name], new_v[name] = g, d, mn, vn

    order = ["c_ctx", "w_mod", "b_mod", "norm_mix", "norm_ffn", "w_in_ab", "conv_a", "conv_b", "conv_b_bias",
             "ln_b_gain", "ln_b_bias", "w_out_ab", "w_qkv", "w_o", "sinks", "w_up", "w_conv_ffn", "w_down",
             "final_norm"]
    return (loss, grad_x, *[grads[n] for n in order], *[deltas[n] for n in order],
            *[new_m[n] for n in order], *[new_v[n] for n in order])
```

```python
import jax
import jax.numpy as jnp
from jax import lax
from jax.experimental import pallas as pl
from jax.experimental.pallas import tpu as pltpu

F32 = jnp.float32
BF16 = jnp.bfloat16
MESH = pl.DeviceIdType.MESH

EPS = 1e-6
NEG_INF = -1e30
GRID_W = 64
HEAD_DIM = 64
N_HEADS = 16
WINDOW = 128
QB = 128
ROPE_THETA = 10000.0
A_W = 512
B_CONV = 31
D_FF = 2816
ADAM_LR, ADAM_B1, ADAM_B2, ADAM_EPS, ADAM_WD, ADAM_STEP = 0.001, 0.9, 0.999, 1e-8, 0.01, 10

TMR = 256
HALO = 16
N_DEV = 8
N_CHIP = 4


def _params(vmem_mb=None):
    if vmem_mb is None:
        return pltpu.CompilerParams()
    return pltpu.CompilerParams(vmem_limit_bytes=vmem_mb * 1024 * 1024)


def _row_tile(rows, cap=768):
    for t in (2816, 1408, 768, 704, 512, 384, 256, 128, 64, 32, 16, 8):
        if t <= cap and rows % t == 0:
            return t
    raise ValueError(rows)


def _colsum8(v):
    r, c = v.shape
    return v.reshape(r // 8, 8, c).sum(axis=0)


def _sigmoid(v):
    return 0.5 * jnp.tanh(0.5 * v) + 0.5


def mm_nn(a, w, out_dtype):
    R = a.shape[0]
    _, kb, nb = w.shape
    tm = _row_tile(R)

    def body(a_ref, w_ref, o_ref):
        av = a_ref[...].astype(BF16)
        for q in range(N_CHIP):
            o_ref[:, q * nb:(q + 1) * nb] = jnp.dot(av, w_ref[q], preferred_element_type=F32).astype(o_ref.dtype)
    return pl.pallas_call(
        body, name="mm_nn_col", grid=(R // tm,),
        in_specs=[pl.BlockSpec((tm, kb), lambda i: (i, 0)),
                  pl.BlockSpec((N_CHIP, kb, nb), lambda i: (0, 0, 0), pipeline_mode=pl.Buffered(1))],
        out_specs=pl.BlockSpec((tm, N_CHIP * nb), lambda i: (i, 0)),
        out_shape=jax.ShapeDtypeStruct((R, N_CHIP * nb), out_dtype),
        compiler_params=_params(48))(a, w)


def mm_nt(d, w, out_dtype):
    R = d.shape[0]
    _, kb, nb = w.shape
    tm = _row_tile(R)
    contract_last = (((1,), (1,)), ((), ()))
    resident = pl.BlockSpec((N_CHIP, kb, nb), lambda i: (0, 0, 0), pipeline_mode=pl.Buffered(1))

    def body(d_ref, w_ref, o_ref):
        wv = w_ref[...].reshape(N_CHIP * kb, nb)
        o_ref[...] = lax.dot_general(d_ref[...].astype(BF16), wv, contract_last,
                                     preferred_element_type=F32).astype(o_ref.dtype)
    return pl.pallas_call(
        body, name="mm_nt_row", grid=(R // tm,),
        in_specs=[pl.BlockSpec((tm, nb), lambda i: (i, 0)), resident],
        out_specs=pl.BlockSpec((tm, N_CHIP * kb), lambda i: (i, 0)),
        out_shape=jax.ShapeDtypeStruct((R, N_CHIP * kb), out_dtype),
        compiler_params=_params(48))(d, w)


def mm_tn(a, d, kind, like):
    R = a.shape[0]
    _, kb, nb = like.shape
    tm = _row_tile(R, 1408 if kind == "col" else 768)
    nsteps = R // tm
    contract_rows = (((0,), (0,)), ((), ()))
    out_shape = jax.ShapeDtypeStruct(like.shape, BF16)

    def accumulate(a_ref, d_ref, acc_ref):
        @pl.when(pl.program_id(1) == 0)
        def _():
            acc_ref[...] = jnp.zeros_like(acc_ref)
        acc_ref[...] += lax.dot_general(a_ref[...].astype(BF16), d_ref[...].astype(BF16), contract_rows,
                                        preferred_element_type=F32)
    if kind == "col":
        def body(a_ref, d_ref, o_ref, acc_ref):
            accumulate(a_ref, d_ref, acc_ref)

            @pl.when(pl.program_id(1) == nsteps - 1)
            def _():
                o_ref[...] = acc_ref[...].astype(BF16)
        return pl.pallas_call(
            body, name="mm_tn_col", grid=(N_CHIP, nsteps),
            in_specs=[pl.BlockSpec((tm, kb), lambda q, i: (i, 0)), pl.BlockSpec((tm, nb), lambda q, i: (i, q))],
            out_specs=pl.BlockSpec((None, kb, nb), lambda q, i: (q, 0, 0)), out_shape=out_shape,
            scratch_shapes=[pltpu.VMEM((kb, nb), F32)], compiler_params=_params(48))(a, d)
    tn = 512

    def body(a_ref, d_ref, o_ref, acc_ref):
        accumulate(a_ref, d_ref, acc_ref)

        @pl.when(pl.program_id(1) == nsteps - 1)
        def _():
            o_ref[...] = acc_ref[...].astype(BF16).reshape(N_CHIP, kb, tn)
    return pl.pallas_call(
        body, name="mm_tn_row", grid=(nb // tn, nsteps),
        in_specs=[pl.BlockSpec((tm, N_CHIP * kb), lambda n, i: (i, 0)), pl.BlockSpec((tm, tn), lambda n, i: (i, n))],
        out_specs=pl.BlockSpec((N_CHIP, kb, tn), lambda n, i: (0, 0, n)), out_shape=out_shape,
        scratch_shapes=[pltpu.VMEM((N_CHIP * kb, tn), F32)], compiler_params=_params(48))(a, d)


def _seg(i, T):
    return (i >= T // TMR).astype(jnp.int32)


def norm_mod_fwd(x, nw, mod, k, T):
    R, dm = x.shape

    def body(x_ref, nw_ref, mod_ref, h_ref):
        seg = _seg(pl.program_id(0), T)
        sh = mod_ref[seg, pl.ds(k, 1), :]
        sc = mod_ref[seg, pl.ds(k + 1, 1), :]
        xv = x_ref[...]
        r = lax.rsqrt(jnp.mean(xv * xv, axis=-1, keepdims=True) + EPS)
        h_ref[...] = ((xv * r * nw_ref[...]) * (1.0 + sc) + sh).astype(BF16)
    return pl.pallas_call(
        body, name="norm_mod_fwd", grid=(R // TMR,),
        in_specs=[pl.BlockSpec((TMR, dm), lambda i: (i, 0)),
                  pl.BlockSpec((1, dm), lambda i: (0, 0)),
                  pl.BlockSpec((2, 6, dm), lambda i: (0, 0, 0))],
        out_specs=pl.BlockSpec((TMR, dm), lambda i: (i, 0)),
        out_shape=jax.ShapeDtypeStruct((R, dm), BF16))(x, nw, mod)


def mm_resid_norm_fwd(a, w, x, nw, mod_g, mod_n, kg, kn, T):
    R, dm = x.shape
    _, kb, nb = w.shape

    def body(a_ref, w_ref, x_ref, nw_ref, mg_ref, mn_ref, y_ref, xo_ref, h_ref):
        seg = _seg(pl.program_id(0), T)
        yv = jnp.dot(a_ref[...].astype(BF16), w_ref[...].reshape(N_CHIP * kb, nb), preferred_element_type=F32)
        y_ref[...] = yv.astype(BF16)
        xv = x_ref[...] + mg_ref[seg, pl.ds(kg, 1), :] * yv
        xo_ref[...] = xv
        r = lax.rsqrt(jnp.mean(xv * xv, axis=-1, keepdims=True) + EPS)
        h_ref[...] = ((xv * r * nw_ref[...]) * (1.0 + mn_ref[seg, pl.ds(kn + 1, 1), :])
                      + mn_ref[seg, pl.ds(kn, 1), :]).astype(BF16)
    tile = pl.BlockSpec((TMR, dm), lambda i: (i, 0))
    modspec = pl.BlockSpec((2, 6, dm), lambda i: (0, 0, 0))
    return pl.pallas_call(
        body, name="mm_resid_norm_fwd", grid=(R // TMR,),
        in_specs=[pl.BlockSpec((TMR, N_CHIP * kb), lambda i: (i, 0)),
                  pl.BlockSpec((N_CHIP, kb, nb), lambda i: (0, 0, 0), pipeline_mode=pl.Buffered(1)),
                  tile, pl.BlockSpec((1, dm), lambda i: (0, 0)), modspec, modspec],
        out_specs=[tile, tile, tile],
        out_shape=[jax.ShapeDtypeStruct((R, dm), BF16), jax.ShapeDtypeStruct((R, dm), F32),
                   jax.ShapeDtypeStruct((R, dm), BF16)],
        compiler_params=_params(48))(a, w, x, nw, mod_g, mod_n)


def _halo_specs(width, R):
    nblk = R // HALO
    per = TMR // HALO
    return (pl.BlockSpec((HALO, width), lambda i: (jnp.maximum(i * per - 1, 0), 0)),
            pl.BlockSpec((TMR, width), lambda i: (i, 0)),
            pl.BlockSpec((HALO, width), lambda i: (jnp.minimum((i + 1) * per, nblk - 1), 0)))


def _halo_live(i, T, R):
    nl = T // TMR
    return (i != 0) & (i != nl), (i != nl - 1) & (i != R // TMR - 1)


def _ext(refs, c0, cw, live, halo=HALO):
    pref, ref, nref = refs
    before = jnp.where(live[0], pref[:, c0:c0 + cw].astype(F32)[HALO - halo:], 0.0)
    after = jnp.where(live[1], nref[:, c0:c0 + cw].astype(F32)[:halo], 0.0)
    return jnp.concatenate([before, ref[:, c0:c0 + cw].astype(F32), after], axis=0)


def _at(ext, off, halo=HALO):
    n = ext.shape[0]
    s = (-off) % n
    y = pltpu.roll(ext, s, 0) if s else ext
    return y[halo:halo + TMR]


def ffn_mid_fwd(u, wc, w_down, x, nw, mod_g, mod_n, kg, kn, T):
    R, w2 = u.shape
    dm = x.shape[1]
    _, kb, nb = w_down.shape
    cw = 256
    with_norm = nw is not None

    def body(*refs):
        if with_norm:
            up_ref, u_ref, un_ref, wc_ref, w_ref, x_ref, nw_ref, mg_ref, mn_ref, z_ref, y_ref, xo_ref, h_ref = refs
        else:
            up_ref, u_ref, un_ref, wc_ref, w_ref, x_ref, mg_ref, z_ref, y_ref, xo_ref = refs
        i = pl.program_id(0)
        seg = _seg(i, T)
        live = _halo_live(i, T, R)

        def conv(c0):
            e = _ext((up_ref, u_ref, un_ref), c0, cw, live, 8)
            return (wc_ref[pl.ds(0, 1), c0:c0 + cw] * _at(e, -1, 8) + wc_ref[pl.ds(1, 1), c0:c0 + cw] * _at(e, 0, 8)
                    + wc_ref[pl.ds(2, 1), c0:c0 + cw] * _at(e, 1, 8))
        yv = None
        for j in range(D_FF // cw):
            a = conv(j * cw)
            g = conv(D_FF + j * cw)
            zc = (g * _sigmoid(g) * a).astype(BF16)
            z_ref[:, j * cw:(j + 1) * cw] = zc
            t = jnp.dot(zc, w_ref[j * cw:(j + 1) * cw, :], preferred_element_type=F32)
            yv = t if yv is None else yv + t
        y_ref[...] = yv.astype(BF16)
        xv = x_ref[...] + mg_ref[seg, pl.ds(kg, 1), :] * yv
        xo_ref[...] = xv
        if with_norm:
            r = lax.rsqrt(jnp.mean(xv * xv, axis=-1, keepdims=True) + EPS)
            h_ref[...] = ((xv * r * nw_ref[...]) * (1.0 + mn_ref[seg, pl.ds(kn + 1, 1), :])
                          + mn_ref[seg, pl.ds(kn, 1), :]).astype(BF16)
    tile = pl.BlockSpec((TMR, dm), lambda i: (i, 0))
    modspec = pl.BlockSpec((2, 6, dm), lambda i: (0, 0, 0))
    w_down = w_down.reshape(N_CHIP * kb, nb)
    in_specs = [*_halo_specs(w2, R), pl.BlockSpec((3, w2), lambda i: (0, 0)),
                pl.BlockSpec((N_CHIP * kb, nb), lambda i: (0, 0), pipeline_mode=pl.Buffered(1)), tile]
    out_specs = [pl.BlockSpec((TMR, D_FF), lambda i: (i, 0)), tile, tile]
    out_shape = [jax.ShapeDtypeStruct((R, D_FF), BF16), jax.ShapeDtypeStruct((R, dm), BF16),
                 jax.ShapeDtypeStruct((R, dm), F32)]
    if with_norm:
        return pl.pallas_call(
            body, name="ffn_mid_fwd", grid=(R // TMR,),
            in_specs=in_specs + [pl.BlockSpec((1, dm), lambda i: (0, 0)), modspec, modspec],
            out_specs=out_specs + [tile], out_shape=out_shape + [jax.ShapeDtypeStruct((R, dm), BF16)],
            compiler_params=_params(48))(u, u, u, wc, w_down, x, nw, mod_g, mod_n)
    return pl.pallas_call(
        body, name="ffn_mid_fwd_last", grid=(R // TMR,), in_specs=in_specs + [modspec],
        out_specs=out_specs, out_shape=out_shape, compiler_params=_params(48))(u, u, u, wc, w_down, x, mod_g)


def ffn_mid_bwd(dz, u, wc, T):
    R, w2 = u.shape
    cw = 256

    def body(dz_ref, up_ref, u_ref, un_ref, wc_ref, duc_ref, dwc_ref):
        i = pl.program_id(0)
        live = _halo_live(i, T, R)

        @pl.when(i == 0)
        def _():
            dwc_ref[...] = jnp.zeros_like(dwc_ref)

        def taps(c0):
            e = _ext((up_ref, u_ref, un_ref), c0, cw, live, 8)
            return [_at(e, -1, 8), _at(e, 0, 8), _at(e, 1, 8)]

        def conv(t, c0):
            return (wc_ref[pl.ds(0, 1), c0:c0 + cw] * t[0] + wc_ref[pl.ds(1, 1), c0:c0 + cw] * t[1]
                    + wc_ref[pl.ds(2, 1), c0:c0 + cw] * t[2])
        for j in range(D_FF // cw):
            ca, cg = j * cw, D_FF + j * cw
            dzv = dz_ref[:, ca:ca + cw].astype(F32)
            ta, tg = taps(ca), taps(cg)
            a, g = conv(ta, ca), conv(tg, cg)
            sg = _sigmoid(g)
            da = dzv * (g * sg)
            dg = dzv * a * (sg * (1.0 + g * (1.0 - sg)))
            duc_ref[:, ca:ca + cw] = da.astype(BF16)
            duc_ref[:, cg:cg + cw] = dg.astype(BF16)
            for k in range(3):
                dwc_ref[k, :, ca:ca + cw] += _colsum8(da * ta[k])
                dwc_ref[k, :, cg:cg + cw] += _colsum8(dg * tg[k])
    return pl.pallas_call(
        body, name="ffn_mid_bwd", grid=(R // TMR,),
        in_specs=[pl.BlockSpec((TMR, D_FF), lambda i: (i, 0)), *_halo_specs(w2, R),
                  pl.BlockSpec((3, w2), lambda i: (0, 0))],
        out_specs=[pl.BlockSpec((TMR, w2), lambda i: (i, 0)), pl.BlockSpec((3, 8, w2), lambda i: (0, 0, 0))],
        out_shape=[jax.ShapeDtypeStruct((R, w2), BF16), jax.ShapeDtypeStruct((3, 8, w2), F32)],
        compiler_params=_params(48))(dz, u, u, u, wc)


def ffn_up_bwd(duc, wc, w_up, x, nw, mod_n, dxr, y, mod_g, kn, kg, T):
    R, w2 = duc.shape
    dm = x.shape[1]
    _, kb, nb = w_up.shape
    cw = 128
    contract_last = (((1,), (1,)), ((), ()))

    def body(dp_ref, d_ref, dn_ref, wc_ref, w_ref, x_ref, nw_ref, mn_ref, dxr_ref, y_ref, mg_ref,
             du_ref, dx_ref, dy_ref, dmod_ref, dnw_ref, dg_ref):
        i = pl.program_id(0)
        seg = _seg(i, T)
        live = _halo_live(i, T, R)

        @pl.when(i == 0)
        def _():
            dmod_ref[...] = jnp.zeros_like(dmod_ref)
            dnw_ref[...] = jnp.zeros_like(dnw_ref)
            dg_ref[...] = jnp.zeros_like(dg_ref)
        dhv = None
        for q in range(N_CHIP):
            for j in range(nb // cw):
                c0 = q * nb + j * cw
                e = _ext((dp_ref, d_ref, dn_ref), c0, cw, live, 8)
                du_ref[:, c0:c0 + cw] = (wc_ref[pl.ds(0, 1), c0:c0 + cw] * _at(e, 1, 8)
                                         + wc_ref[pl.ds(1, 1), c0:c0 + cw] * _at(e, 0, 8)
                                         + wc_ref[pl.ds(2, 1), c0:c0 + cw] * _at(e, -1, 8)).astype(BF16)
            t = lax.dot_general(du_ref[:, q * nb:(q + 1) * nb], w_ref[q], contract_last,
                                preferred_element_type=F32)
            dhv = t if dhv is None else dhv + t
        sc = mn_ref[seg, pl.ds(kn + 1, 1), :]
        nwv = nw_ref[...]
        xv = x_ref[...]
        r = lax.rsqrt(jnp.mean(xv * xv, axis=-1, keepdims=True) + EPS)
        xh = xv * r
        dmod_ref[seg, 0] += _colsum8(dhv)
        dmod_ref[seg, 1] += _colsum8(dhv * (xh * nwv))
        dn = dhv * (1.0 + sc)
        dnw_ref[...] += _colsum8(dn * xh)
        dxh = dn * nwv
        dx = dxr_ref[...] + r * (dxh - xh * jnp.mean(dxh * xh, axis=-1, keepdims=True))
        dx_ref[...] = dx
        dy_ref[...] = (mg_ref[seg, pl.ds(kg, 1), :] * dx).astype(BF16)
        dg_ref[seg] += _colsum8(dx * y_ref[...])
    tile = pl.BlockSpec((TMR, dm), lambda i: (i, 0))
    modspec = pl.BlockSpec((2, 6, dm), lambda i: (0, 0, 0))
    return pl.pallas_call(
        body, name="ffn_up_bwd", grid=(R // TMR,),
        in_specs=[*_halo_specs(w2, R), pl.BlockSpec((3, w2), lambda i: (0, 0)),
                  pl.BlockSpec((N_CHIP, kb, nb), lambda i: (0, 0, 0), pipeline_mode=pl.Buffered(1)),
                  tile, pl.BlockSpec((1, dm), lambda i: (0, 0)), modspec, tile, tile, modspec],
        out_specs=[pl.BlockSpec((TMR, w2), lambda i: (i, 0)), tile, tile,
                   pl.BlockSpec((2, 2, 8, dm), lambda i: (0, 0, 0, 0)), pl.BlockSpec((8, dm), lambda i: (0, 0)),
                   pl.BlockSpec((2, 8, dm), lambda i: (0, 0, 0))],
        out_shape=[jax.ShapeDtypeStruct((R, w2), BF16), jax.ShapeDtypeStruct((R, dm), F32),
                   jax.ShapeDtypeStruct((R, dm), BF16), jax.ShapeDtypeStruct((2, 2, 8, dm), F32),
                   jax.ShapeDtypeStruct((8, dm), F32), jax.ShapeDtypeStruct((2, 8, dm), F32)],
        compiler_params=_params(48))(duc, duc, duc, wc, w_up, x, nw, mod_n, dxr, y, mod_g)


_CW = 128


def _mixer_a(prefs, wa_ref, live):
    cin = _ext(prefs, A_W, A_W, live) * _ext(prefs, 2 * A_W, A_W, live)
    ca = (wa_ref[pl.ds(0, 1), :] * _at(cin, -1) + wa_ref[pl.ds(1, 1), :] * _at(cin, 0)
          + wa_ref[pl.ds(2, 1), :] * _at(cin, 1))
    return cin, ca


def _mixer_b(prefs, wb_ref, bias_ref, live, ub_s, ub2_s):
    for cc in range(A_W // _CW):
        c0 = cc * _CW
        ub = _ext(prefs, 3 * A_W + c0, _CW, live) * _sigmoid(_ext(prefs, 4 * A_W + c0, _CW, live))
        ub_s[:, c0:c0 + _CW] = ub
        acc = jnp.zeros((TMR, _CW), F32) + bias_ref[:, c0:c0 + _CW]
        for k in range(B_CONV):
            acc = acc + wb_ref[pl.ds(k, 1), c0:c0 + _CW] * _at(ub, k - B_CONV // 2)
        ub2_s[:, c0:c0 + _CW] = acc


def _layernorm_stats(v):
    mu = jnp.mean(v, axis=-1, keepdims=True)
    xc = v - mu
    rs = lax.rsqrt(jnp.mean(xc * xc, axis=-1, keepdims=True) + EPS)
    return xc * rs, rs


def mixer_fwd(p, wa, wb, bias, lng, lnb, w_out, x, nw, mod, kg, kn, T):
    R, wp = p.shape
    dm = x.shape[1]
    _, kb, nb = w_out.shape

    def body(pp_ref, p_ref, pn_ref, wa_ref, wb_ref, bias_ref, lng_ref, lnb_ref, w_ref, x_ref, nw_ref, mod_ref,
             o_ref, y_ref, xo_ref, h_ref, ub_s, ub2_s):
        i = pl.program_id(0)
        seg = _seg(i, T)
        live = _halo_live(i, T, R)
        prefs = (pp_ref, p_ref, pn_ref)
        _, ca = _mixer_a(prefs, wa_ref, live)
        ya = (p_ref[:, 0:A_W].astype(F32) * ca).astype(BF16)
        o_ref[:, 0:A_W] = ya
        yv = jnp.dot(ya, w_ref[0:A_W, :], preferred_element_type=F32)
        _mixer_b(prefs, wb_ref, bias_ref, live, ub_s, ub2_s)
        xh, _ = _layernorm_stats(ub2_s[...])
        lv = xh * lng_ref[...] + lnb_ref[...]
        yb = (lv * _sigmoid(lv)).astype(BF16)
        o_ref[:, A_W:2 * A_W] = yb
        yv = yv + jnp.dot(yb, w_ref[A_W:2 * A_W, :], preferred_element_type=F32)
        y_ref[...] = yv.astype(BF16)
        xv = x_ref[...] + mod_ref[seg, pl.ds(kg, 1), :] * yv
        xo_ref[...] = xv
        r = lax.rsqrt(jnp.mean(xv * xv, axis=-1, keepdims=True) + EPS)
        h_ref[...] = ((xv * r * nw_ref[...]) * (1.0 + mod_ref[seg, pl.ds(kn + 1, 1), :])
                      + mod_ref[seg, pl.ds(kn, 1), :]).astype(BF16)
    vec = pl.BlockSpec((1, A_W), lambda i: (0, 0))
    tile = pl.BlockSpec((TMR, dm), lambda i: (i, 0))
    return pl.pallas_call(
        body, name="mixer_fwd", grid=(R // TMR,),
        in_specs=[*_halo_specs(wp, R), pl.BlockSpec((3, A_W), lambda i: (0, 0)),
                  pl.BlockSpec((B_CONV, A_W), lambda i: (0, 0)), vec, vec, vec,
                  pl.BlockSpec((N_CHIP * kb, nb), lambda i: (0, 0), pipeline_mode=pl.Buffered(1)),
                  tile, pl.BlockSpec((1, dm), lambda i: (0, 0)), pl.BlockSpec((2, 6, dm), lambda i: (0, 0, 0))],
        out_specs=[pl.BlockSpec((TMR, 2 * A_W), lambda i: (i, 0)), tile, tile, tile],
        out_shape=[jax.ShapeDtypeStruct((R, 2 * A_W), BF16), jax.ShapeDtypeStruct((R, dm), BF16),
                   jax.ShapeDtypeStruct((R, dm), F32), jax.ShapeDtypeStruct((R, dm), BF16)],
        scratch_shapes=[pltpu.VMEM((TMR + 2 * HALO, A_W), F32), pltpu.VMEM((TMR, A_W), F32)],
        compiler_params=_params(48))(p, p, p, wa, wb, bias, lng, lnb, w_out.reshape(N_CHIP * kb, nb), x, nw, mod)


def convmix_bwd1(dyab, p, wa, wb, bias, lng, lnb, T):
    R, wp = p.shape

    def body(dy_ref, pp_ref, p_ref, pn_ref, wa_ref, wb_ref, bias_ref, lng_ref, lnb_ref,
             dmid_ref, dwa_ref, dwb_ref, dvec_ref, ub_s, ub2_s):
        i = pl.program_id(0)
        live = _halo_live(i, T, R)

        @pl.when(i == 0)
        def _():
            dwa_ref[...] = jnp.zeros_like(dwa_ref)
            dwb_ref[...] = jnp.zeros_like(dwb_ref)
            dvec_ref[...] = jnp.zeros_like(dvec_ref)
        prefs = (pp_ref, p_ref, pn_ref)
        cin, ca = _mixer_a(prefs, wa_ref, live)
        dya = dy_ref[:, 0:A_W]
        dmid_ref[:, 0:A_W] = dya * ca
        dca = dya * p_ref[:, 0:A_W].astype(F32)
        dmid_ref[:, A_W:2 * A_W] = dca
        for k in range(3):
            dwa_ref[k] += _colsum8(dca * _at(cin, k - 1))
        _mixer_b(prefs, wb_ref, bias_ref, live, ub_s, ub2_s)
        xh, rs = _layernorm_stats(ub2_s[...])
        gain = lng_ref[...]
        lv = xh * gain + lnb_ref[...]
        sl = _sigmoid(lv)
        dl = dy_ref[:, A_W:2 * A_W] * (sl * (1.0 + lv * (1.0 - sl)))
        dvec_ref[1] += _colsum8(dl * xh)
        dvec_ref[2] += _colsum8(dl)
        dxh = dl * gain
        dub2 = rs * (dxh - jnp.mean(dxh, axis=-1, keepdims=True)
                     - xh * jnp.mean(dxh * xh, axis=-1, keepdims=True))
        dvec_ref[0] += _colsum8(dub2)
        dmid_ref[:, 2 * A_W:3 * A_W] = dub2
        for cc in range(A_W // _CW):
            c0 = cc * _CW
            ub = ub_s[:, c0:c0 + _CW]
            d = dmid_ref[:, 2 * A_W + c0:2 * A_W + c0 + _CW]
            for k in range(B_CONV):
                dwb_ref[k, :, c0:c0 + _CW] += _colsum8(d * _at(ub, k - B_CONV // 2))
    vec = pl.BlockSpec((1, A_W), lambda i: (0, 0))
    return pl.pallas_call(
        body, name="convmix_bwd1", grid=(R // TMR,),
        in_specs=[pl.BlockSpec((TMR, 2 * A_W), lambda i: (i, 0)), *_halo_specs(wp, R),
                  pl.BlockSpec((3, A_W), lambda i: (0, 0)), pl.BlockSpec((B_CONV, A_W), lambda i: (0, 0)),
                  vec, vec, vec],
        out_specs=[pl.BlockSpec((TMR, 3 * A_W), lambda i: (i, 0)),
                   pl.BlockSpec((3, 8, A_W), lambda i: (0, 0, 0)),
                   pl.BlockSpec((B_CONV, 8, A_W), lambda i: (0, 0, 0)),
                   pl.BlockSpec((3, 8, A_W), lambda i: (0, 0, 0))],
        out_shape=[jax.ShapeDtypeStruct((R, 3 * A_W), F32), jax.ShapeDtypeStruct((3, 8, A_W), F32),
                   jax.ShapeDtypeStruct((B_CONV, 8, A_W), F32), jax.ShapeDtypeStruct((3, 8, A_W), F32)],
        scratch_shapes=[pltpu.VMEM((TMR + 2 * HALO, A_W), F32), pltpu.VMEM((TMR, A_W), F32)],
        compiler_params=_params(48))(dyab, p, p, p, wa, wb, bias, lng, lnb)


def mixer_in_bwd(dmid, p, wa, wb, w_in, x, nw, mod_n, dxr, y, mod_g, kn, kg, T):
    R, wp = p.shape
    dm = x.shape[1]
    _, kb, nb = w_in.shape
    with_resid = y is not None
    contract_last = (((1,), (1,)), ((), ()))

    def body(*refs):
        if with_resid:
            (mp_ref, m_ref, mn_ref, p_ref, wa_ref, wb_ref, w_ref, x_ref, nw_ref, mnorm_ref, dxr_ref, y_ref, mg_ref,
             dp_ref, dx_ref, dy_ref, dmod_ref, dnw_ref, dg_ref) = refs
        else:
            (mp_ref, m_ref, mn_ref, p_ref, wa_ref, wb_ref, w_ref, x_ref, nw_ref, mnorm_ref, dxr_ref,
             dp_ref, dx_ref, dmod_ref, dnw_ref) = refs
        i = pl.program_id(0)
        seg = _seg(i, T)
        live = _halo_live(i, T, R)

        @pl.when(i == 0)
        def _():
            dmod_ref[...] = jnp.zeros_like(dmod_ref)
            dnw_ref[...] = jnp.zeros_like(dnw_ref)
            if with_resid:
                dg_ref[...] = jnp.zeros_like(dg_ref)

        def block(q):
            return lax.dot_general(dp_ref[:, q * nb:(q + 1) * nb], w_ref[q], contract_last,
                                   preferred_element_type=F32)
        mrefs = (mp_ref, m_ref, mn_ref)
        dp_ref[:, 0:A_W] = m_ref[:, 0:A_W].astype(BF16)
        dca = _ext(mrefs, A_W, A_W, live)
        dcin = (wa_ref[pl.ds(0, 1), :] * _at(dca, 1) + wa_ref[pl.ds(1, 1), :] * _at(dca, 0)
                + wa_ref[pl.ds(2, 1), :] * _at(dca, -1))
        dp_ref[:, A_W:2 * A_W] = (dcin * p_ref[:, 2 * A_W:3 * A_W].astype(F32)).astype(BF16)
        dp_ref[:, 2 * A_W:3 * A_W] = (dcin * p_ref[:, A_W:2 * A_W].astype(F32)).astype(BF16)
        dhv = block(0) + block(1)
        for cc in range(A_W // _CW):
            c0 = cc * _CW
            d = _ext(mrefs, 2 * A_W + c0, _CW, live)
            dub = jnp.zeros((TMR, _CW), F32)
            for k in range(B_CONV):
                dub = dub + wb_ref[pl.ds(k, 1), c0:c0 + _CW] * _at(d, B_CONV // 2 - k)
            vb = p_ref[:, 3 * A_W + c0:3 * A_W + c0 + _CW].astype(F32)
            s = _sigmoid(p_ref[:, 4 * A_W + c0:4 * A_W + c0 + _CW].astype(F32))
            dp_ref[:, 3 * A_W + c0:3 * A_W + c0 + _CW] = (dub * s).astype(BF16)
            dp_ref[:, 4 * A_W + c0:4 * A_W + c0 + _CW] = (dub * vb * s * (1.0 - s)).astype(BF16)
        dhv = dhv + block(2) + block(3)
        sc = mnorm_ref[seg, pl.ds(kn + 1, 1), :]
        nwv = nw_ref[...]
        xv = x_ref[...]
        r = lax.rsqrt(jnp.mean(xv * xv, axis=-1, keepdims=True) + EPS)
        xh = xv * r
        dmod_ref[seg, 0] += _colsum8(dhv)
        dmod_ref[seg, 1] += _colsum8(dhv * (xh * nwv))
        dn = dhv * (1.0 + sc)
        dnw_ref[...] += _colsum8(dn * xh)
        dxh = dn * nwv
        dx = dxr_ref[...] + r * (dxh - xh * jnp.mean(dxh * xh, axis=-1, keepdims=True))
        dx_ref[...] = dx
        if with_resid:
            dy_ref[...] = (mg_ref[seg, pl.ds(kg, 1), :] * dx).astype(BF16)
            dg_ref[seg] += _colsum8(dx * y_ref[...])
    assert 2 * nb <= 3 * A_W and N_CHIP * nb == wp
    tile = pl.BlockSpec((TMR, dm), lambda i: (i, 0))
    modspec = pl.BlockSpec((2, 6, dm), lambda i: (0, 0, 0))
    in_specs = [*_halo_specs(3 * A_W, R), pl.BlockSpec((TMR, wp), lambda i: (i, 0)),
                pl.BlockSpec((3, A_W), lambda i: (0, 0)), pl.BlockSpec((B_CONV, A_W), lambda i: (0, 0)),
                pl.BlockSpec((N_CHIP, kb, nb), lambda i: (0, 0, 0), pipeline_mode=pl.Buffered(1)),
                tile, pl.BlockSpec((1, dm), lambda i: (0, 0)), modspec, tile]
    dp_spec = pl.BlockSpec((TMR, wp), lambda i: (i, 0))
    acc_specs = [pl.BlockSpec((2, 2, 8, dm), lambda i: (0, 0, 0, 0)), pl.BlockSpec((8, dm), lambda i: (0, 0))]
    acc_shapes = [jax.ShapeDtypeStruct((2, 2, 8, dm), F32), jax.ShapeDtypeStruct((8, dm), F32)]
    dp_shape, dx_shape = jax.ShapeDtypeStruct((R, wp), BF16), jax.ShapeDtypeStruct((R, dm), F32)
    if with_resid:
        return pl.pallas_call(
            body, name="mixer_in_bwd", grid=(R // TMR,), in_specs=in_specs + [tile, modspec],
            out_specs=[dp_spec, tile, tile] + acc_specs + [pl.BlockSpec((2, 8, dm), lambda i: (0, 0, 0))],
            out_shape=[dp_shape, dx_shape, jax.ShapeDtypeStruct((R, dm), BF16)] + acc_shapes
            + [jax.ShapeDtypeStruct((2, 8, dm), F32)],
            compiler_params=_params(48))(dmid, dmid, dmid, p, wa, wb, w_in, x, nw, mod_n, dxr, y, mod_g)
    return pl.pallas_call(
        body, name="mixer_in_bwd_first", grid=(R // TMR,), in_specs=in_specs,
        out_specs=[dp_spec, tile] + acc_specs, out_shape=[dp_shape, dx_shape] + acc_shapes,
        compiler_params=_params(48))(dmid, dmid, dmid, p, wa, wb, w_in, x, nw, mod_n, dxr)


def _rot_half(v):
    w = v.shape[-1]
    lane = lax.broadcasted_iota(jnp.int32, (1, w), 1)
    return jnp.where(lane % HEAD_DIM < HEAD_DIM // 2, pltpu.roll(v, w - HEAD_DIM // 2, 1),
                     pltpu.roll(v, HEAD_DIM // 2, 1))


def mm_qkv_rope(a, w, cs, sn):
    R = a.shape[0]
    _, kb, nb = w.shape
    wq = N_CHIP * nb
    tm = _row_tile(R)
    qw = N_HEADS * HEAD_DIM
    kw = (wq - qw) // 2
    scale = HEAD_DIM ** -0.5

    def body(a_ref, w_ref, cs_ref, sn_ref, o_ref, x_ref):
        av = a_ref[...].astype(BF16)
        for q in range(N_CHIP):
            x_ref[:, q * nb:(q + 1) * nb] = jnp.dot(av, w_ref[q], preferred_element_type=F32)
        c, s = cs_ref[...], sn_ref[...]
        q = x_ref[:, 0:qw]
        o_ref[:, 0:qw] = ((q * jnp.tile(c, (1, qw // 128)) + _rot_half(q) * jnp.tile(s, (1, qw // 128)))
                          * scale).astype(BF16)
        k = x_ref[:, qw:qw + kw]
        o_ref[:, qw:qw + kw] = (k * jnp.tile(c, (1, kw // 128))
                                + _rot_half(k) * jnp.tile(s, (1, kw // 128))).astype(BF16)
        o_ref[:, qw + kw:] = x_ref[:, qw + kw:].astype(BF16)
    tab = pl.BlockSpec((tm, 128), lambda i: (i, 0))
    return pl.pallas_call(
        body, name="mm_qkv_rope", grid=(R // tm,),
        in_specs=[pl.BlockSpec((tm, kb), lambda i: (i, 0)),
                  pl.BlockSpec((N_CHIP, kb, nb), lambda i: (0, 0, 0), pipeline_mode=pl.Buffered(1)), tab, tab],
        out_specs=pl.BlockSpec((tm, wq), lambda i: (i, 0)),
        out_shape=jax.ShapeDtypeStruct((R, wq), BF16), scratch_shapes=[pltpu.VMEM((tm, wq), F32)],
        compiler_params=_params(48))(a, w, cs, sn)


def attn_in_bwd(dq, dks, dvs, dkc, dvc, cs, sn, w, x, nw, mod_n, dxr, y, mod_g, kn, kg, T):
    R, qw = dq.shape
    kw = dkc.shape[1]
    dm = x.shape[1]
    _, kb, nbw = w.shape
    nb = R // QB
    nl = T // QB
    scale = HEAD_DIM ** -0.5
    contract_last = (((1,), (1,)), ((), ()))

    def body(dq_ref, kp0, kp1, ko_ref, kn0, kn1, vp0, vp1, vo_ref, vn0, vn1, kc_ref, vc_ref, cs_ref, sn_ref,
             w_ref, x_ref, nw_ref, mnorm_ref, dxr_ref, y_ref, mg_ref,
             o_ref, dx_ref, dy_ref, dmod_ref, dnw_ref, dg_ref):
        kp_ref, kn_ref, vp_ref, vn_ref = (kp0, kp1), (kn0, kn1), (vp0, vp1), (vn0, vn1)
        i = pl.program_id(0)
        seg = _seg(i, T)

        @pl.when(i == 0)
        def _():
            dmod_ref[...] = jnp.zeros_like(dmod_ref)
            dnw_ref[...] = jnp.zeros_like(dnw_ref)
            dg_ref[...] = jnp.zeros_like(dg_ref)
        c, s = cs_ref[...], sn_ref[...]
        has_next = (2 * i + 2 < nb).astype(F32)
        has_prev = (i >= 1).astype(F32)
        is_ctx = (i >= T // TMR).astype(F32)

        def band(own, from_next, from_prev, ctx_sum):
            return (own[...] + jnp.concatenate([from_next[0][...], from_next[1][...] * has_next], axis=0)
                    + jnp.concatenate([from_prev[0][...] * has_prev, from_prev[1][...]], axis=0)
                    + ctx_sum[...] * is_ctx)
        g = dq_ref[...] * scale
        o_ref[:, 0:qw] = (g * jnp.tile(c, (1, qw // 128)) + _rot_half(g * jnp.tile(s, (1, qw // 128)))).astype(BF16)
        g = band(ko_ref, kp_ref, kn_ref, kc_ref)
        o_ref[:, qw:qw + kw] = (g * jnp.tile(c, (1, kw // 128))
                                + _rot_half(g * jnp.tile(s, (1, kw // 128)))).astype(BF16)
        o_ref[:, qw + kw:] = band(vo_ref, vp_ref, vn_ref, vc_ref).astype(BF16)
        dhv = None
        for q in range(N_CHIP):
            t = lax.dot_general(o_ref[:, q * nbw:(q + 1) * nbw], w_ref[q], contract_last,
                                preferred_element_type=F32)
            dhv = t if dhv is None else dhv + t
        sc = mnorm_ref[seg, pl.ds(kn + 1, 1), :]
        nwv = nw_ref[...]
        xv = x_ref[...]
        r = lax.rsqrt(jnp.mean(xv * xv, axis=-1, keepdims=True) + EPS)
        xh = xv * r
        dmod_ref[seg, 0] += _colsum8(dhv)
        dmod_ref[seg, 1] += _colsum8(dhv * (xh * nwv))
        dn = dhv * (1.0 + sc)
        dnw_ref[...] += _colsum8(dn * xh)
        dxh = dn * nwv
        dx = dxr_ref[...] + r * (dxh - xh * jnp.mean(dxh * xh, axis=-1, keepdims=True))
        dx_ref[...] = dx
        dy_ref[...] = (mg_ref[seg, pl.ds(kg, 1), :] * dx).astype(BF16)
        dg_ref[seg] += _colsum8(dx * y_ref[...])
    own = pl.BlockSpec((TMR, kw), lambda i: (i, 0))
    from_next = [pl.BlockSpec((QB, kw), lambda i: (2 * i + 1, 0)),
                 pl.BlockSpec((QB, kw), lambda i: (jnp.minimum(2 * i + 2, nb - 1), 0))]
    from_prev = [pl.BlockSpec((QB, kw), lambda i: (jnp.maximum(2 * i - 1, 0), 0)),
                 pl.BlockSpec((QB, kw), lambda i: (2 * i, 0))]
    ctx = pl.BlockSpec((TMR, kw), lambda i: (jnp.maximum(i - T // TMR, 0), 0))
    tab = pl.BlockSpec((TMR, 128), lambda b: (b, 0))
    tile = pl.BlockSpec((TMR, dm), lambda b: (b, 0))
    modspec = pl.BlockSpec((2, 6, dm), lambda b: (0, 0, 0))
    return pl.pallas_call(
        body, name="attn_in_bwd", grid=(R // TMR,),
        in_specs=[pl.BlockSpec((TMR, qw), lambda b: (b, 0)), *from_next, own, *from_prev, *from_next, own, *from_prev,
                  ctx, ctx, tab, tab,
                  pl.BlockSpec((N_CHIP, kb, nbw), lambda b: (0, 0, 0), pipeline_mode=pl.Buffered(1)),
                  tile, pl.BlockSpec((1, dm), lambda b: (0, 0)), modspec, tile, tile, modspec],
        out_specs=[pl.BlockSpec((TMR, qw + 2 * kw), lambda b: (b, 0)), tile, tile,
                   pl.BlockSpec((2, 2, 8, dm), lambda b: (0, 0, 0, 0)), pl.BlockSpec((8, dm), lambda b: (0, 0)),
                   pl.BlockSpec((2, 8, dm), lambda b: (0, 0, 0))],
        out_shape=[jax.ShapeDtypeStruct((R, qw + 2 * kw), BF16), jax.ShapeDtypeStruct((R, dm), F32),
                   jax.ShapeDtypeStruct((R, dm), BF16), jax.ShapeDtypeStruct((2, 2, 8, dm), F32),
                   jax.ShapeDtypeStruct((8, dm), F32), jax.ShapeDtypeStruct((2, 8, dm), F32)],
        compiler_params=_params(48))(
            dq, dks[0], dks[0], dks[1], dks[2], dks[2], dvs[0], dvs[0], dvs[1], dvs[2], dvs[2], dkc, dvc, cs, sn,
            w, x, nw, mod_n, dxr, y, mod_g)


def _attn_specs(T, R):
    nl = T // QB
    kblk = N_HEADS * HEAD_DIM // 256

    def band(col, shift):
        return pl.BlockSpec((QB, 256), lambda b: (jnp.clip(b + shift, 0, nl - 1), col))

    def ctx(col):
        return pl.BlockSpec((R - T, 256), lambda b: (T // (R - T), col))
    q = pl.BlockSpec((QB, N_HEADS * HEAD_DIM), lambda b: (b, 0))
    return (q, [band(kblk, -1), band(kblk, 0), band(kblk, 1), ctx(kblk)],
            [band(kblk + 1, -1), band(kblk + 1, 0), band(kblk + 1, 1), ctx(kblk + 1)])


def _attn_common(T, R):
    nl = T // QB
    nk = 3 * QB + (R - T)

    def low_lanes():
        return lax.broadcasted_iota(jnp.int32, (1, 128), 1) < HEAD_DIM

    def dup(v, par):
        low = low_lanes()
        vf = v.astype(F32)
        r = pltpu.roll(vf, HEAD_DIM, 1)
        return (jnp.where(low, vf, r) if par == 0 else jnp.where(low, r, vf)).astype(BF16)

    def stack(ref, par, base):
        low = low_lanes()
        pa = ref[:, base + (2 * par) * 128:base + (2 * par + 1) * 128].astype(BF16)
        pb = ref[:, base + (2 * par + 1) * 128:base + (2 * par + 2) * 128].astype(BF16)
        zero = jnp.zeros_like(pa)
        return jnp.concatenate([jnp.where(low, pa, zero), jnp.where(low, zero, pa),
                                jnp.where(low, pb, zero), jnp.where(low, zero, pb)], axis=0)

    def unstack(v):
        low = low_lanes()
        return (jnp.where(low, v[0:QB], v[QB:2 * QB]), jnp.where(low, v[2 * QB:3 * QB], v[3 * QB:4 * QB]))

    def mask_of(b):
        col = lax.broadcasted_iota(jnp.int32, (1, nk), 1)
        gone = (((col < QB) & (b == 0)) | ((col >= 2 * QB) & (col < 3 * QB) & (b == nl - 1))
                | ((col < 3 * QB) & (b >= nl)))
        return jnp.where(gone, NEG_INF, 0.0)

    def sink_col(sink_ref, first):
        blk = lax.broadcasted_iota(jnp.int32, (4 * QB, 1), 0) // QB
        out = jnp.zeros((4 * QB, 1), F32) + sink_ref[first]
        for h in range(1, 4):
            out = jnp.where(blk == h, sink_ref[first + h], out)
        return out

    def scores(qs, kd, mask, sink):
        s = lax.dot_general(qs, kd, (((1,), (1,)), ((), ())), preferred_element_type=F32) + mask
        m = jnp.maximum(jnp.max(s, axis=-1, keepdims=True), sink)
        e = jnp.exp(s - m)
        es = jnp.exp(sink - m)
        return e, es, 1.0 / (jnp.sum(e, axis=-1, keepdims=True) + es)
    return low_lanes, dup, stack, unstack, mask_of, sink_col, scores


def window_bias(T, R):
    nk = 3 * QB + (R - T)
    row = jnp.arange(QB)[:, None]
    col = jnp.arange(nk)[None, :]
    near = (jnp.abs(col - QB - row) <= WINDOW) | (col >= 3 * QB)
    return jnp.tile(jnp.where(near, 0.0, NEG_INF).astype(F32), (4, 1))


def _probs_spec(nk):
    return pl.BlockSpec((2, None, 2, 4 * QB, nk + 128), lambda b: (0, b, 0, 0, 0))


def attn_fwd(qkvr, sinks, bias, T):
    R = qkvr.shape[0]
    nk = bias.shape[1]
    qspec, kspecs, vspecs = _attn_specs(T, R)
    _, dup, stack, unstack, mask_of, sink_col, scores = _attn_common(T, R)

    def body(q_ref, kp, ko, kn, kc, vp, vo, vn, vc, sink_ref, bias_ref, o_ref, p_ref):
        mask = bias_ref[...] + mask_of(pl.program_id(0))
        for jj in range(2):
            kv = slice(jj * 128, (jj + 1) * 128)
            k_all = jnp.concatenate([kp[:, kv], ko[:, kv], kn[:, kv], kc[:, kv]], axis=0)
            v_all = jnp.concatenate([vp[:, kv], vo[:, kv], vn[:, kv], vc[:, kv]], axis=0)
            for par in range(2):
                kd, vd = dup(k_all, par), dup(v_all, par)
                e, es, rz = scores(stack(q_ref, par, jj * 512), kd, mask, sink_col(sink_ref, jj * 8 + par * 4))
                p = (e * rz).astype(BF16)
                p_ref[jj, par, :, 0:nk] = p
                p_ref[jj, par, :, nk:nk + 128] = jnp.broadcast_to(es * rz, (4 * QB, 128)).astype(BF16)
                o = jnp.dot(p, vd, preferred_element_type=F32)
                pa, pb = unstack(o)
                c0 = jj * 512 + 2 * par * 128
                o_ref[:, c0:c0 + 128] = pa.astype(BF16)
                o_ref[:, c0 + 128:c0 + 256] = pb.astype(BF16)
    return pl.pallas_call(
        body, name="attn_fwd", grid=(R // QB,),
        in_specs=[qspec, *kspecs, *vspecs, pl.BlockSpec(memory_space=pltpu.SMEM),
                  pl.BlockSpec(bias.shape, lambda b: (0, 0))],
        out_specs=[pl.BlockSpec((QB, N_HEADS * HEAD_DIM), lambda b: (b, 0)), _probs_spec(nk)],
        out_shape=[jax.ShapeDtypeStruct((R, N_HEADS * HEAD_DIM), BF16),
                   jax.ShapeDtypeStruct((2, R // QB, 2, 4 * QB, nk + 128), BF16)],
        compiler_params=_params(48))(qkvr, *([qkvr] * 8), sinks, bias)


def attn_bwd(qkvr, do, probs, T):
    R = qkvr.shape[0]
    tc = R - T
    nk = probs.shape[-1] - 128
    qspec, kspecs, vspecs = _attn_specs(T, R)
    _, dup, stack, unstack, _, _, _ = _attn_common(T, R)
    contract_rows = (((0,), (0,)), ((), ()))
    contract_last = (((1,), (1,)), ((), ()))

    def body(q_ref, kp, ko, kn, kc, vp, vo, vn, vc, do_ref, p_ref,
             dq_ref, dkp, dko, dkn, dvp, dvo, dvn, dkc_ref, dvc_ref, dsink_ref):
        @pl.when(pl.program_id(0) == 0)
        def _():
            dsink_ref[...] = jnp.zeros_like(dsink_ref)
            dkc_ref[...] = jnp.zeros_like(dkc_ref)
            dvc_ref[...] = jnp.zeros_like(dvc_ref)
        lane = lax.broadcasted_iota(jnp.int32, (8, 128), 1)
        srow = lax.broadcasted_iota(jnp.int32, (8, 128), 0)
        low_rows = lax.broadcasted_iota(jnp.int32, (128, 1), 0) < HEAD_DIM
        for jj in range(2):
            kv = slice(jj * 128, (jj + 1) * 128)
            k_all = jnp.concatenate([kp[:, kv], ko[:, kv], kn[:, kv], kc[:, kv]], axis=0)
            v_all = jnp.concatenate([vp[:, kv], vo[:, kv], vn[:, kv], vc[:, kv]], axis=0)
            dk_fold, dv_fold = [], []
            for par in range(2):
                kd, vd = dup(k_all, par), dup(v_all, par)
                first = jj * 8 + par * 4
                qs, dos = stack(q_ref, par, jj * 512), stack(do_ref, par, jj * 512)
                p16 = p_ref[jj, par, :, 0:nk]
                p = p16.astype(F32)
                ps = jnp.max(p_ref[jj, par, :, nk:nk + 128].astype(F32), axis=-1, keepdims=True)
                dp = lax.dot_general(dos, vd, contract_last, preferred_element_type=F32)
                delta = jnp.sum(p * dp, axis=-1, keepdims=True)
                ds = (p * (dp - delta)).astype(BF16)
                t = ps * delta
                for h in range(4):
                    dsink = -jnp.sum(t[h * QB:(h + 1) * QB])
                    dsink_ref[...] += jnp.where((lane == first + h) & (srow == 0), dsink, 0.0)
                pa, pb = unstack(jnp.dot(ds, kd, preferred_element_type=F32))
                c0 = jj * 512 + 2 * par * 128
                dq_ref[:, c0:c0 + 128] = pa
                dq_ref[:, c0 + 128:c0 + 256] = pb
                dk_t = lax.dot_general(qs, ds, contract_rows, preferred_element_type=F32)
                dv_t = lax.dot_general(dos, p16, contract_rows, preferred_element_type=F32)
                dk_fold.append(dk_t + pltpu.roll(dk_t, HEAD_DIM, 0))
                dv_fold.append(dv_t + pltpu.roll(dv_t, HEAD_DIM, 0))
            dk = jnp.where(low_rows, dk_fold[0], dk_fold[1]).T
            dv = jnp.where(low_rows, dv_fold[0], dv_fold[1]).T
            dkp[:, kv], dko[:, kv], dkn[:, kv] = dk[0:QB], dk[QB:2 * QB], dk[2 * QB:3 * QB]
            dvp[:, kv], dvo[:, kv], dvn[:, kv] = dv[0:QB], dv[QB:2 * QB], dv[2 * QB:3 * QB]
            dkc_ref[:, kv] += dk[3 * QB:]
            dvc_ref[:, kv] += dv[3 * QB:]
    blk = pl.BlockSpec((QB, 256), lambda b: (b, 0))
    cblk = pl.BlockSpec((tc, 256), lambda b: (0, 0))
    part = jax.ShapeDtypeStruct((R, 256), F32)
    csum = jax.ShapeDtypeStruct((tc, 256), F32)
    full = pl.BlockSpec((QB, N_HEADS * HEAD_DIM), lambda b: (b, 0))
    outs = pl.pallas_call(
        body, name="attn_bwd", grid=(R // QB,),
        in_specs=[qspec, *kspecs, *vspecs, full, _probs_spec(nk)],
        out_specs=[full, blk, blk, blk, blk, blk, blk, cblk, cblk, pl.BlockSpec((8, 128), lambda b: (0, 0))],
        out_shape=[jax.ShapeDtypeStruct((R, N_HEADS * HEAD_DIM), F32), part, part, part, part, part, part,
                   csum, csum, jax.ShapeDtypeStruct((8, 128), F32)],
        compiler_params=_params(48))(qkvr, *([qkvr] * 8), do, probs)
    return outs[0], outs[1:4], outs[4:7], outs[7], outs[8], outs[9]


def loss_head(x, nw, target, y, mod, kg, T):
    R, dm = x.shape
    nl = T // TMR

    def body(x_ref, nw_ref, t_ref, y_ref, mod_ref, loss_ref, dx_ref, dnw_ref, dy_ref, dg_ref):
        i = pl.program_id(0)
        seg = _seg(i, T)

        @pl.when(i == 0)
        def _():
            loss_ref[...] = jnp.zeros_like(loss_ref)
            dnw_ref[...] = jnp.zeros_like(dnw_ref)
            dg_ref[...] = jnp.zeros_like(dg_ref)
        live = (i < nl).astype(F32)
        nwv = nw_ref[...]
        xv = x_ref[...]
        r = lax.rsqrt(jnp.mean(xv * xv, axis=-1, keepdims=True) + EPS)
        xh = xv * r
        err = xh * nwv - t_ref[...]
        per_row = jnp.mean(err * err, axis=-1, keepdims=True)
        loss_ref[...] += 0.5 * live * jnp.sum(per_row)
        dy = err * (live / dm)
        dnw_ref[...] += _colsum8(dy * xh)
        dxh = dy * nwv
        dx = r * (dxh - xh * jnp.mean(dxh * xh, axis=-1, keepdims=True))
        dx_ref[...] = dx
        dy_ref[...] = (mod_ref[seg, pl.ds(kg, 1), :] * dx).astype(BF16)
        dg_ref[seg] += _colsum8(dx * y_ref[...])
    tile = pl.BlockSpec((TMR, dm), lambda i: (i, 0))
    return pl.pallas_call(
        body, name="loss_head", grid=(R // TMR,),
        in_specs=[tile, pl.BlockSpec((1, dm), lambda i: (0, 0)),
                  pl.BlockSpec((TMR, dm), lambda i: (jnp.minimum(i, nl - 1), 0)), tile,
                  pl.BlockSpec((2, 6, dm), lambda i: (0, 0, 0))],
        out_specs=[pl.BlockSpec((8, 128), lambda i: (0, 0)), tile, pl.BlockSpec((8, dm), lambda i: (0, 0)), tile,
                   pl.BlockSpec((2, 8, dm), lambda i: (0, 0, 0))],
        out_shape=[jax.ShapeDtypeStruct((8, 128), F32), jax.ShapeDtypeStruct((R, dm), F32),
                   jax.ShapeDtypeStruct((8, dm), F32), jax.ShapeDtypeStruct((R, dm), BF16),
                   jax.ShapeDtypeStruct((2, 8, dm), F32)])(x, nw, target, y, mod)


def adaln_fwd(cond, w_mod, b_mod):
    nl, dm, ns = w_mod.shape

    def body(c_ref, w_ref, b_ref, o_ref):
        cv = c_ref[...]
        s = (cv * _sigmoid(cv)).astype(BF16)
        o_ref[...] = jnp.dot(s, w_ref[...].astype(BF16), preferred_element_type=F32) + b_ref[...]
    return pl.pallas_call(
        body, name="adaln_fwd", grid=(nl,),
        in_specs=[pl.BlockSpec((16, dm), lambda l: (0, 0)), pl.BlockSpec((None, dm, ns), lambda l: (l, 0, 0)),
                  pl.BlockSpec((None, 1, ns), lambda l: (l, 0, 0))],
        out_specs=pl.BlockSpec((None, 16, ns), lambda l: (l, 0, 0)),
        out_shape=jax.ShapeDtypeStruct((nl, 16, ns), F32), compiler_params=_params(48))(cond, w_mod, b_mod)


def adaln_bwd(cond, dmod, w_mod):
    nl, dm, ns = w_mod.shape

    def body(c_ref, d_ref, w_ref, gw_ref, ds_ref):
        l = pl.program_id(0)

        @pl.when(l == 0)
        def _():
            ds_ref[...] = jnp.zeros_like(ds_ref)
        cv = c_ref[...]
        s = (cv * _sigmoid(cv)).astype(BF16)
        dv = d_ref[...].astype(BF16)
        gw_ref[...] = lax.dot_general(s, dv, (((0,), (0,)), ((), ())), preferred_element_type=F32)
        ds_ref[...] += lax.dot_general(dv, w_ref[...].astype(BF16), (((1,), (1,)), ((), ())),
                                       preferred_element_type=F32)
    return pl.pallas_call(
        body, name="adaln_bwd", grid=(nl,),
        in_specs=[pl.BlockSpec((16, dm), lambda l: (0, 0)), pl.BlockSpec((None, 16, ns), lambda l: (l, 0, 0)),
                  pl.BlockSpec((None, dm, ns), lambda l: (l, 0, 0))],
        out_specs=[pl.BlockSpec((None, dm, ns), lambda l: (l, 0, 0)), pl.BlockSpec((16, dm), lambda l: (0, 0))],
        out_shape=[jax.ShapeDtypeStruct((nl, dm, ns), F32), jax.ShapeDtypeStruct((16, dm), F32)],
        compiler_params=_params(48))(cond, dmod, w_mod)


def _me():
    return lax.axis_index("x"), lax.axis_index("y"), lax.axis_index("c")


def allgather8(block):
    m_per, n = block.shape

    def body(x_ref, out_ref, send_sems, recv_sems, local_sem):
        x, y, c = _me()
        me, sibling = (x, y, c), (x, y, 1 - c)
        chips = [(1 - x, y), (x, 1 - y), (1 - x, 1 - y)]

        def rows(px, py, pc):
            return out_ref.at[pl.ds((4 * px + 2 * py + pc) * m_per, m_per), :]

        def copy(k, blk, to, src=None):
            return pltpu.make_async_remote_copy(
                src_ref=rows(*blk) if src is None else src, dst_ref=rows(*blk),
                send_sem=send_sems.at[k], recv_sem=recv_sems.at[k], device_id=to, device_id_type=MESH)
        mine = pltpu.make_async_copy(x_ref, rows(*me), local_sem)
        mine.start()
        first = [copy(0, me, sibling, src=x_ref)]
        first += [copy(1 + j, me, (*chip, c), src=x_ref) for j, chip in enumerate(chips)]
        for cp in first:
            cp.start()
        passed = [copy(4 + j, (*chip, c), sibling) for j, chip in enumerate(chips)]
        for j, chip in enumerate(chips):
            copy(1 + j, (*chip, c), me).wait_recv()
            passed[j].start()
        copy(0, sibling, me).wait_recv()
        for j, chip in enumerate(chips):
            copy(4 + j, (*chip, 1 - c), me).wait_recv()
        for cp in first + passed:
            cp.wait_send()
        mine.wait()
    return pl.pallas_call(
        body, name="allgather8",
        out_shape=jax.ShapeDtypeStruct((N_DEV * m_per, n), block.dtype),
        in_specs=[pl.BlockSpec(memory_space=pltpu.VMEM)],
        out_specs=pl.BlockSpec(memory_space=pltpu.VMEM),
        scratch_shapes=[pltpu.SemaphoreType.DMA((7,)), pltpu.SemaphoreType.DMA((7,)), pltpu.SemaphoreType.DMA],
        compiler_params=_params(48))(block)


def _other_chips(x, y):
    return [(1 - x, y), (x, 1 - y), (1 - x, 1 - y)]


_HBM = pl.BlockSpec(memory_space=pltpu.HBM)
_SEM = pl.BlockSpec(memory_space=pltpu.SEMAPHORE)
_ANY = pl.BlockSpec(memory_space=pl.ANY)
_EFFECT = pltpu.SideEffectType.DATAFLOW_SIDE_EFFECTING


def _in_hbm(v):
    return pltpu.with_memory_space_constraint(v, pltpu.HBM)


def cast_into_slot(w, layer, chip_id):
    _, kb, nb = w.shape
    tr = _row_tile(kb)

    def body(chip_ref, w_ref, o_ref):
        del chip_ref
        o_ref[...] = w_ref[...].astype(BF16)
    return pl.pallas_call(
        body, name="cast_into_slot",
        grid_spec=pltpu.PrefetchScalarGridSpec(
            num_scalar_prefetch=1, grid=(kb // tr,),
            in_specs=[pl.BlockSpec((None, tr, nb), lambda i, chip: (layer, i, 0))],
            out_specs=pl.BlockSpec((None, tr, nb), lambda i, chip: (chip[0], i, 0))),
        out_shape=jax.ShapeDtypeStruct((N_CHIP, kb, nb), BF16))(chip_id, w)


def _split_copies(mode, srcs, lands, send_sems, recv_sems):
    x, y, c = _me()
    out = []
    for t in range(len(lands)):
        for k, chip in enumerate(_other_chips(x, y)):
            if mode == "gather":
                src = dst = lands[t].at[2 * x + y]
                landed = lands[t].at[2 * chip[0] + chip[1]]
            else:
                src, dst, landed = srcs[t].at[2 * chip[0] + chip[1]], lands[t].at[k], lands[t].at[k]
            send = pltpu.make_async_remote_copy(src_ref=src, dst_ref=dst, send_sem=send_sems.at[3 * t + k],
                                                recv_sem=recv_sems.at[3 * t + k], device_id=(*chip, c),
                                                device_id_type=MESH)
            recv = pltpu.make_async_remote_copy(src_ref=src, dst_ref=landed, send_sem=send_sems.at[3 * t + k],
                                                recv_sem=recv_sems.at[3 * t + k], device_id=(*chip, c),
                                                device_id_type=MESH)
            out.append((send, recv))
    return out


def exchange_start(name, mode, srcs, lands, after):
    ns, nl = len(srcs), len(lands)
    na = ns + nl

    def body(*refs):
        src_refs, land_refs = refs[:ns], refs[ns:na]
        send_sems, recv_sems = refs[na + 1], refs[na + 2]
        token = refs[-1]
        for send, _ in _split_copies(mode, src_refs, land_refs, send_sems, recv_sems):
            send.start()
        token[...] = jnp.zeros_like(token)
    arrays = list(srcs) + list(lands)
    outs = pl.pallas_call(
        body, name=name,
        out_shape=(pltpu.SemaphoreType.DMA((3 * nl,)), pltpu.SemaphoreType.DMA((3 * nl,)),
                   *[pltpu.HBM(v.shape, v.dtype) for v in arrays], jax.ShapeDtypeStruct((8, 128), F32)),
        in_specs=[_HBM] * na + [_ANY],
        out_specs=(_SEM, _SEM, *[_HBM] * na, pl.BlockSpec(memory_space=pltpu.VMEM)),
        input_output_aliases={i: 2 + i for i in range(na)},
        compiler_params=pltpu.CompilerParams(has_side_effects=_EFFECT))(*[_in_hbm(v) for v in arrays], after)
    return outs[0], outs[1], list(outs[2:2 + ns]), list(outs[2 + ns:2 + na]), outs[-1]


def exchange_wait(name, mode, send_sems, recv_sems, srcs, lands, after):
    ns, nl = len(srcs), len(lands)
    na = ns + nl

    def body(*refs):
        for _, recv in _split_copies(mode, refs[:ns], refs[ns:na], refs[na], refs[na + 1]):
            recv.wait_send()
            recv.wait_recv()
    arrays = list(srcs) + list(lands)
    outs = pl.pallas_call(
        body, name=name,
        out_shape=[pltpu.HBM(v.shape, v.dtype) for v in arrays],
        in_specs=[_HBM] * na + [_SEM, _SEM, _ANY], out_specs=[_HBM] * na,
        input_output_aliases={i: i for i in range(na)},
        compiler_params=pltpu.CompilerParams(has_side_effects=_EFFECT))(*arrays, send_sems, recv_sems, after)
    return list(outs[:ns]), list(outs[ns:])


def swap_with_sibling(vs):
    n = len(vs)

    def body(*refs):
        v_refs, out_refs, send_sems, recv_sems = refs[:n], refs[n:2 * n], refs[2 * n], refs[2 * n + 1]
        x, y, c = _me()
        cps = [pltpu.make_async_remote_copy(src_ref=v_refs[t], dst_ref=out_refs[t], send_sem=send_sems.at[t],
                                            recv_sem=recv_sems.at[t], device_id=(x, y, 1 - c), device_id_type=MESH)
               for t in range(n)]
        for cp in cps:
            cp.start()
        for cp in cps:
            cp.wait()
    return pl.pallas_call(
        body, name="swap_with_sibling", out_shape=[jax.ShapeDtypeStruct(v.shape, v.dtype) for v in vs],
        in_specs=[_ANY] * n, out_specs=[_ANY] * n,
        scratch_shapes=[pltpu.SemaphoreType.DMA((n,)), pltpu.SemaphoreType.DMA((n,))])(*vs)


def sum_slots(parts):
    n, rows, w = parts.shape
    tr = _row_tile(rows)

    def body(p_ref, o_ref):
        acc = p_ref[0].astype(F32)
        for k in range(1, n):
            acc = acc + p_ref[k].astype(F32)
        o_ref[...] = acc
    return pl.pallas_call(
        body, name="sum_slots", grid=(rows // tr,),
        in_specs=[pl.BlockSpec((n, tr, w), lambda i: (0, i, 0))], out_specs=pl.BlockSpec((tr, w), lambda i: (i, 0)),
        out_shape=jax.ShapeDtypeStruct((rows, w), F32), compiler_params=_params(48))(parts)


def sum_landed(landed, own, chip_id, layer, n_layers, buf):
    n, rows, w = landed.shape
    tr = _row_tile(rows)
    base = layer * (rows // tr)

    def compute(l_ref, g_ref, o_ref):
        acc = g_ref[...].astype(F32)
        for k in range(n):
            acc = acc + l_ref[k].astype(F32)
        o_ref[...] = acc
    in_specs = [pl.BlockSpec((n, tr, w), lambda i, chip: (0, i, 0)),
                pl.BlockSpec((None, tr, w), lambda i, chip: (chip[0], i, 0))]
    out_spec = pl.BlockSpec((tr, w), lambda i, chip: (base + i, 0))
    out_shape = jax.ShapeDtypeStruct((n_layers * rows, w), F32)
    if buf is None:
        def body(chip_ref, l_ref, g_ref, o_ref):
            del chip_ref
            compute(l_ref, g_ref, o_ref)
        return pl.pallas_call(
            body, name="sum_landed",
            grid_spec=pltpu.PrefetchScalarGridSpec(num_scalar_prefetch=1, grid=(rows // tr,), in_specs=in_specs,
                                                   out_specs=out_spec),
            out_shape=out_shape, compiler_params=_params(48))(chip_id, landed, own)

    def body(chip_ref, l_ref, g_ref, buf_ref, o_ref):
        del chip_ref, buf_ref
        compute(l_ref, g_ref, o_ref)
    return pl.pallas_call(
        body, name="sum_landed_into",
        grid_spec=pltpu.PrefetchScalarGridSpec(num_scalar_prefetch=1, grid=(rows // tr,),
                                               in_specs=in_specs + [_ANY], out_specs=out_spec),
        out_shape=out_shape, input_output_aliases={3: 0}, compiler_params=_params(48))(chip_id, landed, own, buf)


def adamw(w, ga, gb, m, v):
    rows, wd = w.shape
    tr = min(_row_tile(rows), 128)
    c1 = 1.0 / (1.0 - ADAM_B1 ** ADAM_STEP)
    c2 = 1.0 / (1.0 - ADAM_B2 ** ADAM_STEP)

    def update(wv, g, mv, vv, g_ref, d_ref, m_ref, v_ref):
        mn = ADAM_B1 * mv + (1.0 - ADAM_B1) * g
        vn = ADAM_B2 * vv + (1.0 - ADAM_B2) * (g * g)
        g_ref[...] = g
        m_ref[...] = mn
        v_ref[...] = vn
        d_ref[...] = -ADAM_LR * ((mn * c1) / (jnp.sqrt(vn * c2) + ADAM_EPS) + ADAM_WD * wv)
    tile = pl.BlockSpec((tr, wd), lambda i: (i, 0))
    out = jax.ShapeDtypeStruct((rows, wd), F32)
    if gb is None:
        def body(w_ref, ga_ref, m_ref, v_ref, g_out, d_out, m_out, v_out):
            update(w_ref[...], ga_ref[...], m_ref[...], v_ref[...], g_out, d_out, m_out, v_out)
        return pl.pallas_call(body, name="adamw", grid=(rows // tr,), in_specs=[tile] * 4,
                              out_specs=[tile] * 4, out_shape=[out] * 4)(w, ga, m, v)

    def body(w_ref, ga_ref, gb_ref, m_ref, v_ref, g_out, d_out, m_out, v_out):
        update(w_ref[...], ga_ref[...] + gb_ref[...], m_ref[...], v_ref[...], g_out, d_out, m_out, v_out)
    return pl.pallas_call(body, name="adamw_sum", grid=(rows // tr,), in_specs=[tile] * 5,
                          out_specs=[tile] * 4, out_shape=[out] * 4)(w, ga, gb, m, v)


def _rope_tables(T, R):
    rows = T // GRID_W
    row = jnp.repeat(jnp.arange(rows), GRID_W).astype(F32)
    col = jnp.tile(jnp.arange(GRID_W), rows).astype(F32)
    n_freq = HEAD_DIM // 4
    inv_freq = ROPE_THETA ** (-jnp.arange(n_freq, dtype=F32) / n_freq)
    ang = jnp.concatenate([row[:, None] * inv_freq, col[:, None] * inv_freq], axis=-1)
    cos, sin = jnp.cos(ang), jnp.sin(ang)
    cs = jnp.tile(cos, (1, 4))
    sn = jnp.tile(jnp.concatenate([-sin, sin], axis=-1), (1, 2))
    pad = R - T
    return (jnp.concatenate([cs, jnp.ones((pad, 128), F32)], axis=0),
            jnp.concatenate([sn, jnp.zeros((pad, 128), F32)], axis=0))


def _pack(parts, mult=8 * 128):
    flat = jnp.concatenate([p.reshape(-1).astype(F32) for p in parts])
    pad = (-flat.shape[0]) % mult
    return jnp.pad(flat, (0, pad)).reshape(-1, 128)


def _unpack(buf, shapes):
    flat = buf.reshape(-1)
    out, o = [], 0
    for s in shapes:
        n = 1
        for d in s:
            n *= d
        out.append(flat[o:o + n].reshape(s))
        o += n
    return out


def kernel(x, c, ctx, c_ctx, w_mod, b_mod, norm_mix, norm_ffn, w_in_ab, conv_a, conv_b, conv_b_bias, ln_b_gain, ln_b_bias, w_out_ab, w_qkv, w_o, sinks, w_up, w_conv_ffn, w_down, final_norm, loss_target, m_c_ctx, m_w_mod, m_b_mod, m_norm_mix, m_norm_ffn, m_w_in_ab, m_conv_a, m_conv_b, m_conv_b_bias, m_ln_b_gain, m_ln_b_bias, m_w_out_ab, m_w_qkv, m_w_o, m_sinks, m_w_up, m_w_conv_ffn, m_w_down, m_final_norm, v_c_ctx, v_w_mod, v_b_mod, v_norm_mix, v_norm_ffn, v_w_in_ab, v_conv_a, v_conv_b, v_conv_b_bias, v_ln_b_gain, v_ln_b_bias, v_w_out_ab, v_w_qkv, v_w_o, v_sinks, v_w_up, v_w_conv_ffn, v_w_down, v_final_norm):
    T, dm = x.shape[1], x.shape[2]
    tc = ctx.shape[1]
    R = T + tc
    depth = w_mod.shape[0]
    ax, ay, ac = lax.axis_index("x"), lax.axis_index("y"), lax.axis_index("c")
    chip = 2 * ax + ay
    dev = 4 * ax + 2 * ay + ac

    small_w = [conv_a, conv_b, w_conv_ffn]
    gathered = allgather8(_pack([c] + small_w)).reshape(N_DEV, -1)
    cond8 = gathered[:, :dm]
    off = dm
    full_small = []
    for wsh in small_w:
        n = wsh.size
        per_chip = gathered[0::2, off:off + n].reshape((N_CHIP,) + wsh.shape)
        full_small.append(jnp.concatenate([per_chip[q] for q in range(N_CHIP)], axis=-1))
        off += n
    conv_a_f, conv_b_f, w_conv_ffn_f = full_small
    cond = jnp.concatenate([cond8, c_ctx[None, :], jnp.zeros((7, dm), F32)], axis=0)

    ns_mod = w_mod.shape[2]
    b_mod_sh = lax.dynamic_slice_in_dim(b_mod, chip * ns_mod, ns_mod, axis=1)[:, None, :]
    mod_sh = adaln_fwd(cond, w_mod, b_mod_sh)
    mod_all = allgather8(mod_sh.reshape(depth * 16, ns_mod)).reshape(N_DEV, depth, 16, ns_mod)
    mod_full = jnp.concatenate([mod_all[2 * q] for q in range(N_CHIP)], axis=-1)
    mine = lax.dynamic_index_in_dim(mod_full, dev, axis=1, keepdims=False)
    mods = jnp.stack([mine, mod_full[:, 8]], axis=1).reshape(depth, 2, 6, dm)

    masters = {"w_in_ab": w_in_ab, "w_out_ab": w_out_ab, "w_qkv": w_qkv, "w_o": w_o, "w_up": w_up, "w_down": w_down}
    chip_id = chip.astype(jnp.int32).reshape(1)

    def half_weights(l, half):
        if half == 1:
            return [("w_up", l), ("w_down", l)]
        return [("w_in_ab", l // 2), ("w_out_ab", l // 2)] if l % 2 == 0 else [("w_qkv", l // 2), ("w_o", l // 2)]
    in_flight, after = {}, mods
    for l in range(depth):
        for half in range(2):
            lands = [cast_into_slot(masters[n], j, chip_id) for n, j in half_weights(l, half)]
            send_sems, recv_sems, _, lands, after = exchange_start(f"gather_start_{l}_{half}", "gather", [], lands, after)
            in_flight[l, half] = (send_sems, recv_sems, lands)
    mods = mods + after[0, 0]

    def gathered_weights(l, half, after):
        send_sems, recv_sems, lands = in_flight[l, half]
        _, landed = exchange_wait(f"gather_wait_{l}_{half}", "gather", send_sems, recv_sems, [], lands, after)
        return dict(zip([n for n, _ in half_weights(l, half)], landed))

    cs, sn = _rope_tables(T, R)
    bias = window_bias(T, R)
    sinks_flat = sinks.reshape(-1)

    xs = jnp.concatenate([x[0], ctx[0]], axis=0)
    saved, W = [], []
    h1 = norm_mod_fwd(xs, norm_mix[0][None], mods[0], 0, T)
    for l in range(depth):
        e = l // 2
        wl = gathered_weights(l, 0, h1)
        W.append(wl)
        s = {"x0": xs, "h1": h1}
        if l % 2 == 0:
            p = mm_nn(h1, wl["w_in_ab"], BF16)
            yab, y1, x1, h2 = mixer_fwd(p, conv_a_f[e], conv_b_f[e], conv_b_bias[e][None], ln_b_gain[e][None],
                                        ln_b_bias[e][None], wl["w_out_ab"], xs, norm_ffn[l][None], mods[l], 2, 3, T)
            s.update(p=p, mix=yab)
        else:
            qkvr = mm_qkv_rope(h1, wl["w_qkv"], cs, sn)
            att, probs = attn_fwd(qkvr, sinks_flat[e * N_HEADS:(e + 1) * N_HEADS], bias, T)
            s.update(qkvr=qkvr, mix=att, probs=probs)
            y1, x1, h2 = mm_resid_norm_fwd(att, wl["w_o"], xs, norm_ffn[l][None], mods[l], mods[l], 2, 3, T)
        wl.update(gathered_weights(l, 1, h2))
        u = mm_nn(h2, wl["w_up"], BF16)
        if l + 1 < depth:
            z, y2, xs, h1 = ffn_mid_fwd(u, w_conv_ffn_f[l], wl["w_down"], x1, norm_mix[l + 1][None], mods[l],
                                        mods[l + 1], 5, 0, T)
        else:
            z, y2, xs = ffn_mid_fwd(u, w_conv_ffn_f[l], wl["w_down"], x1, None, mods[l], None, 5, 0, T)
        s.update(y1=y1, x1=x1, h2=h2, u=u, z=z, y2=y2)
        saved.append(s)

    loss_part, dx, d_final, dy2, dg2_last = loss_head(xs, final_norm[None], loss_target[0], saved[depth - 1]["y2"],
                                                      mods[depth - 1], 5, T)
    loss = lax.psum(loss_part[0, 0], ("x", "y", "c"))

    d_mods, d_norm_mix, d_norm_ffn = [None] * depth, [None] * depth, [None] * depth
    d_conv_a, d_conv_b, d_vecs, d_sinks, d_wc = [None] * 2, [None] * 2, [None] * 2, [None] * 2, [None] * depth
    dss1, dss2, dg1, dg2 = [None] * depth, [None] * depth, [None] * depth, [None] * depth
    scattering = {}

    def scatter(l, half, G, after):
        grads_h = [G[n] for n, _ in half_weights(l, half)]
        lands = [lax.empty((N_CHIP - 1, *g.shape[1:]), g.dtype) for g in grads_h]
        send_sems, recv_sems, grads_h, lands, token = exchange_start(
            f"scatter_start_{l}_{half}", "scatter", grads_h, lands, after)
        scattering[l, half] = (send_sems, recv_sems, grads_h, lands)
        return token

    dg2[depth - 1] = dg2_last
    pending = 0.0
    for l in reversed(range(depth)):
        e = l // 2
        s, wl = saved[l], W[l]
        G = {}
        G["w_down"] = mm_tn(s["z"], dy2, "row", wl["w_down"])
        duc, d_wc[l] = ffn_mid_bwd(mm_nt(dy2, wl["w_down"], BF16), s["u"], w_conv_ffn_f[l] + pending, T)
        du, dx, dy1, dss2[l], d_norm_ffn[l], dg1[l] = ffn_up_bwd(
            duc, w_conv_ffn_f[l], wl["w_up"], s["x1"], norm_ffn[l][None], mods[l], dx, s["y1"], mods[l], 3, 2, T)
        G["w_up"] = mm_tn(s["h2"], du, "col", wl["w_up"])
        started = scatter(l, 1, G, du)[0, 0]
        if l % 2 == 0:
            G["w_out_ab"] = mm_tn(s["mix"], dy1, "row", wl["w_out_ab"])
            dyab = mm_nt(dy1, wl["w_out_ab"], F32)
            dmid, d_conv_a[e], d_conv_b[e], d_vecs[e] = convmix_bwd1(
                dyab, s["p"], conv_a_f[e] + started, conv_b_f[e], conv_b_bias[e][None], ln_b_gain[e][None],
                ln_b_bias[e][None], T)
            if l > 0:
                dp, dx, dy2, dss1[l], d_norm_mix[l], dg2[l - 1] = mixer_in_bwd(
                    dmid, s["p"], conv_a_f[e], conv_b_f[e], wl["w_in_ab"], s["x0"], norm_mix[l][None], mods[l], dx,
                    saved[l - 1]["y2"], mods[l - 1], 0, 5, T)
            else:
                dp, dx, dss1[l], d_norm_mix[l] = mixer_in_bwd(
                    dmid, s["p"], conv_a_f[e], conv_b_f[e], wl["w_in_ab"], s["x0"], norm_mix[l][None], mods[l], dx,
                    None, None, 0, 0, T)
            G["w_in_ab"] = mm_tn(s["h1"], dp, "col", wl["w_in_ab"])
        else:
            G["w_o"] = mm_tn(s["mix"], dy1, "row", wl["w_o"])
            datt = mm_nt(dy1, wl["w_o"], BF16)
            dq, dks, dvs, dkc, dvc, d_sinks[e] = attn_bwd(s["qkvr"], datt, s["probs"], T)
            dqkv, dx, dy2, dss1[l], d_norm_mix[l], dg2[l - 1] = attn_in_bwd(
                dq, dks, dvs, dkc, dvc, cs + started, sn, wl["w_qkv"], s["x0"], norm_mix[l][None], mods[l], dx,
                saved[l - 1]["y2"], mods[l - 1], 0, 5, T)
            G["w_qkv"] = mm_tn(s["h1"], dqkv, "col", wl["w_qkv"])
        token = scatter(l, 0, G, dx)
        pending = token[0, 0]
    grad_x = dx[:T][None]
    for l in range(depth):
        a1, a2 = dss1[l].sum(2), dss2[l].sum(2)
        d_mods[l] = jnp.stack([a1[:, 0], a1[:, 1], dg1[l].sum(1), a2[:, 0], a2[:, 1], dg2[l].sum(1)], axis=1)

    d_mods = jnp.stack(d_mods)
    summed_parts = [
        d_mods[:, 1],
        jnp.stack(d_norm_mix).sum(1), jnp.stack(d_norm_ffn).sum(1),
        jnp.stack(d_conv_a).sum(2), jnp.stack(d_conv_b).sum(2),
        jnp.stack(d_vecs).sum(2),
        jnp.stack(d_sinks)[:, 0, :N_HEADS],
        jnp.stack(d_wc).sum(2), d_final.sum(0) + pending]
    summed_shapes = [p.shape for p in summed_parts]
    n_own = depth * 6 * dm
    pack = _pack([d_mods[:, 0]] + summed_parts)
    parts = allgather8(pack).reshape(N_DEV, -1, 128)
    total = sum_slots(parts)
    own_rows = parts.reshape(N_DEV, -1)[:, :n_own].reshape(N_DEV, depth, 6 * dm)
    (dmod_ctx, g_norm_mix, g_norm_ffn, g_conv_a, g_conv_b, g_vecs, g_sinks, g_wc, g_final) = _unpack(
        total.reshape(-1)[n_own:], summed_shapes)
    dmod_rows = jnp.concatenate([jnp.moveaxis(own_rows, 0, 1), dmod_ctx.reshape(depth, 1, 6 * dm),
                                 jnp.zeros((depth, 7, 6 * dm), F32)], axis=1)
    g_b_mod = dmod_rows.sum(1)
    dmod_sh = lax.dynamic_slice_in_dim(dmod_rows, chip * ns_mod, ns_mod, axis=2)
    g_w_mod, dsilu = adaln_bwd(cond, dmod_sh, w_mod)
    dsilu_all = allgather8(dsilu[8:16]).reshape(N_DEV, 8, dm)
    dsilu_ctx = sum_slots(dsilu_all[0::2])[0]
    sg = jax.nn.sigmoid(c_ctx)
    g_c_ctx = dsilu_ctx * (sg * (1.0 + c_ctx * (1.0 - sg)))

    def shard_cols(full, width):
        return lax.dynamic_slice_in_dim(full, chip * width, width, axis=full.ndim - 1)
    g_conv_a_s = shard_cols(g_conv_a, conv_a.shape[-1])
    g_conv_b_s = shard_cols(g_conv_b, conv_b.shape[-1])
    g_wc_s = shard_cols(g_wc, w_conv_ffn.shape[-1])

    grads, deltas, new_m, new_v = {}, {}, {}, {}

    def step_2d(name, wv, ga, gb, mv, vv):
        shp = wv.shape
        r2 = lambda t: t.reshape(-1, shp[-1])
        g, d, mn, vn = adamw(r2(wv), r2(ga), None if gb is None else r2(gb), r2(mv), r2(vv))
        grads[name], deltas[name], new_m[name], new_v[name] = (t.reshape(shp) for t in (g, d, mn, vn))

    step_2d("w_mod", w_mod, g_w_mod, None, m_w_mod, v_w_mod)
    sums = {n: None for n in masters}
    for l in reversed(range(depth)):
        for half in (1, 0):
            send_sems, recv_sems, grads_h, lands = scattering[l, half]
            grads_h, landed = exchange_wait(f"scatter_wait_{l}_{half}", "scatter", send_sems, recv_sems, grads_h,
                                            lands, deltas["w_mod"])
            for (n, j), own, arr in zip(half_weights(l, half), grads_h, landed):
                sums[n] = sum_landed(arr, own, chip_id, j, masters[n].shape[0], sums[n])
    moments = {"w_in_ab": (m_w_in_ab, v_w_in_ab), "w_out_ab": (m_w_out_ab, v_w_out_ab),
               "w_qkv": (m_w_qkv, v_w_qkv), "w_o": (m_w_o, v_w_o), "w_up": (m_w_up, v_w_up),
               "w_down": (m_w_down, v_w_down)}
    others = swap_with_sibling([sums[name] for name in masters])
    for (name, wv), other in zip(masters.items(), others):
        step_2d(name, wv, sums[name].reshape(wv.shape), other.reshape(wv.shape), *moments[name])

    small = [("c_ctx", c_ctx, g_c_ctx, m_c_ctx, v_c_ctx), ("b_mod", b_mod, g_b_mod, m_b_mod, v_b_mod),
             ("norm_mix", norm_mix, g_norm_mix, m_norm_mix, v_norm_mix),
             ("norm_ffn", norm_ffn, g_norm_ffn, m_norm_ffn, v_norm_ffn),
             ("conv_a", conv_a, g_conv_a_s, m_conv_a, v_conv_a), ("conv_b", conv_b, g_conv_b_s, m_conv_b, v_conv_b),
             ("conv_b_bias", conv_b_bias, g_vecs[:, 0], m_conv_b_bias, v_conv_b_bias),
             ("ln_b_gain", ln_b_gain, g_vecs[:, 1], m_ln_b_gain, v_ln_b_gain),
             ("ln_b_bias", ln_b_bias, g_vecs[:, 2], m_ln_b_bias, v_ln_b_bias),
             ("sinks", sinks, g_sinks, m_sinks, v_sinks),
             ("w_conv_ffn", w_conv_ffn, g_wc_s, m_w_conv_ffn, v_w_conv_ffn),
             ("final_norm", final_norm, g_final, m_final_norm, v_final_norm)]
    shapes = [t[1].shape for t in small]
    packed = [_pack([t[k] for t in small]) for k in (1, 2, 3, 4)]
    n_real = sum(t[1].size for t in small)
    lane_id = jnp.arange(packed[3].size).reshape(packed[3].shape)
    packed[3] = jnp.where(lane_id < n_real, packed[3], 1.0)
    outs = adamw(packed[0], packed[1], None, packed[2], packed[3])
    for (name, *_), g, d, mn, vn in zip(small, *[_unpack(o, shapes) for o in outs]):
        grads[name], deltas[name], new_m[name], new_v[name] = g, d, mn, vn

    order = ["c_ctx", "w_mod", "b_mod", "norm_mix", "norm_ffn", "w_in_ab", "conv_a", "conv_b", "conv_b_bias",
             "ln_b_gain", "ln_b_bias", "w_out_ab", "w_qkv", "w_o", "sinks", "w_up", "w_conv_ffn", "w_down",
             "final_norm"]
    return (loss, grad_x, *[grads[n] for n in order], *[deltas[n] for n in order],
            *[new_m[n] for n in order], *[new_v[n] for n in order])
```
